```python
import jax, jax.numpy as jnp
from jax import lax
import numpy as np

D_MODEL = 2048
BATCH = 2
SEQ = 8192
DEPTH = 1

D_MIX = D_MODEL
HG_HEADS = 8
HG_DK = 128
HG_DV = 128
HG_WIDTH = HG_HEADS * HG_DV
ML_HEADS = 4
ML_DQK = 128
ML_DV = 256
ML_WIDTH = ML_HEADS * ML_DV
CONV_W = 4
CHUNK = 64
N_EXPERTS = 32
TOP_K = 4
D_FF = 2048
SWIGLU_ALPHA = 1.702
SWIGLU_LIMIT = 7.0
MOE_BLOCK = 128
EPS = 1e-6

IN_SIZES = (
    HG_HEADS * HG_DK,
    HG_HEADS * HG_DK,
    HG_WIDTH,
    HG_WIDTH,
    ML_HEADS * ML_DQK,
    ML_HEADS * ML_DQK,
    ML_WIDTH,
    ML_WIDTH,
    ML_HEADS,
    ML_HEADS,
)
D_IN = sum(IN_SIZES)

kernel_name = "hymba_hgrn2_mlstm_moe"


def rms_norm(x, w):
    xf = x.astype(jnp.float32)
    y = xf * lax.rsqrt(jnp.mean(xf * xf, axis=-1, keepdims=True) + EPS)
    return (y * w.astype(jnp.float32)).astype(x.dtype)


def head_rms_norm(o, w):
    B, S, H, d = o.shape
    y = o * lax.rsqrt(jnp.mean(o * o, axis=-1, keepdims=True) + EPS)
    return y.reshape(B, S, H * d) * w.astype(jnp.float32)


def to_chunks(t, H):
    B, S, HD = t.shape
    return t.reshape(B, S // CHUNK, CHUNK, H, HD // H).transpose(1, 0, 3, 2, 4)


def gate_chunks(t):
    B, S, H = t.shape
    return t.reshape(B, S // CHUNK, CHUNK, H).transpose(1, 0, 3, 2)


def from_chunks(t):
    NC, B, H, C, d = t.shape
    return t.transpose(1, 0, 3, 2, 4).reshape(B, NC * C, H, d)


def causal_dwconv(u, w, b):
    out = lax.conv_general_dilated(
        u, w[:, None, :], window_strides=(1,), padding=[(CONV_W - 1, 0)],
        dimension_numbers=("NWC", "WIO", "NWC"), feature_group_count=u.shape[-1])
    return out + b


def hgrn2_mix(q, f_pre, i, g, lb, norm_w):
    f32 = jnp.float32
    B, S, _ = q.shape
    z = f_pre.astype(f32)
    lb = lb.astype(f32)
    log_f = jnp.logaddexp(jnp.log(lb), jnp.log1p(-lb) + jax.nn.log_sigmoid(z))
    k = (1.0 - lb) * jax.nn.sigmoid(-z)
    qc = to_chunks(q.astype(f32), HG_HEADS)
    kc = to_chunks(k, HG_HEADS)
    vc = to_chunks(i.astype(f32), HG_HEADS)
    fc = to_chunks(log_f, HG_HEADS)
    causal = jnp.tril(jnp.ones((CHUNK, CHUNK), dtype=bool))

    def step(state, inp):
        q_c, k_c, v_c, lf_c = inp
        b = jnp.cumsum(lf_c, axis=-2)
        inter = jnp.einsum("bhtk,bhkv->bhtv", q_c * jnp.exp(b), state)
        rel = b[:, :, :, None, :] - b[:, :, None, :, :]
        decay = jnp.exp(jnp.where(causal[:, :, None], rel, -jnp.inf))
        att = jnp.einsum("bhtk,bhsk,bhtsk->bhts", q_c, k_c, decay)
        intra = jnp.einsum("bhts,bhsv->bhtv", att, v_c)
        b_last = b[:, :, -1]
        k_dec = k_c * jnp.exp(b_last[:, :, None, :] - b)
        new_state = jnp.exp(b_last)[..., None] * state + jnp.einsum("bhsk,bhsv->bhkv", k_dec, v_c)
        return new_state, inter + intra

    s0 = jnp.zeros((B, HG_HEADS, HG_DK, HG_DV), f32)
    _, oc = lax.scan(step, s0, (qc, kc, vc, fc))
    o = from_chunks(oc)
    out = head_rms_norm(o, norm_w) * jax.nn.silu(g.astype(f32))
    return out.astype(q.dtype)


def mlstm_mix(q, k, v, o_pre, ig_pre, fg_pre, norm_w):
    f32 = jnp.float32
    B, S, _ = q.shape
    qc = to_chunks(q.astype(f32) * (ML_DQK ** -0.5), ML_HEADS)
    kc = to_chunks(k.astype(f32), ML_HEADS)
    vc = to_chunks(v.astype(f32), ML_HEADS)
    lic = gate_chunks(ig_pre.astype(f32))
    lfc = gate_chunks(jax.nn.log_sigmoid(fg_pre.astype(f32)))
    causal = jnp.tril(jnp.ones((CHUNK, CHUNK), dtype=bool))

    def step(carry, inp):
        Cm, n, m = carry
        q_c, k_c, v_c, li_c, lf_c = inp
        g = jnp.cumsum(lf_c, axis=-1)
        a_inter = g + m[..., None]
        dmat = g[..., :, None] - g[..., None, :] + li_c[..., None, :]
        dmat = jnp.where(causal, dmat, -jnp.inf)
        m_t = jnp.maximum(a_inter, jnp.max(dmat, axis=-1))
        w_inter = jnp.exp(a_inter - m_t)
        sqk = jnp.einsum("bhtd,bhsd->bhts", q_c, k_c) * jnp.exp(dmat - m_t[..., None])
        num = (w_inter[..., None] * jnp.einsum("bhtd,bhdv->bhtv", q_c, Cm)
               + jnp.einsum("bhts,bhsv->bhtv", sqk, v_c))
        den = w_inter * jnp.einsum("bhtd,bhd->bht", q_c, n) + jnp.sum(sqk, axis=-1)
        h = num / jnp.maximum(jnp.abs(den), jnp.exp(-m_t))[..., None]
        g_last = g[..., -1]
        log_ws = g_last[..., None] - g + li_c
        m_new = jnp.maximum(g_last + m, jnp.max(log_ws, axis=-1))
        decay = jnp.exp(g_last + m - m_new)
        ws = jnp.exp(log_ws - m_new[..., None])
        C_new = decay[..., None, None] * Cm + jnp.einsum("bhs,bhsd,bhsv->bhdv", ws, k_c, v_c)
        n_new = decay[..., None] * n + jnp.einsum("bhs,bhsd->bhd", ws, k_c)
        return (C_new, n_new, m_new), h

    init = (jnp.zeros((B, ML_HEADS, ML_DQK, ML_DV), f32),
            jnp.zeros((B, ML_HEADS, ML_DQK), f32),
            jnp.zeros((B, ML_HEADS), f32))
    _, hc = lax.scan(step, init, (qc, kc, vc, lic, lfc))
    h = from_chunks(hc)
    out = head_rms_norm(h, norm_w) * jax.nn.sigmoid(o_pre.astype(f32))
    return out.astype(q.dtype)


def moe_ffn(h, w_router, b_router, w_gate, b_gate, w_up, b_up, w_down, b_down):
    B, S, D = h.shape
    T = B * S
    A = T * TOP_K
    xf = h.reshape(T, D)
    logits = xf.astype(jnp.float32) @ w_router.astype(jnp.float32) + b_router.astype(jnp.float32)
    top_vals, top_idx = lax.top_k(logits, TOP_K)
    gates = jax.nn.softmax(top_vals, axis=-1)
    flat_e = top_idx.reshape(A).astype(jnp.int32)
    flat_tok = jnp.repeat(jnp.arange(T, dtype=jnp.int32), TOP_K)
    flat_gate = gates.reshape(A)
    order = jnp.argsort(flat_e)
    e_sorted = flat_e[order]
    tok_sorted = flat_tok[order]
    gate_sorted = flat_gate[order]
    counts = jnp.zeros((N_EXPERTS,), jnp.int32).at[flat_e].add(1)
    padded = ((counts + MOE_BLOCK - 1) // MOE_BLOCK) * MOE_BLOCK
    start = jnp.cumsum(counts) - counts
    pend = jnp.cumsum(padded)
    pstart = pend - padded
    dest = pstart[e_sorted] + (jnp.arange(A, dtype=jnp.int32) - start[e_sorted])
    n_blocks = (A + N_EXPERTS * (MOE_BLOCK - 1) + MOE_BLOCK - 1) // MOE_BLOCK
    n_slots = n_blocks * MOE_BLOCK
    slot_tok = jnp.zeros((n_slots,), jnp.int32).at[dest].set(tok_sorted)
    block_e = jnp.minimum(
        jnp.searchsorted(pend, jnp.arange(n_blocks, dtype=jnp.int32) * MOE_BLOCK, side="right"),
        N_EXPERTS - 1).astype(jnp.int32)
    xs = xf[slot_tok].reshape(n_blocks, MOE_BLOCK, D)

    def expert_block(args):
        xb, e = args
        gt = xb @ w_gate[e] + b_gate[e]
        up = xb @ w_up[e] + b_up[e]
        gt = jnp.minimum(gt, SWIGLU_LIMIT)
        up = jnp.clip(up, -SWIGLU_LIMIT, SWIGLU_LIMIT)
        act = (up + 1.0) * (gt * jax.nn.sigmoid(SWIGLU_ALPHA * gt))
        return act @ w_down[e] + b_down[e]

    ys = lax.map(expert_block, (xs, block_e)).reshape(n_slots, D)
    contrib = ys[dest] * gate_sorted[:, None].astype(ys.dtype)
    out = jax.ops.segment_sum(contrib, tok_sorted, num_segments=T)
    return out.reshape(B, S, D).astype(h.dtype)


def setup_inputs(seed: int = 0) -> dict:
    key = jax.random.key(seed)
    ks = jax.random.split(key, 24)
    f32 = jnp.float32
    nrm = lambda k, shp, s: jax.random.normal(k, shp, f32) * s
    return {
        "x": jax.random.normal(ks[0], (BATCH, SEQ, D_MODEL), f32),
        "ln1_w": 1.0 + nrm(ks[1], (DEPTH, D_MODEL), 0.02),
        "w_in": nrm(ks[2], (DEPTH, D_MODEL, D_IN), D_MODEL ** -0.5),
        "hg_lb_logits": nrm(ks[3], (DEPTH + 1, HG_HEADS * HG_DK), 0.1),
        "hg_norm_w": 1.0 + nrm(ks[4], (DEPTH, HG_WIDTH), 0.02),
        "ml_conv_w": nrm(ks[5], (DEPTH, CONV_W, 2 * ML_HEADS * ML_DQK), CONV_W ** -0.5),
        "ml_conv_b": nrm(ks[6], (DEPTH, 2 * ML_HEADS * ML_DQK), 0.02),
        "ml_igate_b": nrm(ks[7], (DEPTH, ML_HEADS), 0.1),
        "ml_fgate_b": jax.random.uniform(ks[8], (DEPTH, ML_HEADS), f32, 3.0, 6.0),
        "ml_norm_w": 1.0 + nrm(ks[9], (DEPTH, ML_WIDTH), 0.02),
        "w_out": nrm(ks[10], (DEPTH, D_MIX, D_MODEL), D_MIX ** -0.5),
        "ln2_w": 1.0 + nrm(ks[11], (DEPTH, D_MODEL), 0.02),
        "w_router": nrm(ks[12], (DEPTH, D_MODEL, N_EXPERTS), D_MODEL ** -0.5),
        "b_router": nrm(ks[13], (DEPTH, N_EXPERTS), 0.01),
        "w_gate": nrm(ks[14], (DEPTH, N_EXPERTS, D_MODEL, D_FF), D_MODEL ** -0.5),
        "b_gate": nrm(ks[15], (DEPTH, N_EXPERTS, D_FF), 0.02),
        "w_up": nrm(ks[16], (DEPTH, N_EXPERTS, D_MODEL, D_FF), D_MODEL ** -0.5),
        "b_up": nrm(ks[17], (DEPTH, N_EXPERTS, D_FF), 0.02),
        "w_down": nrm(ks[18], (DEPTH, N_EXPERTS, D_FF, D_MODEL), D_FF ** -0.5),
        "b_down": nrm(ks[19], (DEPTH, N_EXPERTS, D_MODEL), 0.02),
        "final_norm_w": 1.0 + nrm(ks[20], (D_MODEL,), 0.02),
    }


def reference(x, ln1_w, w_in, hg_lb_logits, hg_norm_w, ml_conv_w, ml_conv_b, ml_igate_b,
              ml_fgate_b, ml_norm_w, w_out, ln2_w, w_router, b_router, w_gate, b_gate,
              w_up, b_up, w_down, b_down, final_norm_w):
    lb_all = jnp.cumsum(jax.nn.softmax(hg_lb_logits.astype(jnp.float32), axis=0), axis=0)
    split_idx = [int(v) for v in np.cumsum(IN_SIZES)[:-1]]
    qk_w = ML_HEADS * ML_DQK
    for l in range(DEPTH):
        h = rms_norm(x, ln1_w[l])
        proj = h @ w_in[l]
        hq, hf, hi, hg, mq, mk, mv, mo, mig, mfg = jnp.split(proj, split_idx, axis=-1)
        qk = jax.nn.silu(causal_dwconv(jnp.concatenate([mq, mk], axis=-1), ml_conv_w[l], ml_conv_b[l]))
        mq, mk = qk[..., :qk_w], qk[..., qk_w:]
        a_out = hgrn2_mix(hq, hf, hi, hg, lb_all[l], hg_norm_w[l])
        b_out = mlstm_mix(mq, mk, mv, mo, mig + ml_igate_b[l], mfg + ml_fgate_b[l],
                          ml_norm_w[l])
        x = x + jnp.concatenate([a_out, b_out], axis=-1) @ w_out[l]
        x = x + moe_ffn(rms_norm(x, ln2_w[l]), w_router[l], b_router[l], w_gate[l], b_gate[l],
                        w_up[l], b_up[l], w_down[l], b_down[l])
    return rms_norm(x, final_norm_w)
```

```python
import functools

import jax
import jax.numpy as jnp
from jax import lax
from jax.experimental import pallas as pl
from jax.experimental.pallas import tpu as pltpu

F32 = jnp.float32
BF16 = jnp.bfloat16
HIGHEST = lax.Precision.HIGHEST

EPS = 1e-6
HG_HEADS = 8
HG_DK = 128
ML_HEADS = 4
ML_DQK = 128
ML_DV = 256
CONV_W = 4
N_EXPERTS = 32
TOP_K = 4
SWIGLU_ALPHA = 1.702
SWIGLU_LIMIT = 7.0

LANES = 128
SUBLANES = 8
VMEM_LIMIT_BYTES = 56 * 1024 * 1024

HG_CHUNK = 64
HG_SUB = 16
ML_CHUNK = 128
MIX_ROWS = 512

INPROJ_TM = 1024
INPROJ_TN = 512
OUTPROJ_TM = 256
MOE_TM = 512
MOE_TF = 1024
GATHER_ROWS = 1024
FINAL_TM = 256


def _dot_nt(a, b):
    return lax.dot_general(a, b, (((1,), (1,)), ((), ())), preferred_element_type=F32)


def _dot_tn(a, b):
    return lax.dot_general(a, b, (((0,), (0,)), ((), ())), preferred_element_type=F32)


def _log_sigmoid(z):
    return jnp.minimum(z, 0.0) - jnp.log1p(jnp.exp(-jnp.abs(z)))


def _sigmoid(z):
    return 1.0 / (1.0 + jnp.exp(-z))


def _cparams(semantics):
    return pltpu.CompilerParams(dimension_semantics=semantics, vmem_limit_bytes=VMEM_LIMIT_BYTES)


def _inproj_body(x_ref, lnw_ref, w_ref, wg_ref, o_ref, g_ref, h_scr, *, tn):
    @pl.when(pl.program_id(1) == 0)
    def _():
        x = x_ref[...]
        h = x * lax.rsqrt(jnp.mean(x * x, axis=-1, keepdims=True) + EPS) * lnw_ref[...]
        hb = h.astype(BF16)
        h_scr[...] = hb
        g_ref[...] = jnp.dot(hb, wg_ref[...], preferred_element_type=F32)

    res = jnp.dot(h_scr[...], w_ref[...].astype(BF16), preferred_element_type=F32)
    for c in range(tn // LANES):
        o_ref[c] = res[:, c * LANES:(c + 1) * LANES]


def _inproj(x2d, ln_w, w_in, w_gates_pad, n_main):
    T, D = x2d.shape
    tm = min(INPROJ_TM, T)
    tn = INPROJ_TN
    n_slabs = n_main // LANES
    return pl.pallas_call(
        functools.partial(_inproj_body, tn=tn),
        grid=(T // tm, n_main // tn),
        in_specs=[
            pl.BlockSpec((tm, D), lambda i, j: (i, 0)),
            pl.BlockSpec((1, D), lambda i, j: (0, 0)),
            pl.BlockSpec((D, tn), lambda i, j: (0, j)),
            pl.BlockSpec((D, LANES), lambda i, j: (0, 0)),
        ],
        out_specs=[
            pl.BlockSpec((tn // LANES, tm, LANES), lambda i, j: (j, i, 0)),
            pl.BlockSpec((tm, LANES), lambda i, j: (i, 0)),
        ],
        out_shape=[
            jax.ShapeDtypeStruct((n_slabs, T, LANES), F32),
            jax.ShapeDtypeStruct((T, LANES), F32),
        ],
        scratch_shapes=[pltpu.VMEM((tm, D), BF16)],
        compiler_params=_cparams(("parallel", "arbitrary")),
        name="inproj",
    )(x2d, ln_w, w_in, w_gates_pad)


def _hgrn_body(q_ref, f_ref, i_ref, g_ref, lb_ref, nw_ref, o_ref, st_scr, *, rows):
    C, SUB = HG_CHUNK, HG_SUB
    nsub = C // SUB

    @pl.when(pl.program_id(2) == 0)
    def _():
        st_scr[...] = jnp.zeros_like(st_scr)

    lb = lb_ref[0]
    log_lb = jnp.log(lb)
    log_1mlb = jnp.log1p(-lb)
    one_m_lb = 1.0 - lb
    nw = nw_ref[0]

    r_i = lax.broadcasted_iota(jnp.int32, (C, C), 0)
    c_i = lax.broadcasted_iota(jnp.int32, (C, C), 1)
    tri = (c_i <= r_i).astype(F32)
    sub_shift = SUB.bit_length() - 1
    diag_mask = (c_i <= r_i) & (jnp.right_shift(r_i, sub_shift) == jnp.right_shift(c_i, sub_shift))
    row_id = lax.broadcasted_iota(jnp.int32, (C, HG_DK), 0)

    for j in range(rows // C):
        sl = pl.ds(j * C, C)
        q = q_ref[0, sl, :]
        z = f_ref[0, sl, :]
        v = i_ref[0, sl, :]
        g = g_ref[0, sl, :]

        e = jnp.exp(-jnp.abs(z))
        log_sig = jnp.minimum(z, 0.0) - jnp.log1p(e)
        sig_neg = jnp.where(z >= 0, e, 1.0) / (1.0 + e)
        cc = log_1mlb + log_sig
        log_f = jnp.maximum(log_lb, cc) + jnp.log1p(jnp.exp(-jnp.abs(log_lb - cc)))
        kk = one_m_lb * sig_neg
        b = jnp.dot(tri, log_f, precision=HIGHEST, preferred_element_type=F32)
        b_last = b[C - 1:C, :]

        st = st_scr[...]
        vb = v.astype(BF16)
        qe = (q * jnp.exp(b)).astype(BF16)
        kdec = (kk * jnp.exp(b_last - b)).astype(BF16)
        inter = _dot_nt(qe, st.astype(BF16))

        refs = [b[I * SUB:I * SUB + 1, :] for I in range(nsub)]
        refb = jnp.concatenate([jnp.broadcast_to(r, (SUB, HG_DK)) for r in refs], axis=0)
        qd = (q * jnp.exp(b - refb)).astype(BF16)
        kd = (kk * jnp.exp(refb - b)).astype(BF16)
        att = jnp.where(diag_mask, _dot_nt(qd, kd), 0.0)
        q_parts, k_parts = [], []
        for J in range(nsub - 1):
            r = refs[J + 1]
            qj = q * jnp.exp(jnp.minimum(b - r, 0.0))
            kj = kk * jnp.exp(jnp.minimum(r - b, 0.0))
            q_parts.append(jnp.where(row_id >= (J + 1) * SUB, qj, 0.0).astype(BF16))
            k_parts.append(jnp.where((row_id >= J * SUB) & (row_id < (J + 1) * SUB), kj, 0.0).astype(BF16))
        att = att + _dot_nt(jnp.concatenate(q_parts, axis=1), jnp.concatenate(k_parts, axis=1))
        intra = jnp.dot(att.astype(BF16), vb, preferred_element_type=F32)

        o = inter + intra
        y = o * lax.rsqrt(jnp.mean(o * o, axis=-1, keepdims=True) + EPS) * nw
        o_ref[sl, :] = (y * (g * _sigmoid(g))).astype(o_ref.dtype)

        st_scr[...] = st * jnp.exp(b_last) + _dot_tn(vb, kdec)


def _hgrn(proj3, lb, norm_w, batch, seq):
    T = batch * seq
    rows = min(MIX_ROWS, seq)
    nblk = seq // rows
    H = HG_HEADS

    def slab(off):
        return pl.BlockSpec((1, rows, LANES), lambda b, h, c, off=off: (off + h, b * nblk + c, 0))

    vec = pl.BlockSpec((1, 1, LANES), lambda b, h, c: (h, 0, 0))
    return pl.pallas_call(
        functools.partial(_hgrn_body, rows=rows),
        grid=(batch, H, nblk),
        in_specs=[slab(0), slab(H), slab(2 * H), slab(3 * H), vec, vec],
        out_specs=pl.BlockSpec((rows, LANES), lambda b, h, c: (b * nblk + c, h)),
        out_shape=jax.ShapeDtypeStruct((T, H * LANES), BF16),
        scratch_shapes=[pltpu.VMEM((LANES, HG_DK), F32)],
        compiler_params=_cparams(("parallel", "parallel", "arbitrary")),
        name="hgrn2",
    )(proj3, proj3, proj3, proj3, lb.reshape(H, 1, HG_DK), norm_w.reshape(H, 1, LANES))


def _mlstm_body(q_ref, k_ref, v_ref, og_ref, gt_ref, gb_ref, cwq_ref, cwk_ref, cbq_ref, cbk_ref, nw_ref,
                out_ref, c_scr, n_scr, m_scr, qx_scr, kx_scr, qc_scr, kc_scr, *, rows):
    C = ML_CHUNK
    PAD = SUBLANES
    h = pl.program_id(1)

    @pl.when(pl.program_id(2) == 0)
    def _():
        c_scr[...] = jnp.zeros_like(c_scr)
        n_scr[...] = jnp.zeros_like(n_scr)
        m_scr[...] = jnp.zeros_like(m_scr)
        qx_scr[0:PAD, :] = jnp.zeros((PAD, LANES), F32)
        kx_scr[0:PAD, :] = jnp.zeros((PAD, LANES), F32)

    qx_scr[PAD:PAD + rows, :] = q_ref[0]
    kx_scr[PAD:PAD + rows, :] = k_ref[0]
    accq = jnp.zeros((rows, LANES), F32) + cbq_ref[...]
    acck = jnp.zeros((rows, LANES), F32) + cbk_ref[...]
    for j in range(CONV_W):
        off = PAD - (CONV_W - 1) + j
        accq = accq + cwq_ref[j:j + 1, :] * qx_scr[pl.ds(off, rows), :]
        acck = acck + cwk_ref[j:j + 1, :] * kx_scr[pl.ds(off, rows), :]
    qc_scr[...] = accq * _sigmoid(accq) * (ML_DQK ** -0.5)
    kc_scr[...] = acck * _sigmoid(acck)
    qx_scr[0:PAD, :] = qx_scr[rows:rows + PAD, :]
    kx_scr[0:PAD, :] = kx_scr[rows:rows + PAD, :]

    lane = lax.broadcasted_iota(jnp.int32, (C, C), 1)
    sub = lax.broadcasted_iota(jnp.int32, (C, C), 0)
    causal = lane <= sub
    lower = causal.astype(F32)
    upper = (sub <= lane).astype(F32)
    nw = nw_ref[...]
    gbias = gb_ref[...]

    for j in range(rows // C):
        sl = pl.ds(j * C, C)
        gc = gt_ref[sl, :] + gbias
        gct = gc.T
        li_col = jnp.sum(jnp.where(lane == h, gc, 0.0), axis=1, keepdims=True)
        fg_col = jnp.sum(jnp.where(lane == h + ML_HEADS, gc, 0.0), axis=1, keepdims=True)
        li_row = jnp.sum(jnp.where(sub == h, gct, 0.0), axis=0, keepdims=True)
        fg_row = jnp.sum(jnp.where(sub == h + ML_HEADS, gct, 0.0), axis=0, keepdims=True)
        lf_col = _log_sigmoid(fg_col)
        lf_row = _log_sigmoid(fg_row)
        g_t = jnp.dot(lower, jnp.broadcast_to(lf_col, (C, C)), precision=HIGHEST,
                      preferred_element_type=F32)
        g_s = jnp.dot(jnp.broadcast_to(lf_row, (C, C)), upper, precision=HIGHEST,
                      preferred_element_type=F32)
        g_col = g_t[:, 0:1]
        g_last = g_t[C - 1:C, 0:1]
        m_prev = m_scr[:, 0:1]

        a_inter = g_col + m_prev
        dmat = jnp.where(causal, g_t - g_s + li_row, -jnp.inf)
        m_t = jnp.maximum(a_inter, jnp.max(dmat, axis=1, keepdims=True))
        w_inter = jnp.exp(a_inter - m_t)
        p = jnp.exp(dmat - m_t)

        qf = qc_scr[sl, :]
        kf = kc_scr[sl, :]
        qb = qf.astype(BF16)
        vb = jnp.concatenate([v_ref[0, sl, :], v_ref[1, sl, :]], axis=1).astype(BF16)
        cm = c_scr[...]
        nv = n_scr[...]
        sqk = _dot_nt(qb, kf.astype(BF16)) * p
        num = (w_inter * jnp.dot(qb, cm.astype(BF16), preferred_element_type=F32)
               + jnp.dot(sqk.astype(BF16), vb, preferred_element_type=F32))
        den = (w_inter * jnp.sum(qf * nv, axis=1, keepdims=True)
               + jnp.sum(sqk, axis=1, keepdims=True))
        hh = num * (1.0 / jnp.maximum(jnp.abs(den), jnp.exp(-m_t)))

        log_ws = g_last - g_col + li_col
        m_new = jnp.maximum(g_last + m_prev, jnp.max(log_ws, axis=0, keepdims=True))
        decay = jnp.exp(g_last + m_prev - m_new)
        kw = kf * jnp.exp(log_ws - m_new)
        c_scr[...] = decay * cm + _dot_tn(kw.astype(BF16), vb)
        n_scr[...] = decay * nv + jnp.sum(kw, axis=0, keepdims=True)
        m_scr[...] = jnp.broadcast_to(m_new, m_scr.shape)

        y = hh * lax.rsqrt(jnp.mean(hh * hh, axis=-1, keepdims=True) + EPS) * nw
        og = jnp.concatenate([og_ref[0, sl, :], og_ref[1, sl, :]], axis=1)
        out_ref[sl, :] = (y * _sigmoid(og)).astype(out_ref.dtype)


def _mlstm(proj3, gates, gate_bias_pad, conv_w, conv_b, norm_w, batch, seq, q_off):
    T = batch * seq
    rows = min(MIX_ROWS, seq)
    nblk = seq // rows
    H = ML_HEADS
    k_off = q_off + H
    v_off = k_off + H
    o_off = v_off + 2 * H

    def slab(off):
        return pl.BlockSpec((1, rows, LANES), lambda b, h, c, off=off: (off + h, b * nblk + c, 0))

    def slab2(off):
        return pl.BlockSpec((2, rows, LANES), lambda b, h, c, off=off: (off // 2 + h, b * nblk + c, 0))

    qk_w = H * ML_DQK
    return pl.pallas_call(
        functools.partial(_mlstm_body, rows=rows),
        grid=(batch, H, nblk),
        in_specs=[
            slab(q_off), slab(k_off), slab2(v_off), slab2(o_off),
            pl.BlockSpec((rows, LANES), lambda b, h, c: (b * nblk + c, 0)),
            pl.BlockSpec((1, LANES), lambda b, h, c: (0, 0)),
            pl.BlockSpec((CONV_W, LANES), lambda b, h, c: (0, h)),
            pl.BlockSpec((CONV_W, LANES), lambda b, h, c: (0, H + h)),
            pl.BlockSpec((1, LANES), lambda b, h, c: (0, h)),
            pl.BlockSpec((1, LANES), lambda b, h, c: (0, H + h)),
            pl.BlockSpec((1, ML_DV), lambda b, h, c: (0, h)),
        ],
        out_specs=pl.BlockSpec((rows, ML_DV), lambda b, h, c: (b * nblk + c, h)),
        out_shape=jax.ShapeDtypeStruct((T, H * ML_DV), BF16),
        scratch_shapes=[
            pltpu.VMEM((ML_DQK, ML_DV), F32),
            pltpu.VMEM((1, ML_DQK), F32),
            pltpu.VMEM((1, LANES), F32),
            pltpu.VMEM((rows + 2 * SUBLANES, LANES), F32),
            pltpu.VMEM((rows + 2 * SUBLANES, LANES), F32),
            pltpu.VMEM((rows, LANES), F32),
            pltpu.VMEM((rows, LANES), F32),
        ],
        compiler_params=_cparams(("parallel", "parallel", "arbitrary")),
        name="mlstm",
    )(proj3, proj3, proj3, proj3, gates, gate_bias_pad, conv_w, conv_w,
      conv_b.reshape(1, 2 * qk_w), conv_b.reshape(1, 2 * qk_w), norm_w.reshape(1, H * ML_DV))


def _outproj_body(a_ref, b_ref, x_ref, wo_ref, ln_ref, wr_ref, br_ref,
                  x2_ref, h2_ref, idx_ref, gate_ref):
    ka = a_ref.shape[1]
    res = (jnp.dot(a_ref[...], wo_ref[0:ka, :], preferred_element_type=F32)
           + jnp.dot(b_ref[...], wo_ref[ka:, :], preferred_element_type=F32))
    x2 = x_ref[...] + res
    x2_ref[...] = x2
    h2 = x2 * lax.rsqrt(jnp.mean(x2 * x2, axis=-1, keepdims=True) + EPS) * ln_ref[...]
    h2_ref[...] = h2.astype(h2_ref.dtype)

    logits = jnp.dot(h2, wr_ref[...], precision=HIGHEST, preferred_element_type=F32) + br_ref[...]
    lane = lax.broadcasted_iota(jnp.int32, logits.shape, 1).astype(F32)
    vals, idxs = [], []
    cur = logits
    for _ in range(TOP_K):
        m = jnp.max(cur, axis=1, keepdims=True)
        ix = jnp.min(jnp.where(cur == m, lane, float(LANES)), axis=1, keepdims=True)
        vals.append(m)
        idxs.append(ix)
        cur = jnp.where(lane == ix, -jnp.inf, cur)
    es = [jnp.exp(v - vals[0]) for v in vals]
    inv = 1.0 / (es[0] + es[1] + es[2] + es[3])
    gate = jnp.zeros(logits.shape, F32)
    idx = jnp.zeros(logits.shape, F32)
    for k in range(TOP_K):
        gate = jnp.where(lane == float(k), es[k] * inv, gate)
        idx = jnp.where(lane == float(k), idxs[k], idx)
    gate_ref[...] = gate
    idx_ref[...] = idx.astype(jnp.int32)


def _outproj(a_out, b_out, x2d, w_out_bf, ln_w, w_router_pad, b_router_pad):
    T, D = x2d.shape
    tm = min(OUTPROJ_TM, T)
    ka, kb = a_out.shape[1], b_out.shape[1]
    row = lambda w: pl.BlockSpec((tm, w), lambda i: (i, 0))
    full = lambda r, c: pl.BlockSpec((r, c), lambda i: (0, 0))
    return pl.pallas_call(
        _outproj_body,
        grid=(T // tm,),
        in_specs=[row(ka), row(kb), row(D), full(ka + kb, D), full(1, D), full(D, LANES), full(1, LANES)],
        out_specs=[row(D), row(D), row(LANES), row(LANES)],
        out_shape=[
            jax.ShapeDtypeStruct((T, D), F32),
            jax.ShapeDtypeStruct((T, D), BF16),
            jax.ShapeDtypeStruct((T, LANES), jnp.int32),
            jax.ShapeDtypeStruct((T, LANES), F32),
        ],
        compiler_params=_cparams(("parallel",)),
        name="outproj_router",
    )(a_out, b_out, x2d, w_out_bf, ln_w, w_router_pad, b_router_pad)


def _gather_body(idx_ref, src_ref, out_ref, sem, *, n):
    base = pl.program_id(0) * n

    def copy(i):
        return pltpu.make_async_copy(src_ref.at[idx_ref[0, 0, i]], out_ref.at[base + i], sem)

    def start(i, carry):
        copy(i).start()
        return carry

    def wait(i, carry):
        copy(i).wait()
        return carry

    lax.fori_loop(0, n, start, 0)
    lax.fori_loop(0, n, wait, 0)


def _row_gather(idx, src, name):
    n_idx = idx.shape[0]
    n_src, width = src.shape
    n = min(GATHER_ROWS, n_idx)
    out = pl.pallas_call(
        functools.partial(_gather_body, n=n),
        grid=(n_idx // n,),
        in_specs=[
            pl.BlockSpec((1, 1, n), lambda i: (i, 0, 0), memory_space=pltpu.SMEM),
            pl.BlockSpec(memory_space=pl.ANY),
        ],
        out_specs=pl.BlockSpec(memory_space=pl.ANY),
        out_shape=jax.ShapeDtypeStruct((n_idx, width // LANES, LANES), src.dtype),
        scratch_shapes=[pltpu.SemaphoreType.DMA],
        compiler_params=pltpu.CompilerParams(dimension_semantics=("arbitrary",)),
        name=name,
    )(idx.reshape(n_idx // n, 1, n), src.reshape(n_src, width // LANES, LANES))
    return out.reshape(n_idx, width)


def _cast_tile(dst_ref, src_ref, step=256):
    for r in range(0, dst_ref.shape[0], step):
        dst_ref[r:r + step, :] = src_ref[0, r:r + step, :].astype(BF16)


def _moe_up_body(e_ref, wt_ref, r_ref, ot_ref, first_ref, valid_ref,
                 x_ref, wg_ref, wu_ref, bg_ref, bu_ref, o_ref, wg_scr, wu_scr):
    w = pl.program_id(0)

    @pl.when(first_ref[w] == 1)
    def _():
        _cast_tile(wg_scr, wg_ref)
        _cast_tile(wu_scr, wu_ref)

    @pl.when(valid_ref[w] == 1)
    def _():
        x = x_ref[...]
        gt = jnp.dot(x, wg_scr[...], preferred_element_type=F32) + bg_ref[0]
        up = jnp.dot(x, wu_scr[...], preferred_element_type=F32) + bu_ref[0]
        gt = jnp.minimum(gt, SWIGLU_LIMIT)
        up = jnp.clip(up, -SWIGLU_LIMIT, SWIGLU_LIMIT)
        o_ref[...] = ((up + 1.0) * (gt * _sigmoid(SWIGLU_ALPHA * gt))).astype(o_ref.dtype)

    @pl.when(valid_ref[w] == 0)
    def _():
        o_ref[...] = jnp.zeros_like(o_ref)


def _moe_down_body(e_ref, wt_ref, r_ref, ot_ref, first_ref, valid_ref, a_ref, wd_ref, bd_ref, o_ref, wd_scr):
    w = pl.program_id(0)

    @pl.when(first_ref[w] == 1)
    def _():
        _cast_tile(wd_scr, wd_ref)

    @pl.when(valid_ref[w] == 1)
    def _():
        y = jnp.dot(a_ref[...], wd_scr[...], preferred_element_type=F32) + bd_ref[0]
        o_ref[...] = y.astype(o_ref.dtype)

    @pl.when(valid_ref[w] == 0)
    def _():
        o_ref[...] = jnp.zeros_like(o_ref)


def _moe_schedule(blocks_e, n_tiles, n_blocks):
    n_items = n_tiles * n_blocks
    bstart = jnp.cumsum(blocks_e) - blocks_e
    item_end = n_tiles * jnp.cumsum(blocks_e)
    total = item_end[-1]
    w = jnp.arange(n_items, dtype=jnp.int32)
    valid = w < total
    wc = jnp.minimum(w, jnp.maximum(total - 1, 0))
    e = jnp.minimum(jnp.searchsorted(item_end, wc, side="right"), N_EXPERTS - 1).astype(jnp.int32)
    local = wc - n_tiles * bstart[e]
    nb = jnp.maximum(blocks_e[e], 1)
    wtile = local // nb
    jblk = local % nb
    spare = w - total
    rblk = jnp.where(valid, bstart[e] + jblk, total // n_tiles + spare // n_tiles)
    otile = jnp.where(valid, wtile, spare % n_tiles)
    first = (jblk == 0) & valid
    i32 = lambda a: a.astype(jnp.int32)
    return (e, i32(wtile), i32(rblk), i32(otile), i32(first), i32(valid))


def _moe_up(sched, xs, w_gate, w_up, b_gate, b_up):
    n_slots, D = xs.shape
    d_ff = w_gate.shape[2]
    tm, tf = MOE_TM, min(MOE_TF, d_ff)
    n_items = sched[0].shape[0]
    wspec = pl.BlockSpec((1, D, tf), lambda w, e, wt, r, ot, fi, va: (e[w], 0, wt[w]))
    bspec = pl.BlockSpec((1, 1, tf), lambda w, e, wt, r, ot, fi, va: (e[w], 0, wt[w]))
    grid_spec = pltpu.PrefetchScalarGridSpec(
        num_scalar_prefetch=6,
        grid=(n_items,),
        in_specs=[pl.BlockSpec((tm, D), lambda w, e, wt, r, ot, fi, va: (r[w], 0)), wspec, wspec, bspec, bspec],
        out_specs=pl.BlockSpec((tm, tf), lambda w, e, wt, r, ot, fi, va: (r[w], ot[w])),
        scratch_shapes=[pltpu.VMEM((D, tf), BF16), pltpu.VMEM((D, tf), BF16)],
    )
    return pl.pallas_call(
        _moe_up_body,
        grid_spec=grid_spec,
        out_shape=jax.ShapeDtypeStruct((n_slots, d_ff), BF16),
        compiler_params=_cparams(("arbitrary",)),
        name="moe_up",
    )(*sched, xs, w_gate, w_up, b_gate.reshape(N_EXPERTS, 1, d_ff), b_up.reshape(N_EXPERTS, 1, d_ff))


def _moe_down(sched, act, w_down, b_down):
    n_slots, d_ff = act.shape
    D = w_down.shape[2]
    tm, tn = MOE_TM, min(MOE_TF, D)
    n_items = sched[0].shape[0]
    grid_spec = pltpu.PrefetchScalarGridSpec(
        num_scalar_prefetch=6,
        grid=(n_items,),
        in_specs=[
            pl.BlockSpec((tm, d_ff), lambda w, e, wt, r, ot, fi, va: (r[w], 0)),
            pl.BlockSpec((1, d_ff, tn), lambda w, e, wt, r, ot, fi, va: (e[w], 0, wt[w])),
            pl.BlockSpec((1, 1, tn), lambda w, e, wt, r, ot, fi, va: (e[w], 0, wt[w])),
        ],
        out_specs=pl.BlockSpec((tm, tn), lambda w, e, wt, r, ot, fi, va: (r[w], ot[w])),
        scratch_shapes=[pltpu.VMEM((d_ff, tn), BF16)],
    )
    return pl.pallas_call(
        _moe_down_body,
        grid_spec=grid_spec,
        out_shape=jax.ShapeDtypeStruct((n_slots, D), BF16),
        compiler_params=_cparams(("arbitrary",)),
        name="moe_down",
    )(*sched, act, w_down, b_down.reshape(N_EXPERTS, 1, D))


def _final_body(x2_ref, y_ref, gate_ref, w_ref, o_ref):
    D = x2_ref.shape[1]
    gate = gate_ref[...]
    acc = x2_ref[...]
    for k in range(TOP_K):
        acc = acc + gate[:, k:k + 1] * y_ref[:, k * D:(k + 1) * D].astype(F32)
    o_ref[...] = acc * lax.rsqrt(jnp.mean(acc * acc, axis=-1, keepdims=True) + EPS) * w_ref[...]


def _final(x2, ysg, gates_pad, w):
    T, D = x2.shape
    tm = min(FINAL_TM, T)
    return pl.pallas_call(
        _final_body,
        grid=(T // tm,),
        in_specs=[
            pl.BlockSpec((tm, D), lambda i: (i, 0)),
            pl.BlockSpec((tm, TOP_K * D), lambda i: (i, 0)),
            pl.BlockSpec((tm, LANES), lambda i: (i, 0)),
            pl.BlockSpec((1, D), lambda i: (0, 0)),
        ],
        out_specs=pl.BlockSpec((tm, D), lambda i: (i, 0)),
        out_shape=jax.ShapeDtypeStruct((T, D), F32),
        compiler_params=_cparams(("parallel",)),
        name="final_norm",
    )(x2, ysg, gates_pad, w)


def _moe(h2, top_idx, w_gate, b_gate, w_up, b_up, w_down, b_down):
    T, D = h2.shape
    A = T * TOP_K
    tm = MOE_TM
    n_blocks = (A + N_EXPERTS * (tm - 1) + tm - 1) // tm
    per_step = max(GATHER_ROWS // tm, 1)
    n_blocks = -(-n_blocks // per_step) * per_step
    n_slots = n_blocks * tm

    flat_e = top_idx.reshape(A)
    onehot = (flat_e[:, None] == jnp.arange(N_EXPERTS, dtype=jnp.int32)[None, :]).astype(jnp.int32)
    csum = jnp.cumsum(onehot, axis=0)
    rank = jnp.sum(onehot * csum, axis=1) - 1
    counts = csum[-1]
    blocks_e = (counts + tm - 1) // tm
    bstart = jnp.cumsum(blocks_e) - blocks_e
    dest = (bstart[flat_e] * tm + rank).astype(jnp.int32)
    slot_tok = jnp.zeros((n_slots,), jnp.int32).at[dest].set(jnp.arange(A, dtype=jnp.int32) // TOP_K)

    xs = _row_gather(slot_tok, h2, "moe_dispatch")
    d_ff = w_gate.shape[2]
    act = _moe_up(_moe_schedule(blocks_e, -(-d_ff // MOE_TF), n_blocks), xs, w_gate, w_up, b_gate, b_up)
    ys = _moe_down(_moe_schedule(blocks_e, -(-D // MOE_TF), n_blocks), act, w_down, b_down)
    return _row_gather(dest, ys, "moe_collect").reshape(T, TOP_K * D)


def kernel(x, ln1_w, w_in, hg_lb_logits, hg_norm_w, ml_conv_w, ml_conv_b, ml_igate_b, ml_fgate_b, ml_norm_w,
           w_out, ln2_w, w_router, b_router, w_gate, b_gate, w_up, b_up, w_down, b_down, final_norm_w):
    B, S, D = x.shape
    T = B * S
    depth = w_in.shape[0]
    hg_w = HG_HEADS * HG_DK
    n_main = 4 * hg_w + 2 * ML_HEADS * ML_DQK + 2 * ML_HEADS * ML_DV
    lb_all = jnp.cumsum(jax.nn.softmax(hg_lb_logits.astype(F32), axis=0), axis=0)

    xc = x.reshape(T, D)
    for l in range(depth):
        w_gates_pad = jnp.pad(w_in[l][:, n_main:], ((0, 0), (0, LANES - 2 * ML_HEADS))).astype(BF16)
        proj3, gates = _inproj(xc, ln1_w[l].reshape(1, D), w_in[l], w_gates_pad, n_main)
        a_out = _hgrn(proj3, lb_all[l], hg_norm_w[l], B, S)
        gate_bias = jnp.pad(jnp.concatenate([ml_igate_b[l], ml_fgate_b[l]]), (0, LANES - 2 * ML_HEADS))
        b_out = _mlstm(proj3, gates, gate_bias.reshape(1, LANES), ml_conv_w[l], ml_conv_b[l], ml_norm_w[l],
                       B, S, 4 * HG_HEADS)
        wr_pad = jnp.pad(w_router[l], ((0, 0), (0, LANES - N_EXPERTS)))
        br_pad = jnp.pad(b_router[l], (0, LANES - N_EXPERTS), constant_values=-1e30).reshape(1, LANES)
        x2, h2, idx_pad, gates_pad = _outproj(a_out, b_out, xc, w_out[l].astype(BF16), ln2_w[l].reshape(1, D),
                                              wr_pad, br_pad)
        ysg = _moe(h2, idx_pad[:, :TOP_K], w_gate[l], b_gate[l], w_up[l], b_up[l], w_down[l], b_down[l])
        if l + 1 < depth:
            raise NotImplementedError("only the final layer fuses the output norm")
        xc = _final(x2, ysg, gates_pad, final_norm_w.reshape(1, D))
    return xc.reshape(B, S, D)
```

```python
import functools

import jax
import jax.numpy as jnp
from jax import lax
from jax.experimental import pallas as pl
from jax.experimental.pallas import tpu as pltpu

F32 = jnp.float32
BF16 = jnp.bfloat16
HIGHEST = lax.Precision.HIGHEST

EPS = 1e-6
HG_HEADS = 8
HG_DK = 128
ML_HEADS = 4
ML_DQK = 128
ML_DV = 256
CONV_W = 4
N_EXPERTS = 32
TOP_K = 4
SWIGLU_ALPHA = 1.702
SWIGLU_LIMIT = 7.0

LANES = 128
SUBLANES = 8
VMEM_LIMIT_BYTES = 56 * 1024 * 1024

HG_CHUNK = 64
HG_SUB = 16
ML_CHUNK = 128
MIX_ROWS = 512

INPROJ_TM = 1024
INPROJ_TN = 512
OUTPROJ_TM = 256
MOE_TM = 512
MOE_TF = 1024
MOE_DOWN_TM = 256
GATHER_ROWS = 1024
FINAL_TM = 256


def _dot_nt(a, b):
    return lax.dot_general(a, b, (((1,), (1,)), ((), ())), preferred_element_type=F32)


def _dot_tn(a, b):
    return lax.dot_general(a, b, (((0,), (0,)), ((), ())), preferred_element_type=F32)


def _log_sigmoid(z):
    return jnp.minimum(z, 0.0) - jnp.log1p(jnp.exp(-jnp.abs(z)))


def _sigmoid(z):
    return 1.0 / (1.0 + jnp.exp(-z))


def _cparams(semantics):
    return pltpu.CompilerParams(dimension_semantics=semantics, vmem_limit_bytes=VMEM_LIMIT_BYTES)


_HI_MASK = 0xFFFF0000


def _packed_rows(d):
    return d // (2 * LANES)


def _pack_store(o_ref, v):
    n, d = v.shape
    s, half = _packed_rows(d), d // 2
    bits = pltpu.bitcast(v.astype(BF16).astype(F32), jnp.uint32)
    for c in range(s):
        hi = bits[:, c * LANES:(c + 1) * LANES]
        lo = bits[:, half + c * LANES:half + (c + 1) * LANES]
        o_ref[pl.ds(c, n, stride=s), :] = hi | jnp.right_shift(lo, jnp.uint32(16))


def _unpack_load(buf, first_row, n, s, c):
    w = buf[pl.ds(first_row * s + c, n, stride=s), :]
    hi = pltpu.bitcast(w & jnp.uint32(_HI_MASK), F32)
    lo = pltpu.bitcast(jnp.left_shift(w, jnp.uint32(16)), F32)
    return hi, lo


def _inproj_body(x_ref, lnw_ref, w_ref, wg_ref, o_ref, g_ref, h_scr, *, tn):
    @pl.when(pl.program_id(1) == 0)
    def _():
        x = x_ref[...]
        h = x * lax.rsqrt(jnp.mean(x * x, axis=-1, keepdims=True) + EPS) * lnw_ref[...]
        hb = h.astype(BF16)
        h_scr[...] = hb
        g_ref[...] = jnp.dot(hb, wg_ref[...], preferred_element_type=F32)

    res = jnp.dot(h_scr[...], w_ref[...].astype(BF16), preferred_element_type=F32)
    for c in range(tn // LANES):
        o_ref[c] = res[:, c * LANES:(c + 1) * LANES]


def _inproj(x2d, ln_w, w_in, w_gates_pad, n_main):
    T, D = x2d.shape
    tm = min(INPROJ_TM, T)
    tn = INPROJ_TN
    n_slabs = n_main // LANES
    return pl.pallas_call(
        functools.partial(_inproj_body, tn=tn),
        grid=(T // tm, n_main // tn),
        in_specs=[
            pl.BlockSpec((tm, D), lambda i, j: (i, 0)),
            pl.BlockSpec((1, D), lambda i, j: (0, 0)),
            pl.BlockSpec((D, tn), lambda i, j: (0, j)),
            pl.BlockSpec((D, LANES), lambda i, j: (0, 0)),
        ],
        out_specs=[
            pl.BlockSpec((tn // LANES, tm, LANES), lambda i, j: (j, i, 0)),
            pl.BlockSpec((tm, LANES), lambda i, j: (i, 0)),
        ],
        out_shape=[
            jax.ShapeDtypeStruct((n_slabs, T, LANES), F32),
            jax.ShapeDtypeStruct((T, LANES), F32),
        ],
        scratch_shapes=[pltpu.VMEM((tm, D), BF16)],
        compiler_params=_cparams(("parallel", "arbitrary")),
        name="inproj",
    )(x2d, ln_w, w_in, w_gates_pad)


def _hgrn_body(q_ref, f_ref, i_ref, g_ref, lb_ref, nw_ref, o_ref, st_scr, *, rows):
    C, SUB = HG_CHUNK, HG_SUB
    nsub = C // SUB

    @pl.when(pl.program_id(2) == 0)
    def _():
        st_scr[...] = jnp.zeros_like(st_scr)

    lb = lb_ref[0]
    log_lb = jnp.log(lb)
    log_1mlb = jnp.log1p(-lb)
    one_m_lb = 1.0 - lb
    nw = nw_ref[0]

    r_i = lax.broadcasted_iota(jnp.int32, (C, C), 0)
    c_i = lax.broadcasted_iota(jnp.int32, (C, C), 1)
    tri = (c_i <= r_i).astype(F32)
    sub_shift = SUB.bit_length() - 1
    diag_mask = (c_i <= r_i) & (jnp.right_shift(r_i, sub_shift) == jnp.right_shift(c_i, sub_shift))
    row_id = lax.broadcasted_iota(jnp.int32, (C, HG_DK), 0)

    for j in range(rows // C):
        sl = pl.ds(j * C, C)
        q = q_ref[0, sl, :]
        z = f_ref[0, sl, :]
        v = i_ref[0, sl, :]
        g = g_ref[0, sl, :]

        e = jnp.exp(-jnp.abs(z))
        log_sig = jnp.minimum(z, 0.0) - jnp.log1p(e)
        sig_neg = jnp.where(z >= 0, e, 1.0) / (1.0 + e)
        cc = log_1mlb + log_sig
        log_f = jnp.maximum(log_lb, cc) + jnp.log1p(jnp.exp(-jnp.abs(log_lb - cc)))
        kk = one_m_lb * sig_neg
        b = jnp.dot(tri, log_f, precision=HIGHEST, preferred_element_type=F32)
        b_last = b[C - 1:C, :]

        st = st_scr[...]
        vb = v.astype(BF16)
        qe = (q * jnp.exp(b)).astype(BF16)
        kdec = (kk * jnp.exp(b_last - b)).astype(BF16)
        inter = _dot_nt(qe, st.astype(BF16))

        refs = [b[I * SUB:I * SUB + 1, :] for I in range(nsub)]
        refb = jnp.concatenate([jnp.broadcast_to(r, (SUB, HG_DK)) for r in refs], axis=0)
        qd = (q * jnp.exp(b - refb)).astype(BF16)
        kd = (kk * jnp.exp(refb - b)).astype(BF16)
        att = jnp.where(diag_mask, _dot_nt(qd, kd), 0.0)
        q_parts, k_parts = [], []
        for J in range(nsub - 1):
            r = refs[J + 1]
            qj = q * jnp.exp(jnp.minimum(b - r, 0.0))
            kj = kk * jnp.exp(jnp.minimum(r - b, 0.0))
            q_parts.append(jnp.where(row_id >= (J + 1) * SUB, qj, 0.0).astype(BF16))
            k_parts.append(jnp.where((row_id >= J * SUB) & (row_id < (J + 1) * SUB), kj, 0.0).astype(BF16))
        att = att + _dot_nt(jnp.concatenate(q_parts, axis=1), jnp.concatenate(k_parts, axis=1))
        intra = jnp.dot(att.astype(BF16), vb, preferred_element_type=F32)

        o = inter + intra
        y = o * lax.rsqrt(jnp.mean(o * o, axis=-1, keepdims=True) + EPS) * nw
        o_ref[sl, :] = (y * (g * _sigmoid(g))).astype(o_ref.dtype)

        st_scr[...] = st * jnp.exp(b_last) + _dot_tn(vb, kdec)


def _hgrn(proj3, lb, norm_w, batch, seq):
    T = batch * seq
    rows = min(MIX_ROWS, seq)
    nblk = seq // rows
    H = HG_HEADS

    def slab(off):
        return pl.BlockSpec((1, rows, LANES), lambda b, h, c, off=off: (off + h, b * nblk + c, 0))

    vec = pl.BlockSpec((1, 1, LANES), lambda b, h, c: (h, 0, 0))
    return pl.pallas_call(
        functools.partial(_hgrn_body, rows=rows),
        grid=(batch, H, nblk),
        in_specs=[slab(0), slab(H), slab(2 * H), slab(3 * H), vec, vec],
        out_specs=pl.BlockSpec((rows, LANES), lambda b, h, c: (b * nblk + c, h)),
        out_shape=jax.ShapeDtypeStruct((T, H * LANES), BF16),
        scratch_shapes=[pltpu.VMEM((LANES, HG_DK), F32)],
        compiler_params=_cparams(("parallel", "parallel", "arbitrary")),
        name="hgrn2",
    )(proj3, proj3, proj3, proj3, lb.reshape(H, 1, HG_DK), norm_w.reshape(H, 1, LANES))


def _mlstm_body(q_ref, k_ref, v_ref, og_ref, gt_ref, gb_ref, cwq_ref, cwk_ref, cbq_ref, cbk_ref, nw_ref,
                out_ref, c_scr, n_scr, m_scr, qx_scr, kx_scr, qc_scr, kc_scr, *, rows):
    C = ML_CHUNK
    PAD = SUBLANES
    h = pl.program_id(1)

    @pl.when(pl.program_id(2) == 0)
    def _():
        c_scr[...] = jnp.zeros_like(c_scr)
        n_scr[...] = jnp.zeros_like(n_scr)
        m_scr[...] = jnp.zeros_like(m_scr)
        qx_scr[0:PAD, :] = jnp.zeros((PAD, LANES), F32)
        kx_scr[0:PAD, :] = jnp.zeros((PAD, LANES), F32)

    qx_scr[PAD:PAD + rows, :] = q_ref[0]
    kx_scr[PAD:PAD + rows, :] = k_ref[0]
    accq = jnp.zeros((rows, LANES), F32) + cbq_ref[...]
    acck = jnp.zeros((rows, LANES), F32) + cbk_ref[...]
    for j in range(CONV_W):
        off = PAD - (CONV_W - 1) + j
        accq = accq + cwq_ref[j:j + 1, :] * qx_scr[pl.ds(off, rows), :]
        acck = acck + cwk_ref[j:j + 1, :] * kx_scr[pl.ds(off, rows), :]
    qc_scr[...] = accq * _sigmoid(accq) * (ML_DQK ** -0.5)
    kc_scr[...] = acck * _sigmoid(acck)
    qx_scr[0:PAD, :] = qx_scr[rows:rows + PAD, :]
    kx_scr[0:PAD, :] = kx_scr[rows:rows + PAD, :]

    lane = lax.broadcasted_iota(jnp.int32, (C, C), 1)
    sub = lax.broadcasted_iota(jnp.int32, (C, C), 0)
    causal = lane <= sub
    lower = causal.astype(F32)
    upper = (sub <= lane).astype(F32)
    nw = nw_ref[...]
    gbias = gb_ref[...]

    for j in range(rows // C):
        sl = pl.ds(j * C, C)
        gc = gt_ref[sl, :] + gbias
        gct = gc.T
        li_col = jnp.sum(jnp.where(lane == h, gc, 0.0), axis=1, keepdims=True)
        fg_col = jnp.sum(jnp.where(lane == h + ML_HEADS, gc, 0.0), axis=1, keepdims=True)
        li_row = jnp.sum(jnp.where(sub == h, gct, 0.0), axis=0, keepdims=True)
        fg_row = jnp.sum(jnp.where(sub == h + ML_HEADS, gct, 0.0), axis=0, keepdims=True)
        lf_col = _log_sigmoid(fg_col)
        lf_row = _log_sigmoid(fg_row)
        g_t = jnp.dot(lower, jnp.broadcast_to(lf_col, (C, C)), precision=HIGHEST,
                      preferred_element_type=F32)
        g_s = jnp.dot(jnp.broadcast_to(lf_row, (C, C)), upper, precision=HIGHEST,
                      preferred_element_type=F32)
        g_col = g_t[:, 0:1]
        g_last = g_t[C - 1:C, 0:1]
        m_prev = m_scr[:, 0:1]

        a_inter = g_col + m_prev
        dmat = jnp.where(causal, g_t - g_s + li_row, -jnp.inf)
        m_t = jnp.maximum(a_inter, jnp.max(dmat, axis=1, keepdims=True))
        w_inter = jnp.exp(a_inter - m_t)
        p = jnp.exp(dmat - m_t)

        qf = qc_scr[sl, :]
        kf = kc_scr[sl, :]
        qb = qf.astype(BF16)
        vb = jnp.concatenate([v_ref[0, sl, :], v_ref[1, sl, :]], axis=1).astype(BF16)
        cm = c_scr[...]
        nv = n_scr[...]
        sqk = _dot_nt(qb, kf.astype(BF16)) * p
        num = (w_inter * jnp.dot(qb, cm.astype(BF16), preferred_element_type=F32)
               + jnp.dot(sqk.astype(BF16), vb, preferred_element_type=F32))
        den = (w_inter * jnp.sum(qf * nv, axis=1, keepdims=True)
               + jnp.sum(sqk, axis=1, keepdims=True))
        hh = num * (1.0 / jnp.maximum(jnp.abs(den), jnp.exp(-m_t)))

        log_ws = g_last - g_col + li_col
        m_new = jnp.maximum(g_last + m_prev, jnp.max(log_ws, axis=0, keepdims=True))
        decay = jnp.exp(g_last + m_prev - m_new)
        kw = kf * jnp.exp(log_ws - m_new)
        c_scr[...] = decay * cm + _dot_tn(kw.astype(BF16), vb)
        n_scr[...] = decay * nv + jnp.sum(kw, axis=0, keepdims=True)
        m_scr[...] = jnp.broadcast_to(m_new, m_scr.shape)

        y = hh * lax.rsqrt(jnp.mean(hh * hh, axis=-1, keepdims=True) + EPS) * nw
        og = jnp.concatenate([og_ref[0, sl, :], og_ref[1, sl, :]], axis=1)
        out_ref[sl, :] = (y * _sigmoid(og)).astype(out_ref.dtype)


def _mlstm(proj3, gates, gate_bias_pad, conv_w, conv_b, norm_w, batch, seq, q_off):
    T = batch * seq
    rows = min(MIX_ROWS, seq)
    nblk = seq // rows
    H = ML_HEADS
    k_off = q_off + H
    v_off = k_off + H
    o_off = v_off + 2 * H

    def slab(off):
        return pl.BlockSpec((1, rows, LANES), lambda b, h, c, off=off: (off + h, b * nblk + c, 0))

    def slab2(off):
        return pl.BlockSpec((2, rows, LANES), lambda b, h, c, off=off: (off // 2 + h, b * nblk + c, 0))

    qk_w = H * ML_DQK
    return pl.pallas_call(
        functools.partial(_mlstm_body, rows=rows),
        grid=(batch, H, nblk),
        in_specs=[
            slab(q_off), slab(k_off), slab2(v_off), slab2(o_off),
            pl.BlockSpec((rows, LANES), lambda b, h, c: (b * nblk + c, 0)),
            pl.BlockSpec((1, LANES), lambda b, h, c: (0, 0)),
            pl.BlockSpec((CONV_W, LANES), lambda b, h, c: (0, h)),
            pl.BlockSpec((CONV_W, LANES), lambda b, h, c: (0, H + h)),
            pl.BlockSpec((1, LANES), lambda b, h, c: (0, h)),
            pl.BlockSpec((1, LANES), lambda b, h, c: (0, H + h)),
            pl.BlockSpec((1, ML_DV), lambda b, h, c: (0, h)),
        ],
        out_specs=pl.BlockSpec((rows, ML_DV), lambda b, h, c: (b * nblk + c, h)),
        out_shape=jax.ShapeDtypeStruct((T, H * ML_DV), BF16),
        scratch_shapes=[
            pltpu.VMEM((ML_DQK, ML_DV), F32),
            pltpu.VMEM((1, ML_DQK), F32),
            pltpu.VMEM((1, LANES), F32),
            pltpu.VMEM((rows + 2 * SUBLANES, LANES), F32),
            pltpu.VMEM((rows + 2 * SUBLANES, LANES), F32),
            pltpu.VMEM((rows, LANES), F32),
            pltpu.VMEM((rows, LANES), F32),
        ],
        compiler_params=_cparams(("parallel", "parallel", "arbitrary")),
        name="mlstm",
    )(proj3, proj3, proj3, proj3, gates, gate_bias_pad, conv_w, conv_w,
      conv_b.reshape(1, 2 * qk_w), conv_b.reshape(1, 2 * qk_w), norm_w.reshape(1, H * ML_DV))


def _outproj_body(a_ref, b_ref, x_ref, wo_ref, ln_ref, wr_ref, br_ref,
                  x2_ref, h2_ref, idx_ref, gate_ref):
    ka = a_ref.shape[1]
    res = (jnp.dot(a_ref[...], wo_ref[0:ka, :], preferred_element_type=F32)
           + jnp.dot(b_ref[...], wo_ref[ka:, :], preferred_element_type=F32))
    x2 = x_ref[...] + res
    x2_ref[...] = x2
    h2 = x2 * lax.rsqrt(jnp.mean(x2 * x2, axis=-1, keepdims=True) + EPS) * ln_ref[...]
    _pack_store(h2_ref, h2)

    logits = jnp.dot(h2, wr_ref[...], precision=HIGHEST, preferred_element_type=F32) + br_ref[...]
    lane = lax.broadcasted_iota(jnp.int32, logits.shape, 1).astype(F32)
    vals, idxs = [], []
    cur = logits
    for _ in range(TOP_K):
        m = jnp.max(cur, axis=1, keepdims=True)
        ix = jnp.min(jnp.where(cur == m, lane, float(LANES)), axis=1, keepdims=True)
        vals.append(m)
        idxs.append(ix)
        cur = jnp.where(lane == ix, -jnp.inf, cur)
    es = [jnp.exp(v - vals[0]) for v in vals]
    inv = 1.0 / (es[0] + es[1] + es[2] + es[3])
    gate = jnp.zeros(logits.shape, F32)
    idx = jnp.zeros(logits.shape, F32)
    for k in range(TOP_K):
        gate = jnp.where(lane == float(k), es[k] * inv, gate)
        idx = jnp.where(lane == float(k), idxs[k], idx)
    gate_ref[...] = gate
    idx_ref[...] = idx.astype(jnp.int32)


def _outproj(a_out, b_out, x2d, w_out_bf, ln_w, w_router_pad, b_router_pad):
    T, D = x2d.shape
    tm = min(OUTPROJ_TM, T)
    ka, kb = a_out.shape[1], b_out.shape[1]
    s = _packed_rows(D)
    row = lambda w: pl.BlockSpec((tm, w), lambda i: (i, 0))
    full = lambda r, c: pl.BlockSpec((r, c), lambda i: (0, 0))
    return pl.pallas_call(
        _outproj_body,
        grid=(T // tm,),
        in_specs=[row(ka), row(kb), row(D), full(ka + kb, D), full(1, D), full(D, LANES), full(1, LANES)],
        out_specs=[row(D), pl.BlockSpec((tm * s, LANES), lambda i: (i, 0)), row(LANES), row(LANES)],
        out_shape=[
            jax.ShapeDtypeStruct((T, D), F32),
            jax.ShapeDtypeStruct((T * s, LANES), jnp.uint32),
            jax.ShapeDtypeStruct((T, LANES), jnp.int32),
            jax.ShapeDtypeStruct((T, LANES), F32),
        ],
        compiler_params=_cparams(("parallel",)),
        name="outproj_router",
    )(a_out, b_out, x2d, w_out_bf, ln_w, w_router_pad, b_router_pad)


def _start_row_gather(idx_at, src_ref, buf, sem, n, s, slot_of=lambda i: i):
    def start(i, carry):
        src_row = pl.multiple_of(idx_at(i) * s, s)
        dst_row = pl.multiple_of(slot_of(i) * s, s)
        pltpu.make_async_copy(src_ref.at[pl.ds(src_row, s), :], buf.at[pl.ds(dst_row, s), :], sem).start()
        return carry

    lax.fori_loop(0, n, start, 0, unroll=8)


def _wait_row_gather(buf, sem):
    pltpu.make_async_copy(buf, buf, sem).wait()


def _dispatch_body(idx_ref, src_ref, o_ref, buf, sem, *, n, s):
    _start_row_gather(lambda i: idx_ref[0, 0, i], src_ref, buf, sem, n, s)
    _wait_row_gather(buf, sem)
    half = o_ref.shape[1] // 2
    for c in range(s):
        hi, lo = _unpack_load(buf, 0, n, s, c)
        o_ref[:, c * LANES:(c + 1) * LANES] = hi.astype(o_ref.dtype)
        o_ref[:, half + c * LANES:half + (c + 1) * LANES] = lo.astype(o_ref.dtype)


def _dispatch(idx, src_packed, d):
    n_idx = idx.shape[0]
    s = _packed_rows(d)
    n = min(GATHER_ROWS, n_idx)
    return pl.pallas_call(
        functools.partial(_dispatch_body, n=n, s=s),
        grid=(n_idx // n,),
        in_specs=[
            pl.BlockSpec((1, 1, n), lambda i: (i, 0, 0), memory_space=pltpu.SMEM),
            pl.BlockSpec(memory_space=pl.ANY),
        ],
        out_specs=pl.BlockSpec((n, d), lambda i: (i, 0)),
        out_shape=jax.ShapeDtypeStruct((n_idx, d), BF16),
        scratch_shapes=[pltpu.VMEM((n * s, LANES), jnp.uint32), pltpu.SemaphoreType.DMA],
        compiler_params=_cparams(("arbitrary",)),
        name="moe_dispatch",
    )(idx.reshape(n_idx // n, 1, n), src_packed)


def _cast_tile(dst_ref, src_ref, step=256):
    for r in range(0, dst_ref.shape[0], step):
        dst_ref[r:r + step, :] = src_ref[0, r:r + step, :].astype(BF16)


def _moe_up_body(e_ref, wt_ref, r_ref, ot_ref, first_ref, valid_ref,
                 x_ref, wg_ref, wu_ref, bg_ref, bu_ref, o_ref, wg_scr, wu_scr):
    w = pl.program_id(0)

    @pl.when(first_ref[w] == 1)
    def _():
        _cast_tile(wg_scr, wg_ref)
        _cast_tile(wu_scr, wu_ref)

    @pl.when(valid_ref[w] == 1)
    def _():
        x = x_ref[...]
        gt = jnp.dot(x, wg_scr[...], preferred_element_type=F32) + bg_ref[0]
        up = jnp.dot(x, wu_scr[...], preferred_element_type=F32) + bu_ref[0]
        gt = jnp.minimum(gt, SWIGLU_LIMIT)
        up = jnp.clip(up, -SWIGLU_LIMIT, SWIGLU_LIMIT)
        o_ref[...] = ((up + 1.0) * (gt * _sigmoid(SWIGLU_ALPHA * gt))).astype(o_ref.dtype)

    @pl.when(valid_ref[w] == 0)
    def _():
        o_ref[...] = jnp.zeros_like(o_ref)


def _moe_down_body(e_ref, wt_ref, r_ref, ot_ref, first_ref, valid_ref, a_ref, wd_ref, bd_ref, o_ref, wd_scr):
    w = pl.program_id(0)

    @pl.when(first_ref[w] == 1)
    def _():
        _cast_tile(wd_scr, wd_ref)

    @pl.when(valid_ref[w] == 1)
    def _():
        y = jnp.dot(a_ref[...], wd_scr[...], preferred_element_type=F32) + bd_ref[0]
        _pack_store(o_ref, y)

    @pl.when(valid_ref[w] == 0)
    def _():
        o_ref[...] = jnp.zeros_like(o_ref)


def _moe_schedule(blocks_e, n_tiles, n_blocks):
    n_items = n_tiles * n_blocks
    bstart = jnp.cumsum(blocks_e) - blocks_e
    item_end = n_tiles * jnp.cumsum(blocks_e)
    total = item_end[-1]
    w = jnp.arange(n_items, dtype=jnp.int32)
    valid = w < total
    wc = jnp.minimum(w, jnp.maximum(total - 1, 0))
    e = jnp.minimum(jnp.searchsorted(item_end, wc, side="right"), N_EXPERTS - 1).astype(jnp.int32)
    local = wc - n_tiles * bstart[e]
    nb = jnp.maximum(blocks_e[e], 1)
    wtile = local // nb
    jblk = local % nb
    spare = w - total
    rblk = jnp.where(valid, bstart[e] + jblk, total // n_tiles + spare // n_tiles)
    otile = jnp.where(valid, wtile, spare % n_tiles)
    first = (jblk == 0) & valid
    i32 = lambda a: a.astype(jnp.int32)
    return (e, i32(wtile), i32(rblk), i32(otile), i32(first), i32(valid))


def _moe_up(sched, xs, w_gate, w_up, b_gate, b_up):
    n_slots, D = xs.shape
    d_ff = w_gate.shape[2]
    tm, tf = MOE_TM, min(MOE_TF, d_ff)
    n_items = sched[0].shape[0]
    wspec = pl.BlockSpec((1, D, tf), lambda w, e, wt, r, ot, fi, va: (e[w], 0, wt[w]))
    bspec = pl.BlockSpec((1, 1, tf), lambda w, e, wt, r, ot, fi, va: (e[w], 0, wt[w]))
    grid_spec = pltpu.PrefetchScalarGridSpec(
        num_scalar_prefetch=6,
        grid=(n_items,),
        in_specs=[pl.BlockSpec((tm, D), lambda w, e, wt, r, ot, fi, va: (r[w], 0)), wspec, wspec, bspec, bspec],
        out_specs=pl.BlockSpec((tm, tf), lambda w, e, wt, r, ot, fi, va: (r[w], ot[w])),
        scratch_shapes=[pltpu.VMEM((D, tf), BF16), pltpu.VMEM((D, tf), BF16)],
    )
    return pl.pallas_call(
        _moe_up_body,
        grid_spec=grid_spec,
        out_shape=jax.ShapeDtypeStruct((n_slots, d_ff), BF16),
        compiler_params=_cparams(("arbitrary",)),
        name="moe_up",
    )(*sched, xs, w_gate, w_up, b_gate.reshape(N_EXPERTS, 1, d_ff), b_up.reshape(N_EXPERTS, 1, d_ff))


def _moe_down(sched, act, w_down, b_down):
    n_slots, d_ff = act.shape
    D = w_down.shape[2]
    tm = MOE_DOWN_TM
    s = _packed_rows(D)
    n_items = sched[0].shape[0]
    grid_spec = pltpu.PrefetchScalarGridSpec(
        num_scalar_prefetch=6,
        grid=(n_items,),
        in_specs=[
            pl.BlockSpec((tm, d_ff), lambda w, e, wt, r, ot, fi, va: (r[w], 0)),
            pl.BlockSpec((1, d_ff, D), lambda w, e, wt, r, ot, fi, va: (e[w], 0, 0)),
            pl.BlockSpec((1, 1, D), lambda w, e, wt, r, ot, fi, va: (e[w], 0, 0)),
        ],
        out_specs=pl.BlockSpec((tm * s, LANES), lambda w, e, wt, r, ot, fi, va: (r[w], 0)),
        scratch_shapes=[pltpu.VMEM((d_ff, D), BF16)],
    )
    return pl.pallas_call(
        _moe_down_body,
        grid_spec=grid_spec,
        out_shape=jax.ShapeDtypeStruct((n_slots * s, LANES), jnp.uint32),
        compiler_params=_cparams(("arbitrary",)),
        name="moe_down",
    )(*sched, act, w_down, b_down.reshape(N_EXPERTS, 1, D))


def _final_body(dest_ref, x2_ref, gate_ref, w_ref, ys_ref, o_ref, buf, sem, *, tm, s):
    _start_row_gather(lambda i: dest_ref[0, 0, i], ys_ref, buf, sem, tm * TOP_K, s,
                      slot_of=lambda i: lax.rem(i, TOP_K) * tm + lax.div(i, TOP_K))
    o_ref[...] = x2_ref[...]
    gate = gate_ref[...]
    half = o_ref.shape[1] // 2
    _wait_row_gather(buf, sem)
    for k in range(TOP_K):
        g = gate[:, k:k + 1]
        for c in range(s):
            hi, lo = _unpack_load(buf, k * tm, tm, s, c)
            o_ref[:, c * LANES:(c + 1) * LANES] += g * hi
            o_ref[:, half + c * LANES:half + (c + 1) * LANES] += g * lo
    acc = o_ref[...]
    o_ref[...] = acc * lax.rsqrt(jnp.mean(acc * acc, axis=-1, keepdims=True) + EPS) * w_ref[...]


def _final(x2, ys_packed, dest, gates_pad, w):
    T, D = x2.shape
    tm = min(FINAL_TM, T)
    s = _packed_rows(D)
    n = tm * TOP_K
    return pl.pallas_call(
        functools.partial(_final_body, tm=tm, s=s),
        grid=(T // tm,),
        in_specs=[
            pl.BlockSpec((1, 1, n), lambda i: (i, 0, 0), memory_space=pltpu.SMEM),
            pl.BlockSpec((tm, D), lambda i: (i, 0)),
            pl.BlockSpec((tm, LANES), lambda i: (i, 0)),
            pl.BlockSpec((1, D), lambda i: (0, 0)),
            pl.BlockSpec(memory_space=pl.ANY),
        ],
        out_specs=pl.BlockSpec((tm, D), lambda i: (i, 0)),
        out_shape=jax.ShapeDtypeStruct((T, D), F32),
        scratch_shapes=[pltpu.VMEM((n * s, LANES), jnp.uint32), pltpu.SemaphoreType.DMA],
        compiler_params=_cparams(("arbitrary",)),
        name="final_norm",
    )(dest.reshape(T // tm, 1, n), x2, gates_pad, w, ys_packed)


def _moe(h2_packed, T, D, top_idx, w_gate, b_gate, w_up, b_up, w_down, b_down):
    A = T * TOP_K
    tm = MOE_TM
    n_blocks = (A + N_EXPERTS * (tm - 1) + tm - 1) // tm
    per_step = max(GATHER_ROWS // tm, 1)
    n_blocks = -(-n_blocks // per_step) * per_step
    n_slots = n_blocks * tm

    flat_e = top_idx.reshape(A)
    onehot = (flat_e[:, None] == jnp.arange(N_EXPERTS, dtype=jnp.int32)[None, :]).astype(jnp.int32)
    csum = jnp.cumsum(onehot, axis=0)
    rank = jnp.sum(onehot * csum, axis=1) - 1
    counts = csum[-1]
    blocks_e = (counts + tm - 1) // tm
    bstart = jnp.cumsum(blocks_e) - blocks_e
    dest = (bstart[flat_e] * tm + rank).astype(jnp.int32)
    slot_tok = jnp.zeros((n_slots,), jnp.int32).at[dest].set(jnp.arange(A, dtype=jnp.int32) // TOP_K)

    xs = _dispatch(slot_tok, h2_packed, D)
    d_ff = w_gate.shape[2]
    act = _moe_up(_moe_schedule(blocks_e, -(-d_ff // MOE_TF), n_blocks), xs, w_gate, w_up, b_gate, b_up)
    split = tm // MOE_DOWN_TM
    ys = _moe_down(_moe_schedule(blocks_e * split, 1, n_blocks * split), act, w_down, b_down)
    return ys, dest


def kernel(x, ln1_w, w_in, hg_lb_logits, hg_norm_w, ml_conv_w, ml_conv_b, ml_igate_b, ml_fgate_b, ml_norm_w,
           w_out, ln2_w, w_router, b_router, w_gate, b_gate, w_up, b_up, w_down, b_down, final_norm_w):
    B, S, D = x.shape
    T = B * S
    depth = w_in.shape[0]
    hg_w = HG_HEADS * HG_DK
    n_main = 4 * hg_w + 2 * ML_HEADS * ML_DQK + 2 * ML_HEADS * ML_DV
    lb_all = jnp.cumsum(jax.nn.softmax(hg_lb_logits.astype(F32), axis=0), axis=0)

    xc = x.reshape(T, D)
    for l in range(depth):
        w_gates_pad = jnp.pad(w_in[l][:, n_main:], ((0, 0), (0, LANES - 2 * ML_HEADS))).astype(BF16)
        proj3, gates = _inproj(xc, ln1_w[l].reshape(1, D), w_in[l], w_gates_pad, n_main)
        a_out = _hgrn(proj3, lb_all[l], hg_norm_w[l], B, S)
        gate_bias = jnp.pad(jnp.concatenate([ml_igate_b[l], ml_fgate_b[l]]), (0, LANES - 2 * ML_HEADS))
        b_out = _mlstm(proj3, gates, gate_bias.reshape(1, LANES), ml_conv_w[l], ml_conv_b[l], ml_norm_w[l],
                       B, S, 4 * HG_HEADS)
        wr_pad = jnp.pad(w_router[l], ((0, 0), (0, LANES - N_EXPERTS)))
        br_pad = jnp.pad(b_router[l], (0, LANES - N_EXPERTS), constant_values=-1e30).reshape(1, LANES)
        x2, h2, idx_pad, gates_pad = _outproj(a_out, b_out, xc, w_out[l].astype(BF16), ln2_w[l].reshape(1, D),
                                              wr_pad, br_pad)
        ys, dest = _moe(h2, T, D, idx_pad[:, :TOP_K], w_gate[l], b_gate[l], w_up[l], b_up[l], w_down[l],
                        b_down[l])
        if l + 1 < depth:
            raise NotImplementedError("only the final layer fuses the output norm")
        xc = _final(x2, ys, dest, gates_pad, final_norm_w.reshape(1, D))
    return xc.reshape(B, S, D)
```

```python
import functools

import jax
import jax.numpy as jnp
from jax import lax
from jax.experimental import pallas as pl
from jax.experimental.pallas import tpu as pltpu

F32 = jnp.float32
BF16 = jnp.bfloat16
HIGHEST = lax.Precision.HIGHEST

EPS = 1e-6
HG_HEADS = 8
HG_DK = 128
ML_HEADS = 4
ML_DQK = 128
ML_DV = 256
CONV_W = 4
N_EXPERTS = 32
TOP_K = 4
SWIGLU_ALPHA = 1.702
SWIGLU_LIMIT = 7.0

LANES = 128
SUBLANES = 8
VMEM_LIMIT_BYTES = 56 * 1024 * 1024

HG_CHUNK = 64
HG_SUB = 16
ML_CHUNK = 128
MIX_ROWS = 512

INPROJ_TM = 1024
INPROJ_TN = 512
OUTPROJ_TM = 256
OUTPROJ_SUB = 128
DISPATCH_TOKENS = 256
MOE_TM = 512
MOE_TF = 1024
MOE_DOWN_TM = 256
DMA_UNROLL = 32
FINAL_TM = 128


def _dot_nt(a, b):
    return lax.dot_general(a, b, (((1,), (1,)), ((), ())), preferred_element_type=F32)


def _dot_tn(a, b):
    return lax.dot_general(a, b, (((0,), (0,)), ((), ())), preferred_element_type=F32)


def _log_sigmoid(z):
    return jnp.minimum(z, 0.0) - jnp.log1p(jnp.exp(-jnp.abs(z)))


def _sigmoid(z):
    return 1.0 / (1.0 + jnp.exp(-z))


def _cparams(semantics):
    return pltpu.CompilerParams(dimension_semantics=semantics, vmem_limit_bytes=VMEM_LIMIT_BYTES)


_HI_MASK = 0xFFFF0000


def _packed_rows(d):
    return d // (2 * LANES)


def _pack_store(o_ref, v):
    n, d = v.shape
    s, half = _packed_rows(d), d // 2
    bits = pltpu.bitcast(v.astype(BF16).astype(F32), jnp.uint32)
    for c in range(s):
        hi = bits[:, c * LANES:(c + 1) * LANES]
        lo = bits[:, half + c * LANES:half + (c + 1) * LANES]
        o_ref[pl.ds(c, n, stride=s), :] = hi | jnp.right_shift(lo, jnp.uint32(16))


def _unpack_load(buf, first_row, n, s, c):
    w = buf[pl.ds(first_row * s + c, n, stride=s), :]
    hi = pltpu.bitcast(w & jnp.uint32(_HI_MASK), F32)
    lo = pltpu.bitcast(jnp.left_shift(w, jnp.uint32(16)), F32)
    return hi, lo


def _inproj_body(x_ref, lnw_ref, w_ref, wg_ref, o_ref, g_ref, h_scr, *, tn):
    @pl.when(pl.program_id(1) == 0)
    def _():
        x = x_ref[...]
        h = x * lax.rsqrt(jnp.mean(x * x, axis=-1, keepdims=True) + EPS) * lnw_ref[...]
        hb = h.astype(BF16)
        h_scr[...] = hb
        g_ref[...] = jnp.dot(hb, wg_ref[...], preferred_element_type=F32)

    res = jnp.dot(h_scr[...], w_ref[...].astype(BF16), preferred_element_type=F32)
    for c in range(tn // LANES):
        o_ref[c] = res[:, c * LANES:(c + 1) * LANES]


def _inproj(x2d, ln_w, w_in, w_gates_pad, n_main):
    T, D = x2d.shape
    tm = min(INPROJ_TM, T)
    tn = INPROJ_TN
    n_slabs = n_main // LANES
    return pl.pallas_call(
        functools.partial(_inproj_body, tn=tn),
        grid=(T // tm, n_main // tn),
        in_specs=[
            pl.BlockSpec((tm, D), lambda i, j: (i, 0)),
            pl.BlockSpec((1, D), lambda i, j: (0, 0)),
            pl.BlockSpec((D, tn), lambda i, j: (0, j)),
            pl.BlockSpec((D, LANES), lambda i, j: (0, 0)),
        ],
        out_specs=[
            pl.BlockSpec((tn // LANES, tm, LANES), lambda i, j: (j, i, 0)),
            pl.BlockSpec((tm, LANES), lambda i, j: (i, 0)),
        ],
        out_shape=[
            jax.ShapeDtypeStruct((n_slabs, T, LANES), F32),
            jax.ShapeDtypeStruct((T, LANES), F32),
        ],
        scratch_shapes=[pltpu.VMEM((tm, D), BF16)],
        compiler_params=_cparams(("parallel", "arbitrary")),
        name="inproj",
    )(x2d, ln_w, w_in, w_gates_pad)


def _hgrn_body(q_ref, f_ref, i_ref, g_ref, lb_ref, nw_ref, o_ref, st_scr, *, rows):
    C, SUB = HG_CHUNK, HG_SUB
    nsub = C // SUB

    @pl.when(pl.program_id(2) == 0)
    def _():
        st_scr[...] = jnp.zeros_like(st_scr)

    lb = lb_ref[0]
    log_lb = jnp.log(lb)
    log_1mlb = jnp.log1p(-lb)
    one_m_lb = 1.0 - lb
    nw = nw_ref[0]

    r_i = lax.broadcasted_iota(jnp.int32, (C, C), 0)
    c_i = lax.broadcasted_iota(jnp.int32, (C, C), 1)
    tri = (c_i <= r_i).astype(F32)
    sub_shift = SUB.bit_length() - 1
    diag_mask = (c_i <= r_i) & (jnp.right_shift(r_i, sub_shift) == jnp.right_shift(c_i, sub_shift))
    row_id = lax.broadcasted_iota(jnp.int32, (C, HG_DK), 0)

    for j in range(rows // C):
        sl = pl.ds(j * C, C)
        q = q_ref[0, sl, :]
        z = f_ref[0, sl, :]
        v = i_ref[0, sl, :]
        g = g_ref[0, sl, :]

        e = jnp.exp(-jnp.abs(z))
        log_sig = jnp.minimum(z, 0.0) - jnp.log1p(e)
        sig_neg = jnp.where(z >= 0, e, 1.0) / (1.0 + e)
        cc = log_1mlb + log_sig
        log_f = jnp.maximum(log_lb, cc) + jnp.log1p(jnp.exp(-jnp.abs(log_lb - cc)))
        kk = one_m_lb * sig_neg
        b = jnp.dot(tri, log_f, precision=HIGHEST, preferred_element_type=F32)
        b_last = b[C - 1:C, :]

        st = st_scr[...]
        vb = v.astype(BF16)
        qe = (q * jnp.exp(b)).astype(BF16)
        kdec = (kk * jnp.exp(b_last - b)).astype(BF16)
        inter = _dot_nt(qe, st.astype(BF16))

        refs = [b[I * SUB:I * SUB + 1, :] for I in range(nsub)]
        refb = jnp.concatenate([jnp.broadcast_to(r, (SUB, HG_DK)) for r in refs], axis=0)
        qd = (q * jnp.exp(b - refb)).astype(BF16)
        kd = (kk * jnp.exp(refb - b)).astype(BF16)
        att = jnp.where(diag_mask, _dot_nt(qd, kd), 0.0)
        q_parts, k_parts = [], []
        for J in range(nsub - 1):
            r = refs[J + 1]
            qj = q * jnp.exp(jnp.minimum(b - r, 0.0))
            kj = kk * jnp.exp(jnp.minimum(r - b, 0.0))
            q_parts.append(jnp.where(row_id >= (J + 1) * SUB, qj, 0.0).astype(BF16))
            k_parts.append(jnp.where((row_id >= J * SUB) & (row_id < (J + 1) * SUB), kj, 0.0).astype(BF16))
        att = att + _dot_nt(jnp.concatenate(q_parts, axis=1), jnp.concatenate(k_parts, axis=1))
        intra = jnp.dot(att.astype(BF16), vb, preferred_element_type=F32)

        o = inter + intra
        y = o * lax.rsqrt(jnp.mean(o * o, axis=-1, keepdims=True) + EPS) * nw
        o_ref[sl, :] = (y * (g * _sigmoid(g))).astype(o_ref.dtype)

        st_scr[...] = st * jnp.exp(b_last) + _dot_tn(vb, kdec)


def _hgrn(proj3, lb, norm_w, batch, seq):
    T = batch * seq
    rows = min(MIX_ROWS, seq)
    nblk = seq // rows
    H = HG_HEADS

    def slab(off):
        return pl.BlockSpec((1, rows, LANES), lambda b, h, c, off=off: (off + h, b * nblk + c, 0))

    vec = pl.BlockSpec((1, 1, LANES), lambda b, h, c: (h, 0, 0))
    return pl.pallas_call(
        functools.partial(_hgrn_body, rows=rows),
        grid=(batch, H, nblk),
        in_specs=[slab(0), slab(H), slab(2 * H), slab(3 * H), vec, vec],
        out_specs=pl.BlockSpec((rows, LANES), lambda b, h, c: (b * nblk + c, h)),
        out_shape=jax.ShapeDtypeStruct((T, H * LANES), BF16),
        scratch_shapes=[pltpu.VMEM((LANES, HG_DK), F32)],
        compiler_params=_cparams(("parallel", "parallel", "arbitrary")),
        name="hgrn2",
    )(proj3, proj3, proj3, proj3, lb.reshape(H, 1, HG_DK), norm_w.reshape(H, 1, LANES))


def _mlstm_body(q_ref, k_ref, v_ref, og_ref, gt_ref, gb_ref, cwq_ref, cwk_ref, cbq_ref, cbk_ref, nw_ref,
                out_ref, c_scr, n_scr, m_scr, qx_scr, kx_scr, qc_scr, kc_scr, *, rows):
    C = ML_CHUNK
    PAD = SUBLANES
    h = pl.program_id(1)

    @pl.when(pl.program_id(2) == 0)
    def _():
        c_scr[...] = jnp.zeros_like(c_scr)
        n_scr[...] = jnp.zeros_like(n_scr)
        m_scr[...] = jnp.zeros_like(m_scr)
        qx_scr[0:PAD, :] = jnp.zeros((PAD, LANES), F32)
        kx_scr[0:PAD, :] = jnp.zeros((PAD, LANES), F32)

    qx_scr[PAD:PAD + rows, :] = q_ref[0]
    kx_scr[PAD:PAD + rows, :] = k_ref[0]
    accq = jnp.zeros((rows, LANES), F32) + cbq_ref[...]
    acck = jnp.zeros((rows, LANES), F32) + cbk_ref[...]
    for j in range(CONV_W):
        off = PAD - (CONV_W - 1) + j
        accq = accq + cwq_ref[j:j + 1, :] * qx_scr[pl.ds(off, rows), :]
        acck = acck + cwk_ref[j:j + 1, :] * kx_scr[pl.ds(off, rows), :]
    qc_scr[...] = accq * _sigmoid(accq) * (ML_DQK ** -0.5)
    kc_scr[...] = acck * _sigmoid(acck)
    qx_scr[0:PAD, :] = qx_scr[rows:rows + PAD, :]
    kx_scr[0:PAD, :] = kx_scr[rows:rows + PAD, :]

    lane = lax.broadcasted_iota(jnp.int32, (C, C), 1)
    sub = lax.broadcasted_iota(jnp.int32, (C, C), 0)
    causal = lane <= sub
    lower = causal.astype(F32)
    upper = (sub <= lane).astype(F32)
    nw = nw_ref[...]
    gbias = gb_ref[...]

    for j in range(rows // C):
        sl = pl.ds(j * C, C)
        gc = gt_ref[sl, :] + gbias
        gct = gc.T
        li_col = jnp.sum(jnp.where(lane == h, gc, 0.0), axis=1, keepdims=True)
        fg_col = jnp.sum(jnp.where(lane == h + ML_HEADS, gc, 0.0), axis=1, keepdims=True)
        li_row = jnp.sum(jnp.where(sub == h, gct, 0.0), axis=0, keepdims=True)
        fg_row = jnp.sum(jnp.where(sub == h + ML_HEADS, gct, 0.0), axis=0, keepdims=True)
        lf_col = _log_sigmoid(fg_col)
        lf_row = _log_sigmoid(fg_row)
        g_t = jnp.dot(lower, jnp.broadcast_to(lf_col, (C, C)), precision=HIGHEST,
                      preferred_element_type=F32)
        g_s = jnp.dot(jnp.broadcast_to(lf_row, (C, C)), upper, precision=HIGHEST,
                      preferred_element_type=F32)
        g_col = g_t[:, 0:1]
        g_last = g_t[C - 1:C, 0:1]
        m_prev = m_scr[:, 0:1]

        a_inter = g_col + m_prev
        dmat = jnp.where(causal, g_t - g_s + li_row, -jnp.inf)
        m_t = jnp.maximum(a_inter, jnp.max(dmat, axis=1, keepdims=True))
        w_inter = jnp.exp(a_inter - m_t)
        p = jnp.exp(dmat - m_t)

        qf = qc_scr[sl, :]
        kf = kc_scr[sl, :]
        qb = qf.astype(BF16)
        vb = jnp.concatenate([v_ref[0, sl, :], v_ref[1, sl, :]], axis=1).astype(BF16)
        cm = c_scr[...]
        nv = n_scr[...]
        sqk = _dot_nt(qb, kf.astype(BF16)) * p
        num = (w_inter * jnp.dot(qb, cm.astype(BF16), preferred_element_type=F32)
               + jnp.dot(sqk.astype(BF16), vb, preferred_element_type=F32))
        den = (w_inter * jnp.sum(qf * nv, axis=1, keepdims=True)
               + jnp.sum(sqk, axis=1, keepdims=True))
        hh = num * (1.0 / jnp.maximum(jnp.abs(den), jnp.exp(-m_t)))

        log_ws = g_last - g_col + li_col
        m_new = jnp.maximum(g_last + m_prev, jnp.max(log_ws, axis=0, keepdims=True))
        decay = jnp.exp(g_last + m_prev - m_new)
        kw = kf * jnp.exp(log_ws - m_new)
        c_scr[...] = decay * cm + _dot_tn(kw.astype(BF16), vb)
        n_scr[...] = decay * nv + jnp.sum(kw, axis=0, keepdims=True)
        m_scr[...] = jnp.broadcast_to(m_new, m_scr.shape)

        y = hh * lax.rsqrt(jnp.mean(hh * hh, axis=-1, keepdims=True) + EPS) * nw
        og = jnp.concatenate([og_ref[0, sl, :], og_ref[1, sl, :]], axis=1)
        out_ref[sl, :] = (y * _sigmoid(og)).astype(out_ref.dtype)


def _mlstm(proj3, gates, gate_bias_pad, conv_w, conv_b, norm_w, batch, seq, q_off):
    T = batch * seq
    rows = min(MIX_ROWS, seq)
    nblk = seq // rows
    H = ML_HEADS
    k_off = q_off + H
    v_off = k_off + H
    o_off = v_off + 2 * H

    def slab(off):
        return pl.BlockSpec((1, rows, LANES), lambda b, h, c, off=off: (off + h, b * nblk + c, 0))

    def slab2(off):
        return pl.BlockSpec((2, rows, LANES), lambda b, h, c, off=off: (off // 2 + h, b * nblk + c, 0))

    qk_w = H * ML_DQK
    return pl.pallas_call(
        functools.partial(_mlstm_body, rows=rows),
        grid=(batch, H, nblk),
        in_specs=[
            slab(q_off), slab(k_off), slab2(v_off), slab2(o_off),
            pl.BlockSpec((rows, LANES), lambda b, h, c: (b * nblk + c, 0)),
            pl.BlockSpec((1, LANES), lambda b, h, c: (0, 0)),
            pl.BlockSpec((CONV_W, LANES), lambda b, h, c: (0, h)),
            pl.BlockSpec((CONV_W, LANES), lambda b, h, c: (0, H + h)),
            pl.BlockSpec((1, LANES), lambda b, h, c: (0, h)),
            pl.BlockSpec((1, LANES), lambda b, h, c: (0, H + h)),
            pl.BlockSpec((1, ML_DV), lambda b, h, c: (0, h)),
        ],
        out_specs=pl.BlockSpec((rows, ML_DV), lambda b, h, c: (b * nblk + c, h)),
        out_shape=jax.ShapeDtypeStruct((T, H * ML_DV), BF16),
        scratch_shapes=[
            pltpu.VMEM((ML_DQK, ML_DV), F32),
            pltpu.VMEM((1, ML_DQK), F32),
            pltpu.VMEM((1, LANES), F32),
            pltpu.VMEM((rows + 2 * SUBLANES, LANES), F32),
            pltpu.VMEM((rows + 2 * SUBLANES, LANES), F32),
            pltpu.VMEM((rows, LANES), F32),
            pltpu.VMEM((rows, LANES), F32),
        ],
        compiler_params=_cparams(("parallel", "parallel", "arbitrary")),
        name="mlstm",
    )(proj3, proj3, proj3, proj3, gates, gate_bias_pad, conv_w, conv_w,
      conv_b.reshape(1, 2 * qk_w), conv_b.reshape(1, 2 * qk_w), norm_w.reshape(1, H * ML_DV))


def _outproj_body(a_ref, b_ref, x_ref, wo_ref, ln_ref, wrh_ref, wrl_ref, br_ref,
                  x2_ref, h2_ref, idx_ref, gate_ref, rank_ref, cnt_ref, cnt_scr, *, sub_rows):
    tm = x_ref.shape[0]
    ka = a_ref.shape[1]
    s = _packed_rows(x_ref.shape[1])

    @pl.when(pl.program_id(0) == 0)
    def _():
        cnt_scr[...] = jnp.zeros_like(cnt_scr)

    lane = lax.broadcasted_iota(jnp.int32, (sub_rows, LANES), 1).astype(F32)
    onehots = [[] for _ in range(TOP_K)]
    for r0 in range(0, tm, sub_rows):
        rows = pl.ds(r0, sub_rows)
        res = (jnp.dot(a_ref[rows, :], wo_ref[0:ka, :], preferred_element_type=F32)
               + jnp.dot(b_ref[rows, :], wo_ref[ka:, :], preferred_element_type=F32))
        x2 = x_ref[rows, :] + res
        x2_ref[rows, :] = x2
        h2 = x2 * lax.rsqrt(jnp.mean(x2 * x2, axis=-1, keepdims=True) + EPS) * ln_ref[...]
        _pack_store(h2_ref.at[pl.ds(r0 * s, sub_rows * s), :], h2)

        h_hi = h2.astype(BF16)
        h_lo = (h2 - h_hi.astype(F32)).astype(BF16)
        logits = (jnp.dot(h_hi, wrh_ref[...], preferred_element_type=F32)
                  + jnp.dot(h_lo, wrh_ref[...], preferred_element_type=F32)
                  + jnp.dot(h_hi, wrl_ref[...], preferred_element_type=F32)) + br_ref[...]
        vals, idxs = [], []
        cur = logits
        for _ in range(TOP_K):
            m = jnp.max(cur, axis=1, keepdims=True)
            ix = jnp.min(jnp.where(cur == m, lane, float(LANES)), axis=1, keepdims=True)
            vals.append(m)
            idxs.append(ix)
            cur = jnp.where(lane == ix, -jnp.inf, cur)
        es = [jnp.exp(v - vals[0]) for v in vals]
        inv = 1.0 / (es[0] + es[1] + es[2] + es[3])
        gate = jnp.zeros(logits.shape, F32)
        idx = jnp.zeros(logits.shape, F32)
        for k in range(TOP_K):
            gate = jnp.where(lane == float(k), es[k] * inv, gate)
            idx = jnp.where(lane == float(k), idxs[k], idx)
            onehots[k].append((lane == idxs[k]).astype(F32))
        gate_ref[rows, :] = gate
        idx_ref[rows, :] = idx.astype(jnp.int32)

    oh_k = [jnp.concatenate(o, axis=0) for o in onehots]
    oh = oh_k[0] + oh_k[1] + oh_k[2] + oh_k[3]
    r_i = lax.broadcasted_iota(jnp.int32, (tm, tm), 0)
    c_i = lax.broadcasted_iota(jnp.int32, (tm, tm), 1)
    before = jnp.dot((c_i < r_i).astype(BF16), oh.astype(BF16), preferred_element_type=F32) + cnt_scr[...]
    lane_t = lax.broadcasted_iota(jnp.int32, (tm, LANES), 1)
    rank = jnp.zeros((tm, LANES), F32)
    for k in range(TOP_K):
        rank = jnp.where(lane_t == k, jnp.sum(oh_k[k] * before, axis=1, keepdims=True), rank)
    rank_ref[...] = rank.astype(jnp.int32)
    cnt = cnt_scr[...] + jnp.sum(oh, axis=0, keepdims=True)
    cnt_scr[...] = cnt
    cnt_ref[...] = cnt.astype(jnp.int32)


def _outproj(a_out, b_out, x2d, w_out_bf, ln_w, wr_hi, wr_lo, b_router_pad):
    T, D = x2d.shape
    tm = min(OUTPROJ_TM, T)
    ka, kb = a_out.shape[1], b_out.shape[1]
    s = _packed_rows(D)
    row = lambda w: pl.BlockSpec((tm, w), lambda i: (i, 0))
    full = lambda r, c: pl.BlockSpec((r, c), lambda i: (0, 0))
    return pl.pallas_call(
        functools.partial(_outproj_body, sub_rows=min(OUTPROJ_SUB, tm)),
        grid=(T // tm,),
        in_specs=[row(ka), row(kb), row(D), full(ka + kb, D), full(1, D), full(D, LANES), full(D, LANES),
                  full(1, LANES)],
        out_specs=[row(D), pl.BlockSpec((tm * s, LANES), lambda i: (i, 0)), row(LANES), row(LANES), row(LANES),
                   full(1, LANES)],
        out_shape=[
            jax.ShapeDtypeStruct((T, D), F32),
            jax.ShapeDtypeStruct((T * s, LANES), jnp.uint32),
            jax.ShapeDtypeStruct((T, LANES), jnp.int32),
            jax.ShapeDtypeStruct((T, LANES), F32),
            jax.ShapeDtypeStruct((T, LANES), jnp.int32),
            jax.ShapeDtypeStruct((1, LANES), jnp.int32),
        ],
        scratch_shapes=[pltpu.VMEM((1, LANES), F32)],
        compiler_params=_cparams(("arbitrary",)),
        name="outproj_router",
    )(a_out, b_out, x2d, w_out_bf, ln_w, wr_hi, wr_lo, b_router_pad)


def _start_row_gather(idx_at, src_ref, buf, sem, n, s):
    def start(i, carry):
        src_row = pl.multiple_of(idx_at(i) * s, s)
        dst_row = pl.multiple_of(i * s, s)
        pltpu.make_async_copy(src_ref.at[pl.ds(src_row, s), :], buf.at[pl.ds(dst_row, s), :], sem).start()
        return carry

    lax.fori_loop(0, n, start, 0, unroll=DMA_UNROLL)


def _wait_row_gather(buf, sem):
    pltpu.make_async_copy(buf, buf, sem).wait()


def _dispatch_body(zero_ref, dest_ref, h_ref, xs_ref, zbuf, sem, zsem, *, ntok, s, tm, n_blocks):
    blk = tm * s

    def zero_copy(b):
        return pltpu.make_async_copy(zbuf, xs_ref.at[pl.ds(pl.multiple_of(b * blk, blk), blk), :], zsem)

    @pl.when(pl.program_id(0) == 0)
    def _():
        zbuf[...] = jnp.zeros_like(zbuf)

        def zstart(b, carry):
            @pl.when(zero_ref[b] == 1)
            def _():
                zero_copy(b).start()
            return carry

        def zwait(b, carry):
            @pl.when(zero_ref[b] == 1)
            def _():
                zero_copy(b).wait()
            return carry

        lax.fori_loop(0, n_blocks, zstart, 0)
        lax.fori_loop(0, n_blocks, zwait, 0)

    def start(t, carry):
        src = h_ref.at[pl.ds(pl.multiple_of(t * s, s), s), :]
        for k in range(TOP_K):
            dst_row = pl.multiple_of(dest_ref[0, 0, t * TOP_K + k] * s, s)
            pltpu.make_async_copy(src, xs_ref.at[pl.ds(dst_row, s), :], sem).start()
        return carry

    lax.fori_loop(0, ntok, start, 0, unroll=DMA_UNROLL // TOP_K)
    for _ in range(TOP_K):
        pltpu.make_async_copy(h_ref, h_ref, sem).wait()


def _dispatch(dest, zero_blk, h_packed, d, n_slots, tm):
    s = _packed_rows(d)
    T = h_packed.shape[0] // s
    ntok = min(DISPATCH_TOKENS, T)
    n = ntok * TOP_K
    grid_spec = pltpu.PrefetchScalarGridSpec(
        num_scalar_prefetch=1,
        grid=(T // ntok,),
        in_specs=[
            pl.BlockSpec((1, 1, n), lambda i, z: (i, 0, 0), memory_space=pltpu.SMEM),
            pl.BlockSpec((ntok * s, LANES), lambda i, z: (i, 0)),
        ],
        out_specs=pl.BlockSpec(memory_space=pl.ANY),
        scratch_shapes=[pltpu.VMEM((tm * s, LANES), jnp.uint32), pltpu.SemaphoreType.DMA,
                        pltpu.SemaphoreType.DMA],
    )
    return pl.pallas_call(
        functools.partial(_dispatch_body, ntok=ntok, s=s, tm=tm, n_blocks=zero_blk.shape[0]),
        grid_spec=grid_spec,
        out_shape=jax.ShapeDtypeStruct((n_slots * s, LANES), jnp.uint32),
        compiler_params=_cparams(("arbitrary",)),
        name="moe_dispatch",
    )(zero_blk, dest.reshape(T // ntok, 1, n), h_packed)


def _cast_tile(dst_ref, src_ref, step=256):
    for r in range(0, dst_ref.shape[0], step):
        dst_ref[r:r + step, :] = src_ref[0, r:r + step, :].astype(BF16)


def _moe_up_body(e_ref, wt_ref, r_ref, ot_ref, first_ref, valid_ref,
                 x_ref, wg_ref, wu_ref, bg_ref, bu_ref, o_ref, wg_scr, wu_scr, x_scr):
    w = pl.program_id(0)

    @pl.when(first_ref[w] == 1)
    def _():
        _cast_tile(wg_scr, wg_ref)
        _cast_tile(wu_scr, wu_ref)

    @pl.when(valid_ref[w] == 1)
    def _():
        tm, d = x_scr.shape
        s = _packed_rows(d)
        for c in range(s):
            hi, lo = _unpack_load(x_ref, 0, tm, s, c)
            x_scr[:, c * LANES:(c + 1) * LANES] = hi.astype(BF16)
            x_scr[:, d // 2 + c * LANES:d // 2 + (c + 1) * LANES] = lo.astype(BF16)
        x = x_scr[...]
        gt = jnp.dot(x, wg_scr[...], preferred_element_type=F32) + bg_ref[0]
        up = jnp.dot(x, wu_scr[...], preferred_element_type=F32) + bu_ref[0]
        gt = jnp.minimum(gt, SWIGLU_LIMIT)
        up = jnp.clip(up, -SWIGLU_LIMIT, SWIGLU_LIMIT)
        o_ref[...] = ((up + 1.0) * (gt * _sigmoid(SWIGLU_ALPHA * gt))).astype(o_ref.dtype)

    @pl.when(valid_ref[w] == 0)
    def _():
        o_ref[...] = jnp.zeros_like(o_ref)


def _moe_down_body(e_ref, wt_ref, r_ref, ot_ref, first_ref, valid_ref, a_ref, wd_ref, bd_ref, o_ref, wd_scr):
    w = pl.program_id(0)

    @pl.when(first_ref[w] == 1)
    def _():
        _cast_tile(wd_scr, wd_ref)

    @pl.when(valid_ref[w] == 1)
    def _():
        y = jnp.dot(a_ref[...], wd_scr[...], preferred_element_type=F32) + bd_ref[0]
        _pack_store(o_ref, y)

    @pl.when(valid_ref[w] == 0)
    def _():
        o_ref[...] = jnp.zeros_like(o_ref)


def _moe_schedule(blocks_e, n_tiles, n_blocks):
    n_items = n_tiles * n_blocks
    bstart = jnp.cumsum(blocks_e) - blocks_e
    item_end = n_tiles * jnp.cumsum(blocks_e)
    total = item_end[-1]
    w = jnp.arange(n_items, dtype=jnp.int32)
    valid = w < total
    wc = jnp.minimum(w, jnp.maximum(total - 1, 0))
    e = jnp.minimum(jnp.sum((item_end[None, :] <= wc[:, None]).astype(jnp.int32), axis=1), N_EXPERTS - 1)
    local = wc - n_tiles * bstart[e]
    nb = jnp.maximum(blocks_e[e], 1)
    wtile = local // nb
    jblk = local % nb
    spare = w - total
    rblk = jnp.where(valid, bstart[e] + jblk, total // n_tiles + spare // n_tiles)
    otile = jnp.where(valid, wtile, spare % n_tiles)
    first = (jblk == 0) & valid
    i32 = lambda a: a.astype(jnp.int32)
    return (e, i32(wtile), i32(rblk), i32(otile), i32(first), i32(valid))


def _moe_up(sched, xs_packed, w_gate, w_up, b_gate, b_up):
    D, d_ff = w_gate.shape[1], w_gate.shape[2]
    s = _packed_rows(D)
    n_slots = xs_packed.shape[0] // s
    tm, tf = MOE_TM, min(MOE_TF, d_ff)
    n_items = sched[0].shape[0]
    wspec = pl.BlockSpec((1, D, tf), lambda w, e, wt, r, ot, fi, va: (e[w], 0, wt[w]))
    bspec = pl.BlockSpec((1, 1, tf), lambda w, e, wt, r, ot, fi, va: (e[w], 0, wt[w]))
    grid_spec = pltpu.PrefetchScalarGridSpec(
        num_scalar_prefetch=6,
        grid=(n_items,),
        in_specs=[pl.BlockSpec((tm * s, LANES), lambda w, e, wt, r, ot, fi, va: (r[w], 0)),
                  wspec, wspec, bspec, bspec],
        out_specs=pl.BlockSpec((tm, tf), lambda w, e, wt, r, ot, fi, va: (r[w], ot[w])),
        scratch_shapes=[pltpu.VMEM((D, tf), BF16), pltpu.VMEM((D, tf), BF16), pltpu.VMEM((tm, D), BF16)],
    )
    return pl.pallas_call(
        _moe_up_body,
        grid_spec=grid_spec,
        out_shape=jax.ShapeDtypeStruct((n_slots, d_ff), BF16),
        compiler_params=_cparams(("arbitrary",)),
        name="moe_up",
    )(*sched, xs_packed, w_gate, w_up, b_gate.reshape(N_EXPERTS, 1, d_ff), b_up.reshape(N_EXPERTS, 1, d_ff))


def _moe_down(sched, act, w_down, b_down):
    n_slots, d_ff = act.shape
    D = w_down.shape[2]
    tm = MOE_DOWN_TM
    s = _packed_rows(D)
    n_items = sched[0].shape[0]
    grid_spec = pltpu.PrefetchScalarGridSpec(
        num_scalar_prefetch=6,
        grid=(n_items,),
        in_specs=[
            pl.BlockSpec((tm, d_ff), lambda w, e, wt, r, ot, fi, va: (r[w], 0)),
            pl.BlockSpec((1, d_ff, D), lambda w, e, wt, r, ot, fi, va: (e[w], 0, 0)),
            pl.BlockSpec((1, 1, D), lambda w, e, wt, r, ot, fi, va: (e[w], 0, 0)),
        ],
        out_specs=pl.BlockSpec((tm * s, LANES), lambda w, e, wt, r, ot, fi, va: (r[w], 0)),
        scratch_shapes=[pltpu.VMEM((d_ff, D), BF16)],
    )
    return pl.pallas_call(
        _moe_down_body,
        grid_spec=grid_spec,
        out_shape=jax.ShapeDtypeStruct((n_slots * s, LANES), jnp.uint32),
        compiler_params=_cparams(("arbitrary",)),
        name="moe_down",
    )(*sched, act, w_down, b_down.reshape(N_EXPERTS, 1, D))


def _final_body(dest_ref, next_ref, x2_ref, gate_ref, w_ref, ys_ref, o_ref, buf0, buf1, sem0, sem1, *, tm, s):
    n = tm * TOP_K
    i = pl.program_id(0)

    def gather(idx_ref, tile, buf, sem):
        _start_row_gather(lambda r: idx_ref[0, 0, tile * n + r], ys_ref, buf, sem, n, s)

    def combine(tile, buf):
        rows = pl.ds(tile * tm, tm)
        gate = gate_ref[rows, :]
        half = o_ref.shape[1] // 2
        o_ref[rows, :] = x2_ref[rows, :]
        for k in range(TOP_K):
            g = gate[:, k:k + 1]
            for c in range(s):
                hi, lo = _unpack_load(buf, k * tm, tm, s, c)
                o_ref[rows, c * LANES:(c + 1) * LANES] += g * hi
                o_ref[rows, half + c * LANES:half + (c + 1) * LANES] += g * lo
        acc = o_ref[rows, :]
        o_ref[rows, :] = acc * lax.rsqrt(jnp.mean(acc * acc, axis=-1, keepdims=True) + EPS) * w_ref[...]

    @pl.when(i == 0)
    def _():
        gather(dest_ref, 0, buf0, sem0)

    gather(dest_ref, 1, buf1, sem1)
    _wait_row_gather(buf0, sem0)
    combine(0, buf0)
    gather(next_ref, 0, buf0, sem0)
    _wait_row_gather(buf1, sem1)
    combine(1, buf1)

    @pl.when(i == pl.num_programs(0) - 1)
    def _():
        _wait_row_gather(buf0, sem0)


def _final(x2, ys_packed, dest, gates_pad, w):
    T, D = x2.shape
    tm = min(FINAL_TM, T // 2)
    s = _packed_rows(D)
    n = tm * TOP_K
    steps = T // (2 * tm)
    dest_km = dest.reshape(steps, 2, tm, TOP_K).transpose(0, 1, 3, 2).reshape(steps, 1, 2 * n)
    buf = pltpu.VMEM((n * s, LANES), jnp.uint32)
    return pl.pallas_call(
        functools.partial(_final_body, tm=tm, s=s),
        grid=(steps,),
        in_specs=[
            pl.BlockSpec((1, 1, 2 * n), lambda i: (i, 0, 0), memory_space=pltpu.SMEM),
            pl.BlockSpec((1, 1, 2 * n), lambda i: (jnp.minimum(i + 1, steps - 1), 0, 0), memory_space=pltpu.SMEM),
            pl.BlockSpec((2 * tm, D), lambda i: (i, 0)),
            pl.BlockSpec((2 * tm, LANES), lambda i: (i, 0)),
            pl.BlockSpec((1, D), lambda i: (0, 0)),
            pl.BlockSpec(memory_space=pl.ANY),
        ],
        out_specs=pl.BlockSpec((2 * tm, D), lambda i: (i, 0)),
        out_shape=jax.ShapeDtypeStruct((T, D), F32),
        scratch_shapes=[buf, buf, pltpu.SemaphoreType.DMA, pltpu.SemaphoreType.DMA],
        compiler_params=_cparams(("arbitrary",)),
        name="final_norm",
    )(dest_km, dest_km, x2, gates_pad, w, ys_packed)


def _moe(h2_packed, T, D, top_idx, rank, counts, w_gate, b_gate, w_up, b_up, w_down, b_down):
    A = T * TOP_K
    tm = MOE_TM
    n_blocks = (A + N_EXPERTS * (tm - 1) + tm - 1) // tm
    n_slots = n_blocks * tm

    blocks_e = (counts + tm - 1) // tm
    bend = jnp.cumsum(blocks_e)
    bstart = bend - blocks_e
    experts = jnp.arange(N_EXPERTS, dtype=jnp.int32)
    first_slot = jnp.sum(jnp.where(top_idx[:, :, None] == experts, bstart * tm, 0), axis=-1)
    dest = (first_slot + rank).astype(jnp.int32).reshape(A)
    blk = jnp.arange(n_blocks, dtype=jnp.int32)
    blk_e = jnp.minimum(jnp.sum((bend[None, :] <= blk[:, None]).astype(jnp.int32), axis=1), N_EXPERTS - 1)
    zero_blk = ((blk >= bend[-1]) | (blk == bend[blk_e] - 1)).astype(jnp.int32)

    xs = _dispatch(dest, zero_blk, h2_packed, D, n_slots, tm)
    d_ff = w_gate.shape[2]
    act = _moe_up(_moe_schedule(blocks_e, -(-d_ff // MOE_TF), n_blocks), xs, w_gate, w_up, b_gate, b_up)
    split = tm // MOE_DOWN_TM
    ys = _moe_down(_moe_schedule(blocks_e * split, 1, n_blocks * split), act, w_down, b_down)
    return ys, dest


def kernel(x, ln1_w, w_in, hg_lb_logits, hg_norm_w, ml_conv_w, ml_conv_b, ml_igate_b, ml_fgate_b, ml_norm_w,
           w_out, ln2_w, w_router, b_router, w_gate, b_gate, w_up, b_up, w_down, b_down, final_norm_w):
    B, S, D = x.shape
    T = B * S
    depth = w_in.shape[0]
    hg_w = HG_HEADS * HG_DK
    n_main = 4 * hg_w + 2 * ML_HEADS * ML_DQK + 2 * ML_HEADS * ML_DV
    lb_all = jnp.cumsum(jax.nn.softmax(hg_lb_logits.astype(F32), axis=0), axis=0)

    xc = x.reshape(T, D)
    for l in range(depth):
        w_gates_pad = jnp.pad(w_in[l][:, n_main:], ((0, 0), (0, LANES - 2 * ML_HEADS))).astype(BF16)
        proj3, gates = _inproj(xc, ln1_w[l].reshape(1, D), w_in[l], w_gates_pad, n_main)
        a_out = _hgrn(proj3, lb_all[l], hg_norm_w[l], B, S)
        gate_bias = jnp.pad(jnp.concatenate([ml_igate_b[l], ml_fgate_b[l]]), (0, LANES - 2 * ML_HEADS))
        b_out = _mlstm(proj3, gates, gate_bias.reshape(1, LANES), ml_conv_w[l], ml_conv_b[l], ml_norm_w[l],
                       B, S, 4 * HG_HEADS)
        wr_pad = jnp.pad(w_router[l], ((0, 0), (0, LANES - N_EXPERTS)))
        wr_hi = wr_pad.astype(BF16)
        wr_lo = (wr_pad - wr_hi.astype(F32)).astype(BF16)
        br_pad = jnp.pad(b_router[l], (0, LANES - N_EXPERTS), constant_values=-1e30).reshape(1, LANES)
        x2, h2, idx_pad, gates_pad, rank_pad, cnt = _outproj(
            a_out, b_out, xc, w_out[l].astype(BF16), ln2_w[l].reshape(1, D), wr_hi, wr_lo, br_pad)
        ys, dest = _moe(h2, T, D, idx_pad[:, :TOP_K], rank_pad[:, :TOP_K], cnt[0, :N_EXPERTS],
                        w_gate[l], b_gate[l], w_up[l], b_up[l], w_down[l], b_down[l])
        if l + 1 < depth:
            raise NotImplementedError("only the final layer fuses the output norm")
        xc = _final(x2, ys, dest, gates_pad, final_norm_w.reshape(1, D))
    return xc.reshape(B, S, D)
```

```python
import functools

import jax
import jax.numpy as jnp
from jax import lax
from jax.experimental import pallas as pl
from jax.experimental.pallas import tpu as pltpu

F32 = jnp.float32
BF16 = jnp.bfloat16
HIGHEST = lax.Precision.HIGHEST

EPS = 1e-6
HG_HEADS = 8
HG_DK = 128
ML_HEADS = 4
ML_DQK = 128
ML_DV = 256
CONV_W = 4
N_EXPERTS = 32
TOP_K = 4
SWIGLU_ALPHA = 1.702
SWIGLU_LIMIT = 7.0

LANES = 128
SUBLANES = 8
VMEM_LIMIT_BYTES = 56 * 1024 * 1024

HG_CHUNK = 64
HG_SUB = 16
ML_CHUNK = 128
MIX_ROWS = 512
HG_HEADS_PER_STEP = 2

INPROJ_TM = 1024
INPROJ_TN = 512
OUTPROJ_TM = 256
OUTPROJ_SUB = 128
DISPATCH_TOKENS = 256
MOE_TM = 512
MOE_TF = 1024
MOE_DOWN_TM = 256
DMA_UNROLL = 32
FINAL_TM = 128


def _dot_nt(a, b):
    return lax.dot_general(a, b, (((1,), (1,)), ((), ())), preferred_element_type=F32)


def _dot_tn(a, b):
    return lax.dot_general(a, b, (((0,), (0,)), ((), ())), preferred_element_type=F32)


def _log_sigmoid(z):
    return jnp.minimum(z, 0.0) - jnp.log1p(jnp.exp(-jnp.abs(z)))


def _sigmoid(z):
    return 1.0 / (1.0 + jnp.exp(-z))


def _cparams(semantics):
    return pltpu.CompilerParams(dimension_semantics=semantics, vmem_limit_bytes=VMEM_LIMIT_BYTES)


_HI_MASK = 0xFFFF0000


def _packed_rows(d):
    return d // (2 * LANES)


def _pack_store(o_ref, v):
    n, d = v.shape
    s, half = _packed_rows(d), d // 2
    bits = pltpu.bitcast(v.astype(BF16).astype(F32), jnp.uint32)
    for c in range(s):
        hi = bits[:, c * LANES:(c + 1) * LANES]
        lo = bits[:, half + c * LANES:half + (c + 1) * LANES]
        o_ref[pl.ds(c, n, stride=s), :] = hi | jnp.right_shift(lo, jnp.uint32(16))


def _unpack_load(buf, first_row, n, s, c):
    w = buf[pl.ds(first_row * s + c, n, stride=s), :]
    hi = pltpu.bitcast(w & jnp.uint32(_HI_MASK), F32)
    lo = pltpu.bitcast(jnp.left_shift(w, jnp.uint32(16)), F32)
    return hi, lo


def _inproj_body(*refs, tn, with_gates):
    if with_gates:
        x_ref, lnw_ref, w_ref, wg_ref, o_ref, g_ref, h_scr = refs
    else:
        x_ref, lnw_ref, w_ref, o_ref, h_scr = refs

    @pl.when(pl.program_id(1) == 0)
    def _():
        x = x_ref[...]
        h = x * lax.rsqrt(jnp.mean(x * x, axis=-1, keepdims=True) + EPS) * lnw_ref[...]
        hb = h.astype(BF16)
        h_scr[...] = hb
        if with_gates:
            g_ref[...] = jnp.dot(hb, wg_ref[...], preferred_element_type=F32)

    res = jnp.dot(h_scr[...], w_ref[...], preferred_element_type=F32)
    for c in range(tn // LANES):
        o_ref[c] = res[:, c * LANES:(c + 1) * LANES].astype(o_ref.dtype)


def _inproj(x2d, ln_w, w_bf, n_tiles, col_tile, out_dtype, w_gates_pad=None):
    T, D = x2d.shape
    tm = min(INPROJ_TM, T)
    tn = INPROJ_TN
    with_gates = w_gates_pad is not None
    in_specs = [
        pl.BlockSpec((tm, D), lambda i, j: (i, 0)),
        pl.BlockSpec((1, D), lambda i, j: (0, 0)),
        pl.BlockSpec((D, tn), lambda i, j: (0, col_tile(j))),
    ]
    out_specs = [pl.BlockSpec((tn // LANES, tm, LANES), lambda i, j: (j, i, 0))]
    out_shape = [jax.ShapeDtypeStruct((n_tiles * tn // LANES, T, LANES), out_dtype)]
    args = [x2d, ln_w, w_bf]
    if with_gates:
        in_specs.append(pl.BlockSpec((D, LANES), lambda i, j: (0, 0)))
        out_specs.append(pl.BlockSpec((tm, LANES), lambda i, j: (i, 0)))
        out_shape.append(jax.ShapeDtypeStruct((T, LANES), F32))
        args.append(w_gates_pad)
    return pl.pallas_call(
        functools.partial(_inproj_body, tn=tn, with_gates=with_gates),
        grid=(T // tm, n_tiles),
        in_specs=in_specs,
        out_specs=out_specs,
        out_shape=out_shape,
        scratch_shapes=[pltpu.VMEM((tm, D), BF16)],
        compiler_params=_cparams(("parallel", "arbitrary")),
        name="inproj_gates" if with_gates else "inproj",
    )(*args)


def _hgrn_body(q_ref, f_ref, i_ref, g_ref, lb_ref, nw_ref, o_ref, st_scr, *, rows, heads):
    C, SUB = HG_CHUNK, HG_SUB
    nsub = C // SUB

    @pl.when(pl.program_id(2) == 0)
    def _():
        st_scr[...] = jnp.zeros_like(st_scr)

    lbs = [lb_ref[hh] for hh in range(heads)]
    log_lbs = [jnp.log(lb) for lb in lbs]
    log_1mlbs = [jnp.log1p(-lb) for lb in lbs]

    r_i = lax.broadcasted_iota(jnp.int32, (C, C), 0)
    c_i = lax.broadcasted_iota(jnp.int32, (C, C), 1)
    tri = (c_i <= r_i).astype(F32)
    sub_shift = SUB.bit_length() - 1
    diag_mask = (c_i <= r_i) & (jnp.right_shift(r_i, sub_shift) == jnp.right_shift(c_i, sub_shift))
    row_id = lax.broadcasted_iota(jnp.int32, (C, HG_DK), 0)

    for j, hh in [(j, hh) for j in range(rows // C) for hh in range(heads)]:
        sl = pl.ds(j * C, C)
        q = q_ref[hh, sl, :].astype(F32)
        z = f_ref[hh, sl, :].astype(F32)
        v = i_ref[hh, sl, :].astype(F32)
        g = g_ref[hh, sl, :].astype(F32)
        log_lb, log_1mlb, one_m_lb, nw = log_lbs[hh], log_1mlbs[hh], 1.0 - lbs[hh], nw_ref[hh]

        e = jnp.exp(-jnp.abs(z))
        log_sig = jnp.minimum(z, 0.0) - jnp.log1p(e)
        sig_neg = jnp.where(z >= 0, e, 1.0) / (1.0 + e)
        cc = log_1mlb + log_sig
        log_f = jnp.maximum(log_lb, cc) + jnp.log1p(jnp.exp(-jnp.abs(log_lb - cc)))
        kk = one_m_lb * sig_neg
        b = jnp.dot(tri, log_f, precision=HIGHEST, preferred_element_type=F32)
        b_last = b[C - 1:C, :]

        st = st_scr[hh]
        vb = v.astype(BF16)
        qe = (q * jnp.exp(b)).astype(BF16)
        kdec = (kk * jnp.exp(b_last - b)).astype(BF16)
        inter = _dot_nt(qe, st.astype(BF16))

        refs = [b[I * SUB:I * SUB + 1, :] for I in range(nsub)]
        refb = jnp.concatenate([jnp.broadcast_to(r, (SUB, HG_DK)) for r in refs], axis=0)
        qd = (q * jnp.exp(b - refb)).astype(BF16)
        kd = (kk * jnp.exp(refb - b)).astype(BF16)
        att = jnp.where(diag_mask, _dot_nt(qd, kd), 0.0)
        q_parts, k_parts = [], []
        for J in range(nsub - 1):
            r = refs[J + 1]
            qj = q * jnp.exp(jnp.minimum(b - r, 0.0))
            kj = kk * jnp.exp(jnp.minimum(r - b, 0.0))
            q_parts.append(jnp.where(row_id >= (J + 1) * SUB, qj, 0.0).astype(BF16))
            k_parts.append(jnp.where((row_id >= J * SUB) & (row_id < (J + 1) * SUB), kj, 0.0).astype(BF16))
        att = att + _dot_nt(jnp.concatenate(q_parts, axis=1), jnp.concatenate(k_parts, axis=1))
        intra = jnp.dot(att.astype(BF16), vb, preferred_element_type=F32)

        o = inter + intra
        y = o * lax.rsqrt(jnp.mean(o * o, axis=-1, keepdims=True) + EPS) * nw
        o_ref[sl, hh * LANES:(hh + 1) * LANES] = (y * (g * _sigmoid(g))).astype(o_ref.dtype)

        st_scr[hh] = st * jnp.exp(b_last) + _dot_tn(vb, kdec)


def _hgrn(proj_b, proj_f, lb, norm_w, batch, seq):
    T = batch * seq
    rows = min(MIX_ROWS, seq)
    nblk = seq // rows
    H = HG_HEADS
    hp = HG_HEADS_PER_STEP

    def slab(off):
        return pl.BlockSpec((hp, rows, LANES), lambda b, h, c, off=off: (off // hp + h, b * nblk + c, 0))

    vec = pl.BlockSpec((hp, 1, LANES), lambda b, h, c: (h, 0, 0))
    return pl.pallas_call(
        functools.partial(_hgrn_body, rows=rows, heads=hp),
        grid=(batch, H // hp, nblk),
        in_specs=[slab(0), slab(0), slab(H), slab(2 * H), vec, vec],
        out_specs=pl.BlockSpec((rows, hp * LANES), lambda b, h, c: (b * nblk + c, h)),
        out_shape=jax.ShapeDtypeStruct((T, H * LANES), BF16),
        scratch_shapes=[pltpu.VMEM((hp, LANES, HG_DK), F32)],
        compiler_params=_cparams(("parallel", "parallel", "arbitrary")),
        name="hgrn2",
    )(proj_b, proj_f, proj_b, proj_b, lb.reshape(H, 1, HG_DK), norm_w.reshape(H, 1, LANES))


def _mlstm_body(q_ref, k_ref, v_ref, og_ref, gt_ref, gb_ref, cwq_ref, cwk_ref, cbq_ref, cbk_ref, nw_ref,
                out_ref, c_scr, n_scr, m_scr, qx_scr, kx_scr, qc_scr, kc_scr, *, rows):
    C = ML_CHUNK
    PAD = SUBLANES
    h = pl.program_id(1)

    @pl.when(pl.program_id(2) == 0)
    def _():
        c_scr[...] = jnp.zeros_like(c_scr)
        n_scr[...] = jnp.zeros_like(n_scr)
        m_scr[...] = jnp.zeros_like(m_scr)
        qx_scr[0:PAD, :] = jnp.zeros((PAD, LANES), F32)
        kx_scr[0:PAD, :] = jnp.zeros((PAD, LANES), F32)

    qx_scr[PAD:PAD + rows, :] = q_ref[0].astype(F32)
    kx_scr[PAD:PAD + rows, :] = k_ref[0].astype(F32)
    accq = jnp.zeros((rows, LANES), F32) + cbq_ref[...]
    acck = jnp.zeros((rows, LANES), F32) + cbk_ref[...]
    for j in range(CONV_W):
        off = PAD - (CONV_W - 1) + j
        accq = accq + cwq_ref[j:j + 1, :] * qx_scr[pl.ds(off, rows), :]
        acck = acck + cwk_ref[j:j + 1, :] * kx_scr[pl.ds(off, rows), :]
    qc_scr[...] = accq * _sigmoid(accq) * (ML_DQK ** -0.5)
    kc_scr[...] = acck * _sigmoid(acck)
    qx_scr[0:PAD, :] = qx_scr[rows:rows + PAD, :]
    kx_scr[0:PAD, :] = kx_scr[rows:rows + PAD, :]

    lane = lax.broadcasted_iota(jnp.int32, (C, C), 1)
    sub = lax.broadcasted_iota(jnp.int32, (C, C), 0)
    causal = lane <= sub
    lower = causal.astype(F32)
    upper = (sub <= lane).astype(F32)
    nw = nw_ref[...]
    gbias = gb_ref[...]

    for j in range(rows // C):
        sl = pl.ds(j * C, C)
        gc = gt_ref[sl, :] + gbias
        gct = gc.T
        li_col = jnp.sum(jnp.where(lane == h, gc, 0.0), axis=1, keepdims=True)
        fg_col = jnp.sum(jnp.where(lane == h + ML_HEADS, gc, 0.0), axis=1, keepdims=True)
        li_row = jnp.sum(jnp.where(sub == h, gct, 0.0), axis=0, keepdims=True)
        fg_row = jnp.sum(jnp.where(sub == h + ML_HEADS, gct, 0.0), axis=0, keepdims=True)
        lf_col = _log_sigmoid(fg_col)
        lf_row = _log_sigmoid(fg_row)
        g_t = jnp.dot(lower, jnp.broadcast_to(lf_col, (C, C)), precision=HIGHEST,
                      preferred_element_type=F32)
        g_s = jnp.dot(jnp.broadcast_to(lf_row, (C, C)), upper, precision=HIGHEST,
                      preferred_element_type=F32)
        g_col = g_t[:, 0:1]
        g_last = g_t[C - 1:C, 0:1]
        m_prev = m_scr[:, 0:1]

        a_inter = g_col + m_prev
        dmat = jnp.where(causal, g_t - g_s + li_row, -jnp.inf)
        m_t = jnp.maximum(a_inter, jnp.max(dmat, axis=1, keepdims=True))
        w_inter = jnp.exp(a_inter - m_t)
        p = jnp.exp(dmat - m_t)

        qf = qc_scr[sl, :]
        kf = kc_scr[sl, :]
        qb = qf.astype(BF16)
        vb = jnp.concatenate([v_ref[0, sl, :], v_ref[1, sl, :]], axis=1).astype(BF16)
        cm = c_scr[...]
        nv = n_scr[...]
        sqk = _dot_nt(qb, kf.astype(BF16)) * p
        num = (w_inter * jnp.dot(qb, cm.astype(BF16), preferred_element_type=F32)
               + jnp.dot(sqk.astype(BF16), vb, preferred_element_type=F32))
        den = (w_inter * jnp.sum(qf * nv, axis=1, keepdims=True)
               + jnp.sum(sqk, axis=1, keepdims=True))
        hh = num * (1.0 / jnp.maximum(jnp.abs(den), jnp.exp(-m_t)))

        log_ws = g_last - g_col + li_col
        m_new = jnp.maximum(g_last + m_prev, jnp.max(log_ws, axis=0, keepdims=True))
        decay = jnp.exp(g_last + m_prev - m_new)
        kw = kf * jnp.exp(log_ws - m_new)
        c_scr[...] = decay * cm + _dot_tn(kw.astype(BF16), vb)
        n_scr[...] = decay * nv + jnp.sum(kw, axis=0, keepdims=True)
        m_scr[...] = jnp.broadcast_to(m_new, m_scr.shape)

        y = hh * lax.rsqrt(jnp.mean(hh * hh, axis=-1, keepdims=True) + EPS) * nw
        og = jnp.concatenate([og_ref[0, sl, :], og_ref[1, sl, :]], axis=1).astype(F32)
        out_ref[sl, :] = (y * _sigmoid(og)).astype(out_ref.dtype)


def _mlstm(proj3, gates, gate_bias_pad, conv_w, conv_b, norm_w, batch, seq, q_off):
    T = batch * seq
    rows = min(MIX_ROWS, seq)
    nblk = seq // rows
    H = ML_HEADS
    k_off = q_off + H
    v_off = k_off + H
    o_off = v_off + 2 * H

    def slab(off):
        return pl.BlockSpec((1, rows, LANES), lambda b, h, c, off=off: (off + h, b * nblk + c, 0))

    def slab2(off):
        return pl.BlockSpec((2, rows, LANES), lambda b, h, c, off=off: (off // 2 + h, b * nblk + c, 0))

    qk_w = H * ML_DQK
    return pl.pallas_call(
        functools.partial(_mlstm_body, rows=rows),
        grid=(batch, H, nblk),
        in_specs=[
            slab(q_off), slab(k_off), slab2(v_off), slab2(o_off),
            pl.BlockSpec((rows, LANES), lambda b, h, c: (b * nblk + c, 0)),
            pl.BlockSpec((1, LANES), lambda b, h, c: (0, 0)),
            pl.BlockSpec((CONV_W, LANES), lambda b, h, c: (0, h)),
            pl.BlockSpec((CONV_W, LANES), lambda b, h, c: (0, H + h)),
            pl.BlockSpec((1, LANES), lambda b, h, c: (0, h)),
            pl.BlockSpec((1, LANES), lambda b, h, c: (0, H + h)),
            pl.BlockSpec((1, ML_DV), lambda b, h, c: (0, h)),
        ],
        out_specs=pl.BlockSpec((rows, ML_DV), lambda b, h, c: (b * nblk + c, h)),
        out_shape=jax.ShapeDtypeStruct((T, H * ML_DV), BF16),
        scratch_shapes=[
            pltpu.VMEM((ML_DQK, ML_DV), F32),
            pltpu.VMEM((1, ML_DQK), F32),
            pltpu.VMEM((1, LANES), F32),
            pltpu.VMEM((rows + 2 * SUBLANES, LANES), F32),
            pltpu.VMEM((rows + 2 * SUBLANES, LANES), F32),
            pltpu.VMEM((rows, LANES), F32),
            pltpu.VMEM((rows, LANES), F32),
        ],
        compiler_params=_cparams(("parallel", "parallel", "arbitrary")),
        name="mlstm",
    )(proj3, proj3, proj3, proj3, gates, gate_bias_pad, conv_w, conv_w,
      conv_b.reshape(1, 2 * qk_w), conv_b.reshape(1, 2 * qk_w), norm_w.reshape(1, H * ML_DV))


def _outproj_body(a_ref, b_ref, x_ref, wo_ref, ln_ref, wrh_ref, wrl_ref, br_ref,
                  x2_ref, h2_ref, idx_ref, gate_ref, rank_ref, cnt_ref, cnt_scr, *, sub_rows):
    tm = x_ref.shape[0]
    ka = a_ref.shape[1]
    s = _packed_rows(x_ref.shape[1])

    @pl.when(pl.program_id(0) == 0)
    def _():
        cnt_scr[...] = jnp.zeros_like(cnt_scr)

    lane = lax.broadcasted_iota(jnp.int32, (sub_rows, LANES), 1).astype(F32)
    onehots = [[] for _ in range(TOP_K)]
    for r0 in range(0, tm, sub_rows):
        rows = pl.ds(r0, sub_rows)
        res = (jnp.dot(a_ref[rows, :], wo_ref[0:ka, :], preferred_element_type=F32)
               + jnp.dot(b_ref[rows, :], wo_ref[ka:, :], preferred_element_type=F32))
        x2 = x_ref[rows, :] + res
        x2_ref[rows, :] = x2
        h2 = x2 * lax.rsqrt(jnp.mean(x2 * x2, axis=-1, keepdims=True) + EPS) * ln_ref[...]
        _pack_store(h2_ref.at[pl.ds(r0 * s, sub_rows * s), :], h2)

        h_hi = h2.astype(BF16)
        h_lo = (h2 - h_hi.astype(F32)).astype(BF16)
        logits = (jnp.dot(h_hi, wrh_ref[...], preferred_element_type=F32)
                  + jnp.dot(h_lo, wrh_ref[...], preferred_element_type=F32)
                  + jnp.dot(h_hi, wrl_ref[...], preferred_element_type=F32)) + br_ref[...]
        vals, idxs = [], []
        cur = logits
        for _ in range(TOP_K):
            m = jnp.max(cur, axis=1, keepdims=True)
            ix = jnp.min(jnp.where(cur == m, lane, float(LANES)), axis=1, keepdims=True)
            vals.append(m)
            idxs.append(ix)
            cur = jnp.where(lane == ix, -jnp.inf, cur)
        es = [jnp.exp(v - vals[0]) for v in vals]
        inv = 1.0 / (es[0] + es[1] + es[2] + es[3])
        gate = jnp.zeros(logits.shape, F32)
        idx = jnp.zeros(logits.shape, F32)
        for k in range(TOP_K):
            gate = jnp.where(lane == float(k), es[k] * inv, gate)
            idx = jnp.where(lane == float(k), idxs[k], idx)
            onehots[k].append((lane == idxs[k]).astype(F32))
        gate_ref[rows, :] = gate
        idx_ref[rows, :] = idx.astype(jnp.int32)

    oh_k = [jnp.concatenate(o, axis=0) for o in onehots]
    oh = oh_k[0] + oh_k[1] + oh_k[2] + oh_k[3]
    r_i = lax.broadcasted_iota(jnp.int32, (tm, tm), 0)
    c_i = lax.broadcasted_iota(jnp.int32, (tm, tm), 1)
    before = jnp.dot((c_i < r_i).astype(BF16), oh.astype(BF16), preferred_element_type=F32) + cnt_scr[...]
    lane_t = lax.broadcasted_iota(jnp.int32, (tm, LANES), 1)
    rank = jnp.zeros((tm, LANES), F32)
    for k in range(TOP_K):
        rank = jnp.where(lane_t == k, jnp.sum(oh_k[k] * before, axis=1, keepdims=True), rank)
    rank_ref[...] = rank.astype(jnp.int32)
    cnt = cnt_scr[...] + jnp.sum(oh, axis=0, keepdims=True)
    cnt_scr[...] = cnt
    cnt_ref[...] = cnt.astype(jnp.int32)


def _outproj(a_out, b_out, x2d, w_out_bf, ln_w, wr_hi, wr_lo, b_router_pad):
    T, D = x2d.shape
    tm = min(OUTPROJ_TM, T)
    ka, kb = a_out.shape[1], b_out.shape[1]
    s = _packed_rows(D)
    row = lambda w: pl.BlockSpec((tm, w), lambda i: (i, 0))
    full = lambda r, c: pl.BlockSpec((r, c), lambda i: (0, 0))
    return pl.pallas_call(
        functools.partial(_outproj_body, sub_rows=min(OUTPROJ_SUB, tm)),
        grid=(T // tm,),
        in_specs=[row(ka), row(kb), row(D), full(ka + kb, D), full(1, D), full(D, LANES), full(D, LANES),
                  full(1, LANES)],
        out_specs=[row(D), pl.BlockSpec((tm * s, LANES), lambda i: (i, 0)), row(LANES), row(LANES), row(LANES),
                   full(1, LANES)],
        out_shape=[
            jax.ShapeDtypeStruct((T, D), F32),
            jax.ShapeDtypeStruct((T * s, LANES), jnp.uint32),
            jax.ShapeDtypeStruct((T, LANES), jnp.int32),
            jax.ShapeDtypeStruct((T, LANES), F32),
            jax.ShapeDtypeStruct((T, LANES), jnp.int32),
            jax.ShapeDtypeStruct((1, LANES), jnp.int32),
        ],
        scratch_shapes=[pltpu.VMEM((1, LANES), F32)],
        compiler_params=_cparams(("arbitrary",)),
        name="outproj_router",
    )(a_out, b_out, x2d, w_out_bf, ln_w, wr_hi, wr_lo, b_router_pad)


def _start_row_gather(idx_at, src_ref, buf, sem, n, s):
    def start(i, carry):
        src_row = pl.multiple_of(idx_at(i) * s, s)
        dst_row = pl.multiple_of(i * s, s)
        pltpu.make_async_copy(src_ref.at[pl.ds(src_row, s), :], buf.at[pl.ds(dst_row, s), :], sem).start()
        return carry

    lax.fori_loop(0, n, start, 0, unroll=DMA_UNROLL)


def _wait_row_gather(buf, sem):
    pltpu.make_async_copy(buf, buf, sem).wait()


def _dispatch_body(zero_ref, dest_ref, h_ref, xs_ref, zbuf, sem, zsem, *, ntok, s, tm, n_blocks):
    blk = tm * s

    def zero_copy(b):
        return pltpu.make_async_copy(zbuf, xs_ref.at[pl.ds(pl.multiple_of(b * blk, blk), blk), :], zsem)

    @pl.when(pl.program_id(0) == 0)
    def _():
        zbuf[...] = jnp.zeros_like(zbuf)

        def zstart(b, carry):
            @pl.when(zero_ref[b] == 1)
            def _():
                zero_copy(b).start()
            return carry

        def zwait(b, carry):
            @pl.when(zero_ref[b] == 1)
            def _():
                zero_copy(b).wait()
            return carry

        lax.fori_loop(0, n_blocks, zstart, 0)
        lax.fori_loop(0, n_blocks, zwait, 0)

    def start(t, carry):
        src = h_ref.at[pl.ds(pl.multiple_of(t * s, s), s), :]
        for k in range(TOP_K):
            dst_row = pl.multiple_of(dest_ref[0, 0, t * TOP_K + k] * s, s)
            pltpu.make_async_copy(src, xs_ref.at[pl.ds(dst_row, s), :], sem).start()
        return carry

    lax.fori_loop(0, ntok, start, 0, unroll=DMA_UNROLL // TOP_K)
    for _ in range(TOP_K):
        pltpu.make_async_copy(h_ref, h_ref, sem).wait()


def _dispatch(dest, zero_blk, h_packed, d, n_slots, tm):
    s = _packed_rows(d)
    T = h_packed.shape[0] // s
    ntok = min(DISPATCH_TOKENS, T)
    n = ntok * TOP_K
    grid_spec = pltpu.PrefetchScalarGridSpec(
        num_scalar_prefetch=1,
        grid=(T // ntok,),
        in_specs=[
            pl.BlockSpec((1, 1, n), lambda i, z: (i, 0, 0), memory_space=pltpu.SMEM),
            pl.BlockSpec((ntok * s, LANES), lambda i, z: (i, 0)),
        ],
        out_specs=pl.BlockSpec(memory_space=pl.ANY),
        scratch_shapes=[pltpu.VMEM((tm * s, LANES), jnp.uint32), pltpu.SemaphoreType.DMA,
                        pltpu.SemaphoreType.DMA],
    )
    return pl.pallas_call(
        functools.partial(_dispatch_body, ntok=ntok, s=s, tm=tm, n_blocks=zero_blk.shape[0]),
        grid_spec=grid_spec,
        out_shape=jax.ShapeDtypeStruct((n_slots * s, LANES), jnp.uint32),
        compiler_params=_cparams(("arbitrary",)),
        name="moe_dispatch",
    )(zero_blk, dest.reshape(T // ntok, 1, n), h_packed)


def _cast_tile(dst_ref, src_ref, step=256):
    for r in range(0, dst_ref.shape[0], step):
        dst_ref[r:r + step, :] = src_ref[0, r:r + step, :].astype(BF16)


def _moe_up_body(e_ref, wt_ref, r_ref, ot_ref, first_ref, valid_ref,
                 x_ref, wg_ref, wu_ref, bg_ref, bu_ref, o_ref, wg_scr, wu_scr, x_scr):
    w = pl.program_id(0)

    @pl.when(first_ref[w] == 1)
    def _():
        _cast_tile(wg_scr, wg_ref)
        _cast_tile(wu_scr, wu_ref)

    @pl.when(valid_ref[w] == 1)
    def _():
        tm, d = x_scr.shape
        s = _packed_rows(d)
        for c in range(s):
            hi, lo = _unpack_load(x_ref, 0, tm, s, c)
            x_scr[:, c * LANES:(c + 1) * LANES] = hi.astype(BF16)
            x_scr[:, d // 2 + c * LANES:d // 2 + (c + 1) * LANES] = lo.astype(BF16)
        x = x_scr[...]
        gt = jnp.dot(x, wg_scr[...], preferred_element_type=F32) + bg_ref[0]
        up = jnp.dot(x, wu_scr[...], preferred_element_type=F32) + bu_ref[0]
        gt = jnp.minimum(gt, SWIGLU_LIMIT)
        up = jnp.clip(up, -SWIGLU_LIMIT, SWIGLU_LIMIT)
        o_ref[...] = ((up + 1.0) * (gt * _sigmoid(SWIGLU_ALPHA * gt))).astype(o_ref.dtype)

    @pl.when(valid_ref[w] == 0)
    def _():
        o_ref[...] = jnp.zeros_like(o_ref)


def _moe_down_body(e_ref, wt_ref, r_ref, ot_ref, first_ref, valid_ref, a_ref, wd_ref, bd_ref, o_ref, wd_scr):
    w = pl.program_id(0)

    @pl.when(first_ref[w] == 1)
    def _():
        _cast_tile(wd_scr, wd_ref)

    @pl.when(valid_ref[w] == 1)
    def _():
        y = jnp.dot(a_ref[...], wd_scr[...], preferred_element_type=F32) + bd_ref[0]
        _pack_store(o_ref, y)

    @pl.when(valid_ref[w] == 0)
    def _():
        o_ref[...] = jnp.zeros_like(o_ref)


def _moe_schedule(blocks_e, n_tiles, n_blocks):
    n_items = n_tiles * n_blocks
    bstart = jnp.cumsum(blocks_e) - blocks_e
    item_end = n_tiles * jnp.cumsum(blocks_e)
    total = item_end[-1]
    w = jnp.arange(n_items, dtype=jnp.int32)
    valid = w < total
    wc = jnp.minimum(w, jnp.maximum(total - 1, 0))
    e = jnp.minimum(jnp.sum((item_end[None, :] <= wc[:, None]).astype(jnp.int32), axis=1), N_EXPERTS - 1)
    local = wc - n_tiles * bstart[e]
    nb = jnp.maximum(blocks_e[e], 1)
    wtile = local // nb
    jblk = local % nb
    spare = w - total
    rblk = jnp.where(valid, bstart[e] + jblk, total // n_tiles + spare // n_tiles)
    otile = jnp.where(valid, wtile, spare % n_tiles)
    first = (jblk == 0) & valid
    i32 = lambda a: a.astype(jnp.int32)
    return (e, i32(wtile), i32(rblk), i32(otile), i32(first), i32(valid))


def _moe_up(sched, xs_packed, w_gate, w_up, b_gate, b_up):
    D, d_ff = w_gate.shape[1], w_gate.shape[2]
    s = _packed_rows(D)
    n_slots = xs_packed.shape[0] // s
    tm, tf = MOE_TM, min(MOE_TF, d_ff)
    n_items = sched[0].shape[0]
    wspec = pl.BlockSpec((1, D, tf), lambda w, e, wt, r, ot, fi, va: (e[w], 0, wt[w]))
    bspec = pl.BlockSpec((1, 1, tf), lambda w, e, wt, r, ot, fi, va: (e[w], 0, wt[w]))
    grid_spec = pltpu.PrefetchScalarGridSpec(
        num_scalar_prefetch=6,
        grid=(n_items,),
        in_specs=[pl.BlockSpec((tm * s, LANES), lambda w, e, wt, r, ot, fi, va: (r[w], 0)),
                  wspec, wspec, bspec, bspec],
        out_specs=pl.BlockSpec((tm, tf), lambda w, e, wt, r, ot, fi, va: (r[w], ot[w])),
        scratch_shapes=[pltpu.VMEM((D, tf), BF16), pltpu.VMEM((D, tf), BF16), pltpu.VMEM((tm, D), BF16)],
    )
    return pl.pallas_call(
        _moe_up_body,
        grid_spec=grid_spec,
        out_shape=jax.ShapeDtypeStruct((n_slots, d_ff), BF16),
        compiler_params=_cparams(("arbitrary",)),
        name="moe_up",
    )(*sched, xs_packed, w_gate, w_up, b_gate.reshape(N_EXPERTS, 1, d_ff), b_up.reshape(N_EXPERTS, 1, d_ff))


def _moe_down(sched, act, w_down, b_down):
    n_slots, d_ff = act.shape
    D = w_down.shape[2]
    tm = MOE_DOWN_TM
    s = _packed_rows(D)
    n_items = sched[0].shape[0]
    grid_spec = pltpu.PrefetchScalarGridSpec(
        num_scalar_prefetch=6,
        grid=(n_items,),
        in_specs=[
            pl.BlockSpec((tm, d_ff), lambda w, e, wt, r, ot, fi, va: (r[w], 0)),
            pl.BlockSpec((1, d_ff, D), lambda w, e, wt, r, ot, fi, va: (e[w], 0, 0)),
            pl.BlockSpec((1, 1, D), lambda w, e, wt, r, ot, fi, va: (e[w], 0, 0)),
        ],
        out_specs=pl.BlockSpec((tm * s, LANES), lambda w, e, wt, r, ot, fi, va: (r[w], 0)),
        scratch_shapes=[pltpu.VMEM((d_ff, D), BF16)],
    )
    return pl.pallas_call(
        _moe_down_body,
        grid_spec=grid_spec,
        out_shape=jax.ShapeDtypeStruct((n_slots * s, LANES), jnp.uint32),
        compiler_params=_cparams(("arbitrary",)),
        name="moe_down",
    )(*sched, act, w_down, b_down.reshape(N_EXPERTS, 1, D))


def _final_body(dest_ref, next_ref, x2_ref, gate_ref, w_ref, ys_ref, o_ref, buf0, buf1, sem0, sem1, *, tm, s):
    n = tm * TOP_K
    i = pl.program_id(0)

    def gather(idx_ref, tile, buf, sem):
        _start_row_gather(lambda r: idx_ref[0, 0, tile * n + r], ys_ref, buf, sem, n, s)

    def combine(tile, buf):
        rows = pl.ds(tile * tm, tm)
        gate = gate_ref[rows, :]
        half = o_ref.shape[1] // 2
        o_ref[rows, :] = x2_ref[rows, :]
        for k in range(TOP_K):
            g = gate[:, k:k + 1]
            for c in range(s):
                hi, lo = _unpack_load(buf, k * tm, tm, s, c)
                o_ref[rows, c * LANES:(c + 1) * LANES] += g * hi
                o_ref[rows, half + c * LANES:half + (c + 1) * LANES] += g * lo
        acc = o_ref[rows, :]
        o_ref[rows, :] = acc * lax.rsqrt(jnp.mean(acc * acc, axis=-1, keepdims=True) + EPS) * w_ref[...]

    @pl.when(i == 0)
    def _():
        gather(dest_ref, 0, buf0, sem0)

    gather(dest_ref, 1, buf1, sem1)
    _wait_row_gather(buf0, sem0)
    combine(0, buf0)
    gather(next_ref, 0, buf0, sem0)
    _wait_row_gather(buf1, sem1)
    combine(1, buf1)

    @pl.when(i == pl.num_programs(0) - 1)
    def _():
        _wait_row_gather(buf0, sem0)


def _final(x2, ys_packed, dest, gates_pad, w):
    T, D = x2.shape
    tm = min(FINAL_TM, T // 2)
    s = _packed_rows(D)
    n = tm * TOP_K
    steps = T // (2 * tm)
    dest_km = dest.reshape(steps, 2, tm, TOP_K).transpose(0, 1, 3, 2).reshape(steps, 1, 2 * n)
    buf = pltpu.VMEM((n * s, LANES), jnp.uint32)
    return pl.pallas_call(
        functools.partial(_final_body, tm=tm, s=s),
        grid=(steps,),
        in_specs=[
            pl.BlockSpec((1, 1, 2 * n), lambda i: (i, 0, 0), memory_space=pltpu.SMEM),
            pl.BlockSpec((1, 1, 2 * n), lambda i: (jnp.minimum(i + 1, steps - 1), 0, 0), memory_space=pltpu.SMEM),
            pl.BlockSpec((2 * tm, D), lambda i: (i, 0)),
            pl.BlockSpec((2 * tm, LANES), lambda i: (i, 0)),
            pl.BlockSpec((1, D), lambda i: (0, 0)),
            pl.BlockSpec(memory_space=pl.ANY),
        ],
        out_specs=pl.BlockSpec((2 * tm, D), lambda i: (i, 0)),
        out_shape=jax.ShapeDtypeStruct((T, D), F32),
        scratch_shapes=[buf, buf, pltpu.SemaphoreType.DMA, pltpu.SemaphoreType.DMA],
        compiler_params=_cparams(("arbitrary",)),
        name="final_norm",
    )(dest_km, dest_km, x2, gates_pad, w, ys_packed)


def _moe(h2_packed, T, D, top_idx, rank, counts, w_gate, b_gate, w_up, b_up, w_down, b_down):
    A = T * TOP_K
    tm = MOE_TM
    n_blocks = (A + N_EXPERTS * (tm - 1) + tm - 1) // tm
    n_slots = n_blocks * tm

    blocks_e = (counts + tm - 1) // tm
    bend = jnp.cumsum(blocks_e)
    bstart = bend - blocks_e
    experts = jnp.arange(N_EXPERTS, dtype=jnp.int32)
    first_slot = jnp.sum(jnp.where(top_idx[:, :, None] == experts, bstart * tm, 0), axis=-1)
    dest = (first_slot + rank).astype(jnp.int32).reshape(A)
    blk = jnp.arange(n_blocks, dtype=jnp.int32)
    blk_e = jnp.minimum(jnp.sum((bend[None, :] <= blk[:, None]).astype(jnp.int32), axis=1), N_EXPERTS - 1)
    zero_blk = ((blk >= bend[-1]) | (blk == bend[blk_e] - 1)).astype(jnp.int32)

    xs = _dispatch(dest, zero_blk, h2_packed, D, n_slots, tm)
    d_ff = w_gate.shape[2]
    act = _moe_up(_moe_schedule(blocks_e, -(-d_ff // MOE_TF), n_blocks), xs, w_gate, w_up, b_gate, b_up)
    split = tm // MOE_DOWN_TM
    ys = _moe_down(_moe_schedule(blocks_e * split, 1, n_blocks * split), act, w_down, b_down)
    return ys, dest


def kernel(x, ln1_w, w_in, hg_lb_logits, hg_norm_w, ml_conv_w, ml_conv_b, ml_igate_b, ml_fgate_b, ml_norm_w,
           w_out, ln2_w, w_router, b_router, w_gate, b_gate, w_up, b_up, w_down, b_down, final_norm_w):
    B, S, D = x.shape
    T = B * S
    depth = w_in.shape[0]
    hg_w = HG_HEADS * HG_DK
    n_main = 4 * hg_w + 2 * ML_HEADS * ML_DQK + 2 * ML_HEADS * ML_DV
    lb_all = jnp.cumsum(jax.nn.softmax(hg_lb_logits.astype(F32), axis=0), axis=0)

    xc = x.reshape(T, D)
    for l in range(depth):
        w_gates_pad = jnp.pad(w_in[l][:, n_main:], ((0, 0), (0, LANES - 2 * ML_HEADS))).astype(BF16)
        w_bf = w_in[l].astype(BF16)
        ln1 = ln1_w[l].reshape(1, D)
        hf_tiles = hg_w // INPROJ_TN
        proj_f = _inproj(xc, ln1, w_bf, hf_tiles, lambda j: j + hf_tiles, F32)[0]
        proj_b, gates = _inproj(xc, ln1, w_bf, n_main // INPROJ_TN - hf_tiles,
                                lambda j: jnp.where(j >= hf_tiles, j + hf_tiles, j), BF16, w_gates_pad)
        a_out = _hgrn(proj_b, proj_f, lb_all[l], hg_norm_w[l], B, S)
        gate_bias = jnp.pad(jnp.concatenate([ml_igate_b[l], ml_fgate_b[l]]), (0, LANES - 2 * ML_HEADS))
        b_out = _mlstm(proj_b, gates, gate_bias.reshape(1, LANES), ml_conv_w[l], ml_conv_b[l], ml_norm_w[l],
                       B, S, 3 * HG_HEADS)
        wr_pad = jnp.pad(w_router[l], ((0, 0), (0, LANES - N_EXPERTS)))
        wr_hi = wr_pad.astype(BF16)
        wr_lo = (wr_pad - wr_hi.astype(F32)).astype(BF16)
        br_pad = jnp.pad(b_router[l], (0, LANES - N_EXPERTS), constant_values=-1e30).reshape(1, LANES)
        x2, h2, idx_pad, gates_pad, rank_pad, cnt = _outproj(
            a_out, b_out, xc, w_out[l].astype(BF16), ln2_w[l].reshape(1, D), wr_hi, wr_lo, br_pad)
        ys, dest = _moe(h2, T, D, idx_pad[:, :TOP_K], rank_pad[:, :TOP_K], cnt[0, :N_EXPERTS],
                        w_gate[l], b_gate[l], w_up[l], b_up[l], w_down[l], b_down[l])
        if l + 1 < depth:
            raise NotImplementedError("only the final layer fuses the output norm")
        xc = _final(x2, ys, dest, gates_pad, final_norm_w.reshape(1, D))
    return xc.reshape(B, S, D)
```

```python
import functools

import jax
import jax.numpy as jnp
from jax import lax
from jax.experimental import pallas as pl
from jax.experimental.pallas import tpu as pltpu

F32 = jnp.float32
BF16 = jnp.bfloat16
HIGHEST = lax.Precision.HIGHEST

EPS = 1e-6
HG_HEADS = 8
HG_DK = 128
ML_HEADS = 4
ML_DQK = 128
ML_DV = 256
CONV_W = 4
N_EXPERTS = 32
TOP_K = 4
SWIGLU_ALPHA = 1.702
SWIGLU_LIMIT = 7.0

LANES = 128
SUBLANES = 8
VMEM_LIMIT_BYTES = 56 * 1024 * 1024

HG_CHUNK = 64
HG_SUB = 16
ML_CHUNK = 128
MIX_ROWS = 512
HG_HEADS_PER_STEP = 2

INPROJ_TM = 1024
INPROJ_TN = 512
OUTPROJ_TM = 256
OUTPROJ_SUB = 128
DISPATCH_TOKENS = 256
MOE_TM = 512
MOE_TF = 1024
MOE_SUB_ROWS = 256
MOE_DOWN_TM = 512
DMA_UNROLL = 32
DMA_QUEUES = 2
FINAL_TM = 128


def _dot_nt(a, b):
    return lax.dot_general(a, b, (((1,), (1,)), ((), ())), preferred_element_type=F32)


def _dot_tn(a, b):
    return lax.dot_general(a, b, (((0,), (0,)), ((), ())), preferred_element_type=F32)


def _log_sigmoid(z):
    return jnp.minimum(z, 0.0) - jnp.log1p(jnp.exp(-jnp.abs(z)))


def _sigmoid(z):
    return 1.0 / (1.0 + jnp.exp(-z))


def _cparams(semantics):
    return pltpu.CompilerParams(dimension_semantics=semantics, vmem_limit_bytes=VMEM_LIMIT_BYTES)


_HI_MASK = 0xFFFF0000


def _packed_rows(d):
    return d // (2 * LANES)


def _pack_store(o_ref, v):
    n, d = v.shape
    s, half = _packed_rows(d), d // 2
    bits = pltpu.bitcast(v.astype(BF16).astype(F32), jnp.uint32)
    for c in range(s):
        hi = bits[:, c * LANES:(c + 1) * LANES]
        lo = bits[:, half + c * LANES:half + (c + 1) * LANES]
        o_ref[pl.ds(c, n, stride=s), :] = hi | jnp.right_shift(lo, jnp.uint32(16))


def _unpack_load(buf, first_row, n, s, c):
    w = buf[pl.ds(first_row * s + c, n, stride=s), :]
    hi = pltpu.bitcast(w & jnp.uint32(_HI_MASK), F32)
    lo = pltpu.bitcast(jnp.left_shift(w, jnp.uint32(16)), F32)
    return hi, lo


def _inproj_body(*refs, tn, with_gates):
    if with_gates:
        x_ref, lnw_ref, w_ref, wg_ref, o_ref, g_ref, h_scr = refs
    else:
        x_ref, lnw_ref, w_ref, o_ref, h_scr = refs

    @pl.when(pl.program_id(1) == 0)
    def _():
        x = x_ref[...]
        h = x * lax.rsqrt(jnp.mean(x * x, axis=-1, keepdims=True) + EPS) * lnw_ref[...]
        hb = h.astype(BF16)
        h_scr[...] = hb
        if with_gates:
            g_ref[...] = jnp.dot(hb, wg_ref[...], preferred_element_type=F32)

    res = jnp.dot(h_scr[...], w_ref[...], preferred_element_type=F32)
    for c in range(tn // LANES):
        o_ref[c] = res[:, c * LANES:(c + 1) * LANES].astype(o_ref.dtype)


def _inproj(x2d, ln_w, w_bf, n_tiles, col_tile, out_dtype, w_gates_pad=None):
    T, D = x2d.shape
    tm = min(INPROJ_TM, T)
    tn = INPROJ_TN
    with_gates = w_gates_pad is not None
    in_specs = [
        pl.BlockSpec((tm, D), lambda i, j: (i, 0)),
        pl.BlockSpec((1, D), lambda i, j: (0, 0)),
        pl.BlockSpec((D, tn), lambda i, j: (0, col_tile(j))),
    ]
    out_specs = [pl.BlockSpec((tn // LANES, tm, LANES), lambda i, j: (j, i, 0))]
    out_shape = [jax.ShapeDtypeStruct((n_tiles * tn // LANES, T, LANES), out_dtype)]
    args = [x2d, ln_w, w_bf]
    if with_gates:
        in_specs.append(pl.BlockSpec((D, LANES), lambda i, j: (0, 0)))
        out_specs.append(pl.BlockSpec((tm, LANES), lambda i, j: (i, 0)))
        out_shape.append(jax.ShapeDtypeStruct((T, LANES), F32))
        args.append(w_gates_pad)
    return pl.pallas_call(
        functools.partial(_inproj_body, tn=tn, with_gates=with_gates),
        grid=(T // tm, n_tiles),
        in_specs=in_specs,
        out_specs=out_specs,
        out_shape=out_shape,
        scratch_shapes=[pltpu.VMEM((tm, D), BF16)],
        compiler_params=_cparams(("parallel", "arbitrary")),
        name="inproj_gates" if with_gates else "inproj",
    )(*args)


def _hgrn_body(q_ref, f_ref, i_ref, g_ref, lb_ref, nw_ref, o_ref, st_scr, *, rows, heads):
    C, SUB = HG_CHUNK, HG_SUB
    nsub = C // SUB

    @pl.when(pl.program_id(2) == 0)
    def _():
        st_scr[...] = jnp.zeros_like(st_scr)

    lbs = [lb_ref[hh] for hh in range(heads)]
    log_lbs = [jnp.log(lb) for lb in lbs]
    log_1mlbs = [jnp.log1p(-lb) for lb in lbs]

    r_i = lax.broadcasted_iota(jnp.int32, (C, C), 0)
    c_i = lax.broadcasted_iota(jnp.int32, (C, C), 1)
    tri = (c_i <= r_i).astype(F32)
    sub_shift = SUB.bit_length() - 1
    diag_mask = (c_i <= r_i) & (jnp.right_shift(r_i, sub_shift) == jnp.right_shift(c_i, sub_shift))
    row_id = lax.broadcasted_iota(jnp.int32, (C, HG_DK), 0)

    for j, hh in [(j, hh) for j in range(rows // C) for hh in range(heads)]:
        sl = pl.ds(j * C, C)
        q = q_ref[hh, sl, :].astype(F32)
        z = f_ref[hh, sl, :].astype(F32)
        v = i_ref[hh, sl, :].astype(F32)
        g = g_ref[hh, sl, :].astype(F32)
        log_lb, log_1mlb, one_m_lb, nw = log_lbs[hh], log_1mlbs[hh], 1.0 - lbs[hh], nw_ref[hh]

        e = jnp.exp(-jnp.abs(z))
        log_sig = jnp.minimum(z, 0.0) - jnp.log1p(e)
        sig_neg = jnp.where(z >= 0, e, 1.0) / (1.0 + e)
        cc = log_1mlb + log_sig
        log_f = jnp.maximum(log_lb, cc) + jnp.log1p(jnp.exp(-jnp.abs(log_lb - cc)))
        kk = one_m_lb * sig_neg
        b = jnp.dot(tri, log_f, precision=HIGHEST, preferred_element_type=F32)
        b_last = b[C - 1:C, :]

        st = st_scr[hh]
        vb = v.astype(BF16)
        qe = (q * jnp.exp(b)).astype(BF16)
        kdec = (kk * jnp.exp(b_last - b)).astype(BF16)
        inter = _dot_nt(qe, st.astype(BF16))

        refs = [b[I * SUB:I * SUB + 1, :] for I in range(nsub)]
        refb = jnp.concatenate([jnp.broadcast_to(r, (SUB, HG_DK)) for r in refs], axis=0)
        qd = (q * jnp.exp(b - refb)).astype(BF16)
        kd = (kk * jnp.exp(refb - b)).astype(BF16)
        att = jnp.where(diag_mask, _dot_nt(qd, kd), 0.0)
        q_parts, k_parts = [], []
        for J in range(nsub - 1):
            r = refs[J + 1]
            qj = q * jnp.exp(jnp.minimum(b - r, 0.0))
            kj = kk * jnp.exp(jnp.minimum(r - b, 0.0))
            q_parts.append(jnp.where(row_id >= (J + 1) * SUB, qj, 0.0).astype(BF16))
            k_parts.append(jnp.where((row_id >= J * SUB) & (row_id < (J + 1) * SUB), kj, 0.0).astype(BF16))
        att = att + _dot_nt(jnp.concatenate(q_parts, axis=1), jnp.concatenate(k_parts, axis=1))
        intra = jnp.dot(att.astype(BF16), vb, preferred_element_type=F32)

        o = inter + intra
        y = o * lax.rsqrt(jnp.mean(o * o, axis=-1, keepdims=True) + EPS) * nw
        o_ref[sl, hh * LANES:(hh + 1) * LANES] = (y * (g * _sigmoid(g))).astype(o_ref.dtype)

        st_scr[hh] = st * jnp.exp(b_last) + _dot_tn(vb, kdec)


def _hgrn(proj_b, proj_f, lb, norm_w, batch, seq):
    T = batch * seq
    rows = min(MIX_ROWS, seq)
    nblk = seq // rows
    H = HG_HEADS
    hp = HG_HEADS_PER_STEP

    def slab(off):
        return pl.BlockSpec((hp, rows, LANES), lambda b, h, c, off=off: (off // hp + h, b * nblk + c, 0))

    vec = pl.BlockSpec((hp, 1, LANES), lambda b, h, c: (h, 0, 0))
    return pl.pallas_call(
        functools.partial(_hgrn_body, rows=rows, heads=hp),
        grid=(batch, H // hp, nblk),
        in_specs=[slab(0), slab(0), slab(H), slab(2 * H), vec, vec],
        out_specs=pl.BlockSpec((rows, hp * LANES), lambda b, h, c: (b * nblk + c, h)),
        out_shape=jax.ShapeDtypeStruct((T, H * LANES), BF16),
        scratch_shapes=[pltpu.VMEM((hp, LANES, HG_DK), F32)],
        compiler_params=_cparams(("parallel", "parallel", "arbitrary")),
        name="hgrn2",
    )(proj_b, proj_f, proj_b, proj_b, lb.reshape(H, 1, HG_DK), norm_w.reshape(H, 1, LANES))


def _mlstm_body(q_ref, k_ref, v_ref, og_ref, gt_ref, gb_ref, cwq_ref, cwk_ref, cbq_ref, cbk_ref, nw_ref,
                out_ref, c_scr, n_scr, m_scr, qx_scr, kx_scr, qc_scr, kc_scr, *, rows):
    C = ML_CHUNK
    PAD = SUBLANES
    h = pl.program_id(1)

    @pl.when(pl.program_id(2) == 0)
    def _():
        c_scr[...] = jnp.zeros_like(c_scr)
        n_scr[...] = jnp.zeros_like(n_scr)
        m_scr[...] = jnp.zeros_like(m_scr)
        qx_scr[0:PAD, :] = jnp.zeros((PAD, LANES), F32)
        kx_scr[0:PAD, :] = jnp.zeros((PAD, LANES), F32)

    qx_scr[PAD:PAD + rows, :] = q_ref[0].astype(F32)
    kx_scr[PAD:PAD + rows, :] = k_ref[0].astype(F32)
    accq = jnp.zeros((rows, LANES), F32) + cbq_ref[...]
    acck = jnp.zeros((rows, LANES), F32) + cbk_ref[...]
    for j in range(CONV_W):
        off = PAD - (CONV_W - 1) + j
        accq = accq + cwq_ref[j:j + 1, :] * qx_scr[pl.ds(off, rows), :]
        acck = acck + cwk_ref[j:j + 1, :] * kx_scr[pl.ds(off, rows), :]
    qc_scr[...] = accq * _sigmoid(accq) * (ML_DQK ** -0.5)
    kc_scr[...] = acck * _sigmoid(acck)
    qx_scr[0:PAD, :] = qx_scr[rows:rows + PAD, :]
    kx_scr[0:PAD, :] = kx_scr[rows:rows + PAD, :]

    lane = lax.broadcasted_iota(jnp.int32, (C, C), 1)
    sub = lax.broadcasted_iota(jnp.int32, (C, C), 0)
    causal = lane <= sub
    lower = causal.astype(F32)
    upper = (sub <= lane).astype(F32)
    nw = nw_ref[...]
    gbias = gb_ref[...]

    for j in range(rows // C):
        sl = pl.ds(j * C, C)
        gc = gt_ref[sl, :] + gbias
        gct = gc.T
        li_col = jnp.sum(jnp.where(lane == h, gc, 0.0), axis=1, keepdims=True)
        fg_col = jnp.sum(jnp.where(lane == h + ML_HEADS, gc, 0.0), axis=1, keepdims=True)
        li_row = jnp.sum(jnp.where(sub == h, gct, 0.0), axis=0, keepdims=True)
        fg_row = jnp.sum(jnp.where(sub == h + ML_HEADS, gct, 0.0), axis=0, keepdims=True)
        lf_col = _log_sigmoid(fg_col)
        lf_row = _log_sigmoid(fg_row)
        g_t = jnp.dot(lower, jnp.broadcast_to(lf_col, (C, C)), precision=HIGHEST,
                      preferred_element_type=F32)
        g_s = jnp.dot(jnp.broadcast_to(lf_row, (C, C)), upper, precision=HIGHEST,
                      preferred_element_type=F32)
        g_col = g_t[:, 0:1]
        g_last = g_t[C - 1:C, 0:1]
        m_prev = m_scr[:, 0:1]

        a_inter = g_col + m_prev
        dmat = jnp.where(causal, g_t - g_s + li_row, -jnp.inf)
        m_t = jnp.maximum(a_inter, jnp.max(dmat, axis=1, keepdims=True))
        w_inter = jnp.exp(a_inter - m_t)
        p = jnp.exp(dmat - m_t)

        qf = qc_scr[sl, :]
        kf = kc_scr[sl, :]
        qb = qf.astype(BF16)
        vb = jnp.concatenate([v_ref[0, sl, :], v_ref[1, sl, :]], axis=1).astype(BF16)
        cm = c_scr[...]
        nv = n_scr[...]
        sqk = _dot_nt(qb, kf.astype(BF16)) * p
        num = (w_inter * jnp.dot(qb, cm.astype(BF16), preferred_element_type=F32)
               + jnp.dot(sqk.astype(BF16), vb, preferred_element_type=F32))
        den = (w_inter * jnp.sum(qf * nv, axis=1, keepdims=True)
               + jnp.sum(sqk, axis=1, keepdims=True))
        hh = num * (1.0 / jnp.maximum(jnp.abs(den), jnp.exp(-m_t)))

        log_ws = g_last - g_col + li_col
        m_new = jnp.maximum(g_last + m_prev, jnp.max(log_ws, axis=0, keepdims=True))
        decay = jnp.exp(g_last + m_prev - m_new)
        kw = kf * jnp.exp(log_ws - m_new)
        c_scr[...] = decay * cm + _dot_tn(kw.astype(BF16), vb)
        n_scr[...] = decay * nv + jnp.sum(kw, axis=0, keepdims=True)
        m_scr[...] = jnp.broadcast_to(m_new, m_scr.shape)

        y = hh * lax.rsqrt(jnp.mean(hh * hh, axis=-1, keepdims=True) + EPS) * nw
        og = jnp.concatenate([og_ref[0, sl, :], og_ref[1, sl, :]], axis=1).astype(F32)
        out_ref[sl, :] = (y * _sigmoid(og)).astype(out_ref.dtype)


def _mlstm(proj3, gates, gate_bias_pad, conv_w, conv_b, norm_w, batch, seq, q_off):
    T = batch * seq
    rows = min(MIX_ROWS, seq)
    nblk = seq // rows
    H = ML_HEADS
    k_off = q_off + H
    v_off = k_off + H
    o_off = v_off + 2 * H

    def slab(off):
        return pl.BlockSpec((1, rows, LANES), lambda b, h, c, off=off: (off + h, b * nblk + c, 0))

    def slab2(off):
        return pl.BlockSpec((2, rows, LANES), lambda b, h, c, off=off: (off // 2 + h, b * nblk + c, 0))

    qk_w = H * ML_DQK
    return pl.pallas_call(
        functools.partial(_mlstm_body, rows=rows),
        grid=(batch, H, nblk),
        in_specs=[
            slab(q_off), slab(k_off), slab2(v_off), slab2(o_off),
            pl.BlockSpec((rows, LANES), lambda b, h, c: (b * nblk + c, 0)),
            pl.BlockSpec((1, LANES), lambda b, h, c: (0, 0)),
            pl.BlockSpec((CONV_W, LANES), lambda b, h, c: (0, h)),
            pl.BlockSpec((CONV_W, LANES), lambda b, h, c: (0, H + h)),
            pl.BlockSpec((1, LANES), lambda b, h, c: (0, h)),
            pl.BlockSpec((1, LANES), lambda b, h, c: (0, H + h)),
            pl.BlockSpec((1, ML_DV), lambda b, h, c: (0, h)),
        ],
        out_specs=pl.BlockSpec((rows, ML_DV), lambda b, h, c: (b * nblk + c, h)),
        out_shape=jax.ShapeDtypeStruct((T, H * ML_DV), BF16),
        scratch_shapes=[
            pltpu.VMEM((ML_DQK, ML_DV), F32),
            pltpu.VMEM((1, ML_DQK), F32),
            pltpu.VMEM((1, LANES), F32),
            pltpu.VMEM((rows + 2 * SUBLANES, LANES), F32),
            pltpu.VMEM((rows + 2 * SUBLANES, LANES), F32),
            pltpu.VMEM((rows, LANES), F32),
            pltpu.VMEM((rows, LANES), F32),
        ],
        compiler_params=_cparams(("parallel", "parallel", "arbitrary")),
        name="mlstm",
    )(proj3, proj3, proj3, proj3, gates, gate_bias_pad, conv_w, conv_w,
      conv_b.reshape(1, 2 * qk_w), conv_b.reshape(1, 2 * qk_w), norm_w.reshape(1, H * ML_DV))


def _outproj_body(a_ref, b_ref, x_ref, wo_ref, ln_ref, wrh_ref, wrl_ref, br_ref,
                  x2_ref, h2_ref, idx_ref, gate_ref, rank_ref, cnt_ref, cnt_scr, *, sub_rows):
    tm = x_ref.shape[0]
    ka = a_ref.shape[1]
    s = _packed_rows(x_ref.shape[1])

    @pl.when(pl.program_id(0) == 0)
    def _():
        cnt_scr[...] = jnp.zeros_like(cnt_scr)

    lane = lax.broadcasted_iota(jnp.int32, (sub_rows, LANES), 1).astype(F32)
    onehots = [[] for _ in range(TOP_K)]
    for r0 in range(0, tm, sub_rows):
        rows = pl.ds(r0, sub_rows)
        res = (jnp.dot(a_ref[rows, :], wo_ref[0:ka, :], preferred_element_type=F32)
               + jnp.dot(b_ref[rows, :], wo_ref[ka:, :], preferred_element_type=F32))
        x2 = x_ref[rows, :] + res
        x2_ref[rows, :] = x2
        h2 = x2 * lax.rsqrt(jnp.mean(x2 * x2, axis=-1, keepdims=True) + EPS) * ln_ref[...]
        _pack_store(h2_ref.at[pl.ds(r0 * s, sub_rows * s), :], h2)

        h_hi = h2.astype(BF16)
        h_lo = (h2 - h_hi.astype(F32)).astype(BF16)
        logits = (jnp.dot(h_hi, wrh_ref[...], preferred_element_type=F32)
                  + jnp.dot(h_lo, wrh_ref[...], preferred_element_type=F32)
                  + jnp.dot(h_hi, wrl_ref[...], preferred_element_type=F32)) + br_ref[...]
        vals, idxs = [], []
        cur = logits
        for _ in range(TOP_K):
            m = jnp.max(cur, axis=1, keepdims=True)
            ix = jnp.min(jnp.where(cur == m, lane, float(LANES)), axis=1, keepdims=True)
            vals.append(m)
            idxs.append(ix)
            cur = jnp.where(lane == ix, -jnp.inf, cur)
        es = [jnp.exp(v - vals[0]) for v in vals]
        inv = 1.0 / (es[0] + es[1] + es[2] + es[3])
        gate = jnp.zeros(logits.shape, F32)
        idx = jnp.zeros(logits.shape, F32)
        for k in range(TOP_K):
            gate = jnp.where(lane == float(k), es[k] * inv, gate)
            idx = jnp.where(lane == float(k), idxs[k], idx)
            onehots[k].append((lane == idxs[k]).astype(F32))
        gate_ref[rows, :] = gate
        idx_ref[rows, :] = idx.astype(jnp.int32)

    oh_k = [jnp.concatenate(o, axis=0) for o in onehots]
    oh = oh_k[0] + oh_k[1] + oh_k[2] + oh_k[3]
    r_i = lax.broadcasted_iota(jnp.int32, (tm, tm), 0)
    c_i = lax.broadcasted_iota(jnp.int32, (tm, tm), 1)
    before = jnp.dot((c_i < r_i).astype(BF16), oh.astype(BF16), preferred_element_type=F32) + cnt_scr[...]
    lane_t = lax.broadcasted_iota(jnp.int32, (tm, LANES), 1)
    rank = jnp.zeros((tm, LANES), F32)
    for k in range(TOP_K):
        rank = jnp.where(lane_t == k, jnp.sum(oh_k[k] * before, axis=1, keepdims=True), rank)
    rank_ref[...] = rank.astype(jnp.int32)
    cnt = cnt_scr[...] + jnp.sum(oh, axis=0, keepdims=True)
    cnt_scr[...] = cnt
    cnt_ref[...] = cnt.astype(jnp.int32)


def _outproj(a_out, b_out, x2d, w_out_bf, ln_w, wr_hi, wr_lo, b_router_pad):
    T, D = x2d.shape
    tm = min(OUTPROJ_TM, T)
    ka, kb = a_out.shape[1], b_out.shape[1]
    s = _packed_rows(D)
    row = lambda w: pl.BlockSpec((tm, w), lambda i: (i, 0))
    full = lambda r, c: pl.BlockSpec((r, c), lambda i: (0, 0))
    return pl.pallas_call(
        functools.partial(_outproj_body, sub_rows=min(OUTPROJ_SUB, tm)),
        grid=(T // tm,),
        in_specs=[row(ka), row(kb), row(D), full(ka + kb, D), full(1, D), full(D, LANES), full(D, LANES),
                  full(1, LANES)],
        out_specs=[row(D), pl.BlockSpec((tm * s, LANES), lambda i: (i, 0)), row(LANES), row(LANES), row(LANES),
                   full(1, LANES)],
        out_shape=[
            jax.ShapeDtypeStruct((T, D), F32),
            jax.ShapeDtypeStruct((T * s, LANES), jnp.uint32),
            jax.ShapeDtypeStruct((T, LANES), jnp.int32),
            jax.ShapeDtypeStruct((T, LANES), F32),
            jax.ShapeDtypeStruct((T, LANES), jnp.int32),
            jax.ShapeDtypeStruct((1, LANES), jnp.int32),
        ],
        scratch_shapes=[pltpu.VMEM((1, LANES), F32)],
        compiler_params=_cparams(("arbitrary",)),
        name="outproj_router",
    )(a_out, b_out, x2d, w_out_bf, ln_w, wr_hi, wr_lo, b_router_pad)


def _start_row_gather(idx_at, src_ref, buf, sem, n, s):
    def start(pair, carry):
        for p in range(DMA_QUEUES):
            i = pair * DMA_QUEUES + p
            src_row = pl.multiple_of(idx_at(i) * s, s)
            dst_row = pl.multiple_of(i * s, s)
            pltpu.make_async_copy(src_ref.at[pl.ds(src_row, s), :], buf.at[pl.ds(dst_row, s), :],
                                  sem).start(priority=p)
        return carry

    lax.fori_loop(0, n // DMA_QUEUES, start, 0, unroll=DMA_UNROLL // DMA_QUEUES)


def _wait_row_gather(buf, sem):
    pltpu.make_async_copy(buf, buf, sem).wait()


def _dispatch_body(zero_ref, dest_ref, h_ref, xs_ref, zbuf, sem, zsem, *, ntok, s, tm, n_blocks):
    blk = tm * s

    def zero_copy(b):
        return pltpu.make_async_copy(zbuf, xs_ref.at[pl.ds(pl.multiple_of(b * blk, blk), blk), :], zsem)

    @pl.when(pl.program_id(0) == 0)
    def _():
        zbuf[...] = jnp.zeros_like(zbuf)

        def zstart(b, carry):
            @pl.when(zero_ref[b] == 1)
            def _():
                zero_copy(b).start()
            return carry

        def zwait(b, carry):
            @pl.when(zero_ref[b] == 1)
            def _():
                zero_copy(b).wait()
            return carry

        lax.fori_loop(0, n_blocks, zstart, 0)
        lax.fori_loop(0, n_blocks, zwait, 0)

    def start(t, carry):
        src = h_ref.at[pl.ds(pl.multiple_of(t * s, s), s), :]
        for k in range(TOP_K):
            dst_row = pl.multiple_of(dest_ref[0, 0, t * TOP_K + k] * s, s)
            pltpu.make_async_copy(src, xs_ref.at[pl.ds(dst_row, s), :], sem).start(priority=k % DMA_QUEUES)
        return carry

    lax.fori_loop(0, ntok, start, 0, unroll=DMA_UNROLL // TOP_K)
    for _ in range(TOP_K):
        pltpu.make_async_copy(h_ref, h_ref, sem).wait()


def _dispatch(dest, zero_blk, h_packed, d, n_slots, tm):
    s = _packed_rows(d)
    T = h_packed.shape[0] // s
    ntok = min(DISPATCH_TOKENS, T)
    n = ntok * TOP_K
    grid_spec = pltpu.PrefetchScalarGridSpec(
        num_scalar_prefetch=1,
        grid=(T // ntok,),
        in_specs=[
            pl.BlockSpec((1, 1, n), lambda i, z: (i, 0, 0), memory_space=pltpu.SMEM),
            pl.BlockSpec((ntok * s, LANES), lambda i, z: (i, 0)),
        ],
        out_specs=pl.BlockSpec(memory_space=pl.ANY),
        scratch_shapes=[pltpu.VMEM((tm * s, LANES), jnp.uint32), pltpu.SemaphoreType.DMA,
                        pltpu.SemaphoreType.DMA],
    )
    return pl.pallas_call(
        functools.partial(_dispatch_body, ntok=ntok, s=s, tm=tm, n_blocks=zero_blk.shape[0]),
        grid_spec=grid_spec,
        out_shape=jax.ShapeDtypeStruct((n_slots * s, LANES), jnp.uint32),
        compiler_params=_cparams(("arbitrary",)),
        name="moe_dispatch",
    )(zero_blk, dest.reshape(T // ntok, 1, n), h_packed)


def _staged_weights(sched_refs, hbm_refs, stage_refs, bf_refs, sem):
    e_ref, wt_ref, first_ref, ne_ref, nt_ref, more_ref = sched_refs
    w = pl.program_id(0)
    tn = stage_refs[0].shape[1]

    def copies(e, t):
        col = pl.multiple_of(t * tn, tn)
        return [pltpu.make_async_copy(h.at[e, :, pl.ds(col, tn)], st, sem.at[k])
                for k, (h, st) in enumerate(zip(hbm_refs, stage_refs))]

    @pl.when(w == 0)
    def _():
        for c in copies(e_ref[0], wt_ref[0]):
            c.start()

    @pl.when(first_ref[w] == 1)
    def _():
        for c in copies(e_ref[w], wt_ref[w]):
            c.wait()
        for st, bf in zip(stage_refs, bf_refs):
            for r in range(0, st.shape[0], 256):
                bf[r:r + 256, :] = st[r:r + 256, :].astype(BF16)

        @pl.when(more_ref[w] == 1)
        def _():
            for c in copies(ne_ref[w], nt_ref[w]):
                c.start()


def _moe_up_body(e_ref, wt_ref, r_ref, ot_ref, first_ref, valid_ref, ne_ref, nt_ref, more_ref,
                 x_ref, wg_ref, wu_ref, bg_ref, bu_ref, o_ref, wg_stage, wu_stage, wg_scr, wu_scr, x_scr, sem):
    w = pl.program_id(0)
    _staged_weights((e_ref, wt_ref, first_ref, ne_ref, nt_ref, more_ref), (wg_ref, wu_ref),
                    (wg_stage, wu_stage), (wg_scr, wu_scr), sem)

    @pl.when(valid_ref[w] == 1)
    def _():
        tm, d = x_scr.shape
        s = _packed_rows(d)
        sub = min(MOE_SUB_ROWS, tm)
        for r0 in range(0, tm, sub):
            rows = pl.ds(r0, sub)
            for c in range(s):
                hi, lo = _unpack_load(x_ref, r0, sub, s, c)
                x_scr[rows, c * LANES:(c + 1) * LANES] = hi.astype(BF16)
                x_scr[rows, d // 2 + c * LANES:d // 2 + (c + 1) * LANES] = lo.astype(BF16)
            x = x_scr[rows, :]
            gt = jnp.dot(x, wg_scr[...], preferred_element_type=F32) + bg_ref[0]
            up = jnp.dot(x, wu_scr[...], preferred_element_type=F32) + bu_ref[0]
            gt = jnp.minimum(gt, SWIGLU_LIMIT)
            up = jnp.clip(up, -SWIGLU_LIMIT, SWIGLU_LIMIT)
            o_ref[rows, :] = ((up + 1.0) * (gt * _sigmoid(SWIGLU_ALPHA * gt))).astype(o_ref.dtype)

    @pl.when(valid_ref[w] == 0)
    def _():
        o_ref[...] = jnp.zeros_like(o_ref)


def _moe_down_body(e_ref, wt_ref, r_ref, ot_ref, first_ref, valid_ref, ne_ref, nt_ref, more_ref,
                   a_ref, wd_ref, bd_ref, o_ref, wd_stage, wd_scr, sem):
    w = pl.program_id(0)
    _staged_weights((e_ref, wt_ref, first_ref, ne_ref, nt_ref, more_ref), (wd_ref,), (wd_stage,), (wd_scr,), sem)

    @pl.when(valid_ref[w] == 1)
    def _():
        tm = a_ref.shape[0]
        s = o_ref.shape[0] // tm
        sub = min(MOE_SUB_ROWS, tm)
        for r0 in range(0, tm, sub):
            y = jnp.dot(a_ref[pl.ds(r0, sub), :], wd_scr[...], preferred_element_type=F32) + bd_ref[0]
            _pack_store(o_ref.at[pl.ds(r0 * s, sub * s), :], y)

    @pl.when(valid_ref[w] == 0)
    def _():
        o_ref[...] = jnp.zeros_like(o_ref)


def _moe_schedule(blocks_e, n_tiles, n_blocks):
    n_items = n_tiles * n_blocks
    bstart = jnp.cumsum(blocks_e) - blocks_e
    item_end = n_tiles * jnp.cumsum(blocks_e)
    total = item_end[-1]
    w = jnp.arange(n_items, dtype=jnp.int32)
    valid = w < total
    wc = jnp.minimum(w, jnp.maximum(total - 1, 0))
    e = jnp.minimum(jnp.sum((item_end[None, :] <= wc[:, None]).astype(jnp.int32), axis=1), N_EXPERTS - 1)
    local = wc - n_tiles * bstart[e]
    nb = jnp.maximum(blocks_e[e], 1)
    wtile = local // nb
    jblk = local % nb
    spare = w - total
    rblk = jnp.where(valid, bstart[e] + jblk, total // n_tiles + spare // n_tiles)
    otile = jnp.where(valid, wtile, spare % n_tiles)
    first = (jblk == 0) & valid
    nxt = jnp.minimum(w + nb, n_items - 1)
    more = first & (w + nb < total)
    i32 = lambda a: a.astype(jnp.int32)
    return (e, i32(wtile), i32(rblk), i32(otile), i32(first), i32(valid), e[nxt], i32(wtile[nxt]), i32(more))


def _moe_up(sched, xs_packed, w_gate, w_up, b_gate, b_up):
    D, d_ff = w_gate.shape[1], w_gate.shape[2]
    s = _packed_rows(D)
    n_slots = xs_packed.shape[0] // s
    tm, tf = MOE_TM, min(MOE_TF, d_ff)
    n_items = sched[0].shape[0]
    wspec = pl.BlockSpec(memory_space=pl.ANY)
    bspec = pl.BlockSpec((1, 1, tf), lambda w, e, wt, r, ot, *_: (e[w], 0, wt[w]))
    grid_spec = pltpu.PrefetchScalarGridSpec(
        num_scalar_prefetch=len(sched),
        grid=(n_items,),
        in_specs=[pl.BlockSpec((tm * s, LANES), lambda w, e, wt, r, ot, *_: (r[w], 0)),
                  wspec, wspec, bspec, bspec],
        out_specs=pl.BlockSpec((tm, tf), lambda w, e, wt, r, ot, *_: (r[w], ot[w])),
        scratch_shapes=[pltpu.VMEM((D, tf), F32), pltpu.VMEM((D, tf), F32),
                        pltpu.VMEM((D, tf), BF16), pltpu.VMEM((D, tf), BF16), pltpu.VMEM((tm, D), BF16),
                        pltpu.SemaphoreType.DMA((2,))],
    )
    return pl.pallas_call(
        _moe_up_body,
        grid_spec=grid_spec,
        out_shape=jax.ShapeDtypeStruct((n_slots, d_ff), BF16),
        compiler_params=_cparams(("arbitrary",)),
        name="moe_up",
    )(*sched, xs_packed, w_gate, w_up, b_gate.reshape(N_EXPERTS, 1, d_ff), b_up.reshape(N_EXPERTS, 1, d_ff))


def _moe_down(sched, act, w_down, b_down):
    n_slots, d_ff = act.shape
    D = w_down.shape[2]
    tm = MOE_DOWN_TM
    s = _packed_rows(D)
    n_items = sched[0].shape[0]
    grid_spec = pltpu.PrefetchScalarGridSpec(
        num_scalar_prefetch=len(sched),
        grid=(n_items,),
        in_specs=[
            pl.BlockSpec((tm, d_ff), lambda w, e, wt, r, ot, *_: (r[w], 0)),
            pl.BlockSpec(memory_space=pl.ANY),
            pl.BlockSpec((1, 1, D), lambda w, e, wt, r, ot, *_: (e[w], 0, 0)),
        ],
        out_specs=pl.BlockSpec((tm * s, LANES), lambda w, e, wt, r, ot, *_: (r[w], 0)),
        scratch_shapes=[pltpu.VMEM((d_ff, D), F32), pltpu.VMEM((d_ff, D), BF16), pltpu.SemaphoreType.DMA((1,))],
    )
    return pl.pallas_call(
        _moe_down_body,
        grid_spec=grid_spec,
        out_shape=jax.ShapeDtypeStruct((n_slots * s, LANES), jnp.uint32),
        compiler_params=_cparams(("arbitrary",)),
        name="moe_down",
    )(*sched, act, w_down, b_down.reshape(N_EXPERTS, 1, D))


def _final_body(dest_ref, next_ref, x2_ref, gate_ref, w_ref, ys_ref, o_ref, buf0, buf1, sem0, sem1, *, tm, s):
    n = tm * TOP_K
    i = pl.program_id(0)

    def gather(idx_ref, tile, buf, sem):
        _start_row_gather(lambda r: idx_ref[0, 0, tile * n + r], ys_ref, buf, sem, n, s)

    def combine(tile, buf):
        rows = pl.ds(tile * tm, tm)
        gate = gate_ref[rows, :]
        half = o_ref.shape[1] // 2
        o_ref[rows, :] = x2_ref[rows, :]
        for k in range(TOP_K):
            g = gate[:, k:k + 1]
            for c in range(s):
                hi, lo = _unpack_load(buf, k * tm, tm, s, c)
                o_ref[rows, c * LANES:(c + 1) * LANES] += g * hi
                o_ref[rows, half + c * LANES:half + (c + 1) * LANES] += g * lo
        acc = o_ref[rows, :]
        o_ref[rows, :] = acc * lax.rsqrt(jnp.mean(acc * acc, axis=-1, keepdims=True) + EPS) * w_ref[...]

    @pl.when(i == 0)
    def _():
        gather(dest_ref, 0, buf0, sem0)

    gather(dest_ref, 1, buf1, sem1)
    _wait_row_gather(buf0, sem0)
    combine(0, buf0)
    gather(next_ref, 0, buf0, sem0)
    _wait_row_gather(buf1, sem1)
    combine(1, buf1)

    @pl.when(i == pl.num_programs(0) - 1)
    def _():
        _wait_row_gather(buf0, sem0)


def _final(x2, ys_packed, dest, gates_pad, w):
    T, D = x2.shape
    tm = min(FINAL_TM, T // 2)
    s = _packed_rows(D)
    n = tm * TOP_K
    steps = T // (2 * tm)
    dest_km = dest.reshape(steps, 2, tm, TOP_K).transpose(0, 1, 3, 2).reshape(steps, 1, 2 * n)
    buf = pltpu.VMEM((n * s, LANES), jnp.uint32)
    return pl.pallas_call(
        functools.partial(_final_body, tm=tm, s=s),
        grid=(steps,),
        in_specs=[
            pl.BlockSpec((1, 1, 2 * n), lambda i: (i, 0, 0), memory_space=pltpu.SMEM),
            pl.BlockSpec((1, 1, 2 * n), lambda i: (jnp.minimum(i + 1, steps - 1), 0, 0), memory_space=pltpu.SMEM),
            pl.BlockSpec((2 * tm, D), lambda i: (i, 0)),
            pl.BlockSpec((2 * tm, LANES), lambda i: (i, 0)),
            pl.BlockSpec((1, D), lambda i: (0, 0)),
            pl.BlockSpec(memory_space=pl.ANY),
        ],
        out_specs=pl.BlockSpec((2 * tm, D), lambda i: (i, 0)),
        out_shape=jax.ShapeDtypeStruct((T, D), F32),
        scratch_shapes=[buf, buf, pltpu.SemaphoreType.DMA, pltpu.SemaphoreType.DMA],
        compiler_params=_cparams(("arbitrary",)),
        name="final_norm",
    )(dest_km, dest_km, x2, gates_pad, w, ys_packed)


def _moe(h2_packed, T, D, top_idx, rank, counts, w_gate, b_gate, w_up, b_up, w_down, b_down):
    A = T * TOP_K
    tm = MOE_TM
    n_blocks = (A + N_EXPERTS * (tm - 1) + tm - 1) // tm
    n_slots = n_blocks * tm

    blocks_e = (counts + tm - 1) // tm
    bend = jnp.cumsum(blocks_e)
    bstart = bend - blocks_e
    experts = jnp.arange(N_EXPERTS, dtype=jnp.int32)
    first_slot = jnp.sum(jnp.where(top_idx[:, :, None] == experts, bstart * tm, 0), axis=-1)
    dest = (first_slot + rank).astype(jnp.int32).reshape(A)
    blk = jnp.arange(n_blocks, dtype=jnp.int32)
    blk_e = jnp.minimum(jnp.sum((bend[None, :] <= blk[:, None]).astype(jnp.int32), axis=1), N_EXPERTS - 1)
    zero_blk = ((blk >= bend[-1]) | (blk == bend[blk_e] - 1)).astype(jnp.int32)

    xs = _dispatch(dest, zero_blk, h2_packed, D, n_slots, tm)
    d_ff = w_gate.shape[2]
    act = _moe_up(_moe_schedule(blocks_e, -(-d_ff // MOE_TF), n_blocks), xs, w_gate, w_up, b_gate, b_up)
    split = tm // MOE_DOWN_TM
    ys = _moe_down(_moe_schedule(blocks_e * split, 1, n_blocks * split), act, w_down, b_down)
    return ys, dest


def kernel(x, ln1_w, w_in, hg_lb_logits, hg_norm_w, ml_conv_w, ml_conv_b, ml_igate_b, ml_fgate_b, ml_norm_w,
           w_out, ln2_w, w_router, b_router, w_gate, b_gate, w_up, b_up, w_down, b_down, final_norm_w):
    B, S, D = x.shape
    T = B * S
    depth = w_in.shape[0]
    hg_w = HG_HEADS * HG_DK
    n_main = 4 * hg_w + 2 * ML_HEADS * ML_DQK + 2 * ML_HEADS * ML_DV
    lb_all = jnp.cumsum(jax.nn.softmax(hg_lb_logits.astype(F32), axis=0), axis=0)

    xc = x.reshape(T, D)
    for l in range(depth):
        w_gates_pad = jnp.pad(w_in[l][:, n_main:], ((0, 0), (0, LANES - 2 * ML_HEADS))).astype(BF16)
        w_bf = w_in[l].astype(BF16)
        ln1 = ln1_w[l].reshape(1, D)
        hf_tiles = hg_w // INPROJ_TN
        proj_f = _inproj(xc, ln1, w_bf, hf_tiles, lambda j: j + hf_tiles, F32)[0]
        proj_b, gates = _inproj(xc, ln1, w_bf, n_main // INPROJ_TN - hf_tiles,
                                lambda j: jnp.where(j >= hf_tiles, j + hf_tiles, j), BF16, w_gates_pad)
        a_out = _hgrn(proj_b, proj_f, lb_all[l], hg_norm_w[l], B, S)
        gate_bias = jnp.pad(jnp.concatenate([ml_igate_b[l], ml_fgate_b[l]]), (0, LANES - 2 * ML_HEADS))
        b_out = _mlstm(proj_b, gates, gate_bias.reshape(1, LANES), ml_conv_w[l], ml_conv_b[l], ml_norm_w[l],
                       B, S, 3 * HG_HEADS)
        wr_pad = jnp.pad(w_router[l], ((0, 0), (0, LANES - N_EXPERTS)))
        wr_hi = wr_pad.astype(BF16)
        wr_lo = (wr_pad - wr_hi.astype(F32)).astype(BF16)
        br_pad = jnp.pad(b_router[l], (0, LANES - N_EXPERTS), constant_values=-1e30).reshape(1, LANES)
        x2, h2, idx_pad, gates_pad, rank_pad, cnt = _outproj(
            a_out, b_out, xc, w_out[l].astype(BF16), ln2_w[l].reshape(1, D), wr_hi, wr_lo, br_pad)
        ys, dest = _moe(h2, T, D, idx_pad[:, :TOP_K], rank_pad[:, :TOP_K], cnt[0, :N_EXPERTS],
                        w_gate[l], b_gate[l], w_up[l], b_up[l], w_down[l], b_down[l])
        if l + 1 < depth:
            raise NotImplementedError("only the final layer fuses the output norm")
        xc = _final(x2, ys, dest, gates_pad, final_norm_w.reshape(1, D))
    return xc.reshape(B, S, D)
```

```python
import functools

import jax
import jax.numpy as jnp
from jax import lax
from jax.experimental import pallas as pl
from jax.experimental.pallas import tpu as pltpu

F32 = jnp.float32
BF16 = jnp.bfloat16

EPS = 1e-6
HG_HEADS = 8
HG_DK = 128
ML_HEADS = 4
ML_DQK = 128
ML_DV = 256
CONV_W = 4
N_EXPERTS = 32
TOP_K = 4
SWIGLU_ALPHA = 1.702
SWIGLU_LIMIT = 7.0

LANES = 128
SUBLANES = 8
VMEM_LIMIT_BYTES = 56 * 1024 * 1024

HG_CHUNK = 64
HG_SUB = 16
ML_CHUNK = 128
MIX_ROWS = 512
HG_HEADS_PER_STEP = 2

INPROJ_TM = 1024
INPROJ_TN = 1024
OUTPROJ_TM = 256
OUTPROJ_SUB = 128
DISPATCH_TOKENS = 256
MOE_TM = 512
MOE_TF = 1024
MOE_SUB_ROWS = 256
MOE_DOWN_TM = 512
DMA_UNROLL = 32
DMA_QUEUES = 2
FINAL_TM = 128


def _dot_nt(a, b):
    return lax.dot_general(a, b, (((1,), (1,)), ((), ())), preferred_element_type=F32)


def _dot_tn(a, b):
    return lax.dot_general(a, b, (((0,), (0,)), ((), ())), preferred_element_type=F32)


def _log_sigmoid(z):
    return jnp.minimum(z, 0.0) - jnp.log1p(jnp.exp(-jnp.abs(z)))


def _sigmoid(z):
    return 1.0 / (1.0 + jnp.exp(-z))


def _cparams(semantics):
    return pltpu.CompilerParams(dimension_semantics=semantics, vmem_limit_bytes=VMEM_LIMIT_BYTES)


_HI_MASK = 0xFFFF0000


def _packed_rows(d):
    return d // (2 * LANES)


def _pack_store(o_ref, v):
    n, d = v.shape
    s, half = _packed_rows(d), d // 2
    bits = pltpu.bitcast(v.astype(BF16).astype(F32), jnp.uint32)
    for c in range(s):
        hi = bits[:, c * LANES:(c + 1) * LANES]
        lo = bits[:, half + c * LANES:half + (c + 1) * LANES]
        o_ref[pl.ds(c, n, stride=s), :] = hi | jnp.right_shift(lo, jnp.uint32(16))


def _unpack_load(buf, first_row, n, s, c):
    w = buf[pl.ds(first_row * s + c, n, stride=s), :]
    hi = pltpu.bitcast(w & jnp.uint32(_HI_MASK), F32)
    lo = pltpu.bitcast(jnp.left_shift(w, jnp.uint32(16)), F32)
    return hi, lo


def _inproj_body(*refs, tn, with_gates):
    if with_gates:
        x_ref, lnw_ref, w_ref, wg_ref, o_ref, g_ref, h_scr = refs
    else:
        x_ref, lnw_ref, w_ref, o_ref, h_scr = refs

    @pl.when(pl.program_id(1) == 0)
    def _():
        x = x_ref[...]
        h = x * lax.rsqrt(jnp.mean(x * x, axis=-1, keepdims=True) + EPS) * lnw_ref[...]
        hb = h.astype(BF16)
        h_scr[...] = hb
        if with_gates:
            g_ref[...] = jnp.dot(hb, wg_ref[...], preferred_element_type=F32)

    res = jnp.dot(h_scr[...], w_ref[...], preferred_element_type=F32)
    for c in range(tn // LANES):
        o_ref[c] = res[:, c * LANES:(c + 1) * LANES].astype(o_ref.dtype)


def _inproj(x2d, ln_w, w_bf, n_tiles, col_tile, out_dtype, w_gates_pad=None):
    T, D = x2d.shape
    tm = min(INPROJ_TM, T)
    tn = INPROJ_TN
    with_gates = w_gates_pad is not None
    in_specs = [
        pl.BlockSpec((tm, D), lambda i, j: (i, 0)),
        pl.BlockSpec((1, D), lambda i, j: (0, 0)),
        pl.BlockSpec((D, tn), lambda i, j: (0, col_tile(j))),
    ]
    out_specs = [pl.BlockSpec((tn // LANES, tm, LANES), lambda i, j: (j, i, 0))]
    out_shape = [jax.ShapeDtypeStruct((n_tiles * tn // LANES, T, LANES), out_dtype)]
    args = [x2d, ln_w, w_bf]
    if with_gates:
        in_specs.append(pl.BlockSpec((D, LANES), lambda i, j: (0, 0)))
        out_specs.append(pl.BlockSpec((tm, LANES), lambda i, j: (i, 0)))
        out_shape.append(jax.ShapeDtypeStruct((T, LANES), F32))
        args.append(w_gates_pad)
    return pl.pallas_call(
        functools.partial(_inproj_body, tn=tn, with_gates=with_gates),
        grid=(T // tm, n_tiles),
        in_specs=in_specs,
        out_specs=out_specs,
        out_shape=out_shape,
        scratch_shapes=[pltpu.VMEM((tm, D), BF16)],
        compiler_params=_cparams(("parallel", "arbitrary")),
        name="inproj_gates" if with_gates else "inproj",
    )(*args)


def _hgrn_body(q_ref, f_ref, i_ref, g_ref, lb_ref, nw_ref, o_ref, st_scr, *, rows, heads):
    C, SUB = HG_CHUNK, HG_SUB
    nsub = C // SUB

    @pl.when(pl.program_id(2) == 0)
    def _():
        st_scr[...] = jnp.zeros_like(st_scr)

    lbs = [lb_ref[hh] for hh in range(heads)]
    log_lbs = [jnp.log(lb) for lb in lbs]
    log_1mlbs = [jnp.log1p(-lb) for lb in lbs]

    r_i = lax.broadcasted_iota(jnp.int32, (C, C), 0)
    c_i = lax.broadcasted_iota(jnp.int32, (C, C), 1)
    tri = (c_i <= r_i).astype(F32)
    sub_shift = SUB.bit_length() - 1
    diag_mask = (c_i <= r_i) & (jnp.right_shift(r_i, sub_shift) == jnp.right_shift(c_i, sub_shift))
    row_id = lax.broadcasted_iota(jnp.int32, (C, HG_DK), 0)

    pairs = [(j, hh) for j in range(rows // C) for hh in range(heads)]
    tri_b = tri.astype(BF16)
    st = {p: {} for p in pairs}

    for p in pairs:
        j, hh = p
        z = f_ref[hh, pl.ds(j * C, C), :].astype(F32)
        e = jnp.exp(-jnp.abs(z))
        log_sig = jnp.minimum(z, 0.0) - jnp.log1p(e)
        sig_neg = jnp.where(z >= 0, e, 1.0) / (1.0 + e)
        cc = log_1mlbs[hh] + log_sig
        log_f = jnp.maximum(log_lbs[hh], cc) + jnp.log1p(jnp.exp(-jnp.abs(log_lbs[hh] - cc)))
        st[p]["kk"] = (1.0 - lbs[hh]) * sig_neg
        hi = log_f.astype(BF16)
        lo = (log_f - hi.astype(F32)).astype(BF16)
        st[p]["b"] = (jnp.dot(tri_b, hi, preferred_element_type=F32)
                      + jnp.dot(tri_b, lo, preferred_element_type=F32))

    for p in pairs:
        j, hh = p
        q = q_ref[hh, pl.ds(j * C, C), :].astype(F32)
        kk, b = st[p]["kk"], st[p]["b"]
        b_last = b[C - 1:C, :]
        st[p]["dec"] = jnp.exp(b_last)
        st[p]["qe"] = (q * jnp.exp(b)).astype(BF16)
        st[p]["kdec"] = (kk * jnp.exp(b_last - b)).astype(BF16)
        refs = [b[I * SUB:I * SUB + 1, :] for I in range(nsub)]
        refb = jnp.concatenate([jnp.broadcast_to(r, (SUB, HG_DK)) for r in refs], axis=0)
        qd = (q * jnp.exp(b - refb)).astype(BF16)
        kd = (kk * jnp.exp(refb - b)).astype(BF16)
        q_parts, k_parts = [], []
        for J in range(nsub - 1):
            r = refs[J + 1]
            qj = q * jnp.exp(jnp.minimum(b - r, 0.0))
            kj = kk * jnp.exp(jnp.minimum(r - b, 0.0))
            q_parts.append(jnp.where(row_id >= (J + 1) * SUB, qj, 0.0).astype(BF16))
            k_parts.append(jnp.where((row_id >= J * SUB) & (row_id < (J + 1) * SUB), kj, 0.0).astype(BF16))
        st[p]["att_d"] = _dot_nt(qd, kd)
        st[p]["att_o"] = _dot_nt(jnp.concatenate(q_parts, axis=1), jnp.concatenate(k_parts, axis=1))

    for p in pairs:
        j, hh = p
        vb = i_ref[hh, pl.ds(j * C, C), :].astype(BF16)
        att = jnp.where(diag_mask, st[p]["att_d"], 0.0) + st[p]["att_o"]
        st[p]["intra"] = jnp.dot(att.astype(BF16), vb, preferred_element_type=F32)
        st[p]["upd"] = _dot_tn(vb, st[p]["kdec"])

    for hh in range(heads):
        s = st_scr[hh]
        for j in range(rows // C):
            st[(j, hh)]["s_in"] = s.astype(BF16)
            s = s * st[(j, hh)]["dec"] + st[(j, hh)]["upd"]
        st_scr[hh] = s

    for p in pairs:
        j, hh = p
        g = g_ref[hh, pl.ds(j * C, C), :].astype(F32)
        o = _dot_nt(st[p]["qe"], st[p]["s_in"]) + st[p]["intra"]
        y = o * lax.rsqrt(jnp.mean(o * o, axis=-1, keepdims=True) + EPS) * nw_ref[hh]
        o_ref[pl.ds(j * C, C), hh * LANES:(hh + 1) * LANES] = (y * (g * _sigmoid(g))).astype(o_ref.dtype)


def _hgrn(proj_b, proj_f, lb, norm_w, batch, seq):
    T = batch * seq
    rows = min(MIX_ROWS, seq)
    nblk = seq // rows
    H = HG_HEADS
    hp = HG_HEADS_PER_STEP

    def slab(off):
        return pl.BlockSpec((hp, rows, LANES), lambda b, h, c, off=off: (off // hp + h, b * nblk + c, 0))

    vec = pl.BlockSpec((hp, 1, LANES), lambda b, h, c: (h, 0, 0))
    return pl.pallas_call(
        functools.partial(_hgrn_body, rows=rows, heads=hp),
        grid=(batch, H // hp, nblk),
        in_specs=[slab(0), slab(0), slab(H), slab(2 * H), vec, vec],
        out_specs=pl.BlockSpec((rows, hp * LANES), lambda b, h, c: (b * nblk + c, h)),
        out_shape=jax.ShapeDtypeStruct((T, H * LANES), BF16),
        scratch_shapes=[pltpu.VMEM((hp, LANES, HG_DK), F32)],
        compiler_params=_cparams(("parallel", "parallel", "arbitrary")),
        name="hgrn2",
    )(proj_b, proj_f, proj_b, proj_b, lb.reshape(H, 1, HG_DK), norm_w.reshape(H, 1, LANES))


def _mlstm_body(q_ref, k_ref, v_ref, og_ref, gt_ref, gb_ref, cwq_ref, cwk_ref, cbq_ref, cbk_ref, nw_ref,
                out_ref, c_scr, n_scr, m_scr, qx_scr, kx_scr, qc_scr, kc_scr, *, rows):
    C = ML_CHUNK
    PAD = SUBLANES
    h = pl.program_id(1)

    @pl.when(pl.program_id(2) == 0)
    def _():
        c_scr[...] = jnp.zeros_like(c_scr)
        n_scr[...] = jnp.zeros_like(n_scr)
        m_scr[...] = jnp.zeros_like(m_scr)
        qx_scr[0:PAD, :] = jnp.zeros((PAD, LANES), F32)
        kx_scr[0:PAD, :] = jnp.zeros((PAD, LANES), F32)

    qx_scr[PAD:PAD + rows, :] = q_ref[0].astype(F32)
    kx_scr[PAD:PAD + rows, :] = k_ref[0].astype(F32)
    accq = jnp.zeros((rows, LANES), F32) + cbq_ref[...]
    acck = jnp.zeros((rows, LANES), F32) + cbk_ref[...]
    for j in range(CONV_W):
        off = PAD - (CONV_W - 1) + j
        accq = accq + cwq_ref[j:j + 1, :] * qx_scr[pl.ds(off, rows), :]
        acck = acck + cwk_ref[j:j + 1, :] * kx_scr[pl.ds(off, rows), :]
    qc_scr[...] = accq * _sigmoid(accq) * (ML_DQK ** -0.5)
    kc_scr[...] = acck * _sigmoid(acck)
    qx_scr[0:PAD, :] = qx_scr[rows:rows + PAD, :]
    kx_scr[0:PAD, :] = kx_scr[rows:rows + PAD, :]

    lane = lax.broadcasted_iota(jnp.int32, (C, C), 1)
    sub = lax.broadcasted_iota(jnp.int32, (C, C), 0)
    causal = lane <= sub
    lower = causal.astype(F32)
    upper = (sub <= lane).astype(F32)
    nw = nw_ref[...]
    gbias = gb_ref[...]

    chunks = range(rows // C)
    lower_b, upper_b = lower.astype(BF16), upper.astype(BF16)
    st = [{} for _ in chunks]

    def split(x):
        hi = x.astype(BF16)
        return hi, (x - hi.astype(F32)).astype(BF16)

    for j in chunks:
        gc = gt_ref[pl.ds(j * C, C), :] + gbias
        gct = gc.T
        li_col = jnp.sum(jnp.where(lane == h, gc, 0.0), axis=1, keepdims=True)
        fg_col = jnp.sum(jnp.where(lane == h + ML_HEADS, gc, 0.0), axis=1, keepdims=True)
        li_row = jnp.sum(jnp.where(sub == h, gct, 0.0), axis=0, keepdims=True)
        fg_row = jnp.sum(jnp.where(sub == h + ML_HEADS, gct, 0.0), axis=0, keepdims=True)
        ch, cl = split(_log_sigmoid(fg_col))
        rh, rl = split(_log_sigmoid(fg_row))
        g_t = (jnp.dot(lower_b, jnp.broadcast_to(ch, (C, C)), preferred_element_type=F32)
               + jnp.dot(lower_b, jnp.broadcast_to(cl, (C, C)), preferred_element_type=F32))
        g_s = (jnp.dot(jnp.broadcast_to(rh, (C, C)), upper_b, preferred_element_type=F32)
               + jnp.dot(jnp.broadcast_to(rl, (C, C)), upper_b, preferred_element_type=F32))
        st[j].update(g_t=g_t, g_s=g_s, li_col=li_col, li_row=li_row)

    for j in chunks:
        c = st[j]
        g_col = c["g_t"][:, 0:1]
        g_last = c["g_t"][C - 1:C, 0:1]
        dmat = jnp.where(causal, c["g_t"] - c["g_s"] + c["li_row"], -jnp.inf)
        log_ws = g_last - g_col + c["li_col"]
        qf = qc_scr[pl.ds(j * C, C), :]
        kf = kc_scr[pl.ds(j * C, C), :]
        qb = qf.astype(BF16)
        c.update(g_col=g_col, g_last=g_last, dmat=dmat, dmax=jnp.max(dmat, axis=1, keepdims=True),
                 log_ws=log_ws, ws_max=jnp.max(log_ws, axis=0, keepdims=True), qf=qf, kf=kf, qb=qb,
                 qk=_dot_nt(qb, kf.astype(BF16)))

    m = m_scr[:, 0:1]
    for j in chunks:
        c = st[j]
        m_new = jnp.maximum(c["g_last"] + m, c["ws_max"])
        c.update(m_prev=m, m_new=m_new, decay=jnp.exp(c["g_last"] + m - m_new))
        m = m_new
    m_scr[...] = jnp.broadcast_to(m, m_scr.shape)

    for j in chunks:
        c = st[j]
        vb = jnp.concatenate([v_ref[0, pl.ds(j * C, C), :], v_ref[1, pl.ds(j * C, C), :]], axis=1).astype(BF16)
        kw = c["kf"] * jnp.exp(c["log_ws"] - c["m_new"])
        c.update(vb=vb, upd=_dot_tn(kw.astype(BF16), vb), ksum=jnp.sum(kw, axis=0, keepdims=True))

    cm = c_scr[...]
    nv = n_scr[...]
    for j in chunks:
        c = st[j]
        c.update(c_in=cm.astype(BF16), n_in=nv)
        cm = c["decay"] * cm + c["upd"]
        nv = c["decay"] * nv + c["ksum"]
    c_scr[...] = cm
    n_scr[...] = nv

    for j in chunks:
        c = st[j]
        a_inter = c["g_col"] + c["m_prev"]
        m_t = jnp.maximum(a_inter, c["dmax"])
        w_inter = jnp.exp(a_inter - m_t)
        sqk = c["qk"] * jnp.exp(c["dmat"] - m_t)
        num = (w_inter * jnp.dot(c["qb"], c["c_in"], preferred_element_type=F32)
               + jnp.dot(sqk.astype(BF16), c["vb"], preferred_element_type=F32))
        den = (w_inter * jnp.sum(c["qf"] * c["n_in"], axis=1, keepdims=True)
               + jnp.sum(sqk, axis=1, keepdims=True))
        hh = num * (1.0 / jnp.maximum(jnp.abs(den), jnp.exp(-m_t)))
        y = hh * lax.rsqrt(jnp.mean(hh * hh, axis=-1, keepdims=True) + EPS) * nw
        og = jnp.concatenate([og_ref[0, pl.ds(j * C, C), :], og_ref[1, pl.ds(j * C, C), :]], axis=1).astype(F32)
        out_ref[pl.ds(j * C, C), :] = (y * _sigmoid(og)).astype(out_ref.dtype)


def _mlstm(proj3, gates, gate_bias_pad, conv_w, conv_b, norm_w, batch, seq, q_off):
    T = batch * seq
    rows = min(MIX_ROWS, seq)
    nblk = seq // rows
    H = ML_HEADS
    k_off = q_off + H
    v_off = k_off + H
    o_off = v_off + 2 * H

    def slab(off):
        return pl.BlockSpec((1, rows, LANES), lambda b, h, c, off=off: (off + h, b * nblk + c, 0))

    def slab2(off):
        return pl.BlockSpec((2, rows, LANES), lambda b, h, c, off=off: (off // 2 + h, b * nblk + c, 0))

    qk_w = H * ML_DQK
    return pl.pallas_call(
        functools.partial(_mlstm_body, rows=rows),
        grid=(batch, H, nblk),
        in_specs=[
            slab(q_off), slab(k_off), slab2(v_off), slab2(o_off),
            pl.BlockSpec((rows, LANES), lambda b, h, c: (b * nblk + c, 0)),
            pl.BlockSpec((1, LANES), lambda b, h, c: (0, 0)),
            pl.BlockSpec((CONV_W, LANES), lambda b, h, c: (0, h)),
            pl.BlockSpec((CONV_W, LANES), lambda b, h, c: (0, H + h)),
            pl.BlockSpec((1, LANES), lambda b, h, c: (0, h)),
            pl.BlockSpec((1, LANES), lambda b, h, c: (0, H + h)),
            pl.BlockSpec((1, ML_DV), lambda b, h, c: (0, h)),
        ],
        out_specs=pl.BlockSpec((rows, ML_DV), lambda b, h, c: (b * nblk + c, h)),
        out_shape=jax.ShapeDtypeStruct((T, H * ML_DV), BF16),
        scratch_shapes=[
            pltpu.VMEM((ML_DQK, ML_DV), F32),
            pltpu.VMEM((1, ML_DQK), F32),
            pltpu.VMEM((1, LANES), F32),
            pltpu.VMEM((rows + 2 * SUBLANES, LANES), F32),
            pltpu.VMEM((rows + 2 * SUBLANES, LANES), F32),
            pltpu.VMEM((rows, LANES), F32),
            pltpu.VMEM((rows, LANES), F32),
        ],
        compiler_params=_cparams(("parallel", "parallel", "arbitrary")),
        name="mlstm",
    )(proj3, proj3, proj3, proj3, gates, gate_bias_pad, conv_w, conv_w,
      conv_b.reshape(1, 2 * qk_w), conv_b.reshape(1, 2 * qk_w), norm_w.reshape(1, H * ML_DV))


def _outproj_body(a_ref, b_ref, x_ref, wo_ref, ln_ref, wrh_ref, wrl_ref, br_ref,
                  x2_ref, h2_ref, idx_ref, gate_ref, rank_ref, cnt_ref, cnt_scr, *, sub_rows):
    tm = x_ref.shape[0]
    ka = a_ref.shape[1]
    s = _packed_rows(x_ref.shape[1])

    @pl.when(pl.program_id(0) == 0)
    def _():
        cnt_scr[...] = jnp.zeros_like(cnt_scr)

    lane = lax.broadcasted_iota(jnp.int32, (sub_rows, LANES), 1).astype(F32)
    onehots = [[] for _ in range(TOP_K)]
    for r0 in range(0, tm, sub_rows):
        rows = pl.ds(r0, sub_rows)
        res = (jnp.dot(a_ref[rows, :], wo_ref[0:ka, :], preferred_element_type=F32)
               + jnp.dot(b_ref[rows, :], wo_ref[ka:, :], preferred_element_type=F32))
        x2 = x_ref[rows, :] + res
        x2_ref[rows, :] = x2
        h2 = x2 * lax.rsqrt(jnp.mean(x2 * x2, axis=-1, keepdims=True) + EPS) * ln_ref[...]
        _pack_store(h2_ref.at[pl.ds(r0 * s, sub_rows * s), :], h2)

        h_hi = h2.astype(BF16)
        h_lo = (h2 - h_hi.astype(F32)).astype(BF16)
        logits = (jnp.dot(h_hi, wrh_ref[...], preferred_element_type=F32)
                  + jnp.dot(h_lo, wrh_ref[...], preferred_element_type=F32)
                  + jnp.dot(h_hi, wrl_ref[...], preferred_element_type=F32)) + br_ref[...]
        vals, idxs = [], []
        cur = logits
        for _ in range(TOP_K):
            m = jnp.max(cur, axis=1, keepdims=True)
            ix = jnp.min(jnp.where(cur == m, lane, float(LANES)), axis=1, keepdims=True)
            vals.append(m)
            idxs.append(ix)
            cur = jnp.where(lane == ix, -jnp.inf, cur)
        es = [jnp.exp(v - vals[0]) for v in vals]
        inv = 1.0 / (es[0] + es[1] + es[2] + es[3])
        gate = jnp.zeros(logits.shape, F32)
        idx = jnp.zeros(logits.shape, F32)
        for k in range(TOP_K):
            gate = jnp.where(lane == float(k), es[k] * inv, gate)
            idx = jnp.where(lane == float(k), idxs[k], idx)
            onehots[k].append((lane == idxs[k]).astype(F32))
        gate_ref[rows, :] = gate
        idx_ref[rows, :] = idx.astype(jnp.int32)

    oh_k = [jnp.concatenate(o, axis=0) for o in onehots]
    oh = oh_k[0] + oh_k[1] + oh_k[2] + oh_k[3]
    r_i = lax.broadcasted_iota(jnp.int32, (tm, tm), 0)
    c_i = lax.broadcasted_iota(jnp.int32, (tm, tm), 1)
    before = jnp.dot((c_i < r_i).astype(BF16), oh.astype(BF16), preferred_element_type=F32) + cnt_scr[...]
    lane_t = lax.broadcasted_iota(jnp.int32, (tm, LANES), 1)
    rank = jnp.zeros((tm, LANES), F32)
    for k in range(TOP_K):
        rank = jnp.where(lane_t == k, jnp.sum(oh_k[k] * before, axis=1, keepdims=True), rank)
    rank_ref[...] = rank.astype(jnp.int32)
    cnt = cnt_scr[...] + jnp.sum(oh, axis=0, keepdims=True)
    cnt_scr[...] = cnt
    cnt_ref[...] = cnt.astype(jnp.int32)


def _outproj(a_out, b_out, x2d, w_out_bf, ln_w, wr_hi, wr_lo, b_router_pad):
    T, D = x2d.shape
    tm = min(OUTPROJ_TM, T)
    ka, kb = a_out.shape[1], b_out.shape[1]
    s = _packed_rows(D)
    row = lambda w: pl.BlockSpec((tm, w), lambda i: (i, 0))
    full = lambda r, c: pl.BlockSpec((r, c), lambda i: (0, 0))
    return pl.pallas_call(
        functools.partial(_outproj_body, sub_rows=min(OUTPROJ_SUB, tm)),
        grid=(T // tm,),
        in_specs=[row(ka), row(kb), row(D), full(ka + kb, D), full(1, D), full(D, LANES), full(D, LANES),
                  full(1, LANES)],
        out_specs=[row(D), pl.BlockSpec((tm * s, LANES), lambda i: (i, 0)), row(LANES), row(LANES), row(LANES),
                   full(1, LANES)],
        out_shape=[
            jax.ShapeDtypeStruct((T, D), F32),
            jax.ShapeDtypeStruct((T * s, LANES), jnp.uint32),
            jax.ShapeDtypeStruct((T, LANES), jnp.int32),
            jax.ShapeDtypeStruct((T, LANES), F32),
            jax.ShapeDtypeStruct((T, LANES), jnp.int32),
            jax.ShapeDtypeStruct((1, LANES), jnp.int32),
        ],
        scratch_shapes=[pltpu.VMEM((1, LANES), F32)],
        compiler_params=_cparams(("arbitrary",)),
        name="outproj_router",
    )(a_out, b_out, x2d, w_out_bf, ln_w, wr_hi, wr_lo, b_router_pad)


def _start_row_gather(idx_at, src_ref, buf, sem, n, s):
    def start(pair, carry):
        for p in range(DMA_QUEUES):
            i = pair * DMA_QUEUES + p
            src_row = pl.multiple_of(idx_at(i) * s, s)
            dst_row = pl.multiple_of(i * s, s)
            pltpu.make_async_copy(src_ref.at[pl.ds(src_row, s), :], buf.at[pl.ds(dst_row, s), :],
                                  sem).start(priority=p)
        return carry

    lax.fori_loop(0, n // DMA_QUEUES, start, 0, unroll=DMA_UNROLL // DMA_QUEUES)


def _wait_row_gather(buf, sem):
    pltpu.make_async_copy(buf, buf, sem).wait()


def _dispatch_body(zero_ref, dest_ref, h_ref, xs_ref, zbuf, sem, zsem, *, ntok, s, tm, n_blocks):
    blk = tm * s

    def zero_copy(b):
        return pltpu.make_async_copy(zbuf, xs_ref.at[pl.ds(pl.multiple_of(b * blk, blk), blk), :], zsem)

    @pl.when(pl.program_id(0) == 0)
    def _():
        zbuf[...] = jnp.zeros_like(zbuf)

        def zstart(b, carry):
            @pl.when(zero_ref[b] == 1)
            def _():
                zero_copy(b).start()
            return carry

        def zwait(b, carry):
            @pl.when(zero_ref[b] == 1)
            def _():
                zero_copy(b).wait()
            return carry

        lax.fori_loop(0, n_blocks, zstart, 0)
        lax.fori_loop(0, n_blocks, zwait, 0)

    def start(t, carry):
        src = h_ref.at[pl.ds(pl.multiple_of(t * s, s), s), :]
        for k in range(TOP_K):
            dst_row = pl.multiple_of(dest_ref[0, 0, t * TOP_K + k] * s, s)
            pltpu.make_async_copy(src, xs_ref.at[pl.ds(dst_row, s), :], sem).start(priority=k % DMA_QUEUES)
        return carry

    lax.fori_loop(0, ntok, start, 0, unroll=DMA_UNROLL // TOP_K)
    for _ in range(TOP_K):
        pltpu.make_async_copy(h_ref, h_ref, sem).wait()


def _dispatch(dest, zero_blk, h_packed, d, n_slots, tm):
    s = _packed_rows(d)
    T = h_packed.shape[0] // s
    ntok = min(DISPATCH_TOKENS, T)
    n = ntok * TOP_K
    grid_spec = pltpu.PrefetchScalarGridSpec(
        num_scalar_prefetch=1,
        grid=(T // ntok,),
        in_specs=[
            pl.BlockSpec((1, 1, n), lambda i, z: (i, 0, 0), memory_space=pltpu.SMEM),
            pl.BlockSpec((ntok * s, LANES), lambda i, z: (i, 0)),
        ],
        out_specs=pl.BlockSpec(memory_space=pl.ANY),
        scratch_shapes=[pltpu.VMEM((tm * s, LANES), jnp.uint32), pltpu.SemaphoreType.DMA,
                        pltpu.SemaphoreType.DMA],
    )
    return pl.pallas_call(
        functools.partial(_dispatch_body, ntok=ntok, s=s, tm=tm, n_blocks=zero_blk.shape[0]),
        grid_spec=grid_spec,
        out_shape=jax.ShapeDtypeStruct((n_slots * s, LANES), jnp.uint32),
        compiler_params=_cparams(("arbitrary",)),
        name="moe_dispatch",
    )(zero_blk, dest.reshape(T // ntok, 1, n), h_packed)


def _staged_weights(sched_refs, hbm_refs, stage_refs, bf_refs, sem):
    e_ref, wt_ref, first_ref, ne_ref, nt_ref, more_ref = sched_refs
    w = pl.program_id(0)
    tn = stage_refs[0].shape[1]

    def copies(e, t):
        col = pl.multiple_of(t * tn, tn)
        return [pltpu.make_async_copy(h.at[e, :, pl.ds(col, tn)], st, sem.at[k])
                for k, (h, st) in enumerate(zip(hbm_refs, stage_refs))]

    @pl.when(w == 0)
    def _():
        for c in copies(e_ref[0], wt_ref[0]):
            c.start()

    @pl.when(first_ref[w] == 1)
    def _():
        for c in copies(e_ref[w], wt_ref[w]):
            c.wait()
        for st, bf in zip(stage_refs, bf_refs):
            for r in range(0, st.shape[0], 256):
                bf[r:r + 256, :] = st[r:r + 256, :].astype(BF16)

        @pl.when(more_ref[w] == 1)
        def _():
            for c in copies(ne_ref[w], nt_ref[w]):
                c.start()


def _moe_up_body(e_ref, wt_ref, r_ref, ot_ref, first_ref, valid_ref, ne_ref, nt_ref, more_ref,
                 x_ref, wg_ref, wu_ref, bg_ref, bu_ref, o_ref, wg_stage, wu_stage, wg_scr, wu_scr, x_scr, sem):
    w = pl.program_id(0)
    _staged_weights((e_ref, wt_ref, first_ref, ne_ref, nt_ref, more_ref), (wg_ref, wu_ref),
                    (wg_stage, wu_stage), (wg_scr, wu_scr), sem)

    @pl.when(valid_ref[w] == 1)
    def _():
        tm, d = x_scr.shape
        s = _packed_rows(d)
        sub = min(MOE_SUB_ROWS, tm)
        for r0 in range(0, tm, sub):
            rows = pl.ds(r0, sub)
            for c in range(s):
                hi, lo = _unpack_load(x_ref, r0, sub, s, c)
                x_scr[rows, c * LANES:(c + 1) * LANES] = hi.astype(BF16)
                x_scr[rows, d // 2 + c * LANES:d // 2 + (c + 1) * LANES] = lo.astype(BF16)
            x = x_scr[rows, :]
            gt = jnp.dot(x, wg_scr[...], preferred_element_type=F32) + bg_ref[0]
            up = jnp.dot(x, wu_scr[...], preferred_element_type=F32) + bu_ref[0]
            gt = jnp.minimum(gt, SWIGLU_LIMIT)
            up = jnp.clip(up, -SWIGLU_LIMIT, SWIGLU_LIMIT)
            o_ref[rows, :] = ((up + 1.0) * (gt * _sigmoid(SWIGLU_ALPHA * gt))).astype(o_ref.dtype)

    @pl.when(valid_ref[w] == 0)
    def _():
        o_ref[...] = jnp.zeros_like(o_ref)


def _moe_down_body(e_ref, wt_ref, r_ref, ot_ref, first_ref, valid_ref, ne_ref, nt_ref, more_ref,
                   a_ref, wd_ref, bd_ref, o_ref, wd_stage, wd_scr, sem):
    w = pl.program_id(0)
    _staged_weights((e_ref, wt_ref, first_ref, ne_ref, nt_ref, more_ref), (wd_ref,), (wd_stage,), (wd_scr,), sem)

    @pl.when(valid_ref[w] == 1)
    def _():
        tm = a_ref.shape[0]
        s = o_ref.shape[0] // tm
        sub = min(MOE_SUB_ROWS, tm)
        for r0 in range(0, tm, sub):
            y = jnp.dot(a_ref[pl.ds(r0, sub), :], wd_scr[...], preferred_element_type=F32) + bd_ref[0]
            _pack_store(o_ref.at[pl.ds(r0 * s, sub * s), :], y)

    @pl.when(valid_ref[w] == 0)
    def _():
        o_ref[...] = jnp.zeros_like(o_ref)


def _moe_schedule(blocks_e, n_tiles, n_blocks):
    n_items = n_tiles * n_blocks
    bstart = jnp.cumsum(blocks_e) - blocks_e
    item_end = n_tiles * jnp.cumsum(blocks_e)
    total = item_end[-1]
    w = jnp.arange(n_items, dtype=jnp.int32)
    valid = w < total
    wc = jnp.minimum(w, jnp.maximum(total - 1, 0))
    e = jnp.minimum(jnp.sum((item_end[None, :] <= wc[:, None]).astype(jnp.int32), axis=1), N_EXPERTS - 1)
    local = wc - n_tiles * bstart[e]
    nb = jnp.maximum(blocks_e[e], 1)
    wtile = local // nb
    jblk = local % nb
    spare = w - total
    rblk = jnp.where(valid, bstart[e] + jblk, total // n_tiles + spare // n_tiles)
    otile = jnp.where(valid, wtile, spare % n_tiles)
    first = (jblk == 0) & valid
    nxt = jnp.minimum(w + nb, n_items - 1)
    more = first & (w + nb < total)
    i32 = lambda a: a.astype(jnp.int32)
    return (e, i32(wtile), i32(rblk), i32(otile), i32(first), i32(valid), e[nxt], i32(wtile[nxt]), i32(more))


def _moe_up(sched, xs_packed, w_gate, w_up, b_gate, b_up):
    D, d_ff = w_gate.shape[1], w_gate.shape[2]
    s = _packed_rows(D)
    n_slots = xs_packed.shape[0] // s
    tm, tf = MOE_TM, min(MOE_TF, d_ff)
    n_items = sched[0].shape[0]
    wspec = pl.BlockSpec(memory_space=pl.ANY)
    bspec = pl.BlockSpec((1, 1, tf), lambda w, e, wt, r, ot, *_: (e[w], 0, wt[w]))
    grid_spec = pltpu.PrefetchScalarGridSpec(
        num_scalar_prefetch=len(sched),
        grid=(n_items,),
        in_specs=[pl.BlockSpec((tm * s, LANES), lambda w, e, wt, r, ot, *_: (r[w], 0)),
                  wspec, wspec, bspec, bspec],
        out_specs=pl.BlockSpec((tm, tf), lambda w, e, wt, r, ot, *_: (r[w], ot[w])),
        scratch_shapes=[pltpu.VMEM((D, tf), F32), pltpu.VMEM((D, tf), F32),
                        pltpu.VMEM((D, tf), BF16), pltpu.VMEM((D, tf), BF16), pltpu.VMEM((tm, D), BF16),
                        pltpu.SemaphoreType.DMA((2,))],
    )
    return pl.pallas_call(
        _moe_up_body,
        grid_spec=grid_spec,
        out_shape=jax.ShapeDtypeStruct((n_slots, d_ff), BF16),
        compiler_params=_cparams(("arbitrary",)),
        name="moe_up",
    )(*sched, xs_packed, w_gate, w_up, b_gate.reshape(N_EXPERTS, 1, d_ff), b_up.reshape(N_EXPERTS, 1, d_ff))


def _moe_down(sched, act, w_down, b_down):
    n_slots, d_ff = act.shape
    D = w_down.shape[2]
    tm = MOE_DOWN_TM
    s = _packed_rows(D)
    n_items = sched[0].shape[0]
    grid_spec = pltpu.PrefetchScalarGridSpec(
        num_scalar_prefetch=len(sched),
        grid=(n_items,),
        in_specs=[
            pl.BlockSpec((tm, d_ff), lambda w, e, wt, r, ot, *_: (r[w], 0)),
            pl.BlockSpec(memory_space=pl.ANY),
            pl.BlockSpec((1, 1, D), lambda w, e, wt, r, ot, *_: (e[w], 0, 0)),
        ],
        out_specs=pl.BlockSpec((tm * s, LANES), lambda w, e, wt, r, ot, *_: (r[w], 0)),
        scratch_shapes=[pltpu.VMEM((d_ff, D), F32), pltpu.VMEM((d_ff, D), BF16), pltpu.SemaphoreType.DMA((1,))],
    )
    return pl.pallas_call(
        _moe_down_body,
        grid_spec=grid_spec,
        out_shape=jax.ShapeDtypeStruct((n_slots * s, LANES), jnp.uint32),
        compiler_params=_cparams(("arbitrary",)),
        name="moe_down",
    )(*sched, act, w_down, b_down.reshape(N_EXPERTS, 1, D))


def _final_body(dest_ref, next_ref, x2_ref, gate_ref, w_ref, ys_ref, o_ref, buf0, buf1, sem0, sem1, *, tm, s):
    n = tm * TOP_K
    i = pl.program_id(0)

    def gather(idx_ref, tile, buf, sem):
        _start_row_gather(lambda r: idx_ref[0, 0, tile * n + r], ys_ref, buf, sem, n, s)

    def combine(tile, buf):
        rows = pl.ds(tile * tm, tm)
        gate = gate_ref[rows, :]
        half = o_ref.shape[1] // 2
        o_ref[rows, :] = x2_ref[rows, :]
        for k in range(TOP_K):
            g = gate[:, k:k + 1]
            for c in range(s):
                hi, lo = _unpack_load(buf, k * tm, tm, s, c)
                o_ref[rows, c * LANES:(c + 1) * LANES] += g * hi
                o_ref[rows, half + c * LANES:half + (c + 1) * LANES] += g * lo
        acc = o_ref[rows, :]
        o_ref[rows, :] = acc * lax.rsqrt(jnp.mean(acc * acc, axis=-1, keepdims=True) + EPS) * w_ref[...]

    @pl.when(i == 0)
    def _():
        gather(dest_ref, 0, buf0, sem0)

    gather(dest_ref, 1, buf1, sem1)
    _wait_row_gather(buf0, sem0)
    combine(0, buf0)
    gather(next_ref, 0, buf0, sem0)
    _wait_row_gather(buf1, sem1)
    combine(1, buf1)

    @pl.when(i == pl.num_programs(0) - 1)
    def _():
        _wait_row_gather(buf0, sem0)


def _final(x2, ys_packed, dest, gates_pad, w):
    T, D = x2.shape
    tm = min(FINAL_TM, T // 2)
    s = _packed_rows(D)
    n = tm * TOP_K
    steps = T // (2 * tm)
    dest_km = dest.reshape(steps, 2, tm, TOP_K).transpose(0, 1, 3, 2).reshape(steps, 1, 2 * n)
    buf = pltpu.VMEM((n * s, LANES), jnp.uint32)
    return pl.pallas_call(
        functools.partial(_final_body, tm=tm, s=s),
        grid=(steps,),
        in_specs=[
            pl.BlockSpec((1, 1, 2 * n), lambda i: (i, 0, 0), memory_space=pltpu.SMEM),
            pl.BlockSpec((1, 1, 2 * n), lambda i: (jnp.minimum(i + 1, steps - 1), 0, 0), memory_space=pltpu.SMEM),
            pl.BlockSpec((2 * tm, D), lambda i: (i, 0)),
            pl.BlockSpec((2 * tm, LANES), lambda i: (i, 0)),
            pl.BlockSpec((1, D), lambda i: (0, 0)),
            pl.BlockSpec(memory_space=pl.ANY),
        ],
        out_specs=pl.BlockSpec((2 * tm, D), lambda i: (i, 0)),
        out_shape=jax.ShapeDtypeStruct((T, D), F32),
        scratch_shapes=[buf, buf, pltpu.SemaphoreType.DMA, pltpu.SemaphoreType.DMA],
        compiler_params=_cparams(("arbitrary",)),
        name="final_norm",
    )(dest_km, dest_km, x2, gates_pad, w, ys_packed)


def _moe(h2_packed, T, D, top_idx, rank, counts, w_gate, b_gate, w_up, b_up, w_down, b_down):
    A = T * TOP_K
    tm = MOE_TM
    n_blocks = (A + N_EXPERTS * (tm - 1) + tm - 1) // tm
    n_slots = n_blocks * tm

    blocks_e = (counts + tm - 1) // tm
    bend = jnp.cumsum(blocks_e)
    bstart = bend - blocks_e
    experts = jnp.arange(N_EXPERTS, dtype=jnp.int32)
    first_slot = jnp.sum(jnp.where(top_idx[:, :, None] == experts, bstart * tm, 0), axis=-1)
    dest = (first_slot + rank).astype(jnp.int32).reshape(A)
    blk = jnp.arange(n_blocks, dtype=jnp.int32)
    blk_e = jnp.minimum(jnp.sum((bend[None, :] <= blk[:, None]).astype(jnp.int32), axis=1), N_EXPERTS - 1)
    zero_blk = ((blk >= bend[-1]) | (blk == bend[blk_e] - 1)).astype(jnp.int32)

    xs = _dispatch(dest, zero_blk, h2_packed, D, n_slots, tm)
    d_ff = w_gate.shape[2]
    act = _moe_up(_moe_schedule(blocks_e, -(-d_ff // MOE_TF), n_blocks), xs, w_gate, w_up, b_gate, b_up)
    split = tm // MOE_DOWN_TM
    ys = _moe_down(_moe_schedule(blocks_e * split, 1, n_blocks * split), act, w_down, b_down)
    return ys, dest


def kernel(x, ln1_w, w_in, hg_lb_logits, hg_norm_w, ml_conv_w, ml_conv_b, ml_igate_b, ml_fgate_b, ml_norm_w,
           w_out, ln2_w, w_router, b_router, w_gate, b_gate, w_up, b_up, w_down, b_down, final_norm_w):
    B, S, D = x.shape
    T = B * S
    depth = w_in.shape[0]
    hg_w = HG_HEADS * HG_DK
    n_main = 4 * hg_w + 2 * ML_HEADS * ML_DQK + 2 * ML_HEADS * ML_DV
    lb_all = jnp.cumsum(jax.nn.softmax(hg_lb_logits.astype(F32), axis=0), axis=0)

    xc = x.reshape(T, D)
    for l in range(depth):
        w_gates_pad = jnp.pad(w_in[l][:, n_main:], ((0, 0), (0, LANES - 2 * ML_HEADS))).astype(BF16)
        w_bf = w_in[l].astype(BF16)
        ln1 = ln1_w[l].reshape(1, D)
        hf_tiles = hg_w // INPROJ_TN
        proj_f = _inproj(xc, ln1, w_bf, hf_tiles, lambda j: j + hf_tiles, F32)[0]
        proj_b, gates = _inproj(xc, ln1, w_bf, n_main // INPROJ_TN - hf_tiles,
                                lambda j: jnp.where(j >= hf_tiles, j + hf_tiles, j), BF16, w_gates_pad)
        a_out = _hgrn(proj_b, proj_f, lb_all[l], hg_norm_w[l], B, S)
        gate_bias = jnp.pad(jnp.concatenate([ml_igate_b[l], ml_fgate_b[l]]), (0, LANES - 2 * ML_HEADS))
        b_out = _mlstm(proj_b, gates, gate_bias.reshape(1, LANES), ml_conv_w[l], ml_conv_b[l], ml_norm_w[l],
                       B, S, 3 * HG_HEADS)
        wr_pad = jnp.pad(w_router[l], ((0, 0), (0, LANES - N_EXPERTS)))
        wr_hi = wr_pad.astype(BF16)
        wr_lo = (wr_pad - wr_hi.astype(F32)).astype(BF16)
        br_pad = jnp.pad(b_router[l], (0, LANES - N_EXPERTS), constant_values=-1e30).reshape(1, LANES)
        x2, h2, idx_pad, gates_pad, rank_pad, cnt = _outproj(
            a_out, b_out, xc, w_out[l].astype(BF16), ln2_w[l].reshape(1, D), wr_hi, wr_lo, br_pad)
        ys, dest = _moe(h2, T, D, idx_pad[:, :TOP_K], rank_pad[:, :TOP_K], cnt[0, :N_EXPERTS],
                        w_gate[l], b_gate[l], w_up[l], b_up[l], w_down[l], b_down[l])
        if l + 1 < depth:
            raise NotImplementedError("only the final layer fuses the output norm")
        xc = _final(x2, ys, dest, gates_pad, final_norm_w.reshape(1, D))
    return xc.reshape(B, S, D)
```

```python
import functools

import jax
import jax.numpy as jnp
from jax import lax
from jax.experimental import pallas as pl
from jax.experimental.pallas import tpu as pltpu

F32 = jnp.float32
BF16 = jnp.bfloat16

EPS = 1e-6
HG_HEADS = 8
HG_DK = 128
ML_HEADS = 4
ML_DQK = 128
ML_DV = 256
CONV_W = 4
N_EXPERTS = 32
TOP_K = 4
SWIGLU_ALPHA = 1.702
SWIGLU_LIMIT = 7.0

LANES = 128
SUBLANES = 8
VMEM_LIMIT_BYTES = 56 * 1024 * 1024

HG_CHUNK = 64
HG_SUB = 16
ML_CHUNK = 128
MIX_ROWS = 512
HG_HEADS_PER_STEP = 2

INPROJ_TM = 1024
INPROJ_TN = 1024
OUTPROJ_TM = 256
OUTPROJ_SUB = 128
DISPATCH_TOKENS = 256
MOE_TM = 512
MOE_TF = 1024
MOE_SUB_ROWS = 128
DMA_UNROLL = 32
DMA_QUEUES = 2
FINAL_TM = 128


def _dot_nt(a, b):
    return lax.dot_general(a, b, (((1,), (1,)), ((), ())), preferred_element_type=F32)


def _dot_tn(a, b):
    return lax.dot_general(a, b, (((0,), (0,)), ((), ())), preferred_element_type=F32)


def _log_sigmoid(z):
    return jnp.minimum(z, 0.0) - jnp.log1p(jnp.exp(-jnp.abs(z)))


def _sigmoid(z):
    return 1.0 / (1.0 + jnp.exp(-z))


def _cparams(semantics):
    return pltpu.CompilerParams(dimension_semantics=semantics, vmem_limit_bytes=VMEM_LIMIT_BYTES)


_HI_MASK = 0xFFFF0000


def _packed_rows(d):
    return d // (2 * LANES)


def _pack_store(o_ref, v):
    n, d = v.shape
    s, half = _packed_rows(d), d // 2
    bits = pltpu.bitcast(v.astype(BF16).astype(F32), jnp.uint32)
    for c in range(s):
        hi = bits[:, c * LANES:(c + 1) * LANES]
        lo = bits[:, half + c * LANES:half + (c + 1) * LANES]
        o_ref[pl.ds(c, n, stride=s), :] = hi | jnp.right_shift(lo, jnp.uint32(16))


def _unpack_load(buf, first_row, n, s, c):
    w = buf[pl.ds(first_row * s + c, n, stride=s), :]
    hi = pltpu.bitcast(w & jnp.uint32(_HI_MASK), F32)
    lo = pltpu.bitcast(jnp.left_shift(w, jnp.uint32(16)), F32)
    return hi, lo


def _inproj_body(*refs, tn, with_gates):
    if with_gates:
        x_ref, lnw_ref, w_ref, wg_ref, o_ref, g_ref, h_scr = refs
    else:
        x_ref, lnw_ref, w_ref, o_ref, h_scr = refs

    @pl.when(pl.program_id(1) == 0)
    def _():
        x = x_ref[...]
        h = x * lax.rsqrt(jnp.mean(x * x, axis=-1, keepdims=True) + EPS) * lnw_ref[...]
        hb = h.astype(BF16)
        h_scr[...] = hb
        if with_gates:
            g_ref[...] = jnp.dot(hb, wg_ref[...], preferred_element_type=F32)

    res = jnp.dot(h_scr[...], w_ref[...], preferred_element_type=F32)
    for c in range(tn // LANES):
        o_ref[c] = res[:, c * LANES:(c + 1) * LANES].astype(o_ref.dtype)


def _inproj(x2d, ln_w, w_bf, n_tiles, col_tile, out_dtype, w_gates_pad=None):
    T, D = x2d.shape
    tm = min(INPROJ_TM, T)
    tn = INPROJ_TN
    with_gates = w_gates_pad is not None
    in_specs = [
        pl.BlockSpec((tm, D), lambda i, j: (i, 0)),
        pl.BlockSpec((1, D), lambda i, j: (0, 0)),
        pl.BlockSpec((D, tn), lambda i, j: (0, col_tile(j))),
    ]
    out_specs = [pl.BlockSpec((tn // LANES, tm, LANES), lambda i, j: (j, i, 0))]
    out_shape = [jax.ShapeDtypeStruct((n_tiles * tn // LANES, T, LANES), out_dtype)]
    args = [x2d, ln_w, w_bf]
    if with_gates:
        in_specs.append(pl.BlockSpec((D, LANES), lambda i, j: (0, 0)))
        out_specs.append(pl.BlockSpec((tm, LANES), lambda i, j: (i, 0)))
        out_shape.append(jax.ShapeDtypeStruct((T, LANES), F32))
        args.append(w_gates_pad)
    return pl.pallas_call(
        functools.partial(_inproj_body, tn=tn, with_gates=with_gates),
        grid=(T // tm, n_tiles),
        in_specs=in_specs,
        out_specs=out_specs,
        out_shape=out_shape,
        scratch_shapes=[pltpu.VMEM((tm, D), BF16)],
        compiler_params=_cparams(("parallel", "arbitrary")),
        name="inproj_gates" if with_gates else "inproj",
    )(*args)


def _hgrn_body(q_ref, f_ref, i_ref, g_ref, lb_ref, nw_ref, o_ref, st_scr, *, rows, heads):
    C, SUB = HG_CHUNK, HG_SUB
    nsub = C // SUB

    @pl.when(pl.program_id(2) == 0)
    def _():
        st_scr[...] = jnp.zeros_like(st_scr)

    lbs = [lb_ref[hh] for hh in range(heads)]
    log_lbs = [jnp.log(lb) for lb in lbs]
    log_1mlbs = [jnp.log1p(-lb) for lb in lbs]

    r_i = lax.broadcasted_iota(jnp.int32, (C, C), 0)
    c_i = lax.broadcasted_iota(jnp.int32, (C, C), 1)
    tri = (c_i <= r_i).astype(F32)
    sub_shift = SUB.bit_length() - 1
    diag_mask = (c_i <= r_i) & (jnp.right_shift(r_i, sub_shift) == jnp.right_shift(c_i, sub_shift))
    row_id = lax.broadcasted_iota(jnp.int32, (C, HG_DK), 0)

    pairs = [(j, hh) for j in range(rows // C) for hh in range(heads)]
    tri_b = tri.astype(BF16)
    st = {p: {} for p in pairs}

    for p in pairs:
        j, hh = p
        z = f_ref[hh, pl.ds(j * C, C), :].astype(F32)
        e = jnp.exp(-jnp.abs(z))
        log_sig = jnp.minimum(z, 0.0) - jnp.log1p(e)
        sig_neg = jnp.where(z >= 0, e, 1.0) / (1.0 + e)
        cc = log_1mlbs[hh] + log_sig
        log_f = jnp.maximum(log_lbs[hh], cc) + jnp.log1p(jnp.exp(-jnp.abs(log_lbs[hh] - cc)))
        st[p]["kk"] = (1.0 - lbs[hh]) * sig_neg
        hi = log_f.astype(BF16)
        lo = (log_f - hi.astype(F32)).astype(BF16)
        st[p]["b"] = (jnp.dot(tri_b, hi, preferred_element_type=F32)
                      + jnp.dot(tri_b, lo, preferred_element_type=F32))

    for p in pairs:
        j, hh = p
        q = q_ref[hh, pl.ds(j * C, C), :].astype(F32)
        kk, b = st[p]["kk"], st[p]["b"]
        b_last = b[C - 1:C, :]
        st[p]["dec"] = jnp.exp(b_last)
        st[p]["qe"] = (q * jnp.exp(b)).astype(BF16)
        st[p]["kdec"] = (kk * jnp.exp(b_last - b)).astype(BF16)
        refs = [b[I * SUB:I * SUB + 1, :] for I in range(nsub)]
        refb = jnp.concatenate([jnp.broadcast_to(r, (SUB, HG_DK)) for r in refs], axis=0)
        qd = (q * jnp.exp(b - refb)).astype(BF16)
        kd = (kk * jnp.exp(refb - b)).astype(BF16)
        q_parts, k_parts = [], []
        for J in range(nsub - 1):
            r = refs[J + 1]
            qj = q * jnp.exp(jnp.minimum(b - r, 0.0))
            kj = kk * jnp.exp(jnp.minimum(r - b, 0.0))
            q_parts.append(jnp.where(row_id >= (J + 1) * SUB, qj, 0.0).astype(BF16))
            k_parts.append(jnp.where((row_id >= J * SUB) & (row_id < (J + 1) * SUB), kj, 0.0).astype(BF16))
        st[p]["att_d"] = _dot_nt(qd, kd)
        st[p]["att_o"] = _dot_nt(jnp.concatenate(q_parts, axis=1), jnp.concatenate(k_parts, axis=1))

    for p in pairs:
        j, hh = p
        vb = i_ref[hh, pl.ds(j * C, C), :].astype(BF16)
        att = jnp.where(diag_mask, st[p]["att_d"], 0.0) + st[p]["att_o"]
        st[p]["intra"] = jnp.dot(att.astype(BF16), vb, preferred_element_type=F32)
        st[p]["upd"] = _dot_tn(vb, st[p]["kdec"])

    for hh in range(heads):
        s = st_scr[hh]
        for j in range(rows // C):
            st[(j, hh)]["s_in"] = s.astype(BF16)
            s = s * st[(j, hh)]["dec"] + st[(j, hh)]["upd"]
        st_scr[hh] = s

    for p in pairs:
        j, hh = p
        g = g_ref[hh, pl.ds(j * C, C), :].astype(F32)
        o = _dot_nt(st[p]["qe"], st[p]["s_in"]) + st[p]["intra"]
        y = o * lax.rsqrt(jnp.mean(o * o, axis=-1, keepdims=True) + EPS) * nw_ref[hh]
        o_ref[pl.ds(j * C, C), hh * LANES:(hh + 1) * LANES] = (y * (g * _sigmoid(g))).astype(o_ref.dtype)


def _hgrn(proj_b, proj_f, lb, norm_w, batch, seq):
    T = batch * seq
    rows = min(MIX_ROWS, seq)
    nblk = seq // rows
    H = HG_HEADS
    hp = HG_HEADS_PER_STEP

    def slab(off):
        return pl.BlockSpec((hp, rows, LANES), lambda b, h, c, off=off: (off // hp + h, b * nblk + c, 0))

    vec = pl.BlockSpec((hp, 1, LANES), lambda b, h, c: (h, 0, 0))
    return pl.pallas_call(
        functools.partial(_hgrn_body, rows=rows, heads=hp),
        grid=(batch, H // hp, nblk),
        in_specs=[slab(0), slab(0), slab(H), slab(2 * H), vec, vec],
        out_specs=pl.BlockSpec((rows, hp * LANES), lambda b, h, c: (b * nblk + c, h)),
        out_shape=jax.ShapeDtypeStruct((T, H * LANES), BF16),
        scratch_shapes=[pltpu.VMEM((hp, LANES, HG_DK), F32)],
        compiler_params=_cparams(("parallel", "parallel", "arbitrary")),
        name="hgrn2",
    )(proj_b, proj_f, proj_b, proj_b, lb.reshape(H, 1, HG_DK), norm_w.reshape(H, 1, LANES))


def _mlstm_body(q_ref, k_ref, v_ref, og_ref, gt_ref, gb_ref, cwq_ref, cwk_ref, cbq_ref, cbk_ref, nw_ref,
                out_ref, c_scr, n_scr, m_scr, qx_scr, kx_scr, qc_scr, kc_scr, *, rows):
    C = ML_CHUNK
    PAD = SUBLANES
    h = pl.program_id(1)

    @pl.when(pl.program_id(2) == 0)
    def _():
        c_scr[...] = jnp.zeros_like(c_scr)
        n_scr[...] = jnp.zeros_like(n_scr)
        m_scr[...] = jnp.zeros_like(m_scr)
        qx_scr[0:PAD, :] = jnp.zeros((PAD, LANES), F32)
        kx_scr[0:PAD, :] = jnp.zeros((PAD, LANES), F32)

    qx_scr[PAD:PAD + rows, :] = q_ref[0].astype(F32)
    kx_scr[PAD:PAD + rows, :] = k_ref[0].astype(F32)
    accq = jnp.zeros((rows, LANES), F32) + cbq_ref[...]
    acck = jnp.zeros((rows, LANES), F32) + cbk_ref[...]
    for j in range(CONV_W):
        off = PAD - (CONV_W - 1) + j
        accq = accq + cwq_ref[j:j + 1, :] * qx_scr[pl.ds(off, rows), :]
        acck = acck + cwk_ref[j:j + 1, :] * kx_scr[pl.ds(off, rows), :]
    qc_scr[...] = accq * _sigmoid(accq) * (ML_DQK ** -0.5)
    kc_scr[...] = acck * _sigmoid(acck)
    qx_scr[0:PAD, :] = qx_scr[rows:rows + PAD, :]
    kx_scr[0:PAD, :] = kx_scr[rows:rows + PAD, :]

    lane = lax.broadcasted_iota(jnp.int32, (C, C), 1)
    sub = lax.broadcasted_iota(jnp.int32, (C, C), 0)
    causal = lane <= sub
    lower = causal.astype(F32)
    upper = (sub <= lane).astype(F32)
    nw = nw_ref[...]
    gbias = gb_ref[...]

    chunks = range(rows // C)
    lower_b, upper_b = lower.astype(BF16), upper.astype(BF16)
    st = [{} for _ in chunks]

    def split(x):
        hi = x.astype(BF16)
        return hi, (x - hi.astype(F32)).astype(BF16)

    for j in chunks:
        gc = gt_ref[pl.ds(j * C, C), :] + gbias
        gct = gc.T
        li_col = jnp.sum(jnp.where(lane == h, gc, 0.0), axis=1, keepdims=True)
        fg_col = jnp.sum(jnp.where(lane == h + ML_HEADS, gc, 0.0), axis=1, keepdims=True)
        li_row = jnp.sum(jnp.where(sub == h, gct, 0.0), axis=0, keepdims=True)
        fg_row = jnp.sum(jnp.where(sub == h + ML_HEADS, gct, 0.0), axis=0, keepdims=True)
        ch, cl = split(_log_sigmoid(fg_col))
        rh, rl = split(_log_sigmoid(fg_row))
        g_t = (jnp.dot(lower_b, jnp.broadcast_to(ch, (C, C)), preferred_element_type=F32)
               + jnp.dot(lower_b, jnp.broadcast_to(cl, (C, C)), preferred_element_type=F32))
        g_s = (jnp.dot(jnp.broadcast_to(rh, (C, C)), upper_b, preferred_element_type=F32)
               + jnp.dot(jnp.broadcast_to(rl, (C, C)), upper_b, preferred_element_type=F32))
        st[j].update(g_t=g_t, g_s=g_s, li_col=li_col, li_row=li_row)

    for j in chunks:
        c = st[j]
        g_col = c["g_t"][:, 0:1]
        g_last = c["g_t"][C - 1:C, 0:1]
        dmat = jnp.where(causal, c["g_t"] - c["g_s"] + c["li_row"], -jnp.inf)
        log_ws = g_last - g_col + c["li_col"]
        qf = qc_scr[pl.ds(j * C, C), :]
        kf = kc_scr[pl.ds(j * C, C), :]
        qb = qf.astype(BF16)
        c.update(g_col=g_col, g_last=g_last, dmat=dmat, dmax=jnp.max(dmat, axis=1, keepdims=True),
                 log_ws=log_ws, ws_max=jnp.max(log_ws, axis=0, keepdims=True), qf=qf, kf=kf, qb=qb,
                 qk=_dot_nt(qb, kf.astype(BF16)))

    m = m_scr[:, 0:1]
    for j in chunks:
        c = st[j]
        m_new = jnp.maximum(c["g_last"] + m, c["ws_max"])
        c.update(m_prev=m, m_new=m_new, decay=jnp.exp(c["g_last"] + m - m_new))
        m = m_new
    m_scr[...] = jnp.broadcast_to(m, m_scr.shape)

    for j in chunks:
        c = st[j]
        vb = jnp.concatenate([v_ref[0, pl.ds(j * C, C), :], v_ref[1, pl.ds(j * C, C), :]], axis=1).astype(BF16)
        kw = c["kf"] * jnp.exp(c["log_ws"] - c["m_new"])
        c.update(vb=vb, upd=_dot_tn(kw.astype(BF16), vb), ksum=jnp.sum(kw, axis=0, keepdims=True))

    cm = c_scr[...]
    nv = n_scr[...]
    for j in chunks:
        c = st[j]
        c.update(c_in=cm.astype(BF16), n_in=nv)
        cm = c["decay"] * cm + c["upd"]
        nv = c["decay"] * nv + c["ksum"]
    c_scr[...] = cm
    n_scr[...] = nv

    for j in chunks:
        c = st[j]
        a_inter = c["g_col"] + c["m_prev"]
        m_t = jnp.maximum(a_inter, c["dmax"])
        w_inter = jnp.exp(a_inter - m_t)
        sqk = c["qk"] * jnp.exp(c["dmat"] - m_t)
        num = (w_inter * jnp.dot(c["qb"], c["c_in"], preferred_element_type=F32)
               + jnp.dot(sqk.astype(BF16), c["vb"], preferred_element_type=F32))
        den = (w_inter * jnp.sum(c["qf"] * c["n_in"], axis=1, keepdims=True)
               + jnp.sum(sqk, axis=1, keepdims=True))
        hh = num * (1.0 / jnp.maximum(jnp.abs(den), jnp.exp(-m_t)))
        y = hh * lax.rsqrt(jnp.mean(hh * hh, axis=-1, keepdims=True) + EPS) * nw
        og = jnp.concatenate([og_ref[0, pl.ds(j * C, C), :], og_ref[1, pl.ds(j * C, C), :]], axis=1).astype(F32)
        out_ref[pl.ds(j * C, C), :] = (y * _sigmoid(og)).astype(out_ref.dtype)


def _mlstm(proj3, gates, gate_bias_pad, conv_w, conv_b, norm_w, batch, seq, q_off):
    T = batch * seq
    rows = min(MIX_ROWS, seq)
    nblk = seq // rows
    H = ML_HEADS
    k_off = q_off + H
    v_off = k_off + H
    o_off = v_off + 2 * H

    def slab(off):
        return pl.BlockSpec((1, rows, LANES), lambda b, h, c, off=off: (off + h, b * nblk + c, 0))

    def slab2(off):
        return pl.BlockSpec((2, rows, LANES), lambda b, h, c, off=off: (off // 2 + h, b * nblk + c, 0))

    qk_w = H * ML_DQK
    return pl.pallas_call(
        functools.partial(_mlstm_body, rows=rows),
        grid=(batch, H, nblk),
        in_specs=[
            slab(q_off), slab(k_off), slab2(v_off), slab2(o_off),
            pl.BlockSpec((rows, LANES), lambda b, h, c: (b * nblk + c, 0)),
            pl.BlockSpec((1, LANES), lambda b, h, c: (0, 0)),
            pl.BlockSpec((CONV_W, LANES), lambda b, h, c: (0, h)),
            pl.BlockSpec((CONV_W, LANES), lambda b, h, c: (0, H + h)),
            pl.BlockSpec((1, LANES), lambda b, h, c: (0, h)),
            pl.BlockSpec((1, LANES), lambda b, h, c: (0, H + h)),
            pl.BlockSpec((1, ML_DV), lambda b, h, c: (0, h)),
        ],
        out_specs=pl.BlockSpec((rows, ML_DV), lambda b, h, c: (b * nblk + c, h)),
        out_shape=jax.ShapeDtypeStruct((T, H * ML_DV), BF16),
        scratch_shapes=[
            pltpu.VMEM((ML_DQK, ML_DV), F32),
            pltpu.VMEM((1, ML_DQK), F32),
            pltpu.VMEM((1, LANES), F32),
            pltpu.VMEM((rows + 2 * SUBLANES, LANES), F32),
            pltpu.VMEM((rows + 2 * SUBLANES, LANES), F32),
            pltpu.VMEM((rows, LANES), F32),
            pltpu.VMEM((rows, LANES), F32),
        ],
        compiler_params=_cparams(("parallel", "parallel", "arbitrary")),
        name="mlstm",
    )(proj3, proj3, proj3, proj3, gates, gate_bias_pad, conv_w, conv_w,
      conv_b.reshape(1, 2 * qk_w), conv_b.reshape(1, 2 * qk_w), norm_w.reshape(1, H * ML_DV))


def _outproj_body(a_ref, b_ref, x_ref, wo_ref, ln_ref, wrh_ref, wrl_ref, br_ref,
                  x2_ref, h2_ref, idx_ref, gate_ref, rank_ref, cnt_ref, cnt_scr, *, sub_rows):
    tm = x_ref.shape[0]
    ka = a_ref.shape[1]
    s = _packed_rows(x_ref.shape[1])

    @pl.when(pl.program_id(0) == 0)
    def _():
        cnt_scr[...] = jnp.zeros_like(cnt_scr)

    lane = lax.broadcasted_iota(jnp.int32, (sub_rows, LANES), 1).astype(F32)
    onehots = [[] for _ in range(TOP_K)]
    for r0 in range(0, tm, sub_rows):
        rows = pl.ds(r0, sub_rows)
        res = (jnp.dot(a_ref[rows, :], wo_ref[0:ka, :], preferred_element_type=F32)
               + jnp.dot(b_ref[rows, :], wo_ref[ka:, :], preferred_element_type=F32))
        x2 = x_ref[rows, :] + res
        x2_ref[rows, :] = x2
        h2 = x2 * lax.rsqrt(jnp.mean(x2 * x2, axis=-1, keepdims=True) + EPS) * ln_ref[...]
        _pack_store(h2_ref.at[pl.ds(r0 * s, sub_rows * s), :], h2)

        h_hi = h2.astype(BF16)
        h_lo = (h2 - h_hi.astype(F32)).astype(BF16)
        logits = (jnp.dot(h_hi, wrh_ref[...], preferred_element_type=F32)
                  + jnp.dot(h_lo, wrh_ref[...], preferred_element_type=F32)
                  + jnp.dot(h_hi, wrl_ref[...], preferred_element_type=F32)) + br_ref[...]
        vals, idxs = [], []
        cur = logits
        for _ in range(TOP_K):
            m = jnp.max(cur, axis=1, keepdims=True)
            ix = jnp.min(jnp.where(cur == m, lane, float(LANES)), axis=1, keepdims=True)
            vals.append(m)
            idxs.append(ix)
            cur = jnp.where(lane == ix, -jnp.inf, cur)
        es = [jnp.exp(v - vals[0]) for v in vals]
        inv = 1.0 / (es[0] + es[1] + es[2] + es[3])
        gate = jnp.zeros(logits.shape, F32)
        idx = jnp.zeros(logits.shape, F32)
        for k in range(TOP_K):
            gate = jnp.where(lane == float(k), es[k] * inv, gate)
            idx = jnp.where(lane == float(k), idxs[k], idx)
            onehots[k].append((lane == idxs[k]).astype(F32))
        gate_ref[rows, :] = gate
        idx_ref[rows, :] = idx.astype(jnp.int32)

    oh_k = [jnp.concatenate(o, axis=0) for o in onehots]
    oh = oh_k[0] + oh_k[1] + oh_k[2] + oh_k[3]
    r_i = lax.broadcasted_iota(jnp.int32, (tm, tm), 0)
    c_i = lax.broadcasted_iota(jnp.int32, (tm, tm), 1)
    before = jnp.dot((c_i < r_i).astype(BF16), oh.astype(BF16), preferred_element_type=F32) + cnt_scr[...]
    lane_t = lax.broadcasted_iota(jnp.int32, (tm, LANES), 1)
    rank = jnp.zeros((tm, LANES), F32)
    for k in range(TOP_K):
        rank = jnp.where(lane_t == k, jnp.sum(oh_k[k] * before, axis=1, keepdims=True), rank)
    rank_ref[...] = rank.astype(jnp.int32)
    cnt = cnt_scr[...] + jnp.sum(oh, axis=0, keepdims=True)
    cnt_scr[...] = cnt
    cnt_ref[...] = cnt.astype(jnp.int32)


def _outproj(a_out, b_out, x2d, w_out_bf, ln_w, wr_hi, wr_lo, b_router_pad):
    T, D = x2d.shape
    tm = min(OUTPROJ_TM, T)
    ka, kb = a_out.shape[1], b_out.shape[1]
    s = _packed_rows(D)
    row = lambda w: pl.BlockSpec((tm, w), lambda i: (i, 0))
    full = lambda r, c: pl.BlockSpec((r, c), lambda i: (0, 0))
    return pl.pallas_call(
        functools.partial(_outproj_body, sub_rows=min(OUTPROJ_SUB, tm)),
        grid=(T // tm,),
        in_specs=[row(ka), row(kb), row(D), full(ka + kb, D), full(1, D), full(D, LANES), full(D, LANES),
                  full(1, LANES)],
        out_specs=[row(D), pl.BlockSpec((tm * s, LANES), lambda i: (i, 0)), row(LANES), row(LANES), row(LANES),
                   full(1, LANES)],
        out_shape=[
            jax.ShapeDtypeStruct((T, D), F32),
            jax.ShapeDtypeStruct((T * s, LANES), jnp.uint32),
            jax.ShapeDtypeStruct((T, LANES), jnp.int32),
            jax.ShapeDtypeStruct((T, LANES), F32),
            jax.ShapeDtypeStruct((T, LANES), jnp.int32),
            jax.ShapeDtypeStruct((1, LANES), jnp.int32),
        ],
        scratch_shapes=[pltpu.VMEM((1, LANES), F32)],
        compiler_params=_cparams(("arbitrary",)),
        name="outproj_router",
    )(a_out, b_out, x2d, w_out_bf, ln_w, wr_hi, wr_lo, b_router_pad)


def _start_row_gather(idx_at, src_ref, buf, sem, n, s):
    def start(pair, carry):
        for p in range(DMA_QUEUES):
            i = pair * DMA_QUEUES + p
            src_row = pl.multiple_of(idx_at(i) * s, s)
            dst_row = pl.multiple_of(i * s, s)
            pltpu.make_async_copy(src_ref.at[pl.ds(src_row, s), :], buf.at[pl.ds(dst_row, s), :],
                                  sem).start(priority=p)
        return carry

    lax.fori_loop(0, n // DMA_QUEUES, start, 0, unroll=DMA_UNROLL // DMA_QUEUES)


def _wait_row_gather(buf, sem):
    pltpu.make_async_copy(buf, buf, sem).wait()


def _dispatch_body(zero_ref, dest_ref, h_ref, xs_ref, zbuf, sem, zsem, *, ntok, s, tm, n_blocks):
    blk = tm * s

    def zero_copy(b):
        return pltpu.make_async_copy(zbuf, xs_ref.at[pl.ds(pl.multiple_of(b * blk, blk), blk), :], zsem)

    @pl.when(pl.program_id(0) == 0)
    def _():
        zbuf[...] = jnp.zeros_like(zbuf)

        def zstart(b, carry):
            @pl.when(zero_ref[b] == 1)
            def _():
                zero_copy(b).start()
            return carry

        def zwait(b, carry):
            @pl.when(zero_ref[b] == 1)
            def _():
                zero_copy(b).wait()
            return carry

        lax.fori_loop(0, n_blocks, zstart, 0)
        lax.fori_loop(0, n_blocks, zwait, 0)

    def start(t, carry):
        src = h_ref.at[pl.ds(pl.multiple_of(t * s, s), s), :]
        for k in range(TOP_K):
            dst_row = pl.multiple_of(dest_ref[0, 0, t * TOP_K + k] * s, s)
            pltpu.make_async_copy(src, xs_ref.at[pl.ds(dst_row, s), :], sem).start(priority=k % DMA_QUEUES)
        return carry

    lax.fori_loop(0, ntok, start, 0, unroll=DMA_UNROLL // TOP_K)
    for _ in range(TOP_K):
        pltpu.make_async_copy(h_ref, h_ref, sem).wait()


def _dispatch(dest, zero_blk, h_packed, d, n_slots, tm):
    s = _packed_rows(d)
    T = h_packed.shape[0] // s
    ntok = min(DISPATCH_TOKENS, T)
    n = ntok * TOP_K
    grid_spec = pltpu.PrefetchScalarGridSpec(
        num_scalar_prefetch=1,
        grid=(T // ntok,),
        in_specs=[
            pl.BlockSpec((1, 1, n), lambda i, z: (i, 0, 0), memory_space=pltpu.SMEM),
            pl.BlockSpec((ntok * s, LANES), lambda i, z: (i, 0)),
        ],
        out_specs=pl.BlockSpec(memory_space=pl.ANY),
        scratch_shapes=[pltpu.VMEM((tm * s, LANES), jnp.uint32), pltpu.SemaphoreType.DMA,
                        pltpu.SemaphoreType.DMA],
    )
    return pl.pallas_call(
        functools.partial(_dispatch_body, ntok=ntok, s=s, tm=tm, n_blocks=zero_blk.shape[0]),
        grid_spec=grid_spec,
        out_shape=jax.ShapeDtypeStruct((n_slots * s, LANES), jnp.uint32),
        compiler_params=_cparams(("arbitrary",)),
        name="moe_dispatch",
    )(zero_blk, dest.reshape(T // ntok, 1, n), h_packed)


def _staged_weights(sched_refs, hbm_refs, stage_refs, bf_refs, sem):
    e_ref, wt_ref, first_ref, ne_ref, nt_ref, more_ref = sched_refs
    w = pl.program_id(0)
    tn = stage_refs[0].shape[1]

    def copies(e, t):
        col = pl.multiple_of(t * tn, tn)
        return [pltpu.make_async_copy(h.at[e, :, pl.ds(col, tn)], st, sem.at[k])
                for k, (h, st) in enumerate(zip(hbm_refs, stage_refs))]

    @pl.when(w == 0)
    def _():
        for c in copies(e_ref[0], wt_ref[0]):
            c.start()

    @pl.when(first_ref[w] == 1)
    def _():
        for c in copies(e_ref[w], wt_ref[w]):
            c.wait()
        for st, bf in zip(stage_refs, bf_refs):
            for r in range(0, st.shape[0], 256):
                bf[r:r + 256, :] = st[r:r + 256, :].astype(BF16)

        @pl.when(more_ref[w] == 1)
        def _():
            for c in copies(ne_ref[w], nt_ref[w]):
                c.start()


def _row_groups(nrows, tm, compute, clear):
    sub = min(MOE_SUB_ROWS, tm)

    @pl.when(nrows == tm)
    def _():
        for r0 in range(0, tm, sub):
            compute(r0, sub)

    @pl.when(nrows < tm)
    def _():
        for r0 in range(0, tm, sub):
            @pl.when(r0 < nrows)
            def _():
                compute(r0, sub)

            @pl.when(r0 >= nrows)
            def _():
                clear(r0, sub)


def _moe_up_body(e_ref, wt_ref, r_ref, ot_ref, first_ref, nrows_ref, ne_ref, nt_ref, more_ref,
                 x_ref, wg_ref, wu_ref, bg_ref, bu_ref, o_ref, wg_stage, wu_stage, wg_scr, wu_scr, x_scr, sem):
    w = pl.program_id(0)
    _staged_weights((e_ref, wt_ref, first_ref, ne_ref, nt_ref, more_ref), (wg_ref, wu_ref),
                    (wg_stage, wu_stage), (wg_scr, wu_scr), sem)
    tm, d = x_scr.shape
    s = _packed_rows(d)

    def compute(r0, n):
        rows = pl.ds(r0, n)
        for c in range(s):
            hi, lo = _unpack_load(x_ref, r0, n, s, c)
            x_scr[rows, c * LANES:(c + 1) * LANES] = hi.astype(BF16)
            x_scr[rows, d // 2 + c * LANES:d // 2 + (c + 1) * LANES] = lo.astype(BF16)
        x = x_scr[rows, :]
        gt = jnp.dot(x, wg_scr[...], preferred_element_type=F32) + bg_ref[0]
        up = jnp.dot(x, wu_scr[...], preferred_element_type=F32) + bu_ref[0]
        gt = jnp.minimum(gt, SWIGLU_LIMIT)
        up = jnp.clip(up, -SWIGLU_LIMIT, SWIGLU_LIMIT)
        o_ref[rows, :] = ((up + 1.0) * (gt * _sigmoid(SWIGLU_ALPHA * gt))).astype(o_ref.dtype)

    def clear(r0, n):
        o_ref[pl.ds(r0, n), :] = jnp.zeros((n, o_ref.shape[1]), o_ref.dtype)

    _row_groups(nrows_ref[w], tm, compute, clear)


def _moe_down_body(e_ref, wt_ref, r_ref, ot_ref, first_ref, nrows_ref, ne_ref, nt_ref, more_ref,
                   a_ref, wd_ref, bd_ref, o_ref, wd_stage, wd_scr, sem):
    w = pl.program_id(0)
    _staged_weights((e_ref, wt_ref, first_ref, ne_ref, nt_ref, more_ref), (wd_ref,), (wd_stage,), (wd_scr,), sem)
    tm = a_ref.shape[0]
    s = o_ref.shape[0] // tm

    def compute(r0, n):
        y = jnp.dot(a_ref[pl.ds(r0, n), :], wd_scr[...], preferred_element_type=F32) + bd_ref[0]
        _pack_store(o_ref.at[pl.ds(r0 * s, n * s), :], y)

    def clear(r0, n):
        o_ref[pl.ds(r0 * s, n * s), :] = jnp.zeros((n * s, LANES), o_ref.dtype)

    _row_groups(nrows_ref[w], tm, compute, clear)


def _moe_schedule(counts, tm, n_tiles, n_blocks):
    n_items = n_tiles * n_blocks
    experts = jnp.arange(N_EXPERTS, dtype=jnp.int32)
    blocks_e = (counts + tm - 1) // tm
    bend = jnp.cumsum(blocks_e)
    bstart = bend - blocks_e
    item_end = n_tiles * bend
    total = item_end[-1]
    later = (experts[None, :] > experts[:, None]) & (blocks_e[None, :] > 0)
    next_e = jnp.min(jnp.where(later, experts[None, :], N_EXPERTS - 1), axis=1)

    w = jnp.arange(n_items, dtype=jnp.int32)
    valid = w < total
    wc = jnp.minimum(w, jnp.maximum(total - 1, 0))
    e = jnp.minimum(jnp.sum((item_end[None, :] <= wc[:, None]).astype(jnp.int32), axis=1), N_EXPERTS - 1)
    sel = e[:, None] == experts[None, :]
    pick = lambda table: jnp.sum(jnp.where(sel, table[None, :], 0), axis=1)
    nb = jnp.maximum(pick(blocks_e), 1)
    local = wc - n_tiles * pick(bstart)
    wtile = sum((local >= t * nb).astype(jnp.int32) for t in range(1, n_tiles)) if n_tiles > 1 else 0 * local
    jblk = local - wtile * nb
    spare = jnp.maximum(w - total, 0)
    rblk = jnp.where(valid, pick(bstart) + jblk, bend[-1] + spare // n_tiles)
    otile = jnp.where(valid, wtile, spare % n_tiles)
    nrows = jnp.where(valid, jnp.clip(pick(counts) - jblk * tm, 0, tm), 0)
    first = (jblk == 0) & valid
    last_tile = wtile == n_tiles - 1
    more = first & (w + nb < total)
    i32 = lambda a: a.astype(jnp.int32)
    return (e, i32(wtile), i32(rblk), i32(otile), i32(first), i32(nrows),
            i32(jnp.where(last_tile, pick(next_e), e)), i32(jnp.where(last_tile, 0, wtile + 1)), i32(more))


def _moe_up(sched, xs_packed, w_gate, w_up, b_gate, b_up):
    D, d_ff = w_gate.shape[1], w_gate.shape[2]
    s = _packed_rows(D)
    n_slots = xs_packed.shape[0] // s
    tm, tf = MOE_TM, min(MOE_TF, d_ff)
    n_items = sched[0].shape[0]
    wspec = pl.BlockSpec(memory_space=pl.ANY)
    bspec = pl.BlockSpec((1, 1, tf), lambda w, e, wt, r, ot, *_: (e[w], 0, wt[w]))
    grid_spec = pltpu.PrefetchScalarGridSpec(
        num_scalar_prefetch=len(sched),
        grid=(n_items,),
        in_specs=[pl.BlockSpec((tm * s, LANES), lambda w, e, wt, r, ot, *_: (r[w], 0)),
                  wspec, wspec, bspec, bspec],
        out_specs=pl.BlockSpec((tm, tf), lambda w, e, wt, r, ot, *_: (r[w], ot[w])),
        scratch_shapes=[pltpu.VMEM((D, tf), F32), pltpu.VMEM((D, tf), F32),
                        pltpu.VMEM((D, tf), BF16), pltpu.VMEM((D, tf), BF16), pltpu.VMEM((tm, D), BF16),
                        pltpu.SemaphoreType.DMA((2,))],
    )
    return pl.pallas_call(
        _moe_up_body,
        grid_spec=grid_spec,
        out_shape=jax.ShapeDtypeStruct((n_slots, d_ff), BF16),
        compiler_params=_cparams(("arbitrary",)),
        name="moe_up",
    )(*sched, xs_packed, w_gate, w_up, b_gate.reshape(N_EXPERTS, 1, d_ff), b_up.reshape(N_EXPERTS, 1, d_ff))


def _moe_down(sched, act, w_down, b_down):
    n_slots, d_ff = act.shape
    D = w_down.shape[2]
    tm = MOE_TM
    s = _packed_rows(D)
    n_items = sched[0].shape[0]
    grid_spec = pltpu.PrefetchScalarGridSpec(
        num_scalar_prefetch=len(sched),
        grid=(n_items,),
        in_specs=[
            pl.BlockSpec((tm, d_ff), lambda w, e, wt, r, ot, *_: (r[w], 0)),
            pl.BlockSpec(memory_space=pl.ANY),
            pl.BlockSpec((1, 1, D), lambda w, e, wt, r, ot, *_: (e[w], 0, 0)),
        ],
        out_specs=pl.BlockSpec((tm * s, LANES), lambda w, e, wt, r, ot, *_: (r[w], 0)),
        scratch_shapes=[pltpu.VMEM((d_ff, D), F32), pltpu.VMEM((d_ff, D), BF16), pltpu.SemaphoreType.DMA((1,))],
    )
    return pl.pallas_call(
        _moe_down_body,
        grid_spec=grid_spec,
        out_shape=jax.ShapeDtypeStruct((n_slots * s, LANES), jnp.uint32),
        compiler_params=_cparams(("arbitrary",)),
        name="moe_down",
    )(*sched, act, w_down, b_down.reshape(N_EXPERTS, 1, D))


def _final_body(dest_ref, next_ref, x2_ref, gate_ref, w_ref, ys_ref, o_ref, buf0, buf1, sem0, sem1, *, tm, s):
    n = tm * TOP_K
    i = pl.program_id(0)

    def gather(idx_ref, tile, buf, sem):
        _start_row_gather(lambda r: idx_ref[0, 0, tile * n + r], ys_ref, buf, sem, n, s)

    def combine(tile, buf):
        rows = pl.ds(tile * tm, tm)
        gate = gate_ref[rows, :]
        half = o_ref.shape[1] // 2
        o_ref[rows, :] = x2_ref[rows, :]
        for k in range(TOP_K):
            g = gate[:, k:k + 1]
            for c in range(s):
                hi, lo = _unpack_load(buf, k * tm, tm, s, c)
                o_ref[rows, c * LANES:(c + 1) * LANES] += g * hi
                o_ref[rows, half + c * LANES:half + (c + 1) * LANES] += g * lo
        acc = o_ref[rows, :]
        o_ref[rows, :] = acc * lax.rsqrt(jnp.mean(acc * acc, axis=-1, keepdims=True) + EPS) * w_ref[...]

    @pl.when(i == 0)
    def _():
        gather(dest_ref, 0, buf0, sem0)

    gather(dest_ref, 1, buf1, sem1)
    _wait_row_gather(buf0, sem0)
    combine(0, buf0)
    gather(next_ref, 0, buf0, sem0)
    _wait_row_gather(buf1, sem1)
    combine(1, buf1)

    @pl.when(i == pl.num_programs(0) - 1)
    def _():
        _wait_row_gather(buf0, sem0)


def _final(x2, ys_packed, dest, gates_pad, w):
    T, D = x2.shape
    tm = min(FINAL_TM, T // 2)
    s = _packed_rows(D)
    n = tm * TOP_K
    steps = T // (2 * tm)
    dest_km = dest.reshape(steps, 2, tm, TOP_K).transpose(0, 1, 3, 2).reshape(steps, 1, 2 * n)
    buf = pltpu.VMEM((n * s, LANES), jnp.uint32)
    return pl.pallas_call(
        functools.partial(_final_body, tm=tm, s=s),
        grid=(steps,),
        in_specs=[
            pl.BlockSpec((1, 1, 2 * n), lambda i: (i, 0, 0), memory_space=pltpu.SMEM),
            pl.BlockSpec((1, 1, 2 * n), lambda i: (jnp.minimum(i + 1, steps - 1), 0, 0), memory_space=pltpu.SMEM),
            pl.BlockSpec((2 * tm, D), lambda i: (i, 0)),
            pl.BlockSpec((2 * tm, LANES), lambda i: (i, 0)),
            pl.BlockSpec((1, D), lambda i: (0, 0)),
            pl.BlockSpec(memory_space=pl.ANY),
        ],
        out_specs=pl.BlockSpec((2 * tm, D), lambda i: (i, 0)),
        out_shape=jax.ShapeDtypeStruct((T, D), F32),
        scratch_shapes=[buf, buf, pltpu.SemaphoreType.DMA, pltpu.SemaphoreType.DMA],
        compiler_params=_cparams(("arbitrary",)),
        name="final_norm",
    )(dest_km, dest_km, x2, gates_pad, w, ys_packed)


def _moe(h2_packed, T, D, top_idx, rank, counts, w_gate, b_gate, w_up, b_up, w_down, b_down):
    A = T * TOP_K
    tm = MOE_TM
    n_blocks = (A + N_EXPERTS * (tm - 1) + tm - 1) // tm
    n_slots = n_blocks * tm

    blocks_e = (counts + tm - 1) // tm
    bend = jnp.cumsum(blocks_e)
    bstart = bend - blocks_e
    experts = jnp.arange(N_EXPERTS, dtype=jnp.int32)
    first_slot = jnp.sum(jnp.where(top_idx[:, :, None] == experts, bstart * tm, 0), axis=-1)
    dest = (first_slot + rank).astype(jnp.int32).reshape(A)
    blk = jnp.arange(n_blocks, dtype=jnp.int32)
    is_last = jnp.any((blk[:, None] == bend[None, :] - 1) & (blocks_e[None, :] > 0), axis=1)
    zero_blk = ((blk >= bend[-1]) | is_last).astype(jnp.int32)

    xs = _dispatch(dest, zero_blk, h2_packed, D, n_slots, tm)
    d_ff = w_gate.shape[2]
    act = _moe_up(_moe_schedule(counts, tm, -(-d_ff // MOE_TF), n_blocks), xs, w_gate, w_up, b_gate, b_up)
    ys = _moe_down(_moe_schedule(counts, tm, 1, n_blocks), act, w_down, b_down)
    return ys, dest


def kernel(x, ln1_w, w_in, hg_lb_logits, hg_norm_w, ml_conv_w, ml_conv_b, ml_igate_b, ml_fgate_b, ml_norm_w,
           w_out, ln2_w, w_router, b_router, w_gate, b_gate, w_up, b_up, w_down, b_down, final_norm_w):
    B, S, D = x.shape
    T = B * S
    depth = w_in.shape[0]
    hg_w = HG_HEADS * HG_DK
    n_main = 4 * hg_w + 2 * ML_HEADS * ML_DQK + 2 * ML_HEADS * ML_DV
    lb_all = jnp.cumsum(jax.nn.softmax(hg_lb_logits.astype(F32), axis=0), axis=0)

    xc = x.reshape(T, D)
    for l in range(depth):
        w_gates_pad = jnp.pad(w_in[l][:, n_main:], ((0, 0), (0, LANES - 2 * ML_HEADS))).astype(BF16)
        w_bf = w_in[l].astype(BF16)
        ln1 = ln1_w[l].reshape(1, D)
        hf_tiles = hg_w // INPROJ_TN
        proj_f = _inproj(xc, ln1, w_bf, hf_tiles, lambda j: j + hf_tiles, F32)[0]
        proj_b, gates = _inproj(xc, ln1, w_bf, n_main // INPROJ_TN - hf_tiles,
                                lambda j: jnp.where(j >= hf_tiles, j + hf_tiles, j), BF16, w_gates_pad)
        a_out = _hgrn(proj_b, proj_f, lb_all[l], hg_norm_w[l], B, S)
        gate_bias = jnp.pad(jnp.concatenate([ml_igate_b[l], ml_fgate_b[l]]), (0, LANES - 2 * ML_HEADS))
        b_out = _mlstm(proj_b, gates, gate_bias.reshape(1, LANES), ml_conv_w[l], ml_conv_b[l], ml_norm_w[l],
                       B, S, 3 * HG_HEADS)
        wr_pad = jnp.pad(w_router[l], ((0, 0), (0, LANES - N_EXPERTS)))
        wr_hi = wr_pad.astype(BF16)
        wr_lo = (wr_pad - wr_hi.astype(F32)).astype(BF16)
        br_pad = jnp.pad(b_router[l], (0, LANES - N_EXPERTS), constant_values=-1e30).reshape(1, LANES)
        x2, h2, idx_pad, gates_pad, rank_pad, cnt = _outproj(
            a_out, b_out, xc, w_out[l].astype(BF16), ln2_w[l].reshape(1, D), wr_hi, wr_lo, br_pad)
        ys, dest = _moe(h2, T, D, idx_pad[:, :TOP_K], rank_pad[:, :TOP_K], cnt[0, :N_EXPERTS],
                        w_gate[l], b_gate[l], w_up[l], b_up[l], w_down[l], b_down[l])
        if l + 1 < depth:
            raise NotImplementedError("only the final layer fuses the output norm")
        xc = _final(x2, ys, dest, gates_pad, final_norm_w.reshape(1, D))
    return xc.reshape(B, S, D)
```

```python
import functools

import jax
import jax.numpy as jnp
from jax import lax
from jax.experimental import pallas as pl
from jax.experimental.pallas import tpu as pltpu

F32 = jnp.float32
BF16 = jnp.bfloat16

EPS = 1e-6
HG_HEADS = 8
HG_DK = 128
ML_HEADS = 4
ML_DQK = 128
ML_DV = 256
CONV_W = 4
N_EXPERTS = 32
TOP_K = 4
SWIGLU_ALPHA = 1.702
SWIGLU_LIMIT = 7.0

LANES = 128
SUBLANES = 8
VMEM_LIMIT_BYTES = 56 * 1024 * 1024

HG_CHUNK = 64
HG_SUB = 16
ML_CHUNK = 128
MIX_ROWS = 512
HG_HEADS_PER_STEP = 2

INPROJ_TM = 1024
INPROJ_TN = 1024
OUTPROJ_TM = 256
OUTPROJ_SUB = 128
DISPATCH_TOKENS = 256
MOE_TM = 512
MOE_TF = 1024
MOE_SUB_ROWS = 128
DMA_UNROLL = 32
DMA_QUEUES = 2
FINAL_TM = 128


def _dot_nt(a, b):
    return lax.dot_general(a, b, (((1,), (1,)), ((), ())), preferred_element_type=F32)


def _dot_tn(a, b):
    return lax.dot_general(a, b, (((0,), (0,)), ((), ())), preferred_element_type=F32)


def _log_sigmoid(z):
    return jnp.minimum(z, 0.0) - jnp.log1p(jnp.exp(-jnp.abs(z)))


def _sigmoid(z):
    return 1.0 / (1.0 + jnp.exp(-z))


def _cparams(semantics):
    return pltpu.CompilerParams(dimension_semantics=semantics, vmem_limit_bytes=VMEM_LIMIT_BYTES)


_HI_MASK = 0xFFFF0000


def _packed_rows(d):
    return d // (2 * LANES)


def _pack_store(o_ref, v):
    n, d = v.shape
    s, half = _packed_rows(d), d // 2
    bits = pltpu.bitcast(v.astype(BF16).astype(F32), jnp.uint32)
    for c in range(s):
        hi = bits[:, c * LANES:(c + 1) * LANES]
        lo = bits[:, half + c * LANES:half + (c + 1) * LANES]
        o_ref[pl.ds(c, n, stride=s), :] = hi | jnp.right_shift(lo, jnp.uint32(16))


def _unpack_load(buf, first_row, n, s, c):
    w = buf[pl.ds(first_row * s + c, n, stride=s), :]
    hi = pltpu.bitcast(w & jnp.uint32(_HI_MASK), F32)
    lo = pltpu.bitcast(jnp.left_shift(w, jnp.uint32(16)), F32)
    return hi, lo


def _inproj_body(*refs, tn, with_gates):
    if with_gates:
        x_ref, lnw_ref, w_ref, wg_ref, o_ref, g_ref, h_scr = refs
    else:
        x_ref, lnw_ref, w_ref, o_ref, h_scr = refs

    @pl.when(pl.program_id(1) == 0)
    def _():
        x = x_ref[...]
        h = x * lax.rsqrt(jnp.mean(x * x, axis=-1, keepdims=True) + EPS) * lnw_ref[...]
        hb = h.astype(BF16)
        h_scr[...] = hb
        if with_gates:
            g_ref[...] = jnp.dot(hb, wg_ref[...], preferred_element_type=F32)

    res = jnp.dot(h_scr[...], w_ref[...], preferred_element_type=F32)
    for c in range(tn // LANES):
        o_ref[c] = res[:, c * LANES:(c + 1) * LANES].astype(o_ref.dtype)


def _inproj(x2d, ln_w, w_bf, n_tiles, col_tile, out_dtype, w_gates_pad=None):
    T, D = x2d.shape
    tm = min(INPROJ_TM, T)
    tn = INPROJ_TN
    with_gates = w_gates_pad is not None
    in_specs = [
        pl.BlockSpec((tm, D), lambda i, j: (i, 0)),
        pl.BlockSpec((1, D), lambda i, j: (0, 0)),
        pl.BlockSpec((D, tn), lambda i, j: (0, col_tile(j))),
    ]
    out_specs = [pl.BlockSpec((tn // LANES, tm, LANES), lambda i, j: (j, i, 0))]
    out_shape = [jax.ShapeDtypeStruct((n_tiles * tn // LANES, T, LANES), out_dtype)]
    args = [x2d, ln_w, w_bf]
    if with_gates:
        in_specs.append(pl.BlockSpec((D, LANES), lambda i, j: (0, 0)))
        out_specs.append(pl.BlockSpec((tm, LANES), lambda i, j: (i, 0)))
        out_shape.append(jax.ShapeDtypeStruct((T, LANES), F32))
        args.append(w_gates_pad)
    return pl.pallas_call(
        functools.partial(_inproj_body, tn=tn, with_gates=with_gates),
        grid=(T // tm, n_tiles),
        in_specs=in_specs,
        out_specs=out_specs,
        out_shape=out_shape,
        scratch_shapes=[pltpu.VMEM((tm, D), BF16)],
        compiler_params=_cparams(("parallel", "arbitrary")),
        name="inproj_gates" if with_gates else "inproj",
    )(*args)


def _hgrn_body(q_ref, f_ref, i_ref, g_ref, lb_ref, nw_ref, o_ref, st_scr, *, rows, heads):
    C, SUB = HG_CHUNK, HG_SUB
    nsub = C // SUB

    @pl.when(pl.program_id(2) == 0)
    def _():
        st_scr[...] = jnp.zeros_like(st_scr)

    lbs = [lb_ref[hh] for hh in range(heads)]
    log_lbs = [jnp.log(lb) for lb in lbs]
    log_1mlbs = [jnp.log1p(-lb) for lb in lbs]

    r_i = lax.broadcasted_iota(jnp.int32, (C, C), 0)
    c_i = lax.broadcasted_iota(jnp.int32, (C, C), 1)
    tri = (c_i <= r_i).astype(F32)
    sub_shift = SUB.bit_length() - 1
    diag_mask = (c_i <= r_i) & (jnp.right_shift(r_i, sub_shift) == jnp.right_shift(c_i, sub_shift))
    row_id = lax.broadcasted_iota(jnp.int32, (C, HG_DK), 0)

    pairs = [(j, hh) for j in range(rows // C) for hh in range(heads)]
    tri_b = tri.astype(BF16)
    st = {p: {} for p in pairs}

    for p in pairs:
        j, hh = p
        z = f_ref[hh, pl.ds(j * C, C), :].astype(F32)
        e = jnp.exp(-jnp.abs(z))
        log_sig = jnp.minimum(z, 0.0) - jnp.log1p(e)
        sig_neg = jnp.where(z >= 0, e, 1.0) / (1.0 + e)
        cc = log_1mlbs[hh] + log_sig
        log_f = jnp.maximum(log_lbs[hh], cc) + jnp.log1p(jnp.exp(-jnp.abs(log_lbs[hh] - cc)))
        st[p]["kk"] = (1.0 - lbs[hh]) * sig_neg
        hi = log_f.astype(BF16)
        lo = (log_f - hi.astype(F32)).astype(BF16)
        st[p]["b"] = (jnp.dot(tri_b, hi, preferred_element_type=F32)
                      + jnp.dot(tri_b, lo, preferred_element_type=F32))

    for p in pairs:
        j, hh = p
        q = q_ref[hh, pl.ds(j * C, C), :].astype(F32)
        kk, b = st[p]["kk"], st[p]["b"]
        b_last = b[C - 1:C, :]
        st[p]["dec"] = jnp.exp(b_last)
        st[p]["qe"] = (q * jnp.exp(b)).astype(BF16)
        st[p]["kdec"] = (kk * jnp.exp(b_last - b)).astype(BF16)
        refs = [b[I * SUB:I * SUB + 1, :] for I in range(nsub)]
        refb = jnp.concatenate([jnp.broadcast_to(r, (SUB, HG_DK)) for r in refs], axis=0)
        qd = (q * jnp.exp(b - refb)).astype(BF16)
        kd = (kk * jnp.exp(refb - b)).astype(BF16)
        q_parts, k_parts = [], []
        for J in range(nsub - 1):
            r = refs[J + 1]
            qj = q * jnp.exp(jnp.minimum(b - r, 0.0))
            kj = kk * jnp.exp(jnp.minimum(r - b, 0.0))
            q_parts.append(jnp.where(row_id >= (J + 1) * SUB, qj, 0.0).astype(BF16))
            k_parts.append(jnp.where((row_id >= J * SUB) & (row_id < (J + 1) * SUB), kj, 0.0).astype(BF16))
        st[p]["att_d"] = _dot_nt(qd, kd)
        st[p]["att_o"] = _dot_nt(jnp.concatenate(q_parts, axis=1), jnp.concatenate(k_parts, axis=1))

    for p in pairs:
        j, hh = p
        vb = i_ref[hh, pl.ds(j * C, C), :].astype(BF16)
        att = jnp.where(diag_mask, st[p]["att_d"], 0.0) + st[p]["att_o"]
        st[p]["intra"] = jnp.dot(att.astype(BF16), vb, preferred_element_type=F32)
        st[p]["upd"] = _dot_tn(vb, st[p]["kdec"])

    for hh in range(heads):
        s = st_scr[hh]
        for j in range(rows // C):
            st[(j, hh)]["s_in"] = s.astype(BF16)
            s = s * st[(j, hh)]["dec"] + st[(j, hh)]["upd"]
        st_scr[hh] = s

    for p in pairs:
        j, hh = p
        g = g_ref[hh, pl.ds(j * C, C), :].astype(F32)
        o = _dot_nt(st[p]["qe"], st[p]["s_in"]) + st[p]["intra"]
        y = o * lax.rsqrt(jnp.mean(o * o, axis=-1, keepdims=True) + EPS) * nw_ref[hh]
        o_ref[pl.ds(j * C, C), hh * LANES:(hh + 1) * LANES] = (y * (g * _sigmoid(g))).astype(o_ref.dtype)


def _hgrn(proj_b, proj_f, lb, norm_w, batch, seq):
    T = batch * seq
    rows = min(MIX_ROWS, seq)
    nblk = seq // rows
    H = HG_HEADS
    hp = HG_HEADS_PER_STEP

    def slab(off):
        return pl.BlockSpec((hp, rows, LANES), lambda b, h, c, off=off: (off // hp + h, b * nblk + c, 0))

    vec = pl.BlockSpec((hp, 1, LANES), lambda b, h, c: (h, 0, 0))
    return pl.pallas_call(
        functools.partial(_hgrn_body, rows=rows, heads=hp),
        grid=(batch, H // hp, nblk),
        in_specs=[slab(0), slab(0), slab(H), slab(2 * H), vec, vec],
        out_specs=pl.BlockSpec((rows, hp * LANES), lambda b, h, c: (b * nblk + c, h)),
        out_shape=jax.ShapeDtypeStruct((T, H * LANES), BF16),
        scratch_shapes=[pltpu.VMEM((hp, LANES, HG_DK), F32)],
        compiler_params=_cparams(("parallel", "parallel", "arbitrary")),
        name="hgrn2",
    )(proj_b, proj_f, proj_b, proj_b, lb.reshape(H, 1, HG_DK), norm_w.reshape(H, 1, LANES))


def _mlstm_body(q_ref, k_ref, v_ref, og_ref, gt_ref, gb_ref, cwq_ref, cwk_ref, cbq_ref, cbk_ref, nw_ref,
                out_ref, c_scr, n_scr, m_scr, qx_scr, kx_scr, qc_scr, kc_scr, *, rows):
    C = ML_CHUNK
    PAD = SUBLANES
    h = pl.program_id(1)

    @pl.when(pl.program_id(2) == 0)
    def _():
        c_scr[...] = jnp.zeros_like(c_scr)
        n_scr[...] = jnp.zeros_like(n_scr)
        m_scr[...] = jnp.zeros_like(m_scr)
        qx_scr[0:PAD, :] = jnp.zeros((PAD, LANES), F32)
        kx_scr[0:PAD, :] = jnp.zeros((PAD, LANES), F32)

    qx_scr[PAD:PAD + rows, :] = q_ref[0].astype(F32)
    kx_scr[PAD:PAD + rows, :] = k_ref[0].astype(F32)
    accq = jnp.zeros((rows, LANES), F32) + cbq_ref[...]
    acck = jnp.zeros((rows, LANES), F32) + cbk_ref[...]
    for j in range(CONV_W):
        off = PAD - (CONV_W - 1) + j
        accq = accq + cwq_ref[j:j + 1, :] * qx_scr[pl.ds(off, rows), :]
        acck = acck + cwk_ref[j:j + 1, :] * kx_scr[pl.ds(off, rows), :]
    qc_scr[...] = accq * _sigmoid(accq) * (ML_DQK ** -0.5)
    kc_scr[...] = acck * _sigmoid(acck)
    qx_scr[0:PAD, :] = qx_scr[rows:rows + PAD, :]
    kx_scr[0:PAD, :] = kx_scr[rows:rows + PAD, :]

    lane = lax.broadcasted_iota(jnp.int32, (C, C), 1)
    sub = lax.broadcasted_iota(jnp.int32, (C, C), 0)
    causal = lane <= sub
    lower = causal.astype(F32)
    upper = (sub <= lane).astype(F32)
    nw = nw_ref[...]
    gbias = gb_ref[...]

    chunks = range(rows // C)
    lower_b, upper_b = lower.astype(BF16), upper.astype(BF16)
    st = [{} for _ in chunks]

    def split(x):
        hi = x.astype(BF16)
        return hi, (x - hi.astype(F32)).astype(BF16)

    for j in chunks:
        gc = gt_ref[pl.ds(j * C, C), :] + gbias
        gct = gc.T
        li_col = jnp.sum(jnp.where(lane == h, gc, 0.0), axis=1, keepdims=True)
        fg_col = jnp.sum(jnp.where(lane == h + ML_HEADS, gc, 0.0), axis=1, keepdims=True)
        li_row = jnp.sum(jnp.where(sub == h, gct, 0.0), axis=0, keepdims=True)
        fg_row = jnp.sum(jnp.where(sub == h + ML_HEADS, gct, 0.0), axis=0, keepdims=True)
        ch, cl = split(_log_sigmoid(fg_col))
        rh, rl = split(_log_sigmoid(fg_row))
        g_t = (jnp.dot(lower_b, jnp.broadcast_to(ch, (C, C)), preferred_element_type=F32)
               + jnp.dot(lower_b, jnp.broadcast_to(cl, (C, C)), preferred_element_type=F32))
        g_s = (jnp.dot(jnp.broadcast_to(rh, (C, C)), upper_b, preferred_element_type=F32)
               + jnp.dot(jnp.broadcast_to(rl, (C, C)), upper_b, preferred_element_type=F32))
        st[j].update(g_t=g_t, g_s=g_s, li_col=li_col, li_row=li_row)

    for j in chunks:
        c = st[j]
        g_col = c["g_t"][:, 0:1]
        g_last = c["g_t"][C - 1:C, 0:1]
        dmat = jnp.where(causal, c["g_t"] - c["g_s"] + c["li_row"], -jnp.inf)
        log_ws = g_last - g_col + c["li_col"]
        qf = qc_scr[pl.ds(j * C, C), :]
        kf = kc_scr[pl.ds(j * C, C), :]
        qb = qf.astype(BF16)
        c.update(g_col=g_col, g_last=g_last, dmat=dmat, dmax=jnp.max(dmat, axis=1, keepdims=True),
                 log_ws=log_ws, ws_max=jnp.max(log_ws, axis=0, keepdims=True), qf=qf, kf=kf, qb=qb,
                 qk=_dot_nt(qb, kf.astype(BF16)))

    m = m_scr[:, 0:1]
    for j in chunks:
        c = st[j]
        m_new = jnp.maximum(c["g_last"] + m, c["ws_max"])
        c.update(m_prev=m, m_new=m_new, decay=jnp.exp(c["g_last"] + m - m_new))
        m = m_new
    m_scr[...] = jnp.broadcast_to(m, m_scr.shape)

    for j in chunks:
        c = st[j]
        vb = jnp.concatenate([v_ref[0, pl.ds(j * C, C), :], v_ref[1, pl.ds(j * C, C), :]], axis=1).astype(BF16)
        kw = c["kf"] * jnp.exp(c["log_ws"] - c["m_new"])
        c.update(vb=vb, upd=_dot_tn(kw.astype(BF16), vb), ksum=jnp.sum(kw, axis=0, keepdims=True))

    cm = c_scr[...]
    nv = n_scr[...]
    for j in chunks:
        c = st[j]
        c.update(c_in=cm.astype(BF16), n_in=nv)
        cm = c["decay"] * cm + c["upd"]
        nv = c["decay"] * nv + c["ksum"]
    c_scr[...] = cm
    n_scr[...] = nv

    for j in chunks:
        c = st[j]
        a_inter = c["g_col"] + c["m_prev"]
        m_t = jnp.maximum(a_inter, c["dmax"])
        w_inter = jnp.exp(a_inter - m_t)
        sqk = c["qk"] * jnp.exp(c["dmat"] - m_t)
        num = (w_inter * jnp.dot(c["qb"], c["c_in"], preferred_element_type=F32)
               + jnp.dot(sqk.astype(BF16), c["vb"], preferred_element_type=F32))
        den = (w_inter * jnp.sum(c["qf"] * c["n_in"], axis=1, keepdims=True)
               + jnp.sum(sqk, axis=1, keepdims=True))
        hh = num * (1.0 / jnp.maximum(jnp.abs(den), jnp.exp(-m_t)))
        y = hh * lax.rsqrt(jnp.mean(hh * hh, axis=-1, keepdims=True) + EPS) * nw
        og = jnp.concatenate([og_ref[0, pl.ds(j * C, C), :], og_ref[1, pl.ds(j * C, C), :]], axis=1).astype(F32)
        out_ref[pl.ds(j * C, C), :] = (y * _sigmoid(og)).astype(out_ref.dtype)


def _mlstm(proj3, gates, gate_bias_pad, conv_w, conv_b, norm_w, batch, seq, q_off):
    T = batch * seq
    rows = min(MIX_ROWS, seq)
    nblk = seq // rows
    H = ML_HEADS
    k_off = q_off + H
    v_off = k_off + H
    o_off = v_off + 2 * H

    def slab(off):
        return pl.BlockSpec((1, rows, LANES), lambda b, h, c, off=off: (off + h, b * nblk + c, 0))

    def slab2(off):
        return pl.BlockSpec((2, rows, LANES), lambda b, h, c, off=off: (off // 2 + h, b * nblk + c, 0))

    qk_w = H * ML_DQK
    return pl.pallas_call(
        functools.partial(_mlstm_body, rows=rows),
        grid=(batch, H, nblk),
        in_specs=[
            slab(q_off), slab(k_off), slab2(v_off), slab2(o_off),
            pl.BlockSpec((rows, LANES), lambda b, h, c: (b * nblk + c, 0)),
            pl.BlockSpec((1, LANES), lambda b, h, c: (0, 0)),
            pl.BlockSpec((CONV_W, LANES), lambda b, h, c: (0, h)),
            pl.BlockSpec((CONV_W, LANES), lambda b, h, c: (0, H + h)),
            pl.BlockSpec((1, LANES), lambda b, h, c: (0, h)),
            pl.BlockSpec((1, LANES), lambda b, h, c: (0, H + h)),
            pl.BlockSpec((1, ML_DV), lambda b, h, c: (0, h)),
        ],
        out_specs=pl.BlockSpec((rows, ML_DV), lambda b, h, c: (b * nblk + c, h)),
        out_shape=jax.ShapeDtypeStruct((T, H * ML_DV), BF16),
        scratch_shapes=[
            pltpu.VMEM((ML_DQK, ML_DV), F32),
            pltpu.VMEM((1, ML_DQK), F32),
            pltpu.VMEM((1, LANES), F32),
            pltpu.VMEM((rows + 2 * SUBLANES, LANES), F32),
            pltpu.VMEM((rows + 2 * SUBLANES, LANES), F32),
            pltpu.VMEM((rows, LANES), F32),
            pltpu.VMEM((rows, LANES), F32),
        ],
        compiler_params=_cparams(("parallel", "parallel", "arbitrary")),
        name="mlstm",
    )(proj3, proj3, proj3, proj3, gates, gate_bias_pad, conv_w, conv_w,
      conv_b.reshape(1, 2 * qk_w), conv_b.reshape(1, 2 * qk_w), norm_w.reshape(1, H * ML_DV))


def _outproj_body(a_ref, b_ref, x_ref, wo_ref, ln_ref, wrh_ref, wrl_ref, br_ref,
                  x2_ref, h2_ref, idx_ref, gate_ref, rank_ref, cnt_ref, cnt_scr, *, sub_rows):
    tm = x_ref.shape[0]
    ka = a_ref.shape[1]
    s = _packed_rows(x_ref.shape[1])

    @pl.when(pl.program_id(0) == 0)
    def _():
        cnt_scr[...] = jnp.zeros_like(cnt_scr)

    lane = lax.broadcasted_iota(jnp.int32, (sub_rows, LANES), 1).astype(F32)
    onehots = [[] for _ in range(TOP_K)]
    for r0 in range(0, tm, sub_rows):
        rows = pl.ds(r0, sub_rows)
        res = (jnp.dot(a_ref[rows, :], wo_ref[0:ka, :], preferred_element_type=F32)
               + jnp.dot(b_ref[rows, :], wo_ref[ka:, :], preferred_element_type=F32))
        x2 = x_ref[rows, :] + res
        x2_ref[rows, :] = x2
        h2 = x2 * lax.rsqrt(jnp.mean(x2 * x2, axis=-1, keepdims=True) + EPS) * ln_ref[...]
        _pack_store(h2_ref.at[pl.ds(r0 * s, sub_rows * s), :], h2)

        h_hi = h2.astype(BF16)
        h_lo = (h2 - h_hi.astype(F32)).astype(BF16)
        logits = (jnp.dot(h_hi, wrh_ref[...], preferred_element_type=F32)
                  + jnp.dot(h_lo, wrh_ref[...], preferred_element_type=F32)
                  + jnp.dot(h_hi, wrl_ref[...], preferred_element_type=F32)) + br_ref[...]
        vals, idxs = [], []
        cur = logits
        for _ in range(TOP_K):
            m = jnp.max(cur, axis=1, keepdims=True)
            ix = jnp.min(jnp.where(cur == m, lane, float(LANES)), axis=1, keepdims=True)
            vals.append(m)
            idxs.append(ix)
            cur = jnp.where(lane == ix, -jnp.inf, cur)
        es = [jnp.exp(v - vals[0]) for v in vals]
        inv = 1.0 / (es[0] + es[1] + es[2] + es[3])
        gate = jnp.zeros(logits.shape, F32)
        idx = jnp.zeros(logits.shape, F32)
        for k in range(TOP_K):
            gate = jnp.where(lane == float(k), es[k] * inv, gate)
            idx = jnp.where(lane == float(k), idxs[k], idx)
            onehots[k].append((lane == idxs[k]).astype(F32))
        gate_ref[rows, :] = gate
        idx_ref[rows, :] = idx.astype(jnp.int32)

    oh_k = [jnp.concatenate(o, axis=0) for o in onehots]
    oh = oh_k[0] + oh_k[1] + oh_k[2] + oh_k[3]
    r_i = lax.broadcasted_iota(jnp.int32, (tm, tm), 0)
    c_i = lax.broadcasted_iota(jnp.int32, (tm, tm), 1)
    before = jnp.dot((c_i < r_i).astype(BF16), oh.astype(BF16), preferred_element_type=F32) + cnt_scr[...]
    lane_t = lax.broadcasted_iota(jnp.int32, (tm, LANES), 1)
    rank = jnp.zeros((tm, LANES), F32)
    for k in range(TOP_K):
        rank = jnp.where(lane_t == k, jnp.sum(oh_k[k] * before, axis=1, keepdims=True), rank)
    rank_ref[...] = rank.astype(jnp.int32)
    cnt = cnt_scr[...] + jnp.sum(oh, axis=0, keepdims=True)
    cnt_scr[...] = cnt
    cnt_ref[...] = cnt.astype(jnp.int32)


def _outproj(a_out, b_out, x2d, w_out_bf, ln_w, wr_hi, wr_lo, b_router_pad):
    T, D = x2d.shape
    tm = min(OUTPROJ_TM, T)
    ka, kb = a_out.shape[1], b_out.shape[1]
    s = _packed_rows(D)
    row = lambda w: pl.BlockSpec((tm, w), lambda i: (i, 0))
    full = lambda r, c: pl.BlockSpec((r, c), lambda i: (0, 0))
    return pl.pallas_call(
        functools.partial(_outproj_body, sub_rows=min(OUTPROJ_SUB, tm)),
        grid=(T // tm,),
        in_specs=[row(ka), row(kb), row(D), full(ka + kb, D), full(1, D), full(D, LANES), full(D, LANES),
                  full(1, LANES)],
        out_specs=[row(D), pl.BlockSpec((tm * s, LANES), lambda i: (i, 0)), row(LANES), row(LANES), row(LANES),
                   full(1, LANES)],
        out_shape=[
            jax.ShapeDtypeStruct((T, D), F32),
            jax.ShapeDtypeStruct((T * s, LANES), jnp.uint32),
            jax.ShapeDtypeStruct((T, LANES), jnp.int32),
            jax.ShapeDtypeStruct((T, LANES), F32),
            jax.ShapeDtypeStruct((T, LANES), jnp.int32),
            jax.ShapeDtypeStruct((1, LANES), jnp.int32),
        ],
        scratch_shapes=[pltpu.VMEM((1, LANES), F32)],
        compiler_params=_cparams(("arbitrary",)),
        name="outproj_router",
    )(a_out, b_out, x2d, w_out_bf, ln_w, wr_hi, wr_lo, b_router_pad)


def _start_row_gather(idx_at, src_ref, buf, sem, n, s):
    def start(pair, carry):
        for p in range(DMA_QUEUES):
            i = pair * DMA_QUEUES + p
            src_row = pl.multiple_of(idx_at(i) * s, s)
            dst_row = pl.multiple_of(i * s, s)
            pltpu.make_async_copy(src_ref.at[pl.ds(src_row, s), :], buf.at[pl.ds(dst_row, s), :],
                                  sem).start(priority=p)
        return carry

    lax.fori_loop(0, n // DMA_QUEUES, start, 0, unroll=DMA_UNROLL // DMA_QUEUES)


def _wait_row_gather(buf, sem):
    pltpu.make_async_copy(buf, buf, sem).wait()


def _dispatch_body(zero_ref, dest_ref, h_ref, xs_ref, zbuf, sem, zsem, *, ntok, s, tm, n_blocks):
    blk = tm * s

    def zero_copy(b):
        return pltpu.make_async_copy(zbuf, xs_ref.at[pl.ds(pl.multiple_of(b * blk, blk), blk), :], zsem)

    @pl.when(pl.program_id(0) == 0)
    def _():
        zbuf[...] = jnp.zeros_like(zbuf)

        def zstart(b, carry):
            @pl.when(zero_ref[b] == 1)
            def _():
                zero_copy(b).start()
            return carry

        def zwait(b, carry):
            @pl.when(zero_ref[b] == 1)
            def _():
                zero_copy(b).wait()
            return carry

        lax.fori_loop(0, n_blocks, zstart, 0)
        lax.fori_loop(0, n_blocks, zwait, 0)

    def start(t, carry):
        src = h_ref.at[pl.ds(pl.multiple_of(t * s, s), s), :]
        for k in range(TOP_K):
            dst_row = pl.multiple_of(dest_ref[0, 0, t * TOP_K + k] * s, s)
            pltpu.make_async_copy(src, xs_ref.at[pl.ds(dst_row, s), :], sem).start(priority=k % DMA_QUEUES)
        return carry

    lax.fori_loop(0, ntok, start, 0, unroll=DMA_UNROLL // TOP_K)
    for _ in range(TOP_K):
        pltpu.make_async_copy(h_ref, h_ref, sem).wait()


def _dispatch(dest, zero_blk, h_packed, d, n_slots, tm):
    s = _packed_rows(d)
    T = h_packed.shape[0] // s
    ntok = min(DISPATCH_TOKENS, T)
    n = ntok * TOP_K
    grid_spec = pltpu.PrefetchScalarGridSpec(
        num_scalar_prefetch=1,
        grid=(T // ntok,),
        in_specs=[
            pl.BlockSpec((1, 1, n), lambda i, z: (i, 0, 0), memory_space=pltpu.SMEM),
            pl.BlockSpec((ntok * s, LANES), lambda i, z: (i, 0)),
        ],
        out_specs=pl.BlockSpec(memory_space=pl.ANY),
        scratch_shapes=[pltpu.VMEM((tm * s, LANES), jnp.uint32), pltpu.SemaphoreType.DMA,
                        pltpu.SemaphoreType.DMA],
    )
    return pl.pallas_call(
        functools.partial(_dispatch_body, ntok=ntok, s=s, tm=tm, n_blocks=zero_blk.shape[0]),
        grid_spec=grid_spec,
        out_shape=jax.ShapeDtypeStruct((n_slots * s, LANES), jnp.uint32),
        compiler_params=_cparams(("arbitrary",)),
        name="moe_dispatch",
    )(zero_blk, dest.reshape(T // ntok, 1, n), h_packed)


def _staged_weights(sched_refs, hbm_refs, stage_refs, bf_refs, sem):
    e_ref, wt_ref, first_ref, ne_ref, nt_ref, more_ref = sched_refs
    w = pl.program_id(0)
    tn = stage_refs[0].shape[1]

    def copies(e, t):
        col = pl.multiple_of(t * tn, tn)
        return [pltpu.make_async_copy(h.at[e, :, pl.ds(col, tn)], st, sem.at[k])
                for k, (h, st) in enumerate(zip(hbm_refs, stage_refs))]

    @pl.when(w == 0)
    def _():
        for c in copies(e_ref[0], wt_ref[0]):
            c.start()

    @pl.when(first_ref[w] == 1)
    def _():
        for c in copies(e_ref[w], wt_ref[w]):
            c.wait()
        for st, bf in zip(stage_refs, bf_refs):
            for r in range(0, st.shape[0], 256):
                bf[r:r + 256, :] = st[r:r + 256, :].astype(BF16)

        @pl.when(more_ref[w] == 1)
        def _():
            for c in copies(ne_ref[w], nt_ref[w]):
                c.start()


def _row_groups(nrows, tm, compute, clear):
    sub = min(MOE_SUB_ROWS, tm)
    groups = tm // sub
    for live in range(groups + 1):
        @pl.when((nrows > (live - 1) * sub) & (nrows <= live * sub))
        def _(live=live):
            for g in range(live):
                compute(g * sub, sub)
            for g in range(live, groups):
                clear(g * sub, sub)


def _moe_up_body(e_ref, wt_ref, r_ref, ot_ref, first_ref, nrows_ref, ne_ref, nt_ref, more_ref,
                 x_ref, wg_ref, wu_ref, bg_ref, bu_ref, o_ref, wg_stage, wu_stage, wg_scr, wu_scr, x_scr, sem):
    w = pl.program_id(0)
    _staged_weights((e_ref, wt_ref, first_ref, ne_ref, nt_ref, more_ref), (wg_ref, wu_ref),
                    (wg_stage, wu_stage), (wg_scr, wu_scr), sem)
    tm, d = x_scr.shape
    s = _packed_rows(d)

    def compute(r0, n):
        rows = pl.ds(r0, n)
        for c in range(s):
            hi, lo = _unpack_load(x_ref, r0, n, s, c)
            x_scr[rows, c * LANES:(c + 1) * LANES] = hi.astype(BF16)
            x_scr[rows, d // 2 + c * LANES:d // 2 + (c + 1) * LANES] = lo.astype(BF16)
        x = x_scr[rows, :]
        gt = jnp.dot(x, wg_scr[...], preferred_element_type=F32) + bg_ref[0]
        up = jnp.dot(x, wu_scr[...], preferred_element_type=F32) + bu_ref[0]
        gt = jnp.minimum(gt, SWIGLU_LIMIT)
        up = jnp.clip(up, -SWIGLU_LIMIT, SWIGLU_LIMIT)
        o_ref[rows, :] = ((up + 1.0) * (gt * _sigmoid(SWIGLU_ALPHA * gt))).astype(o_ref.dtype)

    def clear(r0, n):
        o_ref[pl.ds(r0, n), :] = jnp.zeros((n, o_ref.shape[1]), o_ref.dtype)

    _row_groups(nrows_ref[w], tm, compute, clear)


def _moe_down_body(e_ref, wt_ref, r_ref, ot_ref, first_ref, nrows_ref, ne_ref, nt_ref, more_ref,
                   a_ref, wd_ref, bd_ref, o_ref, wd_stage, wd_scr, sem):
    w = pl.program_id(0)
    _staged_weights((e_ref, wt_ref, first_ref, ne_ref, nt_ref, more_ref), (wd_ref,), (wd_stage,), (wd_scr,), sem)
    tm = a_ref.shape[0]
    s = o_ref.shape[0] // tm

    def compute(r0, n):
        y = jnp.dot(a_ref[pl.ds(r0, n), :], wd_scr[...], preferred_element_type=F32) + bd_ref[0]
        _pack_store(o_ref.at[pl.ds(r0 * s, n * s), :], y)

    def clear(r0, n):
        o_ref[pl.ds(r0 * s, n * s), :] = jnp.zeros((n * s, LANES), o_ref.dtype)

    _row_groups(nrows_ref[w], tm, compute, clear)


def _moe_schedule(counts, tm, n_tiles, n_blocks):
    n_items = n_tiles * n_blocks
    experts = jnp.arange(N_EXPERTS, dtype=jnp.int32)
    blocks_e = (counts + tm - 1) // tm
    bend = jnp.cumsum(blocks_e)
    bstart = bend - blocks_e
    item_end = n_tiles * bend
    total = item_end[-1]
    later = (experts[None, :] > experts[:, None]) & (blocks_e[None, :] > 0)
    next_e = jnp.min(jnp.where(later, experts[None, :], N_EXPERTS - 1), axis=1)

    w = jnp.arange(n_items, dtype=jnp.int32)
    valid = w < total
    wc = jnp.minimum(w, jnp.maximum(total - 1, 0))
    e = jnp.minimum(jnp.sum((item_end[None, :] <= wc[:, None]).astype(jnp.int32), axis=1), N_EXPERTS - 1)
    sel = e[:, None] == experts[None, :]
    pick = lambda table: jnp.sum(jnp.where(sel, table[None, :], 0), axis=1)
    nb = jnp.maximum(pick(blocks_e), 1)
    local = wc - n_tiles * pick(bstart)
    wtile = sum((local >= t * nb).astype(jnp.int32) for t in range(1, n_tiles)) if n_tiles > 1 else 0 * local
    jblk = local - wtile * nb
    spare = jnp.maximum(w - total, 0)
    rblk = jnp.where(valid, pick(bstart) + jblk, bend[-1] + spare // n_tiles)
    otile = jnp.where(valid, wtile, spare % n_tiles)
    nrows = jnp.where(valid, jnp.clip(pick(counts) - jblk * tm, 0, tm), 0)
    first = (jblk == 0) & valid
    last_tile = wtile == n_tiles - 1
    more = first & (w + nb < total)
    i32 = lambda a: a.astype(jnp.int32)
    return (e, i32(wtile), i32(rblk), i32(otile), i32(first), i32(nrows),
            i32(jnp.where(last_tile, pick(next_e), e)), i32(jnp.where(last_tile, 0, wtile + 1)), i32(more))


def _moe_up(sched, xs_packed, w_gate, w_up, b_gate, b_up):
    D, d_ff = w_gate.shape[1], w_gate.shape[2]
    s = _packed_rows(D)
    n_slots = xs_packed.shape[0] // s
    tm, tf = MOE_TM, min(MOE_TF, d_ff)
    n_items = sched[0].shape[0]
    wspec = pl.BlockSpec(memory_space=pl.ANY)
    bspec = pl.BlockSpec((1, 1, tf), lambda w, e, wt, r, ot, *_: (e[w], 0, wt[w]))
    grid_spec = pltpu.PrefetchScalarGridSpec(
        num_scalar_prefetch=len(sched),
        grid=(n_items,),
        in_specs=[pl.BlockSpec((tm * s, LANES), lambda w, e, wt, r, ot, *_: (r[w], 0)),
                  wspec, wspec, bspec, bspec],
        out_specs=pl.BlockSpec((tm, tf), lambda w, e, wt, r, ot, *_: (r[w], ot[w])),
        scratch_shapes=[pltpu.VMEM((D, tf), F32), pltpu.VMEM((D, tf), F32),
                        pltpu.VMEM((D, tf), BF16), pltpu.VMEM((D, tf), BF16), pltpu.VMEM((tm, D), BF16),
                        pltpu.SemaphoreType.DMA((2,))],
    )
    return pl.pallas_call(
        _moe_up_body,
        grid_spec=grid_spec,
        out_shape=jax.ShapeDtypeStruct((n_slots, d_ff), BF16),
        compiler_params=_cparams(("arbitrary",)),
        name="moe_up",
    )(*sched, xs_packed, w_gate, w_up, b_gate.reshape(N_EXPERTS, 1, d_ff), b_up.reshape(N_EXPERTS, 1, d_ff))


def _moe_down(sched, act, w_down, b_down):
    n_slots, d_ff = act.shape
    D = w_down.shape[2]
    tm = MOE_TM
    s = _packed_rows(D)
    n_items = sched[0].shape[0]
    grid_spec = pltpu.PrefetchScalarGridSpec(
        num_scalar_prefetch=len(sched),
        grid=(n_items,),
        in_specs=[
            pl.BlockSpec((tm, d_ff), lambda w, e, wt, r, ot, *_: (r[w], 0)),
            pl.BlockSpec(memory_space=pl.ANY),
            pl.BlockSpec((1, 1, D), lambda w, e, wt, r, ot, *_: (e[w], 0, 0)),
        ],
        out_specs=pl.BlockSpec((tm * s, LANES), lambda w, e, wt, r, ot, *_: (r[w], 0)),
        scratch_shapes=[pltpu.VMEM((d_ff, D), F32), pltpu.VMEM((d_ff, D), BF16), pltpu.SemaphoreType.DMA((1,))],
    )
    return pl.pallas_call(
        _moe_down_body,
        grid_spec=grid_spec,
        out_shape=jax.ShapeDtypeStruct((n_slots * s, LANES), jnp.uint32),
        compiler_params=_cparams(("arbitrary",)),
        name="moe_down",
    )(*sched, act, w_down, b_down.reshape(N_EXPERTS, 1, D))


def _final_body(dest_ref, next_ref, x2_ref, gate_ref, w_ref, ys_ref, o_ref, buf0, buf1, sem0, sem1, *, tm, s):
    n = tm * TOP_K
    i = pl.program_id(0)

    def gather(idx_ref, tile, buf, sem):
        _start_row_gather(lambda r: idx_ref[0, 0, tile * n + r], ys_ref, buf, sem, n, s)

    def combine(tile, buf):
        rows = pl.ds(tile * tm, tm)
        gate = gate_ref[rows, :]
        half = o_ref.shape[1] // 2
        o_ref[rows, :] = x2_ref[rows, :]
        for k in range(TOP_K):
            g = gate[:, k:k + 1]
            for c in range(s):
                hi, lo = _unpack_load(buf, k * tm, tm, s, c)
                o_ref[rows, c * LANES:(c + 1) * LANES] += g * hi
                o_ref[rows, half + c * LANES:half + (c + 1) * LANES] += g * lo
        acc = o_ref[rows, :]
        o_ref[rows, :] = acc * lax.rsqrt(jnp.mean(acc * acc, axis=-1, keepdims=True) + EPS) * w_ref[...]

    @pl.when(i == 0)
    def _():
        gather(dest_ref, 0, buf0, sem0)

    gather(dest_ref, 1, buf1, sem1)
    _wait_row_gather(buf0, sem0)
    combine(0, buf0)
    gather(next_ref, 0, buf0, sem0)
    _wait_row_gather(buf1, sem1)
    combine(1, buf1)

    @pl.when(i == pl.num_programs(0) - 1)
    def _():
        _wait_row_gather(buf0, sem0)


def _final(x2, ys_packed, dest, gates_pad, w):
    T, D = x2.shape
    tm = min(FINAL_TM, T // 2)
    s = _packed_rows(D)
    n = tm * TOP_K
    steps = T // (2 * tm)
    dest_km = dest.reshape(steps, 2, tm, TOP_K).transpose(0, 1, 3, 2).reshape(steps, 1, 2 * n)
    buf = pltpu.VMEM((n * s, LANES), jnp.uint32)
    return pl.pallas_call(
        functools.partial(_final_body, tm=tm, s=s),
        grid=(steps,),
        in_specs=[
            pl.BlockSpec((1, 1, 2 * n), lambda i: (i, 0, 0), memory_space=pltpu.SMEM),
            pl.BlockSpec((1, 1, 2 * n), lambda i: (jnp.minimum(i + 1, steps - 1), 0, 0), memory_space=pltpu.SMEM),
            pl.BlockSpec((2 * tm, D), lambda i: (i, 0)),
            pl.BlockSpec((2 * tm, LANES), lambda i: (i, 0)),
            pl.BlockSpec((1, D), lambda i: (0, 0)),
            pl.BlockSpec(memory_space=pl.ANY),
        ],
        out_specs=pl.BlockSpec((2 * tm, D), lambda i: (i, 0)),
        out_shape=jax.ShapeDtypeStruct((T, D), F32),
        scratch_shapes=[buf, buf, pltpu.SemaphoreType.DMA, pltpu.SemaphoreType.DMA],
        compiler_params=_cparams(("arbitrary",)),
        name="final_norm",
    )(dest_km, dest_km, x2, gates_pad, w, ys_packed)


def _moe(h2_packed, T, D, top_idx, rank, counts, w_gate, b_gate, w_up, b_up, w_down, b_down):
    A = T * TOP_K
    tm = MOE_TM
    n_blocks = (A + N_EXPERTS * (tm - 1) + tm - 1) // tm
    n_slots = n_blocks * tm

    blocks_e = (counts + tm - 1) // tm
    bend = jnp.cumsum(blocks_e)
    bstart = bend - blocks_e
    experts = jnp.arange(N_EXPERTS, dtype=jnp.int32)
    first_slot = jnp.sum(jnp.where(top_idx[:, :, None] == experts, bstart * tm, 0), axis=-1)
    dest = (first_slot + rank).astype(jnp.int32).reshape(A)
    blk = jnp.arange(n_blocks, dtype=jnp.int32)
    is_last = jnp.any((blk[:, None] == bend[None, :] - 1) & (blocks_e[None, :] > 0), axis=1)
    zero_blk = ((blk >= bend[-1]) | is_last).astype(jnp.int32)

    xs = _dispatch(dest, zero_blk, h2_packed, D, n_slots, tm)
    d_ff = w_gate.shape[2]
    act = _moe_up(_moe_schedule(counts, tm, -(-d_ff // MOE_TF), n_blocks), xs, w_gate, w_up, b_gate, b_up)
    ys = _moe_down(_moe_schedule(counts, tm, 1, n_blocks), act, w_down, b_down)
    return ys, dest


def kernel(x, ln1_w, w_in, hg_lb_logits, hg_norm_w, ml_conv_w, ml_conv_b, ml_igate_b, ml_fgate_b, ml_norm_w,
           w_out, ln2_w, w_router, b_router, w_gate, b_gate, w_up, b_up, w_down, b_down, final_norm_w):
    B, S, D = x.shape
    T = B * S
    depth = w_in.shape[0]
    hg_w = HG_HEADS * HG_DK
    n_main = 4 * hg_w + 2 * ML_HEADS * ML_DQK + 2 * ML_HEADS * ML_DV
    lb_all = jnp.cumsum(jax.nn.softmax(hg_lb_logits.astype(F32), axis=0), axis=0)

    xc = x.reshape(T, D)
    for l in range(depth):
        w_gates_pad = jnp.pad(w_in[l][:, n_main:], ((0, 0), (0, LANES - 2 * ML_HEADS))).astype(BF16)
        w_bf = w_in[l].astype(BF16)
        ln1 = ln1_w[l].reshape(1, D)
        hf_tiles = hg_w // INPROJ_TN
        proj_f = _inproj(xc, ln1, w_bf, hf_tiles, lambda j: j + hf_tiles, F32)[0]
        proj_b, gates = _inproj(xc, ln1, w_bf, n_main // INPROJ_TN - hf_tiles,
                                lambda j: jnp.where(j >= hf_tiles, j + hf_tiles, j), BF16, w_gates_pad)
        a_out = _hgrn(proj_b, proj_f, lb_all[l], hg_norm_w[l], B, S)
        gate_bias = jnp.pad(jnp.concatenate([ml_igate_b[l], ml_fgate_b[l]]), (0, LANES - 2 * ML_HEADS))
        b_out = _mlstm(proj_b, gates, gate_bias.reshape(1, LANES), ml_conv_w[l], ml_conv_b[l], ml_norm_w[l],
                       B, S, 3 * HG_HEADS)
        wr_pad = jnp.pad(w_router[l], ((0, 0), (0, LANES - N_EXPERTS)))
        wr_hi = wr_pad.astype(BF16)
        wr_lo = (wr_pad - wr_hi.astype(F32)).astype(BF16)
        br_pad = jnp.pad(b_router[l], (0, LANES - N_EXPERTS), constant_values=-1e30).reshape(1, LANES)
        x2, h2, idx_pad, gates_pad, rank_pad, cnt = _outproj(
            a_out, b_out, xc, w_out[l].astype(BF16), ln2_w[l].reshape(1, D), wr_hi, wr_lo, br_pad)
        ys, dest = _moe(h2, T, D, idx_pad[:, :TOP_K], rank_pad[:, :TOP_K], cnt[0, :N_EXPERTS],
                        w_gate[l], b_gate[l], w_up[l], b_up[l], w_down[l], b_down[l])
        if l + 1 < depth:
            raise NotImplementedError("only the final layer fuses the output norm")
        xc = _final(x2, ys, dest, gates_pad, final_norm_w.reshape(1, D))
    return xc.reshape(B, S, D)
```

```python
import functools

import jax
import jax.numpy as jnp
from jax import lax
from jax.experimental import pallas as pl
from jax.experimental.pallas import tpu as pltpu

F32 = jnp.float32
BF16 = jnp.bfloat16

EPS = 1e-6
HG_HEADS = 8
HG_DK = 128
ML_HEADS = 4
ML_DQK = 128
ML_DV = 256
CONV_W = 4
N_EXPERTS = 32
TOP_K = 4
SWIGLU_ALPHA = 1.702
SWIGLU_LIMIT = 7.0

LANES = 128
SUBLANES = 8
VMEM_LIMIT_BYTES = 56 * 1024 * 1024

HG_CHUNK = 64
HG_SUB = 16
ML_CHUNK = 128
MIX_ROWS = 512
HG_HEADS_PER_STEP = 2

INPROJ_TM = 1024
INPROJ_TN = 1024
OUTPROJ_TM = 256
OUTPROJ_SUB = 128
DISPATCH_TOKENS = 256
MOE_TM = 512
MOE_TF = 1024
MOE_SUB_ROWS = 128
DMA_UNROLL = 32
WEIGHT_DMA_SPLIT = 8
DMA_QUEUES = 2
FINAL_TM = 128


def _dot_nt(a, b):
    return lax.dot_general(a, b, (((1,), (1,)), ((), ())), preferred_element_type=F32)


def _dot_tn(a, b):
    return lax.dot_general(a, b, (((0,), (0,)), ((), ())), preferred_element_type=F32)


def _log_sigmoid(z):
    return jnp.minimum(z, 0.0) - jnp.log1p(jnp.exp(-jnp.abs(z)))


def _sigmoid(z):
    return 1.0 / (1.0 + jnp.exp(-z))


def _cparams(semantics):
    return pltpu.CompilerParams(dimension_semantics=semantics, vmem_limit_bytes=VMEM_LIMIT_BYTES)


_HI_MASK = 0xFFFF0000


def _packed_rows(d):
    return d // (2 * LANES)


def _pack_store(o_ref, v):
    n, d = v.shape
    s, half = _packed_rows(d), d // 2
    bits = pltpu.bitcast(v.astype(BF16).astype(F32), jnp.uint32)
    for c in range(s):
        hi = bits[:, c * LANES:(c + 1) * LANES]
        lo = bits[:, half + c * LANES:half + (c + 1) * LANES]
        o_ref[pl.ds(c, n, stride=s), :] = hi | jnp.right_shift(lo, jnp.uint32(16))


def _unpack_load(buf, first_row, n, s, c):
    w = buf[pl.ds(first_row * s + c, n, stride=s), :]
    hi = pltpu.bitcast(w & jnp.uint32(_HI_MASK), F32)
    lo = pltpu.bitcast(jnp.left_shift(w, jnp.uint32(16)), F32)
    return hi, lo


def _inproj_body(*refs, tn, with_gates):
    if with_gates:
        x_ref, lnw_ref, w_ref, wg_ref, o_ref, g_ref, h_scr = refs
    else:
        x_ref, lnw_ref, w_ref, o_ref, h_scr = refs

    @pl.when(pl.program_id(1) == 0)
    def _():
        x = x_ref[...]
        h = x * lax.rsqrt(jnp.mean(x * x, axis=-1, keepdims=True) + EPS) * lnw_ref[...]
        hb = h.astype(BF16)
        h_scr[...] = hb
        if with_gates:
            g_ref[...] = jnp.dot(hb, wg_ref[...], preferred_element_type=F32)

    res = jnp.dot(h_scr[...], w_ref[...], preferred_element_type=F32)
    for c in range(tn // LANES):
        o_ref[c] = res[:, c * LANES:(c + 1) * LANES].astype(o_ref.dtype)


def _inproj(x2d, ln_w, w_bf, n_tiles, col_tile, out_dtype, w_gates_pad=None):
    T, D = x2d.shape
    tm = min(INPROJ_TM, T)
    tn = INPROJ_TN
    with_gates = w_gates_pad is not None
    in_specs = [
        pl.BlockSpec((tm, D), lambda i, j: (i, 0)),
        pl.BlockSpec((1, D), lambda i, j: (0, 0)),
        pl.BlockSpec((D, tn), lambda i, j: (0, col_tile(j))),
    ]
    out_specs = [pl.BlockSpec((tn // LANES, tm, LANES), lambda i, j: (j, i, 0))]
    out_shape = [jax.ShapeDtypeStruct((n_tiles * tn // LANES, T, LANES), out_dtype)]
    args = [x2d, ln_w, w_bf]
    if with_gates:
        in_specs.append(pl.BlockSpec((D, LANES), lambda i, j: (0, 0)))
        out_specs.append(pl.BlockSpec((tm, LANES), lambda i, j: (i, 0)))
        out_shape.append(jax.ShapeDtypeStruct((T, LANES), F32))
        args.append(w_gates_pad)
    return pl.pallas_call(
        functools.partial(_inproj_body, tn=tn, with_gates=with_gates),
        grid=(T // tm, n_tiles),
        in_specs=in_specs,
        out_specs=out_specs,
        out_shape=out_shape,
        scratch_shapes=[pltpu.VMEM((tm, D), BF16)],
        compiler_params=_cparams(("parallel", "arbitrary")),
        name="inproj_gates" if with_gates else "inproj",
    )(*args)


def _hgrn_body(q_ref, f_ref, i_ref, g_ref, lb_ref, nw_ref, o_ref, st_scr, *, rows, heads):
    C, SUB = HG_CHUNK, HG_SUB
    nsub = C // SUB

    @pl.when(pl.program_id(2) == 0)
    def _():
        st_scr[...] = jnp.zeros_like(st_scr)

    lbs = [lb_ref[hh] for hh in range(heads)]
    log_lbs = [jnp.log(lb) for lb in lbs]
    log_1mlbs = [jnp.log1p(-lb) for lb in lbs]

    r_i = lax.broadcasted_iota(jnp.int32, (C, C), 0)
    c_i = lax.broadcasted_iota(jnp.int32, (C, C), 1)
    tri = (c_i <= r_i).astype(F32)
    sub_shift = SUB.bit_length() - 1
    diag_mask = (c_i <= r_i) & (jnp.right_shift(r_i, sub_shift) == jnp.right_shift(c_i, sub_shift))
    row_id = lax.broadcasted_iota(jnp.int32, (C, HG_DK), 0)

    pairs = [(j, hh) for j in range(rows // C) for hh in range(heads)]
    tri_b = tri.astype(BF16)
    st = {p: {} for p in pairs}

    for p in pairs:
        j, hh = p
        z = f_ref[hh, pl.ds(j * C, C), :].astype(F32)
        e = jnp.exp(-jnp.abs(z))
        log_sig = jnp.minimum(z, 0.0) - jnp.log1p(e)
        sig_neg = jnp.where(z >= 0, e, 1.0) / (1.0 + e)
        cc = log_1mlbs[hh] + log_sig
        log_f = jnp.maximum(log_lbs[hh], cc) + jnp.log1p(jnp.exp(-jnp.abs(log_lbs[hh] - cc)))
        st[p]["kk"] = (1.0 - lbs[hh]) * sig_neg
        hi = log_f.astype(BF16)
        lo = (log_f - hi.astype(F32)).astype(BF16)
        st[p]["b"] = (jnp.dot(tri_b, hi, preferred_element_type=F32)
                      + jnp.dot(tri_b, lo, preferred_element_type=F32))

    for p in pairs:
        j, hh = p
        q = q_ref[hh, pl.ds(j * C, C), :].astype(F32)
        kk, b = st[p]["kk"], st[p]["b"]
        b_last = b[C - 1:C, :]
        st[p]["dec"] = jnp.exp(b_last)
        st[p]["qe"] = (q * jnp.exp(b)).astype(BF16)
        st[p]["kdec"] = (kk * jnp.exp(b_last - b)).astype(BF16)
        refs = [b[I * SUB:I * SUB + 1, :] for I in range(nsub)]
        refb = jnp.concatenate([jnp.broadcast_to(r, (SUB, HG_DK)) for r in refs], axis=0)
        qd = (q * jnp.exp(b - refb)).astype(BF16)
        kd = (kk * jnp.exp(refb - b)).astype(BF16)
        q_parts, k_parts = [], []
        for J in range(nsub - 1):
            r = refs[J + 1]
            qj = q * jnp.exp(jnp.minimum(b - r, 0.0))
            kj = kk * jnp.exp(jnp.minimum(r - b, 0.0))
            q_parts.append(jnp.where(row_id >= (J + 1) * SUB, qj, 0.0).astype(BF16))
            k_parts.append(jnp.where((row_id >= J * SUB) & (row_id < (J + 1) * SUB), kj, 0.0).astype(BF16))
        st[p]["att_d"] = _dot_nt(qd, kd)
        st[p]["att_o"] = _dot_nt(jnp.concatenate(q_parts, axis=1), jnp.concatenate(k_parts, axis=1))

    for p in pairs:
        j, hh = p
        vb = i_ref[hh, pl.ds(j * C, C), :].astype(BF16)
        att = jnp.where(diag_mask, st[p]["att_d"], 0.0) + st[p]["att_o"]
        st[p]["intra"] = jnp.dot(att.astype(BF16), vb, preferred_element_type=F32)
        st[p]["upd"] = _dot_tn(vb, st[p]["kdec"])

    for hh in range(heads):
        s = st_scr[hh]
        for j in range(rows // C):
            st[(j, hh)]["s_in"] = s.astype(BF16)
            s = s * st[(j, hh)]["dec"] + st[(j, hh)]["upd"]
        st_scr[hh] = s

    for p in pairs:
        j, hh = p
        g = g_ref[hh, pl.ds(j * C, C), :].astype(F32)
        o = _dot_nt(st[p]["qe"], st[p]["s_in"]) + st[p]["intra"]
        y = o * lax.rsqrt(jnp.mean(o * o, axis=-1, keepdims=True) + EPS) * nw_ref[hh]
        o_ref[pl.ds(j * C, C), hh * LANES:(hh + 1) * LANES] = (y * (g * _sigmoid(g))).astype(o_ref.dtype)


def _hgrn(proj_b, proj_f, lb, norm_w, batch, seq):
    T = batch * seq
    rows = min(MIX_ROWS, seq)
    nblk = seq // rows
    H = HG_HEADS
    hp = HG_HEADS_PER_STEP

    def slab(off):
        return pl.BlockSpec((hp, rows, LANES), lambda b, h, c, off=off: (off // hp + h, b * nblk + c, 0))

    vec = pl.BlockSpec((hp, 1, LANES), lambda b, h, c: (h, 0, 0))
    return pl.pallas_call(
        functools.partial(_hgrn_body, rows=rows, heads=hp),
        grid=(batch, H // hp, nblk),
        in_specs=[slab(0), slab(0), slab(H), slab(2 * H), vec, vec],
        out_specs=pl.BlockSpec((rows, hp * LANES), lambda b, h, c: (b * nblk + c, h)),
        out_shape=jax.ShapeDtypeStruct((T, H * LANES), BF16),
        scratch_shapes=[pltpu.VMEM((hp, LANES, HG_DK), F32)],
        compiler_params=_cparams(("parallel", "parallel", "arbitrary")),
        name="hgrn2",
    )(proj_b, proj_f, proj_b, proj_b, lb.reshape(H, 1, HG_DK), norm_w.reshape(H, 1, LANES))


def _mlstm_body(q_ref, k_ref, v_ref, og_ref, gt_ref, gb_ref, cwq_ref, cwk_ref, cbq_ref, cbk_ref, nw_ref,
                out_ref, c_scr, n_scr, m_scr, qx_scr, kx_scr, qc_scr, kc_scr, *, rows):
    C = ML_CHUNK
    PAD = SUBLANES
    h = pl.program_id(1)

    @pl.when(pl.program_id(2) == 0)
    def _():
        c_scr[...] = jnp.zeros_like(c_scr)
        n_scr[...] = jnp.zeros_like(n_scr)
        m_scr[...] = jnp.zeros_like(m_scr)
        qx_scr[0:PAD, :] = jnp.zeros((PAD, LANES), F32)
        kx_scr[0:PAD, :] = jnp.zeros((PAD, LANES), F32)

    qx_scr[PAD:PAD + rows, :] = q_ref[0].astype(F32)
    kx_scr[PAD:PAD + rows, :] = k_ref[0].astype(F32)
    accq = jnp.zeros((rows, LANES), F32) + cbq_ref[...]
    acck = jnp.zeros((rows, LANES), F32) + cbk_ref[...]
    for j in range(CONV_W):
        off = PAD - (CONV_W - 1) + j
        accq = accq + cwq_ref[j:j + 1, :] * qx_scr[pl.ds(off, rows), :]
        acck = acck + cwk_ref[j:j + 1, :] * kx_scr[pl.ds(off, rows), :]
    qc_scr[...] = accq * _sigmoid(accq) * (ML_DQK ** -0.5)
    kc_scr[...] = acck * _sigmoid(acck)
    qx_scr[0:PAD, :] = qx_scr[rows:rows + PAD, :]
    kx_scr[0:PAD, :] = kx_scr[rows:rows + PAD, :]

    lane = lax.broadcasted_iota(jnp.int32, (C, C), 1)
    sub = lax.broadcasted_iota(jnp.int32, (C, C), 0)
    causal = lane <= sub
    lower = causal.astype(F32)
    upper = (sub <= lane).astype(F32)
    nw = nw_ref[...]
    gbias = gb_ref[...]

    chunks = range(rows // C)
    lower_b, upper_b = lower.astype(BF16), upper.astype(BF16)
    st = [{} for _ in chunks]

    def split(x):
        hi = x.astype(BF16)
        return hi, (x - hi.astype(F32)).astype(BF16)

    for j in chunks:
        gc = gt_ref[pl.ds(j * C, C), :] + gbias
        gct = gc.T
        li_col = jnp.sum(jnp.where(lane == h, gc, 0.0), axis=1, keepdims=True)
        fg_col = jnp.sum(jnp.where(lane == h + ML_HEADS, gc, 0.0), axis=1, keepdims=True)
        li_row = jnp.sum(jnp.where(sub == h, gct, 0.0), axis=0, keepdims=True)
        fg_row = jnp.sum(jnp.where(sub == h + ML_HEADS, gct, 0.0), axis=0, keepdims=True)
        ch, cl = split(_log_sigmoid(fg_col))
        rh, rl = split(_log_sigmoid(fg_row))
        g_t = (jnp.dot(lower_b, jnp.broadcast_to(ch, (C, C)), preferred_element_type=F32)
               + jnp.dot(lower_b, jnp.broadcast_to(cl, (C, C)), preferred_element_type=F32))
        g_s = (jnp.dot(jnp.broadcast_to(rh, (C, C)), upper_b, preferred_element_type=F32)
               + jnp.dot(jnp.broadcast_to(rl, (C, C)), upper_b, preferred_element_type=F32))
        st[j].update(g_t=g_t, g_s=g_s, li_col=li_col, li_row=li_row)

    for j in chunks:
        c = st[j]
        g_col = c["g_t"][:, 0:1]
        g_last = c["g_t"][C - 1:C, 0:1]
        dmat = jnp.where(causal, c["g_t"] - c["g_s"] + c["li_row"], -jnp.inf)
        log_ws = g_last - g_col + c["li_col"]
        qf = qc_scr[pl.ds(j * C, C), :]
        kf = kc_scr[pl.ds(j * C, C), :]
        qb = qf.astype(BF16)
        c.update(g_col=g_col, g_last=g_last, dmat=dmat, dmax=jnp.max(dmat, axis=1, keepdims=True),
                 log_ws=log_ws, ws_max=jnp.max(log_ws, axis=0, keepdims=True), qf=qf, kf=kf, qb=qb,
                 qk=_dot_nt(qb, kf.astype(BF16)))

    m = m_scr[:, 0:1]
    for j in chunks:
        c = st[j]
        m_new = jnp.maximum(c["g_last"] + m, c["ws_max"])
        c.update(m_prev=m, m_new=m_new, decay=jnp.exp(c["g_last"] + m - m_new))
        m = m_new
    m_scr[...] = jnp.broadcast_to(m, m_scr.shape)

    for j in chunks:
        c = st[j]
        vb = jnp.concatenate([v_ref[0, pl.ds(j * C, C), :], v_ref[1, pl.ds(j * C, C), :]], axis=1).astype(BF16)
        kw = c["kf"] * jnp.exp(c["log_ws"] - c["m_new"])
        c.update(vb=vb, upd=_dot_tn(kw.astype(BF16), vb), ksum=jnp.sum(kw, axis=0, keepdims=True))

    cm = c_scr[...]
    nv = n_scr[...]
    for j in chunks:
        c = st[j]
        c.update(c_in=cm.astype(BF16), n_in=nv)
        cm = c["decay"] * cm + c["upd"]
        nv = c["decay"] * nv + c["ksum"]
    c_scr[...] = cm
    n_scr[...] = nv

    for j in chunks:
        c = st[j]
        a_inter = c["g_col"] + c["m_prev"]
        m_t = jnp.maximum(a_inter, c["dmax"])
        w_inter = jnp.exp(a_inter - m_t)
        sqk = c["qk"] * jnp.exp(c["dmat"] - m_t)
        num = (w_inter * jnp.dot(c["qb"], c["c_in"], preferred_element_type=F32)
               + jnp.dot(sqk.astype(BF16), c["vb"], preferred_element_type=F32))
        den = (w_inter * jnp.sum(c["qf"] * c["n_in"], axis=1, keepdims=True)
               + jnp.sum(sqk, axis=1, keepdims=True))
        hh = num * (1.0 / jnp.maximum(jnp.abs(den), jnp.exp(-m_t)))
        y = hh * lax.rsqrt(jnp.mean(hh * hh, axis=-1, keepdims=True) + EPS) * nw
        og = jnp.concatenate([og_ref[0, pl.ds(j * C, C), :], og_ref[1, pl.ds(j * C, C), :]], axis=1).astype(F32)
        out_ref[pl.ds(j * C, C), :] = (y * _sigmoid(og)).astype(out_ref.dtype)


def _mlstm(proj3, gates, gate_bias_pad, conv_w, conv_b, norm_w, batch, seq, q_off):
    T = batch * seq
    rows = min(MIX_ROWS, seq)
    nblk = seq // rows
    H = ML_HEADS
    k_off = q_off + H
    v_off = k_off + H
    o_off = v_off + 2 * H

    def slab(off):
        return pl.BlockSpec((1, rows, LANES), lambda b, h, c, off=off: (off + h, b * nblk + c, 0))

    def slab2(off):
        return pl.BlockSpec((2, rows, LANES), lambda b, h, c, off=off: (off // 2 + h, b * nblk + c, 0))

    qk_w = H * ML_DQK
    return pl.pallas_call(
        functools.partial(_mlstm_body, rows=rows),
        grid=(batch, H, nblk),
        in_specs=[
            slab(q_off), slab(k_off), slab2(v_off), slab2(o_off),
            pl.BlockSpec((rows, LANES), lambda b, h, c: (b * nblk + c, 0)),
            pl.BlockSpec((1, LANES), lambda b, h, c: (0, 0)),
            pl.BlockSpec((CONV_W, LANES), lambda b, h, c: (0, h)),
            pl.BlockSpec((CONV_W, LANES), lambda b, h, c: (0, H + h)),
            pl.BlockSpec((1, LANES), lambda b, h, c: (0, h)),
            pl.BlockSpec((1, LANES), lambda b, h, c: (0, H + h)),
            pl.BlockSpec((1, ML_DV), lambda b, h, c: (0, h)),
        ],
        out_specs=pl.BlockSpec((rows, ML_DV), lambda b, h, c: (b * nblk + c, h)),
        out_shape=jax.ShapeDtypeStruct((T, H * ML_DV), BF16),
        scratch_shapes=[
            pltpu.VMEM((ML_DQK, ML_DV), F32),
            pltpu.VMEM((1, ML_DQK), F32),
            pltpu.VMEM((1, LANES), F32),
            pltpu.VMEM((rows + 2 * SUBLANES, LANES), F32),
            pltpu.VMEM((rows + 2 * SUBLANES, LANES), F32),
            pltpu.VMEM((rows, LANES), F32),
            pltpu.VMEM((rows, LANES), F32),
        ],
        compiler_params=_cparams(("parallel", "parallel", "arbitrary")),
        name="mlstm",
    )(proj3, proj3, proj3, proj3, gates, gate_bias_pad, conv_w, conv_w,
      conv_b.reshape(1, 2 * qk_w), conv_b.reshape(1, 2 * qk_w), norm_w.reshape(1, H * ML_DV))


def _outproj_body(a_ref, b_ref, x_ref, wo_ref, ln_ref, wrh_ref, wrl_ref, br_ref,
                  x2_ref, h2_ref, idx_ref, gate_ref, rank_ref, cnt_ref, cnt_scr, *, sub_rows):
    tm = x_ref.shape[0]
    ka = a_ref.shape[1]
    s = _packed_rows(x_ref.shape[1])

    @pl.when(pl.program_id(0) == 0)
    def _():
        cnt_scr[...] = jnp.zeros_like(cnt_scr)

    lane = lax.broadcasted_iota(jnp.int32, (sub_rows, LANES), 1).astype(F32)
    onehots = [[] for _ in range(TOP_K)]
    for r0 in range(0, tm, sub_rows):
        rows = pl.ds(r0, sub_rows)
        res = (jnp.dot(a_ref[rows, :], wo_ref[0:ka, :], preferred_element_type=F32)
               + jnp.dot(b_ref[rows, :], wo_ref[ka:, :], preferred_element_type=F32))
        x2 = x_ref[rows, :] + res
        x2_ref[rows, :] = x2
        h2 = x2 * lax.rsqrt(jnp.mean(x2 * x2, axis=-1, keepdims=True) + EPS) * ln_ref[...]
        _pack_store(h2_ref.at[pl.ds(r0 * s, sub_rows * s), :], h2)

        h_hi = h2.astype(BF16)
        h_lo = (h2 - h_hi.astype(F32)).astype(BF16)
        logits = (jnp.dot(h_hi, wrh_ref[...], preferred_element_type=F32)
                  + jnp.dot(h_lo, wrh_ref[...], preferred_element_type=F32)
                  + jnp.dot(h_hi, wrl_ref[...], preferred_element_type=F32)) + br_ref[...]
        vals, idxs = [], []
        cur = logits
        for _ in range(TOP_K):
            m = jnp.max(cur, axis=1, keepdims=True)
            ix = jnp.min(jnp.where(cur == m, lane, float(LANES)), axis=1, keepdims=True)
            vals.append(m)
            idxs.append(ix)
            cur = jnp.where(lane == ix, -jnp.inf, cur)
        es = [jnp.exp(v - vals[0]) for v in vals]
        inv = 1.0 / (es[0] + es[1] + es[2] + es[3])
        gate = jnp.zeros(logits.shape, F32)
        idx = jnp.zeros(logits.shape, F32)
        for k in range(TOP_K):
            gate = jnp.where(lane == float(k), es[k] * inv, gate)
            idx = jnp.where(lane == float(k), idxs[k], idx)
            onehots[k].append((lane == idxs[k]).astype(F32))
        gate_ref[rows, :] = gate
        idx_ref[rows, :] = idx.astype(jnp.int32)

    oh_k = [jnp.concatenate(o, axis=0) for o in onehots]
    oh = oh_k[0] + oh_k[1] + oh_k[2] + oh_k[3]
    r_i = lax.broadcasted_iota(jnp.int32, (tm, tm), 0)
    c_i = lax.broadcasted_iota(jnp.int32, (tm, tm), 1)
    before = jnp.dot((c_i < r_i).astype(BF16), oh.astype(BF16), preferred_element_type=F32) + cnt_scr[...]
    lane_t = lax.broadcasted_iota(jnp.int32, (tm, LANES), 1)
    rank = jnp.zeros((tm, LANES), F32)
    for k in range(TOP_K):
        rank = jnp.where(lane_t == k, jnp.sum(oh_k[k] * before, axis=1, keepdims=True), rank)
    rank_ref[...] = rank.astype(jnp.int32)
    cnt = cnt_scr[...] + jnp.sum(oh, axis=0, keepdims=True)
    cnt_scr[...] = cnt
    cnt_ref[...] = cnt.astype(jnp.int32)


def _outproj(a_out, b_out, x2d, w_out_bf, ln_w, wr_hi, wr_lo, b_router_pad):
    T, D = x2d.shape
    tm = min(OUTPROJ_TM, T)
    ka, kb = a_out.shape[1], b_out.shape[1]
    s = _packed_rows(D)
    row = lambda w: pl.BlockSpec((tm, w), lambda i: (i, 0))
    full = lambda r, c: pl.BlockSpec((r, c), lambda i: (0, 0))
    return pl.pallas_call(
        functools.partial(_outproj_body, sub_rows=min(OUTPROJ_SUB, tm)),
        grid=(T // tm,),
        in_specs=[row(ka), row(kb), row(D), full(ka + kb, D), full(1, D), full(D, LANES), full(D, LANES),
                  full(1, LANES)],
        out_specs=[row(D), pl.BlockSpec((tm * s, LANES), lambda i: (i, 0)), row(LANES), row(LANES), row(LANES),
                   full(1, LANES)],
        out_shape=[
            jax.ShapeDtypeStruct((T, D), F32),
            jax.ShapeDtypeStruct((T * s, LANES), jnp.uint32),
            jax.ShapeDtypeStruct((T, LANES), jnp.int32),
            jax.ShapeDtypeStruct((T, LANES), F32),
            jax.ShapeDtypeStruct((T, LANES), jnp.int32),
            jax.ShapeDtypeStruct((1, LANES), jnp.int32),
        ],
        scratch_shapes=[pltpu.VMEM((1, LANES), F32)],
        compiler_params=_cparams(("arbitrary",)),
        name="outproj_router",
    )(a_out, b_out, x2d, w_out_bf, ln_w, wr_hi, wr_lo, b_router_pad)


def _start_row_gather(idx_at, src_ref, buf, sem, n, s):
    def start(pair, carry):
        for p in range(DMA_QUEUES):
            i = pair * DMA_QUEUES + p
            src_row = pl.multiple_of(idx_at(i) * s, s)
            dst_row = pl.multiple_of(i * s, s)
            pltpu.make_async_copy(src_ref.at[pl.ds(src_row, s), :], buf.at[pl.ds(dst_row, s), :],
                                  sem).start(priority=p)
        return carry

    lax.fori_loop(0, n // DMA_QUEUES, start, 0, unroll=DMA_UNROLL // DMA_QUEUES)


def _wait_row_gather(buf, sem):
    pltpu.make_async_copy(buf, buf, sem).wait()


def _dispatch_body(zero_ref, dest_ref, h_ref, xs_ref, zbuf, sem, zsem, *, ntok, s, tm, n_blocks):
    blk = tm * s

    def zero_copy(b):
        return pltpu.make_async_copy(zbuf, xs_ref.at[pl.ds(pl.multiple_of(b * blk, blk), blk), :], zsem)

    @pl.when(pl.program_id(0) == 0)
    def _():
        zbuf[...] = jnp.zeros_like(zbuf)

        def zstart(b, carry):
            @pl.when(zero_ref[b] == 1)
            def _():
                zero_copy(b).start()
            return carry

        def zwait(b, carry):
            @pl.when(zero_ref[b] == 1)
            def _():
                zero_copy(b).wait()
            return carry

        lax.fori_loop(0, n_blocks, zstart, 0)
        lax.fori_loop(0, n_blocks, zwait, 0)

    def start(t, carry):
        src = h_ref.at[pl.ds(pl.multiple_of(t * s, s), s), :]
        for k in range(TOP_K):
            dst_row = pl.multiple_of(dest_ref[0, 0, t * TOP_K + k] * s, s)
            pltpu.make_async_copy(src, xs_ref.at[pl.ds(dst_row, s), :], sem).start(priority=k % DMA_QUEUES)
        return carry

    lax.fori_loop(0, ntok, start, 0, unroll=DMA_UNROLL // TOP_K)
    for _ in range(TOP_K):
        pltpu.make_async_copy(h_ref, h_ref, sem).wait()


def _dispatch(dest, zero_blk, h_packed, d, n_slots, tm):
    s = _packed_rows(d)
    T = h_packed.shape[0] // s
    ntok = min(DISPATCH_TOKENS, T)
    n = ntok * TOP_K
    grid_spec = pltpu.PrefetchScalarGridSpec(
        num_scalar_prefetch=1,
        grid=(T // ntok,),
        in_specs=[
            pl.BlockSpec((1, 1, n), lambda i, z: (i, 0, 0), memory_space=pltpu.SMEM),
            pl.BlockSpec((ntok * s, LANES), lambda i, z: (i, 0)),
        ],
        out_specs=pl.BlockSpec(memory_space=pl.ANY),
        scratch_shapes=[pltpu.VMEM((tm * s, LANES), jnp.uint32), pltpu.SemaphoreType.DMA,
                        pltpu.SemaphoreType.DMA],
    )
    return pl.pallas_call(
        functools.partial(_dispatch_body, ntok=ntok, s=s, tm=tm, n_blocks=zero_blk.shape[0]),
        grid_spec=grid_spec,
        out_shape=jax.ShapeDtypeStruct((n_slots * s, LANES), jnp.uint32),
        compiler_params=_cparams(("arbitrary",)),
        name="moe_dispatch",
    )(zero_blk, dest.reshape(T // ntok, 1, n), h_packed)


def _staged_weights(sched_refs, hbm_refs, stage_refs, bf_refs, sem):
    e_ref, wt_ref, first_ref, ne_ref, nt_ref, more_ref = sched_refs
    w = pl.program_id(0)
    tn = stage_refs[0].shape[1]

    def copies(e, t):
        col = pl.multiple_of(t * tn, tn)
        out = []
        for k, (h, st) in enumerate(zip(hbm_refs, stage_refs)):
            slab = st.shape[0] // WEIGHT_DMA_SPLIT
            for r in range(0, st.shape[0], slab):
                out.append(pltpu.make_async_copy(h.at[e, pl.ds(r, slab), pl.ds(col, tn)],
                                                 st.at[pl.ds(r, slab), :], sem.at[k]))
        return out

    def start_all(cs):
        for i, c in enumerate(cs):
            c.start(priority=i % DMA_QUEUES)

    @pl.when(w == 0)
    def _():
        start_all(copies(e_ref[0], wt_ref[0]))

    @pl.when(first_ref[w] == 1)
    def _():
        for c in copies(e_ref[w], wt_ref[w]):
            c.wait()
        for st, bf in zip(stage_refs, bf_refs):
            for r in range(0, st.shape[0], 256):
                bf[r:r + 256, :] = st[r:r + 256, :].astype(BF16)

        @pl.when(more_ref[w] == 1)
        def _():
            start_all(copies(ne_ref[w], nt_ref[w]))


def _row_groups(nrows, tm, compute, clear):
    sub = min(MOE_SUB_ROWS, tm)
    groups = tm // sub
    for live in range(groups + 1):
        @pl.when((nrows > (live - 1) * sub) & (nrows <= live * sub))
        def _(live=live):
            for g in range(live):
                compute(g * sub, sub)
            for g in range(live, groups):
                clear(g * sub, sub)


def _moe_up_body(e_ref, wt_ref, r_ref, ot_ref, first_ref, nrows_ref, ne_ref, nt_ref, more_ref,
                 x_ref, wg_ref, wu_ref, bg_ref, bu_ref, o_ref, wg_stage, wu_stage, wg_scr, wu_scr, x_scr, sem):
    w = pl.program_id(0)
    _staged_weights((e_ref, wt_ref, first_ref, ne_ref, nt_ref, more_ref), (wg_ref, wu_ref),
                    (wg_stage, wu_stage), (wg_scr, wu_scr), sem)
    tm, d = x_scr.shape
    s = _packed_rows(d)

    def compute(r0, n):
        rows = pl.ds(r0, n)
        for c in range(s):
            hi, lo = _unpack_load(x_ref, r0, n, s, c)
            x_scr[rows, c * LANES:(c + 1) * LANES] = hi.astype(BF16)
            x_scr[rows, d // 2 + c * LANES:d // 2 + (c + 1) * LANES] = lo.astype(BF16)
        x = x_scr[rows, :]
        gt = jnp.dot(x, wg_scr[...], preferred_element_type=F32) + bg_ref[0]
        up = jnp.dot(x, wu_scr[...], preferred_element_type=F32) + bu_ref[0]
        gt = jnp.minimum(gt, SWIGLU_LIMIT)
        up = jnp.clip(up, -SWIGLU_LIMIT, SWIGLU_LIMIT)
        o_ref[rows, :] = ((up + 1.0) * (gt * _sigmoid(SWIGLU_ALPHA * gt))).astype(o_ref.dtype)

    def clear(r0, n):
        o_ref[pl.ds(r0, n), :] = jnp.zeros((n, o_ref.shape[1]), o_ref.dtype)

    _row_groups(nrows_ref[w], tm, compute, clear)


def _moe_down_body(e_ref, wt_ref, r_ref, ot_ref, first_ref, nrows_ref, ne_ref, nt_ref, more_ref,
                   a_ref, wd_ref, bd_ref, o_ref, wd_stage, wd_scr, sem):
    w = pl.program_id(0)
    _staged_weights((e_ref, wt_ref, first_ref, ne_ref, nt_ref, more_ref), (wd_ref,), (wd_stage,), (wd_scr,), sem)
    tm = a_ref.shape[0]
    s = o_ref.shape[0] // tm

    def compute(r0, n):
        y = jnp.dot(a_ref[pl.ds(r0, n), :], wd_scr[...], preferred_element_type=F32) + bd_ref[0]
        _pack_store(o_ref.at[pl.ds(r0 * s, n * s), :], y)

    def clear(r0, n):
        o_ref[pl.ds(r0 * s, n * s), :] = jnp.zeros((n * s, LANES), o_ref.dtype)

    _row_groups(nrows_ref[w], tm, compute, clear)


def _moe_schedule(counts, tm, n_tiles, n_blocks):
    n_items = n_tiles * n_blocks
    experts = jnp.arange(N_EXPERTS, dtype=jnp.int32)
    blocks_e = (counts + tm - 1) // tm
    bend = jnp.cumsum(blocks_e)
    bstart = bend - blocks_e
    item_end = n_tiles * bend
    total = item_end[-1]
    later = (experts[None, :] > experts[:, None]) & (blocks_e[None, :] > 0)
    next_e = jnp.min(jnp.where(later, experts[None, :], N_EXPERTS - 1), axis=1)

    w = jnp.arange(n_items, dtype=jnp.int32)
    valid = w < total
    wc = jnp.minimum(w, jnp.maximum(total - 1, 0))
    e = jnp.minimum(jnp.sum((item_end[None, :] <= wc[:, None]).astype(jnp.int32), axis=1), N_EXPERTS - 1)
    sel = e[:, None] == experts[None, :]
    pick = lambda table: jnp.sum(jnp.where(sel, table[None, :], 0), axis=1)
    nb = jnp.maximum(pick(blocks_e), 1)
    local = wc - n_tiles * pick(bstart)
    wtile = sum((local >= t * nb).astype(jnp.int32) for t in range(1, n_tiles)) if n_tiles > 1 else 0 * local
    jblk = local - wtile * nb
    spare = jnp.maximum(w - total, 0)
    rblk = jnp.where(valid, pick(bstart) + jblk, bend[-1] + spare // n_tiles)
    otile = jnp.where(valid, wtile, spare % n_tiles)
    nrows = jnp.where(valid, jnp.clip(pick(counts) - jblk * tm, 0, tm), 0)
    first = (jblk == 0) & valid
    last_tile = wtile == n_tiles - 1
    more = first & (w + nb < total)
    i32 = lambda a: a.astype(jnp.int32)
    return (e, i32(wtile), i32(rblk), i32(otile), i32(first), i32(nrows),
            i32(jnp.where(last_tile, pick(next_e), e)), i32(jnp.where(last_tile, 0, wtile + 1)), i32(more))


def _moe_up(sched, xs_packed, w_gate, w_up, b_gate, b_up):
    D, d_ff = w_gate.shape[1], w_gate.shape[2]
    s = _packed_rows(D)
    n_slots = xs_packed.shape[0] // s
    tm, tf = MOE_TM, min(MOE_TF, d_ff)
    n_items = sched[0].shape[0]
    wspec = pl.BlockSpec(memory_space=pl.ANY)
    bspec = pl.BlockSpec((1, 1, tf), lambda w, e, wt, r, ot, *_: (e[w], 0, wt[w]))
    grid_spec = pltpu.PrefetchScalarGridSpec(
        num_scalar_prefetch=len(sched),
        grid=(n_items,),
        in_specs=[pl.BlockSpec((tm * s, LANES), lambda w, e, wt, r, ot, *_: (r[w], 0)),
                  wspec, wspec, bspec, bspec],
        out_specs=pl.BlockSpec((tm, tf), lambda w, e, wt, r, ot, *_: (r[w], ot[w])),
        scratch_shapes=[pltpu.VMEM((D, tf), F32), pltpu.VMEM((D, tf), F32),
                        pltpu.VMEM((D, tf), BF16), pltpu.VMEM((D, tf), BF16), pltpu.VMEM((tm, D), BF16),
                        pltpu.SemaphoreType.DMA((2,))],
    )
    return pl.pallas_call(
        _moe_up_body,
        grid_spec=grid_spec,
        out_shape=jax.ShapeDtypeStruct((n_slots, d_ff), BF16),
        compiler_params=_cparams(("arbitrary",)),
        name="moe_up",
    )(*sched, xs_packed, w_gate, w_up, b_gate.reshape(N_EXPERTS, 1, d_ff), b_up.reshape(N_EXPERTS, 1, d_ff))


def _moe_down(sched, act, w_down, b_down):
    n_slots, d_ff = act.shape
    D = w_down.shape[2]
    tm = MOE_TM
    s = _packed_rows(D)
    n_items = sched[0].shape[0]
    grid_spec = pltpu.PrefetchScalarGridSpec(
        num_scalar_prefetch=len(sched),
        grid=(n_items,),
        in_specs=[
            pl.BlockSpec((tm, d_ff), lambda w, e, wt, r, ot, *_: (r[w], 0)),
            pl.BlockSpec(memory_space=pl.ANY),
            pl.BlockSpec((1, 1, D), lambda w, e, wt, r, ot, *_: (e[w], 0, 0)),
        ],
        out_specs=pl.BlockSpec((tm * s, LANES), lambda w, e, wt, r, ot, *_: (r[w], 0)),
        scratch_shapes=[pltpu.VMEM((d_ff, D), F32), pltpu.VMEM((d_ff, D), BF16), pltpu.SemaphoreType.DMA((1,))],
    )
    return pl.pallas_call(
        _moe_down_body,
        grid_spec=grid_spec,
        out_shape=jax.ShapeDtypeStruct((n_slots * s, LANES), jnp.uint32),
        compiler_params=_cparams(("arbitrary",)),
        name="moe_down",
    )(*sched, act, w_down, b_down.reshape(N_EXPERTS, 1, D))


def _final_body(dest_ref, next_ref, x2_ref, gate_ref, w_ref, ys_ref, o_ref, buf0, buf1, sem0, sem1, *, tm, s):
    n = tm * TOP_K
    i = pl.program_id(0)

    def gather(idx_ref, tile, buf, sem):
        _start_row_gather(lambda r: idx_ref[0, 0, tile * n + r], ys_ref, buf, sem, n, s)

    def combine(tile, buf):
        rows = pl.ds(tile * tm, tm)
        gate = gate_ref[rows, :]
        half = o_ref.shape[1] // 2
        o_ref[rows, :] = x2_ref[rows, :]
        for k in range(TOP_K):
            g = gate[:, k:k + 1]
            for c in range(s):
                hi, lo = _unpack_load(buf, k * tm, tm, s, c)
                o_ref[rows, c * LANES:(c + 1) * LANES] += g * hi
                o_ref[rows, half + c * LANES:half + (c + 1) * LANES] += g * lo
        acc = o_ref[rows, :]
        o_ref[rows, :] = acc * lax.rsqrt(jnp.mean(acc * acc, axis=-1, keepdims=True) + EPS) * w_ref[...]

    @pl.when(i == 0)
    def _():
        gather(dest_ref, 0, buf0, sem0)

    gather(dest_ref, 1, buf1, sem1)
    _wait_row_gather(buf0, sem0)
    combine(0, buf0)
    gather(next_ref, 0, buf0, sem0)
    _wait_row_gather(buf1, sem1)
    combine(1, buf1)

    @pl.when(i == pl.num_programs(0) - 1)
    def _():
        _wait_row_gather(buf0, sem0)


def _final(x2, ys_packed, dest, gates_pad, w):
    T, D = x2.shape
    tm = min(FINAL_TM, T // 2)
    s = _packed_rows(D)
    n = tm * TOP_K
    steps = T // (2 * tm)
    dest_km = dest.reshape(steps, 2, tm, TOP_K).transpose(0, 1, 3, 2).reshape(steps, 1, 2 * n)
    buf = pltpu.VMEM((n * s, LANES), jnp.uint32)
    return pl.pallas_call(
        functools.partial(_final_body, tm=tm, s=s),
        grid=(steps,),
        in_specs=[
            pl.BlockSpec((1, 1, 2 * n), lambda i: (i, 0, 0), memory_space=pltpu.SMEM),
            pl.BlockSpec((1, 1, 2 * n), lambda i: (jnp.minimum(i + 1, steps - 1), 0, 0), memory_space=pltpu.SMEM),
            pl.BlockSpec((2 * tm, D), lambda i: (i, 0)),
            pl.BlockSpec((2 * tm, LANES), lambda i: (i, 0)),
            pl.BlockSpec((1, D), lambda i: (0, 0)),
            pl.BlockSpec(memory_space=pl.ANY),
        ],
        out_specs=pl.BlockSpec((2 * tm, D), lambda i: (i, 0)),
        out_shape=jax.ShapeDtypeStruct((T, D), F32),
        scratch_shapes=[buf, buf, pltpu.SemaphoreType.DMA, pltpu.SemaphoreType.DMA],
        compiler_params=_cparams(("arbitrary",)),
        name="final_norm",
    )(dest_km, dest_km, x2, gates_pad, w, ys_packed)


def _moe(h2_packed, T, D, top_idx, rank, counts, w_gate, b_gate, w_up, b_up, w_down, b_down):
    A = T * TOP_K
    tm = MOE_TM
    n_blocks = (A + N_EXPERTS * (tm - 1) + tm - 1) // tm
    n_slots = n_blocks * tm

    blocks_e = (counts + tm - 1) // tm
    bend = jnp.cumsum(blocks_e)
    bstart = bend - blocks_e
    experts = jnp.arange(N_EXPERTS, dtype=jnp.int32)
    first_slot = jnp.sum(jnp.where(top_idx[:, :, None] == experts, bstart * tm, 0), axis=-1)
    dest = (first_slot + rank).astype(jnp.int32).reshape(A)
    blk = jnp.arange(n_blocks, dtype=jnp.int32)
    is_last = jnp.any((blk[:, None] == bend[None, :] - 1) & (blocks_e[None, :] > 0), axis=1)
    zero_blk = ((blk >= bend[-1]) | is_last).astype(jnp.int32)

    xs = _dispatch(dest, zero_blk, h2_packed, D, n_slots, tm)
    d_ff = w_gate.shape[2]
    act = _moe_up(_moe_schedule(counts, tm, -(-d_ff // MOE_TF), n_blocks), xs, w_gate, w_up, b_gate, b_up)
    ys = _moe_down(_moe_schedule(counts, tm, 1, n_blocks), act, w_down, b_down)
    return ys, dest


def kernel(x, ln1_w, w_in, hg_lb_logits, hg_norm_w, ml_conv_w, ml_conv_b, ml_igate_b, ml_fgate_b, ml_norm_w,
           w_out, ln2_w, w_router, b_router, w_gate, b_gate, w_up, b_up, w_down, b_down, final_norm_w):
    B, S, D = x.shape
    T = B * S
    depth = w_in.shape[0]
    hg_w = HG_HEADS * HG_DK
    n_main = 4 * hg_w + 2 * ML_HEADS * ML_DQK + 2 * ML_HEADS * ML_DV
    lb_all = jnp.cumsum(jax.nn.softmax(hg_lb_logits.astype(F32), axis=0), axis=0)

    xc = x.reshape(T, D)
    for l in range(depth):
        w_gates_pad = jnp.pad(w_in[l][:, n_main:], ((0, 0), (0, LANES - 2 * ML_HEADS))).astype(BF16)
        w_bf = w_in[l].astype(BF16)
        ln1 = ln1_w[l].reshape(1, D)
        hf_tiles = hg_w // INPROJ_TN
        proj_f = _inproj(xc, ln1, w_bf, hf_tiles, lambda j: j + hf_tiles, F32)[0]
        proj_b, gates = _inproj(xc, ln1, w_bf, n_main // INPROJ_TN - hf_tiles,
                                lambda j: jnp.where(j >= hf_tiles, j + hf_tiles, j), BF16, w_gates_pad)
        a_out = _hgrn(proj_b, proj_f, lb_all[l], hg_norm_w[l], B, S)
        gate_bias = jnp.pad(jnp.concatenate([ml_igate_b[l], ml_fgate_b[l]]), (0, LANES - 2 * ML_HEADS))
        b_out = _mlstm(proj_b, gates, gate_bias.reshape(1, LANES), ml_conv_w[l], ml_conv_b[l], ml_norm_w[l],
                       B, S, 3 * HG_HEADS)
        wr_pad = jnp.pad(w_router[l], ((0, 0), (0, LANES - N_EXPERTS)))
        wr_hi = wr_pad.astype(BF16)
        wr_lo = (wr_pad - wr_hi.astype(F32)).astype(BF16)
        br_pad = jnp.pad(b_router[l], (0, LANES - N_EXPERTS), constant_values=-1e30).reshape(1, LANES)
        x2, h2, idx_pad, gates_pad, rank_pad, cnt = _outproj(
            a_out, b_out, xc, w_out[l].astype(BF16), ln2_w[l].reshape(1, D), wr_hi, wr_lo, br_pad)
        ys, dest = _moe(h2, T, D, idx_pad[:, :TOP_K], rank_pad[:, :TOP_K], cnt[0, :N_EXPERTS],
                        w_gate[l], b_gate[l], w_up[l], b_up[l], w_down[l], b_down[l])
        if l + 1 < depth:
            raise NotImplementedError("only the final layer fuses the output norm")
        xc = _final(x2, ys, dest, gates_pad, final_norm_w.reshape(1, D))
    return xc.reshape(B, S, D)
```

```python
import functools

import jax
import jax.numpy as jnp
from jax import lax
from jax.experimental import pallas as pl
from jax.experimental.pallas import tpu as pltpu

F32 = jnp.float32
BF16 = jnp.bfloat16

EPS = 1e-6
HG_HEADS = 8
HG_DK = 128
ML_HEADS = 4
ML_DQK = 128
ML_DV = 256
CONV_W = 4
N_EXPERTS = 32
TOP_K = 4
SWIGLU_ALPHA = 1.702
SWIGLU_LIMIT = 7.0

LANES = 128
SUBLANES = 8
VMEM_LIMIT_BYTES = 56 * 1024 * 1024

HG_CHUNK = 64
HG_SUB = 16
ML_CHUNK = 128
HG_ROWS = 1024
ML_ROWS = 512
HG_HEADS_PER_STEP = 2

INPROJ_TM = 1024
INPROJ_TN = 1024
OUTPROJ_TM = 256
OUTPROJ_SUB = 128
DISPATCH_TOKENS = 256
MOE_TM = 512
MOE_TF = 1024
MOE_SUB_ROWS = 128
DMA_UNROLL = 32
DMA_QUEUES = 2
FINAL_TM = 128


def _dot_nt(a, b):
    return lax.dot_general(a, b, (((1,), (1,)), ((), ())), preferred_element_type=F32)


def _dot_tn(a, b):
    return lax.dot_general(a, b, (((0,), (0,)), ((), ())), preferred_element_type=F32)


def _log_sigmoid(z):
    return jnp.minimum(z, 0.0) - jnp.log1p(jnp.exp(-jnp.abs(z)))


def _sigmoid(z):
    return 1.0 / (1.0 + jnp.exp(-z))


def _cparams(semantics):
    return pltpu.CompilerParams(dimension_semantics=semantics, vmem_limit_bytes=VMEM_LIMIT_BYTES)


_HI_MASK = 0xFFFF0000


def _packed_rows(d):
    return d // (2 * LANES)


def _pack_store(o_ref, v):
    n, d = v.shape
    s, half = _packed_rows(d), d // 2
    bits = pltpu.bitcast(v.astype(BF16).astype(F32), jnp.uint32)
    for c in range(s):
        hi = bits[:, c * LANES:(c + 1) * LANES]
        lo = bits[:, half + c * LANES:half + (c + 1) * LANES]
        o_ref[pl.ds(c, n, stride=s), :] = hi | jnp.right_shift(lo, jnp.uint32(16))


def _unpack_load(buf, first_row, n, s, c):
    w = buf[pl.ds(first_row * s + c, n, stride=s), :]
    hi = pltpu.bitcast(w & jnp.uint32(_HI_MASK), F32)
    lo = pltpu.bitcast(jnp.left_shift(w, jnp.uint32(16)), F32)
    return hi, lo


def _inproj_body(*refs, tn, with_gates):
    if with_gates:
        x_ref, lnw_ref, w_ref, wg_ref, o_ref, g_ref, h_scr = refs
    else:
        x_ref, lnw_ref, w_ref, o_ref, h_scr = refs

    @pl.when(pl.program_id(1) == 0)
    def _():
        x = x_ref[...]
        h = x * lax.rsqrt(jnp.mean(x * x, axis=-1, keepdims=True) + EPS) * lnw_ref[...]
        hb = h.astype(BF16)
        h_scr[...] = hb
        if with_gates:
            g_ref[...] = jnp.dot(hb, wg_ref[...], preferred_element_type=F32)

    res = jnp.dot(h_scr[...], w_ref[...], preferred_element_type=F32)
    for c in range(tn // LANES):
        o_ref[c] = res[:, c * LANES:(c + 1) * LANES].astype(o_ref.dtype)


def _inproj(x2d, ln_w, w_bf, n_tiles, col_tile, out_dtype, w_gates_pad=None):
    T, D = x2d.shape
    tm = min(INPROJ_TM, T)
    tn = INPROJ_TN
    with_gates = w_gates_pad is not None
    in_specs = [
        pl.BlockSpec((tm, D), lambda i, j: (i, 0)),
        pl.BlockSpec((1, D), lambda i, j: (0, 0)),
        pl.BlockSpec((D, tn), lambda i, j: (0, col_tile(j))),
    ]
    out_specs = [pl.BlockSpec((tn // LANES, tm, LANES), lambda i, j: (j, i, 0))]
    out_shape = [jax.ShapeDtypeStruct((n_tiles * tn // LANES, T, LANES), out_dtype)]
    args = [x2d, ln_w, w_bf]
    if with_gates:
        in_specs.append(pl.BlockSpec((D, LANES), lambda i, j: (0, 0)))
        out_specs.append(pl.BlockSpec((tm, LANES), lambda i, j: (i, 0)))
        out_shape.append(jax.ShapeDtypeStruct((T, LANES), F32))
        args.append(w_gates_pad)
    return pl.pallas_call(
        functools.partial(_inproj_body, tn=tn, with_gates=with_gates),
        grid=(T // tm, n_tiles),
        in_specs=in_specs,
        out_specs=out_specs,
        out_shape=out_shape,
        scratch_shapes=[pltpu.VMEM((tm, D), BF16)],
        compiler_params=_cparams(("parallel", "arbitrary")),
        name="inproj_gates" if with_gates else "inproj",
    )(*args)


def _hgrn_body(q_ref, f_ref, i_ref, g_ref, lb_ref, nw_ref, o_ref, st_scr, *, rows, heads):
    C, SUB = HG_CHUNK, HG_SUB
    nsub = C // SUB

    @pl.when(pl.program_id(2) == 0)
    def _():
        st_scr[...] = jnp.zeros_like(st_scr)

    lbs = [lb_ref[hh] for hh in range(heads)]
    log_lbs = [jnp.log(lb) for lb in lbs]
    log_1mlbs = [jnp.log1p(-lb) for lb in lbs]

    r_i = lax.broadcasted_iota(jnp.int32, (C, C), 0)
    c_i = lax.broadcasted_iota(jnp.int32, (C, C), 1)
    tri = (c_i <= r_i).astype(F32)
    sub_shift = SUB.bit_length() - 1
    diag_mask = (c_i <= r_i) & (jnp.right_shift(r_i, sub_shift) == jnp.right_shift(c_i, sub_shift))
    row_id = lax.broadcasted_iota(jnp.int32, (C, HG_DK), 0)

    pairs = [(j, hh) for j in range(rows // C) for hh in range(heads)]
    tri_b = tri.astype(BF16)
    st = {p: {} for p in pairs}

    for p in pairs:
        j, hh = p
        z = f_ref[hh, pl.ds(j * C, C), :].astype(F32)
        e = jnp.exp(-jnp.abs(z))
        e1 = 1.0 + e
        log_sig = jnp.minimum(z, 0.0) - jnp.log(e1)
        sig_neg = jnp.where(z >= 0, e, 1.0) / e1
        cc = log_1mlbs[hh] + log_sig
        log_f = jnp.maximum(log_lbs[hh], cc) + jnp.log(1.0 + jnp.exp(-jnp.abs(log_lbs[hh] - cc)))
        st[p]["kk"] = (1.0 - lbs[hh]) * sig_neg
        hi = log_f.astype(BF16)
        lo = (log_f - hi.astype(F32)).astype(BF16)
        st[p]["b"] = (jnp.dot(tri_b, hi, preferred_element_type=F32)
                      + jnp.dot(tri_b, lo, preferred_element_type=F32))

    for p in pairs:
        j, hh = p
        q = q_ref[hh, pl.ds(j * C, C), :].astype(F32)
        kk, b = st[p]["kk"], st[p]["b"]
        b_last = b[C - 1:C, :]
        st[p]["dec"] = jnp.exp(b_last)
        st[p]["qe"] = (q * jnp.exp(b)).astype(BF16)
        st[p]["kdec"] = (kk * jnp.exp(b_last - b)).astype(BF16)
        refs = [b[I * SUB:I * SUB + 1, :] for I in range(nsub)]
        refb = jnp.concatenate([jnp.broadcast_to(r, (SUB, HG_DK)) for r in refs], axis=0)
        qd = (q * jnp.exp(b - refb)).astype(BF16)
        kd = (kk * jnp.exp(refb - b)).astype(BF16)
        q_parts, k_parts = [], []
        for J in range(nsub - 1):
            r = refs[J + 1]
            qj = q * jnp.exp(jnp.minimum(b - r, 0.0))
            kj = kk * jnp.exp(jnp.minimum(r - b, 0.0))
            q_parts.append(jnp.where(row_id >= (J + 1) * SUB, qj, 0.0).astype(BF16))
            k_parts.append(jnp.where((row_id >= J * SUB) & (row_id < (J + 1) * SUB), kj, 0.0).astype(BF16))
        st[p]["att_d"] = _dot_nt(qd, kd)
        st[p]["att_o"] = _dot_nt(jnp.concatenate(q_parts, axis=1), jnp.concatenate(k_parts, axis=1))

    for p in pairs:
        j, hh = p
        vb = i_ref[hh, pl.ds(j * C, C), :].astype(BF16)
        att = jnp.where(diag_mask, st[p]["att_d"], 0.0) + st[p]["att_o"]
        st[p]["intra"] = jnp.dot(att.astype(BF16), vb, preferred_element_type=F32)
        st[p]["upd"] = _dot_tn(vb, st[p]["kdec"])

    for hh in range(heads):
        s = st_scr[hh]
        for j in range(rows // C):
            st[(j, hh)]["s_in"] = s.astype(BF16)
            s = s * st[(j, hh)]["dec"] + st[(j, hh)]["upd"]
        st_scr[hh] = s

    for p in pairs:
        j, hh = p
        g = g_ref[hh, pl.ds(j * C, C), :].astype(F32)
        o = _dot_nt(st[p]["qe"], st[p]["s_in"]) + st[p]["intra"]
        y = o * lax.rsqrt(jnp.mean(o * o, axis=-1, keepdims=True) + EPS) * nw_ref[hh]
        o_ref[pl.ds(j * C, C), hh * LANES:(hh + 1) * LANES] = (y * (g * _sigmoid(g))).astype(o_ref.dtype)


def _hgrn(proj_b, proj_f, lb, norm_w, batch, seq):
    T = batch * seq
    rows = min(HG_ROWS, seq)
    nblk = seq // rows
    H = HG_HEADS
    hp = HG_HEADS_PER_STEP

    def slab(off):
        return pl.BlockSpec((hp, rows, LANES), lambda b, h, c, off=off: (off // hp + h, b * nblk + c, 0))

    vec = pl.BlockSpec((hp, 1, LANES), lambda b, h, c: (h, 0, 0))
    return pl.pallas_call(
        functools.partial(_hgrn_body, rows=rows, heads=hp),
        grid=(batch, H // hp, nblk),
        in_specs=[slab(0), slab(0), slab(H), slab(2 * H), vec, vec],
        out_specs=pl.BlockSpec((rows, hp * LANES), lambda b, h, c: (b * nblk + c, h)),
        out_shape=jax.ShapeDtypeStruct((T, H * LANES), BF16),
        scratch_shapes=[pltpu.VMEM((hp, LANES, HG_DK), F32)],
        compiler_params=_cparams(("parallel", "parallel", "arbitrary")),
        name="hgrn2",
    )(proj_b, proj_f, proj_b, proj_b, lb.reshape(H, 1, HG_DK), norm_w.reshape(H, 1, LANES))


def _mlstm_body(q_ref, k_ref, v_ref, og_ref, gt_ref, gb_ref, cwq_ref, cwk_ref, cbq_ref, cbk_ref, nw_ref,
                out_ref, c_scr, n_scr, m_scr, qx_scr, kx_scr, qc_scr, kc_scr, *, rows):
    C = ML_CHUNK
    PAD = SUBLANES
    h = pl.program_id(1)

    @pl.when(pl.program_id(2) == 0)
    def _():
        c_scr[...] = jnp.zeros_like(c_scr)
        n_scr[...] = jnp.zeros_like(n_scr)
        m_scr[...] = jnp.zeros_like(m_scr)
        qx_scr[0:PAD, :] = jnp.zeros((PAD, LANES), F32)
        kx_scr[0:PAD, :] = jnp.zeros((PAD, LANES), F32)

    qx_scr[PAD:PAD + rows, :] = q_ref[0].astype(F32)
    kx_scr[PAD:PAD + rows, :] = k_ref[0].astype(F32)
    accq = jnp.zeros((rows, LANES), F32) + cbq_ref[...]
    acck = jnp.zeros((rows, LANES), F32) + cbk_ref[...]
    for j in range(CONV_W):
        off = PAD - (CONV_W - 1) + j
        accq = accq + cwq_ref[j:j + 1, :] * qx_scr[pl.ds(off, rows), :]
        acck = acck + cwk_ref[j:j + 1, :] * kx_scr[pl.ds(off, rows), :]
    qc_scr[...] = accq * _sigmoid(accq) * (ML_DQK ** -0.5)
    kc_scr[...] = acck * _sigmoid(acck)
    qx_scr[0:PAD, :] = qx_scr[rows:rows + PAD, :]
    kx_scr[0:PAD, :] = kx_scr[rows:rows + PAD, :]

    lane = lax.broadcasted_iota(jnp.int32, (C, C), 1)
    sub = lax.broadcasted_iota(jnp.int32, (C, C), 0)
    causal = lane <= sub
    lower = causal.astype(F32)
    upper = (sub <= lane).astype(F32)
    nw = nw_ref[...]
    gbias = gb_ref[...]

    chunks = range(rows // C)
    lower_b, upper_b = lower.astype(BF16), upper.astype(BF16)
    st = [{} for _ in chunks]

    def split(x):
        hi = x.astype(BF16)
        return hi, (x - hi.astype(F32)).astype(BF16)

    for j in chunks:
        gc = gt_ref[pl.ds(j * C, C), :] + gbias
        gct = gc.T
        li_col = jnp.sum(jnp.where(lane == h, gc, 0.0), axis=1, keepdims=True)
        fg_col = jnp.sum(jnp.where(lane == h + ML_HEADS, gc, 0.0), axis=1, keepdims=True)
        li_row = jnp.sum(jnp.where(sub == h, gct, 0.0), axis=0, keepdims=True)
        fg_row = jnp.sum(jnp.where(sub == h + ML_HEADS, gct, 0.0), axis=0, keepdims=True)
        ch, cl = split(_log_sigmoid(fg_col))
        rh, rl = split(_log_sigmoid(fg_row))
        g_t = (jnp.dot(lower_b, jnp.broadcast_to(ch, (C, C)), preferred_element_type=F32)
               + jnp.dot(lower_b, jnp.broadcast_to(cl, (C, C)), preferred_element_type=F32))
        g_s = (jnp.dot(jnp.broadcast_to(rh, (C, C)), upper_b, preferred_element_type=F32)
               + jnp.dot(jnp.broadcast_to(rl, (C, C)), upper_b, preferred_element_type=F32))
        st[j].update(g_t=g_t, g_s=g_s, li_col=li_col, li_row=li_row)

    for j in chunks:
        c = st[j]
        g_col = c["g_t"][:, 0:1]
        g_last = c["g_t"][C - 1:C, 0:1]
        dmat = jnp.where(causal, c["g_t"] - c["g_s"] + c["li_row"], -jnp.inf)
        log_ws = g_last - g_col + c["li_col"]
        qf = qc_scr[pl.ds(j * C, C), :]
        kf = kc_scr[pl.ds(j * C, C), :]
        qb = qf.astype(BF16)
        c.update(g_col=g_col, g_last=g_last, dmat=dmat, dmax=jnp.max(dmat, axis=1, keepdims=True),
                 log_ws=log_ws, ws_max=jnp.max(log_ws, axis=0, keepdims=True), qf=qf, kf=kf, qb=qb,
                 qk=_dot_nt(qb, kf.astype(BF16)))

    m = m_scr[:, 0:1]
    for j in chunks:
        c = st[j]
        m_new = jnp.maximum(c["g_last"] + m, c["ws_max"])
        c.update(m_prev=m, m_new=m_new, decay=jnp.exp(c["g_last"] + m - m_new))
        m = m_new
    m_scr[...] = jnp.broadcast_to(m, m_scr.shape)

    for j in chunks:
        c = st[j]
        vb = jnp.concatenate([v_ref[0, pl.ds(j * C, C), :], v_ref[1, pl.ds(j * C, C), :]], axis=1).astype(BF16)
        kw = c["kf"] * jnp.exp(c["log_ws"] - c["m_new"])
        c.update(vb=vb, upd=_dot_tn(kw.astype(BF16), vb), ksum=jnp.sum(kw, axis=0, keepdims=True))

    cm = c_scr[...]
    nv = n_scr[...]
    for j in chunks:
        c = st[j]
        c.update(c_in=cm.astype(BF16), n_in=nv)
        cm = c["decay"] * cm + c["upd"]
        nv = c["decay"] * nv + c["ksum"]
    c_scr[...] = cm
    n_scr[...] = nv

    for j in chunks:
        c = st[j]
        a_inter = c["g_col"] + c["m_prev"]
        m_t = jnp.maximum(a_inter, c["dmax"])
        w_inter = jnp.exp(a_inter - m_t)
        sqk = c["qk"] * jnp.exp(c["dmat"] - m_t)
        num = (w_inter * jnp.dot(c["qb"], c["c_in"], preferred_element_type=F32)
               + jnp.dot(sqk.astype(BF16), c["vb"], preferred_element_type=F32))
        den = (w_inter * jnp.sum(c["qf"] * c["n_in"], axis=1, keepdims=True)
               + jnp.sum(sqk, axis=1, keepdims=True))
        hh = num * (1.0 / jnp.maximum(jnp.abs(den), jnp.exp(-m_t)))
        y = hh * lax.rsqrt(jnp.mean(hh * hh, axis=-1, keepdims=True) + EPS) * nw
        og = jnp.concatenate([og_ref[0, pl.ds(j * C, C), :], og_ref[1, pl.ds(j * C, C), :]], axis=1).astype(F32)
        out_ref[pl.ds(j * C, C), :] = (y * _sigmoid(og)).astype(out_ref.dtype)


def _mlstm(proj3, gates, gate_bias_pad, conv_w, conv_b, norm_w, batch, seq, q_off):
    T = batch * seq
    rows = min(ML_ROWS, seq)
    nblk = seq // rows
    H = ML_HEADS
    k_off = q_off + H
    v_off = k_off + H
    o_off = v_off + 2 * H

    def slab(off):
        return pl.BlockSpec((1, rows, LANES), lambda b, h, c, off=off: (off + h, b * nblk + c, 0))

    def slab2(off):
        return pl.BlockSpec((2, rows, LANES), lambda b, h, c, off=off: (off // 2 + h, b * nblk + c, 0))

    qk_w = H * ML_DQK
    return pl.pallas_call(
        functools.partial(_mlstm_body, rows=rows),
        grid=(batch, H, nblk),
        in_specs=[
            slab(q_off), slab(k_off), slab2(v_off), slab2(o_off),
            pl.BlockSpec((rows, LANES), lambda b, h, c: (b * nblk + c, 0)),
            pl.BlockSpec((1, LANES), lambda b, h, c: (0, 0)),
            pl.BlockSpec((CONV_W, LANES), lambda b, h, c: (0, h)),
            pl.BlockSpec((CONV_W, LANES), lambda b, h, c: (0, H + h)),
            pl.BlockSpec((1, LANES), lambda b, h, c: (0, h)),
            pl.BlockSpec((1, LANES), lambda b, h, c: (0, H + h)),
            pl.BlockSpec((1, ML_DV), lambda b, h, c: (0, h)),
        ],
        out_specs=pl.BlockSpec((rows, ML_DV), lambda b, h, c: (b * nblk + c, h)),
        out_shape=jax.ShapeDtypeStruct((T, H * ML_DV), BF16),
        scratch_shapes=[
            pltpu.VMEM((ML_DQK, ML_DV), F32),
            pltpu.VMEM((1, ML_DQK), F32),
            pltpu.VMEM((1, LANES), F32),
            pltpu.VMEM((rows + 2 * SUBLANES, LANES), F32),
            pltpu.VMEM((rows + 2 * SUBLANES, LANES), F32),
            pltpu.VMEM((rows, LANES), F32),
            pltpu.VMEM((rows, LANES), F32),
        ],
        compiler_params=_cparams(("parallel", "parallel", "arbitrary")),
        name="mlstm",
    )(proj3, proj3, proj3, proj3, gates, gate_bias_pad, conv_w, conv_w,
      conv_b.reshape(1, 2 * qk_w), conv_b.reshape(1, 2 * qk_w), norm_w.reshape(1, H * ML_DV))


def _outproj_body(a_ref, b_ref, x_ref, wo_ref, ln_ref, wrh_ref, wrl_ref, br_ref,
                  x2_ref, h2_ref, idx_ref, gate_ref, rank_ref, cnt_ref, cnt_scr, *, sub_rows):
    tm = x_ref.shape[0]
    ka = a_ref.shape[1]
    s = _packed_rows(x_ref.shape[1])

    @pl.when(pl.program_id(0) == 0)
    def _():
        cnt_scr[...] = jnp.zeros_like(cnt_scr)

    lane = lax.broadcasted_iota(jnp.int32, (sub_rows, LANES), 1).astype(F32)
    onehots = [[] for _ in range(TOP_K)]
    for r0 in range(0, tm, sub_rows):
        rows = pl.ds(r0, sub_rows)
        res = (jnp.dot(a_ref[rows, :], wo_ref[0:ka, :], preferred_element_type=F32)
               + jnp.dot(b_ref[rows, :], wo_ref[ka:, :], preferred_element_type=F32))
        x2 = x_ref[rows, :] + res
        x2_ref[rows, :] = x2
        h2 = x2 * lax.rsqrt(jnp.mean(x2 * x2, axis=-1, keepdims=True) + EPS) * ln_ref[...]
        _pack_store(h2_ref.at[pl.ds(r0 * s, sub_rows * s), :], h2)

        h_hi = h2.astype(BF16)
        h_lo = (h2 - h_hi.astype(F32)).astype(BF16)
        logits = (jnp.dot(h_hi, wrh_ref[...], preferred_element_type=F32)
                  + jnp.dot(h_lo, wrh_ref[...], preferred_element_type=F32)
                  + jnp.dot(h_hi, wrl_ref[...], preferred_element_type=F32)) + br_ref[...]
        vals, idxs = [], []
        cur = logits
        for _ in range(TOP_K):
            m = jnp.max(cur, axis=1, keepdims=True)
            ix = jnp.min(jnp.where(cur == m, lane, float(LANES)), axis=1, keepdims=True)
            vals.append(m)
            idxs.append(ix)
            cur = jnp.where(lane == ix, -jnp.inf, cur)
        es = [jnp.exp(v - vals[0]) for v in vals]
        inv = 1.0 / (es[0] + es[1] + es[2] + es[3])
        gate = jnp.zeros(logits.shape, F32)
        idx = jnp.zeros(logits.shape, F32)
        for k in range(TOP_K):
            gate = jnp.where(lane == float(k), es[k] * inv, gate)
            idx = jnp.where(lane == float(k), idxs[k], idx)
            onehots[k].append((lane == idxs[k]).astype(F32))
        gate_ref[rows, :] = gate
        idx_ref[rows, :] = idx.astype(jnp.int32)

    oh_k = [jnp.concatenate(o, axis=0) for o in onehots]
    oh = oh_k[0] + oh_k[1] + oh_k[2] + oh_k[3]
    r_i = lax.broadcasted_iota(jnp.int32, (tm, tm), 0)
    c_i = lax.broadcasted_iota(jnp.int32, (tm, tm), 1)
    before = jnp.dot((c_i < r_i).astype(BF16), oh.astype(BF16), preferred_element_type=F32) + cnt_scr[...]
    lane_t = lax.broadcasted_iota(jnp.int32, (tm, LANES), 1)
    rank = jnp.zeros((tm, LANES), F32)
    for k in range(TOP_K):
        rank = jnp.where(lane_t == k, jnp.sum(oh_k[k] * before, axis=1, keepdims=True), rank)
    rank_ref[...] = rank.astype(jnp.int32)
    cnt = cnt_scr[...] + jnp.sum(oh, axis=0, keepdims=True)
    cnt_scr[...] = cnt
    cnt_ref[...] = cnt.astype(jnp.int32)


def _outproj(a_out, b_out, x2d, w_out_bf, ln_w, wr_hi, wr_lo, b_router_pad):
    T, D = x2d.shape
    tm = min(OUTPROJ_TM, T)
    ka, kb = a_out.shape[1], b_out.shape[1]
    s = _packed_rows(D)
    row = lambda w: pl.BlockSpec((tm, w), lambda i: (i, 0))
    full = lambda r, c: pl.BlockSpec((r, c), lambda i: (0, 0))
    return pl.pallas_call(
        functools.partial(_outproj_body, sub_rows=min(OUTPROJ_SUB, tm)),
        grid=(T // tm,),
        in_specs=[row(ka), row(kb), row(D), full(ka + kb, D), full(1, D), full(D, LANES), full(D, LANES),
                  full(1, LANES)],
        out_specs=[row(D), pl.BlockSpec((tm * s, LANES), lambda i: (i, 0)), row(LANES), row(LANES), row(LANES),
                   full(1, LANES)],
        out_shape=[
            jax.ShapeDtypeStruct((T, D), F32),
            jax.ShapeDtypeStruct((T * s, LANES), jnp.uint32),
            jax.ShapeDtypeStruct((T, LANES), jnp.int32),
            jax.ShapeDtypeStruct((T, LANES), F32),
            jax.ShapeDtypeStruct((T, LANES), jnp.int32),
            jax.ShapeDtypeStruct((1, LANES), jnp.int32),
        ],
        scratch_shapes=[pltpu.VMEM((1, LANES), F32)],
        compiler_params=_cparams(("arbitrary",)),
        name="outproj_router",
    )(a_out, b_out, x2d, w_out_bf, ln_w, wr_hi, wr_lo, b_router_pad)


def _start_row_gather(idx_at, src_ref, buf, sem, n, s):
    def start(pair, carry):
        for p in range(DMA_QUEUES):
            i = pair * DMA_QUEUES + p
            src_row = pl.multiple_of(idx_at(i) * s, s)
            dst_row = pl.multiple_of(i * s, s)
            pltpu.make_async_copy(src_ref.at[pl.ds(src_row, s), :], buf.at[pl.ds(dst_row, s), :],
                                  sem).start(priority=p)
        return carry

    lax.fori_loop(0, n // DMA_QUEUES, start, 0, unroll=DMA_UNROLL // DMA_QUEUES)


def _wait_row_gather(buf, sem):
    pltpu.make_async_copy(buf, buf, sem).wait()


def _dispatch_body(zero_ref, dest_ref, h_ref, xs_ref, zbuf, sem, zsem, *, ntok, s, tm, n_blocks):
    blk = tm * s

    def zero_copy(b):
        return pltpu.make_async_copy(zbuf, xs_ref.at[pl.ds(pl.multiple_of(b * blk, blk), blk), :], zsem)

    @pl.when(pl.program_id(0) == 0)
    def _():
        zbuf[...] = jnp.zeros_like(zbuf)

        def zstart(b, carry):
            @pl.when(zero_ref[b] == 1)
            def _():
                zero_copy(b).start()
            return carry

        def zwait(b, carry):
            @pl.when(zero_ref[b] == 1)
            def _():
                zero_copy(b).wait()
            return carry

        lax.fori_loop(0, n_blocks, zstart, 0)
        lax.fori_loop(0, n_blocks, zwait, 0)

    def start(t, carry):
        src = h_ref.at[pl.ds(pl.multiple_of(t * s, s), s), :]
        for k in range(TOP_K):
            dst_row = pl.multiple_of(dest_ref[0, 0, t * TOP_K + k] * s, s)
            pltpu.make_async_copy(src, xs_ref.at[pl.ds(dst_row, s), :], sem).start(priority=k % DMA_QUEUES)
        return carry

    lax.fori_loop(0, ntok, start, 0, unroll=DMA_UNROLL // TOP_K)
    for _ in range(TOP_K):
        pltpu.make_async_copy(h_ref, h_ref, sem).wait()


def _dispatch(dest, zero_blk, h_packed, d, n_slots, tm):
    s = _packed_rows(d)
    T = h_packed.shape[0] // s
    ntok = min(DISPATCH_TOKENS, T)
    n = ntok * TOP_K
    grid_spec = pltpu.PrefetchScalarGridSpec(
        num_scalar_prefetch=1,
        grid=(T // ntok,),
        in_specs=[
            pl.BlockSpec((1, 1, n), lambda i, z: (i, 0, 0), memory_space=pltpu.SMEM),
            pl.BlockSpec((ntok * s, LANES), lambda i, z: (i, 0)),
        ],
        out_specs=pl.BlockSpec(memory_space=pl.ANY),
        scratch_shapes=[pltpu.VMEM((tm * s, LANES), jnp.uint32), pltpu.SemaphoreType.DMA,
                        pltpu.SemaphoreType.DMA],
    )
    return pl.pallas_call(
        functools.partial(_dispatch_body, ntok=ntok, s=s, tm=tm, n_blocks=zero_blk.shape[0]),
        grid_spec=grid_spec,
        out_shape=jax.ShapeDtypeStruct((n_slots * s, LANES), jnp.uint32),
        compiler_params=_cparams(("arbitrary",)),
        name="moe_dispatch",
    )(zero_blk, dest.reshape(T // ntok, 1, n), h_packed)


def _staged_weights(sched_refs, hbm_refs, stage_refs, bf_refs, sem):
    e_ref, wt_ref, first_ref, ne_ref, nt_ref, more_ref = sched_refs
    w = pl.program_id(0)
    tn = stage_refs[0].shape[1]

    def copies(e, t):
        col = pl.multiple_of(t * tn, tn)
        return [pltpu.make_async_copy(h.at[e, :, pl.ds(col, tn)], st, sem.at[k])
                for k, (h, st) in enumerate(zip(hbm_refs, stage_refs))]

    def start_all(cs):
        for c in cs:
            c.start()

    @pl.when(w == 0)
    def _():
        start_all(copies(e_ref[0], wt_ref[0]))

    @pl.when(first_ref[w] == 1)
    def _():
        for c in copies(e_ref[w], wt_ref[w]):
            c.wait()
        for st, bf in zip(stage_refs, bf_refs):
            for r in range(0, st.shape[0], 256):
                bf[r:r + 256, :] = st[r:r + 256, :].astype(BF16)

        @pl.when(more_ref[w] == 1)
        def _():
            start_all(copies(ne_ref[w], nt_ref[w]))


def _row_groups(nrows, tm, compute, clear):
    sub = min(MOE_SUB_ROWS, tm)
    groups = tm // sub
    for live in range(groups + 1):
        @pl.when((nrows > (live - 1) * sub) & (nrows <= live * sub))
        def _(live=live):
            for g in range(live):
                compute(g * sub, sub)
            for g in range(live, groups):
                clear(g * sub, sub)


def _moe_up_body(e_ref, wt_ref, r_ref, ot_ref, first_ref, nrows_ref, ne_ref, nt_ref, more_ref,
                 x_ref, wg_ref, wu_ref, bg_ref, bu_ref, o_ref, wg_stage, wu_stage, wg_scr, wu_scr, x_scr, sem):
    w = pl.program_id(0)
    _staged_weights((e_ref, wt_ref, first_ref, ne_ref, nt_ref, more_ref), (wg_ref, wu_ref),
                    (wg_stage, wu_stage), (wg_scr, wu_scr), sem)
    tm, d = x_scr.shape
    s = _packed_rows(d)

    def compute(r0, n):
        rows = pl.ds(r0, n)
        for c in range(s):
            hi, lo = _unpack_load(x_ref, r0, n, s, c)
            x_scr[rows, c * LANES:(c + 1) * LANES] = hi.astype(BF16)
            x_scr[rows, d // 2 + c * LANES:d // 2 + (c + 1) * LANES] = lo.astype(BF16)
        x = x_scr[rows, :]
        gt = jnp.dot(x, wg_scr[...], preferred_element_type=F32) + bg_ref[0]
        up = jnp.dot(x, wu_scr[...], preferred_element_type=F32) + bu_ref[0]
        gt = jnp.minimum(gt, SWIGLU_LIMIT)
        up = jnp.clip(up, -SWIGLU_LIMIT, SWIGLU_LIMIT)
        o_ref[rows, :] = ((up + 1.0) * (gt * _sigmoid(SWIGLU_ALPHA * gt))).astype(o_ref.dtype)

    def clear(r0, n):
        o_ref[pl.ds(r0, n), :] = jnp.zeros((n, o_ref.shape[1]), o_ref.dtype)

    _row_groups(nrows_ref[w], tm, compute, clear)


def _moe_down_body(e_ref, wt_ref, r_ref, ot_ref, first_ref, nrows_ref, ne_ref, nt_ref, more_ref,
                   a_ref, wd_ref, bd_ref, o_ref, wd_stage, wd_scr, sem):
    w = pl.program_id(0)
    _staged_weights((e_ref, wt_ref, first_ref, ne_ref, nt_ref, more_ref), (wd_ref,), (wd_stage,), (wd_scr,), sem)
    tm = a_ref.shape[0]
    s = o_ref.shape[0] // tm

    def compute(r0, n):
        y = jnp.dot(a_ref[pl.ds(r0, n), :], wd_scr[...], preferred_element_type=F32) + bd_ref[0]
        _pack_store(o_ref.at[pl.ds(r0 * s, n * s), :], y)

    def clear(r0, n):
        o_ref[pl.ds(r0 * s, n * s), :] = jnp.zeros((n * s, LANES), o_ref.dtype)

    _row_groups(nrows_ref[w], tm, compute, clear)


def _moe_schedule(counts, tm, n_tiles, n_blocks):
    n_items = n_tiles * n_blocks
    experts = jnp.arange(N_EXPERTS, dtype=jnp.int32)
    blocks_e = (counts + tm - 1) // tm
    bend = jnp.cumsum(blocks_e)
    bstart = bend - blocks_e
    item_end = n_tiles * bend
    total = item_end[-1]
    later = (experts[None, :] > experts[:, None]) & (blocks_e[None, :] > 0)
    next_e = jnp.min(jnp.where(later, experts[None, :], N_EXPERTS - 1), axis=1)

    w = jnp.arange(n_items, dtype=jnp.int32)
    valid = w < total
    wc = jnp.minimum(w, jnp.maximum(total - 1, 0))
    e = jnp.minimum(jnp.sum((item_end[None, :] <= wc[:, None]).astype(jnp.int32), axis=1), N_EXPERTS - 1)
    sel = e[:, None] == experts[None, :]
    pick = lambda table: jnp.sum(jnp.where(sel, table[None, :], 0), axis=1)
    nb = jnp.maximum(pick(blocks_e), 1)
    local = wc - n_tiles * pick(bstart)
    wtile = sum((local >= t * nb).astype(jnp.int32) for t in range(1, n_tiles)) if n_tiles > 1 else 0 * local
    jblk = local - wtile * nb
    spare = jnp.maximum(w - total, 0)
    rblk = jnp.where(valid, pick(bstart) + jblk, bend[-1] + spare // n_tiles)
    otile = jnp.where(valid, wtile, spare % n_tiles)
    nrows = jnp.where(valid, jnp.clip(pick(counts) - jblk * tm, 0, tm), 0)
    first = (jblk == 0) & valid
    last_tile = wtile == n_tiles - 1
    more = first & (w + nb < total)
    i32 = lambda a: a.astype(jnp.int32)
    return (e, i32(wtile), i32(rblk), i32(otile), i32(first), i32(nrows),
            i32(jnp.where(last_tile, pick(next_e), e)), i32(jnp.where(last_tile, 0, wtile + 1)), i32(more))


def _moe_up(sched, xs_packed, w_gate, w_up, b_gate, b_up):
    D, d_ff = w_gate.shape[1], w_gate.shape[2]
    s = _packed_rows(D)
    n_slots = xs_packed.shape[0] // s
    tm, tf = MOE_TM, min(MOE_TF, d_ff)
    n_items = sched[0].shape[0]
    wspec = pl.BlockSpec(memory_space=pl.ANY)
    bspec = pl.BlockSpec((1, 1, tf), lambda w, e, wt, r, ot, *_: (e[w], 0, wt[w]))
    grid_spec = pltpu.PrefetchScalarGridSpec(
        num_scalar_prefetch=len(sched),
        grid=(n_items,),
        in_specs=[pl.BlockSpec((tm * s, LANES), lambda w, e, wt, r, ot, *_: (r[w], 0)),
                  wspec, wspec, bspec, bspec],
        out_specs=pl.BlockSpec((tm, tf), lambda w, e, wt, r, ot, *_: (r[w], ot[w])),
        scratch_shapes=[pltpu.VMEM((D, tf), F32), pltpu.VMEM((D, tf), F32),
                        pltpu.VMEM((D, tf), BF16), pltpu.VMEM((D, tf), BF16), pltpu.VMEM((tm, D), BF16),
                        pltpu.SemaphoreType.DMA((2,))],
    )
    return pl.pallas_call(
        _moe_up_body,
        grid_spec=grid_spec,
        out_shape=jax.ShapeDtypeStruct((n_slots, d_ff), BF16),
        compiler_params=_cparams(("arbitrary",)),
        name="moe_up",
    )(*sched, xs_packed, w_gate, w_up, b_gate.reshape(N_EXPERTS, 1, d_ff), b_up.reshape(N_EXPERTS, 1, d_ff))


def _moe_down(sched, act, w_down, b_down):
    n_slots, d_ff = act.shape
    D = w_down.shape[2]
    tm = MOE_TM
    s = _packed_rows(D)
    n_items = sched[0].shape[0]
    grid_spec = pltpu.PrefetchScalarGridSpec(
        num_scalar_prefetch=len(sched),
        grid=(n_items,),
        in_specs=[
            pl.BlockSpec((tm, d_ff), lambda w, e, wt, r, ot, *_: (r[w], 0)),
            pl.BlockSpec(memory_space=pl.ANY),
            pl.BlockSpec((1, 1, D), lambda w, e, wt, r, ot, *_: (e[w], 0, 0)),
        ],
        out_specs=pl.BlockSpec((tm * s, LANES), lambda w, e, wt, r, ot, *_: (r[w], 0)),
        scratch_shapes=[pltpu.VMEM((d_ff, D), F32), pltpu.VMEM((d_ff, D), BF16), pltpu.SemaphoreType.DMA((1,))],
    )
    return pl.pallas_call(
        _moe_down_body,
        grid_spec=grid_spec,
        out_shape=jax.ShapeDtypeStruct((n_slots * s, LANES), jnp.uint32),
        compiler_params=_cparams(("arbitrary",)),
        name="moe_down",
    )(*sched, act, w_down, b_down.reshape(N_EXPERTS, 1, D))


def _final_body(dest_ref, next_ref, x2_ref, gate_ref, w_ref, ys_ref, o_ref, buf0, buf1, sem0, sem1, *, tm, s):
    n = tm * TOP_K
    i = pl.program_id(0)

    def gather(idx_ref, tile, buf, sem):
        _start_row_gather(lambda r: idx_ref[0, 0, tile * n + r], ys_ref, buf, sem, n, s)

    def combine(tile, buf):
        rows = pl.ds(tile * tm, tm)
        gate = gate_ref[rows, :]
        half = o_ref.shape[1] // 2
        o_ref[rows, :] = x2_ref[rows, :]
        for k in range(TOP_K):
            g = gate[:, k:k + 1]
            for c in range(s):
                hi, lo = _unpack_load(buf, k * tm, tm, s, c)
                o_ref[rows, c * LANES:(c + 1) * LANES] += g * hi
                o_ref[rows, half + c * LANES:half + (c + 1) * LANES] += g * lo
        acc = o_ref[rows, :]
        o_ref[rows, :] = acc * lax.rsqrt(jnp.mean(acc * acc, axis=-1, keepdims=True) + EPS) * w_ref[...]

    @pl.when(i == 0)
    def _():
        gather(dest_ref, 0, buf0, sem0)

    gather(dest_ref, 1, buf1, sem1)
    _wait_row_gather(buf0, sem0)
    combine(0, buf0)
    gather(next_ref, 0, buf0, sem0)
    _wait_row_gather(buf1, sem1)
    combine(1, buf1)

    @pl.when(i == pl.num_programs(0) - 1)
    def _():
        _wait_row_gather(buf0, sem0)


def _final(x2, ys_packed, dest, gates_pad, w):
    T, D = x2.shape
    tm = min(FINAL_TM, T // 2)
    s = _packed_rows(D)
    n = tm * TOP_K
    steps = T // (2 * tm)
    dest_km = dest.reshape(steps, 2, tm, TOP_K).transpose(0, 1, 3, 2).reshape(steps, 1, 2 * n)
    buf = pltpu.VMEM((n * s, LANES), jnp.uint32)
    return pl.pallas_call(
        functools.partial(_final_body, tm=tm, s=s),
        grid=(steps,),
        in_specs=[
            pl.BlockSpec((1, 1, 2 * n), lambda i: (i, 0, 0), memory_space=pltpu.SMEM),
            pl.BlockSpec((1, 1, 2 * n), lambda i: (jnp.minimum(i + 1, steps - 1), 0, 0), memory_space=pltpu.SMEM),
            pl.BlockSpec((2 * tm, D), lambda i: (i, 0)),
            pl.BlockSpec((2 * tm, LANES), lambda i: (i, 0)),
            pl.BlockSpec((1, D), lambda i: (0, 0)),
            pl.BlockSpec(memory_space=pl.ANY),
        ],
        out_specs=pl.BlockSpec((2 * tm, D), lambda i: (i, 0)),
        out_shape=jax.ShapeDtypeStruct((T, D), F32),
        scratch_shapes=[buf, buf, pltpu.SemaphoreType.DMA, pltpu.SemaphoreType.DMA],
        compiler_params=_cparams(("arbitrary",)),
        name="final_norm",
    )(dest_km, dest_km, x2, gates_pad, w, ys_packed)


def _moe(h2_packed, T, D, top_idx, rank, counts, w_gate, b_gate, w_up, b_up, w_down, b_down):
    A = T * TOP_K
    tm = MOE_TM
    n_blocks = (A + N_EXPERTS * (tm - 1) + tm - 1) // tm
    n_slots = n_blocks * tm

    blocks_e = (counts + tm - 1) // tm
    bend = jnp.cumsum(blocks_e)
    bstart = bend - blocks_e
    experts = jnp.arange(N_EXPERTS, dtype=jnp.int32)
    first_slot = jnp.sum(jnp.where(top_idx[:, :, None] == experts, bstart * tm, 0), axis=-1)
    dest = (first_slot + rank).astype(jnp.int32).reshape(A)
    blk = jnp.arange(n_blocks, dtype=jnp.int32)
    is_last = jnp.any((blk[:, None] == bend[None, :] - 1) & (blocks_e[None, :] > 0), axis=1)
    zero_blk = ((blk >= bend[-1]) | is_last).astype(jnp.int32)

    xs = _dispatch(dest, zero_blk, h2_packed, D, n_slots, tm)
    d_ff = w_gate.shape[2]
    act = _moe_up(_moe_schedule(counts, tm, -(-d_ff // MOE_TF), n_blocks), xs, w_gate, w_up, b_gate, b_up)
    ys = _moe_down(_moe_schedule(counts, tm, 1, n_blocks), act, w_down, b_down)
    return ys, dest


def kernel(x, ln1_w, w_in, hg_lb_logits, hg_norm_w, ml_conv_w, ml_conv_b, ml_igate_b, ml_fgate_b, ml_norm_w,
           w_out, ln2_w, w_router, b_router, w_gate, b_gate, w_up, b_up, w_down, b_down, final_norm_w):
    B, S, D = x.shape
    T = B * S
    depth = w_in.shape[0]
    hg_w = HG_HEADS * HG_DK
    n_main = 4 * hg_w + 2 * ML_HEADS * ML_DQK + 2 * ML_HEADS * ML_DV
    lb_all = jnp.cumsum(jax.nn.softmax(hg_lb_logits.astype(F32), axis=0), axis=0)

    xc = x.reshape(T, D)
    for l in range(depth):
        w_gates_pad = jnp.pad(w_in[l][:, n_main:], ((0, 0), (0, LANES - 2 * ML_HEADS))).astype(BF16)
        w_bf = w_in[l].astype(BF16)
        ln1 = ln1_w[l].reshape(1, D)
        hf_tiles = hg_w // INPROJ_TN
        proj_f = _inproj(xc, ln1, w_bf, hf_tiles, lambda j: j + hf_tiles, F32)[0]
        proj_b, gates = _inproj(xc, ln1, w_bf, n_main // INPROJ_TN - hf_tiles,
                                lambda j: jnp.where(j >= hf_tiles, j + hf_tiles, j), BF16, w_gates_pad)
        a_out = _hgrn(proj_b, proj_f, lb_all[l], hg_norm_w[l], B, S)
        gate_bias = jnp.pad(jnp.concatenate([ml_igate_b[l], ml_fgate_b[l]]), (0, LANES - 2 * ML_HEADS))
        b_out = _mlstm(proj_b, gates, gate_bias.reshape(1, LANES), ml_conv_w[l], ml_conv_b[l], ml_norm_w[l],
                       B, S, 3 * HG_HEADS)
        wr_pad = jnp.pad(w_router[l], ((0, 0), (0, LANES - N_EXPERTS)))
        wr_hi = wr_pad.astype(BF16)
        wr_lo = (wr_pad - wr_hi.astype(F32)).astype(BF16)
        br_pad = jnp.pad(b_router[l], (0, LANES - N_EXPERTS), constant_values=-1e30).reshape(1, LANES)
        x2, h2, idx_pad, gates_pad, rank_pad, cnt = _outproj(
            a_out, b_out, xc, w_out[l].astype(BF16), ln2_w[l].reshape(1, D), wr_hi, wr_lo, br_pad)
        ys, dest = _moe(h2, T, D, idx_pad[:, :TOP_K], rank_pad[:, :TOP_K], cnt[0, :N_EXPERTS],
                        w_gate[l], b_gate[l], w_up[l], b_up[l], w_down[l], b_down[l])
        if l + 1 < depth:
            raise NotImplementedError("only the final layer fuses the output norm")
        xc = _final(x2, ys, dest, gates_pad, final_norm_w.reshape(1, D))
    return xc.reshape(B, S, D)
```

```python
import functools

import jax
import jax.numpy as jnp
from jax import lax
from jax.experimental import pallas as pl
from jax.experimental.pallas import tpu as pltpu

F32 = jnp.float32
BF16 = jnp.bfloat16

EPS = 1e-6
HG_HEADS = 8
HG_DK = 128
ML_HEADS = 4
ML_DQK = 128
ML_DV = 256
CONV_W = 4
N_EXPERTS = 32
TOP_K = 4
SWIGLU_ALPHA = 1.702
SWIGLU_LIMIT = 7.0

LANES = 128
SUBLANES = 8
VMEM_LIMIT_BYTES = 56 * 1024 * 1024

HG_CHUNK = 64
HG_SUB = 16
ML_CHUNK = 128
HG_ROWS = 1024
ML_ROWS = 512
HG_HEADS_PER_STEP = 2

INPROJ_TM = 1024
INPROJ_TN = 1024
OUTPROJ_TM = 256
OUTPROJ_SUB = 128
DISPATCH_TOKENS = 256
MOE_TM = 512
MOE_TF = 1024
MOE_SUB_ROWS = 128
DMA_UNROLL = 32
DMA_QUEUES = 2
FINAL_TM = 128


def _dot_nt(a, b):
    return lax.dot_general(a, b, (((1,), (1,)), ((), ())), preferred_element_type=F32)


def _dot_tn(a, b):
    return lax.dot_general(a, b, (((0,), (0,)), ((), ())), preferred_element_type=F32)


def _log_sigmoid(z):
    return jnp.minimum(z, 0.0) - jnp.log1p(jnp.exp(-jnp.abs(z)))


def _sigmoid(z):
    return 0.5 * jnp.tanh(0.5 * z) + 0.5


def _cparams(semantics):
    return pltpu.CompilerParams(dimension_semantics=semantics, vmem_limit_bytes=VMEM_LIMIT_BYTES)


_HI_MASK = 0xFFFF0000


def _packed_rows(d):
    return d // (2 * LANES)


def _pack_store(o_ref, v):
    n, d = v.shape
    s, half = _packed_rows(d), d // 2
    bits = pltpu.bitcast(v.astype(BF16).astype(F32), jnp.uint32)
    for c in range(s):
        hi = bits[:, c * LANES:(c + 1) * LANES]
        lo = bits[:, half + c * LANES:half + (c + 1) * LANES]
        o_ref[pl.ds(c, n, stride=s), :] = hi | jnp.right_shift(lo, jnp.uint32(16))


def _unpack_load(buf, first_row, n, s, c):
    w = buf[pl.ds(first_row * s + c, n, stride=s), :]
    hi = pltpu.bitcast(w & jnp.uint32(_HI_MASK), F32)
    lo = pltpu.bitcast(jnp.left_shift(w, jnp.uint32(16)), F32)
    return hi, lo


def _inproj_body(*refs, tn, with_gates):
    if with_gates:
        x_ref, lnw_ref, w_ref, wg_ref, o_ref, g_ref, h_scr = refs
    else:
        x_ref, lnw_ref, w_ref, o_ref, h_scr = refs

    @pl.when(pl.program_id(1) == 0)
    def _():
        x = x_ref[...]
        h = x * lax.rsqrt(jnp.mean(x * x, axis=-1, keepdims=True) + EPS) * lnw_ref[...]
        hb = h.astype(BF16)
        h_scr[...] = hb
        if with_gates:
            g_ref[...] = jnp.dot(hb, wg_ref[...], preferred_element_type=F32)

    res = jnp.dot(h_scr[...], w_ref[...], preferred_element_type=F32)
    for c in range(tn // LANES):
        o_ref[c] = res[:, c * LANES:(c + 1) * LANES].astype(o_ref.dtype)


def _inproj(x2d, ln_w, w_bf, n_tiles, col_tile, out_dtype, w_gates_pad=None):
    T, D = x2d.shape
    tm = min(INPROJ_TM, T)
    tn = INPROJ_TN
    with_gates = w_gates_pad is not None
    in_specs = [
        pl.BlockSpec((tm, D), lambda i, j: (i, 0)),
        pl.BlockSpec((1, D), lambda i, j: (0, 0)),
        pl.BlockSpec((D, tn), lambda i, j: (0, col_tile(j))),
    ]
    out_specs = [pl.BlockSpec((tn // LANES, tm, LANES), lambda i, j: (j, i, 0))]
    out_shape = [jax.ShapeDtypeStruct((n_tiles * tn // LANES, T, LANES), out_dtype)]
    args = [x2d, ln_w, w_bf]
    if with_gates:
        in_specs.append(pl.BlockSpec((D, LANES), lambda i, j: (0, 0)))
        out_specs.append(pl.BlockSpec((tm, LANES), lambda i, j: (i, 0)))
        out_shape.append(jax.ShapeDtypeStruct((T, LANES), F32))
        args.append(w_gates_pad)
    return pl.pallas_call(
        functools.partial(_inproj_body, tn=tn, with_gates=with_gates),
        grid=(T // tm, n_tiles),
        in_specs=in_specs,
        out_specs=out_specs,
        out_shape=out_shape,
        scratch_shapes=[pltpu.VMEM((tm, D), BF16)],
        compiler_params=_cparams(("parallel", "arbitrary")),
        name="inproj_gates" if with_gates else "inproj",
    )(*args)


def _hgrn_body(q_ref, f_ref, i_ref, g_ref, lb_ref, nw_ref, o_ref, st_scr, *, rows, heads):
    C, SUB = HG_CHUNK, HG_SUB
    nsub = C // SUB

    @pl.when(pl.program_id(2) == 0)
    def _():
        st_scr[...] = jnp.zeros_like(st_scr)

    lbs = [lb_ref[hh] for hh in range(heads)]
    log_lbs = [jnp.log(lb) for lb in lbs]
    log_1mlbs = [jnp.log1p(-lb) for lb in lbs]

    r_i = lax.broadcasted_iota(jnp.int32, (C, C), 0)
    c_i = lax.broadcasted_iota(jnp.int32, (C, C), 1)
    tri = (c_i <= r_i).astype(F32)
    sub_shift = SUB.bit_length() - 1
    diag_mask = (c_i <= r_i) & (jnp.right_shift(r_i, sub_shift) == jnp.right_shift(c_i, sub_shift))
    row_id = lax.broadcasted_iota(jnp.int32, (C, HG_DK), 0)

    pairs = [(j, hh) for j in range(rows // C) for hh in range(heads)]
    tri_b = tri.astype(BF16)
    st = {p: {} for p in pairs}

    for p in pairs:
        j, hh = p
        z = f_ref[hh, pl.ds(j * C, C), :].astype(F32)
        e = jnp.exp(-jnp.abs(z))
        e1 = 1.0 + e
        log_sig = jnp.minimum(z, 0.0) - jnp.log(e1)
        sig_neg = jnp.where(z >= 0, e, 1.0) / e1
        cc = log_1mlbs[hh] + log_sig
        log_f = jnp.maximum(log_lbs[hh], cc) + jnp.log(1.0 + jnp.exp(-jnp.abs(log_lbs[hh] - cc)))
        st[p]["kk"] = (1.0 - lbs[hh]) * sig_neg
        hi = log_f.astype(BF16)
        lo = (log_f - hi.astype(F32)).astype(BF16)
        st[p]["b"] = (jnp.dot(tri_b, hi, preferred_element_type=F32)
                      + jnp.dot(tri_b, lo, preferred_element_type=F32))

    for p in pairs:
        j, hh = p
        q = q_ref[hh, pl.ds(j * C, C), :].astype(F32)
        kk, b = st[p]["kk"], st[p]["b"]
        b_last = b[C - 1:C, :]
        st[p]["dec"] = jnp.exp(b_last)
        st[p]["qe"] = (q * jnp.exp(b)).astype(BF16)
        st[p]["kdec"] = (kk * jnp.exp(b_last - b)).astype(BF16)
        refs = [b[I * SUB:I * SUB + 1, :] for I in range(nsub)]
        refb = jnp.concatenate([jnp.broadcast_to(r, (SUB, HG_DK)) for r in refs], axis=0)
        qd = (q * jnp.exp(b - refb)).astype(BF16)
        kd = (kk * jnp.exp(refb - b)).astype(BF16)
        q_parts, k_parts = [], []
        for J in range(nsub - 1):
            r = refs[J + 1]
            qj = q * jnp.exp(jnp.minimum(b - r, 0.0))
            kj = kk * jnp.exp(jnp.minimum(r - b, 0.0))
            q_parts.append(jnp.where(row_id >= (J + 1) * SUB, qj, 0.0).astype(BF16))
            k_parts.append(jnp.where((row_id >= J * SUB) & (row_id < (J + 1) * SUB), kj, 0.0).astype(BF16))
        st[p]["att_d"] = _dot_nt(qd, kd)
        st[p]["att_o"] = _dot_nt(jnp.concatenate(q_parts, axis=1), jnp.concatenate(k_parts, axis=1))

    for p in pairs:
        j, hh = p
        vb = i_ref[hh, pl.ds(j * C, C), :].astype(BF16)
        att = jnp.where(diag_mask, st[p]["att_d"], 0.0) + st[p]["att_o"]
        st[p]["intra"] = jnp.dot(att.astype(BF16), vb, preferred_element_type=F32)
        st[p]["upd"] = _dot_tn(vb, st[p]["kdec"])

    for hh in range(heads):
        s = st_scr[hh]
        for j in range(rows // C):
            st[(j, hh)]["s_in"] = s.astype(BF16)
            s = s * st[(j, hh)]["dec"] + st[(j, hh)]["upd"]
        st_scr[hh] = s

    for p in pairs:
        j, hh = p
        g = g_ref[hh, pl.ds(j * C, C), :].astype(F32)
        o = _dot_nt(st[p]["qe"], st[p]["s_in"]) + st[p]["intra"]
        y = o * lax.rsqrt(jnp.mean(o * o, axis=-1, keepdims=True) + EPS) * nw_ref[hh]
        o_ref[pl.ds(j * C, C), hh * LANES:(hh + 1) * LANES] = (y * (g * _sigmoid(g))).astype(o_ref.dtype)


def _hgrn(proj_b, proj_f, lb, norm_w, batch, seq):
    T = batch * seq
    rows = min(HG_ROWS, seq)
    nblk = seq // rows
    H = HG_HEADS
    hp = HG_HEADS_PER_STEP

    def slab(off):
        return pl.BlockSpec((hp, rows, LANES), lambda b, h, c, off=off: (off // hp + h, b * nblk + c, 0))

    vec = pl.BlockSpec((hp, 1, LANES), lambda b, h, c: (h, 0, 0))
    return pl.pallas_call(
        functools.partial(_hgrn_body, rows=rows, heads=hp),
        grid=(batch, H // hp, nblk),
        in_specs=[slab(0), slab(0), slab(H), slab(2 * H), vec, vec],
        out_specs=pl.BlockSpec((rows, hp * LANES), lambda b, h, c: (b * nblk + c, h)),
        out_shape=jax.ShapeDtypeStruct((T, H * LANES), BF16),
        scratch_shapes=[pltpu.VMEM((hp, LANES, HG_DK), F32)],
        compiler_params=_cparams(("parallel", "parallel", "arbitrary")),
        name="hgrn2",
    )(proj_b, proj_f, proj_b, proj_b, lb.reshape(H, 1, HG_DK), norm_w.reshape(H, 1, LANES))


def _mlstm_body(q_ref, k_ref, v_ref, og_ref, gt_ref, gb_ref, cwq_ref, cwk_ref, cbq_ref, cbk_ref, nw_ref,
                out_ref, c_scr, n_scr, m_scr, qx_scr, kx_scr, qc_scr, kc_scr, *, rows):
    C = ML_CHUNK
    PAD = SUBLANES
    h = pl.program_id(1)

    @pl.when(pl.program_id(2) == 0)
    def _():
        c_scr[...] = jnp.zeros_like(c_scr)
        n_scr[...] = jnp.zeros_like(n_scr)
        m_scr[...] = jnp.zeros_like(m_scr)
        qx_scr[0:PAD, :] = jnp.zeros((PAD, LANES), F32)
        kx_scr[0:PAD, :] = jnp.zeros((PAD, LANES), F32)

    qx_scr[PAD:PAD + rows, :] = q_ref[0].astype(F32)
    kx_scr[PAD:PAD + rows, :] = k_ref[0].astype(F32)
    accq = jnp.zeros((rows, LANES), F32) + cbq_ref[...]
    acck = jnp.zeros((rows, LANES), F32) + cbk_ref[...]
    for j in range(CONV_W):
        off = PAD - (CONV_W - 1) + j
        accq = accq + cwq_ref[j:j + 1, :] * qx_scr[pl.ds(off, rows), :]
        acck = acck + cwk_ref[j:j + 1, :] * kx_scr[pl.ds(off, rows), :]
    qc_scr[...] = accq * _sigmoid(accq) * (ML_DQK ** -0.5)
    kc_scr[...] = acck * _sigmoid(acck)
    qx_scr[0:PAD, :] = qx_scr[rows:rows + PAD, :]
    kx_scr[0:PAD, :] = kx_scr[rows:rows + PAD, :]

    lane = lax.broadcasted_iota(jnp.int32, (C, C), 1)
    sub = lax.broadcasted_iota(jnp.int32, (C, C), 0)
    causal = lane <= sub
    lower = causal.astype(F32)
    upper = (sub <= lane).astype(F32)
    nw = nw_ref[...]
    gbias = gb_ref[...]

    chunks = range(rows // C)
    lower_b, upper_b = lower.astype(BF16), upper.astype(BF16)
    st = [{} for _ in chunks]

    def split(x):
        hi = x.astype(BF16)
        return hi, (x - hi.astype(F32)).astype(BF16)

    for j in chunks:
        gc = gt_ref[pl.ds(j * C, C), :] + gbias
        gct = gc.T
        li_col = jnp.sum(jnp.where(lane == h, gc, 0.0), axis=1, keepdims=True)
        fg_col = jnp.sum(jnp.where(lane == h + ML_HEADS, gc, 0.0), axis=1, keepdims=True)
        li_row = jnp.sum(jnp.where(sub == h, gct, 0.0), axis=0, keepdims=True)
        fg_row = jnp.sum(jnp.where(sub == h + ML_HEADS, gct, 0.0), axis=0, keepdims=True)
        ch, cl = split(_log_sigmoid(fg_col))
        rh, rl = split(_log_sigmoid(fg_row))
        g_t = (jnp.dot(lower_b, jnp.broadcast_to(ch, (C, C)), preferred_element_type=F32)
               + jnp.dot(lower_b, jnp.broadcast_to(cl, (C, C)), preferred_element_type=F32))
        g_s = (jnp.dot(jnp.broadcast_to(rh, (C, C)), upper_b, preferred_element_type=F32)
               + jnp.dot(jnp.broadcast_to(rl, (C, C)), upper_b, preferred_element_type=F32))
        st[j].update(g_t=g_t, g_s=g_s, li_col=li_col, li_row=li_row)

    for j in chunks:
        c = st[j]
        g_col = c["g_t"][:, 0:1]
        g_last = c["g_t"][C - 1:C, 0:1]
        dmat = jnp.where(causal, c["g_t"] - c["g_s"] + c["li_row"], -jnp.inf)
        log_ws = g_last - g_col + c["li_col"]
        qf = qc_scr[pl.ds(j * C, C), :]
        kf = kc_scr[pl.ds(j * C, C), :]
        qb = qf.astype(BF16)
        c.update(g_col=g_col, g_last=g_last, dmat=dmat, dmax=jnp.max(dmat, axis=1, keepdims=True),
                 log_ws=log_ws, ws_max=jnp.max(log_ws, axis=0, keepdims=True), qf=qf, kf=kf, qb=qb,
                 qk=_dot_nt(qb, kf.astype(BF16)))

    m = m_scr[:, 0:1]
    for j in chunks:
        c = st[j]
        m_new = jnp.maximum(c["g_last"] + m, c["ws_max"])
        c.update(m_prev=m, m_new=m_new, decay=jnp.exp(c["g_last"] + m - m_new))
        m = m_new
    m_scr[...] = jnp.broadcast_to(m, m_scr.shape)

    for j in chunks:
        c = st[j]
        vb = jnp.concatenate([v_ref[0, pl.ds(j * C, C), :], v_ref[1, pl.ds(j * C, C), :]], axis=1).astype(BF16)
        kw = c["kf"] * jnp.exp(c["log_ws"] - c["m_new"])
        c.update(vb=vb, upd=_dot_tn(kw.astype(BF16), vb), ksum=jnp.sum(kw, axis=0, keepdims=True))

    cm = c_scr[...]
    nv = n_scr[...]
    for j in chunks:
        c = st[j]
        c.update(c_in=cm.astype(BF16), n_in=nv)
        cm = c["decay"] * cm + c["upd"]
        nv = c["decay"] * nv + c["ksum"]
    c_scr[...] = cm
    n_scr[...] = nv

    for j in chunks:
        c = st[j]
        a_inter = c["g_col"] + c["m_prev"]
        m_t = jnp.maximum(a_inter, c["dmax"])
        w_inter = jnp.exp(a_inter - m_t)
        sqk = c["qk"] * jnp.exp(c["dmat"] - m_t)
        num = (w_inter * jnp.dot(c["qb"], c["c_in"], preferred_element_type=F32)
               + jnp.dot(sqk.astype(BF16), c["vb"], preferred_element_type=F32))
        den = (w_inter * jnp.sum(c["qf"] * c["n_in"], axis=1, keepdims=True)
               + jnp.sum(sqk, axis=1, keepdims=True))
        hh = num * (1.0 / jnp.maximum(jnp.abs(den), jnp.exp(-m_t)))
        y = hh * lax.rsqrt(jnp.mean(hh * hh, axis=-1, keepdims=True) + EPS) * nw
        og = jnp.concatenate([og_ref[0, pl.ds(j * C, C), :], og_ref[1, pl.ds(j * C, C), :]], axis=1).astype(F32)
        out_ref[pl.ds(j * C, C), :] = (y * _sigmoid(og)).astype(out_ref.dtype)


def _mlstm(proj3, gates, gate_bias_pad, conv_w, conv_b, norm_w, batch, seq, q_off):
    T = batch * seq
    rows = min(ML_ROWS, seq)
    nblk = seq // rows
    H = ML_HEADS
    k_off = q_off + H
    v_off = k_off + H
    o_off = v_off + 2 * H

    def slab(off):
        return pl.BlockSpec((1, rows, LANES), lambda b, h, c, off=off: (off + h, b * nblk + c, 0))

    def slab2(off):
        return pl.BlockSpec((2, rows, LANES), lambda b, h, c, off=off: (off // 2 + h, b * nblk + c, 0))

    qk_w = H * ML_DQK
    return pl.pallas_call(
        functools.partial(_mlstm_body, rows=rows),
        grid=(batch, H, nblk),
        in_specs=[
            slab(q_off), slab(k_off), slab2(v_off), slab2(o_off),
            pl.BlockSpec((rows, LANES), lambda b, h, c: (b * nblk + c, 0)),
            pl.BlockSpec((1, LANES), lambda b, h, c: (0, 0)),
            pl.BlockSpec((CONV_W, LANES), lambda b, h, c: (0, h)),
            pl.BlockSpec((CONV_W, LANES), lambda b, h, c: (0, H + h)),
            pl.BlockSpec((1, LANES), lambda b, h, c: (0, h)),
            pl.BlockSpec((1, LANES), lambda b, h, c: (0, H + h)),
            pl.BlockSpec((1, ML_DV), lambda b, h, c: (0, h)),
        ],
        out_specs=pl.BlockSpec((rows, ML_DV), lambda b, h, c: (b * nblk + c, h)),
        out_shape=jax.ShapeDtypeStruct((T, H * ML_DV), BF16),
        scratch_shapes=[
            pltpu.VMEM((ML_DQK, ML_DV), F32),
            pltpu.VMEM((1, ML_DQK), F32),
            pltpu.VMEM((1, LANES), F32),
            pltpu.VMEM((rows + 2 * SUBLANES, LANES), F32),
            pltpu.VMEM((rows + 2 * SUBLANES, LANES), F32),
            pltpu.VMEM((rows, LANES), F32),
            pltpu.VMEM((rows, LANES), F32),
        ],
        compiler_params=_cparams(("parallel", "parallel", "arbitrary")),
        name="mlstm",
    )(proj3, proj3, proj3, proj3, gates, gate_bias_pad, conv_w, conv_w,
      conv_b.reshape(1, 2 * qk_w), conv_b.reshape(1, 2 * qk_w), norm_w.reshape(1, H * ML_DV))


def _outproj_body(a_ref, b_ref, x_ref, wo_ref, ln_ref, wrh_ref, wrl_ref, br_ref,
                  x2_ref, h2_ref, idx_ref, gate_ref, rank_ref, cnt_ref, cnt_scr, *, sub_rows):
    tm = x_ref.shape[0]
    ka = a_ref.shape[1]
    s = _packed_rows(x_ref.shape[1])

    @pl.when(pl.program_id(0) == 0)
    def _():
        cnt_scr[...] = jnp.zeros_like(cnt_scr)

    lane = lax.broadcasted_iota(jnp.int32, (sub_rows, LANES), 1).astype(F32)
    onehots = [[] for _ in range(TOP_K)]
    for r0 in range(0, tm, sub_rows):
        rows = pl.ds(r0, sub_rows)
        res = (jnp.dot(a_ref[rows, :], wo_ref[0:ka, :], preferred_element_type=F32)
               + jnp.dot(b_ref[rows, :], wo_ref[ka:, :], preferred_element_type=F32))
        x2 = x_ref[rows, :] + res
        x2_ref[rows, :] = x2
        h2 = x2 * lax.rsqrt(jnp.mean(x2 * x2, axis=-1, keepdims=True) + EPS) * ln_ref[...]
        _pack_store(h2_ref.at[pl.ds(r0 * s, sub_rows * s), :], h2)

        h_hi = h2.astype(BF16)
        h_lo = (h2 - h_hi.astype(F32)).astype(BF16)
        logits = (jnp.dot(h_hi, wrh_ref[...], preferred_element_type=F32)
                  + jnp.dot(h_lo, wrh_ref[...], preferred_element_type=F32)
                  + jnp.dot(h_hi, wrl_ref[...], preferred_element_type=F32)) + br_ref[...]
        vals, idxs = [], []
        cur = logits
        for _ in range(TOP_K):
            m = jnp.max(cur, axis=1, keepdims=True)
            ix = jnp.min(jnp.where(cur == m, lane, float(LANES)), axis=1, keepdims=True)
            vals.append(m)
            idxs.append(ix)
            cur = jnp.where(lane == ix, -jnp.inf, cur)
        es = [jnp.exp(v - vals[0]) for v in vals]
        inv = 1.0 / (es[0] + es[1] + es[2] + es[3])
        gate = jnp.zeros(logits.shape, F32)
        idx = jnp.zeros(logits.shape, F32)
        for k in range(TOP_K):
            gate = jnp.where(lane == float(k), es[k] * inv, gate)
            idx = jnp.where(lane == float(k), idxs[k], idx)
            onehots[k].append((lane == idxs[k]).astype(F32))
        gate_ref[rows, :] = gate
        idx_ref[rows, :] = idx.astype(jnp.int32)

    oh_k = [jnp.concatenate(o, axis=0) for o in onehots]
    oh = oh_k[0] + oh_k[1] + oh_k[2] + oh_k[3]
    r_i = lax.broadcasted_iota(jnp.int32, (tm, tm), 0)
    c_i = lax.broadcasted_iota(jnp.int32, (tm, tm), 1)
    before = jnp.dot((c_i < r_i).astype(BF16), oh.astype(BF16), preferred_element_type=F32) + cnt_scr[...]
    lane_t = lax.broadcasted_iota(jnp.int32, (tm, LANES), 1)
    rank = jnp.zeros((tm, LANES), F32)
    for k in range(TOP_K):
        rank = jnp.where(lane_t == k, jnp.sum(oh_k[k] * before, axis=1, keepdims=True), rank)
    rank_ref[...] = rank.astype(jnp.int32)
    cnt = cnt_scr[...] + jnp.sum(oh, axis=0, keepdims=True)
    cnt_scr[...] = cnt
    cnt_ref[...] = cnt.astype(jnp.int32)


def _outproj(a_out, b_out, x2d, w_out_bf, ln_w, wr_hi, wr_lo, b_router_pad):
    T, D = x2d.shape
    tm = min(OUTPROJ_TM, T)
    ka, kb = a_out.shape[1], b_out.shape[1]
    s = _packed_rows(D)
    row = lambda w: pl.BlockSpec((tm, w), lambda i: (i, 0))
    full = lambda r, c: pl.BlockSpec((r, c), lambda i: (0, 0))
    return pl.pallas_call(
        functools.partial(_outproj_body, sub_rows=min(OUTPROJ_SUB, tm)),
        grid=(T // tm,),
        in_specs=[row(ka), row(kb), row(D), full(ka + kb, D), full(1, D), full(D, LANES), full(D, LANES),
                  full(1, LANES)],
        out_specs=[row(D), pl.BlockSpec((tm * s, LANES), lambda i: (i, 0)), row(LANES), row(LANES), row(LANES),
                   full(1, LANES)],
        out_shape=[
            jax.ShapeDtypeStruct((T, D), F32),
            jax.ShapeDtypeStruct((T * s, LANES), jnp.uint32),
            jax.ShapeDtypeStruct((T, LANES), jnp.int32),
            jax.ShapeDtypeStruct((T, LANES), F32),
            jax.ShapeDtypeStruct((T, LANES), jnp.int32),
            jax.ShapeDtypeStruct((1, LANES), jnp.int32),
        ],
        scratch_shapes=[pltpu.VMEM((1, LANES), F32)],
        compiler_params=_cparams(("arbitrary",)),
        name="outproj_router",
    )(a_out, b_out, x2d, w_out_bf, ln_w, wr_hi, wr_lo, b_router_pad)


def _start_row_gather(idx_at, src_ref, buf, sem, n, s):
    def start(pair, carry):
        for p in range(DMA_QUEUES):
            i = pair * DMA_QUEUES + p
            src_row = pl.multiple_of(idx_at(i) * s, s)
            dst_row = pl.multiple_of(i * s, s)
            pltpu.make_async_copy(src_ref.at[pl.ds(src_row, s), :], buf.at[pl.ds(dst_row, s), :],
                                  sem).start(priority=p)
        return carry

    lax.fori_loop(0, n // DMA_QUEUES, start, 0, unroll=DMA_UNROLL // DMA_QUEUES)


def _wait_row_gather(buf, sem):
    pltpu.make_async_copy(buf, buf, sem).wait()


def _dispatch_body(zero_ref, dest_ref, h_ref, xs_ref, zbuf, sem, zsem, *, ntok, s, tm, n_blocks):
    blk = tm * s

    def zero_copy(b):
        return pltpu.make_async_copy(zbuf, xs_ref.at[pl.ds(pl.multiple_of(b * blk, blk), blk), :], zsem)

    @pl.when(pl.program_id(0) == 0)
    def _():
        zbuf[...] = jnp.zeros_like(zbuf)

        def zstart(b, carry):
            @pl.when(zero_ref[b] == 1)
            def _():
                zero_copy(b).start()
            return carry

        def zwait(b, carry):
            @pl.when(zero_ref[b] == 1)
            def _():
                zero_copy(b).wait()
            return carry

        lax.fori_loop(0, n_blocks, zstart, 0)
        lax.fori_loop(0, n_blocks, zwait, 0)

    def start(t, carry):
        src = h_ref.at[pl.ds(pl.multiple_of(t * s, s), s), :]
        for k in range(TOP_K):
            dst_row = pl.multiple_of(dest_ref[0, 0, t * TOP_K + k] * s, s)
            pltpu.make_async_copy(src, xs_ref.at[pl.ds(dst_row, s), :], sem).start(priority=k % DMA_QUEUES)
        return carry

    lax.fori_loop(0, ntok, start, 0, unroll=DMA_UNROLL // TOP_K)
    for _ in range(TOP_K):
        pltpu.make_async_copy(h_ref, h_ref, sem).wait()


def _dispatch(dest, zero_blk, h_packed, d, n_slots, tm):
    s = _packed_rows(d)
    T = h_packed.shape[0] // s
    ntok = min(DISPATCH_TOKENS, T)
    n = ntok * TOP_K
    grid_spec = pltpu.PrefetchScalarGridSpec(
        num_scalar_prefetch=1,
        grid=(T // ntok,),
        in_specs=[
            pl.BlockSpec((1, 1, n), lambda i, z: (i, 0, 0), memory_space=pltpu.SMEM),
            pl.BlockSpec((ntok * s, LANES), lambda i, z: (i, 0)),
        ],
        out_specs=pl.BlockSpec(memory_space=pl.ANY),
        scratch_shapes=[pltpu.VMEM((tm * s, LANES), jnp.uint32), pltpu.SemaphoreType.DMA,
                        pltpu.SemaphoreType.DMA],
    )
    return pl.pallas_call(
        functools.partial(_dispatch_body, ntok=ntok, s=s, tm=tm, n_blocks=zero_blk.shape[0]),
        grid_spec=grid_spec,
        out_shape=jax.ShapeDtypeStruct((n_slots * s, LANES), jnp.uint32),
        compiler_params=_cparams(("arbitrary",)),
        name="moe_dispatch",
    )(zero_blk, dest.reshape(T // ntok, 1, n), h_packed)


def _staged_weights(sched_refs, hbm_refs, stage_refs, bf_refs, sem):
    e_ref, wt_ref, first_ref, ne_ref, nt_ref, more_ref = sched_refs
    w = pl.program_id(0)
    tn = stage_refs[0].shape[1]

    def copies(e, t):
        col = pl.multiple_of(t * tn, tn)
        return [pltpu.make_async_copy(h.at[e, :, pl.ds(col, tn)], st, sem.at[k])
                for k, (h, st) in enumerate(zip(hbm_refs, stage_refs))]

    def start_all(cs):
        for c in cs:
            c.start()

    @pl.when(w == 0)
    def _():
        start_all(copies(e_ref[0], wt_ref[0]))

    @pl.when(first_ref[w] == 1)
    def _():
        for c in copies(e_ref[w], wt_ref[w]):
            c.wait()
        for st, bf in zip(stage_refs, bf_refs):
            for r in range(0, st.shape[0], 256):
                bf[r:r + 256, :] = st[r:r + 256, :].astype(BF16)

        @pl.when(more_ref[w] == 1)
        def _():
            start_all(copies(ne_ref[w], nt_ref[w]))


def _row_groups(nrows, tm, compute, clear):
    sub = min(MOE_SUB_ROWS, tm)
    groups = tm // sub
    for live in range(groups + 1):
        @pl.when((nrows > (live - 1) * sub) & (nrows <= live * sub))
        def _(live=live):
            for g in range(live):
                compute(g * sub, sub)
            for g in range(live, groups):
                clear(g * sub, sub)


def _moe_up_body(e_ref, wt_ref, r_ref, ot_ref, first_ref, nrows_ref, ne_ref, nt_ref, more_ref,
                 x_ref, wg_ref, wu_ref, bg_ref, bu_ref, o_ref, wg_stage, wu_stage, wg_scr, wu_scr, x_scr, sem):
    w = pl.program_id(0)
    _staged_weights((e_ref, wt_ref, first_ref, ne_ref, nt_ref, more_ref), (wg_ref, wu_ref),
                    (wg_stage, wu_stage), (wg_scr, wu_scr), sem)
    tm, d = x_scr.shape
    s = _packed_rows(d)

    def compute(r0, n):
        rows = pl.ds(r0, n)
        for c in range(s):
            hi, lo = _unpack_load(x_ref, r0, n, s, c)
            x_scr[rows, c * LANES:(c + 1) * LANES] = hi.astype(BF16)
            x_scr[rows, d // 2 + c * LANES:d // 2 + (c + 1) * LANES] = lo.astype(BF16)
        x = x_scr[rows, :]
        gt = jnp.dot(x, wg_scr[...], preferred_element_type=F32) + bg_ref[0]
        up = jnp.dot(x, wu_scr[...], preferred_element_type=F32) + bu_ref[0]
        gt = jnp.minimum(gt, SWIGLU_LIMIT)
        up = jnp.clip(up, -SWIGLU_LIMIT, SWIGLU_LIMIT)
        o_ref[rows, :] = ((up + 1.0) * (gt * _sigmoid(SWIGLU_ALPHA * gt))).astype(o_ref.dtype)

    def clear(r0, n):
        o_ref[pl.ds(r0, n), :] = jnp.zeros((n, o_ref.shape[1]), o_ref.dtype)

    _row_groups(nrows_ref[w], tm, compute, clear)


def _moe_down_body(e_ref, wt_ref, r_ref, ot_ref, first_ref, nrows_ref, ne_ref, nt_ref, more_ref,
                   a_ref, wd_ref, bd_ref, o_ref, wd_stage, wd_scr, sem):
    w = pl.program_id(0)
    _staged_weights((e_ref, wt_ref, first_ref, ne_ref, nt_ref, more_ref), (wd_ref,), (wd_stage,), (wd_scr,), sem)
    tm = a_ref.shape[0]
    s = o_ref.shape[0] // tm

    def compute(r0, n):
        y = jnp.dot(a_ref[pl.ds(r0, n), :], wd_scr[...], preferred_element_type=F32) + bd_ref[0]
        _pack_store(o_ref.at[pl.ds(r0 * s, n * s), :], y)

    def clear(r0, n):
        o_ref[pl.ds(r0 * s, n * s), :] = jnp.zeros((n * s, LANES), o_ref.dtype)

    _row_groups(nrows_ref[w], tm, compute, clear)


def _moe_schedule(counts, tm, n_tiles, n_blocks):
    n_items = n_tiles * n_blocks
    experts = jnp.arange(N_EXPERTS, dtype=jnp.int32)
    blocks_e = (counts + tm - 1) // tm
    bend = jnp.cumsum(blocks_e)
    bstart = bend - blocks_e
    item_end = n_tiles * bend
    total = item_end[-1]
    later = (experts[None, :] > experts[:, None]) & (blocks_e[None, :] > 0)
    next_e = jnp.min(jnp.where(later, experts[None, :], N_EXPERTS - 1), axis=1)

    w = jnp.arange(n_items, dtype=jnp.int32)
    valid = w < total
    wc = jnp.minimum(w, jnp.maximum(total - 1, 0))
    e = jnp.minimum(jnp.sum((item_end[None, :] <= wc[:, None]).astype(jnp.int32), axis=1), N_EXPERTS - 1)
    sel = e[:, None] == experts[None, :]
    pick = lambda table: jnp.sum(jnp.where(sel, table[None, :], 0), axis=1)
    nb = jnp.maximum(pick(blocks_e), 1)
    local = wc - n_tiles * pick(bstart)
    wtile = sum((local >= t * nb).astype(jnp.int32) for t in range(1, n_tiles)) if n_tiles > 1 else 0 * local
    jblk = local - wtile * nb
    spare = jnp.maximum(w - total, 0)
    rblk = jnp.where(valid, pick(bstart) + jblk, bend[-1] + spare // n_tiles)
    otile = jnp.where(valid, wtile, spare % n_tiles)
    nrows = jnp.where(valid, jnp.clip(pick(counts) - jblk * tm, 0, tm), 0)
    first = (jblk == 0) & valid
    last_tile = wtile == n_tiles - 1
    more = first & (w + nb < total)
    i32 = lambda a: a.astype(jnp.int32)
    return (e, i32(wtile), i32(rblk), i32(otile), i32(first), i32(nrows),
            i32(jnp.where(last_tile, pick(next_e), e)), i32(jnp.where(last_tile, 0, wtile + 1)), i32(more))


def _moe_up(sched, xs_packed, w_gate, w_up, b_gate, b_up):
    D, d_ff = w_gate.shape[1], w_gate.shape[2]
    s = _packed_rows(D)
    n_slots = xs_packed.shape[0] // s
    tm, tf = MOE_TM, min(MOE_TF, d_ff)
    n_items = sched[0].shape[0]
    wspec = pl.BlockSpec(memory_space=pl.ANY)
    bspec = pl.BlockSpec((1, 1, tf), lambda w, e, wt, r, ot, *_: (e[w], 0, wt[w]))
    grid_spec = pltpu.PrefetchScalarGridSpec(
        num_scalar_prefetch=len(sched),
        grid=(n_items,),
        in_specs=[pl.BlockSpec((tm * s, LANES), lambda w, e, wt, r, ot, *_: (r[w], 0)),
                  wspec, wspec, bspec, bspec],
        out_specs=pl.BlockSpec((tm, tf), lambda w, e, wt, r, ot, *_: (r[w], ot[w])),
        scratch_shapes=[pltpu.VMEM((D, tf), F32), pltpu.VMEM((D, tf), F32),
                        pltpu.VMEM((D, tf), BF16), pltpu.VMEM((D, tf), BF16), pltpu.VMEM((tm, D), BF16),
                        pltpu.SemaphoreType.DMA((2,))],
    )
    return pl.pallas_call(
        _moe_up_body,
        grid_spec=grid_spec,
        out_shape=jax.ShapeDtypeStruct((n_slots, d_ff), BF16),
        compiler_params=_cparams(("arbitrary",)),
        name="moe_up",
    )(*sched, xs_packed, w_gate, w_up, b_gate.reshape(N_EXPERTS, 1, d_ff), b_up.reshape(N_EXPERTS, 1, d_ff))


def _moe_down(sched, act, w_down, b_down):
    n_slots, d_ff = act.shape
    D = w_down.shape[2]
    tm = MOE_TM
    s = _packed_rows(D)
    n_items = sched[0].shape[0]
    grid_spec = pltpu.PrefetchScalarGridSpec(
        num_scalar_prefetch=len(sched),
        grid=(n_items,),
        in_specs=[
            pl.BlockSpec((tm, d_ff), lambda w, e, wt, r, ot, *_: (r[w], 0)),
            pl.BlockSpec(memory_space=pl.ANY),
            pl.BlockSpec((1, 1, D), lambda w, e, wt, r, ot, *_: (e[w], 0, 0)),
        ],
        out_specs=pl.BlockSpec((tm * s, LANES), lambda w, e, wt, r, ot, *_: (r[w], 0)),
        scratch_shapes=[pltpu.VMEM((d_ff, D), F32), pltpu.VMEM((d_ff, D), BF16), pltpu.SemaphoreType.DMA((1,))],
    )
    return pl.pallas_call(
        _moe_down_body,
        grid_spec=grid_spec,
        out_shape=jax.ShapeDtypeStruct((n_slots * s, LANES), jnp.uint32),
        compiler_params=_cparams(("arbitrary",)),
        name="moe_down",
    )(*sched, act, w_down, b_down.reshape(N_EXPERTS, 1, D))


def _final_body(dest_ref, next_ref, x2_ref, gate_ref, w_ref, ys_ref, o_ref, buf0, buf1, sem0, sem1, *, tm, s):
    n = tm * TOP_K
    i = pl.program_id(0)

    def gather(idx_ref, tile, buf, sem):
        _start_row_gather(lambda r: idx_ref[0, 0, tile * n + r], ys_ref, buf, sem, n, s)

    def combine(tile, buf):
        rows = pl.ds(tile * tm, tm)
        gate = gate_ref[rows, :]
        half = o_ref.shape[1] // 2
        o_ref[rows, :] = x2_ref[rows, :]
        for k in range(TOP_K):
            g = gate[:, k:k + 1]
            for c in range(s):
                hi, lo = _unpack_load(buf, k * tm, tm, s, c)
                o_ref[rows, c * LANES:(c + 1) * LANES] += g * hi
                o_ref[rows, half + c * LANES:half + (c + 1) * LANES] += g * lo
        acc = o_ref[rows, :]
        o_ref[rows, :] = acc * lax.rsqrt(jnp.mean(acc * acc, axis=-1, keepdims=True) + EPS) * w_ref[...]

    @pl.when(i == 0)
    def _():
        gather(dest_ref, 0, buf0, sem0)

    gather(dest_ref, 1, buf1, sem1)
    _wait_row_gather(buf0, sem0)
    combine(0, buf0)
    gather(next_ref, 0, buf0, sem0)
    _wait_row_gather(buf1, sem1)
    combine(1, buf1)

    @pl.when(i == pl.num_programs(0) - 1)
    def _():
        _wait_row_gather(buf0, sem0)


def _final(x2, ys_packed, dest, gates_pad, w):
    T, D = x2.shape
    tm = min(FINAL_TM, T // 2)
    s = _packed_rows(D)
    n = tm * TOP_K
    steps = T // (2 * tm)
    dest_km = dest.reshape(steps, 2, tm, TOP_K).transpose(0, 1, 3, 2).reshape(steps, 1, 2 * n)
    buf = pltpu.VMEM((n * s, LANES), jnp.uint32)
    return pl.pallas_call(
        functools.partial(_final_body, tm=tm, s=s),
        grid=(steps,),
        in_specs=[
            pl.BlockSpec((1, 1, 2 * n), lambda i: (i, 0, 0), memory_space=pltpu.SMEM),
            pl.BlockSpec((1, 1, 2 * n), lambda i: (jnp.minimum(i + 1, steps - 1), 0, 0), memory_space=pltpu.SMEM),
            pl.BlockSpec((2 * tm, D), lambda i: (i, 0)),
            pl.BlockSpec((2 * tm, LANES), lambda i: (i, 0)),
            pl.BlockSpec((1, D), lambda i: (0, 0)),
            pl.BlockSpec(memory_space=pl.ANY),
        ],
        out_specs=pl.BlockSpec((2 * tm, D), lambda i: (i, 0)),
        out_shape=jax.ShapeDtypeStruct((T, D), F32),
        scratch_shapes=[buf, buf, pltpu.SemaphoreType.DMA, pltpu.SemaphoreType.DMA],
        compiler_params=_cparams(("arbitrary",)),
        name="final_norm",
    )(dest_km, dest_km, x2, gates_pad, w, ys_packed)


def _moe(h2_packed, T, D, top_idx, rank, counts, w_gate, b_gate, w_up, b_up, w_down, b_down):
    A = T * TOP_K
    tm = MOE_TM
    n_blocks = (A + N_EXPERTS * (tm - 1) + tm - 1) // tm
    n_slots = n_blocks * tm

    blocks_e = (counts + tm - 1) // tm
    bend = jnp.cumsum(blocks_e)
    bstart = bend - blocks_e
    experts = jnp.arange(N_EXPERTS, dtype=jnp.int32)
    first_slot = jnp.sum(jnp.where(top_idx[:, :, None] == experts, bstart * tm, 0), axis=-1)
    dest = (first_slot + rank).astype(jnp.int32).reshape(A)
    blk = jnp.arange(n_blocks, dtype=jnp.int32)
    is_last = jnp.any((blk[:, None] == bend[None, :] - 1) & (blocks_e[None, :] > 0), axis=1)
    zero_blk = ((blk >= bend[-1]) | is_last).astype(jnp.int32)

    xs = _dispatch(dest, zero_blk, h2_packed, D, n_slots, tm)
    d_ff = w_gate.shape[2]
    act = _moe_up(_moe_schedule(counts, tm, -(-d_ff // MOE_TF), n_blocks), xs, w_gate, w_up, b_gate, b_up)
    ys = _moe_down(_moe_schedule(counts, tm, 1, n_blocks), act, w_down, b_down)
    return ys, dest


def kernel(x, ln1_w, w_in, hg_lb_logits, hg_norm_w, ml_conv_w, ml_conv_b, ml_igate_b, ml_fgate_b, ml_norm_w,
           w_out, ln2_w, w_router, b_router, w_gate, b_gate, w_up, b_up, w_down, b_down, final_norm_w):
    B, S, D = x.shape
    T = B * S
    depth = w_in.shape[0]
    hg_w = HG_HEADS * HG_DK
    n_main = 4 * hg_w + 2 * ML_HEADS * ML_DQK + 2 * ML_HEADS * ML_DV
    lb_all = jnp.cumsum(jax.nn.softmax(hg_lb_logits.astype(F32), axis=0), axis=0)

    xc = x.reshape(T, D)
    for l in range(depth):
        w_gates_pad = jnp.pad(w_in[l][:, n_main:], ((0, 0), (0, LANES - 2 * ML_HEADS))).astype(BF16)
        w_bf = w_in[l].astype(BF16)
        ln1 = ln1_w[l].reshape(1, D)
        hf_tiles = hg_w // INPROJ_TN
        proj_f = _inproj(xc, ln1, w_bf, hf_tiles, lambda j: j + hf_tiles, F32)[0]
        proj_b, gates = _inproj(xc, ln1, w_bf, n_main // INPROJ_TN - hf_tiles,
                                lambda j: jnp.where(j >= hf_tiles, j + hf_tiles, j), BF16, w_gates_pad)
        a_out = _hgrn(proj_b, proj_f, lb_all[l], hg_norm_w[l], B, S)
        gate_bias = jnp.pad(jnp.concatenate([ml_igate_b[l], ml_fgate_b[l]]), (0, LANES - 2 * ML_HEADS))
        b_out = _mlstm(proj_b, gates, gate_bias.reshape(1, LANES), ml_conv_w[l], ml_conv_b[l], ml_norm_w[l],
                       B, S, 3 * HG_HEADS)
        wr_pad = jnp.pad(w_router[l], ((0, 0), (0, LANES - N_EXPERTS)))
        wr_hi = wr_pad.astype(BF16)
        wr_lo = (wr_pad - wr_hi.astype(F32)).astype(BF16)
        br_pad = jnp.pad(b_router[l], (0, LANES - N_EXPERTS), constant_values=-1e30).reshape(1, LANES)
        x2, h2, idx_pad, gates_pad, rank_pad, cnt = _outproj(
            a_out, b_out, xc, w_out[l].astype(BF16), ln2_w[l].reshape(1, D), wr_hi, wr_lo, br_pad)
        ys, dest = _moe(h2, T, D, idx_pad[:, :TOP_K], rank_pad[:, :TOP_K], cnt[0, :N_EXPERTS],
                        w_gate[l], b_gate[l], w_up[l], b_up[l], w_down[l], b_down[l])
        if l + 1 < depth:
            raise NotImplementedError("only the final layer fuses the output norm")
        xc = _final(x2, ys, dest, gates_pad, final_norm_w.reshape(1, D))
    return xc.reshape(B, S, D)
```

```python
import functools

import jax
import jax.numpy as jnp
from jax import lax
from jax.experimental import pallas as pl
from jax.experimental.pallas import tpu as pltpu

F32 = jnp.float32
BF16 = jnp.bfloat16

EPS = 1e-6
HG_HEADS = 8
HG_DK = 128
ML_HEADS = 4
ML_DQK = 128
ML_DV = 256
CONV_W = 4
N_EXPERTS = 32
TOP_K = 4
SWIGLU_ALPHA = 1.702
SWIGLU_LIMIT = 7.0

LANES = 128
SUBLANES = 8
VMEM_LIMIT_BYTES = 56 * 1024 * 1024

HG_CHUNK = 64
HG_SUB = 16
ML_CHUNK = 128
HG_ROWS = 1024
ML_ROWS = 512
HG_HEADS_PER_STEP = 2

INPROJ_TM = 1024
INPROJ_TN = 1024
OUTPROJ_TM = 256
OUTPROJ_SUB = 128
DISPATCH_TOKENS = 256
MOE_TM = 512
MOE_TF = 1024
MOE_SUB_ROWS = 128
DMA_UNROLL = 32
DMA_QUEUES = 2
FINAL_TM = 128


def _dot_nt(a, b):
    return lax.dot_general(a, b, (((1,), (1,)), ((), ())), preferred_element_type=F32)


def _dot_tn(a, b):
    return lax.dot_general(a, b, (((0,), (0,)), ((), ())), preferred_element_type=F32)


def _log_sigmoid(z):
    return jnp.minimum(z, 0.0) - jnp.log(1.0 + jnp.exp(-jnp.abs(z)))


def _sigmoid(z):
    return 0.5 * jnp.tanh(0.5 * z) + 0.5


def _cparams(semantics):
    return pltpu.CompilerParams(dimension_semantics=semantics, vmem_limit_bytes=VMEM_LIMIT_BYTES)


_HI_MASK = 0xFFFF0000


def _packed_rows(d):
    return d // (2 * LANES)


def _pack_store(o_ref, v):
    n, d = v.shape
    s, half = _packed_rows(d), d // 2
    bits = pltpu.bitcast(v.astype(BF16).astype(F32), jnp.uint32)
    for c in range(s):
        hi = bits[:, c * LANES:(c + 1) * LANES]
        lo = bits[:, half + c * LANES:half + (c + 1) * LANES]
        o_ref[pl.ds(c, n, stride=s), :] = hi | jnp.right_shift(lo, jnp.uint32(16))


def _unpack_load(buf, first_row, n, s, c):
    w = buf[pl.ds(first_row * s + c, n, stride=s), :]
    hi = pltpu.bitcast(w & jnp.uint32(_HI_MASK), F32)
    lo = pltpu.bitcast(jnp.left_shift(w, jnp.uint32(16)), F32)
    return hi, lo


def _inproj_body(*refs, tn, with_gates):
    if with_gates:
        x_ref, lnw_ref, w_ref, wg_ref, o_ref, g_ref, h_scr = refs
    else:
        x_ref, lnw_ref, w_ref, o_ref, h_scr = refs

    @pl.when(pl.program_id(1) == 0)
    def _():
        x = x_ref[...]
        h = x * lax.rsqrt(jnp.mean(x * x, axis=-1, keepdims=True) + EPS) * lnw_ref[...]
        hb = h.astype(BF16)
        h_scr[...] = hb
        if with_gates:
            g_ref[...] = jnp.dot(hb, wg_ref[...], preferred_element_type=F32)

    res = jnp.dot(h_scr[...], w_ref[...], preferred_element_type=F32)
    for c in range(tn // LANES):
        o_ref[c] = res[:, c * LANES:(c + 1) * LANES].astype(o_ref.dtype)


def _inproj(x2d, ln_w, w_bf, n_tiles, col_tile, out_dtype, w_gates_pad=None):
    T, D = x2d.shape
    tm = min(INPROJ_TM, T)
    tn = INPROJ_TN
    with_gates = w_gates_pad is not None
    in_specs = [
        pl.BlockSpec((tm, D), lambda i, j: (i, 0)),
        pl.BlockSpec((1, D), lambda i, j: (0, 0)),
        pl.BlockSpec((D, tn), lambda i, j: (0, col_tile(j))),
    ]
    out_specs = [pl.BlockSpec((tn // LANES, tm, LANES), lambda i, j: (j, i, 0))]
    out_shape = [jax.ShapeDtypeStruct((n_tiles * tn // LANES, T, LANES), out_dtype)]
    args = [x2d, ln_w, w_bf]
    if with_gates:
        in_specs.append(pl.BlockSpec((D, LANES), lambda i, j: (0, 0)))
        out_specs.append(pl.BlockSpec((tm, LANES), lambda i, j: (i, 0)))
        out_shape.append(jax.ShapeDtypeStruct((T, LANES), F32))
        args.append(w_gates_pad)
    return pl.pallas_call(
        functools.partial(_inproj_body, tn=tn, with_gates=with_gates),
        grid=(T // tm, n_tiles),
        in_specs=in_specs,
        out_specs=out_specs,
        out_shape=out_shape,
        scratch_shapes=[pltpu.VMEM((tm, D), BF16)],
        compiler_params=_cparams(("parallel", "arbitrary")),
        name="inproj_gates" if with_gates else "inproj",
    )(*args)


def _hgrn_body(q_ref, f_ref, i_ref, g_ref, lb_ref, nw_ref, o_ref, st_scr, *, rows, heads):
    C, SUB = HG_CHUNK, HG_SUB
    nsub = C // SUB

    @pl.when(pl.program_id(2) == 0)
    def _():
        st_scr[...] = jnp.zeros_like(st_scr)

    lbs = [lb_ref[hh] for hh in range(heads)]
    log_lbs = [jnp.log(lb) for lb in lbs]
    log_1mlbs = [jnp.log1p(-lb) for lb in lbs]

    r_i = lax.broadcasted_iota(jnp.int32, (C, C), 0)
    c_i = lax.broadcasted_iota(jnp.int32, (C, C), 1)
    tri = (c_i <= r_i).astype(F32)
    sub_shift = SUB.bit_length() - 1
    diag_mask = (c_i <= r_i) & (jnp.right_shift(r_i, sub_shift) == jnp.right_shift(c_i, sub_shift))
    row_id = lax.broadcasted_iota(jnp.int32, (C, HG_DK), 0)

    pairs = [(j, hh) for j in range(rows // C) for hh in range(heads)]
    tri_b = tri.astype(BF16)
    st = {p: {} for p in pairs}

    for p in pairs:
        j, hh = p
        z = f_ref[hh, pl.ds(j * C, C), :].astype(F32)
        e = jnp.exp(-jnp.abs(z))
        e1 = 1.0 + e
        log_sig = jnp.minimum(z, 0.0) - jnp.log(e1)
        sig_neg = jnp.where(z >= 0, e, 1.0) / e1
        cc = log_1mlbs[hh] + log_sig
        log_f = jnp.maximum(log_lbs[hh], cc) + jnp.log(1.0 + jnp.exp(-jnp.abs(log_lbs[hh] - cc)))
        st[p]["kk"] = (1.0 - lbs[hh]) * sig_neg
        hi = log_f.astype(BF16)
        lo = (log_f - hi.astype(F32)).astype(BF16)
        st[p]["b"] = (jnp.dot(tri_b, hi, preferred_element_type=F32)
                      + jnp.dot(tri_b, lo, preferred_element_type=F32))

    for p in pairs:
        j, hh = p
        q = q_ref[hh, pl.ds(j * C, C), :].astype(F32)
        kk, b = st[p]["kk"], st[p]["b"]
        b_last = b[C - 1:C, :]
        st[p]["dec"] = jnp.exp(b_last)
        st[p]["qe"] = (q * jnp.exp(b)).astype(BF16)
        st[p]["kdec"] = (kk * jnp.exp(b_last - b)).astype(BF16)
        refs = [b[I * SUB:I * SUB + 1, :] for I in range(nsub)]
        refb = jnp.concatenate([jnp.broadcast_to(r, (SUB, HG_DK)) for r in refs], axis=0)
        qd = (q * jnp.exp(b - refb)).astype(BF16)
        kd = (kk * jnp.exp(refb - b)).astype(BF16)
        q_parts, k_parts = [], []
        for J in range(nsub - 1):
            r = refs[J + 1]
            qj = q * jnp.exp(jnp.minimum(b - r, 0.0))
            kj = kk * jnp.exp(jnp.minimum(r - b, 0.0))
            q_parts.append(jnp.where(row_id >= (J + 1) * SUB, qj, 0.0).astype(BF16))
            k_parts.append(jnp.where((row_id >= J * SUB) & (row_id < (J + 1) * SUB), kj, 0.0).astype(BF16))
        st[p]["att_d"] = _dot_nt(qd, kd)
        st[p]["att_o"] = _dot_nt(jnp.concatenate(q_parts, axis=1), jnp.concatenate(k_parts, axis=1))

    for p in pairs:
        j, hh = p
        vb = i_ref[hh, pl.ds(j * C, C), :].astype(BF16)
        att = jnp.where(diag_mask, st[p]["att_d"], 0.0) + st[p]["att_o"]
        st[p]["intra"] = jnp.dot(att.astype(BF16), vb, preferred_element_type=F32)
        st[p]["upd"] = _dot_tn(vb, st[p]["kdec"])

    for hh in range(heads):
        s = st_scr[hh]
        for j in range(rows // C):
            st[(j, hh)]["s_in"] = s.astype(BF16)
            s = s * st[(j, hh)]["dec"] + st[(j, hh)]["upd"]
        st_scr[hh] = s

    for p in pairs:
        j, hh = p
        g = g_ref[hh, pl.ds(j * C, C), :].astype(F32)
        o = _dot_nt(st[p]["qe"], st[p]["s_in"]) + st[p]["intra"]
        y = o * lax.rsqrt(jnp.mean(o * o, axis=-1, keepdims=True) + EPS) * nw_ref[hh]
        o_ref[pl.ds(j * C, C), hh * LANES:(hh + 1) * LANES] = (y * (g * _sigmoid(g))).astype(o_ref.dtype)


def _hgrn(proj_b, proj_f, lb, norm_w, batch, seq):
    T = batch * seq
    rows = min(HG_ROWS, seq)
    nblk = seq // rows
    H = HG_HEADS
    hp = HG_HEADS_PER_STEP

    def slab(off):
        return pl.BlockSpec((hp, rows, LANES), lambda b, h, c, off=off: (off // hp + h, b * nblk + c, 0))

    vec = pl.BlockSpec((hp, 1, LANES), lambda b, h, c: (h, 0, 0))
    return pl.pallas_call(
        functools.partial(_hgrn_body, rows=rows, heads=hp),
        grid=(batch, H // hp, nblk),
        in_specs=[slab(0), slab(0), slab(H), slab(2 * H), vec, vec],
        out_specs=pl.BlockSpec((rows, hp * LANES), lambda b, h, c: (b * nblk + c, h)),
        out_shape=jax.ShapeDtypeStruct((T, H * LANES), BF16),
        scratch_shapes=[pltpu.VMEM((hp, LANES, HG_DK), F32)],
        compiler_params=_cparams(("parallel", "parallel", "arbitrary")),
        name="hgrn2",
    )(proj_b, proj_f, proj_b, proj_b, lb.reshape(H, 1, HG_DK), norm_w.reshape(H, 1, LANES))


def _mlstm_body(q_ref, k_ref, v_ref, og_ref, gt_ref, gb_ref, cwq_ref, cwk_ref, cbq_ref, cbk_ref, nw_ref,
                out_ref, c_scr, n_scr, m_scr, qx_scr, kx_scr, qc_scr, kc_scr, *, rows):
    C = ML_CHUNK
    PAD = SUBLANES
    h = pl.program_id(1)

    @pl.when(pl.program_id(2) == 0)
    def _():
        c_scr[...] = jnp.zeros_like(c_scr)
        n_scr[...] = jnp.zeros_like(n_scr)
        m_scr[...] = jnp.zeros_like(m_scr)
        qx_scr[0:PAD, :] = jnp.zeros((PAD, LANES), F32)
        kx_scr[0:PAD, :] = jnp.zeros((PAD, LANES), F32)

    qx_scr[PAD:PAD + rows, :] = q_ref[0].astype(F32)
    kx_scr[PAD:PAD + rows, :] = k_ref[0].astype(F32)
    accq = jnp.zeros((rows, LANES), F32) + cbq_ref[...]
    acck = jnp.zeros((rows, LANES), F32) + cbk_ref[...]
    for j in range(CONV_W):
        off = PAD - (CONV_W - 1) + j
        accq = accq + cwq_ref[j:j + 1, :] * qx_scr[pl.ds(off, rows), :]
        acck = acck + cwk_ref[j:j + 1, :] * kx_scr[pl.ds(off, rows), :]
    qc_scr[...] = accq * _sigmoid(accq) * (ML_DQK ** -0.5)
    kc_scr[...] = acck * _sigmoid(acck)
    qx_scr[0:PAD, :] = qx_scr[rows:rows + PAD, :]
    kx_scr[0:PAD, :] = kx_scr[rows:rows + PAD, :]

    lane = lax.broadcasted_iota(jnp.int32, (C, C), 1)
    sub = lax.broadcasted_iota(jnp.int32, (C, C), 0)
    causal = lane <= sub
    lower = causal.astype(F32)
    upper = (sub <= lane).astype(F32)
    nw = nw_ref[...]
    gbias = gb_ref[0]

    chunks = range(rows // C)
    lower_b, upper_b = lower.astype(BF16), upper.astype(BF16)
    st = [{} for _ in chunks]

    def split(x):
        hi = x.astype(BF16)
        return hi, (x - hi.astype(F32)).astype(BF16)

    for j in chunks:
        gc = gt_ref[0, pl.ds(j * C, C), :] + gbias
        gct = gc.T
        li_col = gc[:, 0:1]
        li_row = gct[0:1, :]
        ch, cl = split(_log_sigmoid(gc[:, ML_HEADS:ML_HEADS + 1]))
        rh, rl = split(_log_sigmoid(gct[ML_HEADS:ML_HEADS + 1, :]))
        g_t = (jnp.dot(lower_b, jnp.broadcast_to(ch, (C, C)), preferred_element_type=F32)
               + jnp.dot(lower_b, jnp.broadcast_to(cl, (C, C)), preferred_element_type=F32))
        g_s = (jnp.dot(jnp.broadcast_to(rh, (C, C)), upper_b, preferred_element_type=F32)
               + jnp.dot(jnp.broadcast_to(rl, (C, C)), upper_b, preferred_element_type=F32))
        st[j].update(g_t=g_t, g_s=g_s, li_col=li_col, li_row=li_row)

    for j in chunks:
        c = st[j]
        g_col = c["g_t"][:, 0:1]
        g_last = c["g_t"][C - 1:C, 0:1]
        dmat = jnp.where(causal, c["g_t"] - c["g_s"] + c["li_row"], -jnp.inf)
        log_ws = g_last - g_col + c["li_col"]
        qf = qc_scr[pl.ds(j * C, C), :]
        kf = kc_scr[pl.ds(j * C, C), :]
        qb = qf.astype(BF16)
        c.update(g_col=g_col, g_last=g_last, dmat=dmat, dmax=jnp.max(dmat, axis=1, keepdims=True),
                 log_ws=log_ws, ws_max=jnp.max(log_ws, axis=0, keepdims=True), qf=qf, kf=kf, qb=qb,
                 qk=_dot_nt(qb, kf.astype(BF16)))

    m = m_scr[:, 0:1]
    for j in chunks:
        c = st[j]
        m_new = jnp.maximum(c["g_last"] + m, c["ws_max"])
        c.update(m_prev=m, m_new=m_new, decay=jnp.exp(c["g_last"] + m - m_new))
        m = m_new
    m_scr[...] = jnp.broadcast_to(m, m_scr.shape)

    for j in chunks:
        c = st[j]
        vb = jnp.concatenate([v_ref[0, pl.ds(j * C, C), :], v_ref[1, pl.ds(j * C, C), :]], axis=1).astype(BF16)
        kw = c["kf"] * jnp.exp(c["log_ws"] - c["m_new"])
        c.update(vb=vb, upd=_dot_tn(kw.astype(BF16), vb), ksum=jnp.sum(kw, axis=0, keepdims=True))

    cm = c_scr[...]
    nv = n_scr[...]
    for j in chunks:
        c = st[j]
        c.update(c_in=cm.astype(BF16), n_in=nv)
        cm = c["decay"] * cm + c["upd"]
        nv = c["decay"] * nv + c["ksum"]
    c_scr[...] = cm
    n_scr[...] = nv

    for j in chunks:
        c = st[j]
        a_inter = c["g_col"] + c["m_prev"]
        m_t = jnp.maximum(a_inter, c["dmax"])
        w_inter = jnp.exp(a_inter - m_t)
        sqk = c["qk"] * jnp.exp(c["dmat"] - m_t)
        num = (w_inter * jnp.dot(c["qb"], c["c_in"], preferred_element_type=F32)
               + jnp.dot(sqk.astype(BF16), c["vb"], preferred_element_type=F32))
        den = (w_inter * jnp.sum(c["qf"] * c["n_in"], axis=1, keepdims=True)
               + jnp.sum(sqk, axis=1, keepdims=True))
        hh = num * (1.0 / jnp.maximum(jnp.abs(den), jnp.exp(-m_t)))
        y = hh * lax.rsqrt(jnp.mean(hh * hh, axis=-1, keepdims=True) + EPS) * nw
        og = jnp.concatenate([og_ref[0, pl.ds(j * C, C), :], og_ref[1, pl.ds(j * C, C), :]], axis=1).astype(F32)
        out_ref[pl.ds(j * C, C), :] = (y * _sigmoid(og)).astype(out_ref.dtype)


def _mlstm(proj3, gates, gate_bias_pad, conv_w, conv_b, norm_w, batch, seq, q_off):
    T = batch * seq
    rows = min(ML_ROWS, seq)
    nblk = seq // rows
    H = ML_HEADS
    k_off = q_off + H
    v_off = k_off + H
    o_off = v_off + 2 * H

    def slab(off):
        return pl.BlockSpec((1, rows, LANES), lambda b, h, c, off=off: (off + h, b * nblk + c, 0))

    def slab2(off):
        return pl.BlockSpec((2, rows, LANES), lambda b, h, c, off=off: (off // 2 + h, b * nblk + c, 0))

    qk_w = H * ML_DQK
    return pl.pallas_call(
        functools.partial(_mlstm_body, rows=rows),
        grid=(batch, H, nblk),
        in_specs=[
            slab(q_off), slab(k_off), slab2(v_off), slab2(o_off),
            pl.BlockSpec((1, rows, LANES), lambda b, h, c: (h, b * nblk + c, 0)),
            pl.BlockSpec((1, 1, LANES), lambda b, h, c: (h, 0, 0)),
            pl.BlockSpec((CONV_W, LANES), lambda b, h, c: (0, h)),
            pl.BlockSpec((CONV_W, LANES), lambda b, h, c: (0, H + h)),
            pl.BlockSpec((1, LANES), lambda b, h, c: (0, h)),
            pl.BlockSpec((1, LANES), lambda b, h, c: (0, H + h)),
            pl.BlockSpec((1, ML_DV), lambda b, h, c: (0, h)),
        ],
        out_specs=pl.BlockSpec((rows, ML_DV), lambda b, h, c: (b * nblk + c, h)),
        out_shape=jax.ShapeDtypeStruct((T, H * ML_DV), BF16),
        scratch_shapes=[
            pltpu.VMEM((ML_DQK, ML_DV), F32),
            pltpu.VMEM((1, ML_DQK), F32),
            pltpu.VMEM((1, LANES), F32),
            pltpu.VMEM((rows + 2 * SUBLANES, LANES), F32),
            pltpu.VMEM((rows + 2 * SUBLANES, LANES), F32),
            pltpu.VMEM((rows, LANES), F32),
            pltpu.VMEM((rows, LANES), F32),
        ],
        compiler_params=_cparams(("parallel", "parallel", "arbitrary")),
        name="mlstm",
    )(proj3, proj3, proj3, proj3, gates, gate_bias_pad, conv_w, conv_w,
      conv_b.reshape(1, 2 * qk_w), conv_b.reshape(1, 2 * qk_w), norm_w.reshape(1, H * ML_DV))


def _outproj_body(a_ref, b_ref, x_ref, wo_ref, ln_ref, wrh_ref, wrl_ref, br_ref,
                  x2_ref, h2_ref, idx_ref, gate_ref, rank_ref, cnt_ref, cnt_scr, *, sub_rows):
    tm = x_ref.shape[0]
    ka = a_ref.shape[1]
    s = _packed_rows(x_ref.shape[1])

    @pl.when(pl.program_id(0) == 0)
    def _():
        cnt_scr[...] = jnp.zeros_like(cnt_scr)

    lane = lax.broadcasted_iota(jnp.int32, (sub_rows, LANES), 1).astype(F32)
    onehots = [[] for _ in range(TOP_K)]
    for r0 in range(0, tm, sub_rows):
        rows = pl.ds(r0, sub_rows)
        res = (jnp.dot(a_ref[rows, :], wo_ref[0:ka, :], preferred_element_type=F32)
               + jnp.dot(b_ref[rows, :], wo_ref[ka:, :], preferred_element_type=F32))
        x2 = x_ref[rows, :] + res
        x2_ref[rows, :] = x2
        h2 = x2 * lax.rsqrt(jnp.mean(x2 * x2, axis=-1, keepdims=True) + EPS) * ln_ref[...]
        _pack_store(h2_ref.at[pl.ds(r0 * s, sub_rows * s), :], h2)

        h_hi = h2.astype(BF16)
        h_lo = (h2 - h_hi.astype(F32)).astype(BF16)
        logits = (jnp.dot(h_hi, wrh_ref[...], preferred_element_type=F32)
                  + jnp.dot(h_lo, wrh_ref[...], preferred_element_type=F32)
                  + jnp.dot(h_hi, wrl_ref[...], preferred_element_type=F32)) + br_ref[...]
        vals, idxs = [], []
        cur = logits
        for _ in range(TOP_K):
            m = jnp.max(cur, axis=1, keepdims=True)
            ix = jnp.min(jnp.where(cur == m, lane, float(LANES)), axis=1, keepdims=True)
            vals.append(m)
            idxs.append(ix)
            cur = jnp.where(lane == ix, -jnp.inf, cur)
        es = [jnp.exp(v - vals[0]) for v in vals]
        inv = 1.0 / (es[0] + es[1] + es[2] + es[3])
        gate = jnp.zeros(logits.shape, F32)
        idx = jnp.zeros(logits.shape, F32)
        for k in range(TOP_K):
            gate = jnp.where(lane == float(k), es[k] * inv, gate)
            idx = jnp.where(lane == float(k), idxs[k], idx)
            onehots[k].append((lane == idxs[k]).astype(F32))
        gate_ref[rows, :] = gate
        idx_ref[rows, :] = idx.astype(jnp.int32)

    oh_k = [jnp.concatenate(o, axis=0) for o in onehots]
    oh = oh_k[0] + oh_k[1] + oh_k[2] + oh_k[3]
    r_i = lax.broadcasted_iota(jnp.int32, (tm, tm), 0)
    c_i = lax.broadcasted_iota(jnp.int32, (tm, tm), 1)
    before = jnp.dot((c_i < r_i).astype(BF16), oh.astype(BF16), preferred_element_type=F32) + cnt_scr[...]
    lane_t = lax.broadcasted_iota(jnp.int32, (tm, LANES), 1)
    rank = jnp.zeros((tm, LANES), F32)
    for k in range(TOP_K):
        rank = jnp.where(lane_t == k, jnp.sum(oh_k[k] * before, axis=1, keepdims=True), rank)
    rank_ref[...] = rank.astype(jnp.int32)
    cnt = cnt_scr[...] + jnp.sum(oh, axis=0, keepdims=True)
    cnt_scr[...] = cnt
    cnt_ref[...] = cnt.astype(jnp.int32)


def _outproj(a_out, b_out, x2d, w_out_bf, ln_w, wr_hi, wr_lo, b_router_pad):
    T, D = x2d.shape
    tm = min(OUTPROJ_TM, T)
    ka, kb = a_out.shape[1], b_out.shape[1]
    s = _packed_rows(D)
    row = lambda w: pl.BlockSpec((tm, w), lambda i: (i, 0))
    full = lambda r, c: pl.BlockSpec((r, c), lambda i: (0, 0))
    return pl.pallas_call(
        functools.partial(_outproj_body, sub_rows=min(OUTPROJ_SUB, tm)),
        grid=(T // tm,),
        in_specs=[row(ka), row(kb), row(D), full(ka + kb, D), full(1, D), full(D, LANES), full(D, LANES),
                  full(1, LANES)],
        out_specs=[row(D), pl.BlockSpec((tm * s, LANES), lambda i: (i, 0)), row(LANES), row(LANES), row(LANES),
                   full(1, LANES)],
        out_shape=[
            jax.ShapeDtypeStruct((T, D), F32),
            jax.ShapeDtypeStruct((T * s, LANES), jnp.uint32),
            jax.ShapeDtypeStruct((T, LANES), jnp.int32),
            jax.ShapeDtypeStruct((T, LANES), F32),
            jax.ShapeDtypeStruct((T, LANES), jnp.int32),
            jax.ShapeDtypeStruct((1, LANES), jnp.int32),
        ],
        scratch_shapes=[pltpu.VMEM((1, LANES), F32)],
        compiler_params=_cparams(("arbitrary",)),
        name="outproj_router",
    )(a_out, b_out, x2d, w_out_bf, ln_w, wr_hi, wr_lo, b_router_pad)


def _start_row_gather(idx_at, src_ref, buf, sem, n, s):
    def start(pair, carry):
        for p in range(DMA_QUEUES):
            i = pair * DMA_QUEUES + p
            src_row = pl.multiple_of(idx_at(i) * s, s)
            dst_row = pl.multiple_of(i * s, s)
            pltpu.make_async_copy(src_ref.at[pl.ds(src_row, s), :], buf.at[pl.ds(dst_row, s), :],
                                  sem).start(priority=p)
        return carry

    lax.fori_loop(0, n // DMA_QUEUES, start, 0, unroll=DMA_UNROLL // DMA_QUEUES)


def _wait_row_gather(buf, sem):
    pltpu.make_async_copy(buf, buf, sem).wait()


def _dispatch_body(zero_ref, dest_ref, h_ref, xs_ref, zbuf, sem, zsem, *, ntok, s, tm, n_blocks):
    blk = tm * s

    def zero_copy(b):
        return pltpu.make_async_copy(zbuf, xs_ref.at[pl.ds(pl.multiple_of(b * blk, blk), blk), :], zsem)

    @pl.when(pl.program_id(0) == 0)
    def _():
        zbuf[...] = jnp.zeros_like(zbuf)

        def zstart(b, carry):
            @pl.when(zero_ref[b] == 1)
            def _():
                zero_copy(b).start()
            return carry

        def zwait(b, carry):
            @pl.when(zero_ref[b] == 1)
            def _():
                zero_copy(b).wait()
            return carry

        lax.fori_loop(0, n_blocks, zstart, 0)
        lax.fori_loop(0, n_blocks, zwait, 0)

    def start(t, carry):
        src = h_ref.at[pl.ds(pl.multiple_of(t * s, s), s), :]
        for k in range(TOP_K):
            dst_row = pl.multiple_of(dest_ref[0, 0, t * TOP_K + k] * s, s)
            pltpu.make_async_copy(src, xs_ref.at[pl.ds(dst_row, s), :], sem).start(priority=k % DMA_QUEUES)
        return carry

    lax.fori_loop(0, ntok, start, 0, unroll=DMA_UNROLL // TOP_K)
    for _ in range(TOP_K):
        pltpu.make_async_copy(h_ref, h_ref, sem).wait()


def _dispatch(dest, zero_blk, h_packed, d, n_slots, tm):
    s = _packed_rows(d)
    T = h_packed.shape[0] // s
    ntok = min(DISPATCH_TOKENS, T)
    n = ntok * TOP_K
    grid_spec = pltpu.PrefetchScalarGridSpec(
        num_scalar_prefetch=1,
        grid=(T // ntok,),
        in_specs=[
            pl.BlockSpec((1, 1, n), lambda i, z: (i, 0, 0), memory_space=pltpu.SMEM),
            pl.BlockSpec((ntok * s, LANES), lambda i, z: (i, 0)),
        ],
        out_specs=pl.BlockSpec(memory_space=pl.ANY),
        scratch_shapes=[pltpu.VMEM((tm * s, LANES), jnp.uint32), pltpu.SemaphoreType.DMA,
                        pltpu.SemaphoreType.DMA],
    )
    return pl.pallas_call(
        functools.partial(_dispatch_body, ntok=ntok, s=s, tm=tm, n_blocks=zero_blk.shape[0]),
        grid_spec=grid_spec,
        out_shape=jax.ShapeDtypeStruct((n_slots * s, LANES), jnp.uint32),
        compiler_params=_cparams(("arbitrary",)),
        name="moe_dispatch",
    )(zero_blk, dest.reshape(T // ntok, 1, n), h_packed)


def _staged_weights(sched_refs, hbm_refs, stage_refs, bf_refs, sem):
    e_ref, wt_ref, first_ref, ne_ref, nt_ref, more_ref = sched_refs
    w = pl.program_id(0)
    tn = stage_refs[0].shape[1]

    def copies(e, t):
        col = pl.multiple_of(t * tn, tn)
        return [pltpu.make_async_copy(h.at[e, :, pl.ds(col, tn)], st, sem.at[k])
                for k, (h, st) in enumerate(zip(hbm_refs, stage_refs))]

    def start_all(cs):
        for c in cs:
            c.start()

    @pl.when(w == 0)
    def _():
        start_all(copies(e_ref[0], wt_ref[0]))

    @pl.when(first_ref[w] == 1)
    def _():
        for c in copies(e_ref[w], wt_ref[w]):
            c.wait()
        for st, bf in zip(stage_refs, bf_refs):
            for r in range(0, st.shape[0], 256):
                bf[r:r + 256, :] = st[r:r + 256, :].astype(BF16)

        @pl.when(more_ref[w] == 1)
        def _():
            start_all(copies(ne_ref[w], nt_ref[w]))


def _row_groups(nrows, tm, compute, clear):
    sub = min(MOE_SUB_ROWS, tm)
    groups = tm // sub
    for live in range(groups + 1):
        @pl.when((nrows > (live - 1) * sub) & (nrows <= live * sub))
        def _(live=live):
            for g in range(live):
                compute(g * sub, sub)
            for g in range(live, groups):
                clear(g * sub, sub)


def _moe_up_body(e_ref, wt_ref, r_ref, ot_ref, first_ref, nrows_ref, ne_ref, nt_ref, more_ref,
                 x_ref, wg_ref, wu_ref, bg_ref, bu_ref, o_ref, wg_stage, wu_stage, wg_scr, wu_scr, x_scr, sem):
    w = pl.program_id(0)
    _staged_weights((e_ref, wt_ref, first_ref, ne_ref, nt_ref, more_ref), (wg_ref, wu_ref),
                    (wg_stage, wu_stage), (wg_scr, wu_scr), sem)
    tm, d = x_scr.shape
    s = _packed_rows(d)

    def compute(r0, n):
        rows = pl.ds(r0, n)
        for c in range(s):
            hi, lo = _unpack_load(x_ref, r0, n, s, c)
            x_scr[rows, c * LANES:(c + 1) * LANES] = hi.astype(BF16)
            x_scr[rows, d // 2 + c * LANES:d // 2 + (c + 1) * LANES] = lo.astype(BF16)
        x = x_scr[rows, :]
        gt = jnp.dot(x, wg_scr[...], preferred_element_type=F32) + bg_ref[0]
        up = jnp.dot(x, wu_scr[...], preferred_element_type=F32) + bu_ref[0]
        gt = jnp.minimum(gt, SWIGLU_LIMIT)
        up = jnp.clip(up, -SWIGLU_LIMIT, SWIGLU_LIMIT)
        swish = (0.5 * gt) * (1.0 + jnp.tanh((0.5 * SWIGLU_ALPHA) * gt))
        o_ref[rows, :] = ((up + 1.0) * swish).astype(o_ref.dtype)

    def clear(r0, n):
        o_ref[pl.ds(r0, n), :] = jnp.zeros((n, o_ref.shape[1]), o_ref.dtype)

    _row_groups(nrows_ref[w], tm, compute, clear)


def _moe_down_body(e_ref, wt_ref, r_ref, ot_ref, first_ref, nrows_ref, ne_ref, nt_ref, more_ref,
                   a_ref, wd_ref, bd_ref, o_ref, wd_stage, wd_scr, sem):
    w = pl.program_id(0)
    _staged_weights((e_ref, wt_ref, first_ref, ne_ref, nt_ref, more_ref), (wd_ref,), (wd_stage,), (wd_scr,), sem)
    tm = a_ref.shape[0]
    s = o_ref.shape[0] // tm

    def compute(r0, n):
        y = jnp.dot(a_ref[pl.ds(r0, n), :], wd_scr[...], preferred_element_type=F32) + bd_ref[0]
        _pack_store(o_ref.at[pl.ds(r0 * s, n * s), :], y)

    def clear(r0, n):
        o_ref[pl.ds(r0 * s, n * s), :] = jnp.zeros((n * s, LANES), o_ref.dtype)

    _row_groups(nrows_ref[w], tm, compute, clear)


def _moe_schedule(counts, tm, n_tiles, n_blocks):
    n_items = n_tiles * n_blocks
    experts = jnp.arange(N_EXPERTS, dtype=jnp.int32)
    blocks_e = (counts + tm - 1) // tm
    bend = jnp.cumsum(blocks_e)
    bstart = bend - blocks_e
    item_end = n_tiles * bend
    total = item_end[-1]
    later = (experts[None, :] > experts[:, None]) & (blocks_e[None, :] > 0)
    next_e = jnp.min(jnp.where(later, experts[None, :], N_EXPERTS - 1), axis=1)

    w = jnp.arange(n_items, dtype=jnp.int32)
    valid = w < total
    wc = jnp.minimum(w, jnp.maximum(total - 1, 0))
    e = jnp.minimum(jnp.sum((item_end[None, :] <= wc[:, None]).astype(jnp.int32), axis=1), N_EXPERTS - 1)
    sel = e[:, None] == experts[None, :]
    pick = lambda table: jnp.sum(jnp.where(sel, table[None, :], 0), axis=1)
    nb = jnp.maximum(pick(blocks_e), 1)
    local = wc - n_tiles * pick(bstart)
    wtile = sum((local >= t * nb).astype(jnp.int32) for t in range(1, n_tiles)) if n_tiles > 1 else 0 * local
    jblk = local - wtile * nb
    spare = jnp.maximum(w - total, 0)
    rblk = jnp.where(valid, pick(bstart) + jblk, bend[-1] + spare // n_tiles)
    otile = jnp.where(valid, wtile, spare % n_tiles)
    nrows = jnp.where(valid, jnp.clip(pick(counts) - jblk * tm, 0, tm), 0)
    first = (jblk == 0) & valid
    last_tile = wtile == n_tiles - 1
    more = first & (w + nb < total)
    i32 = lambda a: a.astype(jnp.int32)
    return (e, i32(wtile), i32(rblk), i32(otile), i32(first), i32(nrows),
            i32(jnp.where(last_tile, pick(next_e), e)), i32(jnp.where(last_tile, 0, wtile + 1)), i32(more))


def _moe_up(sched, xs_packed, w_gate, w_up, b_gate, b_up):
    D, d_ff = w_gate.shape[1], w_gate.shape[2]
    s = _packed_rows(D)
    n_slots = xs_packed.shape[0] // s
    tm, tf = MOE_TM, min(MOE_TF, d_ff)
    n_items = sched[0].shape[0]
    wspec = pl.BlockSpec(memory_space=pl.ANY)
    bspec = pl.BlockSpec((1, 1, tf), lambda w, e, wt, r, ot, *_: (e[w], 0, wt[w]))
    grid_spec = pltpu.PrefetchScalarGridSpec(
        num_scalar_prefetch=len(sched),
        grid=(n_items,),
        in_specs=[pl.BlockSpec((tm * s, LANES), lambda w, e, wt, r, ot, *_: (r[w], 0)),
                  wspec, wspec, bspec, bspec],
        out_specs=pl.BlockSpec((tm, tf), lambda w, e, wt, r, ot, *_: (r[w], ot[w])),
        scratch_shapes=[pltpu.VMEM((D, tf), F32), pltpu.VMEM((D, tf), F32),
                        pltpu.VMEM((D, tf), BF16), pltpu.VMEM((D, tf), BF16), pltpu.VMEM((tm, D), BF16),
                        pltpu.SemaphoreType.DMA((2,))],
    )
    return pl.pallas_call(
        _moe_up_body,
        grid_spec=grid_spec,
        out_shape=jax.ShapeDtypeStruct((n_slots, d_ff), BF16),
        compiler_params=_cparams(("arbitrary",)),
        name="moe_up",
    )(*sched, xs_packed, w_gate, w_up, b_gate.reshape(N_EXPERTS, 1, d_ff), b_up.reshape(N_EXPERTS, 1, d_ff))


def _moe_down(sched, act, w_down, b_down):
    n_slots, d_ff = act.shape
    D = w_down.shape[2]
    tm = MOE_TM
    s = _packed_rows(D)
    n_items = sched[0].shape[0]
    grid_spec = pltpu.PrefetchScalarGridSpec(
        num_scalar_prefetch=len(sched),
        grid=(n_items,),
        in_specs=[
            pl.BlockSpec((tm, d_ff), lambda w, e, wt, r, ot, *_: (r[w], 0)),
            pl.BlockSpec(memory_space=pl.ANY),
            pl.BlockSpec((1, 1, D), lambda w, e, wt, r, ot, *_: (e[w], 0, 0)),
        ],
        out_specs=pl.BlockSpec((tm * s, LANES), lambda w, e, wt, r, ot, *_: (r[w], 0)),
        scratch_shapes=[pltpu.VMEM((d_ff, D), F32), pltpu.VMEM((d_ff, D), BF16), pltpu.SemaphoreType.DMA((1,))],
    )
    return pl.pallas_call(
        _moe_down_body,
        grid_spec=grid_spec,
        out_shape=jax.ShapeDtypeStruct((n_slots * s, LANES), jnp.uint32),
        compiler_params=_cparams(("arbitrary",)),
        name="moe_down",
    )(*sched, act, w_down, b_down.reshape(N_EXPERTS, 1, D))


def _final_body(dest_ref, next_ref, x2_ref, gate_ref, w_ref, ys_ref, o_ref, buf0, buf1, sem0, sem1, *, tm, s):
    n = tm * TOP_K
    i = pl.program_id(0)

    def gather(idx_ref, tile, buf, sem):
        _start_row_gather(lambda r: idx_ref[0, 0, tile * n + r], ys_ref, buf, sem, n, s)

    def combine(tile, buf):
        rows = pl.ds(tile * tm, tm)
        gate = gate_ref[rows, :]
        half = o_ref.shape[1] // 2
        o_ref[rows, :] = x2_ref[rows, :]
        for k in range(TOP_K):
            g = gate[:, k:k + 1]
            for c in range(s):
                hi, lo = _unpack_load(buf, k * tm, tm, s, c)
                o_ref[rows, c * LANES:(c + 1) * LANES] += g * hi
                o_ref[rows, half + c * LANES:half + (c + 1) * LANES] += g * lo
        acc = o_ref[rows, :]
        o_ref[rows, :] = acc * lax.rsqrt(jnp.mean(acc * acc, axis=-1, keepdims=True) + EPS) * w_ref[...]

    @pl.when(i == 0)
    def _():
        gather(dest_ref, 0, buf0, sem0)

    gather(dest_ref, 1, buf1, sem1)
    _wait_row_gather(buf0, sem0)
    combine(0, buf0)
    gather(next_ref, 0, buf0, sem0)
    _wait_row_gather(buf1, sem1)
    combine(1, buf1)

    @pl.when(i == pl.num_programs(0) - 1)
    def _():
        _wait_row_gather(buf0, sem0)


def _final(x2, ys_packed, dest, gates_pad, w):
    T, D = x2.shape
    tm = min(FINAL_TM, T // 2)
    s = _packed_rows(D)
    n = tm * TOP_K
    steps = T // (2 * tm)
    dest_km = dest.reshape(steps, 2, tm, TOP_K).transpose(0, 1, 3, 2).reshape(steps, 1, 2 * n)
    buf = pltpu.VMEM((n * s, LANES), jnp.uint32)
    return pl.pallas_call(
        functools.partial(_final_body, tm=tm, s=s),
        grid=(steps,),
        in_specs=[
            pl.BlockSpec((1, 1, 2 * n), lambda i: (i, 0, 0), memory_space=pltpu.SMEM),
            pl.BlockSpec((1, 1, 2 * n), lambda i: (jnp.minimum(i + 1, steps - 1), 0, 0), memory_space=pltpu.SMEM),
            pl.BlockSpec((2 * tm, D), lambda i: (i, 0)),
            pl.BlockSpec((2 * tm, LANES), lambda i: (i, 0)),
            pl.BlockSpec((1, D), lambda i: (0, 0)),
            pl.BlockSpec(memory_space=pl.ANY),
        ],
        out_specs=pl.BlockSpec((2 * tm, D), lambda i: (i, 0)),
        out_shape=jax.ShapeDtypeStruct((T, D), F32),
        scratch_shapes=[buf, buf, pltpu.SemaphoreType.DMA, pltpu.SemaphoreType.DMA],
        compiler_params=_cparams(("arbitrary",)),
        name="final_norm",
    )(dest_km, dest_km, x2, gates_pad, w, ys_packed)


def _moe(h2_packed, T, D, top_idx, rank, counts, w_gate, b_gate, w_up, b_up, w_down, b_down):
    A = T * TOP_K
    tm = MOE_TM
    n_blocks = (A + N_EXPERTS * (tm - 1) + tm - 1) // tm
    n_slots = n_blocks * tm

    blocks_e = (counts + tm - 1) // tm
    bend = jnp.cumsum(blocks_e)
    bstart = bend - blocks_e
    experts = jnp.arange(N_EXPERTS, dtype=jnp.int32)
    first_slot = jnp.sum(jnp.where(top_idx[:, :, None] == experts, bstart * tm, 0), axis=-1)
    dest = (first_slot + rank).astype(jnp.int32).reshape(A)
    blk = jnp.arange(n_blocks, dtype=jnp.int32)
    is_last = jnp.any((blk[:, None] == bend[None, :] - 1) & (blocks_e[None, :] > 0), axis=1)
    zero_blk = ((blk >= bend[-1]) | is_last).astype(jnp.int32)

    xs = _dispatch(dest, zero_blk, h2_packed, D, n_slots, tm)
    d_ff = w_gate.shape[2]
    act = _moe_up(_moe_schedule(counts, tm, -(-d_ff // MOE_TF), n_blocks), xs, w_gate, w_up, b_gate, b_up)
    ys = _moe_down(_moe_schedule(counts, tm, 1, n_blocks), act, w_down, b_down)
    return ys, dest


def kernel(x, ln1_w, w_in, hg_lb_logits, hg_norm_w, ml_conv_w, ml_conv_b, ml_igate_b, ml_fgate_b, ml_norm_w,
           w_out, ln2_w, w_router, b_router, w_gate, b_gate, w_up, b_up, w_down, b_down, final_norm_w):
    B, S, D = x.shape
    T = B * S
    depth = w_in.shape[0]
    hg_w = HG_HEADS * HG_DK
    n_main = 4 * hg_w + 2 * ML_HEADS * ML_DQK + 2 * ML_HEADS * ML_DV
    lb_all = jnp.cumsum(jax.nn.softmax(hg_lb_logits.astype(F32), axis=0), axis=0)

    xc = x.reshape(T, D)
    for l in range(depth):
        w_gates_pad = jnp.pad(w_in[l][:, n_main:], ((0, 0), (0, LANES - 2 * ML_HEADS))).astype(BF16)
        w_bf = w_in[l].astype(BF16)
        ln1 = ln1_w[l].reshape(1, D)
        hf_tiles = hg_w // INPROJ_TN
        proj_f = _inproj(xc, ln1, w_bf, hf_tiles, lambda j: j + hf_tiles, F32)[0]
        proj_b, gates = _inproj(xc, ln1, w_bf, n_main // INPROJ_TN - hf_tiles,
                                lambda j: jnp.where(j >= hf_tiles, j + hf_tiles, j), BF16, w_gates_pad)
        a_out = _hgrn(proj_b, proj_f, lb_all[l], hg_norm_w[l], B, S)
        gate_bias = jnp.pad(jnp.concatenate([ml_igate_b[l], ml_fgate_b[l]]), (0, LANES - 2 * ML_HEADS))
        gates_h = jnp.stack([jnp.roll(gates, -hd, axis=1) for hd in range(ML_HEADS)])
        bias_h = jnp.stack([jnp.roll(gate_bias, -hd) for hd in range(ML_HEADS)]).reshape(ML_HEADS, 1, LANES)
        b_out = _mlstm(proj_b, gates_h, bias_h, ml_conv_w[l], ml_conv_b[l], ml_norm_w[l], B, S, 3 * HG_HEADS)
        wr_pad = jnp.pad(w_router[l], ((0, 0), (0, LANES - N_EXPERTS)))
        wr_hi = wr_pad.astype(BF16)
        wr_lo = (wr_pad - wr_hi.astype(F32)).astype(BF16)
        br_pad = jnp.pad(b_router[l], (0, LANES - N_EXPERTS), constant_values=-1e30).reshape(1, LANES)
        x2, h2, idx_pad, gates_pad, rank_pad, cnt = _outproj(
            a_out, b_out, xc, w_out[l].astype(BF16), ln2_w[l].reshape(1, D), wr_hi, wr_lo, br_pad)
        ys, dest = _moe(h2, T, D, idx_pad[:, :TOP_K], rank_pad[:, :TOP_K], cnt[0, :N_EXPERTS],
                        w_gate[l], b_gate[l], w_up[l], b_up[l], w_down[l], b_down[l])
        if l + 1 < depth:
            raise NotImplementedError("only the final layer fuses the output norm")
        xc = _final(x2, ys, dest, gates_pad, final_norm_w.reshape(1, D))
    return xc.reshape(B, S, D)
```

```python
import functools

import jax
import jax.numpy as jnp
from jax import lax
from jax.experimental import pallas as pl
from jax.experimental.pallas import tpu as pltpu

F32 = jnp.float32
BF16 = jnp.bfloat16

EPS = 1e-6
HG_HEADS = 8
HG_DK = 128
ML_HEADS = 4
ML_DQK = 128
ML_DV = 256
CONV_W = 4
N_EXPERTS = 32
TOP_K = 4
SWIGLU_ALPHA = 1.702
SWIGLU_LIMIT = 7.0

LANES = 128
SUBLANES = 8
VMEM_LIMIT_BYTES = 56 * 1024 * 1024

HG_CHUNK = 64
HG_SUB = 16
ML_CHUNK = 128
HG_ROWS = 1024
ML_ROWS = 512
HG_HEADS_PER_STEP = 2

INPROJ_TM = 1024
INPROJ_TN = 1024
OUTPROJ_TM = 256
OUTPROJ_SUB = 128
DISPATCH_TOKENS = 256
MOE_TM = 512
MOE_TF = 1024
MOE_X_CACHE_BLOCKS = 6
MOE_SUB_ROWS = 128
DMA_UNROLL = 32
DMA_QUEUES = 2
FINAL_TM = 128


def _dot_nt(a, b):
    return lax.dot_general(a, b, (((1,), (1,)), ((), ())), preferred_element_type=F32)


def _dot_tn(a, b):
    return lax.dot_general(a, b, (((0,), (0,)), ((), ())), preferred_element_type=F32)


def _log_sigmoid(z):
    return jnp.minimum(z, 0.0) - jnp.log(1.0 + jnp.exp(-jnp.abs(z)))


def _sigmoid(z):
    return 0.5 * jnp.tanh(0.5 * z) + 0.5


def _cparams(semantics):
    return pltpu.CompilerParams(dimension_semantics=semantics, vmem_limit_bytes=VMEM_LIMIT_BYTES)


_HI_MASK = 0xFFFF0000


def _packed_rows(d):
    return d // (2 * LANES)


def _pack_store(o_ref, v):
    n, d = v.shape
    s, half = _packed_rows(d), d // 2
    bits = pltpu.bitcast(v.astype(BF16).astype(F32), jnp.uint32)
    for c in range(s):
        hi = bits[:, c * LANES:(c + 1) * LANES]
        lo = bits[:, half + c * LANES:half + (c + 1) * LANES]
        o_ref[pl.ds(c, n, stride=s), :] = hi | jnp.right_shift(lo, jnp.uint32(16))


def _unpack_load(buf, first_row, n, s, c):
    w = buf[pl.ds(first_row * s + c, n, stride=s), :]
    hi = pltpu.bitcast(w & jnp.uint32(_HI_MASK), F32)
    lo = pltpu.bitcast(jnp.left_shift(w, jnp.uint32(16)), F32)
    return hi, lo


def _inproj_body(*refs, tn, with_gates):
    if with_gates:
        x_ref, lnw_ref, w_ref, wg_ref, o_ref, g_ref, h_scr = refs
    else:
        x_ref, lnw_ref, w_ref, o_ref, h_scr = refs

    @pl.when(pl.program_id(1) == 0)
    def _():
        x = x_ref[...]
        h = x * lax.rsqrt(jnp.mean(x * x, axis=-1, keepdims=True) + EPS) * lnw_ref[...]
        hb = h.astype(BF16)
        h_scr[...] = hb
        if with_gates:
            g_ref[...] = jnp.dot(hb, wg_ref[...], preferred_element_type=F32)

    res = jnp.dot(h_scr[...], w_ref[...], preferred_element_type=F32)
    for c in range(tn // LANES):
        o_ref[c] = res[:, c * LANES:(c + 1) * LANES].astype(o_ref.dtype)


def _inproj(x2d, ln_w, w_bf, n_tiles, col_tile, out_dtype, w_gates_pad=None):
    T, D = x2d.shape
    tm = min(INPROJ_TM, T)
    tn = INPROJ_TN
    with_gates = w_gates_pad is not None
    in_specs = [
        pl.BlockSpec((tm, D), lambda i, j: (i, 0)),
        pl.BlockSpec((1, D), lambda i, j: (0, 0)),
        pl.BlockSpec((D, tn), lambda i, j: (0, col_tile(j))),
    ]
    out_specs = [pl.BlockSpec((tn // LANES, tm, LANES), lambda i, j: (j, i, 0))]
    out_shape = [jax.ShapeDtypeStruct((n_tiles * tn // LANES, T, LANES), out_dtype)]
    args = [x2d, ln_w, w_bf]
    if with_gates:
        in_specs.append(pl.BlockSpec((D, LANES), lambda i, j: (0, 0)))
        out_specs.append(pl.BlockSpec((tm, LANES), lambda i, j: (i, 0)))
        out_shape.append(jax.ShapeDtypeStruct((T, LANES), F32))
        args.append(w_gates_pad)
    return pl.pallas_call(
        functools.partial(_inproj_body, tn=tn, with_gates=with_gates),
        grid=(T // tm, n_tiles),
        in_specs=in_specs,
        out_specs=out_specs,
        out_shape=out_shape,
        scratch_shapes=[pltpu.VMEM((tm, D), BF16)],
        compiler_params=_cparams(("parallel", "arbitrary")),
        name="inproj_gates" if with_gates else "inproj",
    )(*args)


def _hgrn_body(q_ref, f_ref, i_ref, g_ref, lb_ref, nw_ref, o_ref, st_scr, *, rows, heads):
    C, SUB = HG_CHUNK, HG_SUB
    nsub = C // SUB

    @pl.when(pl.program_id(2) == 0)
    def _():
        st_scr[...] = jnp.zeros_like(st_scr)

    lbs = [lb_ref[hh] for hh in range(heads)]
    log_lbs = [jnp.log(lb) for lb in lbs]
    log_1mlbs = [jnp.log1p(-lb) for lb in lbs]

    r_i = lax.broadcasted_iota(jnp.int32, (C, C), 0)
    c_i = lax.broadcasted_iota(jnp.int32, (C, C), 1)
    tri = (c_i <= r_i).astype(F32)
    sub_shift = SUB.bit_length() - 1
    diag_mask = (c_i <= r_i) & (jnp.right_shift(r_i, sub_shift) == jnp.right_shift(c_i, sub_shift))
    row_id = lax.broadcasted_iota(jnp.int32, (C, HG_DK), 0)

    pairs = [(j, hh) for j in range(rows // C) for hh in range(heads)]
    tri_b = tri.astype(BF16)
    st = {p: {} for p in pairs}

    for p in pairs:
        j, hh = p
        z = f_ref[hh, pl.ds(j * C, C), :].astype(F32)
        e = jnp.exp(-jnp.abs(z))
        e1 = 1.0 + e
        log_sig = jnp.minimum(z, 0.0) - jnp.log(e1)
        sig_neg = jnp.where(z >= 0, e, 1.0) / e1
        cc = log_1mlbs[hh] + log_sig
        log_f = jnp.maximum(log_lbs[hh], cc) + jnp.log(1.0 + jnp.exp(-jnp.abs(log_lbs[hh] - cc)))
        st[p]["kk"] = (1.0 - lbs[hh]) * sig_neg
        hi = log_f.astype(BF16)
        lo = (log_f - hi.astype(F32)).astype(BF16)
        st[p]["b"] = (jnp.dot(tri_b, hi, preferred_element_type=F32)
                      + jnp.dot(tri_b, lo, preferred_element_type=F32))

    for p in pairs:
        j, hh = p
        q = q_ref[hh, pl.ds(j * C, C), :].astype(F32)
        kk, b = st[p]["kk"], st[p]["b"]
        b_last = b[C - 1:C, :]
        st[p]["dec"] = jnp.exp(b_last)
        st[p]["qe"] = (q * jnp.exp(b)).astype(BF16)
        st[p]["kdec"] = (kk * jnp.exp(b_last - b)).astype(BF16)
        refs = [b[I * SUB:I * SUB + 1, :] for I in range(nsub)]
        refb = jnp.concatenate([jnp.broadcast_to(r, (SUB, HG_DK)) for r in refs], axis=0)
        qd = (q * jnp.exp(b - refb)).astype(BF16)
        kd = (kk * jnp.exp(refb - b)).astype(BF16)
        q_parts, k_parts = [], []
        for J in range(nsub - 1):
            r = refs[J + 1]
            qj = q * jnp.exp(jnp.minimum(b - r, 0.0))
            kj = kk * jnp.exp(jnp.minimum(r - b, 0.0))
            q_parts.append(jnp.where(row_id >= (J + 1) * SUB, qj, 0.0).astype(BF16))
            k_parts.append(jnp.where((row_id >= J * SUB) & (row_id < (J + 1) * SUB), kj, 0.0).astype(BF16))
        st[p]["att_d"] = _dot_nt(qd, kd)
        st[p]["att_o"] = _dot_nt(jnp.concatenate(q_parts, axis=1), jnp.concatenate(k_parts, axis=1))

    for p in pairs:
        j, hh = p
        vb = i_ref[hh, pl.ds(j * C, C), :].astype(BF16)
        att = jnp.where(diag_mask, st[p]["att_d"], 0.0) + st[p]["att_o"]
        st[p]["intra"] = jnp.dot(att.astype(BF16), vb, preferred_element_type=F32)
        st[p]["upd"] = _dot_tn(vb, st[p]["kdec"])

    for hh in range(heads):
        s = st_scr[hh]
        for j in range(rows // C):
            st[(j, hh)]["s_in"] = s.astype(BF16)
            s = s * st[(j, hh)]["dec"] + st[(j, hh)]["upd"]
        st_scr[hh] = s

    for p in pairs:
        j, hh = p
        g = g_ref[hh, pl.ds(j * C, C), :].astype(F32)
        o = _dot_nt(st[p]["qe"], st[p]["s_in"]) + st[p]["intra"]
        y = o * lax.rsqrt(jnp.mean(o * o, axis=-1, keepdims=True) + EPS) * nw_ref[hh]
        o_ref[pl.ds(j * C, C), hh * LANES:(hh + 1) * LANES] = (y * (g * _sigmoid(g))).astype(o_ref.dtype)


def _hgrn(proj_b, proj_f, lb, norm_w, batch, seq):
    T = batch * seq
    rows = min(HG_ROWS, seq)
    nblk = seq // rows
    H = HG_HEADS
    hp = HG_HEADS_PER_STEP

    def slab(off):
        return pl.BlockSpec((hp, rows, LANES), lambda b, h, c, off=off: (off // hp + h, b * nblk + c, 0))

    vec = pl.BlockSpec((hp, 1, LANES), lambda b, h, c: (h, 0, 0))
    return pl.pallas_call(
        functools.partial(_hgrn_body, rows=rows, heads=hp),
        grid=(batch, H // hp, nblk),
        in_specs=[slab(0), slab(0), slab(H), slab(2 * H), vec, vec],
        out_specs=pl.BlockSpec((rows, hp * LANES), lambda b, h, c: (b * nblk + c, h)),
        out_shape=jax.ShapeDtypeStruct((T, H * LANES), BF16),
        scratch_shapes=[pltpu.VMEM((hp, LANES, HG_DK), F32)],
        compiler_params=_cparams(("parallel", "parallel", "arbitrary")),
        name="hgrn2",
    )(proj_b, proj_f, proj_b, proj_b, lb.reshape(H, 1, HG_DK), norm_w.reshape(H, 1, LANES))


def _mlstm_body(q_ref, k_ref, v_ref, og_ref, gt_ref, gb_ref, cwq_ref, cwk_ref, cbq_ref, cbk_ref, nw_ref,
                out_ref, c_scr, n_scr, m_scr, qx_scr, kx_scr, qc_scr, kc_scr, *, rows):
    C = ML_CHUNK
    PAD = SUBLANES
    h = pl.program_id(1)

    @pl.when(pl.program_id(2) == 0)
    def _():
        c_scr[...] = jnp.zeros_like(c_scr)
        n_scr[...] = jnp.zeros_like(n_scr)
        m_scr[...] = jnp.zeros_like(m_scr)
        qx_scr[0:PAD, :] = jnp.zeros((PAD, LANES), F32)
        kx_scr[0:PAD, :] = jnp.zeros((PAD, LANES), F32)

    qx_scr[PAD:PAD + rows, :] = q_ref[0].astype(F32)
    kx_scr[PAD:PAD + rows, :] = k_ref[0].astype(F32)
    accq = jnp.zeros((rows, LANES), F32) + cbq_ref[...]
    acck = jnp.zeros((rows, LANES), F32) + cbk_ref[...]
    for j in range(CONV_W):
        off = PAD - (CONV_W - 1) + j
        accq = accq + cwq_ref[j:j + 1, :] * qx_scr[pl.ds(off, rows), :]
        acck = acck + cwk_ref[j:j + 1, :] * kx_scr[pl.ds(off, rows), :]
    qc_scr[...] = accq * _sigmoid(accq) * (ML_DQK ** -0.5)
    kc_scr[...] = acck * _sigmoid(acck)
    qx_scr[0:PAD, :] = qx_scr[rows:rows + PAD, :]
    kx_scr[0:PAD, :] = kx_scr[rows:rows + PAD, :]

    lane = lax.broadcasted_iota(jnp.int32, (C, C), 1)
    sub = lax.broadcasted_iota(jnp.int32, (C, C), 0)
    causal = lane <= sub
    lower = causal.astype(F32)
    upper = (sub <= lane).astype(F32)
    nw = nw_ref[...]
    gbias = gb_ref[0]

    chunks = range(rows // C)
    lower_b, upper_b = lower.astype(BF16), upper.astype(BF16)
    st = [{} for _ in chunks]

    def split(x):
        hi = x.astype(BF16)
        return hi, (x - hi.astype(F32)).astype(BF16)

    for j in chunks:
        gc = gt_ref[0, pl.ds(j * C, C), :] + gbias
        gct = gc.T
        li_col = gc[:, 0:1]
        li_row = gct[0:1, :]
        ch, cl = split(_log_sigmoid(gc[:, ML_HEADS:ML_HEADS + 1]))
        rh, rl = split(_log_sigmoid(gct[ML_HEADS:ML_HEADS + 1, :]))
        g_t = (jnp.dot(lower_b, jnp.broadcast_to(ch, (C, C)), preferred_element_type=F32)
               + jnp.dot(lower_b, jnp.broadcast_to(cl, (C, C)), preferred_element_type=F32))
        g_s = (jnp.dot(jnp.broadcast_to(rh, (C, C)), upper_b, preferred_element_type=F32)
               + jnp.dot(jnp.broadcast_to(rl, (C, C)), upper_b, preferred_element_type=F32))
        st[j].update(g_t=g_t, g_s=g_s, li_col=li_col, li_row=li_row)

    for j in chunks:
        c = st[j]
        g_col = c["g_t"][:, 0:1]
        g_last = c["g_t"][C - 1:C, 0:1]
        dmat = jnp.where(causal, c["g_t"] - c["g_s"] + c["li_row"], -jnp.inf)
        log_ws = g_last - g_col + c["li_col"]
        qf = qc_scr[pl.ds(j * C, C), :]
        kf = kc_scr[pl.ds(j * C, C), :]
        qb = qf.astype(BF16)
        c.update(g_col=g_col, g_last=g_last, dmat=dmat, dmax=jnp.max(dmat, axis=1, keepdims=True),
                 log_ws=log_ws, ws_max=jnp.max(log_ws, axis=0, keepdims=True), qf=qf, kf=kf, qb=qb,
                 qk=_dot_nt(qb, kf.astype(BF16)))

    m = m_scr[:, 0:1]
    for j in chunks:
        c = st[j]
        m_new = jnp.maximum(c["g_last"] + m, c["ws_max"])
        c.update(m_prev=m, m_new=m_new, decay=jnp.exp(c["g_last"] + m - m_new))
        m = m_new
    m_scr[...] = jnp.broadcast_to(m, m_scr.shape)

    for j in chunks:
        c = st[j]
        vb = jnp.concatenate([v_ref[0, pl.ds(j * C, C), :], v_ref[1, pl.ds(j * C, C), :]], axis=1).astype(BF16)
        kw = c["kf"] * jnp.exp(c["log_ws"] - c["m_new"])
        c.update(vb=vb, upd=_dot_tn(kw.astype(BF16), vb), ksum=jnp.sum(kw, axis=0, keepdims=True))

    cm = c_scr[...]
    nv = n_scr[...]
    for j in chunks:
        c = st[j]
        c.update(c_in=cm.astype(BF16), n_in=nv)
        cm = c["decay"] * cm + c["upd"]
        nv = c["decay"] * nv + c["ksum"]
    c_scr[...] = cm
    n_scr[...] = nv

    for j in chunks:
        c = st[j]
        a_inter = c["g_col"] + c["m_prev"]
        m_t = jnp.maximum(a_inter, c["dmax"])
        w_inter = jnp.exp(a_inter - m_t)
        sqk = c["qk"] * jnp.exp(c["dmat"] - m_t)
        num = (w_inter * jnp.dot(c["qb"], c["c_in"], preferred_element_type=F32)
               + jnp.dot(sqk.astype(BF16), c["vb"], preferred_element_type=F32))
        den = (w_inter * jnp.sum(c["qf"] * c["n_in"], axis=1, keepdims=True)
               + jnp.sum(sqk, axis=1, keepdims=True))
        hh = num * (1.0 / jnp.maximum(jnp.abs(den), jnp.exp(-m_t)))
        y = hh * lax.rsqrt(jnp.mean(hh * hh, axis=-1, keepdims=True) + EPS) * nw
        og = jnp.concatenate([og_ref[0, pl.ds(j * C, C), :], og_ref[1, pl.ds(j * C, C), :]], axis=1).astype(F32)
        out_ref[pl.ds(j * C, C), :] = (y * _sigmoid(og)).astype(out_ref.dtype)


def _mlstm(proj3, gates, gate_bias_pad, conv_w, conv_b, norm_w, batch, seq, q_off):
    T = batch * seq
    rows = min(ML_ROWS, seq)
    nblk = seq // rows
    H = ML_HEADS
    k_off = q_off + H
    v_off = k_off + H
    o_off = v_off + 2 * H

    def slab(off):
        return pl.BlockSpec((1, rows, LANES), lambda b, h, c, off=off: (off + h, b * nblk + c, 0))

    def slab2(off):
        return pl.BlockSpec((2, rows, LANES), lambda b, h, c, off=off: (off // 2 + h, b * nblk + c, 0))

    qk_w = H * ML_DQK
    return pl.pallas_call(
        functools.partial(_mlstm_body, rows=rows),
        grid=(batch, H, nblk),
        in_specs=[
            slab(q_off), slab(k_off), slab2(v_off), slab2(o_off),
            pl.BlockSpec((1, rows, LANES), lambda b, h, c: (h, b * nblk + c, 0)),
            pl.BlockSpec((1, 1, LANES), lambda b, h, c: (h, 0, 0)),
            pl.BlockSpec((CONV_W, LANES), lambda b, h, c: (0, h)),
            pl.BlockSpec((CONV_W, LANES), lambda b, h, c: (0, H + h)),
            pl.BlockSpec((1, LANES), lambda b, h, c: (0, h)),
            pl.BlockSpec((1, LANES), lambda b, h, c: (0, H + h)),
            pl.BlockSpec((1, ML_DV), lambda b, h, c: (0, h)),
        ],
        out_specs=pl.BlockSpec((rows, ML_DV), lambda b, h, c: (b * nblk + c, h)),
        out_shape=jax.ShapeDtypeStruct((T, H * ML_DV), BF16),
        scratch_shapes=[
            pltpu.VMEM((ML_DQK, ML_DV), F32),
            pltpu.VMEM((1, ML_DQK), F32),
            pltpu.VMEM((1, LANES), F32),
            pltpu.VMEM((rows + 2 * SUBLANES, LANES), F32),
            pltpu.VMEM((rows + 2 * SUBLANES, LANES), F32),
            pltpu.VMEM((rows, LANES), F32),
            pltpu.VMEM((rows, LANES), F32),
        ],
        compiler_params=_cparams(("parallel", "parallel", "arbitrary")),
        name="mlstm",
    )(proj3, proj3, proj3, proj3, gates, gate_bias_pad, conv_w, conv_w,
      conv_b.reshape(1, 2 * qk_w), conv_b.reshape(1, 2 * qk_w), norm_w.reshape(1, H * ML_DV))


def _outproj_body(a_ref, b_ref, x_ref, wo_ref, ln_ref, wrh_ref, wrl_ref, br_ref,
                  x2_ref, h2_ref, idx_ref, gate_ref, rank_ref, cnt_ref, cnt_scr, *, sub_rows):
    tm = x_ref.shape[0]
    ka = a_ref.shape[1]
    s = _packed_rows(x_ref.shape[1])

    @pl.when(pl.program_id(0) == 0)
    def _():
        cnt_scr[...] = jnp.zeros_like(cnt_scr)

    lane = lax.broadcasted_iota(jnp.int32, (sub_rows, LANES), 1).astype(F32)
    onehots = [[] for _ in range(TOP_K)]
    for r0 in range(0, tm, sub_rows):
        rows = pl.ds(r0, sub_rows)
        res = (jnp.dot(a_ref[rows, :], wo_ref[0:ka, :], preferred_element_type=F32)
               + jnp.dot(b_ref[rows, :], wo_ref[ka:, :], preferred_element_type=F32))
        x2 = x_ref[rows, :] + res
        x2_ref[rows, :] = x2
        h2 = x2 * lax.rsqrt(jnp.mean(x2 * x2, axis=-1, keepdims=True) + EPS) * ln_ref[...]
        _pack_store(h2_ref.at[pl.ds(r0 * s, sub_rows * s), :], h2)

        h_hi = h2.astype(BF16)
        h_lo = (h2 - h_hi.astype(F32)).astype(BF16)
        logits = (jnp.dot(h_hi, wrh_ref[...], preferred_element_type=F32)
                  + jnp.dot(h_lo, wrh_ref[...], preferred_element_type=F32)
                  + jnp.dot(h_hi, wrl_ref[...], preferred_element_type=F32)) + br_ref[...]
        vals, idxs = [], []
        cur = logits
        for _ in range(TOP_K):
            m = jnp.max(cur, axis=1, keepdims=True)
            ix = jnp.min(jnp.where(cur == m, lane, float(LANES)), axis=1, keepdims=True)
            vals.append(m)
            idxs.append(ix)
            cur = jnp.where(lane == ix, -jnp.inf, cur)
        es = [jnp.exp(v - vals[0]) for v in vals]
        inv = 1.0 / (es[0] + es[1] + es[2] + es[3])
        gate = jnp.zeros(logits.shape, F32)
        idx = jnp.zeros(logits.shape, F32)
        for k in range(TOP_K):
            gate = jnp.where(lane == float(k), es[k] * inv, gate)
            idx = jnp.where(lane == float(k), idxs[k], idx)
            onehots[k].append((lane == idxs[k]).astype(F32))
        gate_ref[rows, :] = gate
        idx_ref[rows, :] = idx.astype(jnp.int32)

    oh_k = [jnp.concatenate(o, axis=0) for o in onehots]
    oh = oh_k[0] + oh_k[1] + oh_k[2] + oh_k[3]
    r_i = lax.broadcasted_iota(jnp.int32, (tm, tm), 0)
    c_i = lax.broadcasted_iota(jnp.int32, (tm, tm), 1)
    before = jnp.dot((c_i < r_i).astype(BF16), oh.astype(BF16), preferred_element_type=F32) + cnt_scr[...]
    lane_t = lax.broadcasted_iota(jnp.int32, (tm, LANES), 1)
    rank = jnp.zeros((tm, LANES), F32)
    for k in range(TOP_K):
        rank = jnp.where(lane_t == k, jnp.sum(oh_k[k] * before, axis=1, keepdims=True), rank)
    rank_ref[...] = rank.astype(jnp.int32)
    cnt = cnt_scr[...] + jnp.sum(oh, axis=0, keepdims=True)
    cnt_scr[...] = cnt
    cnt_ref[...] = cnt.astype(jnp.int32)


def _outproj(a_out, b_out, x2d, w_out_bf, ln_w, wr_hi, wr_lo, b_router_pad):
    T, D = x2d.shape
    tm = min(OUTPROJ_TM, T)
    ka, kb = a_out.shape[1], b_out.shape[1]
    s = _packed_rows(D)
    row = lambda w: pl.BlockSpec((tm, w), lambda i: (i, 0))
    full = lambda r, c: pl.BlockSpec((r, c), lambda i: (0, 0))
    return pl.pallas_call(
        functools.partial(_outproj_body, sub_rows=min(OUTPROJ_SUB, tm)),
        grid=(T // tm,),
        in_specs=[row(ka), row(kb), row(D), full(ka + kb, D), full(1, D), full(D, LANES), full(D, LANES),
                  full(1, LANES)],
        out_specs=[row(D), pl.BlockSpec((tm * s, LANES), lambda i: (i, 0)), row(LANES), row(LANES), row(LANES),
                   full(1, LANES)],
        out_shape=[
            jax.ShapeDtypeStruct((T, D), F32),
            jax.ShapeDtypeStruct((T * s, LANES), jnp.uint32),
            jax.ShapeDtypeStruct((T, LANES), jnp.int32),
            jax.ShapeDtypeStruct((T, LANES), F32),
            jax.ShapeDtypeStruct((T, LANES), jnp.int32),
            jax.ShapeDtypeStruct((1, LANES), jnp.int32),
        ],
        scratch_shapes=[pltpu.VMEM((1, LANES), F32)],
        compiler_params=_cparams(("arbitrary",)),
        name="outproj_router",
    )(a_out, b_out, x2d, w_out_bf, ln_w, wr_hi, wr_lo, b_router_pad)


def _start_row_gather(idx_at, src_ref, buf, sem, n, s):
    def start(pair, carry):
        for p in range(DMA_QUEUES):
            i = pair * DMA_QUEUES + p
            src_row = pl.multiple_of(idx_at(i) * s, s)
            dst_row = pl.multiple_of(i * s, s)
            pltpu.make_async_copy(src_ref.at[pl.ds(src_row, s), :], buf.at[pl.ds(dst_row, s), :],
                                  sem).start(priority=p)
        return carry

    lax.fori_loop(0, n // DMA_QUEUES, start, 0, unroll=DMA_UNROLL // DMA_QUEUES)


def _wait_row_gather(buf, sem):
    pltpu.make_async_copy(buf, buf, sem).wait()


def _dispatch_body(zero_ref, dest_ref, h_ref, xs_ref, zbuf, sem, zsem, *, ntok, s, tm, n_blocks):
    blk = tm * s

    def zero_copy(b):
        return pltpu.make_async_copy(zbuf, xs_ref.at[pl.ds(pl.multiple_of(b * blk, blk), blk), :], zsem)

    @pl.when(pl.program_id(0) == 0)
    def _():
        zbuf[...] = jnp.zeros_like(zbuf)

        def zstart(b, carry):
            @pl.when(zero_ref[b] == 1)
            def _():
                zero_copy(b).start()
            return carry

        def zwait(b, carry):
            @pl.when(zero_ref[b] == 1)
            def _():
                zero_copy(b).wait()
            return carry

        lax.fori_loop(0, n_blocks, zstart, 0)
        lax.fori_loop(0, n_blocks, zwait, 0)

    def start(t, carry):
        src = h_ref.at[pl.ds(pl.multiple_of(t * s, s), s), :]
        for k in range(TOP_K):
            dst_row = pl.multiple_of(dest_ref[0, 0, t * TOP_K + k] * s, s)
            pltpu.make_async_copy(src, xs_ref.at[pl.ds(dst_row, s), :], sem).start(priority=k % DMA_QUEUES)
        return carry

    lax.fori_loop(0, ntok, start, 0, unroll=DMA_UNROLL // TOP_K)
    for _ in range(TOP_K):
        pltpu.make_async_copy(h_ref, h_ref, sem).wait()


def _dispatch(dest, zero_blk, h_packed, d, n_slots, tm):
    s = _packed_rows(d)
    T = h_packed.shape[0] // s
    ntok = min(DISPATCH_TOKENS, T)
    n = ntok * TOP_K
    grid_spec = pltpu.PrefetchScalarGridSpec(
        num_scalar_prefetch=1,
        grid=(T // ntok,),
        in_specs=[
            pl.BlockSpec((1, 1, n), lambda i, z: (i, 0, 0), memory_space=pltpu.SMEM),
            pl.BlockSpec((ntok * s, LANES), lambda i, z: (i, 0)),
        ],
        out_specs=pl.BlockSpec(memory_space=pl.ANY),
        scratch_shapes=[pltpu.VMEM((tm * s, LANES), jnp.uint32), pltpu.SemaphoreType.DMA,
                        pltpu.SemaphoreType.DMA],
    )
    return pl.pallas_call(
        functools.partial(_dispatch_body, ntok=ntok, s=s, tm=tm, n_blocks=zero_blk.shape[0]),
        grid_spec=grid_spec,
        out_shape=jax.ShapeDtypeStruct((n_slots * s, LANES), jnp.uint32),
        compiler_params=_cparams(("arbitrary",)),
        name="moe_dispatch",
    )(zero_blk, dest.reshape(T // ntok, 1, n), h_packed)


def _staged_weights(sched_refs, hbm_refs, stage_refs, bf_refs, sem):
    e_ref, wt_ref, first_ref, ne_ref, nt_ref, more_ref = sched_refs
    w = pl.program_id(0)
    tn = stage_refs[0].shape[1]

    def copies(e, t):
        col = pl.multiple_of(t * tn, tn)
        return [pltpu.make_async_copy(h.at[e, :, pl.ds(col, tn)], st, sem.at[k])
                for k, (h, st) in enumerate(zip(hbm_refs, stage_refs))]

    def start_all(cs):
        for c in cs:
            c.start()

    @pl.when(w == 0)
    def _():
        start_all(copies(e_ref[0], wt_ref[0]))

    @pl.when(first_ref[w] == 1)
    def _():
        for c in copies(e_ref[w], wt_ref[w]):
            c.wait()
        for st, bf in zip(stage_refs, bf_refs):
            for r in range(0, st.shape[0], 256):
                bf[r:r + 256, :] = st[r:r + 256, :].astype(BF16)

        @pl.when(more_ref[w] == 1)
        def _():
            start_all(copies(ne_ref[w], nt_ref[w]))


def _row_groups(nrows, tm, compute, clear):
    sub = min(MOE_SUB_ROWS, tm)
    groups = tm // sub
    for live in range(groups + 1):
        @pl.when((nrows > (live - 1) * sub) & (nrows <= live * sub))
        def _(live=live):
            for g in range(live):
                compute(g * sub, sub)
            for g in range(live, groups):
                clear(g * sub, sub)


def _moe_up_body(e_ref, wt_ref, r_ref, ot_ref, first_ref, nrows_ref, ne_ref, nt_ref, more_ref, jb_ref,
                 x_ref, wg_ref, wu_ref, bg_ref, bu_ref, o_ref, wg_stage, wu_stage, wg_scr, wu_scr, x_cache, sem):
    w = pl.program_id(0)
    _staged_weights((e_ref, wt_ref, first_ref, ne_ref, nt_ref, more_ref), (wg_ref, wu_ref),
                    (wg_stage, wu_stage), (wg_scr, wu_scr), sem)
    tm = o_ref.shape[0]
    d = x_cache.shape[1]
    s = _packed_rows(d)
    last_slot = x_cache.shape[0] // tm - 1
    slot = jnp.minimum(jb_ref[w], last_slot)
    base = pl.multiple_of(slot * tm, tm)
    fresh = (wt_ref[w] == 0) | (slot == last_slot)

    def unpack(r0, n):
        rows = pl.ds(base + r0, n)
        for c in range(s):
            hi, lo = _unpack_load(x_ref, r0, n, s, c)
            x_cache[rows, c * LANES:(c + 1) * LANES] = hi.astype(BF16)
            x_cache[rows, d // 2 + c * LANES:d // 2 + (c + 1) * LANES] = lo.astype(BF16)

    def matmuls(r0, n):
        x = x_cache[pl.ds(base + r0, n), :]
        gt = jnp.dot(x, wg_scr[...], preferred_element_type=F32) + bg_ref[0]
        up = jnp.dot(x, wu_scr[...], preferred_element_type=F32) + bu_ref[0]
        gt = jnp.minimum(gt, SWIGLU_LIMIT)
        up = jnp.clip(up, -SWIGLU_LIMIT, SWIGLU_LIMIT)
        swish = (0.5 * gt) * (1.0 + jnp.tanh((0.5 * SWIGLU_ALPHA) * gt))
        o_ref[pl.ds(r0, n), :] = ((up + 1.0) * swish).astype(o_ref.dtype)

    def unpack_and_matmuls(r0, n):
        unpack(r0, n)
        matmuls(r0, n)

    def clear(r0, n):
        o_ref[pl.ds(r0, n), :] = jnp.zeros((n, o_ref.shape[1]), o_ref.dtype)

    @pl.when(fresh)
    def _():
        _row_groups(nrows_ref[w], tm, unpack_and_matmuls, clear)

    @pl.when(jnp.logical_not(fresh))
    def _():
        _row_groups(nrows_ref[w], tm, matmuls, clear)


def _moe_down_body(e_ref, wt_ref, r_ref, ot_ref, first_ref, nrows_ref, ne_ref, nt_ref, more_ref, jb_ref,
                   a_ref, wd_ref, bd_ref, o_ref, wd_stage, wd_scr, sem):
    w = pl.program_id(0)
    _staged_weights((e_ref, wt_ref, first_ref, ne_ref, nt_ref, more_ref), (wd_ref,), (wd_stage,), (wd_scr,), sem)
    tm = a_ref.shape[0]
    s = o_ref.shape[0] // tm

    def compute(r0, n):
        y = jnp.dot(a_ref[pl.ds(r0, n), :], wd_scr[...], preferred_element_type=F32) + bd_ref[0]
        _pack_store(o_ref.at[pl.ds(r0 * s, n * s), :], y)

    def clear(r0, n):
        o_ref[pl.ds(r0 * s, n * s), :] = jnp.zeros((n * s, LANES), o_ref.dtype)

    _row_groups(nrows_ref[w], tm, compute, clear)


def _moe_schedule(counts, tm, n_tiles, n_blocks):
    n_items = n_tiles * n_blocks
    experts = jnp.arange(N_EXPERTS, dtype=jnp.int32)
    blocks_e = (counts + tm - 1) // tm
    bend = jnp.cumsum(blocks_e)
    bstart = bend - blocks_e
    item_end = n_tiles * bend
    total = item_end[-1]
    later = (experts[None, :] > experts[:, None]) & (blocks_e[None, :] > 0)
    next_e = jnp.min(jnp.where(later, experts[None, :], N_EXPERTS - 1), axis=1)

    w = jnp.arange(n_items, dtype=jnp.int32)
    valid = w < total
    wc = jnp.minimum(w, jnp.maximum(total - 1, 0))
    e = jnp.minimum(jnp.sum((item_end[None, :] <= wc[:, None]).astype(jnp.int32), axis=1), N_EXPERTS - 1)
    sel = e[:, None] == experts[None, :]
    pick = lambda table: jnp.sum(jnp.where(sel, table[None, :], 0), axis=1)
    nb = jnp.maximum(pick(blocks_e), 1)
    local = wc - n_tiles * pick(bstart)
    wtile = sum((local >= t * nb).astype(jnp.int32) for t in range(1, n_tiles)) if n_tiles > 1 else 0 * local
    jblk = local - wtile * nb
    spare = jnp.maximum(w - total, 0)
    rblk = jnp.where(valid, pick(bstart) + jblk, bend[-1] + spare // n_tiles)
    otile = jnp.where(valid, wtile, spare % n_tiles)
    nrows = jnp.where(valid, jnp.clip(pick(counts) - jblk * tm, 0, tm), 0)
    first = (jblk == 0) & valid
    last_tile = wtile == n_tiles - 1
    more = first & (w + nb < total)
    i32 = lambda a: a.astype(jnp.int32)
    return (e, i32(wtile), i32(rblk), i32(otile), i32(first), i32(nrows),
            i32(jnp.where(last_tile, pick(next_e), e)), i32(jnp.where(last_tile, 0, wtile + 1)), i32(more),
            i32(jblk))


def _moe_up(sched, xs_packed, w_gate, w_up, b_gate, b_up):
    D, d_ff = w_gate.shape[1], w_gate.shape[2]
    s = _packed_rows(D)
    n_slots = xs_packed.shape[0] // s
    tm, tf = MOE_TM, min(MOE_TF, d_ff)
    n_items = sched[0].shape[0]
    wspec = pl.BlockSpec(memory_space=pl.ANY)
    bspec = pl.BlockSpec((1, 1, tf), lambda w, e, wt, r, ot, *_: (e[w], 0, wt[w]))
    grid_spec = pltpu.PrefetchScalarGridSpec(
        num_scalar_prefetch=len(sched),
        grid=(n_items,),
        in_specs=[pl.BlockSpec((tm * s, LANES), lambda w, e, wt, r, ot, *_: (r[w], 0)),
                  wspec, wspec, bspec, bspec],
        out_specs=pl.BlockSpec((tm, tf), lambda w, e, wt, r, ot, *_: (r[w], ot[w])),
        scratch_shapes=[pltpu.VMEM((D, tf), F32), pltpu.VMEM((D, tf), F32),
                        pltpu.VMEM((D, tf), BF16), pltpu.VMEM((D, tf), BF16),
                        pltpu.VMEM(((MOE_X_CACHE_BLOCKS + 1) * tm, D), BF16),
                        pltpu.SemaphoreType.DMA((2,))],
    )
    return pl.pallas_call(
        _moe_up_body,
        grid_spec=grid_spec,
        out_shape=jax.ShapeDtypeStruct((n_slots, d_ff), BF16),
        compiler_params=_cparams(("arbitrary",)),
        name="moe_up",
    )(*sched, xs_packed, w_gate, w_up, b_gate.reshape(N_EXPERTS, 1, d_ff), b_up.reshape(N_EXPERTS, 1, d_ff))


def _moe_down(sched, act, w_down, b_down):
    n_slots, d_ff = act.shape
    D = w_down.shape[2]
    tm = MOE_TM
    s = _packed_rows(D)
    n_items = sched[0].shape[0]
    grid_spec = pltpu.PrefetchScalarGridSpec(
        num_scalar_prefetch=len(sched),
        grid=(n_items,),
        in_specs=[
            pl.BlockSpec((tm, d_ff), lambda w, e, wt, r, ot, *_: (r[w], 0)),
            pl.BlockSpec(memory_space=pl.ANY),
            pl.BlockSpec((1, 1, D), lambda w, e, wt, r, ot, *_: (e[w], 0, 0)),
        ],
        out_specs=pl.BlockSpec((tm * s, LANES), lambda w, e, wt, r, ot, *_: (r[w], 0)),
        scratch_shapes=[pltpu.VMEM((d_ff, D), F32), pltpu.VMEM((d_ff, D), BF16), pltpu.SemaphoreType.DMA((1,))],
    )
    return pl.pallas_call(
        _moe_down_body,
        grid_spec=grid_spec,
        out_shape=jax.ShapeDtypeStruct((n_slots * s, LANES), jnp.uint32),
        compiler_params=_cparams(("arbitrary",)),
        name="moe_down",
    )(*sched, act, w_down, b_down.reshape(N_EXPERTS, 1, D))


def _final_body(dest_ref, next_ref, x2_ref, gate_ref, w_ref, ys_ref, o_ref, buf0, buf1, sem0, sem1, *, tm, s):
    n = tm * TOP_K
    i = pl.program_id(0)

    def gather(idx_ref, tile, buf, sem):
        _start_row_gather(lambda r: idx_ref[0, 0, tile * n + r], ys_ref, buf, sem, n, s)

    def combine(tile, buf):
        rows = pl.ds(tile * tm, tm)
        gate = gate_ref[rows, :]
        half = o_ref.shape[1] // 2
        o_ref[rows, :] = x2_ref[rows, :]
        for k in range(TOP_K):
            g = gate[:, k:k + 1]
            for c in range(s):
                hi, lo = _unpack_load(buf, k * tm, tm, s, c)
                o_ref[rows, c * LANES:(c + 1) * LANES] += g * hi
                o_ref[rows, half + c * LANES:half + (c + 1) * LANES] += g * lo
        acc = o_ref[rows, :]
        o_ref[rows, :] = acc * lax.rsqrt(jnp.mean(acc * acc, axis=-1, keepdims=True) + EPS) * w_ref[...]

    @pl.when(i == 0)
    def _():
        gather(dest_ref, 0, buf0, sem0)

    gather(dest_ref, 1, buf1, sem1)
    _wait_row_gather(buf0, sem0)
    combine(0, buf0)
    gather(next_ref, 0, buf0, sem0)
    _wait_row_gather(buf1, sem1)
    combine(1, buf1)

    @pl.when(i == pl.num_programs(0) - 1)
    def _():
        _wait_row_gather(buf0, sem0)


def _final(x2, ys_packed, dest, gates_pad, w):
    T, D = x2.shape
    tm = min(FINAL_TM, T // 2)
    s = _packed_rows(D)
    n = tm * TOP_K
    steps = T // (2 * tm)
    dest_km = dest.reshape(steps, 2, tm, TOP_K).transpose(0, 1, 3, 2).reshape(steps, 1, 2 * n)
    buf = pltpu.VMEM((n * s, LANES), jnp.uint32)
    return pl.pallas_call(
        functools.partial(_final_body, tm=tm, s=s),
        grid=(steps,),
        in_specs=[
            pl.BlockSpec((1, 1, 2 * n), lambda i: (i, 0, 0), memory_space=pltpu.SMEM),
            pl.BlockSpec((1, 1, 2 * n), lambda i: (jnp.minimum(i + 1, steps - 1), 0, 0), memory_space=pltpu.SMEM),
            pl.BlockSpec((2 * tm, D), lambda i: (i, 0)),
            pl.BlockSpec((2 * tm, LANES), lambda i: (i, 0)),
            pl.BlockSpec((1, D), lambda i: (0, 0)),
            pl.BlockSpec(memory_space=pl.ANY),
        ],
        out_specs=pl.BlockSpec((2 * tm, D), lambda i: (i, 0)),
        out_shape=jax.ShapeDtypeStruct((T, D), F32),
        scratch_shapes=[buf, buf, pltpu.SemaphoreType.DMA, pltpu.SemaphoreType.DMA],
        compiler_params=_cparams(("arbitrary",)),
        name="final_norm",
    )(dest_km, dest_km, x2, gates_pad, w, ys_packed)


def _moe(h2_packed, T, D, top_idx, rank, counts, w_gate, b_gate, w_up, b_up, w_down, b_down):
    A = T * TOP_K
    tm = MOE_TM
    n_blocks = (A + N_EXPERTS * (tm - 1) + tm - 1) // tm
    n_slots = n_blocks * tm

    blocks_e = (counts + tm - 1) // tm
    bend = jnp.cumsum(blocks_e)
    bstart = bend - blocks_e
    experts = jnp.arange(N_EXPERTS, dtype=jnp.int32)
    first_slot = jnp.sum(jnp.where(top_idx[:, :, None] == experts, bstart * tm, 0), axis=-1)
    dest = (first_slot + rank).astype(jnp.int32).reshape(A)
    blk = jnp.arange(n_blocks, dtype=jnp.int32)
    is_last = jnp.any((blk[:, None] == bend[None, :] - 1) & (blocks_e[None, :] > 0), axis=1)
    zero_blk = ((blk >= bend[-1]) | is_last).astype(jnp.int32)

    xs = _dispatch(dest, zero_blk, h2_packed, D, n_slots, tm)
    d_ff = w_gate.shape[2]
    act = _moe_up(_moe_schedule(counts, tm, -(-d_ff // MOE_TF), n_blocks), xs, w_gate, w_up, b_gate, b_up)
    ys = _moe_down(_moe_schedule(counts, tm, 1, n_blocks), act, w_down, b_down)
    return ys, dest


def kernel(x, ln1_w, w_in, hg_lb_logits, hg_norm_w, ml_conv_w, ml_conv_b, ml_igate_b, ml_fgate_b, ml_norm_w,
           w_out, ln2_w, w_router, b_router, w_gate, b_gate, w_up, b_up, w_down, b_down, final_norm_w):
    B, S, D = x.shape
    T = B * S
    depth = w_in.shape[0]
    hg_w = HG_HEADS * HG_DK
    n_main = 4 * hg_w + 2 * ML_HEADS * ML_DQK + 2 * ML_HEADS * ML_DV
    lb_all = jnp.cumsum(jax.nn.softmax(hg_lb_logits.astype(F32), axis=0), axis=0)

    xc = x.reshape(T, D)
    for l in range(depth):
        w_gates_pad = jnp.pad(w_in[l][:, n_main:], ((0, 0), (0, LANES - 2 * ML_HEADS))).astype(BF16)
        w_bf = w_in[l].astype(BF16)
        ln1 = ln1_w[l].reshape(1, D)
        hf_tiles = hg_w // INPROJ_TN
        proj_f = _inproj(xc, ln1, w_bf, hf_tiles, lambda j: j + hf_tiles, F32)[0]
        proj_b, gates = _inproj(xc, ln1, w_bf, n_main // INPROJ_TN - hf_tiles,
                                lambda j: jnp.where(j >= hf_tiles, j + hf_tiles, j), BF16, w_gates_pad)
        a_out = _hgrn(proj_b, proj_f, lb_all[l], hg_norm_w[l], B, S)
        gate_bias = jnp.pad(jnp.concatenate([ml_igate_b[l], ml_fgate_b[l]]), (0, LANES - 2 * ML_HEADS))
        gates_h = jnp.stack([jnp.roll(gates, -hd, axis=1) for hd in range(ML_HEADS)])
        bias_h = jnp.stack([jnp.roll(gate_bias, -hd) for hd in range(ML_HEADS)]).reshape(ML_HEADS, 1, LANES)
        b_out = _mlstm(proj_b, gates_h, bias_h, ml_conv_w[l], ml_conv_b[l], ml_norm_w[l], B, S, 3 * HG_HEADS)
        wr_pad = jnp.pad(w_router[l], ((0, 0), (0, LANES - N_EXPERTS)))
        wr_hi = wr_pad.astype(BF16)
        wr_lo = (wr_pad - wr_hi.astype(F32)).astype(BF16)
        br_pad = jnp.pad(b_router[l], (0, LANES - N_EXPERTS), constant_values=-1e30).reshape(1, LANES)
        x2, h2, idx_pad, gates_pad, rank_pad, cnt = _outproj(
            a_out, b_out, xc, w_out[l].astype(BF16), ln2_w[l].reshape(1, D), wr_hi, wr_lo, br_pad)
        ys, dest = _moe(h2, T, D, idx_pad[:, :TOP_K], rank_pad[:, :TOP_K], cnt[0, :N_EXPERTS],
                        w_gate[l], b_gate[l], w_up[l], b_up[l], w_down[l], b_down[l])
        if l + 1 < depth:
            raise NotImplementedError("only the final layer fuses the output norm")
        xc = _final(x2, ys, dest, gates_pad, final_norm_w.reshape(1, D))
    return xc.reshape(B, S, D)
```

```python
import functools

import jax
import jax.numpy as jnp
from jax import lax
from jax.experimental import pallas as pl
from jax.experimental.pallas import tpu as pltpu

F32 = jnp.float32
BF16 = jnp.bfloat16

EPS = 1e-6
HG_HEADS = 8
HG_DK = 128
ML_HEADS = 4
ML_DQK = 128
ML_DV = 256
CONV_W = 4
N_EXPERTS = 32
TOP_K = 4
SWIGLU_ALPHA = 1.702
SWIGLU_LIMIT = 7.0

LANES = 128
SUBLANES = 8
VMEM_LIMIT_BYTES = 56 * 1024 * 1024

HG_CHUNK = 64
HG_SUB = 16
ML_CHUNK = 128
HG_ROWS = 1024
ML_ROWS = 512
HG_HEADS_PER_STEP = 2

INPROJ_TM = 1024
INPROJ_TN = 1024
OUTPROJ_TM = 256
OUTPROJ_SUB = 128
DISPATCH_TOKENS = 256
MOE_TM = 512
MOE_TF = 1024
MOE_SUB_ROWS = 128
DMA_UNROLL = 32
DMA_QUEUES = 2
FINAL_TM = 128


def _dot_nt(a, b):
    return lax.dot_general(a, b, (((1,), (1,)), ((), ())), preferred_element_type=F32)


def _dot_tn(a, b):
    return lax.dot_general(a, b, (((0,), (0,)), ((), ())), preferred_element_type=F32)


def _log_sigmoid(z):
    return jnp.minimum(z, 0.0) - jnp.log(1.0 + jnp.exp(-jnp.abs(z)))


def _sigmoid(z):
    return 0.5 * jnp.tanh(0.5 * z) + 0.5


def _cparams(semantics):
    return pltpu.CompilerParams(dimension_semantics=semantics, vmem_limit_bytes=VMEM_LIMIT_BYTES)


_HI_MASK = 0xFFFF0000


def _packed_rows(d):
    return d // (2 * LANES)


def _pack_store(o_ref, v, rounded=False):
    n, d = v.shape
    s, half = _packed_rows(d), d // 2
    bits = pltpu.bitcast(v if rounded else v.astype(BF16).astype(F32), jnp.uint32)
    for c in range(s):
        hi = bits[:, c * LANES:(c + 1) * LANES]
        lo = bits[:, half + c * LANES:half + (c + 1) * LANES]
        o_ref[pl.ds(c, n, stride=s), :] = hi | jnp.right_shift(lo, jnp.uint32(16))


def _unpack_load(buf, first_row, n, s, c):
    w = buf[pl.ds(first_row * s + c, n, stride=s), :]
    hi = pltpu.bitcast(w & jnp.uint32(_HI_MASK), F32)
    lo = pltpu.bitcast(jnp.left_shift(w, jnp.uint32(16)), F32)
    return hi, lo


def _inproj_body(*refs, tn, with_gates):
    if with_gates:
        x_ref, lnw_ref, w_ref, wg_ref, o_ref, g_ref, h_scr = refs
    else:
        x_ref, lnw_ref, w_ref, o_ref, h_scr = refs

    @pl.when(pl.program_id(1) == 0)
    def _():
        x = x_ref[...]
        h = x * lax.rsqrt(jnp.mean(x * x, axis=-1, keepdims=True) + EPS) * lnw_ref[...]
        hb = h.astype(BF16)
        h_scr[...] = hb
        if with_gates:
            g_ref[...] = jnp.dot(hb, wg_ref[...], preferred_element_type=F32)

    res = jnp.dot(h_scr[...], w_ref[...], preferred_element_type=F32)
    for c in range(tn // LANES):
        o_ref[c] = res[:, c * LANES:(c + 1) * LANES].astype(o_ref.dtype)


def _inproj(x2d, ln_w, w_bf, n_tiles, col_tile, out_dtype, w_gates_pad=None):
    T, D = x2d.shape
    tm = min(INPROJ_TM, T)
    tn = INPROJ_TN
    with_gates = w_gates_pad is not None
    in_specs = [
        pl.BlockSpec((tm, D), lambda i, j: (i, 0)),
        pl.BlockSpec((1, D), lambda i, j: (0, 0)),
        pl.BlockSpec((D, tn), lambda i, j: (0, col_tile(j))),
    ]
    out_specs = [pl.BlockSpec((tn // LANES, tm, LANES), lambda i, j: (j, i, 0))]
    out_shape = [jax.ShapeDtypeStruct((n_tiles * tn // LANES, T, LANES), out_dtype)]
    args = [x2d, ln_w, w_bf]
    if with_gates:
        in_specs.append(pl.BlockSpec((D, LANES), lambda i, j: (0, 0)))
        out_specs.append(pl.BlockSpec((tm, LANES), lambda i, j: (i, 0)))
        out_shape.append(jax.ShapeDtypeStruct((T, LANES), F32))
        args.append(w_gates_pad)
    return pl.pallas_call(
        functools.partial(_inproj_body, tn=tn, with_gates=with_gates),
        grid=(T // tm, n_tiles),
        in_specs=in_specs,
        out_specs=out_specs,
        out_shape=out_shape,
        scratch_shapes=[pltpu.VMEM((tm, D), BF16)],
        compiler_params=_cparams(("parallel", "arbitrary")),
        name="inproj_gates" if with_gates else "inproj",
    )(*args)


def _hgrn_body(q_ref, f_ref, i_ref, g_ref, lb_ref, nw_ref, o_ref, st_scr, *, rows, heads):
    C, SUB = HG_CHUNK, HG_SUB
    nsub = C // SUB

    @pl.when(pl.program_id(2) == 0)
    def _():
        st_scr[...] = jnp.zeros_like(st_scr)

    lbs = [lb_ref[hh] for hh in range(heads)]
    log_lbs = [jnp.log(lb) for lb in lbs]
    log_1mlbs = [jnp.log1p(-lb) for lb in lbs]

    r_i = lax.broadcasted_iota(jnp.int32, (C, C), 0)
    c_i = lax.broadcasted_iota(jnp.int32, (C, C), 1)
    tri = (c_i <= r_i).astype(F32)
    sub_shift = SUB.bit_length() - 1
    diag_mask = (c_i <= r_i) & (jnp.right_shift(r_i, sub_shift) == jnp.right_shift(c_i, sub_shift))
    row_id = lax.broadcasted_iota(jnp.int32, (C, HG_DK), 0)

    pairs = [(j, hh) for j in range(rows // C) for hh in range(heads)]
    tri_b = tri.astype(BF16)
    st = {p: {} for p in pairs}

    for p in pairs:
        j, hh = p
        z = f_ref[hh, pl.ds(j * C, C), :].astype(F32)
        e = jnp.exp(-jnp.abs(z))
        e1 = 1.0 + e
        log_sig = jnp.minimum(z, 0.0) - jnp.log(e1)
        sig_neg = jnp.where(z >= 0, e, 1.0) / e1
        cc = log_1mlbs[hh] + log_sig
        log_f = jnp.maximum(log_lbs[hh], cc) + jnp.log(1.0 + jnp.exp(-jnp.abs(log_lbs[hh] - cc)))
        st[p]["kk"] = (1.0 - lbs[hh]) * sig_neg
        hi = log_f.astype(BF16)
        lo = (log_f - hi.astype(F32)).astype(BF16)
        st[p]["b"] = (jnp.dot(tri_b, hi, preferred_element_type=F32)
                      + jnp.dot(tri_b, lo, preferred_element_type=F32))

    for p in pairs:
        j, hh = p
        q = q_ref[hh, pl.ds(j * C, C), :].astype(F32)
        kk, b = st[p]["kk"], st[p]["b"]
        b_last = b[C - 1:C, :]
        st[p]["dec"] = jnp.exp(b_last)
        st[p]["qe"] = (q * jnp.exp(b)).astype(BF16)
        st[p]["kdec"] = (kk * jnp.exp(b_last - b)).astype(BF16)
        refs = [b[I * SUB:I * SUB + 1, :] for I in range(nsub)]
        refb = jnp.concatenate([jnp.broadcast_to(r, (SUB, HG_DK)) for r in refs], axis=0)
        qd = (q * jnp.exp(b - refb)).astype(BF16)
        kd = (kk * jnp.exp(refb - b)).astype(BF16)
        q_parts, k_parts = [], []
        for J in range(nsub - 1):
            r = refs[J + 1]
            qj = q * jnp.exp(jnp.minimum(b - r, 0.0))
            kj = kk * jnp.exp(jnp.minimum(r - b, 0.0))
            q_parts.append(jnp.where(row_id >= (J + 1) * SUB, qj, 0.0).astype(BF16))
            k_parts.append(jnp.where((row_id >= J * SUB) & (row_id < (J + 1) * SUB), kj, 0.0).astype(BF16))
        st[p]["att_d"] = _dot_nt(qd, kd)
        st[p]["att_o"] = _dot_nt(jnp.concatenate(q_parts, axis=1), jnp.concatenate(k_parts, axis=1))

    for p in pairs:
        j, hh = p
        vb = i_ref[hh, pl.ds(j * C, C), :].astype(BF16)
        att = jnp.where(diag_mask, st[p]["att_d"], 0.0) + st[p]["att_o"]
        st[p]["intra"] = jnp.dot(att.astype(BF16), vb, preferred_element_type=F32)
        st[p]["upd"] = _dot_tn(vb, st[p]["kdec"])

    for hh in range(heads):
        s = st_scr[hh]
        for j in range(rows // C):
            st[(j, hh)]["s_in"] = s.astype(BF16)
            s = s * st[(j, hh)]["dec"] + st[(j, hh)]["upd"]
        st_scr[hh] = s

    for p in pairs:
        j, hh = p
        g = g_ref[hh, pl.ds(j * C, C), :].astype(F32)
        o = _dot_nt(st[p]["qe"], st[p]["s_in"]) + st[p]["intra"]
        y = o * lax.rsqrt(jnp.mean(o * o, axis=-1, keepdims=True) + EPS) * nw_ref[hh]
        o_ref[pl.ds(j * C, C), hh * LANES:(hh + 1) * LANES] = (y * (g * _sigmoid(g))).astype(o_ref.dtype)


def _hgrn(proj_b, proj_f, lb, norm_w, batch, seq):
    T = batch * seq
    rows = min(HG_ROWS, seq)
    nblk = seq // rows
    H = HG_HEADS
    hp = HG_HEADS_PER_STEP

    def slab(off):
        return pl.BlockSpec((hp, rows, LANES), lambda b, h, c, off=off: (off // hp + h, b * nblk + c, 0))

    vec = pl.BlockSpec((hp, 1, LANES), lambda b, h, c: (h, 0, 0))
    return pl.pallas_call(
        functools.partial(_hgrn_body, rows=rows, heads=hp),
        grid=(batch, H // hp, nblk),
        in_specs=[slab(0), slab(0), slab(H), slab(2 * H), vec, vec],
        out_specs=pl.BlockSpec((rows, hp * LANES), lambda b, h, c: (b * nblk + c, h)),
        out_shape=jax.ShapeDtypeStruct((T, H * LANES), BF16),
        scratch_shapes=[pltpu.VMEM((hp, LANES, HG_DK), F32)],
        compiler_params=_cparams(("parallel", "parallel", "arbitrary")),
        name="hgrn2",
    )(proj_b, proj_f, proj_b, proj_b, lb.reshape(H, 1, HG_DK), norm_w.reshape(H, 1, LANES))


def _mlstm_body(q_ref, k_ref, v_ref, og_ref, gt_ref, gb_ref, cwq_ref, cwk_ref, cbq_ref, cbk_ref, nw_ref,
                out_ref, c_scr, n_scr, m_scr, qx_scr, kx_scr, qc_scr, kc_scr, *, rows):
    C = ML_CHUNK
    PAD = SUBLANES

    @pl.when(pl.program_id(2) == 0)
    def _():
        c_scr[...] = jnp.zeros_like(c_scr)
        n_scr[...] = jnp.zeros_like(n_scr)
        m_scr[...] = jnp.zeros_like(m_scr)
        qx_scr[0:PAD, :] = jnp.zeros((PAD, LANES), F32)
        kx_scr[0:PAD, :] = jnp.zeros((PAD, LANES), F32)

    qx_scr[PAD:PAD + rows, :] = q_ref[0].astype(F32)
    kx_scr[PAD:PAD + rows, :] = k_ref[0].astype(F32)
    accq = jnp.zeros((rows, LANES), F32) + cbq_ref[...]
    acck = jnp.zeros((rows, LANES), F32) + cbk_ref[...]
    for j in range(CONV_W):
        off = PAD - (CONV_W - 1) + j
        accq = accq + cwq_ref[j:j + 1, :] * qx_scr[pl.ds(off, rows), :]
        acck = acck + cwk_ref[j:j + 1, :] * kx_scr[pl.ds(off, rows), :]
    qc_scr[...] = accq * _sigmoid(accq) * (ML_DQK ** -0.5)
    kc_scr[...] = acck * _sigmoid(acck)
    qx_scr[0:PAD, :] = qx_scr[rows:rows + PAD, :]
    kx_scr[0:PAD, :] = kx_scr[rows:rows + PAD, :]

    lane = lax.broadcasted_iota(jnp.int32, (C, C), 1)
    sub = lax.broadcasted_iota(jnp.int32, (C, C), 0)
    causal = lane <= sub
    lower = causal.astype(F32)
    upper = (sub <= lane).astype(F32)
    nw = nw_ref[...]
    gbias = gb_ref[0]

    chunks = range(rows // C)
    lower_b, upper_b = lower.astype(BF16), upper.astype(BF16)
    st = [{} for _ in chunks]

    def split(x):
        hi = x.astype(BF16)
        return hi, (x - hi.astype(F32)).astype(BF16)

    for j in chunks:
        gc = gt_ref[0, pl.ds(j * C, C), :] + gbias
        gct = gc.T
        li_col = gc[:, 0:1]
        li_row = gct[0:1, :]
        ch, cl = split(_log_sigmoid(gc[:, ML_HEADS:ML_HEADS + 1]))
        rh, rl = split(_log_sigmoid(gct[ML_HEADS:ML_HEADS + 1, :]))
        g_t = (jnp.dot(lower_b, jnp.broadcast_to(ch, (C, C)), preferred_element_type=F32)
               + jnp.dot(lower_b, jnp.broadcast_to(cl, (C, C)), preferred_element_type=F32))
        g_s = (jnp.dot(jnp.broadcast_to(rh, (C, C)), upper_b, preferred_element_type=F32)
               + jnp.dot(jnp.broadcast_to(rl, (C, C)), upper_b, preferred_element_type=F32))
        st[j].update(g_t=g_t, g_s=g_s, li_col=li_col, li_row=li_row)

    for j in chunks:
        c = st[j]
        g_col = c["g_t"][:, 0:1]
        g_last = c["g_t"][C - 1:C, 0:1]
        dmat = jnp.where(causal, c["g_t"] - c["g_s"] + c["li_row"], -jnp.inf)
        log_ws = g_last - g_col + c["li_col"]
        qf = qc_scr[pl.ds(j * C, C), :]
        kf = kc_scr[pl.ds(j * C, C), :]
        qb = qf.astype(BF16)
        c.update(g_col=g_col, g_last=g_last, dmat=dmat, dmax=jnp.max(dmat, axis=1, keepdims=True),
                 log_ws=log_ws, ws_max=jnp.max(log_ws, axis=0, keepdims=True), qf=qf, kf=kf, qb=qb,
                 qk=_dot_nt(qb, kf.astype(BF16)))

    m = m_scr[:, 0:1]
    for j in chunks:
        c = st[j]
        m_new = jnp.maximum(c["g_last"] + m, c["ws_max"])
        c.update(m_prev=m, m_new=m_new, decay=jnp.exp(c["g_last"] + m - m_new))
        m = m_new
    m_scr[...] = jnp.broadcast_to(m, m_scr.shape)

    for j in chunks:
        c = st[j]
        vb = jnp.concatenate([v_ref[0, pl.ds(j * C, C), :], v_ref[1, pl.ds(j * C, C), :]], axis=1).astype(BF16)
        kw = c["kf"] * jnp.exp(c["log_ws"] - c["m_new"])
        c.update(vb=vb, upd=_dot_tn(kw.astype(BF16), vb), ksum=jnp.sum(kw, axis=0, keepdims=True))

    cm = c_scr[...]
    nv = n_scr[...]
    for j in chunks:
        c = st[j]
        c.update(c_in=cm.astype(BF16), n_in=nv)
        cm = c["decay"] * cm + c["upd"]
        nv = c["decay"] * nv + c["ksum"]
    c_scr[...] = cm
    n_scr[...] = nv

    for j in chunks:
        c = st[j]
        a_inter = c["g_col"] + c["m_prev"]
        m_t = jnp.maximum(a_inter, c["dmax"])
        w_inter = jnp.exp(a_inter - m_t)
        sqk = c["qk"] * jnp.exp(c["dmat"] - m_t)
        num = (w_inter * jnp.dot(c["qb"], c["c_in"], preferred_element_type=F32)
               + jnp.dot(sqk.astype(BF16), c["vb"], preferred_element_type=F32))
        den = (w_inter * jnp.sum(c["qf"] * c["n_in"], axis=1, keepdims=True)
               + jnp.sum(sqk, axis=1, keepdims=True))
        hh = num * (1.0 / jnp.maximum(jnp.abs(den), jnp.exp(-m_t)))
        y = hh * lax.rsqrt(jnp.mean(hh * hh, axis=-1, keepdims=True) + EPS) * nw
        og = jnp.concatenate([og_ref[0, pl.ds(j * C, C), :], og_ref[1, pl.ds(j * C, C), :]], axis=1).astype(F32)
        out_ref[pl.ds(j * C, C), :] = (y * _sigmoid(og)).astype(out_ref.dtype)


def _mlstm(proj3, gates, gate_bias_pad, conv_w, conv_b, norm_w, batch, seq, q_off):
    T = batch * seq
    rows = min(ML_ROWS, seq)
    nblk = seq // rows
    H = ML_HEADS
    k_off = q_off + H
    v_off = k_off + H
    o_off = v_off + 2 * H

    def slab(off):
        return pl.BlockSpec((1, rows, LANES), lambda b, h, c, off=off: (off + h, b * nblk + c, 0))

    def slab2(off):
        return pl.BlockSpec((2, rows, LANES), lambda b, h, c, off=off: (off // 2 + h, b * nblk + c, 0))

    qk_w = H * ML_DQK
    return pl.pallas_call(
        functools.partial(_mlstm_body, rows=rows),
        grid=(batch, H, nblk),
        in_specs=[
            slab(q_off), slab(k_off), slab2(v_off), slab2(o_off),
            pl.BlockSpec((1, rows, LANES), lambda b, h, c: (h, b * nblk + c, 0)),
            pl.BlockSpec((1, 1, LANES), lambda b, h, c: (h, 0, 0)),
            pl.BlockSpec((CONV_W, LANES), lambda b, h, c: (0, h)),
            pl.BlockSpec((CONV_W, LANES), lambda b, h, c: (0, H + h)),
            pl.BlockSpec((1, LANES), lambda b, h, c: (0, h)),
            pl.BlockSpec((1, LANES), lambda b, h, c: (0, H + h)),
            pl.BlockSpec((1, ML_DV), lambda b, h, c: (0, h)),
        ],
        out_specs=pl.BlockSpec((rows, ML_DV), lambda b, h, c: (b * nblk + c, h)),
        out_shape=jax.ShapeDtypeStruct((T, H * ML_DV), BF16),
        scratch_shapes=[
            pltpu.VMEM((ML_DQK, ML_DV), F32),
            pltpu.VMEM((1, ML_DQK), F32),
            pltpu.VMEM((1, LANES), F32),
            pltpu.VMEM((rows + 2 * SUBLANES, LANES), F32),
            pltpu.VMEM((rows + 2 * SUBLANES, LANES), F32),
            pltpu.VMEM((rows, LANES), F32),
            pltpu.VMEM((rows, LANES), F32),
        ],
        compiler_params=_cparams(("parallel", "parallel", "arbitrary")),
        name="mlstm",
    )(proj3, proj3, proj3, proj3, gates, gate_bias_pad, conv_w, conv_w,
      conv_b.reshape(1, 2 * qk_w), conv_b.reshape(1, 2 * qk_w), norm_w.reshape(1, H * ML_DV))


def _outproj_body(a_ref, b_ref, x_ref, wo_ref, ln_ref, wrh_ref, wrl_ref, br_ref,
                  x2_ref, h2_ref, idx_ref, gate_ref, rank_ref, cnt_ref, cnt_scr, *, sub_rows):
    tm = x_ref.shape[0]
    ka = a_ref.shape[1]
    s = _packed_rows(x_ref.shape[1])

    @pl.when(pl.program_id(0) == 0)
    def _():
        cnt_scr[...] = jnp.zeros_like(cnt_scr)

    lane = lax.broadcasted_iota(jnp.int32, (sub_rows, LANES), 1).astype(F32)
    onehots = [[] for _ in range(TOP_K)]
    for r0 in range(0, tm, sub_rows):
        rows = pl.ds(r0, sub_rows)
        res = (jnp.dot(a_ref[rows, :], wo_ref[0:ka, :], preferred_element_type=F32)
               + jnp.dot(b_ref[rows, :], wo_ref[ka:, :], preferred_element_type=F32))
        x2 = x_ref[rows, :] + res
        x2_ref[rows, :] = x2
        h2 = x2 * lax.rsqrt(jnp.mean(x2 * x2, axis=-1, keepdims=True) + EPS) * ln_ref[...]
        h_hi = h2.astype(BF16)
        h_hi32 = h_hi.astype(F32)
        _pack_store(h2_ref.at[pl.ds(r0 * s, sub_rows * s), :], h_hi32, rounded=True)

        h_lo = (h2 - h_hi32).astype(BF16)
        logits = (jnp.dot(h_hi, wrh_ref[...], preferred_element_type=F32)
                  + jnp.dot(h_lo, wrh_ref[...], preferred_element_type=F32)
                  + jnp.dot(h_hi, wrl_ref[...], preferred_element_type=F32)) + br_ref[...]
        vals, idxs = [], []
        cur = logits
        for _ in range(TOP_K):
            m = jnp.max(cur, axis=1, keepdims=True)
            ix = jnp.min(jnp.where(cur == m, lane, float(LANES)), axis=1, keepdims=True)
            vals.append(m)
            idxs.append(ix)
            cur = jnp.where(lane == ix, -jnp.inf, cur)
        es = [jnp.exp(v - vals[0]) for v in vals]
        inv = 1.0 / (es[0] + es[1] + es[2] + es[3])
        gate = jnp.zeros(logits.shape, F32)
        idx = jnp.zeros(logits.shape, F32)
        for k in range(TOP_K):
            gate = jnp.where(lane == float(k), es[k] * inv, gate)
            idx = jnp.where(lane == float(k), idxs[k], idx)
            onehots[k].append((lane == idxs[k]).astype(F32))
        gate_ref[rows, :] = gate
        idx_ref[rows, :] = idx.astype(jnp.int32)

    oh_k = [jnp.concatenate(o, axis=0) for o in onehots]
    oh = oh_k[0] + oh_k[1] + oh_k[2] + oh_k[3]
    r_i = lax.broadcasted_iota(jnp.int32, (tm, tm), 0)
    c_i = lax.broadcasted_iota(jnp.int32, (tm, tm), 1)
    before = jnp.dot((c_i < r_i).astype(BF16), oh.astype(BF16), preferred_element_type=F32) + cnt_scr[...]
    lane_t = lax.broadcasted_iota(jnp.int32, (tm, LANES), 1)
    rank = jnp.zeros((tm, LANES), F32)
    for k in range(TOP_K):
        rank = jnp.where(lane_t == k, jnp.sum(oh_k[k] * before, axis=1, keepdims=True), rank)
    rank_ref[...] = rank.astype(jnp.int32)
    cnt = cnt_scr[...] + jnp.sum(oh, axis=0, keepdims=True)
    cnt_scr[...] = cnt
    cnt_ref[...] = cnt.astype(jnp.int32)


def _outproj(a_out, b_out, x2d, w_out_bf, ln_w, wr_hi, wr_lo, b_router_pad):
    T, D = x2d.shape
    tm = min(OUTPROJ_TM, T)
    ka, kb = a_out.shape[1], b_out.shape[1]
    s = _packed_rows(D)
    row = lambda w: pl.BlockSpec((tm, w), lambda i: (i, 0))
    full = lambda r, c: pl.BlockSpec((r, c), lambda i: (0, 0))
    return pl.pallas_call(
        functools.partial(_outproj_body, sub_rows=min(OUTPROJ_SUB, tm)),
        grid=(T // tm,),
        in_specs=[row(ka), row(kb), row(D), full(ka + kb, D), full(1, D), full(D, LANES), full(D, LANES),
                  full(1, LANES)],
        out_specs=[row(D), pl.BlockSpec((tm * s, LANES), lambda i: (i, 0)), row(LANES), row(LANES), row(LANES),
                   full(1, LANES)],
        out_shape=[
            jax.ShapeDtypeStruct((T, D), F32),
            jax.ShapeDtypeStruct((T * s, LANES), jnp.uint32),
            jax.ShapeDtypeStruct((T, LANES), jnp.int32),
            jax.ShapeDtypeStruct((T, LANES), F32),
            jax.ShapeDtypeStruct((T, LANES), jnp.int32),
            jax.ShapeDtypeStruct((1, LANES), jnp.int32),
        ],
        scratch_shapes=[pltpu.VMEM((1, LANES), F32)],
        compiler_params=_cparams(("arbitrary",)),
        name="outproj_router",
    )(a_out, b_out, x2d, w_out_bf, ln_w, wr_hi, wr_lo, b_router_pad)


def _start_row_gather(idx_at, src_ref, buf, sem, n, s):
    def start(pair, carry):
        for p in range(DMA_QUEUES):
            i = pair * DMA_QUEUES + p
            src_row = pl.multiple_of(idx_at(i) * s, s)
            dst_row = pl.multiple_of(i * s, s)
            pltpu.make_async_copy(src_ref.at[pl.ds(src_row, s), :], buf.at[pl.ds(dst_row, s), :],
                                  sem).start(priority=p)
        return carry

    lax.fori_loop(0, n // DMA_QUEUES, start, 0, unroll=DMA_UNROLL // DMA_QUEUES)


def _wait_row_gather(buf, sem):
    pltpu.make_async_copy(buf, buf, sem).wait()


def _dispatch_body(zero_ref, dest_ref, h_ref, xs_ref, zbuf, sem, zsem, *, ntok, s, tm, n_blocks):
    blk = tm * s

    def zero_copy(b):
        return pltpu.make_async_copy(zbuf, xs_ref.at[pl.ds(pl.multiple_of(b * blk, blk), blk), :], zsem)

    @pl.when(pl.program_id(0) == 0)
    def _():
        zbuf[...] = jnp.zeros_like(zbuf)

        def zstart(b, carry):
            @pl.when(zero_ref[b] == 1)
            def _():
                zero_copy(b).start()
            return carry

        def zwait(b, carry):
            @pl.when(zero_ref[b] == 1)
            def _():
                zero_copy(b).wait()
            return carry

        lax.fori_loop(0, n_blocks, zstart, 0)
        lax.fori_loop(0, n_blocks, zwait, 0)

    def start(t, carry):
        src = h_ref.at[pl.ds(pl.multiple_of(t * s, s), s), :]
        for k in range(TOP_K):
            dst_row = pl.multiple_of(dest_ref[0, 0, t * TOP_K + k] * s, s)
            pltpu.make_async_copy(src, xs_ref.at[pl.ds(dst_row, s), :], sem).start(priority=k % DMA_QUEUES)
        return carry

    lax.fori_loop(0, ntok, start, 0, unroll=DMA_UNROLL // TOP_K)
    for _ in range(TOP_K):
        pltpu.make_async_copy(h_ref, h_ref, sem).wait()


def _dispatch(dest, zero_blk, h_packed, d, n_slots, tm):
    s = _packed_rows(d)
    T = h_packed.shape[0] // s
    ntok = min(DISPATCH_TOKENS, T)
    n = ntok * TOP_K
    grid_spec = pltpu.PrefetchScalarGridSpec(
        num_scalar_prefetch=1,
        grid=(T // ntok,),
        in_specs=[
            pl.BlockSpec((1, 1, n), lambda i, z: (i, 0, 0), memory_space=pltpu.SMEM),
            pl.BlockSpec((ntok * s, LANES), lambda i, z: (i, 0)),
        ],
        out_specs=pl.BlockSpec(memory_space=pl.ANY),
        scratch_shapes=[pltpu.VMEM((tm * s, LANES), jnp.uint32), pltpu.SemaphoreType.DMA,
                        pltpu.SemaphoreType.DMA],
    )
    return pl.pallas_call(
        functools.partial(_dispatch_body, ntok=ntok, s=s, tm=tm, n_blocks=zero_blk.shape[0]),
        grid_spec=grid_spec,
        out_shape=jax.ShapeDtypeStruct((n_slots * s, LANES), jnp.uint32),
        compiler_params=_cparams(("arbitrary",)),
        name="moe_dispatch",
    )(zero_blk, dest.reshape(T // ntok, 1, n), h_packed)


def _staged_weights(sched_refs, hbm_refs, stage_refs, bf_refs, sem):
    e_ref, wt_ref, first_ref, ne_ref, nt_ref, more_ref = sched_refs
    w = pl.program_id(0)
    tn = stage_refs[0].shape[1]

    def copies(e, t):
        col = pl.multiple_of(t * tn, tn)
        return [pltpu.make_async_copy(h.at[e, :, pl.ds(col, tn)], st, sem.at[k])
                for k, (h, st) in enumerate(zip(hbm_refs, stage_refs))]

    def start_all(cs):
        for c in cs:
            c.start()

    @pl.when(w == 0)
    def _():
        start_all(copies(e_ref[0], wt_ref[0]))

    @pl.when(first_ref[w] == 1)
    def _():
        for c in copies(e_ref[w], wt_ref[w]):
            c.wait()
        for st, bf in zip(stage_refs, bf_refs):
            for r in range(0, st.shape[0], 256):
                bf[r:r + 256, :] = st[r:r + 256, :].astype(BF16)

        @pl.when(more_ref[w] == 1)
        def _():
            start_all(copies(ne_ref[w], nt_ref[w]))


def _row_groups(nrows, tm, compute, clear):
    sub = min(MOE_SUB_ROWS, tm)
    groups = tm // sub
    for live in range(groups + 1):
        @pl.when((nrows > (live - 1) * sub) & (nrows <= live * sub))
        def _(live=live):
            for g in range(live):
                compute(g * sub, sub)
            for g in range(live, groups):
                clear(g * sub, sub)


def _moe_up_body(e_ref, wt_ref, r_ref, ot_ref, first_ref, nrows_ref, ne_ref, nt_ref, more_ref,
                 x_ref, wg_ref, wu_ref, bg_ref, bu_ref, o_ref, wg_stage, wu_stage, wg_scr, wu_scr, x_scr, sem):
    w = pl.program_id(0)
    _staged_weights((e_ref, wt_ref, first_ref, ne_ref, nt_ref, more_ref), (wg_ref, wu_ref),
                    (wg_stage, wu_stage), (wg_scr, wu_scr), sem)
    tm, d = x_scr.shape
    s = _packed_rows(d)

    def compute(r0, n):
        rows = pl.ds(r0, n)
        for c in range(s):
            hi, lo = _unpack_load(x_ref, r0, n, s, c)
            x_scr[rows, c * LANES:(c + 1) * LANES] = hi.astype(BF16)
            x_scr[rows, d // 2 + c * LANES:d // 2 + (c + 1) * LANES] = lo.astype(BF16)
        x = x_scr[rows, :]
        gt = jnp.dot(x, wg_scr[...], preferred_element_type=F32) + bg_ref[0]
        up = jnp.dot(x, wu_scr[...], preferred_element_type=F32) + bu_ref[0]
        gt = jnp.minimum(gt, SWIGLU_LIMIT)
        up = jnp.clip(up, -SWIGLU_LIMIT, SWIGLU_LIMIT)
        swish = (0.5 * gt) * (1.0 + jnp.tanh((0.5 * SWIGLU_ALPHA) * gt))
        o_ref[rows, :] = ((up + 1.0) * swish).astype(o_ref.dtype)

    def clear(r0, n):
        o_ref[pl.ds(r0, n), :] = jnp.zeros((n, o_ref.shape[1]), o_ref.dtype)

    _row_groups(nrows_ref[w], tm, compute, clear)


def _moe_down_body(e_ref, wt_ref, r_ref, ot_ref, first_ref, nrows_ref, ne_ref, nt_ref, more_ref,
                   a_ref, wd_ref, bd_ref, o_ref, wd_stage, wd_scr, sem):
    w = pl.program_id(0)
    _staged_weights((e_ref, wt_ref, first_ref, ne_ref, nt_ref, more_ref), (wd_ref,), (wd_stage,), (wd_scr,), sem)
    tm = a_ref.shape[0]
    s = o_ref.shape[0] // tm

    def compute(r0, n):
        y = jnp.dot(a_ref[pl.ds(r0, n), :], wd_scr[...], preferred_element_type=F32) + bd_ref[0]
        _pack_store(o_ref.at[pl.ds(r0 * s, n * s), :], y)

    def clear(r0, n):
        o_ref[pl.ds(r0 * s, n * s), :] = jnp.zeros((n * s, LANES), o_ref.dtype)

    _row_groups(nrows_ref[w], tm, compute, clear)


def _moe_schedule(counts, tm, n_tiles, n_blocks):
    n_items = n_tiles * n_blocks
    experts = jnp.arange(N_EXPERTS, dtype=jnp.int32)
    blocks_e = (counts + tm - 1) // tm
    bend = jnp.cumsum(blocks_e)
    bstart = bend - blocks_e
    item_end = n_tiles * bend
    total = item_end[-1]
    later = (experts[None, :] > experts[:, None]) & (blocks_e[None, :] > 0)
    next_e = jnp.min(jnp.where(later, experts[None, :], N_EXPERTS - 1), axis=1)

    w = jnp.arange(n_items, dtype=jnp.int32)
    valid = w < total
    wc = jnp.minimum(w, jnp.maximum(total - 1, 0))
    e = jnp.minimum(jnp.sum((item_end[None, :] <= wc[:, None]).astype(jnp.int32), axis=1), N_EXPERTS - 1)
    sel = e[:, None] == experts[None, :]
    pick = lambda table: jnp.sum(jnp.where(sel, table[None, :], 0), axis=1)
    nb = jnp.maximum(pick(blocks_e), 1)
    local = wc - n_tiles * pick(bstart)
    wtile = sum((local >= t * nb).astype(jnp.int32) for t in range(1, n_tiles)) if n_tiles > 1 else 0 * local
    jblk = local - wtile * nb
    spare = jnp.maximum(w - total, 0)
    rblk = jnp.where(valid, pick(bstart) + jblk, bend[-1] + spare // n_tiles)
    otile = jnp.where(valid, wtile, spare % n_tiles)
    nrows = jnp.where(valid, jnp.clip(pick(counts) - jblk * tm, 0, tm), 0)
    first = (jblk == 0) & valid
    last_tile = wtile == n_tiles - 1
    more = first & (w + nb < total)
    i32 = lambda a: a.astype(jnp.int32)
    return (e, i32(wtile), i32(rblk), i32(otile), i32(first), i32(nrows),
            i32(jnp.where(last_tile, pick(next_e), e)), i32(jnp.where(last_tile, 0, wtile + 1)), i32(more))


def _moe_up(sched, xs_packed, w_gate, w_up, b_gate, b_up):
    D, d_ff = w_gate.shape[1], w_gate.shape[2]
    s = _packed_rows(D)
    n_slots = xs_packed.shape[0] // s
    tm, tf = MOE_TM, min(MOE_TF, d_ff)
    n_items = sched[0].shape[0]
    wspec = pl.BlockSpec(memory_space=pl.ANY)
    bspec = pl.BlockSpec((1, 1, tf), lambda w, e, wt, r, ot, *_: (e[w], 0, wt[w]))
    grid_spec = pltpu.PrefetchScalarGridSpec(
        num_scalar_prefetch=len(sched),
        grid=(n_items,),
        in_specs=[pl.BlockSpec((tm * s, LANES), lambda w, e, wt, r, ot, *_: (r[w], 0)),
                  wspec, wspec, bspec, bspec],
        out_specs=pl.BlockSpec((tm, tf), lambda w, e, wt, r, ot, *_: (r[w], ot[w])),
        scratch_shapes=[pltpu.VMEM((D, tf), F32), pltpu.VMEM((D, tf), F32),
                        pltpu.VMEM((D, tf), BF16), pltpu.VMEM((D, tf), BF16), pltpu.VMEM((tm, D), BF16),
                        pltpu.SemaphoreType.DMA((2,))],
    )
    return pl.pallas_call(
        _moe_up_body,
        grid_spec=grid_spec,
        out_shape=jax.ShapeDtypeStruct((n_slots, d_ff), BF16),
        compiler_params=_cparams(("arbitrary",)),
        name="moe_up",
    )(*sched, xs_packed, w_gate, w_up, b_gate.reshape(N_EXPERTS, 1, d_ff), b_up.reshape(N_EXPERTS, 1, d_ff))


def _moe_down(sched, act, w_down, b_down):
    n_slots, d_ff = act.shape
    D = w_down.shape[2]
    tm = MOE_TM
    s = _packed_rows(D)
    n_items = sched[0].shape[0]
    grid_spec = pltpu.PrefetchScalarGridSpec(
        num_scalar_prefetch=len(sched),
        grid=(n_items,),
        in_specs=[
            pl.BlockSpec((tm, d_ff), lambda w, e, wt, r, ot, *_: (r[w], 0)),
            pl.BlockSpec(memory_space=pl.ANY),
            pl.BlockSpec((1, 1, D), lambda w, e, wt, r, ot, *_: (e[w], 0, 0)),
        ],
        out_specs=pl.BlockSpec((tm * s, LANES), lambda w, e, wt, r, ot, *_: (r[w], 0)),
        scratch_shapes=[pltpu.VMEM((d_ff, D), F32), pltpu.VMEM((d_ff, D), BF16), pltpu.SemaphoreType.DMA((1,))],
    )
    return pl.pallas_call(
        _moe_down_body,
        grid_spec=grid_spec,
        out_shape=jax.ShapeDtypeStruct((n_slots * s, LANES), jnp.uint32),
        compiler_params=_cparams(("arbitrary",)),
        name="moe_down",
    )(*sched, act, w_down, b_down.reshape(N_EXPERTS, 1, D))


def _final_body(dest_ref, next_ref, x2_ref, gate_ref, w_ref, ys_ref, o_ref, buf0, buf1, sem0, sem1, *, tm, s):
    n = tm * TOP_K
    i = pl.program_id(0)

    def gather(idx_ref, tile, buf, sem):
        _start_row_gather(lambda r: idx_ref[0, 0, tile * n + r], ys_ref, buf, sem, n, s)

    def combine(tile, buf):
        rows = pl.ds(tile * tm, tm)
        gate = gate_ref[rows, :]
        half = o_ref.shape[1] // 2
        gates = [jnp.broadcast_to(gate[:, k:k + 1], (tm, LANES)) for k in range(TOP_K)]
        for c in range(s):
            acc_hi = x2_ref[rows, c * LANES:(c + 1) * LANES]
            acc_lo = x2_ref[rows, half + c * LANES:half + (c + 1) * LANES]
            for k in range(TOP_K):
                hi, lo = _unpack_load(buf, k * tm, tm, s, c)
                acc_hi = acc_hi + gates[k] * hi
                acc_lo = acc_lo + gates[k] * lo
            o_ref[rows, c * LANES:(c + 1) * LANES] = acc_hi
            o_ref[rows, half + c * LANES:half + (c + 1) * LANES] = acc_lo
        acc = o_ref[rows, :]
        o_ref[rows, :] = acc * lax.rsqrt(jnp.mean(acc * acc, axis=-1, keepdims=True) + EPS) * w_ref[...]

    @pl.when(i == 0)
    def _():
        gather(dest_ref, 0, buf0, sem0)

    gather(dest_ref, 1, buf1, sem1)
    _wait_row_gather(buf0, sem0)
    combine(0, buf0)
    gather(next_ref, 0, buf0, sem0)
    _wait_row_gather(buf1, sem1)
    combine(1, buf1)

    @pl.when(i == pl.num_programs(0) - 1)
    def _():
        _wait_row_gather(buf0, sem0)


def _final(x2, ys_packed, dest, gates_pad, w):
    T, D = x2.shape
    tm = min(FINAL_TM, T // 2)
    s = _packed_rows(D)
    n = tm * TOP_K
    steps = T // (2 * tm)
    dest_km = dest.reshape(steps, 2, tm, TOP_K).transpose(0, 1, 3, 2).reshape(steps, 1, 2 * n)
    buf = pltpu.VMEM((n * s, LANES), jnp.uint32)
    return pl.pallas_call(
        functools.partial(_final_body, tm=tm, s=s),
        grid=(steps,),
        in_specs=[
            pl.BlockSpec((1, 1, 2 * n), lambda i: (i, 0, 0), memory_space=pltpu.SMEM),
            pl.BlockSpec((1, 1, 2 * n), lambda i: (jnp.minimum(i + 1, steps - 1), 0, 0), memory_space=pltpu.SMEM),
            pl.BlockSpec((2 * tm, D), lambda i: (i, 0)),
            pl.BlockSpec((2 * tm, LANES), lambda i: (i, 0)),
            pl.BlockSpec((1, D), lambda i: (0, 0)),
            pl.BlockSpec(memory_space=pl.ANY),
        ],
        out_specs=pl.BlockSpec((2 * tm, D), lambda i: (i, 0)),
        out_shape=jax.ShapeDtypeStruct((T, D), F32),
        scratch_shapes=[buf, buf, pltpu.SemaphoreType.DMA, pltpu.SemaphoreType.DMA],
        compiler_params=_cparams(("arbitrary",)),
        name="final_norm",
    )(dest_km, dest_km, x2, gates_pad, w, ys_packed)


def _moe(h2_packed, T, D, top_idx, rank, counts, w_gate, b_gate, w_up, b_up, w_down, b_down):
    A = T * TOP_K
    tm = MOE_TM
    n_blocks = (A + N_EXPERTS * (tm - 1) + tm - 1) // tm
    n_slots = n_blocks * tm

    blocks_e = (counts + tm - 1) // tm
    bend = jnp.cumsum(blocks_e)
    bstart = bend - blocks_e
    experts = jnp.arange(N_EXPERTS, dtype=jnp.int32)
    first_slot = jnp.sum(jnp.where(top_idx[:, :, None] == experts, bstart * tm, 0), axis=-1)
    dest = (first_slot + rank).astype(jnp.int32).reshape(A)
    blk = jnp.arange(n_blocks, dtype=jnp.int32)
    is_last = jnp.any((blk[:, None] == bend[None, :] - 1) & (blocks_e[None, :] > 0), axis=1)
    zero_blk = ((blk >= bend[-1]) | is_last).astype(jnp.int32)

    xs = _dispatch(dest, zero_blk, h2_packed, D, n_slots, tm)
    d_ff = w_gate.shape[2]
    act = _moe_up(_moe_schedule(counts, tm, -(-d_ff // MOE_TF), n_blocks), xs, w_gate, w_up, b_gate, b_up)
    ys = _moe_down(_moe_schedule(counts, tm, 1, n_blocks), act, w_down, b_down)
    return ys, dest


def kernel(x, ln1_w, w_in, hg_lb_logits, hg_norm_w, ml_conv_w, ml_conv_b, ml_igate_b, ml_fgate_b, ml_norm_w,
           w_out, ln2_w, w_router, b_router, w_gate, b_gate, w_up, b_up, w_down, b_down, final_norm_w):
    B, S, D = x.shape
    T = B * S
    depth = w_in.shape[0]
    hg_w = HG_HEADS * HG_DK
    n_main = 4 * hg_w + 2 * ML_HEADS * ML_DQK + 2 * ML_HEADS * ML_DV
    lb_all = jnp.cumsum(jax.nn.softmax(hg_lb_logits.astype(F32), axis=0), axis=0)

    xc = x.reshape(T, D)
    for l in range(depth):
        w_gates_pad = jnp.pad(w_in[l][:, n_main:], ((0, 0), (0, LANES - 2 * ML_HEADS))).astype(BF16)
        w_bf = w_in[l].astype(BF16)
        ln1 = ln1_w[l].reshape(1, D)
        hf_tiles = hg_w // INPROJ_TN
        proj_f = _inproj(xc, ln1, w_bf, hf_tiles, lambda j: j + hf_tiles, F32)[0]
        proj_b, gates = _inproj(xc, ln1, w_bf, n_main // INPROJ_TN - hf_tiles,
                                lambda j: jnp.where(j >= hf_tiles, j + hf_tiles, j), BF16, w_gates_pad)
        a_out = _hgrn(proj_b, proj_f, lb_all[l], hg_norm_w[l], B, S)
        gate_bias = jnp.pad(jnp.concatenate([ml_igate_b[l], ml_fgate_b[l]]), (0, LANES - 2 * ML_HEADS))
        gates_h = jnp.stack([jnp.roll(gates, -hd, axis=1) for hd in range(ML_HEADS)])
        bias_h = jnp.stack([jnp.roll(gate_bias, -hd) for hd in range(ML_HEADS)]).reshape(ML_HEADS, 1, LANES)
        b_out = _mlstm(proj_b, gates_h, bias_h, ml_conv_w[l], ml_conv_b[l], ml_norm_w[l], B, S, 3 * HG_HEADS)
        wr_pad = jnp.pad(w_router[l], ((0, 0), (0, LANES - N_EXPERTS)))
        wr_hi = wr_pad.astype(BF16)
        wr_lo = (wr_pad - wr_hi.astype(F32)).astype(BF16)
        br_pad = jnp.pad(b_router[l], (0, LANES - N_EXPERTS), constant_values=-1e30).reshape(1, LANES)
        x2, h2, idx_pad, gates_pad, rank_pad, cnt = _outproj(
            a_out, b_out, xc, w_out[l].astype(BF16), ln2_w[l].reshape(1, D), wr_hi, wr_lo, br_pad)
        ys, dest = _moe(h2, T, D, idx_pad[:, :TOP_K], rank_pad[:, :TOP_K], cnt[0, :N_EXPERTS],
                        w_gate[l], b_gate[l], w_up[l], b_up[l], w_down[l], b_down[l])
        if l + 1 < depth:
            raise NotImplementedError("only the final layer fuses the output norm")
        xc = _final(x2, ys, dest, gates_pad, final_norm_w.reshape(1, D))
    return xc.reshape(B, S, D)
```

```python
import functools

import jax
import jax.numpy as jnp
from jax import lax
from jax.experimental import pallas as pl
from jax.experimental.pallas import tpu as pltpu

F32 = jnp.float32
BF16 = jnp.bfloat16

EPS = 1e-6
HG_HEADS = 8
HG_DK = 128
ML_HEADS = 4
ML_DQK = 128
ML_DV = 256
CONV_W = 4
N_EXPERTS = 32
TOP_K = 4
SWIGLU_ALPHA = 1.702
SWIGLU_LIMIT = 7.0

LANES = 128
SUBLANES = 8
VMEM_LIMIT_BYTES = 56 * 1024 * 1024

HG_CHUNK = 64
HG_SUB = 16
ML_CHUNK = 128
HG_ROWS = 1024
ML_ROWS = 512
HG_HEADS_PER_STEP = 2

INPROJ_TM = 1024
INPROJ_TN = 1024
OUTPROJ_TM = 256
OUTPROJ_SUB = 128
DISPATCH_TOKENS = 256
MOE_TM = 512
MOE_TF = 1024
MOE_SUB_ROWS = 128
DMA_UNROLL = 32
DMA_QUEUES = 2
FINAL_TM = 128


def _dot_nt(a, b):
    return lax.dot_general(a, b, (((1,), (1,)), ((), ())), preferred_element_type=F32)


def _dot_tn(a, b):
    return lax.dot_general(a, b, (((0,), (0,)), ((), ())), preferred_element_type=F32)


def _log_sigmoid(z):
    return jnp.minimum(z, 0.0) - jnp.log(1.0 + jnp.exp(-jnp.abs(z)))


def _sigmoid(z):
    return 0.5 * jnp.tanh(0.5 * z) + 0.5


def _cparams(semantics):
    return pltpu.CompilerParams(dimension_semantics=semantics, vmem_limit_bytes=VMEM_LIMIT_BYTES)


_HI_MASK = 0xFFFF0000


def _packed_rows(d):
    return d // (2 * LANES)


def _pack_store(o_ref, v, rounded=False):
    n, d = v.shape
    s, half = _packed_rows(d), d // 2
    bits = pltpu.bitcast(v if rounded else v.astype(BF16).astype(F32), jnp.uint32)
    for c in range(s):
        hi = bits[:, c * LANES:(c + 1) * LANES]
        lo = bits[:, half + c * LANES:half + (c + 1) * LANES]
        o_ref[pl.ds(c, n, stride=s), :] = hi | jnp.right_shift(lo, jnp.uint32(16))


def _unpack_load(buf, first_row, n, s, c):
    w = buf[pl.ds(first_row * s + c, n, stride=s), :]
    hi = pltpu.bitcast(w & jnp.uint32(_HI_MASK), F32)
    lo = pltpu.bitcast(jnp.left_shift(w, jnp.uint32(16)), F32)
    return hi, lo


def _inproj_body(*refs, tn, with_gates):
    if with_gates:
        x_ref, lnw_ref, w_ref, wg_ref, o_ref, g_ref, h_scr = refs
    else:
        x_ref, lnw_ref, w_ref, o_ref, h_scr = refs

    @pl.when(pl.program_id(1) == 0)
    def _():
        x = x_ref[...]
        h = x * lax.rsqrt(jnp.mean(x * x, axis=-1, keepdims=True) + EPS) * lnw_ref[...]
        hb = h.astype(BF16)
        h_scr[...] = hb
        if with_gates:
            g_ref[...] = jnp.dot(hb, wg_ref[...], preferred_element_type=F32)

    res = jnp.dot(h_scr[...], w_ref[...], preferred_element_type=F32)
    for c in range(tn // LANES):
        o_ref[c] = res[:, c * LANES:(c + 1) * LANES].astype(o_ref.dtype)


def _inproj(x2d, ln_w, w_bf, n_tiles, col_tile, out_dtype, w_gates_pad=None):
    T, D = x2d.shape
    tm = min(INPROJ_TM, T)
    tn = INPROJ_TN
    with_gates = w_gates_pad is not None
    in_specs = [
        pl.BlockSpec((tm, D), lambda i, j: (i, 0)),
        pl.BlockSpec((1, D), lambda i, j: (0, 0)),
        pl.BlockSpec((D, tn), lambda i, j: (0, col_tile(j))),
    ]
    out_specs = [pl.BlockSpec((tn // LANES, tm, LANES), lambda i, j: (j, i, 0))]
    out_shape = [jax.ShapeDtypeStruct((n_tiles * tn // LANES, T, LANES), out_dtype)]
    args = [x2d, ln_w, w_bf]
    if with_gates:
        in_specs.append(pl.BlockSpec((D, LANES), lambda i, j: (0, 0)))
        out_specs.append(pl.BlockSpec((tm, LANES), lambda i, j: (i, 0)))
        out_shape.append(jax.ShapeDtypeStruct((T, LANES), F32))
        args.append(w_gates_pad)
    return pl.pallas_call(
        functools.partial(_inproj_body, tn=tn, with_gates=with_gates),
        grid=(T // tm, n_tiles),
        in_specs=in_specs,
        out_specs=out_specs,
        out_shape=out_shape,
        scratch_shapes=[pltpu.VMEM((tm, D), BF16)],
        compiler_params=_cparams(("parallel", "arbitrary")),
        name="inproj_gates" if with_gates else "inproj",
    )(*args)


def _hgrn_body(q_ref, f_ref, i_ref, g_ref, lb_ref, nw_ref, o_ref, st_scr, *, rows, heads):
    C, SUB = HG_CHUNK, HG_SUB
    nsub = C // SUB

    @pl.when(pl.program_id(2) == 0)
    def _():
        st_scr[...] = jnp.zeros_like(st_scr)

    lbs = [lb_ref[hh] for hh in range(heads)]
    log_lbs = [jnp.log(lb) for lb in lbs]
    log_1mlbs = [jnp.log1p(-lb) for lb in lbs]

    r_i = lax.broadcasted_iota(jnp.int32, (C, C), 0)
    c_i = lax.broadcasted_iota(jnp.int32, (C, C), 1)
    tri = (c_i <= r_i).astype(F32)
    sub_shift = SUB.bit_length() - 1
    diag_mask = (c_i <= r_i) & (jnp.right_shift(r_i, sub_shift) == jnp.right_shift(c_i, sub_shift))
    row_id = lax.broadcasted_iota(jnp.int32, (C, HG_DK), 0)

    pairs = [(j, hh) for j in range(rows // C) for hh in range(heads)]
    tri_b = tri.astype(BF16)
    st = {p: {} for p in pairs}

    for p in pairs:
        j, hh = p
        z = f_ref[hh, pl.ds(j * C, C), :].astype(F32)
        e = jnp.exp(-jnp.abs(z))
        e1 = 1.0 + e
        log_sig = jnp.minimum(z, 0.0) - jnp.log(e1)
        sig_neg = jnp.where(z >= 0, e, 1.0) / e1
        cc = log_1mlbs[hh] + log_sig
        log_f = jnp.maximum(log_lbs[hh], cc) + jnp.log(1.0 + jnp.exp(-jnp.abs(log_lbs[hh] - cc)))
        st[p]["kk"] = (1.0 - lbs[hh]) * sig_neg
        hi = log_f.astype(BF16)
        lo = (log_f - hi.astype(F32)).astype(BF16)
        st[p]["b"] = (jnp.dot(tri_b, hi, preferred_element_type=F32)
                      + jnp.dot(tri_b, lo, preferred_element_type=F32))

    for p in pairs:
        j, hh = p
        q = q_ref[hh, pl.ds(j * C, C), :].astype(F32)
        kk, b = st[p]["kk"], st[p]["b"]
        b_last = b[C - 1:C, :]
        st[p]["dec"] = jnp.exp(b_last)
        st[p]["qe"] = (q * jnp.exp(b)).astype(BF16)
        st[p]["kdec"] = (kk * jnp.exp(b_last - b)).astype(BF16)
        refs = [b[I * SUB:I * SUB + 1, :] for I in range(nsub)]
        refb = jnp.concatenate([jnp.broadcast_to(r, (SUB, HG_DK)) for r in refs], axis=0)
        qd = (q * jnp.exp(b - refb)).astype(BF16)
        kd = (kk * jnp.exp(refb - b)).astype(BF16)
        q_parts, k_parts = [], []
        for J in range(nsub - 1):
            r = refs[J + 1]
            qj = q * jnp.exp(jnp.minimum(b - r, 0.0))
            kj = kk * jnp.exp(jnp.minimum(r - b, 0.0))
            q_parts.append(jnp.where(row_id >= (J + 1) * SUB, qj, 0.0).astype(BF16))
            k_parts.append(jnp.where((row_id >= J * SUB) & (row_id < (J + 1) * SUB), kj, 0.0).astype(BF16))
        st[p]["att_d"] = _dot_nt(qd, kd)
        st[p]["att_o"] = _dot_nt(jnp.concatenate(q_parts, axis=1), jnp.concatenate(k_parts, axis=1))

    for p in pairs:
        j, hh = p
        vb = i_ref[hh, pl.ds(j * C, C), :].astype(BF16)
        att = jnp.where(diag_mask, st[p]["att_d"], 0.0) + st[p]["att_o"]
        st[p]["intra"] = jnp.dot(att.astype(BF16), vb, preferred_element_type=F32)
        st[p]["upd"] = _dot_tn(vb, st[p]["kdec"])

    for hh in range(heads):
        s = st_scr[hh]
        for j in range(rows // C):
            st[(j, hh)]["s_in"] = s.astype(BF16)
            s = s * st[(j, hh)]["dec"] + st[(j, hh)]["upd"]
        st_scr[hh] = s

    for p in pairs:
        j, hh = p
        g = g_ref[hh, pl.ds(j * C, C), :].astype(F32)
        o = _dot_nt(st[p]["qe"], st[p]["s_in"]) + st[p]["intra"]
        y = o * lax.rsqrt(jnp.mean(o * o, axis=-1, keepdims=True) + EPS) * nw_ref[hh]
        o_ref[pl.ds(j * C, C), hh * LANES:(hh + 1) * LANES] = (y * (g * _sigmoid(g))).astype(o_ref.dtype)


def _hgrn(proj_b, proj_f, lb, norm_w, batch, seq):
    T = batch * seq
    rows = min(HG_ROWS, seq)
    nblk = seq // rows
    H = HG_HEADS
    hp = HG_HEADS_PER_STEP

    def slab(off):
        return pl.BlockSpec((hp, rows, LANES), lambda b, h, c, off=off: (off // hp + h, b * nblk + c, 0))

    vec = pl.BlockSpec((hp, 1, LANES), lambda b, h, c: (h, 0, 0))
    return pl.pallas_call(
        functools.partial(_hgrn_body, rows=rows, heads=hp),
        grid=(batch, H // hp, nblk),
        in_specs=[slab(0), slab(0), slab(H), slab(2 * H), vec, vec],
        out_specs=pl.BlockSpec((rows, hp * LANES), lambda b, h, c: (b * nblk + c, h)),
        out_shape=jax.ShapeDtypeStruct((T, H * LANES), BF16),
        scratch_shapes=[pltpu.VMEM((hp, LANES, HG_DK), F32)],
        compiler_params=_cparams(("parallel", "parallel", "arbitrary")),
        name="hgrn2",
    )(proj_b, proj_f, proj_b, proj_b, lb.reshape(H, 1, HG_DK), norm_w.reshape(H, 1, LANES))


def _mlstm_body(q_ref, k_ref, v_ref, og_ref, gt_ref, gb_ref, cwq_ref, cwk_ref, cbq_ref, cbk_ref, nw_ref,
                out_ref, c_scr, n_scr, m_scr, qx_scr, kx_scr, qc_scr, kc_scr, *, rows):
    C = ML_CHUNK
    PAD = SUBLANES

    @pl.when(pl.program_id(2) == 0)
    def _():
        c_scr[...] = jnp.zeros_like(c_scr)
        n_scr[...] = jnp.zeros_like(n_scr)
        m_scr[...] = jnp.zeros_like(m_scr)
        qx_scr[0:PAD, :] = jnp.zeros((PAD, LANES), F32)
        kx_scr[0:PAD, :] = jnp.zeros((PAD, LANES), F32)

    qx_scr[PAD:PAD + rows, :] = q_ref[0].astype(F32)
    kx_scr[PAD:PAD + rows, :] = k_ref[0].astype(F32)
    accq = jnp.zeros((rows, LANES), F32) + cbq_ref[...]
    acck = jnp.zeros((rows, LANES), F32) + cbk_ref[...]
    for j in range(CONV_W):
        off = PAD - (CONV_W - 1) + j
        accq = accq + cwq_ref[j:j + 1, :] * qx_scr[pl.ds(off, rows), :]
        acck = acck + cwk_ref[j:j + 1, :] * kx_scr[pl.ds(off, rows), :]
    qc_scr[...] = accq * _sigmoid(accq) * (ML_DQK ** -0.5)
    kc_scr[...] = acck * _sigmoid(acck)
    qx_scr[0:PAD, :] = qx_scr[rows:rows + PAD, :]
    kx_scr[0:PAD, :] = kx_scr[rows:rows + PAD, :]

    lane = lax.broadcasted_iota(jnp.int32, (C, C), 1)
    sub = lax.broadcasted_iota(jnp.int32, (C, C), 0)
    causal = lane <= sub
    lower = causal.astype(F32)
    upper = (sub <= lane).astype(F32)
    nw = nw_ref[...]
    gbias = gb_ref[0]

    chunks = range(rows // C)
    lower_b, upper_b = lower.astype(BF16), upper.astype(BF16)
    st = [{} for _ in chunks]

    def split(x):
        hi = x.astype(BF16)
        return hi, (x - hi.astype(F32)).astype(BF16)

    for j in chunks:
        gc = gt_ref[0, pl.ds(j * C, C), :] + gbias
        gct = gc.T
        li_col = gc[:, 0:1]
        li_row = gct[0:1, :]
        ch, cl = split(_log_sigmoid(gc[:, ML_HEADS:ML_HEADS + 1]))
        rh, rl = split(_log_sigmoid(gct[ML_HEADS:ML_HEADS + 1, :]))
        g_t = (jnp.dot(lower_b, jnp.broadcast_to(ch, (C, C)), preferred_element_type=F32)
               + jnp.dot(lower_b, jnp.broadcast_to(cl, (C, C)), preferred_element_type=F32))
        g_s = (jnp.dot(jnp.broadcast_to(rh, (C, C)), upper_b, preferred_element_type=F32)
               + jnp.dot(jnp.broadcast_to(rl, (C, C)), upper_b, preferred_element_type=F32))
        st[j].update(g_t=g_t, g_s=g_s, li_col=li_col, li_row=li_row)

    for j in chunks:
        c = st[j]
        g_col = c["g_t"][:, 0:1]
        g_last = c["g_t"][C - 1:C, 0:1]
        dmat = jnp.where(causal, c["g_t"] - c["g_s"] + c["li_row"], -jnp.inf)
        log_ws = g_last - g_col + c["li_col"]
        qf = qc_scr[pl.ds(j * C, C), :]
        kf = kc_scr[pl.ds(j * C, C), :]
        qb = qf.astype(BF16)
        c.update(g_col=g_col, g_last=g_last, dmat=dmat, dmax=jnp.max(dmat, axis=1, keepdims=True),
                 log_ws=log_ws, ws_max=jnp.max(log_ws, axis=0, keepdims=True), qf=qf, kf=kf, qb=qb,
                 qk=_dot_nt(qb, kf.astype(BF16)))

    m = m_scr[:, 0:1]
    for j in chunks:
        c = st[j]
        m_new = jnp.maximum(c["g_last"] + m, c["ws_max"])
        c.update(m_prev=m, m_new=m_new, decay=jnp.exp(c["g_last"] + m - m_new))
        m = m_new
    m_scr[...] = jnp.broadcast_to(m, m_scr.shape)

    for j in chunks:
        c = st[j]
        vb = jnp.concatenate([v_ref[0, pl.ds(j * C, C), :], v_ref[1, pl.ds(j * C, C), :]], axis=1).astype(BF16)
        kw = c["kf"] * jnp.exp(c["log_ws"] - c["m_new"])
        c.update(vb=vb, upd=_dot_tn(kw.astype(BF16), vb), ksum=jnp.sum(kw, axis=0, keepdims=True))

    cm = c_scr[...]
    nv = n_scr[...]
    for j in chunks:
        c = st[j]
        c.update(c_in=cm.astype(BF16), n_in=nv)
        cm = c["decay"] * cm + c["upd"]
        nv = c["decay"] * nv + c["ksum"]
    c_scr[...] = cm
    n_scr[...] = nv

    for j in chunks:
        c = st[j]
        a_inter = c["g_col"] + c["m_prev"]
        m_t = jnp.maximum(a_inter, c["dmax"])
        w_inter = jnp.exp(a_inter - m_t)
        sqk = c["qk"] * jnp.exp(c["dmat"] - m_t)
        num = (w_inter * jnp.dot(c["qb"], c["c_in"], preferred_element_type=F32)
               + jnp.dot(sqk.astype(BF16), c["vb"], preferred_element_type=F32))
        den = (w_inter * jnp.sum(c["qf"] * c["n_in"], axis=1, keepdims=True)
               + jnp.sum(sqk, axis=1, keepdims=True))
        hh = num * (1.0 / jnp.maximum(jnp.abs(den), jnp.exp(-m_t)))
        y = hh * lax.rsqrt(jnp.mean(hh * hh, axis=-1, keepdims=True) + EPS) * nw
        og = jnp.concatenate([og_ref[0, pl.ds(j * C, C), :], og_ref[1, pl.ds(j * C, C), :]], axis=1).astype(F32)
        out_ref[pl.ds(j * C, C), :] = (y * _sigmoid(og)).astype(out_ref.dtype)


def _mlstm(proj3, gates, gate_bias_pad, conv_w, conv_b, norm_w, batch, seq, q_off):
    T = batch * seq
    rows = min(ML_ROWS, seq)
    nblk = seq // rows
    H = ML_HEADS
    k_off = q_off + H
    v_off = k_off + H
    o_off = v_off + 2 * H

    def slab(off):
        return pl.BlockSpec((1, rows, LANES), lambda b, h, c, off=off: (off + h, b * nblk + c, 0))

    def slab2(off):
        return pl.BlockSpec((2, rows, LANES), lambda b, h, c, off=off: (off // 2 + h, b * nblk + c, 0))

    qk_w = H * ML_DQK
    return pl.pallas_call(
        functools.partial(_mlstm_body, rows=rows),
        grid=(batch, H, nblk),
        in_specs=[
            slab(q_off), slab(k_off), slab2(v_off), slab2(o_off),
            pl.BlockSpec((1, rows, LANES), lambda b, h, c: (h, b * nblk + c, 0)),
            pl.BlockSpec((1, 1, LANES), lambda b, h, c: (h, 0, 0)),
            pl.BlockSpec((CONV_W, LANES), lambda b, h, c: (0, h)),
            pl.BlockSpec((CONV_W, LANES), lambda b, h, c: (0, H + h)),
            pl.BlockSpec((1, LANES), lambda b, h, c: (0, h)),
            pl.BlockSpec((1, LANES), lambda b, h, c: (0, H + h)),
            pl.BlockSpec((1, ML_DV), lambda b, h, c: (0, h)),
        ],
        out_specs=pl.BlockSpec((rows, ML_DV), lambda b, h, c: (b * nblk + c, h)),
        out_shape=jax.ShapeDtypeStruct((T, H * ML_DV), BF16),
        scratch_shapes=[
            pltpu.VMEM((ML_DQK, ML_DV), F32),
            pltpu.VMEM((1, ML_DQK), F32),
            pltpu.VMEM((1, LANES), F32),
            pltpu.VMEM((rows + 2 * SUBLANES, LANES), F32),
            pltpu.VMEM((rows + 2 * SUBLANES, LANES), F32),
            pltpu.VMEM((rows, LANES), F32),
            pltpu.VMEM((rows, LANES), F32),
        ],
        compiler_params=_cparams(("parallel", "parallel", "arbitrary")),
        name="mlstm",
    )(proj3, proj3, proj3, proj3, gates, gate_bias_pad, conv_w, conv_w,
      conv_b.reshape(1, 2 * qk_w), conv_b.reshape(1, 2 * qk_w), norm_w.reshape(1, H * ML_DV))


def _outproj_body(a_ref, b_ref, x_ref, wo_ref, ln_ref, wrh_ref, wrl_ref, br_ref,
                  x2_ref, h2_ref, idx_ref, gate_ref, rank_ref, cnt_ref, cnt_scr, *, sub_rows):
    tm = x_ref.shape[0]
    ka = a_ref.shape[1]
    s = _packed_rows(x_ref.shape[1])

    @pl.when(pl.program_id(0) == 0)
    def _():
        cnt_scr[...] = jnp.zeros_like(cnt_scr)

    lane = lax.broadcasted_iota(jnp.int32, (sub_rows, LANES), 1).astype(F32)
    onehots = [[] for _ in range(TOP_K)]
    for r0 in range(0, tm, sub_rows):
        rows = pl.ds(r0, sub_rows)
        res = (jnp.dot(a_ref[rows, :], wo_ref[0:ka, :], preferred_element_type=F32)
               + jnp.dot(b_ref[rows, :], wo_ref[ka:, :], preferred_element_type=F32))
        x2 = x_ref[rows, :] + res
        x2_ref[rows, :] = x2
        h2 = x2 * lax.rsqrt(jnp.mean(x2 * x2, axis=-1, keepdims=True) + EPS) * ln_ref[...]
        h_hi = h2.astype(BF16)
        h_hi32 = h_hi.astype(F32)
        _pack_store(h2_ref.at[pl.ds(r0 * s, sub_rows * s), :], h_hi32, rounded=True)

        h_lo = (h2 - h_hi32).astype(BF16)
        logits = (jnp.dot(h_hi, wrh_ref[...], preferred_element_type=F32)
                  + jnp.dot(h_lo, wrh_ref[...], preferred_element_type=F32)
                  + jnp.dot(h_hi, wrl_ref[...], preferred_element_type=F32)) + br_ref[...]
        vals, idxs = [], []
        cur = logits
        for _ in range(TOP_K):
            m = jnp.max(cur, axis=1, keepdims=True)
            ix = jnp.min(jnp.where(cur == m, lane, float(LANES)), axis=1, keepdims=True)
            vals.append(m)
            idxs.append(ix)
            cur = jnp.where(lane == ix, -jnp.inf, cur)
        es = [jnp.exp(v - vals[0]) for v in vals]
        inv = 1.0 / (es[0] + es[1] + es[2] + es[3])
        gate = jnp.zeros(logits.shape, F32)
        idx = jnp.zeros(logits.shape, F32)
        for k in range(TOP_K):
            gate = jnp.where(lane == float(k), es[k] * inv, gate)
            idx = jnp.where(lane == float(k), idxs[k], idx)
            onehots[k].append((lane == idxs[k]).astype(F32))
        gate_ref[rows, :] = gate
        idx_ref[rows, :] = idx.astype(jnp.int32)

    oh_k = [jnp.concatenate(o, axis=0) for o in onehots]
    oh = oh_k[0] + oh_k[1] + oh_k[2] + oh_k[3]
    r_i = lax.broadcasted_iota(jnp.int32, (tm, tm), 0)
    c_i = lax.broadcasted_iota(jnp.int32, (tm, tm), 1)
    before = jnp.dot((c_i < r_i).astype(BF16), oh.astype(BF16), preferred_element_type=F32) + cnt_scr[...]
    lane_t = lax.broadcasted_iota(jnp.int32, (tm, LANES), 1)
    rank = jnp.zeros((tm, LANES), F32)
    for k in range(TOP_K):
        rank = jnp.where(lane_t == k, jnp.sum(oh_k[k] * before, axis=1, keepdims=True), rank)
    rank_ref[...] = rank.astype(jnp.int32)
    cnt = cnt_scr[...] + jnp.sum(oh, axis=0, keepdims=True)
    cnt_scr[...] = cnt
    cnt_ref[...] = cnt.astype(jnp.int32)


def _outproj(a_out, b_out, x2d, w_out_bf, ln_w, wr_hi, wr_lo, b_router_pad):
    T, D = x2d.shape
    tm = min(OUTPROJ_TM, T)
    ka, kb = a_out.shape[1], b_out.shape[1]
    s = _packed_rows(D)
    row = lambda w: pl.BlockSpec((tm, w), lambda i: (i, 0))
    full = lambda r, c: pl.BlockSpec((r, c), lambda i: (0, 0))
    return pl.pallas_call(
        functools.partial(_outproj_body, sub_rows=min(OUTPROJ_SUB, tm)),
        grid=(T // tm,),
        in_specs=[row(ka), row(kb), row(D), full(ka + kb, D), full(1, D), full(D, LANES), full(D, LANES),
                  full(1, LANES)],
        out_specs=[row(D), pl.BlockSpec((tm * s, LANES), lambda i: (i, 0)), row(LANES), row(LANES), row(LANES),
                   full(1, LANES)],
        out_shape=[
            jax.ShapeDtypeStruct((T, D), F32),
            jax.ShapeDtypeStruct((T * s, LANES), jnp.uint32),
            jax.ShapeDtypeStruct((T, LANES), jnp.int32),
            jax.ShapeDtypeStruct((T, LANES), F32),
            jax.ShapeDtypeStruct((T, LANES), jnp.int32),
            jax.ShapeDtypeStruct((1, LANES), jnp.int32),
        ],
        scratch_shapes=[pltpu.VMEM((1, LANES), F32)],
        compiler_params=_cparams(("arbitrary",)),
        name="outproj_router",
    )(a_out, b_out, x2d, w_out_bf, ln_w, wr_hi, wr_lo, b_router_pad)


def _start_row_gather(idx_at, src_ref, buf, sem, n, s):
    def start(pair, carry):
        for p in range(DMA_QUEUES):
            i = pair * DMA_QUEUES + p
            src_row = pl.multiple_of(idx_at(i) * s, s)
            dst_row = pl.multiple_of(i * s, s)
            pltpu.make_async_copy(src_ref.at[pl.ds(src_row, s), :], buf.at[pl.ds(dst_row, s), :],
                                  sem).start(priority=p)
        return carry

    lax.fori_loop(0, n // DMA_QUEUES, start, 0, unroll=DMA_UNROLL // DMA_QUEUES)


def _wait_row_gather(buf, sem):
    pltpu.make_async_copy(buf, buf, sem).wait()


def _dispatch_body(zero_ref, dest_ref, h_ref, xs_ref, zbuf, sem, zsem, *, ntok, s, tm, n_blocks):
    blk = tm * s

    def zero_copy(b):
        return pltpu.make_async_copy(zbuf, xs_ref.at[pl.ds(pl.multiple_of(b * blk, blk), blk), :], zsem)

    @pl.when(pl.program_id(0) == 0)
    def _():
        zbuf[...] = jnp.zeros_like(zbuf)

        def zstart(b, carry):
            @pl.when(zero_ref[b] == 1)
            def _():
                zero_copy(b).start()
            return carry

        def zwait(b, carry):
            @pl.when(zero_ref[b] == 1)
            def _():
                zero_copy(b).wait()
            return carry

        lax.fori_loop(0, n_blocks, zstart, 0)
        lax.fori_loop(0, n_blocks, zwait, 0)

    def start(t, carry):
        src = h_ref.at[pl.ds(pl.multiple_of(t * s, s), s), :]
        for k in range(TOP_K):
            dst_row = pl.multiple_of(dest_ref[0, 0, t * TOP_K + k] * s, s)
            pltpu.make_async_copy(src, xs_ref.at[pl.ds(dst_row, s), :], sem).start(priority=k % DMA_QUEUES)
        return carry

    lax.fori_loop(0, ntok, start, 0, unroll=DMA_UNROLL // TOP_K)
    for _ in range(TOP_K):
        pltpu.make_async_copy(h_ref, h_ref, sem).wait()


def _dispatch(dest, zero_blk, h_packed, d, n_slots, tm):
    s = _packed_rows(d)
    T = h_packed.shape[0] // s
    ntok = min(DISPATCH_TOKENS, T)
    n = ntok * TOP_K
    grid_spec = pltpu.PrefetchScalarGridSpec(
        num_scalar_prefetch=1,
        grid=(T // ntok,),
        in_specs=[
            pl.BlockSpec((1, 1, n), lambda i, z: (i, 0, 0), memory_space=pltpu.SMEM),
            pl.BlockSpec((ntok * s, LANES), lambda i, z: (i, 0)),
        ],
        out_specs=pl.BlockSpec(memory_space=pl.ANY),
        scratch_shapes=[pltpu.VMEM((tm * s, LANES), jnp.uint32), pltpu.SemaphoreType.DMA,
                        pltpu.SemaphoreType.DMA],
    )
    return pl.pallas_call(
        functools.partial(_dispatch_body, ntok=ntok, s=s, tm=tm, n_blocks=zero_blk.shape[0]),
        grid_spec=grid_spec,
        out_shape=jax.ShapeDtypeStruct((n_slots * s, LANES), jnp.uint32),
        compiler_params=_cparams(("arbitrary",)),
        name="moe_dispatch",
    )(zero_blk, dest.reshape(T // ntok, 1, n), h_packed)


def _staged_weights(sched_refs, hbm_refs, stage_refs, bf_refs, sem):
    e_ref, wt_ref, first_ref, ne_ref, nt_ref, more_ref = sched_refs
    w = pl.program_id(0)
    tn = stage_refs[0].shape[1]

    def copies(e, t):
        col = pl.multiple_of(t * tn, tn)
        return [pltpu.make_async_copy(h.at[e, :, pl.ds(col, tn)], st, sem.at[k])
                for k, (h, st) in enumerate(zip(hbm_refs, stage_refs))]

    def start_all(cs):
        for c in cs:
            c.start()

    @pl.when(w == 0)
    def _():
        start_all(copies(e_ref[0], wt_ref[0]))

    @pl.when(first_ref[w] == 1)
    def _():
        for c in copies(e_ref[w], wt_ref[w]):
            c.wait()
        for st, bf in zip(stage_refs, bf_refs):
            for r in range(0, st.shape[0], 256):
                bf[r:r + 256, :] = st[r:r + 256, :].astype(BF16)

        @pl.when(more_ref[w] == 1)
        def _():
            start_all(copies(ne_ref[w], nt_ref[w]))


def _row_groups(nrows, tm, compute, clear):
    sub = min(MOE_SUB_ROWS, tm)
    groups = tm // sub
    for live in range(groups + 1):
        @pl.when((nrows > (live - 1) * sub) & (nrows <= live * sub))
        def _(live=live):
            for g in range(live):
                compute(g * sub, sub)
            for g in range(live, groups):
                clear(g * sub, sub)


def _moe_up_body(e_ref, wt_ref, r_ref, ot_ref, first_ref, nrows_ref, ne_ref, nt_ref, more_ref,
                 x_ref, wg_ref, wu_ref, bg_ref, bu_ref, o_ref, wg_stage, wu_stage, wg_scr, wu_scr, x_scr, sem):
    w = pl.program_id(0)
    _staged_weights((e_ref, wt_ref, first_ref, ne_ref, nt_ref, more_ref), (wg_ref, wu_ref),
                    (wg_stage, wu_stage), (wg_scr, wu_scr), sem)
    tm, d = x_scr.shape
    s = _packed_rows(d)
    tf = o_ref.shape[1]
    col = pl.multiple_of(wt_ref[w] * tf, tf)
    bias_g = bg_ref[pl.ds(e_ref[w], 1), pl.ds(col, tf)]
    bias_u = bu_ref[pl.ds(e_ref[w], 1), pl.ds(col, tf)]

    def compute(r0, n):
        rows = pl.ds(r0, n)
        for c in range(s):
            hi, lo = _unpack_load(x_ref, r0, n, s, c)
            x_scr[rows, c * LANES:(c + 1) * LANES] = hi.astype(BF16)
            x_scr[rows, d // 2 + c * LANES:d // 2 + (c + 1) * LANES] = lo.astype(BF16)
        x = x_scr[rows, :]
        gt = jnp.dot(x, wg_scr[...], preferred_element_type=F32) + bias_g
        up = jnp.dot(x, wu_scr[...], preferred_element_type=F32) + bias_u
        gt = jnp.minimum(gt, SWIGLU_LIMIT)
        up = jnp.clip(up, -SWIGLU_LIMIT, SWIGLU_LIMIT)
        swish = (0.5 * gt) * (1.0 + jnp.tanh((0.5 * SWIGLU_ALPHA) * gt))
        o_ref[rows, :] = ((up + 1.0) * swish).astype(o_ref.dtype)

    def clear(r0, n):
        o_ref[pl.ds(r0, n), :] = jnp.zeros((n, o_ref.shape[1]), o_ref.dtype)

    _row_groups(nrows_ref[w], tm, compute, clear)


def _moe_down_body(e_ref, wt_ref, r_ref, ot_ref, first_ref, nrows_ref, ne_ref, nt_ref, more_ref,
                   a_ref, wd_ref, bd_ref, o_ref, wd_stage, wd_scr, sem):
    w = pl.program_id(0)
    _staged_weights((e_ref, wt_ref, first_ref, ne_ref, nt_ref, more_ref), (wd_ref,), (wd_stage,), (wd_scr,), sem)
    tm = a_ref.shape[0]
    s = o_ref.shape[0] // tm
    bias_d = bd_ref[pl.ds(e_ref[w], 1), :]

    def compute(r0, n):
        y = jnp.dot(a_ref[pl.ds(r0, n), :], wd_scr[...], preferred_element_type=F32) + bias_d
        _pack_store(o_ref.at[pl.ds(r0 * s, n * s), :], y)

    def clear(r0, n):
        o_ref[pl.ds(r0 * s, n * s), :] = jnp.zeros((n * s, LANES), o_ref.dtype)

    _row_groups(nrows_ref[w], tm, compute, clear)


def _moe_schedule(counts, tm, n_tiles, n_blocks):
    n_items = n_tiles * n_blocks
    experts = jnp.arange(N_EXPERTS, dtype=jnp.int32)
    blocks_e = (counts + tm - 1) // tm
    bend = jnp.cumsum(blocks_e)
    bstart = bend - blocks_e
    item_end = n_tiles * bend
    total = item_end[-1]
    later = (experts[None, :] > experts[:, None]) & (blocks_e[None, :] > 0)
    next_e = jnp.min(jnp.where(later, experts[None, :], N_EXPERTS - 1), axis=1)

    w = jnp.arange(n_items, dtype=jnp.int32)
    valid = w < total
    wc = jnp.minimum(w, jnp.maximum(total - 1, 0))
    e = jnp.minimum(jnp.sum((item_end[None, :] <= wc[:, None]).astype(jnp.int32), axis=1), N_EXPERTS - 1)
    sel = e[:, None] == experts[None, :]
    pick = lambda table: jnp.sum(jnp.where(sel, table[None, :], 0), axis=1)
    nb = jnp.maximum(pick(blocks_e), 1)
    local = wc - n_tiles * pick(bstart)
    wtile = sum((local >= t * nb).astype(jnp.int32) for t in range(1, n_tiles)) if n_tiles > 1 else 0 * local
    jblk = local - wtile * nb
    spare = jnp.maximum(w - total, 0)
    rblk = jnp.where(valid, pick(bstart) + jblk, bend[-1] + spare // n_tiles)
    otile = jnp.where(valid, wtile, spare % n_tiles)
    nrows = jnp.where(valid, jnp.clip(pick(counts) - jblk * tm, 0, tm), 0)
    first = (jblk == 0) & valid
    last_tile = wtile == n_tiles - 1
    more = first & (w + nb < total)
    i32 = lambda a: a.astype(jnp.int32)
    return (e, i32(wtile), i32(rblk), i32(otile), i32(first), i32(nrows),
            i32(jnp.where(last_tile, pick(next_e), e)), i32(jnp.where(last_tile, 0, wtile + 1)), i32(more))


def _moe_up(sched, xs_packed, w_gate, w_up, b_gate, b_up):
    D, d_ff = w_gate.shape[1], w_gate.shape[2]
    s = _packed_rows(D)
    n_slots = xs_packed.shape[0] // s
    tm, tf = MOE_TM, min(MOE_TF, d_ff)
    n_items = sched[0].shape[0]
    wspec = pl.BlockSpec(memory_space=pl.ANY)
    bspec = pl.BlockSpec((N_EXPERTS, d_ff), lambda w, *_: (0, 0))
    grid_spec = pltpu.PrefetchScalarGridSpec(
        num_scalar_prefetch=len(sched),
        grid=(n_items,),
        in_specs=[pl.BlockSpec((tm * s, LANES), lambda w, e, wt, r, ot, *_: (r[w], 0)),
                  wspec, wspec, bspec, bspec],
        out_specs=pl.BlockSpec((tm, tf), lambda w, e, wt, r, ot, *_: (r[w], ot[w])),
        scratch_shapes=[pltpu.VMEM((D, tf), F32), pltpu.VMEM((D, tf), F32),
                        pltpu.VMEM((D, tf), BF16), pltpu.VMEM((D, tf), BF16), pltpu.VMEM((tm, D), BF16),
                        pltpu.SemaphoreType.DMA((2,))],
    )
    return pl.pallas_call(
        _moe_up_body,
        grid_spec=grid_spec,
        out_shape=jax.ShapeDtypeStruct((n_slots, d_ff), BF16),
        compiler_params=_cparams(("arbitrary",)),
        name="moe_up",
    )(*sched, xs_packed, w_gate, w_up, b_gate, b_up)


def _moe_down(sched, act, w_down, b_down):
    n_slots, d_ff = act.shape
    D = w_down.shape[2]
    tm = MOE_TM
    s = _packed_rows(D)
    n_items = sched[0].shape[0]
    grid_spec = pltpu.PrefetchScalarGridSpec(
        num_scalar_prefetch=len(sched),
        grid=(n_items,),
        in_specs=[
            pl.BlockSpec((tm, d_ff), lambda w, e, wt, r, ot, *_: (r[w], 0)),
            pl.BlockSpec(memory_space=pl.ANY),
            pl.BlockSpec((N_EXPERTS, D), lambda w, *_: (0, 0)),
        ],
        out_specs=pl.BlockSpec((tm * s, LANES), lambda w, e, wt, r, ot, *_: (r[w], 0)),
        scratch_shapes=[pltpu.VMEM((d_ff, D), F32), pltpu.VMEM((d_ff, D), BF16), pltpu.SemaphoreType.DMA((1,))],
    )
    return pl.pallas_call(
        _moe_down_body,
        grid_spec=grid_spec,
        out_shape=jax.ShapeDtypeStruct((n_slots * s, LANES), jnp.uint32),
        compiler_params=_cparams(("arbitrary",)),
        name="moe_down",
    )(*sched, act, w_down, b_down)


def _final_body(dest_ref, next_ref, x2_ref, gate_ref, w_ref, ys_ref, o_ref, buf0, buf1, sem0, sem1, *, tm, s):
    n = tm * TOP_K
    i = pl.program_id(0)

    def gather(idx_ref, tile, buf, sem):
        _start_row_gather(lambda r: idx_ref[0, 0, tile * n + r], ys_ref, buf, sem, n, s)

    def combine(tile, buf):
        rows = pl.ds(tile * tm, tm)
        gate = gate_ref[rows, :]
        half = o_ref.shape[1] // 2
        gates = [jnp.broadcast_to(gate[:, k:k + 1], (tm, LANES)) for k in range(TOP_K)]
        for c in range(s):
            acc_hi = x2_ref[rows, c * LANES:(c + 1) * LANES]
            acc_lo = x2_ref[rows, half + c * LANES:half + (c + 1) * LANES]
            for k in range(TOP_K):
                hi, lo = _unpack_load(buf, k * tm, tm, s, c)
                acc_hi = acc_hi + gates[k] * hi
                acc_lo = acc_lo + gates[k] * lo
            o_ref[rows, c * LANES:(c + 1) * LANES] = acc_hi
            o_ref[rows, half + c * LANES:half + (c + 1) * LANES] = acc_lo
        acc = o_ref[rows, :]
        o_ref[rows, :] = acc * lax.rsqrt(jnp.mean(acc * acc, axis=-1, keepdims=True) + EPS) * w_ref[...]

    @pl.when(i == 0)
    def _():
        gather(dest_ref, 0, buf0, sem0)

    gather(dest_ref, 1, buf1, sem1)
    _wait_row_gather(buf0, sem0)
    combine(0, buf0)
    gather(next_ref, 0, buf0, sem0)
    _wait_row_gather(buf1, sem1)
    combine(1, buf1)

    @pl.when(i == pl.num_programs(0) - 1)
    def _():
        _wait_row_gather(buf0, sem0)


def _final(x2, ys_packed, dest, gates_pad, w):
    T, D = x2.shape
    tm = min(FINAL_TM, T // 2)
    s = _packed_rows(D)
    n = tm * TOP_K
    steps = T // (2 * tm)
    dest_km = dest.reshape(steps, 2, tm, TOP_K).transpose(0, 1, 3, 2).reshape(steps, 1, 2 * n)
    buf = pltpu.VMEM((n * s, LANES), jnp.uint32)
    return pl.pallas_call(
        functools.partial(_final_body, tm=tm, s=s),
        grid=(steps,),
        in_specs=[
            pl.BlockSpec((1, 1, 2 * n), lambda i: (i, 0, 0), memory_space=pltpu.SMEM),
            pl.BlockSpec((1, 1, 2 * n), lambda i: (jnp.minimum(i + 1, steps - 1), 0, 0), memory_space=pltpu.SMEM),
            pl.BlockSpec((2 * tm, D), lambda i: (i, 0)),
            pl.BlockSpec((2 * tm, LANES), lambda i: (i, 0)),
            pl.BlockSpec((1, D), lambda i: (0, 0)),
            pl.BlockSpec(memory_space=pl.ANY),
        ],
        out_specs=pl.BlockSpec((2 * tm, D), lambda i: (i, 0)),
        out_shape=jax.ShapeDtypeStruct((T, D), F32),
        scratch_shapes=[buf, buf, pltpu.SemaphoreType.DMA, pltpu.SemaphoreType.DMA],
        compiler_params=_cparams(("arbitrary",)),
        name="final_norm",
    )(dest_km, dest_km, x2, gates_pad, w, ys_packed)


def _moe(h2_packed, T, D, top_idx, rank, counts, w_gate, b_gate, w_up, b_up, w_down, b_down):
    A = T * TOP_K
    tm = MOE_TM
    n_blocks = (A + N_EXPERTS * (tm - 1) + tm - 1) // tm
    n_slots = n_blocks * tm

    blocks_e = (counts + tm - 1) // tm
    bend = jnp.cumsum(blocks_e)
    bstart = bend - blocks_e
    experts = jnp.arange(N_EXPERTS, dtype=jnp.int32)
    first_slot = jnp.sum(jnp.where(top_idx[:, :, None] == experts, bstart * tm, 0), axis=-1)
    dest = (first_slot + rank).astype(jnp.int32).reshape(A)
    blk = jnp.arange(n_blocks, dtype=jnp.int32)
    is_last = jnp.any((blk[:, None] == bend[None, :] - 1) & (blocks_e[None, :] > 0), axis=1)
    zero_blk = ((blk >= bend[-1]) | is_last).astype(jnp.int32)

    xs = _dispatch(dest, zero_blk, h2_packed, D, n_slots, tm)
    d_ff = w_gate.shape[2]
    act = _moe_up(_moe_schedule(counts, tm, -(-d_ff // MOE_TF), n_blocks), xs, w_gate, w_up, b_gate, b_up)
    ys = _moe_down(_moe_schedule(counts, tm, 1, n_blocks), act, w_down, b_down)
    return ys, dest


def kernel(x, ln1_w, w_in, hg_lb_logits, hg_norm_w, ml_conv_w, ml_conv_b, ml_igate_b, ml_fgate_b, ml_norm_w,
           w_out, ln2_w, w_router, b_router, w_gate, b_gate, w_up, b_up, w_down, b_down, final_norm_w):
    B, S, D = x.shape
    T = B * S
    depth = w_in.shape[0]
    hg_w = HG_HEADS * HG_DK
    n_main = 4 * hg_w + 2 * ML_HEADS * ML_DQK + 2 * ML_HEADS * ML_DV
    lb_all = jnp.cumsum(jax.nn.softmax(hg_lb_logits.astype(F32), axis=0), axis=0)

    xc = x.reshape(T, D)
    for l in range(depth):
        w_gates_pad = jnp.pad(w_in[l][:, n_main:], ((0, 0), (0, LANES - 2 * ML_HEADS))).astype(BF16)
        w_bf = w_in[l].astype(BF16)
        ln1 = ln1_w[l].reshape(1, D)
        hf_tiles = hg_w // INPROJ_TN
        proj_f = _inproj(xc, ln1, w_bf, hf_tiles, lambda j: j + hf_tiles, F32)[0]
        proj_b, gates = _inproj(xc, ln1, w_bf, n_main // INPROJ_TN - hf_tiles,
                                lambda j: jnp.where(j >= hf_tiles, j + hf_tiles, j), BF16, w_gates_pad)
        a_out = _hgrn(proj_b, proj_f, lb_all[l], hg_norm_w[l], B, S)
        gate_bias = jnp.pad(jnp.concatenate([ml_igate_b[l], ml_fgate_b[l]]), (0, LANES - 2 * ML_HEADS))
        gates_h = jnp.stack([jnp.roll(gates, -hd, axis=1) for hd in range(ML_HEADS)])
        bias_h = jnp.stack([jnp.roll(gate_bias, -hd) for hd in range(ML_HEADS)]).reshape(ML_HEADS, 1, LANES)
        b_out = _mlstm(proj_b, gates_h, bias_h, ml_conv_w[l], ml_conv_b[l], ml_norm_w[l], B, S, 3 * HG_HEADS)
        wr_pad = jnp.pad(w_router[l], ((0, 0), (0, LANES - N_EXPERTS)))
        wr_hi = wr_pad.astype(BF16)
        wr_lo = (wr_pad - wr_hi.astype(F32)).astype(BF16)
        br_pad = jnp.pad(b_router[l], (0, LANES - N_EXPERTS), constant_values=-1e30).reshape(1, LANES)
        x2, h2, idx_pad, gates_pad, rank_pad, cnt = _outproj(
            a_out, b_out, xc, w_out[l].astype(BF16), ln2_w[l].reshape(1, D), wr_hi, wr_lo, br_pad)
        ys, dest = _moe(h2, T, D, idx_pad[:, :TOP_K], rank_pad[:, :TOP_K], cnt[0, :N_EXPERTS],
                        w_gate[l], b_gate[l], w_up[l], b_up[l], w_down[l], b_down[l])
        if l + 1 < depth:
            raise NotImplementedError("only the final layer fuses the output norm")
        xc = _final(x2, ys, dest, gates_pad, final_norm_w.reshape(1, D))
    return xc.reshape(B, S, D)
```

```python
import functools

import jax
import jax.numpy as jnp
from jax import lax
from jax.experimental import pallas as pl
from jax.experimental.pallas import tpu as pltpu

F32 = jnp.float32
BF16 = jnp.bfloat16

EPS = 1e-6
HG_HEADS = 8
HG_DK = 128
ML_HEADS = 4
ML_DQK = 128
ML_DV = 256
CONV_W = 4
N_EXPERTS = 32
TOP_K = 4
SWIGLU_ALPHA = 1.702
SWIGLU_LIMIT = 7.0

LANES = 128
SUBLANES = 8
VMEM_LIMIT_BYTES = 56 * 1024 * 1024

HG_CHUNK = 64
HG_SUB = 16
ML_CHUNK = 128
HG_ROWS = 1024
ML_ROWS = 512
HG_HEADS_PER_STEP = 2

INPROJ_TM = 1024
INPROJ_TN = 1024
OUTPROJ_TM = 512
OUTPROJ_SUB = 256
DISPATCH_TOKENS = 256
MOE_TM = 512
MOE_TF = 1024
MOE_SUB_ROWS = 128
DMA_UNROLL = 32
DMA_QUEUES = 2
FINAL_TM = 128


def _dot_nt(a, b):
    return lax.dot_general(a, b, (((1,), (1,)), ((), ())), preferred_element_type=F32)


def _dot_tn(a, b):
    return lax.dot_general(a, b, (((0,), (0,)), ((), ())), preferred_element_type=F32)


def _log_sigmoid(z):
    return jnp.minimum(z, 0.0) - jnp.log(1.0 + jnp.exp(-jnp.abs(z)))


def _sigmoid(z):
    return 0.5 * jnp.tanh(0.5 * z) + 0.5


def _cparams(semantics):
    return pltpu.CompilerParams(dimension_semantics=semantics, vmem_limit_bytes=VMEM_LIMIT_BYTES)


_HI_MASK = 0xFFFF0000


def _packed_rows(d):
    return d // (2 * LANES)


def _pack_store(o_ref, v, rounded=False):
    n, d = v.shape
    s, half = _packed_rows(d), d // 2
    bits = pltpu.bitcast(v if rounded else v.astype(BF16).astype(F32), jnp.uint32)
    for c in range(s):
        hi = bits[:, c * LANES:(c + 1) * LANES]
        lo = bits[:, half + c * LANES:half + (c + 1) * LANES]
        o_ref[pl.ds(c, n, stride=s), :] = hi | jnp.right_shift(lo, jnp.uint32(16))


def _unpack_load(buf, first_row, n, s, c):
    w = buf[pl.ds(first_row * s + c, n, stride=s), :]
    hi = pltpu.bitcast(w & jnp.uint32(_HI_MASK), F32)
    lo = pltpu.bitcast(jnp.left_shift(w, jnp.uint32(16)), F32)
    return hi, lo


def _inproj_body(*refs, tn, with_gates):
    if with_gates:
        x_ref, lnw_ref, w_ref, wg_ref, o_ref, g_ref, h_scr = refs
    else:
        x_ref, lnw_ref, w_ref, o_ref, h_scr = refs

    @pl.when(pl.program_id(1) == 0)
    def _():
        x = x_ref[...]
        h = x * lax.rsqrt(jnp.mean(x * x, axis=-1, keepdims=True) + EPS) * lnw_ref[...]
        hb = h.astype(BF16)
        h_scr[...] = hb
        if with_gates:
            g_ref[...] = jnp.dot(hb, wg_ref[...], preferred_element_type=F32)

    res = jnp.dot(h_scr[...], w_ref[...], preferred_element_type=F32)
    for c in range(tn // LANES):
        o_ref[c] = res[:, c * LANES:(c + 1) * LANES].astype(o_ref.dtype)


def _inproj(x2d, ln_w, w_bf, n_tiles, col_tile, out_dtype, w_gates_pad=None):
    T, D = x2d.shape
    tm = min(INPROJ_TM, T)
    tn = INPROJ_TN
    with_gates = w_gates_pad is not None
    in_specs = [
        pl.BlockSpec((tm, D), lambda i, j: (i, 0)),
        pl.BlockSpec((1, D), lambda i, j: (0, 0)),
        pl.BlockSpec((D, tn), lambda i, j: (0, col_tile(j))),
    ]
    out_specs = [pl.BlockSpec((tn // LANES, tm, LANES), lambda i, j: (j, i, 0))]
    out_shape = [jax.ShapeDtypeStruct((n_tiles * tn // LANES, T, LANES), out_dtype)]
    args = [x2d, ln_w, w_bf]
    if with_gates:
        in_specs.append(pl.BlockSpec((D, LANES), lambda i, j: (0, 0)))
        out_specs.append(pl.BlockSpec((tm, LANES), lambda i, j: (i, 0)))
        out_shape.append(jax.ShapeDtypeStruct((T, LANES), F32))
        args.append(w_gates_pad)
    return pl.pallas_call(
        functools.partial(_inproj_body, tn=tn, with_gates=with_gates),
        grid=(T // tm, n_tiles),
        in_specs=in_specs,
        out_specs=out_specs,
        out_shape=out_shape,
        scratch_shapes=[pltpu.VMEM((tm, D), BF16)],
        compiler_params=_cparams(("parallel", "arbitrary")),
        name="inproj_gates" if with_gates else "inproj",
    )(*args)


def _hgrn_body(q_ref, f_ref, i_ref, g_ref, lb_ref, nw_ref, o_ref, st_scr, *, rows, heads):
    C, SUB = HG_CHUNK, HG_SUB
    nsub = C // SUB

    @pl.when(pl.program_id(2) == 0)
    def _():
        st_scr[...] = jnp.zeros_like(st_scr)

    lbs = [lb_ref[hh] for hh in range(heads)]
    log_lbs = [jnp.log(lb) for lb in lbs]
    log_1mlbs = [jnp.log1p(-lb) for lb in lbs]

    r_i = lax.broadcasted_iota(jnp.int32, (C, C), 0)
    c_i = lax.broadcasted_iota(jnp.int32, (C, C), 1)
    tri = (c_i <= r_i).astype(F32)
    sub_shift = SUB.bit_length() - 1
    diag_mask = (c_i <= r_i) & (jnp.right_shift(r_i, sub_shift) == jnp.right_shift(c_i, sub_shift))
    row_id = lax.broadcasted_iota(jnp.int32, (C, HG_DK), 0)

    pairs = [(j, hh) for j in range(rows // C) for hh in range(heads)]
    tri_b = tri.astype(BF16)
    st = {p: {} for p in pairs}

    for p in pairs:
        j, hh = p
        z = f_ref[hh, pl.ds(j * C, C), :].astype(F32)
        e = jnp.exp(-jnp.abs(z))
        e1 = 1.0 + e
        log_sig = jnp.minimum(z, 0.0) - jnp.log(e1)
        sig_neg = jnp.where(z >= 0, e, 1.0) / e1
        cc = log_1mlbs[hh] + log_sig
        log_f = jnp.maximum(log_lbs[hh], cc) + jnp.log(1.0 + jnp.exp(-jnp.abs(log_lbs[hh] - cc)))
        st[p]["kk"] = (1.0 - lbs[hh]) * sig_neg
        hi = log_f.astype(BF16)
        lo = (log_f - hi.astype(F32)).astype(BF16)
        st[p]["b"] = (jnp.dot(tri_b, hi, preferred_element_type=F32)
                      + jnp.dot(tri_b, lo, preferred_element_type=F32))

    for p in pairs:
        j, hh = p
        q = q_ref[hh, pl.ds(j * C, C), :].astype(F32)
        kk, b = st[p]["kk"], st[p]["b"]
        b_last = b[C - 1:C, :]
        st[p]["dec"] = jnp.exp(b_last)
        st[p]["qe"] = (q * jnp.exp(b)).astype(BF16)
        st[p]["kdec"] = (kk * jnp.exp(b_last - b)).astype(BF16)
        refs = [b[I * SUB:I * SUB + 1, :] for I in range(nsub)]
        refb = jnp.concatenate([jnp.broadcast_to(r, (SUB, HG_DK)) for r in refs], axis=0)
        qd = (q * jnp.exp(b - refb)).astype(BF16)
        kd = (kk * jnp.exp(refb - b)).astype(BF16)
        q_parts, k_parts = [], []
        for J in range(nsub - 1):
            r = refs[J + 1]
            qj = q * jnp.exp(jnp.minimum(b - r, 0.0))
            kj = kk * jnp.exp(jnp.minimum(r - b, 0.0))
            q_parts.append(jnp.where(row_id >= (J + 1) * SUB, qj, 0.0).astype(BF16))
            k_parts.append(jnp.where((row_id >= J * SUB) & (row_id < (J + 1) * SUB), kj, 0.0).astype(BF16))
        st[p]["att_d"] = _dot_nt(qd, kd)
        st[p]["att_o"] = _dot_nt(jnp.concatenate(q_parts, axis=1), jnp.concatenate(k_parts, axis=1))

    for p in pairs:
        j, hh = p
        vb = i_ref[hh, pl.ds(j * C, C), :].astype(BF16)
        att = jnp.where(diag_mask, st[p]["att_d"], 0.0) + st[p]["att_o"]
        st[p]["intra"] = jnp.dot(att.astype(BF16), vb, preferred_element_type=F32)
        st[p]["upd"] = _dot_tn(vb, st[p]["kdec"])

    for hh in range(heads):
        s = st_scr[hh]
        for j in range(rows // C):
            st[(j, hh)]["s_in"] = s.astype(BF16)
            s = s * st[(j, hh)]["dec"] + st[(j, hh)]["upd"]
        st_scr[hh] = s

    for p in pairs:
        j, hh = p
        g = g_ref[hh, pl.ds(j * C, C), :].astype(F32)
        o = _dot_nt(st[p]["qe"], st[p]["s_in"]) + st[p]["intra"]
        y = o * lax.rsqrt(jnp.mean(o * o, axis=-1, keepdims=True) + EPS) * nw_ref[hh]
        o_ref[pl.ds(j * C, C), hh * LANES:(hh + 1) * LANES] = (y * (g * _sigmoid(g))).astype(o_ref.dtype)


def _hgrn(proj_b, proj_f, lb, norm_w, batch, seq):
    T = batch * seq
    rows = min(HG_ROWS, seq)
    nblk = seq // rows
    H = HG_HEADS
    hp = HG_HEADS_PER_STEP

    def slab(off):
        return pl.BlockSpec((hp, rows, LANES), lambda b, h, c, off=off: (off // hp + h, b * nblk + c, 0))

    vec = pl.BlockSpec((hp, 1, LANES), lambda b, h, c: (h, 0, 0))
    return pl.pallas_call(
        functools.partial(_hgrn_body, rows=rows, heads=hp),
        grid=(batch, H // hp, nblk),
        in_specs=[slab(0), slab(0), slab(H), slab(2 * H), vec, vec],
        out_specs=pl.BlockSpec((rows, hp * LANES), lambda b, h, c: (b * nblk + c, h)),
        out_shape=jax.ShapeDtypeStruct((T, H * LANES), BF16),
        scratch_shapes=[pltpu.VMEM((hp, LANES, HG_DK), F32)],
        compiler_params=_cparams(("parallel", "parallel", "arbitrary")),
        name="hgrn2",
    )(proj_b, proj_f, proj_b, proj_b, lb.reshape(H, 1, HG_DK), norm_w.reshape(H, 1, LANES))


def _mlstm_body(q_ref, k_ref, v_ref, og_ref, gt_ref, gb_ref, cwq_ref, cwk_ref, cbq_ref, cbk_ref, nw_ref,
                out_ref, c_scr, n_scr, m_scr, qx_scr, kx_scr, qc_scr, kc_scr, *, rows):
    C = ML_CHUNK
    PAD = SUBLANES

    @pl.when(pl.program_id(2) == 0)
    def _():
        c_scr[...] = jnp.zeros_like(c_scr)
        n_scr[...] = jnp.zeros_like(n_scr)
        m_scr[...] = jnp.zeros_like(m_scr)
        qx_scr[0:PAD, :] = jnp.zeros((PAD, LANES), F32)
        kx_scr[0:PAD, :] = jnp.zeros((PAD, LANES), F32)

    qx_scr[PAD:PAD + rows, :] = q_ref[0].astype(F32)
    kx_scr[PAD:PAD + rows, :] = k_ref[0].astype(F32)
    accq = jnp.zeros((rows, LANES), F32) + cbq_ref[...]
    acck = jnp.zeros((rows, LANES), F32) + cbk_ref[...]
    for j in range(CONV_W):
        off = PAD - (CONV_W - 1) + j
        accq = accq + cwq_ref[j:j + 1, :] * qx_scr[pl.ds(off, rows), :]
        acck = acck + cwk_ref[j:j + 1, :] * kx_scr[pl.ds(off, rows), :]
    qc_scr[...] = accq * _sigmoid(accq) * (ML_DQK ** -0.5)
    kc_scr[...] = acck * _sigmoid(acck)
    qx_scr[0:PAD, :] = qx_scr[rows:rows + PAD, :]
    kx_scr[0:PAD, :] = kx_scr[rows:rows + PAD, :]

    lane = lax.broadcasted_iota(jnp.int32, (C, C), 1)
    sub = lax.broadcasted_iota(jnp.int32, (C, C), 0)
    causal = lane <= sub
    lower = causal.astype(F32)
    upper = (sub <= lane).astype(F32)
    nw = nw_ref[...]
    gbias = gb_ref[0]

    chunks = range(rows // C)
    lower_b, upper_b = lower.astype(BF16), upper.astype(BF16)
    st = [{} for _ in chunks]

    def split(x):
        hi = x.astype(BF16)
        return hi, (x - hi.astype(F32)).astype(BF16)

    for j in chunks:
        gc = gt_ref[0, pl.ds(j * C, C), :] + gbias
        gct = gc.T
        li_col = gc[:, 0:1]
        li_row = gct[0:1, :]
        ch, cl = split(_log_sigmoid(gc[:, ML_HEADS:ML_HEADS + 1]))
        rh, rl = split(_log_sigmoid(gct[ML_HEADS:ML_HEADS + 1, :]))
        g_t = (jnp.dot(lower_b, jnp.broadcast_to(ch, (C, C)), preferred_element_type=F32)
               + jnp.dot(lower_b, jnp.broadcast_to(cl, (C, C)), preferred_element_type=F32))
        g_s = (jnp.dot(jnp.broadcast_to(rh, (C, C)), upper_b, preferred_element_type=F32)
               + jnp.dot(jnp.broadcast_to(rl, (C, C)), upper_b, preferred_element_type=F32))
        st[j].update(g_t=g_t, g_s=g_s, li_col=li_col, li_row=li_row)

    for j in chunks:
        c = st[j]
        g_col = c["g_t"][:, 0:1]
        g_last = c["g_t"][C - 1:C, 0:1]
        dmat = jnp.where(causal, c["g_t"] - c["g_s"] + c["li_row"], -jnp.inf)
        log_ws = g_last - g_col + c["li_col"]
        qf = qc_scr[pl.ds(j * C, C), :]
        kf = kc_scr[pl.ds(j * C, C), :]
        qb = qf.astype(BF16)
        c.update(g_col=g_col, g_last=g_last, dmat=dmat, dmax=jnp.max(dmat, axis=1, keepdims=True),
                 log_ws=log_ws, ws_max=jnp.max(log_ws, axis=0, keepdims=True), qf=qf, kf=kf, qb=qb,
                 qk=_dot_nt(qb, kf.astype(BF16)))

    m = m_scr[:, 0:1]
    for j in chunks:
        c = st[j]
        m_new = jnp.maximum(c["g_last"] + m, c["ws_max"])
        c.update(m_prev=m, m_new=m_new, decay=jnp.exp(c["g_last"] + m - m_new))
        m = m_new
    m_scr[...] = jnp.broadcast_to(m, m_scr.shape)

    for j in chunks:
        c = st[j]
        vb = jnp.concatenate([v_ref[0, pl.ds(j * C, C), :], v_ref[1, pl.ds(j * C, C), :]], axis=1).astype(BF16)
        kw = c["kf"] * jnp.exp(c["log_ws"] - c["m_new"])
        c.update(vb=vb, upd=_dot_tn(kw.astype(BF16), vb), ksum=jnp.sum(kw, axis=0, keepdims=True))

    cm = c_scr[...]
    nv = n_scr[...]
    for j in chunks:
        c = st[j]
        c.update(c_in=cm.astype(BF16), n_in=nv)
        cm = c["decay"] * cm + c["upd"]
        nv = c["decay"] * nv + c["ksum"]
    c_scr[...] = cm
    n_scr[...] = nv

    for j in chunks:
        c = st[j]
        a_inter = c["g_col"] + c["m_prev"]
        m_t = jnp.maximum(a_inter, c["dmax"])
        w_inter = jnp.exp(a_inter - m_t)
        sqk = c["qk"] * jnp.exp(c["dmat"] - m_t)
        num = (w_inter * jnp.dot(c["qb"], c["c_in"], preferred_element_type=F32)
               + jnp.dot(sqk.astype(BF16), c["vb"], preferred_element_type=F32))
        den = (w_inter * jnp.sum(c["qf"] * c["n_in"], axis=1, keepdims=True)
               + jnp.sum(sqk, axis=1, keepdims=True))
        hh = num * (1.0 / jnp.maximum(jnp.abs(den), jnp.exp(-m_t)))
        y = hh * lax.rsqrt(jnp.mean(hh * hh, axis=-1, keepdims=True) + EPS) * nw
        og = jnp.concatenate([og_ref[0, pl.ds(j * C, C), :], og_ref[1, pl.ds(j * C, C), :]], axis=1).astype(F32)
        out_ref[pl.ds(j * C, C), :] = (y * _sigmoid(og)).astype(out_ref.dtype)


def _mlstm(proj3, gates, gate_bias_pad, conv_w, conv_b, norm_w, batch, seq, q_off):
    T = batch * seq
    rows = min(ML_ROWS, seq)
    nblk = seq // rows
    H = ML_HEADS
    k_off = q_off + H
    v_off = k_off + H
    o_off = v_off + 2 * H

    def slab(off):
        return pl.BlockSpec((1, rows, LANES), lambda b, h, c, off=off: (off + h, b * nblk + c, 0))

    def slab2(off):
        return pl.BlockSpec((2, rows, LANES), lambda b, h, c, off=off: (off // 2 + h, b * nblk + c, 0))

    qk_w = H * ML_DQK
    return pl.pallas_call(
        functools.partial(_mlstm_body, rows=rows),
        grid=(batch, H, nblk),
        in_specs=[
            slab(q_off), slab(k_off), slab2(v_off), slab2(o_off),
            pl.BlockSpec((1, rows, LANES), lambda b, h, c: (h, b * nblk + c, 0)),
            pl.BlockSpec((1, 1, LANES), lambda b, h, c: (h, 0, 0)),
            pl.BlockSpec((CONV_W, LANES), lambda b, h, c: (0, h)),
            pl.BlockSpec((CONV_W, LANES), lambda b, h, c: (0, H + h)),
            pl.BlockSpec((1, LANES), lambda b, h, c: (0, h)),
            pl.BlockSpec((1, LANES), lambda b, h, c: (0, H + h)),
            pl.BlockSpec((1, ML_DV), lambda b, h, c: (0, h)),
        ],
        out_specs=pl.BlockSpec((rows, ML_DV), lambda b, h, c: (b * nblk + c, h)),
        out_shape=jax.ShapeDtypeStruct((T, H * ML_DV), BF16),
        scratch_shapes=[
            pltpu.VMEM((ML_DQK, ML_DV), F32),
            pltpu.VMEM((1, ML_DQK), F32),
            pltpu.VMEM((1, LANES), F32),
            pltpu.VMEM((rows + 2 * SUBLANES, LANES), F32),
            pltpu.VMEM((rows + 2 * SUBLANES, LANES), F32),
            pltpu.VMEM((rows, LANES), F32),
            pltpu.VMEM((rows, LANES), F32),
        ],
        compiler_params=_cparams(("parallel", "parallel", "arbitrary")),
        name="mlstm",
    )(proj3, proj3, proj3, proj3, gates, gate_bias_pad, conv_w, conv_w,
      conv_b.reshape(1, 2 * qk_w), conv_b.reshape(1, 2 * qk_w), norm_w.reshape(1, H * ML_DV))


def _outproj_body(a_ref, b_ref, x_ref, wo_ref, ln_ref, wrh_ref, wrl_ref, br_ref,
                  x2_ref, h2_ref, idx_ref, gate_ref, rank_ref, cnt_ref, cnt_scr, *, sub_rows):
    tm = x_ref.shape[0]
    ka = a_ref.shape[1]
    s = _packed_rows(x_ref.shape[1])

    @pl.when(pl.program_id(0) == 0)
    def _():
        cnt_scr[...] = jnp.zeros_like(cnt_scr)

    lane = lax.broadcasted_iota(jnp.int32, (sub_rows, LANES), 1).astype(F32)
    onehots = [[] for _ in range(TOP_K)]
    for r0 in range(0, tm, sub_rows):
        rows = pl.ds(r0, sub_rows)
        res = (jnp.dot(a_ref[rows, :], wo_ref[0:ka, :], preferred_element_type=F32)
               + jnp.dot(b_ref[rows, :], wo_ref[ka:, :], preferred_element_type=F32))
        x2 = x_ref[rows, :] + res
        x2_ref[rows, :] = x2
        h2 = x2 * lax.rsqrt(jnp.mean(x2 * x2, axis=-1, keepdims=True) + EPS) * ln_ref[...]
        h_hi = h2.astype(BF16)
        h_hi32 = h_hi.astype(F32)
        _pack_store(h2_ref.at[pl.ds(r0 * s, sub_rows * s), :], h_hi32, rounded=True)

        h_lo = (h2 - h_hi32).astype(BF16)
        logits = (jnp.dot(h_hi, wrh_ref[...], preferred_element_type=F32)
                  + jnp.dot(h_lo, wrh_ref[...], preferred_element_type=F32)
                  + jnp.dot(h_hi, wrl_ref[...], preferred_element_type=F32)) + br_ref[...]
        vals, idxs = [], []
        cur = logits
        for _ in range(TOP_K):
            m = jnp.max(cur, axis=1, keepdims=True)
            ix = jnp.min(jnp.where(cur == m, lane, float(LANES)), axis=1, keepdims=True)
            vals.append(m)
            idxs.append(ix)
            cur = jnp.where(lane == ix, -jnp.inf, cur)
        es = [jnp.exp(v - vals[0]) for v in vals]
        inv = 1.0 / (es[0] + es[1] + es[2] + es[3])
        gate = jnp.zeros(logits.shape, F32)
        idx = jnp.zeros(logits.shape, F32)
        for k in range(TOP_K):
            gate = jnp.where(lane == float(k), es[k] * inv, gate)
            idx = jnp.where(lane == float(k), idxs[k], idx)
            onehots[k].append((lane == idxs[k]).astype(F32))
        gate_ref[rows, :] = gate
        idx_ref[rows, :] = idx.astype(jnp.int32)

    oh_k = [jnp.concatenate(o, axis=0) for o in onehots]
    oh = oh_k[0] + oh_k[1] + oh_k[2] + oh_k[3]
    r_i = lax.broadcasted_iota(jnp.int32, (tm, tm), 0)
    c_i = lax.broadcasted_iota(jnp.int32, (tm, tm), 1)
    before = jnp.dot((c_i < r_i).astype(BF16), oh.astype(BF16), preferred_element_type=F32) + cnt_scr[...]
    lane_t = lax.broadcasted_iota(jnp.int32, (tm, LANES), 1)
    rank = jnp.zeros((tm, LANES), F32)
    for k in range(TOP_K):
        rank = jnp.where(lane_t == k, jnp.sum(oh_k[k] * before, axis=1, keepdims=True), rank)
    rank_ref[...] = rank.astype(jnp.int32)
    cnt = cnt_scr[...] + jnp.sum(oh, axis=0, keepdims=True)
    cnt_scr[...] = cnt
    cnt_ref[...] = cnt.astype(jnp.int32)


def _outproj(a_out, b_out, x2d, w_out_bf, ln_w, wr_hi, wr_lo, b_router_pad):
    T, D = x2d.shape
    tm = min(OUTPROJ_TM, T)
    ka, kb = a_out.shape[1], b_out.shape[1]
    s = _packed_rows(D)
    row = lambda w: pl.BlockSpec((tm, w), lambda i: (i, 0))
    full = lambda r, c: pl.BlockSpec((r, c), lambda i: (0, 0))
    return pl.pallas_call(
        functools.partial(_outproj_body, sub_rows=min(OUTPROJ_SUB, tm)),
        grid=(T // tm,),
        in_specs=[row(ka), row(kb), row(D), full(ka + kb, D), full(1, D), full(D, LANES), full(D, LANES),
                  full(1, LANES)],
        out_specs=[row(D), pl.BlockSpec((tm * s, LANES), lambda i: (i, 0)), row(LANES), row(LANES), row(LANES),
                   full(1, LANES)],
        out_shape=[
            jax.ShapeDtypeStruct((T, D), F32),
            jax.ShapeDtypeStruct((T * s, LANES), jnp.uint32),
            jax.ShapeDtypeStruct((T, LANES), jnp.int32),
            jax.ShapeDtypeStruct((T, LANES), F32),
            jax.ShapeDtypeStruct((T, LANES), jnp.int32),
            jax.ShapeDtypeStruct((1, LANES), jnp.int32),
        ],
        scratch_shapes=[pltpu.VMEM((1, LANES), F32)],
        compiler_params=_cparams(("arbitrary",)),
        name="outproj_router",
    )(a_out, b_out, x2d, w_out_bf, ln_w, wr_hi, wr_lo, b_router_pad)


def _start_row_gather(idx_at, src_ref, buf, sem, n, s):
    def start(pair, carry):
        for p in range(DMA_QUEUES):
            i = pair * DMA_QUEUES + p
            src_row = pl.multiple_of(idx_at(i) * s, s)
            dst_row = pl.multiple_of(i * s, s)
            pltpu.make_async_copy(src_ref.at[pl.ds(src_row, s), :], buf.at[pl.ds(dst_row, s), :],
                                  sem).start(priority=p)
        return carry

    lax.fori_loop(0, n // DMA_QUEUES, start, 0, unroll=DMA_UNROLL // DMA_QUEUES)


def _wait_row_gather(buf, sem):
    pltpu.make_async_copy(buf, buf, sem).wait()


def _dispatch_body(zero_ref, dest_ref, h_ref, xs_ref, zbuf, sem, zsem, *, ntok, s, tm, n_blocks):
    blk = tm * s

    def zero_copy(b):
        return pltpu.make_async_copy(zbuf, xs_ref.at[pl.ds(pl.multiple_of(b * blk, blk), blk), :], zsem)

    @pl.when(pl.program_id(0) == 0)
    def _():
        zbuf[...] = jnp.zeros_like(zbuf)

        def zstart(b, carry):
            @pl.when(zero_ref[b] == 1)
            def _():
                zero_copy(b).start()
            return carry

        def zwait(b, carry):
            @pl.when(zero_ref[b] == 1)
            def _():
                zero_copy(b).wait()
            return carry

        lax.fori_loop(0, n_blocks, zstart, 0)
        lax.fori_loop(0, n_blocks, zwait, 0)

    def start(t, carry):
        src = h_ref.at[pl.ds(pl.multiple_of(t * s, s), s), :]
        for k in range(TOP_K):
            dst_row = pl.multiple_of(dest_ref[0, 0, t * TOP_K + k] * s, s)
            pltpu.make_async_copy(src, xs_ref.at[pl.ds(dst_row, s), :], sem).start(priority=k % DMA_QUEUES)
        return carry

    lax.fori_loop(0, ntok, start, 0, unroll=DMA_UNROLL // TOP_K)
    for _ in range(TOP_K):
        pltpu.make_async_copy(h_ref, h_ref, sem).wait()


def _dispatch(dest, zero_blk, h_packed, d, n_slots, tm):
    s = _packed_rows(d)
    T = h_packed.shape[0] // s
    ntok = min(DISPATCH_TOKENS, T)
    n = ntok * TOP_K
    grid_spec = pltpu.PrefetchScalarGridSpec(
        num_scalar_prefetch=1,
        grid=(T // ntok,),
        in_specs=[
            pl.BlockSpec((1, 1, n), lambda i, z: (i, 0, 0), memory_space=pltpu.SMEM),
            pl.BlockSpec((ntok * s, LANES), lambda i, z: (i, 0)),
        ],
        out_specs=pl.BlockSpec(memory_space=pl.ANY),
        scratch_shapes=[pltpu.VMEM((tm * s, LANES), jnp.uint32), pltpu.SemaphoreType.DMA,
                        pltpu.SemaphoreType.DMA],
    )
    return pl.pallas_call(
        functools.partial(_dispatch_body, ntok=ntok, s=s, tm=tm, n_blocks=zero_blk.shape[0]),
        grid_spec=grid_spec,
        out_shape=jax.ShapeDtypeStruct((n_slots * s, LANES), jnp.uint32),
        compiler_params=_cparams(("arbitrary",)),
        name="moe_dispatch",
    )(zero_blk, dest.reshape(T // ntok, 1, n), h_packed)


def _staged_weights(sched_refs, hbm_refs, stage_refs, bf_refs, sem):
    e_ref, wt_ref, first_ref, ne_ref, nt_ref, more_ref = sched_refs
    w = pl.program_id(0)
    tn = stage_refs[0].shape[1]

    def copies(e, t):
        col = pl.multiple_of(t * tn, tn)
        return [pltpu.make_async_copy(h.at[e, :, pl.ds(col, tn)], st, sem.at[k])
                for k, (h, st) in enumerate(zip(hbm_refs, stage_refs))]

    def start_all(cs):
        for c in cs:
            c.start()

    @pl.when(w == 0)
    def _():
        start_all(copies(e_ref[0], wt_ref[0]))

    @pl.when(first_ref[w] == 1)
    def _():
        for c in copies(e_ref[w], wt_ref[w]):
            c.wait()
        for st, bf in zip(stage_refs, bf_refs):
            for r in range(0, st.shape[0], 256):
                bf[r:r + 256, :] = st[r:r + 256, :].astype(BF16)

        @pl.when(more_ref[w] == 1)
        def _():
            start_all(copies(ne_ref[w], nt_ref[w]))


def _row_groups(nrows, tm, compute, clear):
    sub = min(MOE_SUB_ROWS, tm)
    groups = tm // sub
    for live in range(groups + 1):
        @pl.when((nrows > (live - 1) * sub) & (nrows <= live * sub))
        def _(live=live):
            for g in range(live):
                compute(g * sub, sub)
            for g in range(live, groups):
                clear(g * sub, sub)


def _moe_up_body(e_ref, wt_ref, r_ref, ot_ref, first_ref, nrows_ref, ne_ref, nt_ref, more_ref,
                 x_ref, wg_ref, wu_ref, bg_ref, bu_ref, o_ref, wg_stage, wu_stage, wg_scr, wu_scr, x_scr, sem):
    w = pl.program_id(0)
    _staged_weights((e_ref, wt_ref, first_ref, ne_ref, nt_ref, more_ref), (wg_ref, wu_ref),
                    (wg_stage, wu_stage), (wg_scr, wu_scr), sem)
    tm, d = x_scr.shape
    s = _packed_rows(d)

    def compute(r0, n):
        rows = pl.ds(r0, n)
        for c in range(s):
            hi, lo = _unpack_load(x_ref, r0, n, s, c)
            x_scr[rows, c * LANES:(c + 1) * LANES] = hi.astype(BF16)
            x_scr[rows, d // 2 + c * LANES:d // 2 + (c + 1) * LANES] = lo.astype(BF16)
        x = x_scr[rows, :]
        gt = jnp.dot(x, wg_scr[...], preferred_element_type=F32) + bg_ref[0]
        up = jnp.dot(x, wu_scr[...], preferred_element_type=F32) + bu_ref[0]
        gt = jnp.minimum(gt, SWIGLU_LIMIT)
        up = jnp.clip(up, -SWIGLU_LIMIT, SWIGLU_LIMIT)
        swish = (0.5 * gt) * (1.0 + jnp.tanh((0.5 * SWIGLU_ALPHA) * gt))
        o_ref[rows, :] = ((up + 1.0) * swish).astype(o_ref.dtype)

    def clear(r0, n):
        o_ref[pl.ds(r0, n), :] = jnp.zeros((n, o_ref.shape[1]), o_ref.dtype)

    _row_groups(nrows_ref[w], tm, compute, clear)


def _moe_down_body(e_ref, wt_ref, r_ref, ot_ref, first_ref, nrows_ref, ne_ref, nt_ref, more_ref,
                   a_ref, wd_ref, bd_ref, o_ref, wd_stage, wd_scr, sem):
    w = pl.program_id(0)
    _staged_weights((e_ref, wt_ref, first_ref, ne_ref, nt_ref, more_ref), (wd_ref,), (wd_stage,), (wd_scr,), sem)
    tm = a_ref.shape[0]
    s = o_ref.shape[0] // tm

    def compute(r0, n):
        y = jnp.dot(a_ref[pl.ds(r0, n), :], wd_scr[...], preferred_element_type=F32) + bd_ref[0]
        _pack_store(o_ref.at[pl.ds(r0 * s, n * s), :], y)

    def clear(r0, n):
        o_ref[pl.ds(r0 * s, n * s), :] = jnp.zeros((n * s, LANES), o_ref.dtype)

    _row_groups(nrows_ref[w], tm, compute, clear)


def _moe_schedule(counts, tm, n_tiles, n_blocks):
    n_items = n_tiles * n_blocks
    experts = jnp.arange(N_EXPERTS, dtype=jnp.int32)
    blocks_e = (counts + tm - 1) // tm
    bend = jnp.cumsum(blocks_e)
    bstart = bend - blocks_e
    item_end = n_tiles * bend
    total = item_end[-1]
    later = (experts[None, :] > experts[:, None]) & (blocks_e[None, :] > 0)
    next_e = jnp.min(jnp.where(later, experts[None, :], N_EXPERTS - 1), axis=1)

    w = jnp.arange(n_items, dtype=jnp.int32)
    valid = w < total
    wc = jnp.minimum(w, jnp.maximum(total - 1, 0))
    e = jnp.minimum(jnp.sum((item_end[None, :] <= wc[:, None]).astype(jnp.int32), axis=1), N_EXPERTS - 1)
    sel = e[:, None] == experts[None, :]
    pick = lambda table: jnp.sum(jnp.where(sel, table[None, :], 0), axis=1)
    nb = jnp.maximum(pick(blocks_e), 1)
    local = wc - n_tiles * pick(bstart)
    wtile = sum((local >= t * nb).astype(jnp.int32) for t in range(1, n_tiles)) if n_tiles > 1 else 0 * local
    jblk = local - wtile * nb
    spare = jnp.maximum(w - total, 0)
    rblk = jnp.where(valid, pick(bstart) + jblk, bend[-1] + spare // n_tiles)
    otile = jnp.where(valid, wtile, spare % n_tiles)
    nrows = jnp.where(valid, jnp.clip(pick(counts) - jblk * tm, 0, tm), 0)
    first = (jblk == 0) & valid
    last_tile = wtile == n_tiles - 1
    more = first & (w + nb < total)
    i32 = lambda a: a.astype(jnp.int32)
    return (e, i32(wtile), i32(rblk), i32(otile), i32(first), i32(nrows),
            i32(jnp.where(last_tile, pick(next_e), e)), i32(jnp.where(last_tile, 0, wtile + 1)), i32(more))


def _moe_up(sched, xs_packed, w_gate, w_up, b_gate, b_up):
    D, d_ff = w_gate.shape[1], w_gate.shape[2]
    s = _packed_rows(D)
    n_slots = xs_packed.shape[0] // s
    tm, tf = MOE_TM, min(MOE_TF, d_ff)
    n_items = sched[0].shape[0]
    wspec = pl.BlockSpec(memory_space=pl.ANY)
    bspec = pl.BlockSpec((1, 1, tf), lambda w, e, wt, r, ot, *_: (e[w], 0, wt[w]))
    grid_spec = pltpu.PrefetchScalarGridSpec(
        num_scalar_prefetch=len(sched),
        grid=(n_items,),
        in_specs=[pl.BlockSpec((tm * s, LANES), lambda w, e, wt, r, ot, *_: (r[w], 0)),
                  wspec, wspec, bspec, bspec],
        out_specs=pl.BlockSpec((tm, tf), lambda w, e, wt, r, ot, *_: (r[w], ot[w])),
        scratch_shapes=[pltpu.VMEM((D, tf), F32), pltpu.VMEM((D, tf), F32),
                        pltpu.VMEM((D, tf), BF16), pltpu.VMEM((D, tf), BF16), pltpu.VMEM((tm, D), BF16),
                        pltpu.SemaphoreType.DMA((2,))],
    )
    return pl.pallas_call(
        _moe_up_body,
        grid_spec=grid_spec,
        out_shape=jax.ShapeDtypeStruct((n_slots, d_ff), BF16),
        compiler_params=_cparams(("arbitrary",)),
        name="moe_up",
    )(*sched, xs_packed, w_gate, w_up, b_gate.reshape(N_EXPERTS, 1, d_ff), b_up.reshape(N_EXPERTS, 1, d_ff))


def _moe_down(sched, act, w_down, b_down):
    n_slots, d_ff = act.shape
    D = w_down.shape[2]
    tm = MOE_TM
    s = _packed_rows(D)
    n_items = sched[0].shape[0]
    grid_spec = pltpu.PrefetchScalarGridSpec(
        num_scalar_prefetch=len(sched),
        grid=(n_items,),
        in_specs=[
            pl.BlockSpec((tm, d_ff), lambda w, e, wt, r, ot, *_: (r[w], 0)),
            pl.BlockSpec(memory_space=pl.ANY),
            pl.BlockSpec((1, 1, D), lambda w, e, wt, r, ot, *_: (e[w], 0, 0)),
        ],
        out_specs=pl.BlockSpec((tm * s, LANES), lambda w, e, wt, r, ot, *_: (r[w], 0)),
        scratch_shapes=[pltpu.VMEM((d_ff, D), F32), pltpu.VMEM((d_ff, D), BF16), pltpu.SemaphoreType.DMA((1,))],
    )
    return pl.pallas_call(
        _moe_down_body,
        grid_spec=grid_spec,
        out_shape=jax.ShapeDtypeStruct((n_slots * s, LANES), jnp.uint32),
        compiler_params=_cparams(("arbitrary",)),
        name="moe_down",
    )(*sched, act, w_down, b_down.reshape(N_EXPERTS, 1, D))


def _final_body(dest_ref, next_ref, x2_ref, gate_ref, w_ref, ys_ref, o_ref, buf0, buf1, sem0, sem1, *, tm, s):
    n = tm * TOP_K
    i = pl.program_id(0)

    def gather(idx_ref, tile, buf, sem):
        _start_row_gather(lambda r: idx_ref[0, 0, tile * n + r], ys_ref, buf, sem, n, s)

    def combine(tile, buf):
        rows = pl.ds(tile * tm, tm)
        gate = gate_ref[rows, :]
        half = o_ref.shape[1] // 2
        gates = [jnp.broadcast_to(gate[:, k:k + 1], (tm, LANES)) for k in range(TOP_K)]
        for c in range(s):
            acc_hi = x2_ref[rows, c * LANES:(c + 1) * LANES]
            acc_lo = x2_ref[rows, half + c * LANES:half + (c + 1) * LANES]
            for k in range(TOP_K):
                hi, lo = _unpack_load(buf, k * tm, tm, s, c)
                acc_hi = acc_hi + gates[k] * hi
                acc_lo = acc_lo + gates[k] * lo
            o_ref[rows, c * LANES:(c + 1) * LANES] = acc_hi
            o_ref[rows, half + c * LANES:half + (c + 1) * LANES] = acc_lo
        acc = o_ref[rows, :]
        o_ref[rows, :] = acc * lax.rsqrt(jnp.mean(acc * acc, axis=-1, keepdims=True) + EPS) * w_ref[...]

    @pl.when(i == 0)
    def _():
        gather(dest_ref, 0, buf0, sem0)

    gather(dest_ref, 1, buf1, sem1)
    _wait_row_gather(buf0, sem0)
    combine(0, buf0)
    gather(next_ref, 0, buf0, sem0)
    _wait_row_gather(buf1, sem1)
    combine(1, buf1)

    @pl.when(i == pl.num_programs(0) - 1)
    def _():
        _wait_row_gather(buf0, sem0)


def _final(x2, ys_packed, dest, gates_pad, w):
    T, D = x2.shape
    tm = min(FINAL_TM, T // 2)
    s = _packed_rows(D)
    n = tm * TOP_K
    steps = T // (2 * tm)
    dest_km = dest.reshape(steps, 2, tm, TOP_K).transpose(0, 1, 3, 2).reshape(steps, 1, 2 * n)
    buf = pltpu.VMEM((n * s, LANES), jnp.uint32)
    return pl.pallas_call(
        functools.partial(_final_body, tm=tm, s=s),
        grid=(steps,),
        in_specs=[
            pl.BlockSpec((1, 1, 2 * n), lambda i: (i, 0, 0), memory_space=pltpu.SMEM),
            pl.BlockSpec((1, 1, 2 * n), lambda i: (jnp.minimum(i + 1, steps - 1), 0, 0), memory_space=pltpu.SMEM),
            pl.BlockSpec((2 * tm, D), lambda i: (i, 0)),
            pl.BlockSpec((2 * tm, LANES), lambda i: (i, 0)),
            pl.BlockSpec((1, D), lambda i: (0, 0)),
            pl.BlockSpec(memory_space=pl.ANY),
        ],
        out_specs=pl.BlockSpec((2 * tm, D), lambda i: (i, 0)),
        out_shape=jax.ShapeDtypeStruct((T, D), F32),
        scratch_shapes=[buf, buf, pltpu.SemaphoreType.DMA, pltpu.SemaphoreType.DMA],
        compiler_params=_cparams(("arbitrary",)),
        name="final_norm",
    )(dest_km, dest_km, x2, gates_pad, w, ys_packed)


def _moe(h2_packed, T, D, top_idx, rank, counts, w_gate, b_gate, w_up, b_up, w_down, b_down):
    A = T * TOP_K
    tm = MOE_TM
    n_blocks = (A + N_EXPERTS * (tm - 1) + tm - 1) // tm
    n_slots = n_blocks * tm

    blocks_e = (counts + tm - 1) // tm
    bend = jnp.cumsum(blocks_e)
    bstart = bend - blocks_e
    experts = jnp.arange(N_EXPERTS, dtype=jnp.int32)
    first_slot = jnp.sum(jnp.where(top_idx[:, :, None] == experts, bstart * tm, 0), axis=-1)
    dest = (first_slot + rank).astype(jnp.int32).reshape(A)
    blk = jnp.arange(n_blocks, dtype=jnp.int32)
    is_last = jnp.any((blk[:, None] == bend[None, :] - 1) & (blocks_e[None, :] > 0), axis=1)
    zero_blk = ((blk >= bend[-1]) | is_last).astype(jnp.int32)

    xs = _dispatch(dest, zero_blk, h2_packed, D, n_slots, tm)
    d_ff = w_gate.shape[2]
    act = _moe_up(_moe_schedule(counts, tm, -(-d_ff // MOE_TF), n_blocks), xs, w_gate, w_up, b_gate, b_up)
    ys = _moe_down(_moe_schedule(counts, tm, 1, n_blocks), act, w_down, b_down)
    return ys, dest


def kernel(x, ln1_w, w_in, hg_lb_logits, hg_norm_w, ml_conv_w, ml_conv_b, ml_igate_b, ml_fgate_b, ml_norm_w,
           w_out, ln2_w, w_router, b_router, w_gate, b_gate, w_up, b_up, w_down, b_down, final_norm_w):
    B, S, D = x.shape
    T = B * S
    depth = w_in.shape[0]
    hg_w = HG_HEADS * HG_DK
    n_main = 4 * hg_w + 2 * ML_HEADS * ML_DQK + 2 * ML_HEADS * ML_DV
    lb_all = jnp.cumsum(jax.nn.softmax(hg_lb_logits.astype(F32), axis=0), axis=0)

    xc = x.reshape(T, D)
    for l in range(depth):
        w_gates_pad = jnp.pad(w_in[l][:, n_main:], ((0, 0), (0, LANES - 2 * ML_HEADS))).astype(BF16)
        w_bf = w_in[l].astype(BF16)
        ln1 = ln1_w[l].reshape(1, D)
        hf_tiles = hg_w // INPROJ_TN
        proj_f = _inproj(xc, ln1, w_bf, hf_tiles, lambda j: j + hf_tiles, F32)[0]
        proj_b, gates = _inproj(xc, ln1, w_bf, n_main // INPROJ_TN - hf_tiles,
                                lambda j: jnp.where(j >= hf_tiles, j + hf_tiles, j), BF16, w_gates_pad)
        a_out = _hgrn(proj_b, proj_f, lb_all[l], hg_norm_w[l], B, S)
        gate_bias = jnp.pad(jnp.concatenate([ml_igate_b[l], ml_fgate_b[l]]), (0, LANES - 2 * ML_HEADS))
        gates_h = jnp.stack([jnp.roll(gates, -hd, axis=1) for hd in range(ML_HEADS)])
        bias_h = jnp.stack([jnp.roll(gate_bias, -hd) for hd in range(ML_HEADS)]).reshape(ML_HEADS, 1, LANES)
        b_out = _mlstm(proj_b, gates_h, bias_h, ml_conv_w[l], ml_conv_b[l], ml_norm_w[l], B, S, 3 * HG_HEADS)
        wr_pad = jnp.pad(w_router[l], ((0, 0), (0, LANES - N_EXPERTS)))
        wr_hi = wr_pad.astype(BF16)
        wr_lo = (wr_pad - wr_hi.astype(F32)).astype(BF16)
        br_pad = jnp.pad(b_router[l], (0, LANES - N_EXPERTS), constant_values=-1e30).reshape(1, LANES)
        x2, h2, idx_pad, gates_pad, rank_pad, cnt = _outproj(
            a_out, b_out, xc, w_out[l].astype(BF16), ln2_w[l].reshape(1, D), wr_hi, wr_lo, br_pad)
        ys, dest = _moe(h2, T, D, idx_pad[:, :TOP_K], rank_pad[:, :TOP_K], cnt[0, :N_EXPERTS],
                        w_gate[l], b_gate[l], w_up[l], b_up[l], w_down[l], b_down[l])
        if l + 1 < depth:
            raise NotImplementedError("only the final layer fuses the output norm")
        xc = _final(x2, ys, dest, gates_pad, final_norm_w.reshape(1, D))
    return xc.reshape(B, S, D)
```

```python
import functools

import jax
import jax.numpy as jnp
from jax import lax
from jax.experimental import pallas as pl
from jax.experimental.pallas import tpu as pltpu

F32 = jnp.float32
BF16 = jnp.bfloat16

EPS = 1e-6
HG_HEADS = 8
HG_DK = 128
ML_HEADS = 4
ML_DQK = 128
ML_DV = 256
CONV_W = 4
N_EXPERTS = 32
TOP_K = 4
SWIGLU_ALPHA = 1.702
SWIGLU_LIMIT = 7.0

LANES = 128
SUBLANES = 8
VMEM_LIMIT_BYTES = 56 * 1024 * 1024

HG_CHUNK = 64
HG_SUB = 16
ML_CHUNK = 128
HG_ROWS = 1024
ML_ROWS = 512
HG_HEADS_PER_STEP = 2

INPROJ_TM = 1024
INPROJ_TN = 1024
OUTPROJ_TM = 512
OUTPROJ_SUB = 256
DISPATCH_TOKENS = 512
MOE_TM = 512
MOE_TF = 1024
MOE_SUB_ROWS = 128
DMA_UNROLL = 64
DMA_QUEUES = 2
FINAL_TM = 256


def _dot_nt(a, b):
    return lax.dot_general(a, b, (((1,), (1,)), ((), ())), preferred_element_type=F32)


def _dot_tn(a, b):
    return lax.dot_general(a, b, (((0,), (0,)), ((), ())), preferred_element_type=F32)


def _log_sigmoid(z):
    return jnp.minimum(z, 0.0) - jnp.log(1.0 + jnp.exp(-jnp.abs(z)))


def _sigmoid(z):
    return 0.5 * jnp.tanh(0.5 * z) + 0.5


def _cparams(semantics):
    return pltpu.CompilerParams(dimension_semantics=semantics, vmem_limit_bytes=VMEM_LIMIT_BYTES)


_HI_MASK = 0xFFFF0000


def _packed_rows(d):
    return d // (2 * LANES)


def _pack_store(o_ref, v, rounded=False):
    n, d = v.shape
    s, half = _packed_rows(d), d // 2
    bits = pltpu.bitcast(v if rounded else v.astype(BF16).astype(F32), jnp.uint32)
    for c in range(s):
        hi = bits[:, c * LANES:(c + 1) * LANES]
        lo = bits[:, half + c * LANES:half + (c + 1) * LANES]
        o_ref[pl.ds(c, n, stride=s), :] = hi | jnp.right_shift(lo, jnp.uint32(16))


def _unpack_load(buf, first_row, n, s, c):
    w = buf[pl.ds(first_row * s + c, n, stride=s), :]
    hi = pltpu.bitcast(w & jnp.uint32(_HI_MASK), F32)
    lo = pltpu.bitcast(jnp.left_shift(w, jnp.uint32(16)), F32)
    return hi, lo


def _inproj_body(*refs, tn, with_gates):
    if with_gates:
        x_ref, lnw_ref, w_ref, wg_ref, o_ref, g_ref, h_scr = refs
    else:
        x_ref, lnw_ref, w_ref, o_ref, h_scr = refs

    @pl.when(pl.program_id(1) == 0)
    def _():
        x = x_ref[...]
        h = x * lax.rsqrt(jnp.mean(x * x, axis=-1, keepdims=True) + EPS) * lnw_ref[...]
        hb = h.astype(BF16)
        h_scr[...] = hb
        if with_gates:
            g_ref[...] = jnp.dot(hb, wg_ref[...], preferred_element_type=F32)

    res = jnp.dot(h_scr[...], w_ref[...], preferred_element_type=F32)
    for c in range(tn // LANES):
        o_ref[c] = res[:, c * LANES:(c + 1) * LANES].astype(o_ref.dtype)


def _inproj(x2d, ln_w, w_bf, n_tiles, col_tile, out_dtype, w_gates_pad=None):
    T, D = x2d.shape
    tm = min(INPROJ_TM, T)
    tn = INPROJ_TN
    with_gates = w_gates_pad is not None
    in_specs = [
        pl.BlockSpec((tm, D), lambda i, j: (i, 0)),
        pl.BlockSpec((1, D), lambda i, j: (0, 0)),
        pl.BlockSpec((D, tn), lambda i, j: (0, col_tile(j))),
    ]
    out_specs = [pl.BlockSpec((tn // LANES, tm, LANES), lambda i, j: (j, i, 0))]
    out_shape = [jax.ShapeDtypeStruct((n_tiles * tn // LANES, T, LANES), out_dtype)]
    args = [x2d, ln_w, w_bf]
    if with_gates:
        in_specs.append(pl.BlockSpec((D, LANES), lambda i, j: (0, 0)))
        out_specs.append(pl.BlockSpec((tm, LANES), lambda i, j: (i, 0)))
        out_shape.append(jax.ShapeDtypeStruct((T, LANES), F32))
        args.append(w_gates_pad)
    return pl.pallas_call(
        functools.partial(_inproj_body, tn=tn, with_gates=with_gates),
        grid=(T // tm, n_tiles),
        in_specs=in_specs,
        out_specs=out_specs,
        out_shape=out_shape,
        scratch_shapes=[pltpu.VMEM((tm, D), BF16)],
        compiler_params=_cparams(("parallel", "arbitrary")),
        name="inproj_gates" if with_gates else "inproj",
    )(*args)


def _hgrn_body(q_ref, f_ref, i_ref, g_ref, lb_ref, nw_ref, o_ref, st_scr, *, rows, heads):
    C, SUB = HG_CHUNK, HG_SUB
    nsub = C // SUB

    @pl.when(pl.program_id(2) == 0)
    def _():
        st_scr[...] = jnp.zeros_like(st_scr)

    lbs = [lb_ref[hh] for hh in range(heads)]
    log_lbs = [jnp.log(lb) for lb in lbs]
    log_1mlbs = [jnp.log1p(-lb) for lb in lbs]

    r_i = lax.broadcasted_iota(jnp.int32, (C, C), 0)
    c_i = lax.broadcasted_iota(jnp.int32, (C, C), 1)
    tri = (c_i <= r_i).astype(F32)
    sub_shift = SUB.bit_length() - 1
    diag_mask = (c_i <= r_i) & (jnp.right_shift(r_i, sub_shift) == jnp.right_shift(c_i, sub_shift))
    row_id = lax.broadcasted_iota(jnp.int32, (C, HG_DK), 0)

    pairs = [(j, hh) for j in range(rows // C) for hh in range(heads)]
    tri_b = tri.astype(BF16)
    st = {p: {} for p in pairs}

    for p in pairs:
        j, hh = p
        z = f_ref[hh, pl.ds(j * C, C), :].astype(F32)
        e = jnp.exp(-jnp.abs(z))
        e1 = 1.0 + e
        log_sig = jnp.minimum(z, 0.0) - jnp.log(e1)
        sig_neg = jnp.where(z >= 0, e, 1.0) / e1
        cc = log_1mlbs[hh] + log_sig
        log_f = jnp.maximum(log_lbs[hh], cc) + jnp.log(1.0 + jnp.exp(-jnp.abs(log_lbs[hh] - cc)))
        st[p]["kk"] = (1.0 - lbs[hh]) * sig_neg
        hi = log_f.astype(BF16)
        lo = (log_f - hi.astype(F32)).astype(BF16)
        st[p]["b"] = (jnp.dot(tri_b, hi, preferred_element_type=F32)
                      + jnp.dot(tri_b, lo, preferred_element_type=F32))

    for p in pairs:
        j, hh = p
        q = q_ref[hh, pl.ds(j * C, C), :].astype(F32)
        kk, b = st[p]["kk"], st[p]["b"]
        b_last = b[C - 1:C, :]
        st[p]["dec"] = jnp.exp(b_last)
        st[p]["qe"] = (q * jnp.exp(b)).astype(BF16)
        st[p]["kdec"] = (kk * jnp.exp(b_last - b)).astype(BF16)
        refs = [b[I * SUB:I * SUB + 1, :] for I in range(nsub)]
        refb = jnp.concatenate([jnp.broadcast_to(r, (SUB, HG_DK)) for r in refs], axis=0)
        qd = (q * jnp.exp(b - refb)).astype(BF16)
        kd = (kk * jnp.exp(refb - b)).astype(BF16)
        q_parts, k_parts = [], []
        for J in range(nsub - 1):
            r = refs[J + 1]
            qj = q * jnp.exp(jnp.minimum(b - r, 0.0))
            kj = kk * jnp.exp(jnp.minimum(r - b, 0.0))
            q_parts.append(jnp.where(row_id >= (J + 1) * SUB, qj, 0.0).astype(BF16))
            k_parts.append(jnp.where((row_id >= J * SUB) & (row_id < (J + 1) * SUB), kj, 0.0).astype(BF16))
        st[p]["att_d"] = _dot_nt(qd, kd)
        st[p]["att_o"] = _dot_nt(jnp.concatenate(q_parts, axis=1), jnp.concatenate(k_parts, axis=1))

    for p in pairs:
        j, hh = p
        vb = i_ref[hh, pl.ds(j * C, C), :].astype(BF16)
        att = jnp.where(diag_mask, st[p]["att_d"], 0.0) + st[p]["att_o"]
        st[p]["intra"] = jnp.dot(att.astype(BF16), vb, preferred_element_type=F32)
        st[p]["upd"] = _dot_tn(vb, st[p]["kdec"])

    for hh in range(heads):
        s = st_scr[hh]
        for j in range(rows // C):
            st[(j, hh)]["s_in"] = s.astype(BF16)
            s = s * st[(j, hh)]["dec"] + st[(j, hh)]["upd"]
        st_scr[hh] = s

    for p in pairs:
        j, hh = p
        g = g_ref[hh, pl.ds(j * C, C), :].astype(F32)
        o = _dot_nt(st[p]["qe"], st[p]["s_in"]) + st[p]["intra"]
        y = o * lax.rsqrt(jnp.mean(o * o, axis=-1, keepdims=True) + EPS) * nw_ref[hh]
        o_ref[pl.ds(j * C, C), hh * LANES:(hh + 1) * LANES] = (y * (g * _sigmoid(g))).astype(o_ref.dtype)


def _hgrn(proj_b, proj_f, lb, norm_w, batch, seq):
    T = batch * seq
    rows = min(HG_ROWS, seq)
    nblk = seq // rows
    H = HG_HEADS
    hp = HG_HEADS_PER_STEP

    def slab(off):
        return pl.BlockSpec((hp, rows, LANES), lambda b, h, c, off=off: (off // hp + h, b * nblk + c, 0))

    vec = pl.BlockSpec((hp, 1, LANES), lambda b, h, c: (h, 0, 0))
    return pl.pallas_call(
        functools.partial(_hgrn_body, rows=rows, heads=hp),
        grid=(batch, H // hp, nblk),
        in_specs=[slab(0), slab(0), slab(H), slab(2 * H), vec, vec],
        out_specs=pl.BlockSpec((rows, hp * LANES), lambda b, h, c: (b * nblk + c, h)),
        out_shape=jax.ShapeDtypeStruct((T, H * LANES), BF16),
        scratch_shapes=[pltpu.VMEM((hp, LANES, HG_DK), F32)],
        compiler_params=_cparams(("parallel", "parallel", "arbitrary")),
        name="hgrn2",
    )(proj_b, proj_f, proj_b, proj_b, lb.reshape(H, 1, HG_DK), norm_w.reshape(H, 1, LANES))


def _mlstm_body(q_ref, k_ref, v_ref, og_ref, gt_ref, gb_ref, cwq_ref, cwk_ref, cbq_ref, cbk_ref, nw_ref,
                out_ref, c_scr, n_scr, m_scr, qx_scr, kx_scr, qc_scr, kc_scr, *, rows):
    C = ML_CHUNK
    PAD = SUBLANES

    @pl.when(pl.program_id(2) == 0)
    def _():
        c_scr[...] = jnp.zeros_like(c_scr)
        n_scr[...] = jnp.zeros_like(n_scr)
        m_scr[...] = jnp.zeros_like(m_scr)
        qx_scr[0:PAD, :] = jnp.zeros((PAD, LANES), F32)
        kx_scr[0:PAD, :] = jnp.zeros((PAD, LANES), F32)

    qx_scr[PAD:PAD + rows, :] = q_ref[0].astype(F32)
    kx_scr[PAD:PAD + rows, :] = k_ref[0].astype(F32)
    accq = jnp.zeros((rows, LANES), F32) + cbq_ref[...]
    acck = jnp.zeros((rows, LANES), F32) + cbk_ref[...]
    for j in range(CONV_W):
        off = PAD - (CONV_W - 1) + j
        accq = accq + cwq_ref[j:j + 1, :] * qx_scr[pl.ds(off, rows), :]
        acck = acck + cwk_ref[j:j + 1, :] * kx_scr[pl.ds(off, rows), :]
    qc_scr[...] = accq * _sigmoid(accq) * (ML_DQK ** -0.5)
    kc_scr[...] = acck * _sigmoid(acck)
    qx_scr[0:PAD, :] = qx_scr[rows:rows + PAD, :]
    kx_scr[0:PAD, :] = kx_scr[rows:rows + PAD, :]

    lane = lax.broadcasted_iota(jnp.int32, (C, C), 1)
    sub = lax.broadcasted_iota(jnp.int32, (C, C), 0)
    causal = lane <= sub
    lower = causal.astype(F32)
    upper = (sub <= lane).astype(F32)
    nw = nw_ref[...]
    gbias = gb_ref[0]

    chunks = range(rows // C)
    lower_b, upper_b = lower.astype(BF16), upper.astype(BF16)
    st = [{} for _ in chunks]

    def split(x):
        hi = x.astype(BF16)
        return hi, (x - hi.astype(F32)).astype(BF16)

    for j in chunks:
        gc = gt_ref[0, pl.ds(j * C, C), :] + gbias
        gct = gc.T
        li_col = gc[:, 0:1]
        li_row = gct[0:1, :]
        ch, cl = split(_log_sigmoid(gc[:, ML_HEADS:ML_HEADS + 1]))
        rh, rl = split(_log_sigmoid(gct[ML_HEADS:ML_HEADS + 1, :]))
        g_t = (jnp.dot(lower_b, jnp.broadcast_to(ch, (C, C)), preferred_element_type=F32)
               + jnp.dot(lower_b, jnp.broadcast_to(cl, (C, C)), preferred_element_type=F32))
        g_s = (jnp.dot(jnp.broadcast_to(rh, (C, C)), upper_b, preferred_element_type=F32)
               + jnp.dot(jnp.broadcast_to(rl, (C, C)), upper_b, preferred_element_type=F32))
        st[j].update(g_t=g_t, g_s=g_s, li_col=li_col, li_row=li_row)

    for j in chunks:
        c = st[j]
        g_col = c["g_t"][:, 0:1]
        g_last = c["g_t"][C - 1:C, 0:1]
        dmat = jnp.where(causal, c["g_t"] - c["g_s"] + c["li_row"], -jnp.inf)
        log_ws = g_last - g_col + c["li_col"]
        qf = qc_scr[pl.ds(j * C, C), :]
        kf = kc_scr[pl.ds(j * C, C), :]
        qb = qf.astype(BF16)
        c.update(g_col=g_col, g_last=g_last, dmat=dmat, dmax=jnp.max(dmat, axis=1, keepdims=True),
                 log_ws=log_ws, ws_max=jnp.max(log_ws, axis=0, keepdims=True), qf=qf, kf=kf, qb=qb,
                 qk=_dot_nt(qb, kf.astype(BF16)))

    m = m_scr[:, 0:1]
    for j in chunks:
        c = st[j]
        m_new = jnp.maximum(c["g_last"] + m, c["ws_max"])
        c.update(m_prev=m, m_new=m_new, decay=jnp.exp(c["g_last"] + m - m_new))
        m = m_new
    m_scr[...] = jnp.broadcast_to(m, m_scr.shape)

    for j in chunks:
        c = st[j]
        vb = jnp.concatenate([v_ref[0, pl.ds(j * C, C), :], v_ref[1, pl.ds(j * C, C), :]], axis=1).astype(BF16)
        kw = c["kf"] * jnp.exp(c["log_ws"] - c["m_new"])
        c.update(vb=vb, upd=_dot_tn(kw.astype(BF16), vb), ksum=jnp.sum(kw, axis=0, keepdims=True))

    cm = c_scr[...]
    nv = n_scr[...]
    for j in chunks:
        c = st[j]
        c.update(c_in=cm.astype(BF16), n_in=nv)
        cm = c["decay"] * cm + c["upd"]
        nv = c["decay"] * nv + c["ksum"]
    c_scr[...] = cm
    n_scr[...] = nv

    for j in chunks:
        c = st[j]
        a_inter = c["g_col"] + c["m_prev"]
        m_t = jnp.maximum(a_inter, c["dmax"])
        w_inter = jnp.exp(a_inter - m_t)
        sqk = c["qk"] * jnp.exp(c["dmat"] - m_t)
        num = (w_inter * jnp.dot(c["qb"], c["c_in"], preferred_element_type=F32)
               + jnp.dot(sqk.astype(BF16), c["vb"], preferred_element_type=F32))
        den = (w_inter * jnp.sum(c["qf"] * c["n_in"], axis=1, keepdims=True)
               + jnp.sum(sqk, axis=1, keepdims=True))
        hh = num * (1.0 / jnp.maximum(jnp.abs(den), jnp.exp(-m_t)))
        y = hh * lax.rsqrt(jnp.mean(hh * hh, axis=-1, keepdims=True) + EPS) * nw
        og = jnp.concatenate([og_ref[0, pl.ds(j * C, C), :], og_ref[1, pl.ds(j * C, C), :]], axis=1).astype(F32)
        out_ref[pl.ds(j * C, C), :] = (y * _sigmoid(og)).astype(out_ref.dtype)


def _mlstm(proj3, gates, gate_bias_pad, conv_w, conv_b, norm_w, batch, seq, q_off):
    T = batch * seq
    rows = min(ML_ROWS, seq)
    nblk = seq // rows
    H = ML_HEADS
    k_off = q_off + H
    v_off = k_off + H
    o_off = v_off + 2 * H

    def slab(off):
        return pl.BlockSpec((1, rows, LANES), lambda b, h, c, off=off: (off + h, b * nblk + c, 0))

    def slab2(off):
        return pl.BlockSpec((2, rows, LANES), lambda b, h, c, off=off: (off // 2 + h, b * nblk + c, 0))

    qk_w = H * ML_DQK
    return pl.pallas_call(
        functools.partial(_mlstm_body, rows=rows),
        grid=(batch, H, nblk),
        in_specs=[
            slab(q_off), slab(k_off), slab2(v_off), slab2(o_off),
            pl.BlockSpec((1, rows, LANES), lambda b, h, c: (h, b * nblk + c, 0)),
            pl.BlockSpec((1, 1, LANES), lambda b, h, c: (h, 0, 0)),
            pl.BlockSpec((CONV_W, LANES), lambda b, h, c: (0, h)),
            pl.BlockSpec((CONV_W, LANES), lambda b, h, c: (0, H + h)),
            pl.BlockSpec((1, LANES), lambda b, h, c: (0, h)),
            pl.BlockSpec((1, LANES), lambda b, h, c: (0, H + h)),
            pl.BlockSpec((1, ML_DV), lambda b, h, c: (0, h)),
        ],
        out_specs=pl.BlockSpec((rows, ML_DV), lambda b, h, c: (b * nblk + c, h)),
        out_shape=jax.ShapeDtypeStruct((T, H * ML_DV), BF16),
        scratch_shapes=[
            pltpu.VMEM((ML_DQK, ML_DV), F32),
            pltpu.VMEM((1, ML_DQK), F32),
            pltpu.VMEM((1, LANES), F32),
            pltpu.VMEM((rows + 2 * SUBLANES, LANES), F32),
            pltpu.VMEM((rows + 2 * SUBLANES, LANES), F32),
            pltpu.VMEM((rows, LANES), F32),
            pltpu.VMEM((rows, LANES), F32),
        ],
        compiler_params=_cparams(("parallel", "parallel", "arbitrary")),
        name="mlstm",
    )(proj3, proj3, proj3, proj3, gates, gate_bias_pad, conv_w, conv_w,
      conv_b.reshape(1, 2 * qk_w), conv_b.reshape(1, 2 * qk_w), norm_w.reshape(1, H * ML_DV))


def _outproj_body(a_ref, b_ref, x_ref, wo_ref, ln_ref, wrh_ref, wrl_ref, br_ref,
                  x2_ref, h2_ref, idx_ref, gate_ref, rank_ref, cnt_ref, cnt_scr, *, sub_rows):
    tm = x_ref.shape[0]
    ka = a_ref.shape[1]
    s = _packed_rows(x_ref.shape[1])

    @pl.when(pl.program_id(0) == 0)
    def _():
        cnt_scr[...] = jnp.zeros_like(cnt_scr)

    lane = lax.broadcasted_iota(jnp.int32, (sub_rows, LANES), 1).astype(F32)
    onehots = [[] for _ in range(TOP_K)]
    for r0 in range(0, tm, sub_rows):
        rows = pl.ds(r0, sub_rows)
        res = (jnp.dot(a_ref[rows, :], wo_ref[0:ka, :], preferred_element_type=F32)
               + jnp.dot(b_ref[rows, :], wo_ref[ka:, :], preferred_element_type=F32))
        x2 = x_ref[rows, :] + res
        x2_ref[rows, :] = x2
        h2 = x2 * lax.rsqrt(jnp.mean(x2 * x2, axis=-1, keepdims=True) + EPS) * ln_ref[...]
        h_hi = h2.astype(BF16)
        h_hi32 = h_hi.astype(F32)
        _pack_store(h2_ref.at[pl.ds(r0 * s, sub_rows * s), :], h_hi32, rounded=True)

        h_lo = (h2 - h_hi32).astype(BF16)
        logits = (jnp.dot(h_hi, wrh_ref[...], preferred_element_type=F32)
                  + jnp.dot(h_lo, wrh_ref[...], preferred_element_type=F32)
                  + jnp.dot(h_hi, wrl_ref[...], preferred_element_type=F32)) + br_ref[...]
        vals, idxs = [], []
        cur = logits
        for _ in range(TOP_K):
            m = jnp.max(cur, axis=1, keepdims=True)
            ix = jnp.min(jnp.where(cur == m, lane, float(LANES)), axis=1, keepdims=True)
            vals.append(m)
            idxs.append(ix)
            cur = jnp.where(lane == ix, -jnp.inf, cur)
        es = [jnp.exp(v - vals[0]) for v in vals]
        inv = 1.0 / (es[0] + es[1] + es[2] + es[3])
        gate = jnp.zeros(logits.shape, F32)
        idx = jnp.zeros(logits.shape, F32)
        for k in range(TOP_K):
            gate = jnp.where(lane == float(k), es[k] * inv, gate)
            idx = jnp.where(lane == float(k), idxs[k], idx)
            onehots[k].append((lane == idxs[k]).astype(F32))
        gate_ref[rows, :] = gate
        idx_ref[rows, :] = idx.astype(jnp.int32)

    oh_k = [jnp.concatenate(o, axis=0) for o in onehots]
    oh = oh_k[0] + oh_k[1] + oh_k[2] + oh_k[3]
    r_i = lax.broadcasted_iota(jnp.int32, (tm, tm), 0)
    c_i = lax.broadcasted_iota(jnp.int32, (tm, tm), 1)
    before = jnp.dot((c_i < r_i).astype(BF16), oh.astype(BF16), preferred_element_type=F32) + cnt_scr[...]
    lane_t = lax.broadcasted_iota(jnp.int32, (tm, LANES), 1)
    rank = jnp.zeros((tm, LANES), F32)
    for k in range(TOP_K):
        rank = jnp.where(lane_t == k, jnp.sum(oh_k[k] * before, axis=1, keepdims=True), rank)
    rank_ref[...] = rank.astype(jnp.int32)
    cnt = cnt_scr[...] + jnp.sum(oh, axis=0, keepdims=True)
    cnt_scr[...] = cnt
    cnt_ref[...] = cnt.astype(jnp.int32)


def _outproj(a_out, b_out, x2d, w_out_bf, ln_w, wr_hi, wr_lo, b_router_pad):
    T, D = x2d.shape
    tm = min(OUTPROJ_TM, T)
    ka, kb = a_out.shape[1], b_out.shape[1]
    s = _packed_rows(D)
    row = lambda w: pl.BlockSpec((tm, w), lambda i: (i, 0))
    full = lambda r, c: pl.BlockSpec((r, c), lambda i: (0, 0))
    return pl.pallas_call(
        functools.partial(_outproj_body, sub_rows=min(OUTPROJ_SUB, tm)),
        grid=(T // tm,),
        in_specs=[row(ka), row(kb), row(D), full(ka + kb, D), full(1, D), full(D, LANES), full(D, LANES),
                  full(1, LANES)],
        out_specs=[row(D), pl.BlockSpec((tm * s, LANES), lambda i: (i, 0)), row(LANES), row(LANES), row(LANES),
                   full(1, LANES)],
        out_shape=[
            jax.ShapeDtypeStruct((T, D), F32),
            jax.ShapeDtypeStruct((T * s, LANES), jnp.uint32),
            jax.ShapeDtypeStruct((T, LANES), jnp.int32),
            jax.ShapeDtypeStruct((T, LANES), F32),
            jax.ShapeDtypeStruct((T, LANES), jnp.int32),
            jax.ShapeDtypeStruct((1, LANES), jnp.int32),
        ],
        scratch_shapes=[pltpu.VMEM((1, LANES), F32)],
        compiler_params=_cparams(("arbitrary",)),
        name="outproj_router",
    )(a_out, b_out, x2d, w_out_bf, ln_w, wr_hi, wr_lo, b_router_pad)


def _start_row_gather(idx_at, src_ref, buf, sem, n, s):
    def start(pair, carry):
        for p in range(DMA_QUEUES):
            i = pair * DMA_QUEUES + p
            src_row = pl.multiple_of(idx_at(i) * s, s)
            dst_row = pl.multiple_of(i * s, s)
            pltpu.make_async_copy(src_ref.at[pl.ds(src_row, s), :], buf.at[pl.ds(dst_row, s), :],
                                  sem).start(priority=p)
        return carry

    lax.fori_loop(0, n // DMA_QUEUES, start, 0, unroll=DMA_UNROLL // DMA_QUEUES)


def _wait_row_gather(buf, sem):
    pltpu.make_async_copy(buf, buf, sem).wait()


def _dispatch_body(zero_ref, dest_ref, h_ref, xs_ref, zbuf, sem, zsem, *, ntok, s, tm, n_blocks):
    blk = tm * s

    def zero_copy(b):
        return pltpu.make_async_copy(zbuf, xs_ref.at[pl.ds(pl.multiple_of(b * blk, blk), blk), :], zsem)

    @pl.when(pl.program_id(0) == 0)
    def _():
        zbuf[...] = jnp.zeros_like(zbuf)

        def zstart(b, carry):
            @pl.when(zero_ref[b] == 1)
            def _():
                zero_copy(b).start()
            return carry

        def zwait(b, carry):
            @pl.when(zero_ref[b] == 1)
            def _():
                zero_copy(b).wait()
            return carry

        lax.fori_loop(0, n_blocks, zstart, 0)
        lax.fori_loop(0, n_blocks, zwait, 0)

    def start(t, carry):
        src = h_ref.at[pl.ds(pl.multiple_of(t * s, s), s), :]
        for k in range(TOP_K):
            dst_row = pl.multiple_of(dest_ref[0, 0, t * TOP_K + k] * s, s)
            pltpu.make_async_copy(src, xs_ref.at[pl.ds(dst_row, s), :], sem).start(priority=k % DMA_QUEUES)
        return carry

    lax.fori_loop(0, ntok, start, 0, unroll=DMA_UNROLL // TOP_K)
    for _ in range(TOP_K):
        pltpu.make_async_copy(h_ref, h_ref, sem).wait()


def _dispatch(dest, zero_blk, h_packed, d, n_slots, tm):
    s = _packed_rows(d)
    T = h_packed.shape[0] // s
    ntok = min(DISPATCH_TOKENS, T)
    n = ntok * TOP_K
    grid_spec = pltpu.PrefetchScalarGridSpec(
        num_scalar_prefetch=1,
        grid=(T // ntok,),
        in_specs=[
            pl.BlockSpec((1, 1, n), lambda i, z: (i, 0, 0), memory_space=pltpu.SMEM),
            pl.BlockSpec((ntok * s, LANES), lambda i, z: (i, 0)),
        ],
        out_specs=pl.BlockSpec(memory_space=pl.ANY),
        scratch_shapes=[pltpu.VMEM((tm * s, LANES), jnp.uint32), pltpu.SemaphoreType.DMA,
                        pltpu.SemaphoreType.DMA],
    )
    return pl.pallas_call(
        functools.partial(_dispatch_body, ntok=ntok, s=s, tm=tm, n_blocks=zero_blk.shape[0]),
        grid_spec=grid_spec,
        out_shape=jax.ShapeDtypeStruct((n_slots * s, LANES), jnp.uint32),
        compiler_params=_cparams(("arbitrary",)),
        name="moe_dispatch",
    )(zero_blk, dest.reshape(T // ntok, 1, n), h_packed)


def _staged_weights(sched_refs, hbm_refs, stage_refs, bf_refs, sem):
    e_ref, wt_ref, first_ref, ne_ref, nt_ref, more_ref = sched_refs
    w = pl.program_id(0)
    tn = stage_refs[0].shape[1]

    def copies(e, t):
        col = pl.multiple_of(t * tn, tn)
        return [pltpu.make_async_copy(h.at[e, :, pl.ds(col, tn)], st, sem.at[k])
                for k, (h, st) in enumerate(zip(hbm_refs, stage_refs))]

    def start_all(cs):
        for c in cs:
            c.start()

    @pl.when(w == 0)
    def _():
        start_all(copies(e_ref[0], wt_ref[0]))

    @pl.when(first_ref[w] == 1)
    def _():
        for c in copies(e_ref[w], wt_ref[w]):
            c.wait()
        for st, bf in zip(stage_refs, bf_refs):
            for r in range(0, st.shape[0], 256):
                bf[r:r + 256, :] = st[r:r + 256, :].astype(BF16)

        @pl.when(more_ref[w] == 1)
        def _():
            start_all(copies(ne_ref[w], nt_ref[w]))


def _row_groups(nrows, tm, compute, clear):
    sub = min(MOE_SUB_ROWS, tm)
    groups = tm // sub
    for live in range(groups + 1):
        @pl.when((nrows > (live - 1) * sub) & (nrows <= live * sub))
        def _(live=live):
            for g in range(live):
                compute(g * sub, sub)
            for g in range(live, groups):
                clear(g * sub, sub)


def _moe_up_body(e_ref, wt_ref, r_ref, ot_ref, first_ref, nrows_ref, ne_ref, nt_ref, more_ref,
                 x_ref, wg_ref, wu_ref, bg_ref, bu_ref, o_ref, wg_stage, wu_stage, wg_scr, wu_scr, x_scr, sem):
    w = pl.program_id(0)
    _staged_weights((e_ref, wt_ref, first_ref, ne_ref, nt_ref, more_ref), (wg_ref, wu_ref),
                    (wg_stage, wu_stage), (wg_scr, wu_scr), sem)
    tm, d = x_scr.shape
    s = _packed_rows(d)

    def compute(r0, n):
        rows = pl.ds(r0, n)
        for c in range(s):
            hi, lo = _unpack_load(x_ref, r0, n, s, c)
            x_scr[rows, c * LANES:(c + 1) * LANES] = hi.astype(BF16)
            x_scr[rows, d // 2 + c * LANES:d // 2 + (c + 1) * LANES] = lo.astype(BF16)
        x = x_scr[rows, :]
        gt = jnp.dot(x, wg_scr[...], preferred_element_type=F32) + bg_ref[0]
        up = jnp.dot(x, wu_scr[...], preferred_element_type=F32) + bu_ref[0]
        gt = jnp.minimum(gt, SWIGLU_LIMIT)
        up = jnp.clip(up, -SWIGLU_LIMIT, SWIGLU_LIMIT)
        swish = (0.5 * gt) * (1.0 + jnp.tanh((0.5 * SWIGLU_ALPHA) * gt))
        o_ref[rows, :] = ((up + 1.0) * swish).astype(o_ref.dtype)

    def clear(r0, n):
        o_ref[pl.ds(r0, n), :] = jnp.zeros((n, o_ref.shape[1]), o_ref.dtype)

    _row_groups(nrows_ref[w], tm, compute, clear)


def _moe_down_body(e_ref, wt_ref, r_ref, ot_ref, first_ref, nrows_ref, ne_ref, nt_ref, more_ref,
                   a_ref, wd_ref, bd_ref, o_ref, wd_stage, wd_scr, sem):
    w = pl.program_id(0)
    _staged_weights((e_ref, wt_ref, first_ref, ne_ref, nt_ref, more_ref), (wd_ref,), (wd_stage,), (wd_scr,), sem)
    tm = a_ref.shape[0]
    s = o_ref.shape[0] // tm

    def compute(r0, n):
        y = jnp.dot(a_ref[pl.ds(r0, n), :], wd_scr[...], preferred_element_type=F32) + bd_ref[0]
        _pack_store(o_ref.at[pl.ds(r0 * s, n * s), :], y)

    def clear(r0, n):
        o_ref[pl.ds(r0 * s, n * s), :] = jnp.zeros((n * s, LANES), o_ref.dtype)

    _row_groups(nrows_ref[w], tm, compute, clear)


def _moe_schedule(counts, tm, n_tiles, n_blocks):
    n_items = n_tiles * n_blocks
    experts = jnp.arange(N_EXPERTS, dtype=jnp.int32)
    blocks_e = (counts + tm - 1) // tm
    bend = jnp.cumsum(blocks_e)
    bstart = bend - blocks_e
    item_end = n_tiles * bend
    total = item_end[-1]
    later = (experts[None, :] > experts[:, None]) & (blocks_e[None, :] > 0)
    next_e = jnp.min(jnp.where(later, experts[None, :], N_EXPERTS - 1), axis=1)

    w = jnp.arange(n_items, dtype=jnp.int32)
    valid = w < total
    wc = jnp.minimum(w, jnp.maximum(total - 1, 0))
    e = jnp.minimum(jnp.sum((item_end[None, :] <= wc[:, None]).astype(jnp.int32), axis=1), N_EXPERTS - 1)
    sel = e[:, None] == experts[None, :]
    pick = lambda table: jnp.sum(jnp.where(sel, table[None, :], 0), axis=1)
    nb = jnp.maximum(pick(blocks_e), 1)
    local = wc - n_tiles * pick(bstart)
    wtile = sum((local >= t * nb).astype(jnp.int32) for t in range(1, n_tiles)) if n_tiles > 1 else 0 * local
    jblk = local - wtile * nb
    spare = jnp.maximum(w - total, 0)
    rblk = jnp.where(valid, pick(bstart) + jblk, bend[-1] + spare // n_tiles)
    otile = jnp.where(valid, wtile, spare % n_tiles)
    nrows = jnp.where(valid, jnp.clip(pick(counts) - jblk * tm, 0, tm), 0)
    first = (jblk == 0) & valid
    last_tile = wtile == n_tiles - 1
    more = first & (w + nb < total)
    i32 = lambda a: a.astype(jnp.int32)
    return (e, i32(wtile), i32(rblk), i32(otile), i32(first), i32(nrows),
            i32(jnp.where(last_tile, pick(next_e), e)), i32(jnp.where(last_tile, 0, wtile + 1)), i32(more))


def _moe_up(sched, xs_packed, w_gate, w_up, b_gate, b_up):
    D, d_ff = w_gate.shape[1], w_gate.shape[2]
    s = _packed_rows(D)
    n_slots = xs_packed.shape[0] // s
    tm, tf = MOE_TM, min(MOE_TF, d_ff)
    n_items = sched[0].shape[0]
    wspec = pl.BlockSpec(memory_space=pl.ANY)
    bspec = pl.BlockSpec((1, 1, tf), lambda w, e, wt, r, ot, *_: (e[w], 0, wt[w]))
    grid_spec = pltpu.PrefetchScalarGridSpec(
        num_scalar_prefetch=len(sched),
        grid=(n_items,),
        in_specs=[pl.BlockSpec((tm * s, LANES), lambda w, e, wt, r, ot, *_: (r[w], 0)),
                  wspec, wspec, bspec, bspec],
        out_specs=pl.BlockSpec((tm, tf), lambda w, e, wt, r, ot, *_: (r[w], ot[w])),
        scratch_shapes=[pltpu.VMEM((D, tf), F32), pltpu.VMEM((D, tf), F32),
                        pltpu.VMEM((D, tf), BF16), pltpu.VMEM((D, tf), BF16), pltpu.VMEM((tm, D), BF16),
                        pltpu.SemaphoreType.DMA((2,))],
    )
    return pl.pallas_call(
        _moe_up_body,
        grid_spec=grid_spec,
        out_shape=jax.ShapeDtypeStruct((n_slots, d_ff), BF16),
        compiler_params=_cparams(("arbitrary",)),
        name="moe_up",
    )(*sched, xs_packed, w_gate, w_up, b_gate.reshape(N_EXPERTS, 1, d_ff), b_up.reshape(N_EXPERTS, 1, d_ff))


def _moe_down(sched, act, w_down, b_down):
    n_slots, d_ff = act.shape
    D = w_down.shape[2]
    tm = MOE_TM
    s = _packed_rows(D)
    n_items = sched[0].shape[0]
    grid_spec = pltpu.PrefetchScalarGridSpec(
        num_scalar_prefetch=len(sched),
        grid=(n_items,),
        in_specs=[
            pl.BlockSpec((tm, d_ff), lambda w, e, wt, r, ot, *_: (r[w], 0)),
            pl.BlockSpec(memory_space=pl.ANY),
            pl.BlockSpec((1, 1, D), lambda w, e, wt, r, ot, *_: (e[w], 0, 0)),
        ],
        out_specs=pl.BlockSpec((tm * s, LANES), lambda w, e, wt, r, ot, *_: (r[w], 0)),
        scratch_shapes=[pltpu.VMEM((d_ff, D), F32), pltpu.VMEM((d_ff, D), BF16), pltpu.SemaphoreType.DMA((1,))],
    )
    return pl.pallas_call(
        _moe_down_body,
        grid_spec=grid_spec,
        out_shape=jax.ShapeDtypeStruct((n_slots * s, LANES), jnp.uint32),
        compiler_params=_cparams(("arbitrary",)),
        name="moe_down",
    )(*sched, act, w_down, b_down.reshape(N_EXPERTS, 1, D))


def _final_body(dest_ref, next_ref, x2_ref, gate_ref, w_ref, ys_ref, o_ref, buf0, buf1, sem0, sem1, *, tm, s):
    n = tm * TOP_K
    i = pl.program_id(0)

    def gather(idx_ref, tile, buf, sem):
        _start_row_gather(lambda r: idx_ref[0, 0, tile * n + r], ys_ref, buf, sem, n, s)

    def combine(tile, buf):
        rows = pl.ds(tile * tm, tm)
        gate = gate_ref[rows, :]
        half = o_ref.shape[1] // 2
        gates = [jnp.broadcast_to(gate[:, k:k + 1], (tm, LANES)) for k in range(TOP_K)]
        for c in range(s):
            acc_hi = x2_ref[rows, c * LANES:(c + 1) * LANES]
            acc_lo = x2_ref[rows, half + c * LANES:half + (c + 1) * LANES]
            for k in range(TOP_K):
                hi, lo = _unpack_load(buf, k * tm, tm, s, c)
                acc_hi = acc_hi + gates[k] * hi
                acc_lo = acc_lo + gates[k] * lo
            o_ref[rows, c * LANES:(c + 1) * LANES] = acc_hi
            o_ref[rows, half + c * LANES:half + (c + 1) * LANES] = acc_lo
        acc = o_ref[rows, :]
        o_ref[rows, :] = acc * lax.rsqrt(jnp.mean(acc * acc, axis=-1, keepdims=True) + EPS) * w_ref[...]

    @pl.when(i == 0)
    def _():
        gather(dest_ref, 0, buf0, sem0)

    gather(dest_ref, 1, buf1, sem1)
    _wait_row_gather(buf0, sem0)
    combine(0, buf0)
    gather(next_ref, 0, buf0, sem0)
    _wait_row_gather(buf1, sem1)
    combine(1, buf1)

    @pl.when(i == pl.num_programs(0) - 1)
    def _():
        _wait_row_gather(buf0, sem0)


def _final(x2, ys_packed, dest, gates_pad, w):
    T, D = x2.shape
    tm = min(FINAL_TM, T // 2)
    s = _packed_rows(D)
    n = tm * TOP_K
    steps = T // (2 * tm)
    dest_km = dest.reshape(steps, 2, tm, TOP_K).transpose(0, 1, 3, 2).reshape(steps, 1, 2 * n)
    buf = pltpu.VMEM((n * s, LANES), jnp.uint32)
    return pl.pallas_call(
        functools.partial(_final_body, tm=tm, s=s),
        grid=(steps,),
        in_specs=[
            pl.BlockSpec((1, 1, 2 * n), lambda i: (i, 0, 0), memory_space=pltpu.SMEM),
            pl.BlockSpec((1, 1, 2 * n), lambda i: (jnp.minimum(i + 1, steps - 1), 0, 0), memory_space=pltpu.SMEM),
            pl.BlockSpec((2 * tm, D), lambda i: (i, 0)),
            pl.BlockSpec((2 * tm, LANES), lambda i: (i, 0)),
            pl.BlockSpec((1, D), lambda i: (0, 0)),
            pl.BlockSpec(memory_space=pl.ANY),
        ],
        out_specs=pl.BlockSpec((2 * tm, D), lambda i: (i, 0)),
        out_shape=jax.ShapeDtypeStruct((T, D), F32),
        scratch_shapes=[buf, buf, pltpu.SemaphoreType.DMA, pltpu.SemaphoreType.DMA],
        compiler_params=_cparams(("arbitrary",)),
        name="final_norm",
    )(dest_km, dest_km, x2, gates_pad, w, ys_packed)


def _moe(h2_packed, T, D, top_idx, rank, counts, w_gate, b_gate, w_up, b_up, w_down, b_down):
    A = T * TOP_K
    tm = MOE_TM
    n_blocks = (A + N_EXPERTS * (tm - 1) + tm - 1) // tm
    n_slots = n_blocks * tm

    blocks_e = (counts + tm - 1) // tm
    bend = jnp.cumsum(blocks_e)
    bstart = bend - blocks_e
    experts = jnp.arange(N_EXPERTS, dtype=jnp.int32)
    first_slot = jnp.sum(jnp.where(top_idx[:, :, None] == experts, bstart * tm, 0), axis=-1)
    dest = (first_slot + rank).astype(jnp.int32).reshape(A)
    blk = jnp.arange(n_blocks, dtype=jnp.int32)
    is_last = jnp.any((blk[:, None] == bend[None, :] - 1) & (blocks_e[None, :] > 0), axis=1)
    zero_blk = ((blk >= bend[-1]) | is_last).astype(jnp.int32)

    xs = _dispatch(dest, zero_blk, h2_packed, D, n_slots, tm)
    d_ff = w_gate.shape[2]
    act = _moe_up(_moe_schedule(counts, tm, -(-d_ff // MOE_TF), n_blocks), xs, w_gate, w_up, b_gate, b_up)
    ys = _moe_down(_moe_schedule(counts, tm, 1, n_blocks), act, w_down, b_down)
    return ys, dest


def kernel(x, ln1_w, w_in, hg_lb_logits, hg_norm_w, ml_conv_w, ml_conv_b, ml_igate_b, ml_fgate_b, ml_norm_w,
           w_out, ln2_w, w_router, b_router, w_gate, b_gate, w_up, b_up, w_down, b_down, final_norm_w):
    B, S, D = x.shape
    T = B * S
    depth = w_in.shape[0]
    hg_w = HG_HEADS * HG_DK
    n_main = 4 * hg_w + 2 * ML_HEADS * ML_DQK + 2 * ML_HEADS * ML_DV
    lb_all = jnp.cumsum(jax.nn.softmax(hg_lb_logits.astype(F32), axis=0), axis=0)

    xc = x.reshape(T, D)
    for l in range(depth):
        w_gates_pad = jnp.pad(w_in[l][:, n_main:], ((0, 0), (0, LANES - 2 * ML_HEADS))).astype(BF16)
        w_bf = w_in[l].astype(BF16)
        ln1 = ln1_w[l].reshape(1, D)
        hf_tiles = hg_w // INPROJ_TN
        proj_f = _inproj(xc, ln1, w_bf, hf_tiles, lambda j: j + hf_tiles, F32)[0]
        proj_b, gates = _inproj(xc, ln1, w_bf, n_main // INPROJ_TN - hf_tiles,
                                lambda j: jnp.where(j >= hf_tiles, j + hf_tiles, j), BF16, w_gates_pad)
        a_out = _hgrn(proj_b, proj_f, lb_all[l], hg_norm_w[l], B, S)
        gate_bias = jnp.pad(jnp.concatenate([ml_igate_b[l], ml_fgate_b[l]]), (0, LANES - 2 * ML_HEADS))
        gates_h = jnp.stack([jnp.roll(gates, -hd, axis=1) for hd in range(ML_HEADS)])
        bias_h = jnp.stack([jnp.roll(gate_bias, -hd) for hd in range(ML_HEADS)]).reshape(ML_HEADS, 1, LANES)
        b_out = _mlstm(proj_b, gates_h, bias_h, ml_conv_w[l], ml_conv_b[l], ml_norm_w[l], B, S, 3 * HG_HEADS)
        wr_pad = jnp.pad(w_router[l], ((0, 0), (0, LANES - N_EXPERTS)))
        wr_hi = wr_pad.astype(BF16)
        wr_lo = (wr_pad - wr_hi.astype(F32)).astype(BF16)
        br_pad = jnp.pad(b_router[l], (0, LANES - N_EXPERTS), constant_values=-1e30).reshape(1, LANES)
        x2, h2, idx_pad, gates_pad, rank_pad, cnt = _outproj(
            a_out, b_out, xc, w_out[l].astype(BF16), ln2_w[l].reshape(1, D), wr_hi, wr_lo, br_pad)
        ys, dest = _moe(h2, T, D, idx_pad[:, :TOP_K], rank_pad[:, :TOP_K], cnt[0, :N_EXPERTS],
                        w_gate[l], b_gate[l], w_up[l], b_up[l], w_down[l], b_down[l])
        if l + 1 < depth:
            raise NotImplementedError("only the final layer fuses the output norm")
        xc = _final(x2, ys, dest, gates_pad, final_norm_w.reshape(1, D))
    return xc.reshape(B, S, D)
```

```python
import functools

import jax
import jax.numpy as jnp
from jax import lax
from jax.experimental import pallas as pl
from jax.experimental.pallas import tpu as pltpu

F32 = jnp.float32
BF16 = jnp.bfloat16

EPS = 1e-6
HG_HEADS = 8
HG_DK = 128
ML_HEADS = 4
ML_DQK = 128
ML_DV = 256
CONV_W = 4
N_EXPERTS = 32
TOP_K = 4
SWIGLU_ALPHA = 1.702
SWIGLU_LIMIT = 7.0

LANES = 128
SUBLANES = 8
VMEM_LIMIT_BYTES = 56 * 1024 * 1024

HG_CHUNK = 64
HG_SUB = 16
ML_CHUNK = 128
HG_ROWS = 1024
ML_ROWS = 512
HG_HEADS_PER_STEP = 2

INPROJ_TM = 1024
INPROJ_TN = 1024
OUTPROJ_TM = 512
OUTPROJ_SUB = 256
DISPATCH_TOKENS = 512
MOE_TM = 512
MOE_TF = 1024
MOE_SUB_ROWS = 128
DMA_UNROLL = 32
DISPATCH_UNROLL = 64
DMA_QUEUES = 2
FINAL_TM = 128


def _dot_nt(a, b):
    return lax.dot_general(a, b, (((1,), (1,)), ((), ())), preferred_element_type=F32)


def _dot_tn(a, b):
    return lax.dot_general(a, b, (((0,), (0,)), ((), ())), preferred_element_type=F32)


def _log_sigmoid(z):
    return jnp.minimum(z, 0.0) - jnp.log(1.0 + jnp.exp(-jnp.abs(z)))


def _sigmoid(z):
    return 0.5 * jnp.tanh(0.5 * z) + 0.5


def _cparams(semantics):
    return pltpu.CompilerParams(dimension_semantics=semantics, vmem_limit_bytes=VMEM_LIMIT_BYTES)


_HI_MASK = 0xFFFF0000


def _packed_rows(d):
    return d // (2 * LANES)


def _pack_store(o_ref, v, rounded=False):
    n, d = v.shape
    s, half = _packed_rows(d), d // 2
    bits = pltpu.bitcast(v if rounded else v.astype(BF16).astype(F32), jnp.uint32)
    for c in range(s):
        hi = bits[:, c * LANES:(c + 1) * LANES]
        lo = bits[:, half + c * LANES:half + (c + 1) * LANES]
        o_ref[pl.ds(c, n, stride=s), :] = hi | jnp.right_shift(lo, jnp.uint32(16))


def _unpack_load(buf, first_row, n, s, c):
    w = buf[pl.ds(first_row * s + c, n, stride=s), :]
    hi = pltpu.bitcast(w & jnp.uint32(_HI_MASK), F32)
    lo = pltpu.bitcast(jnp.left_shift(w, jnp.uint32(16)), F32)
    return hi, lo


def _inproj_body(*refs, tn, with_gates):
    if with_gates:
        x_ref, lnw_ref, w_ref, wg_ref, o_ref, g_ref, h_scr = refs
    else:
        x_ref, lnw_ref, w_ref, o_ref, h_scr = refs

    @pl.when(pl.program_id(1) == 0)
    def _():
        x = x_ref[...]
        h = x * lax.rsqrt(jnp.mean(x * x, axis=-1, keepdims=True) + EPS) * lnw_ref[...]
        hb = h.astype(BF16)
        h_scr[...] = hb
        if with_gates:
            g_ref[...] = jnp.dot(hb, wg_ref[...], preferred_element_type=F32)

    res = jnp.dot(h_scr[...], w_ref[...], preferred_element_type=F32)
    for c in range(tn // LANES):
        o_ref[c] = res[:, c * LANES:(c + 1) * LANES].astype(o_ref.dtype)


def _inproj(x2d, ln_w, w_bf, n_tiles, col_tile, out_dtype, w_gates_pad=None):
    T, D = x2d.shape
    tm = min(INPROJ_TM, T)
    tn = INPROJ_TN
    with_gates = w_gates_pad is not None
    in_specs = [
        pl.BlockSpec((tm, D), lambda i, j: (i, 0)),
        pl.BlockSpec((1, D), lambda i, j: (0, 0)),
        pl.BlockSpec((D, tn), lambda i, j: (0, col_tile(j))),
    ]
    out_specs = [pl.BlockSpec((tn // LANES, tm, LANES), lambda i, j: (j, i, 0))]
    out_shape = [jax.ShapeDtypeStruct((n_tiles * tn // LANES, T, LANES), out_dtype)]
    args = [x2d, ln_w, w_bf]
    if with_gates:
        in_specs.append(pl.BlockSpec((D, LANES), lambda i, j: (0, 0)))
        out_specs.append(pl.BlockSpec((tm, LANES), lambda i, j: (i, 0)))
        out_shape.append(jax.ShapeDtypeStruct((T, LANES), F32))
        args.append(w_gates_pad)
    return pl.pallas_call(
        functools.partial(_inproj_body, tn=tn, with_gates=with_gates),
        grid=(T // tm, n_tiles),
        in_specs=in_specs,
        out_specs=out_specs,
        out_shape=out_shape,
        scratch_shapes=[pltpu.VMEM((tm, D), BF16)],
        compiler_params=_cparams(("parallel", "arbitrary")),
        name="inproj_gates" if with_gates else "inproj",
    )(*args)


def _hgrn_body(q_ref, f_ref, i_ref, g_ref, lb_ref, nw_ref, o_ref, st_scr, *, rows, heads):
    C, SUB = HG_CHUNK, HG_SUB
    nsub = C // SUB

    @pl.when(pl.program_id(2) == 0)
    def _():
        st_scr[...] = jnp.zeros_like(st_scr)

    lbs = [lb_ref[hh] for hh in range(heads)]
    log_lbs = [jnp.log(lb) for lb in lbs]
    log_1mlbs = [jnp.log1p(-lb) for lb in lbs]

    r_i = lax.broadcasted_iota(jnp.int32, (C, C), 0)
    c_i = lax.broadcasted_iota(jnp.int32, (C, C), 1)
    tri = (c_i <= r_i).astype(F32)
    sub_shift = SUB.bit_length() - 1
    diag_mask = (c_i <= r_i) & (jnp.right_shift(r_i, sub_shift) == jnp.right_shift(c_i, sub_shift))
    row_id = lax.broadcasted_iota(jnp.int32, (C, HG_DK), 0)

    pairs = [(j, hh) for j in range(rows // C) for hh in range(heads)]
    tri_b = tri.astype(BF16)
    st = {p: {} for p in pairs}

    for p in pairs:
        j, hh = p
        z = f_ref[hh, pl.ds(j * C, C), :].astype(F32)
        e = jnp.exp(-jnp.abs(z))
        e1 = 1.0 + e
        log_sig = jnp.minimum(z, 0.0) - jnp.log(e1)
        sig_neg = jnp.where(z >= 0, e, 1.0) / e1
        cc = log_1mlbs[hh] + log_sig
        log_f = jnp.maximum(log_lbs[hh], cc) + jnp.log(1.0 + jnp.exp(-jnp.abs(log_lbs[hh] - cc)))
        st[p]["kk"] = (1.0 - lbs[hh]) * sig_neg
        hi = log_f.astype(BF16)
        lo = (log_f - hi.astype(F32)).astype(BF16)
        st[p]["b"] = (jnp.dot(tri_b, hi, preferred_element_type=F32)
                      + jnp.dot(tri_b, lo, preferred_element_type=F32))

    for p in pairs:
        j, hh = p
        q = q_ref[hh, pl.ds(j * C, C), :].astype(F32)
        kk, b = st[p]["kk"], st[p]["b"]
        b_last = b[C - 1:C, :]
        st[p]["dec"] = jnp.exp(b_last)
        st[p]["qe"] = (q * jnp.exp(b)).astype(BF16)
        st[p]["kdec"] = (kk * jnp.exp(b_last - b)).astype(BF16)
        refs = [b[I * SUB:I * SUB + 1, :] for I in range(nsub)]
        refb = jnp.concatenate([jnp.broadcast_to(r, (SUB, HG_DK)) for r in refs], axis=0)
        qd = (q * jnp.exp(b - refb)).astype(BF16)
        kd = (kk * jnp.exp(refb - b)).astype(BF16)
        q_parts, k_parts = [], []
        for J in range(nsub - 1):
            r = refs[J + 1]
            qj = q * jnp.exp(jnp.minimum(b - r, 0.0))
            kj = kk * jnp.exp(jnp.minimum(r - b, 0.0))
            q_parts.append(jnp.where(row_id >= (J + 1) * SUB, qj, 0.0).astype(BF16))
            k_parts.append(jnp.where((row_id >= J * SUB) & (row_id < (J + 1) * SUB), kj, 0.0).astype(BF16))
        st[p]["att_d"] = _dot_nt(qd, kd)
        st[p]["att_o"] = _dot_nt(jnp.concatenate(q_parts, axis=1), jnp.concatenate(k_parts, axis=1))

    for p in pairs:
        j, hh = p
        vb = i_ref[hh, pl.ds(j * C, C), :].astype(BF16)
        att = jnp.where(diag_mask, st[p]["att_d"], 0.0) + st[p]["att_o"]
        st[p]["intra"] = jnp.dot(att.astype(BF16), vb, preferred_element_type=F32)
        st[p]["upd"] = _dot_tn(vb, st[p]["kdec"])

    for hh in range(heads):
        s = st_scr[hh]
        for j in range(rows // C):
            st[(j, hh)]["s_in"] = s.astype(BF16)
            s = s * st[(j, hh)]["dec"] + st[(j, hh)]["upd"]
        st_scr[hh] = s

    for p in pairs:
        j, hh = p
        g = g_ref[hh, pl.ds(j * C, C), :].astype(F32)
        o = _dot_nt(st[p]["qe"], st[p]["s_in"]) + st[p]["intra"]
        y = o * lax.rsqrt(jnp.mean(o * o, axis=-1, keepdims=True) + EPS) * nw_ref[hh]
        o_ref[pl.ds(j * C, C), hh * LANES:(hh + 1) * LANES] = (y * (g * _sigmoid(g))).astype(o_ref.dtype)


def _hgrn(proj_b, proj_f, lb, norm_w, batch, seq):
    T = batch * seq
    rows = min(HG_ROWS, seq)
    nblk = seq // rows
    H = HG_HEADS
    hp = HG_HEADS_PER_STEP

    def slab(off):
        return pl.BlockSpec((hp, rows, LANES), lambda b, h, c, off=off: (off // hp + h, b * nblk + c, 0))

    vec = pl.BlockSpec((hp, 1, LANES), lambda b, h, c: (h, 0, 0))
    return pl.pallas_call(
        functools.partial(_hgrn_body, rows=rows, heads=hp),
        grid=(batch, H // hp, nblk),
        in_specs=[slab(0), slab(0), slab(H), slab(2 * H), vec, vec],
        out_specs=pl.BlockSpec((rows, hp * LANES), lambda b, h, c: (b * nblk + c, h)),
        out_shape=jax.ShapeDtypeStruct((T, H * LANES), BF16),
        scratch_shapes=[pltpu.VMEM((hp, LANES, HG_DK), F32)],
        compiler_params=_cparams(("parallel", "parallel", "arbitrary")),
        name="hgrn2",
    )(proj_b, proj_f, proj_b, proj_b, lb.reshape(H, 1, HG_DK), norm_w.reshape(H, 1, LANES))


def _mlstm_body(q_ref, k_ref, v_ref, og_ref, gt_ref, gb_ref, cwq_ref, cwk_ref, cbq_ref, cbk_ref, nw_ref,
                out_ref, c_scr, n_scr, m_scr, qx_scr, kx_scr, qc_scr, kc_scr, *, rows):
    C = ML_CHUNK
    PAD = SUBLANES

    @pl.when(pl.program_id(2) == 0)
    def _():
        c_scr[...] = jnp.zeros_like(c_scr)
        n_scr[...] = jnp.zeros_like(n_scr)
        m_scr[...] = jnp.zeros_like(m_scr)
        qx_scr[0:PAD, :] = jnp.zeros((PAD, LANES), F32)
        kx_scr[0:PAD, :] = jnp.zeros((PAD, LANES), F32)

    qx_scr[PAD:PAD + rows, :] = q_ref[0].astype(F32)
    kx_scr[PAD:PAD + rows, :] = k_ref[0].astype(F32)
    accq = jnp.zeros((rows, LANES), F32) + cbq_ref[...]
    acck = jnp.zeros((rows, LANES), F32) + cbk_ref[...]
    for j in range(CONV_W):
        off = PAD - (CONV_W - 1) + j
        accq = accq + cwq_ref[j:j + 1, :] * qx_scr[pl.ds(off, rows), :]
        acck = acck + cwk_ref[j:j + 1, :] * kx_scr[pl.ds(off, rows), :]
    qc_scr[...] = accq * _sigmoid(accq) * (ML_DQK ** -0.5)
    kc_scr[...] = acck * _sigmoid(acck)
    qx_scr[0:PAD, :] = qx_scr[rows:rows + PAD, :]
    kx_scr[0:PAD, :] = kx_scr[rows:rows + PAD, :]

    lane = lax.broadcasted_iota(jnp.int32, (C, C), 1)
    sub = lax.broadcasted_iota(jnp.int32, (C, C), 0)
    causal = lane <= sub
    lower = causal.astype(F32)
    upper = (sub <= lane).astype(F32)
    nw = nw_ref[...]
    gbias = gb_ref[0]

    chunks = range(rows // C)
    lower_b, upper_b = lower.astype(BF16), upper.astype(BF16)
    st = [{} for _ in chunks]

    def split(x):
        hi = x.astype(BF16)
        return hi, (x - hi.astype(F32)).astype(BF16)

    for j in chunks:
        gc = gt_ref[0, pl.ds(j * C, C), :] + gbias
        gct = gc.T
        li_col = gc[:, 0:1]
        li_row = gct[0:1, :]
        ch, cl = split(_log_sigmoid(gc[:, ML_HEADS:ML_HEADS + 1]))
        rh, rl = split(_log_sigmoid(gct[ML_HEADS:ML_HEADS + 1, :]))
        g_t = (jnp.dot(lower_b, jnp.broadcast_to(ch, (C, C)), preferred_element_type=F32)
               + jnp.dot(lower_b, jnp.broadcast_to(cl, (C, C)), preferred_element_type=F32))
        g_s = (jnp.dot(jnp.broadcast_to(rh, (C, C)), upper_b, preferred_element_type=F32)
               + jnp.dot(jnp.broadcast_to(rl, (C, C)), upper_b, preferred_element_type=F32))
        st[j].update(g_t=g_t, g_s=g_s, li_col=li_col, li_row=li_row)

    for j in chunks:
        c = st[j]
        g_col = c["g_t"][:, 0:1]
        g_last = c["g_t"][C - 1:C, 0:1]
        dmat = jnp.where(causal, c["g_t"] - c["g_s"] + c["li_row"], -jnp.inf)
        log_ws = g_last - g_col + c["li_col"]
        qf = qc_scr[pl.ds(j * C, C), :]
        kf = kc_scr[pl.ds(j * C, C), :]
        qb = qf.astype(BF16)
        c.update(g_col=g_col, g_last=g_last, dmat=dmat, dmax=jnp.max(dmat, axis=1, keepdims=True),
                 log_ws=log_ws, ws_max=jnp.max(log_ws, axis=0, keepdims=True), qf=qf, kf=kf, qb=qb,
                 qk=_dot_nt(qb, kf.astype(BF16)))

    m = m_scr[:, 0:1]
    for j in chunks:
        c = st[j]
        m_new = jnp.maximum(c["g_last"] + m, c["ws_max"])
        c.update(m_prev=m, m_new=m_new, decay=jnp.exp(c["g_last"] + m - m_new))
        m = m_new
    m_scr[...] = jnp.broadcast_to(m, m_scr.shape)

    for j in chunks:
        c = st[j]
        vb = jnp.concatenate([v_ref[0, pl.ds(j * C, C), :], v_ref[1, pl.ds(j * C, C), :]], axis=1).astype(BF16)
        kw = c["kf"] * jnp.exp(c["log_ws"] - c["m_new"])
        c.update(vb=vb, upd=_dot_tn(kw.astype(BF16), vb), ksum=jnp.sum(kw, axis=0, keepdims=True))

    cm = c_scr[...]
    nv = n_scr[...]
    for j in chunks:
        c = st[j]
        c.update(c_in=cm.astype(BF16), n_in=nv)
        cm = c["decay"] * cm + c["upd"]
        nv = c["decay"] * nv + c["ksum"]
    c_scr[...] = cm
    n_scr[...] = nv

    for j in chunks:
        c = st[j]
        a_inter = c["g_col"] + c["m_prev"]
        m_t = jnp.maximum(a_inter, c["dmax"])
        w_inter = jnp.exp(a_inter - m_t)
        sqk = c["qk"] * jnp.exp(c["dmat"] - m_t)
        num = (w_inter * jnp.dot(c["qb"], c["c_in"], preferred_element_type=F32)
               + jnp.dot(sqk.astype(BF16), c["vb"], preferred_element_type=F32))
        den = (w_inter * jnp.sum(c["qf"] * c["n_in"], axis=1, keepdims=True)
               + jnp.sum(sqk, axis=1, keepdims=True))
        hh = num * (1.0 / jnp.maximum(jnp.abs(den), jnp.exp(-m_t)))
        y = hh * lax.rsqrt(jnp.mean(hh * hh, axis=-1, keepdims=True) + EPS) * nw
        og = jnp.concatenate([og_ref[0, pl.ds(j * C, C), :], og_ref[1, pl.ds(j * C, C), :]], axis=1).astype(F32)
        out_ref[pl.ds(j * C, C), :] = (y * _sigmoid(og)).astype(out_ref.dtype)


def _mlstm(proj3, gates, gate_bias_pad, conv_w, conv_b, norm_w, batch, seq, q_off):
    T = batch * seq
    rows = min(ML_ROWS, seq)
    nblk = seq // rows
    H = ML_HEADS
    k_off = q_off + H
    v_off = k_off + H
    o_off = v_off + 2 * H

    def slab(off):
        return pl.BlockSpec((1, rows, LANES), lambda b, h, c, off=off: (off + h, b * nblk + c, 0))

    def slab2(off):
        return pl.BlockSpec((2, rows, LANES), lambda b, h, c, off=off: (off // 2 + h, b * nblk + c, 0))

    qk_w = H * ML_DQK
    return pl.pallas_call(
        functools.partial(_mlstm_body, rows=rows),
        grid=(batch, H, nblk),
        in_specs=[
            slab(q_off), slab(k_off), slab2(v_off), slab2(o_off),
            pl.BlockSpec((1, rows, LANES), lambda b, h, c: (h, b * nblk + c, 0)),
            pl.BlockSpec((1, 1, LANES), lambda b, h, c: (h, 0, 0)),
            pl.BlockSpec((CONV_W, LANES), lambda b, h, c: (0, h)),
            pl.BlockSpec((CONV_W, LANES), lambda b, h, c: (0, H + h)),
            pl.BlockSpec((1, LANES), lambda b, h, c: (0, h)),
            pl.BlockSpec((1, LANES), lambda b, h, c: (0, H + h)),
            pl.BlockSpec((1, ML_DV), lambda b, h, c: (0, h)),
        ],
        out_specs=pl.BlockSpec((rows, ML_DV), lambda b, h, c: (b * nblk + c, h)),
        out_shape=jax.ShapeDtypeStruct((T, H * ML_DV), BF16),
        scratch_shapes=[
            pltpu.VMEM((ML_DQK, ML_DV), F32),
            pltpu.VMEM((1, ML_DQK), F32),
            pltpu.VMEM((1, LANES), F32),
            pltpu.VMEM((rows + 2 * SUBLANES, LANES), F32),
            pltpu.VMEM((rows + 2 * SUBLANES, LANES), F32),
            pltpu.VMEM((rows, LANES), F32),
            pltpu.VMEM((rows, LANES), F32),
        ],
        compiler_params=_cparams(("parallel", "parallel", "arbitrary")),
        name="mlstm",
    )(proj3, proj3, proj3, proj3, gates, gate_bias_pad, conv_w, conv_w,
      conv_b.reshape(1, 2 * qk_w), conv_b.reshape(1, 2 * qk_w), norm_w.reshape(1, H * ML_DV))


def _outproj_body(a_ref, b_ref, x_ref, wo_ref, ln_ref, wrh_ref, wrl_ref, br_ref,
                  x2_ref, h2_ref, idx_ref, gate_ref, rank_ref, cnt_ref, cnt_scr, *, sub_rows):
    tm = x_ref.shape[0]
    ka = a_ref.shape[1]
    s = _packed_rows(x_ref.shape[1])

    @pl.when(pl.program_id(0) == 0)
    def _():
        cnt_scr[...] = jnp.zeros_like(cnt_scr)

    lane = lax.broadcasted_iota(jnp.int32, (sub_rows, LANES), 1).astype(F32)
    onehots = [[] for _ in range(TOP_K)]
    for r0 in range(0, tm, sub_rows):
        rows = pl.ds(r0, sub_rows)
        res = (jnp.dot(a_ref[rows, :], wo_ref[0:ka, :], preferred_element_type=F32)
               + jnp.dot(b_ref[rows, :], wo_ref[ka:, :], preferred_element_type=F32))
        x2 = x_ref[rows, :] + res
        x2_ref[rows, :] = x2
        h2 = x2 * lax.rsqrt(jnp.mean(x2 * x2, axis=-1, keepdims=True) + EPS) * ln_ref[...]
        h_hi = h2.astype(BF16)
        h_hi32 = h_hi.astype(F32)
        _pack_store(h2_ref.at[pl.ds(r0 * s, sub_rows * s), :], h_hi32, rounded=True)

        h_lo = (h2 - h_hi32).astype(BF16)
        logits = (jnp.dot(h_hi, wrh_ref[...], preferred_element_type=F32)
                  + jnp.dot(h_lo, wrh_ref[...], preferred_element_type=F32)
                  + jnp.dot(h_hi, wrl_ref[...], preferred_element_type=F32)) + br_ref[...]
        vals, idxs = [], []
        cur = logits
        for _ in range(TOP_K):
            m = jnp.max(cur, axis=1, keepdims=True)
            ix = jnp.min(jnp.where(cur == m, lane, float(LANES)), axis=1, keepdims=True)
            vals.append(m)
            idxs.append(ix)
            cur = jnp.where(lane == ix, -jnp.inf, cur)
        es = [jnp.exp(v - vals[0]) for v in vals]
        inv = 1.0 / (es[0] + es[1] + es[2] + es[3])
        gate = jnp.zeros(logits.shape, F32)
        idx = jnp.zeros(logits.shape, F32)
        for k in range(TOP_K):
            gate = jnp.where(lane == float(k), es[k] * inv, gate)
            idx = jnp.where(lane == float(k), idxs[k], idx)
            onehots[k].append((lane == idxs[k]).astype(F32))
        gate_ref[rows, :] = gate
        idx_ref[rows, :] = idx.astype(jnp.int32)

    oh_k = [jnp.concatenate(o, axis=0) for o in onehots]
    oh = oh_k[0] + oh_k[1] + oh_k[2] + oh_k[3]
    r_i = lax.broadcasted_iota(jnp.int32, (tm, tm), 0)
    c_i = lax.broadcasted_iota(jnp.int32, (tm, tm), 1)
    before = jnp.dot((c_i < r_i).astype(BF16), oh.astype(BF16), preferred_element_type=F32) + cnt_scr[...]
    lane_t = lax.broadcasted_iota(jnp.int32, (tm, LANES), 1)
    rank = jnp.zeros((tm, LANES), F32)
    for k in range(TOP_K):
        rank = jnp.where(lane_t == k, jnp.sum(oh_k[k] * before, axis=1, keepdims=True), rank)
    rank_ref[...] = rank.astype(jnp.int32)
    cnt = cnt_scr[...] + jnp.sum(oh, axis=0, keepdims=True)
    cnt_scr[...] = cnt
    cnt_ref[...] = cnt.astype(jnp.int32)


def _outproj(a_out, b_out, x2d, w_out_bf, ln_w, wr_hi, wr_lo, b_router_pad):
    T, D = x2d.shape
    tm = min(OUTPROJ_TM, T)
    ka, kb = a_out.shape[1], b_out.shape[1]
    s = _packed_rows(D)
    row = lambda w: pl.BlockSpec((tm, w), lambda i: (i, 0))
    full = lambda r, c: pl.BlockSpec((r, c), lambda i: (0, 0))
    return pl.pallas_call(
        functools.partial(_outproj_body, sub_rows=min(OUTPROJ_SUB, tm)),
        grid=(T // tm,),
        in_specs=[row(ka), row(kb), row(D), full(ka + kb, D), full(1, D), full(D, LANES), full(D, LANES),
                  full(1, LANES)],
        out_specs=[row(D), pl.BlockSpec((tm * s, LANES), lambda i: (i, 0)), row(LANES), row(LANES), row(LANES),
                   full(1, LANES)],
        out_shape=[
            jax.ShapeDtypeStruct((T, D), F32),
            jax.ShapeDtypeStruct((T * s, LANES), jnp.uint32),
            jax.ShapeDtypeStruct((T, LANES), jnp.int32),
            jax.ShapeDtypeStruct((T, LANES), F32),
            jax.ShapeDtypeStruct((T, LANES), jnp.int32),
            jax.ShapeDtypeStruct((1, LANES), jnp.int32),
        ],
        scratch_shapes=[pltpu.VMEM((1, LANES), F32)],
        compiler_params=_cparams(("arbitrary",)),
        name="outproj_router",
    )(a_out, b_out, x2d, w_out_bf, ln_w, wr_hi, wr_lo, b_router_pad)


def _start_row_gather(idx_at, src_ref, buf, sem, n, s):
    def start(pair, carry):
        for p in range(DMA_QUEUES):
            i = pair * DMA_QUEUES + p
            src_row = pl.multiple_of(idx_at(i) * s, s)
            dst_row = pl.multiple_of(i * s, s)
            pltpu.make_async_copy(src_ref.at[pl.ds(src_row, s), :], buf.at[pl.ds(dst_row, s), :],
                                  sem).start(priority=p)
        return carry

    lax.fori_loop(0, n // DMA_QUEUES, start, 0, unroll=DMA_UNROLL // DMA_QUEUES)


def _wait_row_gather(buf, sem):
    pltpu.make_async_copy(buf, buf, sem).wait()


def _dispatch_body(zero_ref, dest_ref, h_ref, xs_ref, zbuf, sem, zsem, *, ntok, s, tm, n_blocks):
    blk = tm * s

    def zero_copy(b):
        return pltpu.make_async_copy(zbuf, xs_ref.at[pl.ds(pl.multiple_of(b * blk, blk), blk), :], zsem)

    @pl.when(pl.program_id(0) == 0)
    def _():
        zbuf[...] = jnp.zeros_like(zbuf)

        def zstart(b, carry):
            @pl.when(zero_ref[b] == 1)
            def _():
                zero_copy(b).start()
            return carry

        def zwait(b, carry):
            @pl.when(zero_ref[b] == 1)
            def _():
                zero_copy(b).wait()
            return carry

        lax.fori_loop(0, n_blocks, zstart, 0)
        lax.fori_loop(0, n_blocks, zwait, 0)

    def start(t, carry):
        src = h_ref.at[pl.ds(pl.multiple_of(t * s, s), s), :]
        for k in range(TOP_K):
            dst_row = pl.multiple_of(dest_ref[0, 0, t * TOP_K + k] * s, s)
            pltpu.make_async_copy(src, xs_ref.at[pl.ds(dst_row, s), :], sem).start(priority=k % DMA_QUEUES)
        return carry

    lax.fori_loop(0, ntok, start, 0, unroll=DISPATCH_UNROLL // TOP_K)
    for _ in range(TOP_K):
        pltpu.make_async_copy(h_ref, h_ref, sem).wait()


def _dispatch(dest, zero_blk, h_packed, d, n_slots, tm):
    s = _packed_rows(d)
    T = h_packed.shape[0] // s
    ntok = min(DISPATCH_TOKENS, T)
    n = ntok * TOP_K
    grid_spec = pltpu.PrefetchScalarGridSpec(
        num_scalar_prefetch=1,
        grid=(T // ntok,),
        in_specs=[
            pl.BlockSpec((1, 1, n), lambda i, z: (i, 0, 0), memory_space=pltpu.SMEM),
            pl.BlockSpec((ntok * s, LANES), lambda i, z: (i, 0)),
        ],
        out_specs=pl.BlockSpec(memory_space=pl.ANY),
        scratch_shapes=[pltpu.VMEM((tm * s, LANES), jnp.uint32), pltpu.SemaphoreType.DMA,
                        pltpu.SemaphoreType.DMA],
    )
    return pl.pallas_call(
        functools.partial(_dispatch_body, ntok=ntok, s=s, tm=tm, n_blocks=zero_blk.shape[0]),
        grid_spec=grid_spec,
        out_shape=jax.ShapeDtypeStruct((n_slots * s, LANES), jnp.uint32),
        compiler_params=_cparams(("arbitrary",)),
        name="moe_dispatch",
    )(zero_blk, dest.reshape(T // ntok, 1, n), h_packed)


def _staged_weights(sched_refs, hbm_refs, stage_refs, bf_refs, sem):
    e_ref, wt_ref, first_ref, ne_ref, nt_ref, more_ref = sched_refs
    w = pl.program_id(0)
    tn = stage_refs[0].shape[1]

    def copies(e, t):
        col = pl.multiple_of(t * tn, tn)
        return [pltpu.make_async_copy(h.at[e, :, pl.ds(col, tn)], st, sem.at[k])
                for k, (h, st) in enumerate(zip(hbm_refs, stage_refs))]

    def start_all(cs):
        for c in cs:
            c.start()

    @pl.when(w == 0)
    def _():
        start_all(copies(e_ref[0], wt_ref[0]))

    @pl.when(first_ref[w] == 1)
    def _():
        for c in copies(e_ref[w], wt_ref[w]):
            c.wait()
        for st, bf in zip(stage_refs, bf_refs):
            for r in range(0, st.shape[0], 256):
                bf[r:r + 256, :] = st[r:r + 256, :].astype(BF16)

        @pl.when(more_ref[w] == 1)
        def _():
            start_all(copies(ne_ref[w], nt_ref[w]))


def _row_groups(nrows, tm, compute, clear):
    sub = min(MOE_SUB_ROWS, tm)
    groups = tm // sub
    for live in range(groups + 1):
        @pl.when((nrows > (live - 1) * sub) & (nrows <= live * sub))
        def _(live=live):
            for g in range(live):
                compute(g * sub, sub)
            for g in range(live, groups):
                clear(g * sub, sub)


def _moe_up_body(e_ref, wt_ref, r_ref, ot_ref, first_ref, nrows_ref, ne_ref, nt_ref, more_ref,
                 x_ref, wg_ref, wu_ref, bg_ref, bu_ref, o_ref, wg_stage, wu_stage, wg_scr, wu_scr, x_scr, sem):
    w = pl.program_id(0)
    _staged_weights((e_ref, wt_ref, first_ref, ne_ref, nt_ref, more_ref), (wg_ref, wu_ref),
                    (wg_stage, wu_stage), (wg_scr, wu_scr), sem)
    tm, d = x_scr.shape
    s = _packed_rows(d)

    def compute(r0, n):
        rows = pl.ds(r0, n)
        for c in range(s):
            hi, lo = _unpack_load(x_ref, r0, n, s, c)
            x_scr[rows, c * LANES:(c + 1) * LANES] = hi.astype(BF16)
            x_scr[rows, d // 2 + c * LANES:d // 2 + (c + 1) * LANES] = lo.astype(BF16)
        x = x_scr[rows, :]
        gt = jnp.dot(x, wg_scr[...], preferred_element_type=F32) + bg_ref[0]
        up = jnp.dot(x, wu_scr[...], preferred_element_type=F32) + bu_ref[0]
        gt = jnp.minimum(gt, SWIGLU_LIMIT)
        up = jnp.clip(up, -SWIGLU_LIMIT, SWIGLU_LIMIT)
        swish = (0.5 * gt) * (1.0 + jnp.tanh((0.5 * SWIGLU_ALPHA) * gt))
        o_ref[rows, :] = ((up + 1.0) * swish).astype(o_ref.dtype)

    def clear(r0, n):
        o_ref[pl.ds(r0, n), :] = jnp.zeros((n, o_ref.shape[1]), o_ref.dtype)

    _row_groups(nrows_ref[w], tm, compute, clear)


def _moe_down_body(e_ref, wt_ref, r_ref, ot_ref, first_ref, nrows_ref, ne_ref, nt_ref, more_ref,
                   a_ref, wd_ref, bd_ref, o_ref, wd_stage, wd_scr, sem):
    w = pl.program_id(0)
    _staged_weights((e_ref, wt_ref, first_ref, ne_ref, nt_ref, more_ref), (wd_ref,), (wd_stage,), (wd_scr,), sem)
    tm = a_ref.shape[0]
    s = o_ref.shape[0] // tm

    def compute(r0, n):
        y = jnp.dot(a_ref[pl.ds(r0, n), :], wd_scr[...], preferred_element_type=F32) + bd_ref[0]
        _pack_store(o_ref.at[pl.ds(r0 * s, n * s), :], y)

    def clear(r0, n):
        o_ref[pl.ds(r0 * s, n * s), :] = jnp.zeros((n * s, LANES), o_ref.dtype)

    _row_groups(nrows_ref[w], tm, compute, clear)


def _moe_schedule(counts, tm, n_tiles, n_blocks):
    n_items = n_tiles * n_blocks
    experts = jnp.arange(N_EXPERTS, dtype=jnp.int32)
    blocks_e = (counts + tm - 1) // tm
    bend = jnp.cumsum(blocks_e)
    bstart = bend - blocks_e
    item_end = n_tiles * bend
    total = item_end[-1]
    later = (experts[None, :] > experts[:, None]) & (blocks_e[None, :] > 0)
    next_e = jnp.min(jnp.where(later, experts[None, :], N_EXPERTS - 1), axis=1)

    w = jnp.arange(n_items, dtype=jnp.int32)
    valid = w < total
    wc = jnp.minimum(w, jnp.maximum(total - 1, 0))
    e = jnp.minimum(jnp.sum((item_end[None, :] <= wc[:, None]).astype(jnp.int32), axis=1), N_EXPERTS - 1)
    sel = e[:, None] == experts[None, :]
    pick = lambda table: jnp.sum(jnp.where(sel, table[None, :], 0), axis=1)
    nb = jnp.maximum(pick(blocks_e), 1)
    local = wc - n_tiles * pick(bstart)
    wtile = sum((local >= t * nb).astype(jnp.int32) for t in range(1, n_tiles)) if n_tiles > 1 else 0 * local
    jblk = local - wtile * nb
    spare = jnp.maximum(w - total, 0)
    rblk = jnp.where(valid, pick(bstart) + jblk, bend[-1] + spare // n_tiles)
    otile = jnp.where(valid, wtile, spare % n_tiles)
    nrows = jnp.where(valid, jnp.clip(pick(counts) - jblk * tm, 0, tm), 0)
    first = (jblk == 0) & valid
    last_tile = wtile == n_tiles - 1
    more = first & (w + nb < total)
    i32 = lambda a: a.astype(jnp.int32)
    return (e, i32(wtile), i32(rblk), i32(otile), i32(first), i32(nrows),
            i32(jnp.where(last_tile, pick(next_e), e)), i32(jnp.where(last_tile, 0, wtile + 1)), i32(more))


def _moe_up(sched, xs_packed, w_gate, w_up, b_gate, b_up):
    D, d_ff = w_gate.shape[1], w_gate.shape[2]
    s = _packed_rows(D)
    n_slots = xs_packed.shape[0] // s
    tm, tf = MOE_TM, min(MOE_TF, d_ff)
    n_items = sched[0].shape[0]
    wspec = pl.BlockSpec(memory_space=pl.ANY)
    bspec = pl.BlockSpec((1, 1, tf), lambda w, e, wt, r, ot, *_: (e[w], 0, wt[w]))
    grid_spec = pltpu.PrefetchScalarGridSpec(
        num_scalar_prefetch=len(sched),
        grid=(n_items,),
        in_specs=[pl.BlockSpec((tm * s, LANES), lambda w, e, wt, r, ot, *_: (r[w], 0)),
                  wspec, wspec, bspec, bspec],
        out_specs=pl.BlockSpec((tm, tf), lambda w, e, wt, r, ot, *_: (r[w], ot[w])),
        scratch_shapes=[pltpu.VMEM((D, tf), F32), pltpu.VMEM((D, tf), F32),
                        pltpu.VMEM((D, tf), BF16), pltpu.VMEM((D, tf), BF16), pltpu.VMEM((tm, D), BF16),
                        pltpu.SemaphoreType.DMA((2,))],
    )
    return pl.pallas_call(
        _moe_up_body,
        grid_spec=grid_spec,
        out_shape=jax.ShapeDtypeStruct((n_slots, d_ff), BF16),
        compiler_params=_cparams(("arbitrary",)),
        name="moe_up",
    )(*sched, xs_packed, w_gate, w_up, b_gate.reshape(N_EXPERTS, 1, d_ff), b_up.reshape(N_EXPERTS, 1, d_ff))


def _moe_down(sched, act, w_down, b_down):
    n_slots, d_ff = act.shape
    D = w_down.shape[2]
    tm = MOE_TM
    s = _packed_rows(D)
    n_items = sched[0].shape[0]
    grid_spec = pltpu.PrefetchScalarGridSpec(
        num_scalar_prefetch=len(sched),
        grid=(n_items,),
        in_specs=[
            pl.BlockSpec((tm, d_ff), lambda w, e, wt, r, ot, *_: (r[w], 0)),
            pl.BlockSpec(memory_space=pl.ANY),
            pl.BlockSpec((1, 1, D), lambda w, e, wt, r, ot, *_: (e[w], 0, 0)),
        ],
        out_specs=pl.BlockSpec((tm * s, LANES), lambda w, e, wt, r, ot, *_: (r[w], 0)),
        scratch_shapes=[pltpu.VMEM((d_ff, D), F32), pltpu.VMEM((d_ff, D), BF16), pltpu.SemaphoreType.DMA((1,))],
    )
    return pl.pallas_call(
        _moe_down_body,
        grid_spec=grid_spec,
        out_shape=jax.ShapeDtypeStruct((n_slots * s, LANES), jnp.uint32),
        compiler_params=_cparams(("arbitrary",)),
        name="moe_down",
    )(*sched, act, w_down, b_down.reshape(N_EXPERTS, 1, D))


def _final_body(dest_ref, next_ref, x2_ref, gate_ref, w_ref, ys_ref, o_ref, buf0, buf1, sem0, sem1, *, tm, s):
    n = tm * TOP_K
    i = pl.program_id(0)

    def gather(idx_ref, tile, buf, sem):
        _start_row_gather(lambda r: idx_ref[0, 0, tile * n + r], ys_ref, buf, sem, n, s)

    def combine(tile, buf):
        rows = pl.ds(tile * tm, tm)
        gate = gate_ref[rows, :]
        half = o_ref.shape[1] // 2
        gates = [jnp.broadcast_to(gate[:, k:k + 1], (tm, LANES)) for k in range(TOP_K)]
        for c in range(s):
            acc_hi = x2_ref[rows, c * LANES:(c + 1) * LANES]
            acc_lo = x2_ref[rows, half + c * LANES:half + (c + 1) * LANES]
            for k in range(TOP_K):
                hi, lo = _unpack_load(buf, k * tm, tm, s, c)
                acc_hi = acc_hi + gates[k] * hi
                acc_lo = acc_lo + gates[k] * lo
            o_ref[rows, c * LANES:(c + 1) * LANES] = acc_hi
            o_ref[rows, half + c * LANES:half + (c + 1) * LANES] = acc_lo
        acc = o_ref[rows, :]
        o_ref[rows, :] = acc * lax.rsqrt(jnp.mean(acc * acc, axis=-1, keepdims=True) + EPS) * w_ref[...]

    @pl.when(i == 0)
    def _():
        gather(dest_ref, 0, buf0, sem0)

    gather(dest_ref, 1, buf1, sem1)
    _wait_row_gather(buf0, sem0)
    combine(0, buf0)
    gather(next_ref, 0, buf0, sem0)
    _wait_row_gather(buf1, sem1)
    combine(1, buf1)

    @pl.when(i == pl.num_programs(0) - 1)
    def _():
        _wait_row_gather(buf0, sem0)


def _final(x2, ys_packed, dest, gates_pad, w):
    T, D = x2.shape
    tm = min(FINAL_TM, T // 2)
    s = _packed_rows(D)
    n = tm * TOP_K
    steps = T // (2 * tm)
    dest_km = dest.reshape(steps, 2, tm, TOP_K).transpose(0, 1, 3, 2).reshape(steps, 1, 2 * n)
    buf = pltpu.VMEM((n * s, LANES), jnp.uint32)
    return pl.pallas_call(
        functools.partial(_final_body, tm=tm, s=s),
        grid=(steps,),
        in_specs=[
            pl.BlockSpec((1, 1, 2 * n), lambda i: (i, 0, 0), memory_space=pltpu.SMEM),
            pl.BlockSpec((1, 1, 2 * n), lambda i: (jnp.minimum(i + 1, steps - 1), 0, 0), memory_space=pltpu.SMEM),
            pl.BlockSpec((2 * tm, D), lambda i: (i, 0)),
            pl.BlockSpec((2 * tm, LANES), lambda i: (i, 0)),
            pl.BlockSpec((1, D), lambda i: (0, 0)),
            pl.BlockSpec(memory_space=pl.ANY),
        ],
        out_specs=pl.BlockSpec((2 * tm, D), lambda i: (i, 0)),
        out_shape=jax.ShapeDtypeStruct((T, D), F32),
        scratch_shapes=[buf, buf, pltpu.SemaphoreType.DMA, pltpu.SemaphoreType.DMA],
        compiler_params=_cparams(("arbitrary",)),
        name="final_norm",
    )(dest_km, dest_km, x2, gates_pad, w, ys_packed)


def _moe(h2_packed, T, D, top_idx, rank, counts, w_gate, b_gate, w_up, b_up, w_down, b_down):
    A = T * TOP_K
    tm = MOE_TM
    n_blocks = (A + N_EXPERTS * (tm - 1) + tm - 1) // tm
    n_slots = n_blocks * tm

    blocks_e = (counts + tm - 1) // tm
    bend = jnp.cumsum(blocks_e)
    bstart = bend - blocks_e
    experts = jnp.arange(N_EXPERTS, dtype=jnp.int32)
    first_slot = jnp.sum(jnp.where(top_idx[:, :, None] == experts, bstart * tm, 0), axis=-1)
    dest = (first_slot + rank).astype(jnp.int32).reshape(A)
    blk = jnp.arange(n_blocks, dtype=jnp.int32)
    is_last = jnp.any((blk[:, None] == bend[None, :] - 1) & (blocks_e[None, :] > 0), axis=1)
    zero_blk = ((blk >= bend[-1]) | is_last).astype(jnp.int32)

    xs = _dispatch(dest, zero_blk, h2_packed, D, n_slots, tm)
    d_ff = w_gate.shape[2]
    act = _moe_up(_moe_schedule(counts, tm, -(-d_ff // MOE_TF), n_blocks), xs, w_gate, w_up, b_gate, b_up)
    ys = _moe_down(_moe_schedule(counts, tm, 1, n_blocks), act, w_down, b_down)
    return ys, dest


def kernel(x, ln1_w, w_in, hg_lb_logits, hg_norm_w, ml_conv_w, ml_conv_b, ml_igate_b, ml_fgate_b, ml_norm_w,
           w_out, ln2_w, w_router, b_router, w_gate, b_gate, w_up, b_up, w_down, b_down, final_norm_w):
    B, S, D = x.shape
    T = B * S
    depth = w_in.shape[0]
    hg_w = HG_HEADS * HG_DK
    n_main = 4 * hg_w + 2 * ML_HEADS * ML_DQK + 2 * ML_HEADS * ML_DV
    lb_all = jnp.cumsum(jax.nn.softmax(hg_lb_logits.astype(F32), axis=0), axis=0)

    xc = x.reshape(T, D)
    for l in range(depth):
        w_gates_pad = jnp.pad(w_in[l][:, n_main:], ((0, 0), (0, LANES - 2 * ML_HEADS))).astype(BF16)
        w_bf = w_in[l].astype(BF16)
        ln1 = ln1_w[l].reshape(1, D)
        hf_tiles = hg_w // INPROJ_TN
        proj_f = _inproj(xc, ln1, w_bf, hf_tiles, lambda j: j + hf_tiles, F32)[0]
        proj_b, gates = _inproj(xc, ln1, w_bf, n_main // INPROJ_TN - hf_tiles,
                                lambda j: jnp.where(j >= hf_tiles, j + hf_tiles, j), BF16, w_gates_pad)
        a_out = _hgrn(proj_b, proj_f, lb_all[l], hg_norm_w[l], B, S)
        gate_bias = jnp.pad(jnp.concatenate([ml_igate_b[l], ml_fgate_b[l]]), (0, LANES - 2 * ML_HEADS))
        gates_h = jnp.stack([jnp.roll(gates, -hd, axis=1) for hd in range(ML_HEADS)])
        bias_h = jnp.stack([jnp.roll(gate_bias, -hd) for hd in range(ML_HEADS)]).reshape(ML_HEADS, 1, LANES)
        b_out = _mlstm(proj_b, gates_h, bias_h, ml_conv_w[l], ml_conv_b[l], ml_norm_w[l], B, S, 3 * HG_HEADS)
        wr_pad = jnp.pad(w_router[l], ((0, 0), (0, LANES - N_EXPERTS)))
        wr_hi = wr_pad.astype(BF16)
        wr_lo = (wr_pad - wr_hi.astype(F32)).astype(BF16)
        br_pad = jnp.pad(b_router[l], (0, LANES - N_EXPERTS), constant_values=-1e30).reshape(1, LANES)
        x2, h2, idx_pad, gates_pad, rank_pad, cnt = _outproj(
            a_out, b_out, xc, w_out[l].astype(BF16), ln2_w[l].reshape(1, D), wr_hi, wr_lo, br_pad)
        ys, dest = _moe(h2, T, D, idx_pad[:, :TOP_K], rank_pad[:, :TOP_K], cnt[0, :N_EXPERTS],
                        w_gate[l], b_gate[l], w_up[l], b_up[l], w_down[l], b_down[l])
        if l + 1 < depth:
            raise NotImplementedError("only the final layer fuses the output norm")
        xc = _final(x2, ys, dest, gates_pad, final_norm_w.reshape(1, D))
    return xc.reshape(B, S, D)
```

```python
import functools

import jax
import jax.numpy as jnp
from jax import lax
from jax.experimental import pallas as pl
from jax.experimental.pallas import tpu as pltpu

F32 = jnp.float32
BF16 = jnp.bfloat16

EPS = 1e-6
HG_HEADS = 8
HG_DK = 128
ML_HEADS = 4
ML_DQK = 128
ML_DV = 256
CONV_W = 4
N_EXPERTS = 32
TOP_K = 4
SWIGLU_ALPHA = 1.702
SWIGLU_LIMIT = 7.0

LANES = 128
SUBLANES = 8
VMEM_LIMIT_BYTES = 56 * 1024 * 1024

HG_CHUNK = 64
HG_SUB = 16
ML_CHUNK = 128
HG_ROWS = 2048
ML_ROWS = 512
HG_HEADS_PER_STEP = 2

INPROJ_TM = 1024
INPROJ_TN = 1024
OUTPROJ_TM = 512
OUTPROJ_SUB = 256
DISPATCH_TOKENS = 1024
MOE_TM = 512
MOE_TF = 1024
MOE_SUB_ROWS = 128
DMA_UNROLL = 32
DISPATCH_UNROLL = 64
DMA_QUEUES = 2
FINAL_TM = 128


def _dot_nt(a, b):
    return lax.dot_general(a, b, (((1,), (1,)), ((), ())), preferred_element_type=F32)


def _dot_tn(a, b):
    return lax.dot_general(a, b, (((0,), (0,)), ((), ())), preferred_element_type=F32)


def _log_sigmoid(z):
    return jnp.minimum(z, 0.0) - jnp.log(1.0 + jnp.exp(-jnp.abs(z)))


def _sigmoid(z):
    return 0.5 * jnp.tanh(0.5 * z) + 0.5


def _cparams(semantics):
    return pltpu.CompilerParams(dimension_semantics=semantics, vmem_limit_bytes=VMEM_LIMIT_BYTES)


_HI_MASK = 0xFFFF0000


def _packed_rows(d):
    return d // (2 * LANES)


def _pack_store(o_ref, v, rounded=False):
    n, d = v.shape
    s, half = _packed_rows(d), d // 2
    bits = pltpu.bitcast(v if rounded else v.astype(BF16).astype(F32), jnp.uint32)
    for c in range(s):
        hi = bits[:, c * LANES:(c + 1) * LANES]
        lo = bits[:, half + c * LANES:half + (c + 1) * LANES]
        o_ref[pl.ds(c, n, stride=s), :] = hi | jnp.right_shift(lo, jnp.uint32(16))


def _unpack_load(buf, first_row, n, s, c):
    w = buf[pl.ds(first_row * s + c, n, stride=s), :]
    hi = pltpu.bitcast(w & jnp.uint32(_HI_MASK), F32)
    lo = pltpu.bitcast(jnp.left_shift(w, jnp.uint32(16)), F32)
    return hi, lo


def _inproj_body(*refs, tn, with_gates):
    if with_gates:
        x_ref, lnw_ref, w_ref, wg_ref, o_ref, g_ref, h_scr = refs
    else:
        x_ref, lnw_ref, w_ref, o_ref, h_scr = refs

    @pl.when(pl.program_id(1) == 0)
    def _():
        x = x_ref[...]
        h = x * lax.rsqrt(jnp.mean(x * x, axis=-1, keepdims=True) + EPS) * lnw_ref[...]
        hb = h.astype(BF16)
        h_scr[...] = hb
        if with_gates:
            g_ref[...] = jnp.dot(hb, wg_ref[...], preferred_element_type=F32)

    res = jnp.dot(h_scr[...], w_ref[...], preferred_element_type=F32)
    for c in range(tn // LANES):
        o_ref[c] = res[:, c * LANES:(c + 1) * LANES].astype(o_ref.dtype)


def _inproj(x2d, ln_w, w_bf, n_tiles, col_tile, out_dtype, w_gates_pad=None):
    T, D = x2d.shape
    tm = min(INPROJ_TM, T)
    tn = INPROJ_TN
    with_gates = w_gates_pad is not None
    in_specs = [
        pl.BlockSpec((tm, D), lambda i, j: (i, 0)),
        pl.BlockSpec((1, D), lambda i, j: (0, 0)),
        pl.BlockSpec((D, tn), lambda i, j: (0, col_tile(j))),
    ]
    out_specs = [pl.BlockSpec((tn // LANES, tm, LANES), lambda i, j: (j, i, 0))]
    out_shape = [jax.ShapeDtypeStruct((n_tiles * tn // LANES, T, LANES), out_dtype)]
    args = [x2d, ln_w, w_bf]
    if with_gates:
        in_specs.append(pl.BlockSpec((D, LANES), lambda i, j: (0, 0)))
        out_specs.append(pl.BlockSpec((tm, LANES), lambda i, j: (i, 0)))
        out_shape.append(jax.ShapeDtypeStruct((T, LANES), F32))
        args.append(w_gates_pad)
    return pl.pallas_call(
        functools.partial(_inproj_body, tn=tn, with_gates=with_gates),
        grid=(T // tm, n_tiles),
        in_specs=in_specs,
        out_specs=out_specs,
        out_shape=out_shape,
        scratch_shapes=[pltpu.VMEM((tm, D), BF16)],
        compiler_params=_cparams(("parallel", "arbitrary")),
        name="inproj_gates" if with_gates else "inproj",
    )(*args)


def _hgrn_body(q_ref, f_ref, i_ref, g_ref, lb_ref, nw_ref, o_ref, st_scr, *, rows, heads):
    C, SUB = HG_CHUNK, HG_SUB
    nsub = C // SUB

    @pl.when(pl.program_id(2) == 0)
    def _():
        st_scr[...] = jnp.zeros_like(st_scr)

    lbs = [lb_ref[hh] for hh in range(heads)]
    log_lbs = [jnp.log(lb) for lb in lbs]
    log_1mlbs = [jnp.log1p(-lb) for lb in lbs]

    r_i = lax.broadcasted_iota(jnp.int32, (C, C), 0)
    c_i = lax.broadcasted_iota(jnp.int32, (C, C), 1)
    tri = (c_i <= r_i).astype(F32)
    sub_shift = SUB.bit_length() - 1
    diag_mask = (c_i <= r_i) & (jnp.right_shift(r_i, sub_shift) == jnp.right_shift(c_i, sub_shift))
    row_id = lax.broadcasted_iota(jnp.int32, (C, HG_DK), 0)

    pairs = [(j, hh) for j in range(rows // C) for hh in range(heads)]
    tri_b = tri.astype(BF16)
    st = {p: {} for p in pairs}

    for p in pairs:
        j, hh = p
        z = f_ref[hh, pl.ds(j * C, C), :].astype(F32)
        e = jnp.exp(-jnp.abs(z))
        e1 = 1.0 + e
        log_sig = jnp.minimum(z, 0.0) - jnp.log(e1)
        sig_neg = jnp.where(z >= 0, e, 1.0) / e1
        cc = log_1mlbs[hh] + log_sig
        log_f = jnp.maximum(log_lbs[hh], cc) + jnp.log(1.0 + jnp.exp(-jnp.abs(log_lbs[hh] - cc)))
        st[p]["kk"] = (1.0 - lbs[hh]) * sig_neg
        hi = log_f.astype(BF16)
        lo = (log_f - hi.astype(F32)).astype(BF16)
        st[p]["b"] = (jnp.dot(tri_b, hi, preferred_element_type=F32)
                      + jnp.dot(tri_b, lo, preferred_element_type=F32))

    for p in pairs:
        j, hh = p
        q = q_ref[hh, pl.ds(j * C, C), :].astype(F32)
        kk, b = st[p]["kk"], st[p]["b"]
        b_last = b[C - 1:C, :]
        st[p]["dec"] = jnp.exp(b_last)
        st[p]["qe"] = (q * jnp.exp(b)).astype(BF16)
        st[p]["kdec"] = (kk * jnp.exp(b_last - b)).astype(BF16)
        refs = [b[I * SUB:I * SUB + 1, :] for I in range(nsub)]
        refb = jnp.concatenate([jnp.broadcast_to(r, (SUB, HG_DK)) for r in refs], axis=0)
        qd = (q * jnp.exp(b - refb)).astype(BF16)
        kd = (kk * jnp.exp(refb - b)).astype(BF16)
        q_parts, k_parts = [], []
        for J in range(nsub - 1):
            r = refs[J + 1]
            qj = q * jnp.exp(jnp.minimum(b - r, 0.0))
            kj = kk * jnp.exp(jnp.minimum(r - b, 0.0))
            q_parts.append(jnp.where(row_id >= (J + 1) * SUB, qj, 0.0).astype(BF16))
            k_parts.append(jnp.where((row_id >= J * SUB) & (row_id < (J + 1) * SUB), kj, 0.0).astype(BF16))
        st[p]["att_d"] = _dot_nt(qd, kd)
        st[p]["att_o"] = _dot_nt(jnp.concatenate(q_parts, axis=1), jnp.concatenate(k_parts, axis=1))

    for p in pairs:
        j, hh = p
        vb = i_ref[hh, pl.ds(j * C, C), :].astype(BF16)
        att = jnp.where(diag_mask, st[p]["att_d"], 0.0) + st[p]["att_o"]
        st[p]["intra"] = jnp.dot(att.astype(BF16), vb, preferred_element_type=F32)
        st[p]["upd"] = _dot_tn(vb, st[p]["kdec"])

    for hh in range(heads):
        s = st_scr[hh]
        for j in range(rows // C):
            st[(j, hh)]["s_in"] = s.astype(BF16)
            s = s * st[(j, hh)]["dec"] + st[(j, hh)]["upd"]
        st_scr[hh] = s

    for p in pairs:
        j, hh = p
        g = g_ref[hh, pl.ds(j * C, C), :].astype(F32)
        o = _dot_nt(st[p]["qe"], st[p]["s_in"]) + st[p]["intra"]
        y = o * lax.rsqrt(jnp.mean(o * o, axis=-1, keepdims=True) + EPS) * nw_ref[hh]
        o_ref[pl.ds(j * C, C), hh * LANES:(hh + 1) * LANES] = (y * (g * _sigmoid(g))).astype(o_ref.dtype)


def _hgrn(proj_b, proj_f, lb, norm_w, batch, seq):
    T = batch * seq
    rows = min(HG_ROWS, seq)
    nblk = seq // rows
    H = HG_HEADS
    hp = HG_HEADS_PER_STEP

    def slab(off):
        return pl.BlockSpec((hp, rows, LANES), lambda b, h, c, off=off: (off // hp + h, b * nblk + c, 0))

    vec = pl.BlockSpec((hp, 1, LANES), lambda b, h, c: (h, 0, 0))
    return pl.pallas_call(
        functools.partial(_hgrn_body, rows=rows, heads=hp),
        grid=(batch, H // hp, nblk),
        in_specs=[slab(0), slab(0), slab(H), slab(2 * H), vec, vec],
        out_specs=pl.BlockSpec((rows, hp * LANES), lambda b, h, c: (b * nblk + c, h)),
        out_shape=jax.ShapeDtypeStruct((T, H * LANES), BF16),
        scratch_shapes=[pltpu.VMEM((hp, LANES, HG_DK), F32)],
        compiler_params=_cparams(("parallel", "parallel", "arbitrary")),
        name="hgrn2",
    )(proj_b, proj_f, proj_b, proj_b, lb.reshape(H, 1, HG_DK), norm_w.reshape(H, 1, LANES))


def _mlstm_body(q_ref, k_ref, v_ref, og_ref, gt_ref, gb_ref, cwq_ref, cwk_ref, cbq_ref, cbk_ref, nw_ref,
                out_ref, c_scr, n_scr, m_scr, qx_scr, kx_scr, qc_scr, kc_scr, *, rows):
    C = ML_CHUNK
    PAD = SUBLANES

    @pl.when(pl.program_id(2) == 0)
    def _():
        c_scr[...] = jnp.zeros_like(c_scr)
        n_scr[...] = jnp.zeros_like(n_scr)
        m_scr[...] = jnp.zeros_like(m_scr)
        qx_scr[0:PAD, :] = jnp.zeros((PAD, LANES), F32)
        kx_scr[0:PAD, :] = jnp.zeros((PAD, LANES), F32)

    qx_scr[PAD:PAD + rows, :] = q_ref[0].astype(F32)
    kx_scr[PAD:PAD + rows, :] = k_ref[0].astype(F32)
    accq = jnp.zeros((rows, LANES), F32) + cbq_ref[...]
    acck = jnp.zeros((rows, LANES), F32) + cbk_ref[...]
    for j in range(CONV_W):
        off = PAD - (CONV_W - 1) + j
        accq = accq + cwq_ref[j:j + 1, :] * qx_scr[pl.ds(off, rows), :]
        acck = acck + cwk_ref[j:j + 1, :] * kx_scr[pl.ds(off, rows), :]
    qc_scr[...] = accq * _sigmoid(accq) * (ML_DQK ** -0.5)
    kc_scr[...] = acck * _sigmoid(acck)
    qx_scr[0:PAD, :] = qx_scr[rows:rows + PAD, :]
    kx_scr[0:PAD, :] = kx_scr[rows:rows + PAD, :]

    lane = lax.broadcasted_iota(jnp.int32, (C, C), 1)
    sub = lax.broadcasted_iota(jnp.int32, (C, C), 0)
    causal = lane <= sub
    lower = causal.astype(F32)
    upper = (sub <= lane).astype(F32)
    nw = nw_ref[...]
    gbias = gb_ref[0]

    chunks = range(rows // C)
    lower_b, upper_b = lower.astype(BF16), upper.astype(BF16)
    st = [{} for _ in chunks]

    def split(x):
        hi = x.astype(BF16)
        return hi, (x - hi.astype(F32)).astype(BF16)

    for j in chunks:
        gc = gt_ref[0, pl.ds(j * C, C), :] + gbias
        gct = gc.T
        li_col = gc[:, 0:1]
        li_row = gct[0:1, :]
        ch, cl = split(_log_sigmoid(gc[:, ML_HEADS:ML_HEADS + 1]))
        rh, rl = split(_log_sigmoid(gct[ML_HEADS:ML_HEADS + 1, :]))
        g_t = (jnp.dot(lower_b, jnp.broadcast_to(ch, (C, C)), preferred_element_type=F32)
               + jnp.dot(lower_b, jnp.broadcast_to(cl, (C, C)), preferred_element_type=F32))
        g_s = (jnp.dot(jnp.broadcast_to(rh, (C, C)), upper_b, preferred_element_type=F32)
               + jnp.dot(jnp.broadcast_to(rl, (C, C)), upper_b, preferred_element_type=F32))
        st[j].update(g_t=g_t, g_s=g_s, li_col=li_col, li_row=li_row)

    for j in chunks:
        c = st[j]
        g_col = c["g_t"][:, 0:1]
        g_last = c["g_t"][C - 1:C, 0:1]
        dmat = jnp.where(causal, c["g_t"] - c["g_s"] + c["li_row"], -jnp.inf)
        log_ws = g_last - g_col + c["li_col"]
        qf = qc_scr[pl.ds(j * C, C), :]
        kf = kc_scr[pl.ds(j * C, C), :]
        qb = qf.astype(BF16)
        c.update(g_col=g_col, g_last=g_last, dmat=dmat, dmax=jnp.max(dmat, axis=1, keepdims=True),
                 log_ws=log_ws, ws_max=jnp.max(log_ws, axis=0, keepdims=True), qf=qf, kf=kf, qb=qb,
                 qk=_dot_nt(qb, kf.astype(BF16)))

    m = m_scr[:, 0:1]
    for j in chunks:
        c = st[j]
        m_new = jnp.maximum(c["g_last"] + m, c["ws_max"])
        c.update(m_prev=m, m_new=m_new, decay=jnp.exp(c["g_last"] + m - m_new))
        m = m_new
    m_scr[...] = jnp.broadcast_to(m, m_scr.shape)

    for j in chunks:
        c = st[j]
        vb = jnp.concatenate([v_ref[0, pl.ds(j * C, C), :], v_ref[1, pl.ds(j * C, C), :]], axis=1).astype(BF16)
        kw = c["kf"] * jnp.exp(c["log_ws"] - c["m_new"])
        c.update(vb=vb, upd=_dot_tn(kw.astype(BF16), vb), ksum=jnp.sum(kw, axis=0, keepdims=True))

    cm = c_scr[...]
    nv = n_scr[...]
    for j in chunks:
        c = st[j]
        c.update(c_in=cm.astype(BF16), n_in=nv)
        cm = c["decay"] * cm + c["upd"]
        nv = c["decay"] * nv + c["ksum"]
    c_scr[...] = cm
    n_scr[...] = nv

    for j in chunks:
        c = st[j]
        a_inter = c["g_col"] + c["m_prev"]
        m_t = jnp.maximum(a_inter, c["dmax"])
        w_inter = jnp.exp(a_inter - m_t)
        sqk = c["qk"] * jnp.exp(c["dmat"] - m_t)
        num = (w_inter * jnp.dot(c["qb"], c["c_in"], preferred_element_type=F32)
               + jnp.dot(sqk.astype(BF16), c["vb"], preferred_element_type=F32))
        den = (w_inter * jnp.sum(c["qf"] * c["n_in"], axis=1, keepdims=True)
               + jnp.sum(sqk, axis=1, keepdims=True))
        hh = num * (1.0 / jnp.maximum(jnp.abs(den), jnp.exp(-m_t)))
        y = hh * lax.rsqrt(jnp.mean(hh * hh, axis=-1, keepdims=True) + EPS) * nw
        og = jnp.concatenate([og_ref[0, pl.ds(j * C, C), :], og_ref[1, pl.ds(j * C, C), :]], axis=1).astype(F32)
        out_ref[pl.ds(j * C, C), :] = (y * _sigmoid(og)).astype(out_ref.dtype)


def _mlstm(proj3, gates, gate_bias_pad, conv_w, conv_b, norm_w, batch, seq, q_off):
    T = batch * seq
    rows = min(ML_ROWS, seq)
    nblk = seq // rows
    H = ML_HEADS
    k_off = q_off + H
    v_off = k_off + H
    o_off = v_off + 2 * H

    def slab(off):
        return pl.BlockSpec((1, rows, LANES), lambda b, h, c, off=off: (off + h, b * nblk + c, 0))

    def slab2(off):
        return pl.BlockSpec((2, rows, LANES), lambda b, h, c, off=off: (off // 2 + h, b * nblk + c, 0))

    qk_w = H * ML_DQK
    return pl.pallas_call(
        functools.partial(_mlstm_body, rows=rows),
        grid=(batch, H, nblk),
        in_specs=[
            slab(q_off), slab(k_off), slab2(v_off), slab2(o_off),
            pl.BlockSpec((1, rows, LANES), lambda b, h, c: (h, b * nblk + c, 0)),
            pl.BlockSpec((1, 1, LANES), lambda b, h, c: (h, 0, 0)),
            pl.BlockSpec((CONV_W, LANES), lambda b, h, c: (0, h)),
            pl.BlockSpec((CONV_W, LANES), lambda b, h, c: (0, H + h)),
            pl.BlockSpec((1, LANES), lambda b, h, c: (0, h)),
            pl.BlockSpec((1, LANES), lambda b, h, c: (0, H + h)),
            pl.BlockSpec((1, ML_DV), lambda b, h, c: (0, h)),
        ],
        out_specs=pl.BlockSpec((rows, ML_DV), lambda b, h, c: (b * nblk + c, h)),
        out_shape=jax.ShapeDtypeStruct((T, H * ML_DV), BF16),
        scratch_shapes=[
            pltpu.VMEM((ML_DQK, ML_DV), F32),
            pltpu.VMEM((1, ML_DQK), F32),
            pltpu.VMEM((1, LANES), F32),
            pltpu.VMEM((rows + 2 * SUBLANES, LANES), F32),
            pltpu.VMEM((rows + 2 * SUBLANES, LANES), F32),
            pltpu.VMEM((rows, LANES), F32),
            pltpu.VMEM((rows, LANES), F32),
        ],
        compiler_params=_cparams(("parallel", "parallel", "arbitrary")),
        name="mlstm",
    )(proj3, proj3, proj3, proj3, gates, gate_bias_pad, conv_w, conv_w,
      conv_b.reshape(1, 2 * qk_w), conv_b.reshape(1, 2 * qk_w), norm_w.reshape(1, H * ML_DV))


def _outproj_body(a_ref, b_ref, x_ref, wo_ref, ln_ref, wrh_ref, wrl_ref, br_ref,
                  x2_ref, h2_ref, idx_ref, gate_ref, rank_ref, cnt_ref, cnt_scr, *, sub_rows):
    tm = x_ref.shape[0]
    ka = a_ref.shape[1]
    s = _packed_rows(x_ref.shape[1])

    @pl.when(pl.program_id(0) == 0)
    def _():
        cnt_scr[...] = jnp.zeros_like(cnt_scr)

    lane = lax.broadcasted_iota(jnp.int32, (sub_rows, LANES), 1).astype(F32)
    onehots = [[] for _ in range(TOP_K)]
    for r0 in range(0, tm, sub_rows):
        rows = pl.ds(r0, sub_rows)
        res = (jnp.dot(a_ref[rows, :], wo_ref[0:ka, :], preferred_element_type=F32)
               + jnp.dot(b_ref[rows, :], wo_ref[ka:, :], preferred_element_type=F32))
        x2 = x_ref[rows, :] + res
        x2_ref[rows, :] = x2
        h2 = x2 * lax.rsqrt(jnp.mean(x2 * x2, axis=-1, keepdims=True) + EPS) * ln_ref[...]
        h_hi = h2.astype(BF16)
        h_hi32 = h_hi.astype(F32)
        _pack_store(h2_ref.at[pl.ds(r0 * s, sub_rows * s), :], h_hi32, rounded=True)

        h_lo = (h2 - h_hi32).astype(BF16)
        logits = (jnp.dot(h_hi, wrh_ref[...], preferred_element_type=F32)
                  + jnp.dot(h_lo, wrh_ref[...], preferred_element_type=F32)
                  + jnp.dot(h_hi, wrl_ref[...], preferred_element_type=F32)) + br_ref[...]
        vals, idxs = [], []
        cur = logits
        for _ in range(TOP_K):
            m = jnp.max(cur, axis=1, keepdims=True)
            ix = jnp.min(jnp.where(cur == m, lane, float(LANES)), axis=1, keepdims=True)
            vals.append(m)
            idxs.append(ix)
            cur = jnp.where(lane == ix, -jnp.inf, cur)
        es = [jnp.exp(v - vals[0]) for v in vals]
        inv = 1.0 / (es[0] + es[1] + es[2] + es[3])
        gate = jnp.zeros(logits.shape, F32)
        idx = jnp.zeros(logits.shape, F32)
        for k in range(TOP_K):
            gate = jnp.where(lane == float(k), es[k] * inv, gate)
            idx = jnp.where(lane == float(k), idxs[k], idx)
            onehots[k].append((lane == idxs[k]).astype(F32))
        gate_ref[rows, :] = gate
        idx_ref[rows, :] = idx.astype(jnp.int32)

    oh_k = [jnp.concatenate(o, axis=0) for o in onehots]
    oh = oh_k[0] + oh_k[1] + oh_k[2] + oh_k[3]
    r_i = lax.broadcasted_iota(jnp.int32, (tm, tm), 0)
    c_i = lax.broadcasted_iota(jnp.int32, (tm, tm), 1)
    before = jnp.dot((c_i < r_i).astype(BF16), oh.astype(BF16), preferred_element_type=F32) + cnt_scr[...]
    lane_t = lax.broadcasted_iota(jnp.int32, (tm, LANES), 1)
    rank = jnp.zeros((tm, LANES), F32)
    for k in range(TOP_K):
        rank = jnp.where(lane_t == k, jnp.sum(oh_k[k] * before, axis=1, keepdims=True), rank)
    rank_ref[...] = rank.astype(jnp.int32)
    cnt = cnt_scr[...] + jnp.sum(oh, axis=0, keepdims=True)
    cnt_scr[...] = cnt
    cnt_ref[...] = cnt.astype(jnp.int32)


def _outproj(a_out, b_out, x2d, w_out_bf, ln_w, wr_hi, wr_lo, b_router_pad):
    T, D = x2d.shape
    tm = min(OUTPROJ_TM, T)
    ka, kb = a_out.shape[1], b_out.shape[1]
    s = _packed_rows(D)
    row = lambda w: pl.BlockSpec((tm, w), lambda i: (i, 0))
    full = lambda r, c: pl.BlockSpec((r, c), lambda i: (0, 0))
    return pl.pallas_call(
        functools.partial(_outproj_body, sub_rows=min(OUTPROJ_SUB, tm)),
        grid=(T // tm,),
        in_specs=[row(ka), row(kb), row(D), full(ka + kb, D), full(1, D), full(D, LANES), full(D, LANES),
                  full(1, LANES)],
        out_specs=[row(D), pl.BlockSpec((tm * s, LANES), lambda i: (i, 0)), row(LANES), row(LANES), row(LANES),
                   full(1, LANES)],
        out_shape=[
            jax.ShapeDtypeStruct((T, D), F32),
            jax.ShapeDtypeStruct((T * s, LANES), jnp.uint32),
            jax.ShapeDtypeStruct((T, LANES), jnp.int32),
            jax.ShapeDtypeStruct((T, LANES), F32),
            jax.ShapeDtypeStruct((T, LANES), jnp.int32),
            jax.ShapeDtypeStruct((1, LANES), jnp.int32),
        ],
        scratch_shapes=[pltpu.VMEM((1, LANES), F32)],
        compiler_params=_cparams(("arbitrary",)),
        name="outproj_router",
    )(a_out, b_out, x2d, w_out_bf, ln_w, wr_hi, wr_lo, b_router_pad)


def _start_row_gather(idx_at, src_ref, buf, sem, n, s):
    def start(pair, carry):
        for p in range(DMA_QUEUES):
            i = pair * DMA_QUEUES + p
            src_row = pl.multiple_of(idx_at(i) * s, s)
            dst_row = pl.multiple_of(i * s, s)
            pltpu.make_async_copy(src_ref.at[pl.ds(src_row, s), :], buf.at[pl.ds(dst_row, s), :],
                                  sem).start(priority=p)
        return carry

    lax.fori_loop(0, n // DMA_QUEUES, start, 0, unroll=DMA_UNROLL // DMA_QUEUES)


def _wait_row_gather(buf, sem):
    pltpu.make_async_copy(buf, buf, sem).wait()


def _dispatch_body(zero_ref, dest_ref, h_ref, xs_ref, zbuf, sem, zsem, *, ntok, s, tm, n_blocks):
    blk = tm * s

    def zero_copy(b):
        return pltpu.make_async_copy(zbuf, xs_ref.at[pl.ds(pl.multiple_of(b * blk, blk), blk), :], zsem)

    @pl.when(pl.program_id(0) == 0)
    def _():
        zbuf[...] = jnp.zeros_like(zbuf)

        def zstart(b, carry):
            @pl.when(zero_ref[b] == 1)
            def _():
                zero_copy(b).start()
            return carry

        def zwait(b, carry):
            @pl.when(zero_ref[b] == 1)
            def _():
                zero_copy(b).wait()
            return carry

        lax.fori_loop(0, n_blocks, zstart, 0)
        lax.fori_loop(0, n_blocks, zwait, 0)

    def start(t, carry):
        src = h_ref.at[pl.ds(pl.multiple_of(t * s, s), s), :]
        for k in range(TOP_K):
            dst_row = pl.multiple_of(dest_ref[0, 0, t * TOP_K + k] * s, s)
            pltpu.make_async_copy(src, xs_ref.at[pl.ds(dst_row, s), :], sem).start(priority=k % DMA_QUEUES)
        return carry

    lax.fori_loop(0, ntok, start, 0, unroll=DISPATCH_UNROLL // TOP_K)
    for _ in range(TOP_K):
        pltpu.make_async_copy(h_ref, h_ref, sem).wait()


def _dispatch(dest, zero_blk, h_packed, d, n_slots, tm):
    s = _packed_rows(d)
    T = h_packed.shape[0] // s
    ntok = min(DISPATCH_TOKENS, T)
    n = ntok * TOP_K
    grid_spec = pltpu.PrefetchScalarGridSpec(
        num_scalar_prefetch=1,
        grid=(T // ntok,),
        in_specs=[
            pl.BlockSpec((1, 1, n), lambda i, z: (i, 0, 0), memory_space=pltpu.SMEM),
            pl.BlockSpec((ntok * s, LANES), lambda i, z: (i, 0)),
        ],
        out_specs=pl.BlockSpec(memory_space=pl.ANY),
        scratch_shapes=[pltpu.VMEM((tm * s, LANES), jnp.uint32), pltpu.SemaphoreType.DMA,
                        pltpu.SemaphoreType.DMA],
    )
    return pl.pallas_call(
        functools.partial(_dispatch_body, ntok=ntok, s=s, tm=tm, n_blocks=zero_blk.shape[0]),
        grid_spec=grid_spec,
        out_shape=jax.ShapeDtypeStruct((n_slots * s, LANES), jnp.uint32),
        compiler_params=_cparams(("arbitrary",)),
        name="moe_dispatch",
    )(zero_blk, dest.reshape(T // ntok, 1, n), h_packed)


def _staged_weights(sched_refs, hbm_refs, stage_refs, bf_refs, sem):
    e_ref, wt_ref, first_ref, ne_ref, nt_ref, more_ref = sched_refs
    w = pl.program_id(0)
    tn = stage_refs[0].shape[1]

    def copies(e, t):
        col = pl.multiple_of(t * tn, tn)
        return [pltpu.make_async_copy(h.at[e, :, pl.ds(col, tn)], st, sem.at[k])
                for k, (h, st) in enumerate(zip(hbm_refs, stage_refs))]

    def start_all(cs):
        for c in cs:
            c.start()

    @pl.when(w == 0)
    def _():
        start_all(copies(e_ref[0], wt_ref[0]))

    @pl.when(first_ref[w] == 1)
    def _():
        for c in copies(e_ref[w], wt_ref[w]):
            c.wait()
        for st, bf in zip(stage_refs, bf_refs):
            for r in range(0, st.shape[0], 256):
                bf[r:r + 256, :] = st[r:r + 256, :].astype(BF16)

        @pl.when(more_ref[w] == 1)
        def _():
            start_all(copies(ne_ref[w], nt_ref[w]))


def _row_groups(nrows, tm, compute, clear):
    sub = min(MOE_SUB_ROWS, tm)
    groups = tm // sub
    for live in range(groups + 1):
        @pl.when((nrows > (live - 1) * sub) & (nrows <= live * sub))
        def _(live=live):
            for g in range(live):
                compute(g * sub, sub)
            for g in range(live, groups):
                clear(g * sub, sub)


def _moe_up_body(e_ref, wt_ref, r_ref, ot_ref, first_ref, nrows_ref, ne_ref, nt_ref, more_ref,
                 x_ref, wg_ref, wu_ref, bg_ref, bu_ref, o_ref, wg_stage, wu_stage, wg_scr, wu_scr, x_scr, sem):
    w = pl.program_id(0)
    _staged_weights((e_ref, wt_ref, first_ref, ne_ref, nt_ref, more_ref), (wg_ref, wu_ref),
                    (wg_stage, wu_stage), (wg_scr, wu_scr), sem)
    tm, d = x_scr.shape
    s = _packed_rows(d)

    def compute(r0, n):
        rows = pl.ds(r0, n)
        for c in range(s):
            hi, lo = _unpack_load(x_ref, r0, n, s, c)
            x_scr[rows, c * LANES:(c + 1) * LANES] = hi.astype(BF16)
            x_scr[rows, d // 2 + c * LANES:d // 2 + (c + 1) * LANES] = lo.astype(BF16)
        x = x_scr[rows, :]
        gt = jnp.dot(x, wg_scr[...], preferred_element_type=F32) + bg_ref[0]
        up = jnp.dot(x, wu_scr[...], preferred_element_type=F32) + bu_ref[0]
        gt = jnp.minimum(gt, SWIGLU_LIMIT)
        up = jnp.clip(up, -SWIGLU_LIMIT, SWIGLU_LIMIT)
        swish = (0.5 * gt) * (1.0 + jnp.tanh((0.5 * SWIGLU_ALPHA) * gt))
        o_ref[rows, :] = ((up + 1.0) * swish).astype(o_ref.dtype)

    def clear(r0, n):
        o_ref[pl.ds(r0, n), :] = jnp.zeros((n, o_ref.shape[1]), o_ref.dtype)

    _row_groups(nrows_ref[w], tm, compute, clear)


def _moe_down_body(e_ref, wt_ref, r_ref, ot_ref, first_ref, nrows_ref, ne_ref, nt_ref, more_ref,
                   a_ref, wd_ref, bd_ref, o_ref, wd_stage, wd_scr, sem):
    w = pl.program_id(0)
    _staged_weights((e_ref, wt_ref, first_ref, ne_ref, nt_ref, more_ref), (wd_ref,), (wd_stage,), (wd_scr,), sem)
    tm = a_ref.shape[0]
    s = o_ref.shape[0] // tm

    def compute(r0, n):
        y = jnp.dot(a_ref[pl.ds(r0, n), :], wd_scr[...], preferred_element_type=F32) + bd_ref[0]
        _pack_store(o_ref.at[pl.ds(r0 * s, n * s), :], y)

    def clear(r0, n):
        o_ref[pl.ds(r0 * s, n * s), :] = jnp.zeros((n * s, LANES), o_ref.dtype)

    _row_groups(nrows_ref[w], tm, compute, clear)


def _moe_schedule(counts, tm, n_tiles, n_blocks):
    n_items = n_tiles * n_blocks
    experts = jnp.arange(N_EXPERTS, dtype=jnp.int32)
    blocks_e = (counts + tm - 1) // tm
    bend = jnp.cumsum(blocks_e)
    bstart = bend - blocks_e
    item_end = n_tiles * bend
    total = item_end[-1]
    later = (experts[None, :] > experts[:, None]) & (blocks_e[None, :] > 0)
    next_e = jnp.min(jnp.where(later, experts[None, :], N_EXPERTS - 1), axis=1)

    w = jnp.arange(n_items, dtype=jnp.int32)
    valid = w < total
    wc = jnp.minimum(w, jnp.maximum(total - 1, 0))
    e = jnp.minimum(jnp.sum((item_end[None, :] <= wc[:, None]).astype(jnp.int32), axis=1), N_EXPERTS - 1)
    sel = e[:, None] == experts[None, :]
    pick = lambda table: jnp.sum(jnp.where(sel, table[None, :], 0), axis=1)
    nb = jnp.maximum(pick(blocks_e), 1)
    local = wc - n_tiles * pick(bstart)
    wtile = sum((local >= t * nb).astype(jnp.int32) for t in range(1, n_tiles)) if n_tiles > 1 else 0 * local
    jblk = local - wtile * nb
    spare = jnp.maximum(w - total, 0)
    rblk = jnp.where(valid, pick(bstart) + jblk, bend[-1] + spare // n_tiles)
    otile = jnp.where(valid, wtile, spare % n_tiles)
    nrows = jnp.where(valid, jnp.clip(pick(counts) - jblk * tm, 0, tm), 0)
    first = (jblk == 0) & valid
    last_tile = wtile == n_tiles - 1
    more = first & (w + nb < total)
    i32 = lambda a: a.astype(jnp.int32)
    return (e, i32(wtile), i32(rblk), i32(otile), i32(first), i32(nrows),
            i32(jnp.where(last_tile, pick(next_e), e)), i32(jnp.where(last_tile, 0, wtile + 1)), i32(more))


def _moe_up(sched, xs_packed, w_gate, w_up, b_gate, b_up):
    D, d_ff = w_gate.shape[1], w_gate.shape[2]
    s = _packed_rows(D)
    n_slots = xs_packed.shape[0] // s
    tm, tf = MOE_TM, min(MOE_TF, d_ff)
    n_items = sched[0].shape[0]
    wspec = pl.BlockSpec(memory_space=pl.ANY)
    bspec = pl.BlockSpec((1, 1, tf), lambda w, e, wt, r, ot, *_: (e[w], 0, wt[w]))
    grid_spec = pltpu.PrefetchScalarGridSpec(
        num_scalar_prefetch=len(sched),
        grid=(n_items,),
        in_specs=[pl.BlockSpec((tm * s, LANES), lambda w, e, wt, r, ot, *_: (r[w], 0)),
                  wspec, wspec, bspec, bspec],
        out_specs=pl.BlockSpec((tm, tf), lambda w, e, wt, r, ot, *_: (r[w], ot[w])),
        scratch_shapes=[pltpu.VMEM((D, tf), F32), pltpu.VMEM((D, tf), F32),
                        pltpu.VMEM((D, tf), BF16), pltpu.VMEM((D, tf), BF16), pltpu.VMEM((tm, D), BF16),
                        pltpu.SemaphoreType.DMA((2,))],
    )
    return pl.pallas_call(
        _moe_up_body,
        grid_spec=grid_spec,
        out_shape=jax.ShapeDtypeStruct((n_slots, d_ff), BF16),
        compiler_params=_cparams(("arbitrary",)),
        name="moe_up",
    )(*sched, xs_packed, w_gate, w_up, b_gate.reshape(N_EXPERTS, 1, d_ff), b_up.reshape(N_EXPERTS, 1, d_ff))


def _moe_down(sched, act, w_down, b_down):
    n_slots, d_ff = act.shape
    D = w_down.shape[2]
    tm = MOE_TM
    s = _packed_rows(D)
    n_items = sched[0].shape[0]
    grid_spec = pltpu.PrefetchScalarGridSpec(
        num_scalar_prefetch=len(sched),
        grid=(n_items,),
        in_specs=[
            pl.BlockSpec((tm, d_ff), lambda w, e, wt, r, ot, *_: (r[w], 0)),
            pl.BlockSpec(memory_space=pl.ANY),
            pl.BlockSpec((1, 1, D), lambda w, e, wt, r, ot, *_: (e[w], 0, 0)),
        ],
        out_specs=pl.BlockSpec((tm * s, LANES), lambda w, e, wt, r, ot, *_: (r[w], 0)),
        scratch_shapes=[pltpu.VMEM((d_ff, D), F32), pltpu.VMEM((d_ff, D), BF16), pltpu.SemaphoreType.DMA((1,))],
    )
    return pl.pallas_call(
        _moe_down_body,
        grid_spec=grid_spec,
        out_shape=jax.ShapeDtypeStruct((n_slots * s, LANES), jnp.uint32),
        compiler_params=_cparams(("arbitrary",)),
        name="moe_down",
    )(*sched, act, w_down, b_down.reshape(N_EXPERTS, 1, D))


def _final_body(dest_ref, next_ref, x2_ref, gate_ref, w_ref, ys_ref, o_ref, buf0, buf1, sem0, sem1, *, tm, s):
    n = tm * TOP_K
    i = pl.program_id(0)

    def gather(idx_ref, tile, buf, sem):
        _start_row_gather(lambda r: idx_ref[0, 0, tile * n + r], ys_ref, buf, sem, n, s)

    def combine(tile, buf):
        rows = pl.ds(tile * tm, tm)
        gate = gate_ref[rows, :]
        half = o_ref.shape[1] // 2
        gates = [jnp.broadcast_to(gate[:, k:k + 1], (tm, LANES)) for k in range(TOP_K)]
        for c in range(s):
            acc_hi = x2_ref[rows, c * LANES:(c + 1) * LANES]
            acc_lo = x2_ref[rows, half + c * LANES:half + (c + 1) * LANES]
            for k in range(TOP_K):
                hi, lo = _unpack_load(buf, k * tm, tm, s, c)
                acc_hi = acc_hi + gates[k] * hi
                acc_lo = acc_lo + gates[k] * lo
            o_ref[rows, c * LANES:(c + 1) * LANES] = acc_hi
            o_ref[rows, half + c * LANES:half + (c + 1) * LANES] = acc_lo
        acc = o_ref[rows, :]
        o_ref[rows, :] = acc * lax.rsqrt(jnp.mean(acc * acc, axis=-1, keepdims=True) + EPS) * w_ref[...]

    @pl.when(i == 0)
    def _():
        gather(dest_ref, 0, buf0, sem0)

    gather(dest_ref, 1, buf1, sem1)
    _wait_row_gather(buf0, sem0)
    combine(0, buf0)
    gather(next_ref, 0, buf0, sem0)
    _wait_row_gather(buf1, sem1)
    combine(1, buf1)

    @pl.when(i == pl.num_programs(0) - 1)
    def _():
        _wait_row_gather(buf0, sem0)


def _final(x2, ys_packed, dest, gates_pad, w):
    T, D = x2.shape
    tm = min(FINAL_TM, T // 2)
    s = _packed_rows(D)
    n = tm * TOP_K
    steps = T // (2 * tm)
    dest_km = dest.reshape(steps, 2, tm, TOP_K).transpose(0, 1, 3, 2).reshape(steps, 1, 2 * n)
    buf = pltpu.VMEM((n * s, LANES), jnp.uint32)
    return pl.pallas_call(
        functools.partial(_final_body, tm=tm, s=s),
        grid=(steps,),
        in_specs=[
            pl.BlockSpec((1, 1, 2 * n), lambda i: (i, 0, 0), memory_space=pltpu.SMEM),
            pl.BlockSpec((1, 1, 2 * n), lambda i: (jnp.minimum(i + 1, steps - 1), 0, 0), memory_space=pltpu.SMEM),
            pl.BlockSpec((2 * tm, D), lambda i: (i, 0)),
            pl.BlockSpec((2 * tm, LANES), lambda i: (i, 0)),
            pl.BlockSpec((1, D), lambda i: (0, 0)),
            pl.BlockSpec(memory_space=pl.ANY),
        ],
        out_specs=pl.BlockSpec((2 * tm, D), lambda i: (i, 0)),
        out_shape=jax.ShapeDtypeStruct((T, D), F32),
        scratch_shapes=[buf, buf, pltpu.SemaphoreType.DMA, pltpu.SemaphoreType.DMA],
        compiler_params=_cparams(("arbitrary",)),
        name="final_norm",
    )(dest_km, dest_km, x2, gates_pad, w, ys_packed)


def _moe(h2_packed, T, D, top_idx, rank, counts, w_gate, b_gate, w_up, b_up, w_down, b_down):
    A = T * TOP_K
    tm = MOE_TM
    n_blocks = (A + N_EXPERTS * (tm - 1) + tm - 1) // tm
    n_slots = n_blocks * tm

    blocks_e = (counts + tm - 1) // tm
    bend = jnp.cumsum(blocks_e)
    bstart = bend - blocks_e
    experts = jnp.arange(N_EXPERTS, dtype=jnp.int32)
    first_slot = jnp.sum(jnp.where(top_idx[:, :, None] == experts, bstart * tm, 0), axis=-1)
    dest = (first_slot + rank).astype(jnp.int32).reshape(A)
    blk = jnp.arange(n_blocks, dtype=jnp.int32)
    is_last = jnp.any((blk[:, None] == bend[None, :] - 1) & (blocks_e[None, :] > 0), axis=1)
    zero_blk = ((blk >= bend[-1]) | is_last).astype(jnp.int32)

    xs = _dispatch(dest, zero_blk, h2_packed, D, n_slots, tm)
    d_ff = w_gate.shape[2]
    act = _moe_up(_moe_schedule(counts, tm, -(-d_ff // MOE_TF), n_blocks), xs, w_gate, w_up, b_gate, b_up)
    ys = _moe_down(_moe_schedule(counts, tm, 1, n_blocks), act, w_down, b_down)
    return ys, dest


def kernel(x, ln1_w, w_in, hg_lb_logits, hg_norm_w, ml_conv_w, ml_conv_b, ml_igate_b, ml_fgate_b, ml_norm_w,
           w_out, ln2_w, w_router, b_router, w_gate, b_gate, w_up, b_up, w_down, b_down, final_norm_w):
    B, S, D = x.shape
    T = B * S
    depth = w_in.shape[0]
    hg_w = HG_HEADS * HG_DK
    n_main = 4 * hg_w + 2 * ML_HEADS * ML_DQK + 2 * ML_HEADS * ML_DV
    lb_all = jnp.cumsum(jax.nn.softmax(hg_lb_logits.astype(F32), axis=0), axis=0)

    xc = x.reshape(T, D)
    for l in range(depth):
        w_gates_pad = jnp.pad(w_in[l][:, n_main:], ((0, 0), (0, LANES - 2 * ML_HEADS))).astype(BF16)
        w_bf = w_in[l].astype(BF16)
        ln1 = ln1_w[l].reshape(1, D)
        hf_tiles = hg_w // INPROJ_TN
        proj_f = _inproj(xc, ln1, w_bf, hf_tiles, lambda j: j + hf_tiles, F32)[0]
        proj_b, gates = _inproj(xc, ln1, w_bf, n_main // INPROJ_TN - hf_tiles,
                                lambda j: jnp.where(j >= hf_tiles, j + hf_tiles, j), BF16, w_gates_pad)
        a_out = _hgrn(proj_b, proj_f, lb_all[l], hg_norm_w[l], B, S)
        gate_bias = jnp.pad(jnp.concatenate([ml_igate_b[l], ml_fgate_b[l]]), (0, LANES - 2 * ML_HEADS))
        gates_h = jnp.stack([jnp.roll(gates, -hd, axis=1) for hd in range(ML_HEADS)])
        bias_h = jnp.stack([jnp.roll(gate_bias, -hd) for hd in range(ML_HEADS)]).reshape(ML_HEADS, 1, LANES)
        b_out = _mlstm(proj_b, gates_h, bias_h, ml_conv_w[l], ml_conv_b[l], ml_norm_w[l], B, S, 3 * HG_HEADS)
        wr_pad = jnp.pad(w_router[l], ((0, 0), (0, LANES - N_EXPERTS)))
        wr_hi = wr_pad.astype(BF16)
        wr_lo = (wr_pad - wr_hi.astype(F32)).astype(BF16)
        br_pad = jnp.pad(b_router[l], (0, LANES - N_EXPERTS), constant_values=-1e30).reshape(1, LANES)
        x2, h2, idx_pad, gates_pad, rank_pad, cnt = _outproj(
            a_out, b_out, xc, w_out[l].astype(BF16), ln2_w[l].reshape(1, D), wr_hi, wr_lo, br_pad)
        ys, dest = _moe(h2, T, D, idx_pad[:, :TOP_K], rank_pad[:, :TOP_K], cnt[0, :N_EXPERTS],
                        w_gate[l], b_gate[l], w_up[l], b_up[l], w_down[l], b_down[l])
        if l + 1 < depth:
            raise NotImplementedError("only the final layer fuses the output norm")
        xc = _final(x2, ys, dest, gates_pad, final_norm_w.reshape(1, D))
    return xc.reshape(B, S, D)
```

```python
import functools

import jax
import jax.numpy as jnp
from jax import lax
from jax.experimental import pallas as pl
from jax.experimental.pallas import tpu as pltpu

F32 = jnp.float32
BF16 = jnp.bfloat16

EPS = 1e-6
HG_HEADS = 8
HG_DK = 128
ML_HEADS = 4
ML_DQK = 128
ML_DV = 256
CONV_W = 4
N_EXPERTS = 32
TOP_K = 4
SWIGLU_ALPHA = 1.702
SWIGLU_LIMIT = 7.0

LANES = 128
SUBLANES = 8
VMEM_LIMIT_BYTES = 56 * 1024 * 1024

HG_CHUNK = 64
HG_SUB = 16
ML_CHUNK = 128
HG_ROWS = 2048
ML_ROWS = 512
HG_HEADS_PER_STEP = 2

INPROJ_TM = 1024
INPROJ_TN = 1024
OUTPROJ_TM = 512
OUTPROJ_SUB = 256
DISPATCH_TOKENS = 1024
MOE_TM = 512
MOE_TF = 1024
MOE_SUB_ROWS = 128
DMA_UNROLL = 32
DISPATCH_UNROLL = 64
DMA_QUEUES = 2
FINAL_TM = 64


def _dot_nt(a, b):
    return lax.dot_general(a, b, (((1,), (1,)), ((), ())), preferred_element_type=F32)


def _dot_tn(a, b):
    return lax.dot_general(a, b, (((0,), (0,)), ((), ())), preferred_element_type=F32)


def _log_sigmoid(z):
    return jnp.minimum(z, 0.0) - jnp.log(1.0 + jnp.exp(-jnp.abs(z)))


def _sigmoid(z):
    return 0.5 * jnp.tanh(0.5 * z) + 0.5


def _cparams(semantics):
    return pltpu.CompilerParams(dimension_semantics=semantics, vmem_limit_bytes=VMEM_LIMIT_BYTES)


_HI_MASK = 0xFFFF0000


def _packed_rows(d):
    return d // (2 * LANES)


def _pack_store(o_ref, v, rounded=False):
    n, d = v.shape
    s, half = _packed_rows(d), d // 2
    bits = pltpu.bitcast(v if rounded else v.astype(BF16).astype(F32), jnp.uint32)
    for c in range(s):
        hi = bits[:, c * LANES:(c + 1) * LANES]
        lo = bits[:, half + c * LANES:half + (c + 1) * LANES]
        o_ref[pl.ds(c, n, stride=s), :] = hi | jnp.right_shift(lo, jnp.uint32(16))


def _unpack_load(buf, first_row, n, s, c):
    w = buf[pl.ds(first_row * s + c, n, stride=s), :]
    hi = pltpu.bitcast(w & jnp.uint32(_HI_MASK), F32)
    lo = pltpu.bitcast(jnp.left_shift(w, jnp.uint32(16)), F32)
    return hi, lo


def _inproj_body(*refs, tn, with_gates):
    if with_gates:
        x_ref, lnw_ref, w_ref, wg_ref, o_ref, g_ref, h_scr = refs
    else:
        x_ref, lnw_ref, w_ref, o_ref, h_scr = refs

    @pl.when(pl.program_id(1) == 0)
    def _():
        x = x_ref[...]
        h = x * lax.rsqrt(jnp.mean(x * x, axis=-1, keepdims=True) + EPS) * lnw_ref[...]
        hb = h.astype(BF16)
        h_scr[...] = hb
        if with_gates:
            g_ref[...] = jnp.dot(hb, wg_ref[...], preferred_element_type=F32)

    res = jnp.dot(h_scr[...], w_ref[...], preferred_element_type=F32)
    for c in range(tn // LANES):
        o_ref[c] = res[:, c * LANES:(c + 1) * LANES].astype(o_ref.dtype)


def _inproj(x2d, ln_w, w_bf, n_tiles, col_tile, out_dtype, w_gates_pad=None):
    T, D = x2d.shape
    tm = min(INPROJ_TM, T)
    tn = INPROJ_TN
    with_gates = w_gates_pad is not None
    in_specs = [
        pl.BlockSpec((tm, D), lambda i, j: (i, 0)),
        pl.BlockSpec((1, D), lambda i, j: (0, 0)),
        pl.BlockSpec((D, tn), lambda i, j: (0, col_tile(j))),
    ]
    out_specs = [pl.BlockSpec((tn // LANES, tm, LANES), lambda i, j: (j, i, 0))]
    out_shape = [jax.ShapeDtypeStruct((n_tiles * tn // LANES, T, LANES), out_dtype)]
    args = [x2d, ln_w, w_bf]
    if with_gates:
        in_specs.append(pl.BlockSpec((D, LANES), lambda i, j: (0, 0)))
        out_specs.append(pl.BlockSpec((tm, LANES), lambda i, j: (i, 0)))
        out_shape.append(jax.ShapeDtypeStruct((T, LANES), F32))
        args.append(w_gates_pad)
    return pl.pallas_call(
        functools.partial(_inproj_body, tn=tn, with_gates=with_gates),
        grid=(T // tm, n_tiles),
        in_specs=in_specs,
        out_specs=out_specs,
        out_shape=out_shape,
        scratch_shapes=[pltpu.VMEM((tm, D), BF16)],
        compiler_params=_cparams(("parallel", "arbitrary")),
        name="inproj_gates" if with_gates else "inproj",
    )(*args)


def _hgrn_body(q_ref, f_ref, i_ref, g_ref, lb_ref, nw_ref, o_ref, st_scr, *, rows, heads):
    C, SUB = HG_CHUNK, HG_SUB
    nsub = C // SUB

    @pl.when(pl.program_id(2) == 0)
    def _():
        st_scr[...] = jnp.zeros_like(st_scr)

    lbs = [lb_ref[hh] for hh in range(heads)]
    log_lbs = [jnp.log(lb) for lb in lbs]
    log_1mlbs = [jnp.log1p(-lb) for lb in lbs]

    r_i = lax.broadcasted_iota(jnp.int32, (C, C), 0)
    c_i = lax.broadcasted_iota(jnp.int32, (C, C), 1)
    tri = (c_i <= r_i).astype(F32)
    sub_shift = SUB.bit_length() - 1
    diag_mask = (c_i <= r_i) & (jnp.right_shift(r_i, sub_shift) == jnp.right_shift(c_i, sub_shift))
    row_id = lax.broadcasted_iota(jnp.int32, (C, HG_DK), 0)

    pairs = [(j, hh) for j in range(rows // C) for hh in range(heads)]
    tri_b = tri.astype(BF16)
    st = {p: {} for p in pairs}

    for p in pairs:
        j, hh = p
        z = f_ref[hh, pl.ds(j * C, C), :].astype(F32)
        e = jnp.exp(-jnp.abs(z))
        e1 = 1.0 + e
        log_sig = jnp.minimum(z, 0.0) - jnp.log(e1)
        sig_neg = jnp.where(z >= 0, e, 1.0) / e1
        cc = log_1mlbs[hh] + log_sig
        log_f = jnp.maximum(log_lbs[hh], cc) + jnp.log(1.0 + jnp.exp(-jnp.abs(log_lbs[hh] - cc)))
        st[p]["kk"] = (1.0 - lbs[hh]) * sig_neg
        hi = log_f.astype(BF16)
        lo = (log_f - hi.astype(F32)).astype(BF16)
        st[p]["b"] = (jnp.dot(tri_b, hi, preferred_element_type=F32)
                      + jnp.dot(tri_b, lo, preferred_element_type=F32))

    for p in pairs:
        j, hh = p
        q = q_ref[hh, pl.ds(j * C, C), :].astype(F32)
        kk, b = st[p]["kk"], st[p]["b"]
        b_last = b[C - 1:C, :]
        st[p]["dec"] = jnp.exp(b_last)
        st[p]["qe"] = (q * jnp.exp(b)).astype(BF16)
        st[p]["kdec"] = (kk * jnp.exp(b_last - b)).astype(BF16)
        refs = [b[I * SUB:I * SUB + 1, :] for I in range(nsub)]
        refb = jnp.concatenate([jnp.broadcast_to(r, (SUB, HG_DK)) for r in refs], axis=0)
        qd = (q * jnp.exp(b - refb)).astype(BF16)
        kd = (kk * jnp.exp(refb - b)).astype(BF16)
        q_parts, k_parts = [], []
        for J in range(nsub - 1):
            r = refs[J + 1]
            qj = q * jnp.exp(jnp.minimum(b - r, 0.0))
            kj = kk * jnp.exp(jnp.minimum(r - b, 0.0))
            q_parts.append(jnp.where(row_id >= (J + 1) * SUB, qj, 0.0).astype(BF16))
            k_parts.append(jnp.where((row_id >= J * SUB) & (row_id < (J + 1) * SUB), kj, 0.0).astype(BF16))
        st[p]["att_d"] = _dot_nt(qd, kd)
        st[p]["att_o"] = _dot_nt(jnp.concatenate(q_parts, axis=1), jnp.concatenate(k_parts, axis=1))

    for p in pairs:
        j, hh = p
        vb = i_ref[hh, pl.ds(j * C, C), :].astype(BF16)
        att = jnp.where(diag_mask, st[p]["att_d"], 0.0) + st[p]["att_o"]
        st[p]["intra"] = jnp.dot(att.astype(BF16), vb, preferred_element_type=F32)
        st[p]["upd"] = _dot_tn(vb, st[p]["kdec"])

    for hh in range(heads):
        s = st_scr[hh]
        for j in range(rows // C):
            st[(j, hh)]["s_in"] = s.astype(BF16)
            s = s * st[(j, hh)]["dec"] + st[(j, hh)]["upd"]
        st_scr[hh] = s

    for p in pairs:
        j, hh = p
        g = g_ref[hh, pl.ds(j * C, C), :].astype(F32)
        o = _dot_nt(st[p]["qe"], st[p]["s_in"]) + st[p]["intra"]
        y = o * lax.rsqrt(jnp.mean(o * o, axis=-1, keepdims=True) + EPS) * nw_ref[hh]
        o_ref[pl.ds(j * C, C), hh * LANES:(hh + 1) * LANES] = (y * (g * _sigmoid(g))).astype(o_ref.dtype)


def _hgrn(proj_b, proj_f, lb, norm_w, batch, seq):
    T = batch * seq
    rows = min(HG_ROWS, seq)
    nblk = seq // rows
    H = HG_HEADS
    hp = HG_HEADS_PER_STEP

    def slab(off):
        return pl.BlockSpec((hp, rows, LANES), lambda b, h, c, off=off: (off // hp + h, b * nblk + c, 0))

    vec = pl.BlockSpec((hp, 1, LANES), lambda b, h, c: (h, 0, 0))
    return pl.pallas_call(
        functools.partial(_hgrn_body, rows=rows, heads=hp),
        grid=(batch, H // hp, nblk),
        in_specs=[slab(0), slab(0), slab(H), slab(2 * H), vec, vec],
        out_specs=pl.BlockSpec((rows, hp * LANES), lambda b, h, c: (b * nblk + c, h)),
        out_shape=jax.ShapeDtypeStruct((T, H * LANES), BF16),
        scratch_shapes=[pltpu.VMEM((hp, LANES, HG_DK), F32)],
        compiler_params=_cparams(("parallel", "parallel", "arbitrary")),
        name="hgrn2",
    )(proj_b, proj_f, proj_b, proj_b, lb.reshape(H, 1, HG_DK), norm_w.reshape(H, 1, LANES))


def _mlstm_body(q_ref, k_ref, v_ref, og_ref, gt_ref, gb_ref, cwq_ref, cwk_ref, cbq_ref, cbk_ref, nw_ref,
                out_ref, c_scr, n_scr, m_scr, qx_scr, kx_scr, qc_scr, kc_scr, *, rows):
    C = ML_CHUNK
    PAD = SUBLANES

    @pl.when(pl.program_id(2) == 0)
    def _():
        c_scr[...] = jnp.zeros_like(c_scr)
        n_scr[...] = jnp.zeros_like(n_scr)
        m_scr[...] = jnp.zeros_like(m_scr)
        qx_scr[0:PAD, :] = jnp.zeros((PAD, LANES), F32)
        kx_scr[0:PAD, :] = jnp.zeros((PAD, LANES), F32)

    qx_scr[PAD:PAD + rows, :] = q_ref[0].astype(F32)
    kx_scr[PAD:PAD + rows, :] = k_ref[0].astype(F32)
    accq = jnp.zeros((rows, LANES), F32) + cbq_ref[...]
    acck = jnp.zeros((rows, LANES), F32) + cbk_ref[...]
    for j in range(CONV_W):
        off = PAD - (CONV_W - 1) + j
        accq = accq + cwq_ref[j:j + 1, :] * qx_scr[pl.ds(off, rows), :]
        acck = acck + cwk_ref[j:j + 1, :] * kx_scr[pl.ds(off, rows), :]
    qc_scr[...] = accq * _sigmoid(accq) * (ML_DQK ** -0.5)
    kc_scr[...] = acck * _sigmoid(acck)
    qx_scr[0:PAD, :] = qx_scr[rows:rows + PAD, :]
    kx_scr[0:PAD, :] = kx_scr[rows:rows + PAD, :]

    lane = lax.broadcasted_iota(jnp.int32, (C, C), 1)
    sub = lax.broadcasted_iota(jnp.int32, (C, C), 0)
    causal = lane <= sub
    lower = causal.astype(F32)
    upper = (sub <= lane).astype(F32)
    nw = nw_ref[...]
    gbias = gb_ref[0]

    chunks = range(rows // C)
    lower_b, upper_b = lower.astype(BF16), upper.astype(BF16)
    st = [{} for _ in chunks]

    def split(x):
        hi = x.astype(BF16)
        return hi, (x - hi.astype(F32)).astype(BF16)

    for j in chunks:
        gc = gt_ref[0, pl.ds(j * C, C), :] + gbias
        gct = gc.T
        li_col = gc[:, 0:1]
        li_row = gct[0:1, :]
        ch, cl = split(_log_sigmoid(gc[:, ML_HEADS:ML_HEADS + 1]))
        rh, rl = split(_log_sigmoid(gct[ML_HEADS:ML_HEADS + 1, :]))
        g_t = (jnp.dot(lower_b, jnp.broadcast_to(ch, (C, C)), preferred_element_type=F32)
               + jnp.dot(lower_b, jnp.broadcast_to(cl, (C, C)), preferred_element_type=F32))
        g_s = (jnp.dot(jnp.broadcast_to(rh, (C, C)), upper_b, preferred_element_type=F32)
               + jnp.dot(jnp.broadcast_to(rl, (C, C)), upper_b, preferred_element_type=F32))
        st[j].update(g_t=g_t, g_s=g_s, li_col=li_col, li_row=li_row)

    for j in chunks:
        c = st[j]
        g_col = c["g_t"][:, 0:1]
        g_last = c["g_t"][C - 1:C, 0:1]
        dmat = jnp.where(causal, c["g_t"] - c["g_s"] + c["li_row"], -jnp.inf)
        log_ws = g_last - g_col + c["li_col"]
        qf = qc_scr[pl.ds(j * C, C), :]
        kf = kc_scr[pl.ds(j * C, C), :]
        qb = qf.astype(BF16)
        c.update(g_col=g_col, g_last=g_last, dmat=dmat, dmax=jnp.max(dmat, axis=1, keepdims=True),
                 log_ws=log_ws, ws_max=jnp.max(log_ws, axis=0, keepdims=True), qf=qf, kf=kf, qb=qb,
                 qk=_dot_nt(qb, kf.astype(BF16)))

    m = m_scr[:, 0:1]
    for j in chunks:
        c = st[j]
        m_new = jnp.maximum(c["g_last"] + m, c["ws_max"])
        c.update(m_prev=m, m_new=m_new, decay=jnp.exp(c["g_last"] + m - m_new))
        m = m_new
    m_scr[...] = jnp.broadcast_to(m, m_scr.shape)

    for j in chunks:
        c = st[j]
        vb = jnp.concatenate([v_ref[0, pl.ds(j * C, C), :], v_ref[1, pl.ds(j * C, C), :]], axis=1).astype(BF16)
        kw = c["kf"] * jnp.exp(c["log_ws"] - c["m_new"])
        c.update(vb=vb, upd=_dot_tn(kw.astype(BF16), vb), ksum=jnp.sum(kw, axis=0, keepdims=True))

    cm = c_scr[...]
    nv = n_scr[...]
    for j in chunks:
        c = st[j]
        c.update(c_in=cm.astype(BF16), n_in=nv)
        cm = c["decay"] * cm + c["upd"]
        nv = c["decay"] * nv + c["ksum"]
    c_scr[...] = cm
    n_scr[...] = nv

    for j in chunks:
        c = st[j]
        a_inter = c["g_col"] + c["m_prev"]
        m_t = jnp.maximum(a_inter, c["dmax"])
        w_inter = jnp.exp(a_inter - m_t)
        sqk = c["qk"] * jnp.exp(c["dmat"] - m_t)
        num = (w_inter * jnp.dot(c["qb"], c["c_in"], preferred_element_type=F32)
               + jnp.dot(sqk.astype(BF16), c["vb"], preferred_element_type=F32))
        den = (w_inter * jnp.sum(c["qf"] * c["n_in"], axis=1, keepdims=True)
               + jnp.sum(sqk, axis=1, keepdims=True))
        hh = num * (1.0 / jnp.maximum(jnp.abs(den), jnp.exp(-m_t)))
        y = hh * lax.rsqrt(jnp.mean(hh * hh, axis=-1, keepdims=True) + EPS) * nw
        og = jnp.concatenate([og_ref[0, pl.ds(j * C, C), :], og_ref[1, pl.ds(j * C, C), :]], axis=1).astype(F32)
        out_ref[pl.ds(j * C, C), :] = (y * _sigmoid(og)).astype(out_ref.dtype)


def _mlstm(proj3, gates, gate_bias_pad, conv_w, conv_b, norm_w, batch, seq, q_off):
    T = batch * seq
    rows = min(ML_ROWS, seq)
    nblk = seq // rows
    H = ML_HEADS
    k_off = q_off + H
    v_off = k_off + H
    o_off = v_off + 2 * H

    def slab(off):
        return pl.BlockSpec((1, rows, LANES), lambda b, h, c, off=off: (off + h, b * nblk + c, 0))

    def slab2(off):
        return pl.BlockSpec((2, rows, LANES), lambda b, h, c, off=off: (off // 2 + h, b * nblk + c, 0))

    qk_w = H * ML_DQK
    return pl.pallas_call(
        functools.partial(_mlstm_body, rows=rows),
        grid=(batch, H, nblk),
        in_specs=[
            slab(q_off), slab(k_off), slab2(v_off), slab2(o_off),
            pl.BlockSpec((1, rows, LANES), lambda b, h, c: (h, b * nblk + c, 0)),
            pl.BlockSpec((1, 1, LANES), lambda b, h, c: (h, 0, 0)),
            pl.BlockSpec((CONV_W, LANES), lambda b, h, c: (0, h)),
            pl.BlockSpec((CONV_W, LANES), lambda b, h, c: (0, H + h)),
            pl.BlockSpec((1, LANES), lambda b, h, c: (0, h)),
            pl.BlockSpec((1, LANES), lambda b, h, c: (0, H + h)),
            pl.BlockSpec((1, ML_DV), lambda b, h, c: (0, h)),
        ],
        out_specs=pl.BlockSpec((rows, ML_DV), lambda b, h, c: (b * nblk + c, h)),
        out_shape=jax.ShapeDtypeStruct((T, H * ML_DV), BF16),
        scratch_shapes=[
            pltpu.VMEM((ML_DQK, ML_DV), F32),
            pltpu.VMEM((1, ML_DQK), F32),
            pltpu.VMEM((1, LANES), F32),
            pltpu.VMEM((rows + 2 * SUBLANES, LANES), F32),
            pltpu.VMEM((rows + 2 * SUBLANES, LANES), F32),
            pltpu.VMEM((rows, LANES), F32),
            pltpu.VMEM((rows, LANES), F32),
        ],
        compiler_params=_cparams(("parallel", "parallel", "arbitrary")),
        name="mlstm",
    )(proj3, proj3, proj3, proj3, gates, gate_bias_pad, conv_w, conv_w,
      conv_b.reshape(1, 2 * qk_w), conv_b.reshape(1, 2 * qk_w), norm_w.reshape(1, H * ML_DV))


def _outproj_body(a_ref, b_ref, x_ref, wo_ref, ln_ref, wrh_ref, wrl_ref, br_ref,
                  x2_ref, h2_ref, idx_ref, gate_ref, rank_ref, cnt_ref, cnt_scr, *, sub_rows):
    tm = x_ref.shape[0]
    ka = a_ref.shape[1]
    s = _packed_rows(x_ref.shape[1])

    @pl.when(pl.program_id(0) == 0)
    def _():
        cnt_scr[...] = jnp.zeros_like(cnt_scr)

    lane = lax.broadcasted_iota(jnp.int32, (sub_rows, LANES), 1).astype(F32)
    onehots = [[] for _ in range(TOP_K)]
    for r0 in range(0, tm, sub_rows):
        rows = pl.ds(r0, sub_rows)
        res = (jnp.dot(a_ref[rows, :], wo_ref[0:ka, :], preferred_element_type=F32)
               + jnp.dot(b_ref[rows, :], wo_ref[ka:, :], preferred_element_type=F32))
        x2 = x_ref[rows, :] + res
        x2_ref[rows, :] = x2
        h2 = x2 * lax.rsqrt(jnp.mean(x2 * x2, axis=-1, keepdims=True) + EPS) * ln_ref[...]
        h_hi = h2.astype(BF16)
        h_hi32 = h_hi.astype(F32)
        _pack_store(h2_ref.at[pl.ds(r0 * s, sub_rows * s), :], h_hi32, rounded=True)

        h_lo = (h2 - h_hi32).astype(BF16)
        logits = (jnp.dot(h_hi, wrh_ref[...], preferred_element_type=F32)
                  + jnp.dot(h_lo, wrh_ref[...], preferred_element_type=F32)
                  + jnp.dot(h_hi, wrl_ref[...], preferred_element_type=F32)) + br_ref[...]
        vals, idxs = [], []
        cur = logits
        for _ in range(TOP_K):
            m = jnp.max(cur, axis=1, keepdims=True)
            ix = jnp.min(jnp.where(cur == m, lane, float(LANES)), axis=1, keepdims=True)
            vals.append(m)
            idxs.append(ix)
            cur = jnp.where(lane == ix, -jnp.inf, cur)
        es = [jnp.exp(v - vals[0]) for v in vals]
        inv = 1.0 / (es[0] + es[1] + es[2] + es[3])
        gate = jnp.zeros(logits.shape, F32)
        idx = jnp.zeros(logits.shape, F32)
        for k in range(TOP_K):
            gate = jnp.where(lane == float(k), es[k] * inv, gate)
            idx = jnp.where(lane == float(k), idxs[k], idx)
            onehots[k].append((lane == idxs[k]).astype(F32))
        gate_ref[rows, :] = gate
        idx_ref[rows, :] = idx.astype(jnp.int32)

    oh_k = [jnp.concatenate(o, axis=0) for o in onehots]
    oh = oh_k[0] + oh_k[1] + oh_k[2] + oh_k[3]
    r_i = lax.broadcasted_iota(jnp.int32, (tm, tm), 0)
    c_i = lax.broadcasted_iota(jnp.int32, (tm, tm), 1)
    before = jnp.dot((c_i < r_i).astype(BF16), oh.astype(BF16), preferred_element_type=F32) + cnt_scr[...]
    lane_t = lax.broadcasted_iota(jnp.int32, (tm, LANES), 1)
    rank = jnp.zeros((tm, LANES), F32)
    for k in range(TOP_K):
        rank = jnp.where(lane_t == k, jnp.sum(oh_k[k] * before, axis=1, keepdims=True), rank)
    rank_ref[...] = rank.astype(jnp.int32)
    cnt = cnt_scr[...] + jnp.sum(oh, axis=0, keepdims=True)
    cnt_scr[...] = cnt
    cnt_ref[...] = cnt.astype(jnp.int32)


def _outproj(a_out, b_out, x2d, w_out_bf, ln_w, wr_hi, wr_lo, b_router_pad):
    T, D = x2d.shape
    tm = min(OUTPROJ_TM, T)
    ka, kb = a_out.shape[1], b_out.shape[1]
    s = _packed_rows(D)
    row = lambda w: pl.BlockSpec((tm, w), lambda i: (i, 0))
    full = lambda r, c: pl.BlockSpec((r, c), lambda i: (0, 0))
    return pl.pallas_call(
        functools.partial(_outproj_body, sub_rows=min(OUTPROJ_SUB, tm)),
        grid=(T // tm,),
        in_specs=[row(ka), row(kb), row(D), full(ka + kb, D), full(1, D), full(D, LANES), full(D, LANES),
                  full(1, LANES)],
        out_specs=[row(D), pl.BlockSpec((tm * s, LANES), lambda i: (i, 0)), row(LANES), row(LANES), row(LANES),
                   full(1, LANES)],
        out_shape=[
            jax.ShapeDtypeStruct((T, D), F32),
            jax.ShapeDtypeStruct((T * s, LANES), jnp.uint32),
            jax.ShapeDtypeStruct((T, LANES), jnp.int32),
            jax.ShapeDtypeStruct((T, LANES), F32),
            jax.ShapeDtypeStruct((T, LANES), jnp.int32),
            jax.ShapeDtypeStruct((1, LANES), jnp.int32),
        ],
        scratch_shapes=[pltpu.VMEM((1, LANES), F32)],
        compiler_params=_cparams(("arbitrary",)),
        name="outproj_router",
    )(a_out, b_out, x2d, w_out_bf, ln_w, wr_hi, wr_lo, b_router_pad)


def _start_row_gather(idx_at, src_ref, buf, sem, n, s):
    def start(pair, carry):
        for p in range(DMA_QUEUES):
            i = pair * DMA_QUEUES + p
            src_row = pl.multiple_of(idx_at(i) * s, s)
            dst_row = pl.multiple_of(i * s, s)
            pltpu.make_async_copy(src_ref.at[pl.ds(src_row, s), :], buf.at[pl.ds(dst_row, s), :],
                                  sem).start(priority=p)
        return carry

    lax.fori_loop(0, n // DMA_QUEUES, start, 0, unroll=DMA_UNROLL // DMA_QUEUES)


def _wait_row_gather(buf, sem):
    pltpu.make_async_copy(buf, buf, sem).wait()


def _dispatch_body(zero_ref, dest_ref, h_ref, xs_ref, zbuf, sem, zsem, *, ntok, s, tm, n_blocks):
    blk = tm * s

    def zero_copy(b):
        return pltpu.make_async_copy(zbuf, xs_ref.at[pl.ds(pl.multiple_of(b * blk, blk), blk), :], zsem)

    @pl.when(pl.program_id(0) == 0)
    def _():
        zbuf[...] = jnp.zeros_like(zbuf)

        def zstart(b, carry):
            @pl.when(zero_ref[b] == 1)
            def _():
                zero_copy(b).start()
            return carry

        def zwait(b, carry):
            @pl.when(zero_ref[b] == 1)
            def _():
                zero_copy(b).wait()
            return carry

        lax.fori_loop(0, n_blocks, zstart, 0)
        lax.fori_loop(0, n_blocks, zwait, 0)

    def start(t, carry):
        src = h_ref.at[pl.ds(pl.multiple_of(t * s, s), s), :]
        for k in range(TOP_K):
            dst_row = pl.multiple_of(dest_ref[0, 0, t * TOP_K + k] * s, s)
            pltpu.make_async_copy(src, xs_ref.at[pl.ds(dst_row, s), :], sem).start(priority=k % DMA_QUEUES)
        return carry

    lax.fori_loop(0, ntok, start, 0, unroll=DISPATCH_UNROLL // TOP_K)
    for _ in range(TOP_K):
        pltpu.make_async_copy(h_ref, h_ref, sem).wait()


def _dispatch(dest, zero_blk, h_packed, d, n_slots, tm):
    s = _packed_rows(d)
    T = h_packed.shape[0] // s
    ntok = min(DISPATCH_TOKENS, T)
    n = ntok * TOP_K
    grid_spec = pltpu.PrefetchScalarGridSpec(
        num_scalar_prefetch=1,
        grid=(T // ntok,),
        in_specs=[
            pl.BlockSpec((1, 1, n), lambda i, z: (i, 0, 0), memory_space=pltpu.SMEM),
            pl.BlockSpec((ntok * s, LANES), lambda i, z: (i, 0)),
        ],
        out_specs=pl.BlockSpec(memory_space=pl.ANY),
        scratch_shapes=[pltpu.VMEM((tm * s, LANES), jnp.uint32), pltpu.SemaphoreType.DMA,
                        pltpu.SemaphoreType.DMA],
    )
    return pl.pallas_call(
        functools.partial(_dispatch_body, ntok=ntok, s=s, tm=tm, n_blocks=zero_blk.shape[0]),
        grid_spec=grid_spec,
        out_shape=jax.ShapeDtypeStruct((n_slots * s, LANES), jnp.uint32),
        compiler_params=_cparams(("arbitrary",)),
        name="moe_dispatch",
    )(zero_blk, dest.reshape(T // ntok, 1, n), h_packed)


def _staged_weights(sched_refs, hbm_refs, stage_refs, bf_refs, sem):
    e_ref, wt_ref, first_ref, ne_ref, nt_ref, more_ref = sched_refs
    w = pl.program_id(0)
    tn = stage_refs[0].shape[1]

    def copies(e, t):
        col = pl.multiple_of(t * tn, tn)
        return [pltpu.make_async_copy(h.at[e, :, pl.ds(col, tn)], st, sem.at[k])
                for k, (h, st) in enumerate(zip(hbm_refs, stage_refs))]

    def start_all(cs):
        for c in cs:
            c.start()

    @pl.when(w == 0)
    def _():
        start_all(copies(e_ref[0], wt_ref[0]))

    @pl.when(first_ref[w] == 1)
    def _():
        for c in copies(e_ref[w], wt_ref[w]):
            c.wait()
        for st, bf in zip(stage_refs, bf_refs):
            for r in range(0, st.shape[0], 256):
                bf[r:r + 256, :] = st[r:r + 256, :].astype(BF16)

        @pl.when(more_ref[w] == 1)
        def _():
            start_all(copies(ne_ref[w], nt_ref[w]))


def _row_groups(nrows, tm, compute, clear):
    sub = min(MOE_SUB_ROWS, tm)
    groups = tm // sub
    for live in range(groups + 1):
        @pl.when((nrows > (live - 1) * sub) & (nrows <= live * sub))
        def _(live=live):
            for g in range(live):
                compute(g * sub, sub)
            for g in range(live, groups):
                clear(g * sub, sub)


def _moe_up_body(e_ref, wt_ref, r_ref, ot_ref, first_ref, nrows_ref, ne_ref, nt_ref, more_ref,
                 x_ref, wg_ref, wu_ref, bg_ref, bu_ref, o_ref, wg_stage, wu_stage, wg_scr, wu_scr, x_scr, sem):
    w = pl.program_id(0)
    _staged_weights((e_ref, wt_ref, first_ref, ne_ref, nt_ref, more_ref), (wg_ref, wu_ref),
                    (wg_stage, wu_stage), (wg_scr, wu_scr), sem)
    tm, d = x_scr.shape
    s = _packed_rows(d)

    def compute(r0, n):
        rows = pl.ds(r0, n)
        for c in range(s):
            hi, lo = _unpack_load(x_ref, r0, n, s, c)
            x_scr[rows, c * LANES:(c + 1) * LANES] = hi.astype(BF16)
            x_scr[rows, d // 2 + c * LANES:d // 2 + (c + 1) * LANES] = lo.astype(BF16)
        x = x_scr[rows, :]
        gt = jnp.dot(x, wg_scr[...], preferred_element_type=F32) + bg_ref[0]
        up = jnp.dot(x, wu_scr[...], preferred_element_type=F32) + bu_ref[0]
        gt = jnp.minimum(gt, SWIGLU_LIMIT)
        up = jnp.clip(up, -SWIGLU_LIMIT, SWIGLU_LIMIT)
        swish = (0.5 * gt) * (1.0 + jnp.tanh((0.5 * SWIGLU_ALPHA) * gt))
        o_ref[rows, :] = ((up + 1.0) * swish).astype(o_ref.dtype)

    def clear(r0, n):
        o_ref[pl.ds(r0, n), :] = jnp.zeros((n, o_ref.shape[1]), o_ref.dtype)

    _row_groups(nrows_ref[w], tm, compute, clear)


def _moe_down_body(e_ref, wt_ref, r_ref, ot_ref, first_ref, nrows_ref, ne_ref, nt_ref, more_ref,
                   a_ref, wd_ref, bd_ref, o_ref, wd_stage, wd_scr, sem):
    w = pl.program_id(0)
    _staged_weights((e_ref, wt_ref, first_ref, ne_ref, nt_ref, more_ref), (wd_ref,), (wd_stage,), (wd_scr,), sem)
    tm = a_ref.shape[0]
    s = o_ref.shape[0] // tm

    def compute(r0, n):
        y = jnp.dot(a_ref[pl.ds(r0, n), :], wd_scr[...], preferred_element_type=F32) + bd_ref[0]
        _pack_store(o_ref.at[pl.ds(r0 * s, n * s), :], y)

    def clear(r0, n):
        o_ref[pl.ds(r0 * s, n * s), :] = jnp.zeros((n * s, LANES), o_ref.dtype)

    _row_groups(nrows_ref[w], tm, compute, clear)


def _moe_schedule(counts, tm, n_tiles, n_blocks):
    n_items = n_tiles * n_blocks
    experts = jnp.arange(N_EXPERTS, dtype=jnp.int32)
    blocks_e = (counts + tm - 1) // tm
    bend = jnp.cumsum(blocks_e)
    bstart = bend - blocks_e
    item_end = n_tiles * bend
    total = item_end[-1]
    later = (experts[None, :] > experts[:, None]) & (blocks_e[None, :] > 0)
    next_e = jnp.min(jnp.where(later, experts[None, :], N_EXPERTS - 1), axis=1)

    w = jnp.arange(n_items, dtype=jnp.int32)
    valid = w < total
    wc = jnp.minimum(w, jnp.maximum(total - 1, 0))
    e = jnp.minimum(jnp.sum((item_end[None, :] <= wc[:, None]).astype(jnp.int32), axis=1), N_EXPERTS - 1)
    sel = e[:, None] == experts[None, :]
    pick = lambda table: jnp.sum(jnp.where(sel, table[None, :], 0), axis=1)
    nb = jnp.maximum(pick(blocks_e), 1)
    local = wc - n_tiles * pick(bstart)
    wtile = sum((local >= t * nb).astype(jnp.int32) for t in range(1, n_tiles)) if n_tiles > 1 else 0 * local
    jblk = local - wtile * nb
    spare = jnp.maximum(w - total, 0)
    rblk = jnp.where(valid, pick(bstart) + jblk, bend[-1] + spare // n_tiles)
    otile = jnp.where(valid, wtile, spare % n_tiles)
    nrows = jnp.where(valid, jnp.clip(pick(counts) - jblk * tm, 0, tm), 0)
    first = (jblk == 0) & valid
    last_tile = wtile == n_tiles - 1
    more = first & (w + nb < total)
    i32 = lambda a: a.astype(jnp.int32)
    return (e, i32(wtile), i32(rblk), i32(otile), i32(first), i32(nrows),
            i32(jnp.where(last_tile, pick(next_e), e)), i32(jnp.where(last_tile, 0, wtile + 1)), i32(more))


def _moe_up(sched, xs_packed, w_gate, w_up, b_gate, b_up):
    D, d_ff = w_gate.shape[1], w_gate.shape[2]
    s = _packed_rows(D)
    n_slots = xs_packed.shape[0] // s
    tm, tf = MOE_TM, min(MOE_TF, d_ff)
    n_items = sched[0].shape[0]
    wspec = pl.BlockSpec(memory_space=pl.ANY)
    bspec = pl.BlockSpec((1, 1, tf), lambda w, e, wt, r, ot, *_: (e[w], 0, wt[w]))
    grid_spec = pltpu.PrefetchScalarGridSpec(
        num_scalar_prefetch=len(sched),
        grid=(n_items,),
        in_specs=[pl.BlockSpec((tm * s, LANES), lambda w, e, wt, r, ot, *_: (r[w], 0)),
                  wspec, wspec, bspec, bspec],
        out_specs=pl.BlockSpec((tm, tf), lambda w, e, wt, r, ot, *_: (r[w], ot[w])),
        scratch_shapes=[pltpu.VMEM((D, tf), F32), pltpu.VMEM((D, tf), F32),
                        pltpu.VMEM((D, tf), BF16), pltpu.VMEM((D, tf), BF16), pltpu.VMEM((tm, D), BF16),
                        pltpu.SemaphoreType.DMA((2,))],
    )
    return pl.pallas_call(
        _moe_up_body,
        grid_spec=grid_spec,
        out_shape=jax.ShapeDtypeStruct((n_slots, d_ff), BF16),
        compiler_params=_cparams(("arbitrary",)),
        name="moe_up",
    )(*sched, xs_packed, w_gate, w_up, b_gate.reshape(N_EXPERTS, 1, d_ff), b_up.reshape(N_EXPERTS, 1, d_ff))


def _moe_down(sched, act, w_down, b_down):
    n_slots, d_ff = act.shape
    D = w_down.shape[2]
    tm = MOE_TM
    s = _packed_rows(D)
    n_items = sched[0].shape[0]
    grid_spec = pltpu.PrefetchScalarGridSpec(
        num_scalar_prefetch=len(sched),
        grid=(n_items,),
        in_specs=[
            pl.BlockSpec((tm, d_ff), lambda w, e, wt, r, ot, *_: (r[w], 0)),
            pl.BlockSpec(memory_space=pl.ANY),
            pl.BlockSpec((1, 1, D), lambda w, e, wt, r, ot, *_: (e[w], 0, 0)),
        ],
        out_specs=pl.BlockSpec((tm * s, LANES), lambda w, e, wt, r, ot, *_: (r[w], 0)),
        scratch_shapes=[pltpu.VMEM((d_ff, D), F32), pltpu.VMEM((d_ff, D), BF16), pltpu.SemaphoreType.DMA((1,))],
    )
    return pl.pallas_call(
        _moe_down_body,
        grid_spec=grid_spec,
        out_shape=jax.ShapeDtypeStruct((n_slots * s, LANES), jnp.uint32),
        compiler_params=_cparams(("arbitrary",)),
        name="moe_down",
    )(*sched, act, w_down, b_down.reshape(N_EXPERTS, 1, D))


def _final_body(dest_ref, next_ref, x2_ref, gate_ref, w_ref, ys_ref, o_ref, buf0, buf1, sem0, sem1, *, tm, s):
    n = tm * TOP_K
    i = pl.program_id(0)

    def gather(idx_ref, tile, buf, sem):
        _start_row_gather(lambda r: idx_ref[0, 0, tile * n + r], ys_ref, buf, sem, n, s)

    def combine(tile, buf):
        rows = pl.ds(tile * tm, tm)
        gate = gate_ref[rows, :]
        half = o_ref.shape[1] // 2
        gates = [jnp.broadcast_to(gate[:, k:k + 1], (tm, LANES)) for k in range(TOP_K)]
        for c in range(s):
            acc_hi = x2_ref[rows, c * LANES:(c + 1) * LANES]
            acc_lo = x2_ref[rows, half + c * LANES:half + (c + 1) * LANES]
            for k in range(TOP_K):
                hi, lo = _unpack_load(buf, k * tm, tm, s, c)
                acc_hi = acc_hi + gates[k] * hi
                acc_lo = acc_lo + gates[k] * lo
            o_ref[rows, c * LANES:(c + 1) * LANES] = acc_hi
            o_ref[rows, half + c * LANES:half + (c + 1) * LANES] = acc_lo
        acc = o_ref[rows, :]
        o_ref[rows, :] = acc * lax.rsqrt(jnp.mean(acc * acc, axis=-1, keepdims=True) + EPS) * w_ref[...]

    @pl.when(i == 0)
    def _():
        gather(dest_ref, 0, buf0, sem0)

    gather(dest_ref, 1, buf1, sem1)
    _wait_row_gather(buf0, sem0)
    combine(0, buf0)
    gather(next_ref, 0, buf0, sem0)
    _wait_row_gather(buf1, sem1)
    combine(1, buf1)

    @pl.when(i == pl.num_programs(0) - 1)
    def _():
        _wait_row_gather(buf0, sem0)


def _final(x2, ys_packed, dest, gates_pad, w):
    T, D = x2.shape
    tm = min(FINAL_TM, T // 2)
    s = _packed_rows(D)
    n = tm * TOP_K
    steps = T // (2 * tm)
    dest_km = dest.reshape(steps, 2, tm, TOP_K).transpose(0, 1, 3, 2).reshape(steps, 1, 2 * n)
    buf = pltpu.VMEM((n * s, LANES), jnp.uint32)
    return pl.pallas_call(
        functools.partial(_final_body, tm=tm, s=s),
        grid=(steps,),
        in_specs=[
            pl.BlockSpec((1, 1, 2 * n), lambda i: (i, 0, 0), memory_space=pltpu.SMEM),
            pl.BlockSpec((1, 1, 2 * n), lambda i: (jnp.minimum(i + 1, steps - 1), 0, 0), memory_space=pltpu.SMEM),
            pl.BlockSpec((2 * tm, D), lambda i: (i, 0)),
            pl.BlockSpec((2 * tm, LANES), lambda i: (i, 0)),
            pl.BlockSpec((1, D), lambda i: (0, 0)),
            pl.BlockSpec(memory_space=pl.ANY),
        ],
        out_specs=pl.BlockSpec((2 * tm, D), lambda i: (i, 0)),
        out_shape=jax.ShapeDtypeStruct((T, D), F32),
        scratch_shapes=[buf, buf, pltpu.SemaphoreType.DMA, pltpu.SemaphoreType.DMA],
        compiler_params=_cparams(("arbitrary",)),
        name="final_norm",
    )(dest_km, dest_km, x2, gates_pad, w, ys_packed)


def _moe(h2_packed, T, D, top_idx, rank, counts, w_gate, b_gate, w_up, b_up, w_down, b_down):
    A = T * TOP_K
    tm = MOE_TM
    n_blocks = (A + N_EXPERTS * (tm - 1) + tm - 1) // tm
    n_slots = n_blocks * tm

    blocks_e = (counts + tm - 1) // tm
    bend = jnp.cumsum(blocks_e)
    bstart = bend - blocks_e
    experts = jnp.arange(N_EXPERTS, dtype=jnp.int32)
    first_slot = jnp.sum(jnp.where(top_idx[:, :, None] == experts, bstart * tm, 0), axis=-1)
    dest = (first_slot + rank).astype(jnp.int32).reshape(A)
    blk = jnp.arange(n_blocks, dtype=jnp.int32)
    is_last = jnp.any((blk[:, None] == bend[None, :] - 1) & (blocks_e[None, :] > 0), axis=1)
    zero_blk = ((blk >= bend[-1]) | is_last).astype(jnp.int32)

    xs = _dispatch(dest, zero_blk, h2_packed, D, n_slots, tm)
    d_ff = w_gate.shape[2]
    act = _moe_up(_moe_schedule(counts, tm, -(-d_ff // MOE_TF), n_blocks), xs, w_gate, w_up, b_gate, b_up)
    ys = _moe_down(_moe_schedule(counts, tm, 1, n_blocks), act, w_down, b_down)
    return ys, dest


def kernel(x, ln1_w, w_in, hg_lb_logits, hg_norm_w, ml_conv_w, ml_conv_b, ml_igate_b, ml_fgate_b, ml_norm_w,
           w_out, ln2_w, w_router, b_router, w_gate, b_gate, w_up, b_up, w_down, b_down, final_norm_w):
    B, S, D = x.shape
    T = B * S
    depth = w_in.shape[0]
    hg_w = HG_HEADS * HG_DK
    n_main = 4 * hg_w + 2 * ML_HEADS * ML_DQK + 2 * ML_HEADS * ML_DV
    lb_all = jnp.cumsum(jax.nn.softmax(hg_lb_logits.astype(F32), axis=0), axis=0)

    xc = x.reshape(T, D)
    for l in range(depth):
        w_gates_pad = jnp.pad(w_in[l][:, n_main:], ((0, 0), (0, LANES - 2 * ML_HEADS))).astype(BF16)
        w_bf = w_in[l].astype(BF16)
        ln1 = ln1_w[l].reshape(1, D)
        hf_tiles = hg_w // INPROJ_TN
        proj_f = _inproj(xc, ln1, w_bf, hf_tiles, lambda j: j + hf_tiles, F32)[0]
        proj_b, gates = _inproj(xc, ln1, w_bf, n_main // INPROJ_TN - hf_tiles,
                                lambda j: jnp.where(j >= hf_tiles, j + hf_tiles, j), BF16, w_gates_pad)
        a_out = _hgrn(proj_b, proj_f, lb_all[l], hg_norm_w[l], B, S)
        gate_bias = jnp.pad(jnp.concatenate([ml_igate_b[l], ml_fgate_b[l]]), (0, LANES - 2 * ML_HEADS))
        gates_h = jnp.stack([jnp.roll(gates, -hd, axis=1) for hd in range(ML_HEADS)])
        bias_h = jnp.stack([jnp.roll(gate_bias, -hd) for hd in range(ML_HEADS)]).reshape(ML_HEADS, 1, LANES)
        b_out = _mlstm(proj_b, gates_h, bias_h, ml_conv_w[l], ml_conv_b[l], ml_norm_w[l], B, S, 3 * HG_HEADS)
        wr_pad = jnp.pad(w_router[l], ((0, 0), (0, LANES - N_EXPERTS)))
        wr_hi = wr_pad.astype(BF16)
        wr_lo = (wr_pad - wr_hi.astype(F32)).astype(BF16)
        br_pad = jnp.pad(b_router[l], (0, LANES - N_EXPERTS), constant_values=-1e30).reshape(1, LANES)
        x2, h2, idx_pad, gates_pad, rank_pad, cnt = _outproj(
            a_out, b_out, xc, w_out[l].astype(BF16), ln2_w[l].reshape(1, D), wr_hi, wr_lo, br_pad)
        ys, dest = _moe(h2, T, D, idx_pad[:, :TOP_K], rank_pad[:, :TOP_K], cnt[0, :N_EXPERTS],
                        w_gate[l], b_gate[l], w_up[l], b_up[l], w_down[l], b_down[l])
        if l + 1 < depth:
            raise NotImplementedError("only the final layer fuses the output norm")
        xc = _final(x2, ys, dest, gates_pad, final_norm_w.reshape(1, D))
    return xc.reshape(B, S, D)
```

```python
import functools

import jax
import jax.numpy as jnp
from jax import lax
from jax.experimental import pallas as pl
from jax.experimental.pallas import tpu as pltpu

F32 = jnp.float32
BF16 = jnp.bfloat16

EPS = 1e-6
HG_HEADS = 8
HG_DK = 128
ML_HEADS = 4
ML_DQK = 128
ML_DV = 256
CONV_W = 4
N_EXPERTS = 32
TOP_K = 4
SWIGLU_ALPHA = 1.702
SWIGLU_LIMIT = 7.0

LANES = 128
SUBLANES = 8
VMEM_LIMIT_BYTES = 56 * 1024 * 1024

HG_CHUNK = 64
HG_SUB = 16
ML_CHUNK = 128
HG_ROWS = 2048
ML_ROWS = 512
HG_HEADS_PER_STEP = 2

INPROJ_TM = 1024
INPROJ_TN = 1024
OUTPROJ_TM = 512
OUTPROJ_SUB = 256
DISPATCH_TOKENS = 1024
MOE_TM = 512
MOE_TF = 1024
MOE_SUB_ROWS = 128
DMA_UNROLL = 32
DISPATCH_UNROLL = 64
DMA_QUEUES = 2
FINAL_TM = 128


def _dot_nt(a, b):
    return lax.dot_general(a, b, (((1,), (1,)), ((), ())), preferred_element_type=F32)


def _dot_tn(a, b):
    return lax.dot_general(a, b, (((0,), (0,)), ((), ())), preferred_element_type=F32)


def _log_sigmoid(z):
    return jnp.minimum(z, 0.0) - jnp.log(1.0 + jnp.exp(-jnp.abs(z)))


def _sigmoid(z):
    return 0.5 * jnp.tanh(0.5 * z) + 0.5


def _cparams(semantics):
    return pltpu.CompilerParams(dimension_semantics=semantics, vmem_limit_bytes=VMEM_LIMIT_BYTES)


_HI_MASK = 0xFFFF0000


def _packed_rows(d):
    return d // (2 * LANES)


def _pack_store(o_ref, v, rounded=False):
    n, d = v.shape
    s, half = _packed_rows(d), d // 2
    bits = pltpu.bitcast(v if rounded else v.astype(BF16).astype(F32), jnp.uint32)
    for c in range(s):
        hi = bits[:, c * LANES:(c + 1) * LANES]
        lo = bits[:, half + c * LANES:half + (c + 1) * LANES]
        o_ref[pl.ds(c, n, stride=s), :] = hi | jnp.right_shift(lo, jnp.uint32(16))


def _unpack_load(buf, first_row, n, s, c):
    w = buf[pl.ds(first_row * s + c, n, stride=s), :]
    hi = pltpu.bitcast(w & jnp.uint32(_HI_MASK), F32)
    lo = pltpu.bitcast(jnp.left_shift(w, jnp.uint32(16)), F32)
    return hi, lo


def _inproj_body(*refs, tn, with_gates):
    if with_gates:
        x_ref, lnw_ref, w_ref, wg_ref, o_ref, g_ref, h_scr = refs
    else:
        x_ref, lnw_ref, w_ref, o_ref, h_scr = refs

    @pl.when(pl.program_id(1) == 0)
    def _():
        x = x_ref[...]
        h = x * lax.rsqrt(jnp.mean(x * x, axis=-1, keepdims=True) + EPS) * lnw_ref[...]
        hb = h.astype(BF16)
        h_scr[...] = hb
        if with_gates:
            g_ref[...] = jnp.dot(hb, wg_ref[...], preferred_element_type=F32)

    res = jnp.dot(h_scr[...], w_ref[...], preferred_element_type=F32)
    for c in range(tn // LANES):
        o_ref[c] = res[:, c * LANES:(c + 1) * LANES].astype(o_ref.dtype)


def _inproj(x2d, ln_w, w_bf, n_tiles, col_tile, out_dtype, w_gates_pad=None):
    T, D = x2d.shape
    tm = min(INPROJ_TM, T)
    tn = INPROJ_TN
    with_gates = w_gates_pad is not None
    in_specs = [
        pl.BlockSpec((tm, D), lambda i, j: (i, 0)),
        pl.BlockSpec((1, D), lambda i, j: (0, 0)),
        pl.BlockSpec((D, tn), lambda i, j: (0, col_tile(j))),
    ]
    out_specs = [pl.BlockSpec((tn // LANES, tm, LANES), lambda i, j: (j, i, 0))]
    out_shape = [jax.ShapeDtypeStruct((n_tiles * tn // LANES, T, LANES), out_dtype)]
    args = [x2d, ln_w, w_bf]
    if with_gates:
        in_specs.append(pl.BlockSpec((D, LANES), lambda i, j: (0, 0)))
        out_specs.append(pl.BlockSpec((tm, LANES), lambda i, j: (i, 0)))
        out_shape.append(jax.ShapeDtypeStruct((T, LANES), F32))
        args.append(w_gates_pad)
    return pl.pallas_call(
        functools.partial(_inproj_body, tn=tn, with_gates=with_gates),
        grid=(T // tm, n_tiles),
        in_specs=in_specs,
        out_specs=out_specs,
        out_shape=out_shape,
        scratch_shapes=[pltpu.VMEM((tm, D), BF16)],
        compiler_params=_cparams(("parallel", "arbitrary")),
        name="inproj_gates" if with_gates else "inproj",
    )(*args)


def _hgrn_body(q_ref, f_ref, i_ref, g_ref, lb_ref, nw_ref, o_ref, st_scr, *, rows, heads):
    C, SUB = HG_CHUNK, HG_SUB
    nsub = C // SUB

    @pl.when(pl.program_id(2) == 0)
    def _():
        st_scr[...] = jnp.zeros_like(st_scr)

    lbs = [lb_ref[hh] for hh in range(heads)]
    log_lbs = [jnp.log(lb) for lb in lbs]
    log_1mlbs = [jnp.log1p(-lb) for lb in lbs]

    r_i = lax.broadcasted_iota(jnp.int32, (C, C), 0)
    c_i = lax.broadcasted_iota(jnp.int32, (C, C), 1)
    tri = (c_i <= r_i).astype(F32)
    sub_shift = SUB.bit_length() - 1
    diag_mask = (c_i <= r_i) & (jnp.right_shift(r_i, sub_shift) == jnp.right_shift(c_i, sub_shift))
    row_id = lax.broadcasted_iota(jnp.int32, (C, HG_DK), 0)

    pairs = [(j, hh) for j in range(rows // C) for hh in range(heads)]
    tri_b = tri.astype(BF16)
    st = {p: {} for p in pairs}

    for p in pairs:
        j, hh = p
        z = f_ref[hh, pl.ds(j * C, C), :].astype(F32)
        e = jnp.exp(-jnp.abs(z))
        e1 = 1.0 + e
        log_sig = jnp.minimum(z, 0.0) - jnp.log(e1)
        sig_neg = jnp.where(z >= 0, e, 1.0) / e1
        cc = log_1mlbs[hh] + log_sig
        log_f = jnp.maximum(log_lbs[hh], cc) + jnp.log(1.0 + jnp.exp(-jnp.abs(log_lbs[hh] - cc)))
        st[p]["kk"] = (1.0 - lbs[hh]) * sig_neg
        hi = log_f.astype(BF16)
        lo = (log_f - hi.astype(F32)).astype(BF16)
        st[p]["b"] = (jnp.dot(tri_b, hi, preferred_element_type=F32)
                      + jnp.dot(tri_b, lo, preferred_element_type=F32))

    for p in pairs:
        j, hh = p
        q = q_ref[hh, pl.ds(j * C, C), :].astype(F32)
        kk, b = st[p]["kk"], st[p]["b"]
        b_last = b[C - 1:C, :]
        st[p]["dec"] = jnp.exp(b_last)
        st[p]["qe"] = (q * jnp.exp(b)).astype(BF16)
        st[p]["kdec"] = (kk * jnp.exp(b_last - b)).astype(BF16)
        refs = [b[I * SUB:I * SUB + 1, :] for I in range(nsub)]
        refb = jnp.concatenate([jnp.broadcast_to(r, (SUB, HG_DK)) for r in refs], axis=0)
        qd = (q * jnp.exp(b - refb)).astype(BF16)
        kd = (kk * jnp.exp(refb - b)).astype(BF16)
        q_parts, k_parts = [], []
        for J in range(nsub - 1):
            r = refs[J + 1]
            qj = q * jnp.exp(jnp.minimum(b - r, 0.0))
            kj = kk * jnp.exp(jnp.minimum(r - b, 0.0))
            q_parts.append(jnp.where(row_id >= (J + 1) * SUB, qj, 0.0).astype(BF16))
            k_parts.append(jnp.where((row_id >= J * SUB) & (row_id < (J + 1) * SUB), kj, 0.0).astype(BF16))
        st[p]["att_d"] = _dot_nt(qd, kd)
        st[p]["att_o"] = _dot_nt(jnp.concatenate(q_parts, axis=1), jnp.concatenate(k_parts, axis=1))

    for p in pairs:
        j, hh = p
        vb = i_ref[hh, pl.ds(j * C, C), :].astype(BF16)
        att = jnp.where(diag_mask, st[p]["att_d"], 0.0) + st[p]["att_o"]
        st[p]["intra"] = jnp.dot(att.astype(BF16), vb, preferred_element_type=F32)
        st[p]["upd"] = _dot_tn(vb, st[p]["kdec"])

    for hh in range(heads):
        s = st_scr[hh]
        for j in range(rows // C):
            st[(j, hh)]["s_in"] = s.astype(BF16)
            s = s * st[(j, hh)]["dec"] + st[(j, hh)]["upd"]
        st_scr[hh] = s

    for p in pairs:
        j, hh = p
        g = g_ref[hh, pl.ds(j * C, C), :].astype(F32)
        o = _dot_nt(st[p]["qe"], st[p]["s_in"]) + st[p]["intra"]
        y = o * lax.rsqrt(jnp.mean(o * o, axis=-1, keepdims=True) + EPS) * nw_ref[hh]
        o_ref[pl.ds(j * C, C), hh * LANES:(hh + 1) * LANES] = (y * (g * _sigmoid(g))).astype(o_ref.dtype)


def _hgrn(proj_b, proj_f, lb, norm_w, batch, seq):
    T = batch * seq
    rows = min(HG_ROWS, seq)
    nblk = seq // rows
    H = HG_HEADS
    hp = HG_HEADS_PER_STEP

    def slab(off):
        return pl.BlockSpec((hp, rows, LANES), lambda b, h, c, off=off: (off // hp + h, b * nblk + c, 0))

    vec = pl.BlockSpec((hp, 1, LANES), lambda b, h, c: (h, 0, 0))
    return pl.pallas_call(
        functools.partial(_hgrn_body, rows=rows, heads=hp),
        grid=(batch, H // hp, nblk),
        in_specs=[slab(0), slab(0), slab(H), slab(2 * H), vec, vec],
        out_specs=pl.BlockSpec((rows, hp * LANES), lambda b, h, c: (b * nblk + c, h)),
        out_shape=jax.ShapeDtypeStruct((T, H * LANES), BF16),
        scratch_shapes=[pltpu.VMEM((hp, LANES, HG_DK), F32)],
        compiler_params=_cparams(("parallel", "parallel", "arbitrary")),
        name="hgrn2",
    )(proj_b, proj_f, proj_b, proj_b, lb.reshape(H, 1, HG_DK), norm_w.reshape(H, 1, LANES))


def _mlstm_body(q_ref, k_ref, v_ref, og_ref, gt_ref, gb_ref, cwq_ref, cwk_ref, cbq_ref, cbk_ref, nw_ref,
                out_ref, c_scr, n_scr, m_scr, qx_scr, kx_scr, qc_scr, kc_scr, *, rows):
    C = ML_CHUNK
    PAD = SUBLANES

    @pl.when(pl.program_id(2) == 0)
    def _():
        c_scr[...] = jnp.zeros_like(c_scr)
        n_scr[...] = jnp.zeros_like(n_scr)
        m_scr[...] = jnp.zeros_like(m_scr)
        qx_scr[0:PAD, :] = jnp.zeros((PAD, LANES), F32)
        kx_scr[0:PAD, :] = jnp.zeros((PAD, LANES), F32)

    qx_scr[PAD:PAD + rows, :] = q_ref[0].astype(F32)
    kx_scr[PAD:PAD + rows, :] = k_ref[0].astype(F32)
    accq = jnp.zeros((rows, LANES), F32) + cbq_ref[...]
    acck = jnp.zeros((rows, LANES), F32) + cbk_ref[...]
    for j in range(CONV_W):
        off = PAD - (CONV_W - 1) + j
        accq = accq + cwq_ref[j:j + 1, :] * qx_scr[pl.ds(off, rows), :]
        acck = acck + cwk_ref[j:j + 1, :] * kx_scr[pl.ds(off, rows), :]
    qc_scr[...] = accq * _sigmoid(accq) * (ML_DQK ** -0.5)
    kc_scr[...] = acck * _sigmoid(acck)
    qx_scr[0:PAD, :] = qx_scr[rows:rows + PAD, :]
    kx_scr[0:PAD, :] = kx_scr[rows:rows + PAD, :]

    lane = lax.broadcasted_iota(jnp.int32, (C, C), 1)
    sub = lax.broadcasted_iota(jnp.int32, (C, C), 0)
    causal = lane <= sub
    lower = causal.astype(F32)
    upper = (sub <= lane).astype(F32)
    nw = nw_ref[...]
    gbias = gb_ref[0]

    chunks = range(rows // C)
    lower_b, upper_b = lower.astype(BF16), upper.astype(BF16)
    st = [{} for _ in chunks]

    def split(x):
        hi = x.astype(BF16)
        return hi, (x - hi.astype(F32)).astype(BF16)

    for j in chunks:
        gc = gt_ref[0, pl.ds(j * C, C), :] + gbias
        gct = gc.T
        li_col = gc[:, 0:1]
        li_row = gct[0:1, :]
        ch, cl = split(_log_sigmoid(gc[:, ML_HEADS:ML_HEADS + 1]))
        rh, rl = split(_log_sigmoid(gct[ML_HEADS:ML_HEADS + 1, :]))
        g_t = (jnp.dot(lower_b, jnp.broadcast_to(ch, (C, C)), preferred_element_type=F32)
               + jnp.dot(lower_b, jnp.broadcast_to(cl, (C, C)), preferred_element_type=F32))
        g_s = (jnp.dot(jnp.broadcast_to(rh, (C, C)), upper_b, preferred_element_type=F32)
               + jnp.dot(jnp.broadcast_to(rl, (C, C)), upper_b, preferred_element_type=F32))
        st[j].update(g_t=g_t, g_s=g_s, li_col=li_col, li_row=li_row)

    for j in chunks:
        c = st[j]
        g_col = c["g_t"][:, 0:1]
        g_last = c["g_t"][C - 1:C, 0:1]
        dmat = jnp.where(causal, c["g_t"] - c["g_s"] + c["li_row"], -jnp.inf)
        log_ws = g_last - g_col + c["li_col"]
        qf = qc_scr[pl.ds(j * C, C), :]
        kf = kc_scr[pl.ds(j * C, C), :]
        qb = qf.astype(BF16)
        c.update(g_col=g_col, g_last=g_last, dmat=dmat, dmax=jnp.max(dmat, axis=1, keepdims=True),
                 log_ws=log_ws, ws_max=jnp.max(log_ws, axis=0, keepdims=True), qf=qf, kf=kf, qb=qb,
                 qk=_dot_nt(qb, kf.astype(BF16)))

    m = m_scr[:, 0:1]
    for j in chunks:
        c = st[j]
        m_new = jnp.maximum(c["g_last"] + m, c["ws_max"])
        c.update(m_prev=m, m_new=m_new, decay=jnp.exp(c["g_last"] + m - m_new))
        m = m_new
    m_scr[...] = jnp.broadcast_to(m, m_scr.shape)

    for j in chunks:
        c = st[j]
        vb = jnp.concatenate([v_ref[0, pl.ds(j * C, C), :], v_ref[1, pl.ds(j * C, C), :]], axis=1).astype(BF16)
        kw = c["kf"] * jnp.exp(c["log_ws"] - c["m_new"])
        c.update(vb=vb, upd=_dot_tn(kw.astype(BF16), vb), ksum=jnp.sum(kw, axis=0, keepdims=True))

    cm = c_scr[...]
    nv = n_scr[...]
    for j in chunks:
        c = st[j]
        c.update(c_in=cm.astype(BF16), n_in=nv)
        cm = c["decay"] * cm + c["upd"]
        nv = c["decay"] * nv + c["ksum"]
    c_scr[...] = cm
    n_scr[...] = nv

    for j in chunks:
        c = st[j]
        a_inter = c["g_col"] + c["m_prev"]
        m_t = jnp.maximum(a_inter, c["dmax"])
        w_inter = jnp.exp(a_inter - m_t)
        sqk = c["qk"] * jnp.exp(c["dmat"] - m_t)
        num = (w_inter * jnp.dot(c["qb"], c["c_in"], preferred_element_type=F32)
               + jnp.dot(sqk.astype(BF16), c["vb"], preferred_element_type=F32))
        den = (w_inter * jnp.sum(c["qf"] * c["n_in"], axis=1, keepdims=True)
               + jnp.sum(sqk, axis=1, keepdims=True))
        hh = num * (1.0 / jnp.maximum(jnp.abs(den), jnp.exp(-m_t)))
        y = hh * lax.rsqrt(jnp.mean(hh * hh, axis=-1, keepdims=True) + EPS) * nw
        og = jnp.concatenate([og_ref[0, pl.ds(j * C, C), :], og_ref[1, pl.ds(j * C, C), :]], axis=1).astype(F32)
        out_ref[pl.ds(j * C, C), :] = (y * _sigmoid(og)).astype(out_ref.dtype)


def _mlstm(proj3, gates, gate_bias_pad, conv_w, conv_b, norm_w, batch, seq, q_off):
    T = batch * seq
    rows = min(ML_ROWS, seq)
    nblk = seq // rows
    H = ML_HEADS
    k_off = q_off + H
    v_off = k_off + H
    o_off = v_off + 2 * H

    def slab(off):
        return pl.BlockSpec((1, rows, LANES), lambda b, h, c, off=off: (off + h, b * nblk + c, 0))

    def slab2(off):
        return pl.BlockSpec((2, rows, LANES), lambda b, h, c, off=off: (off // 2 + h, b * nblk + c, 0))

    qk_w = H * ML_DQK
    return pl.pallas_call(
        functools.partial(_mlstm_body, rows=rows),
        grid=(batch, H, nblk),
        in_specs=[
            slab(q_off), slab(k_off), slab2(v_off), slab2(o_off),
            pl.BlockSpec((1, rows, LANES), lambda b, h, c: (h, b * nblk + c, 0)),
            pl.BlockSpec((1, 1, LANES), lambda b, h, c: (h, 0, 0)),
            pl.BlockSpec((CONV_W, LANES), lambda b, h, c: (0, h)),
            pl.BlockSpec((CONV_W, LANES), lambda b, h, c: (0, H + h)),
            pl.BlockSpec((1, LANES), lambda b, h, c: (0, h)),
            pl.BlockSpec((1, LANES), lambda b, h, c: (0, H + h)),
            pl.BlockSpec((1, ML_DV), lambda b, h, c: (0, h)),
        ],
        out_specs=pl.BlockSpec((rows, ML_DV), lambda b, h, c: (b * nblk + c, h)),
        out_shape=jax.ShapeDtypeStruct((T, H * ML_DV), BF16),
        scratch_shapes=[
            pltpu.VMEM((ML_DQK, ML_DV), F32),
            pltpu.VMEM((1, ML_DQK), F32),
            pltpu.VMEM((1, LANES), F32),
            pltpu.VMEM((rows + 2 * SUBLANES, LANES), F32),
            pltpu.VMEM((rows + 2 * SUBLANES, LANES), F32),
            pltpu.VMEM((rows, LANES), F32),
            pltpu.VMEM((rows, LANES), F32),
        ],
        compiler_params=_cparams(("parallel", "parallel", "arbitrary")),
        name="mlstm",
    )(proj3, proj3, proj3, proj3, gates, gate_bias_pad, conv_w, conv_w,
      conv_b.reshape(1, 2 * qk_w), conv_b.reshape(1, 2 * qk_w), norm_w.reshape(1, H * ML_DV))


def _outproj_body(a_ref, b_ref, x_ref, wo_ref, ln_ref, wrh_ref, wrl_ref, br_ref,
                  x2_ref, h2_ref, idx_ref, gate_ref, rank_ref, cnt_ref, cnt_scr, *, sub_rows):
    tm = x_ref.shape[0]
    ka = a_ref.shape[1]
    s = _packed_rows(x_ref.shape[1])

    @pl.when(pl.program_id(0) == 0)
    def _():
        cnt_scr[...] = jnp.zeros_like(cnt_scr)

    lane = lax.broadcasted_iota(jnp.int32, (sub_rows, LANES), 1).astype(F32)
    onehots = [[] for _ in range(TOP_K)]
    for r0 in range(0, tm, sub_rows):
        rows = pl.ds(r0, sub_rows)
        res = (jnp.dot(a_ref[rows, :], wo_ref[0:ka, :], preferred_element_type=F32)
               + jnp.dot(b_ref[rows, :], wo_ref[ka:, :], preferred_element_type=F32))
        x2 = x_ref[rows, :] + res
        x2_ref[rows, :] = x2
        h2 = x2 * lax.rsqrt(jnp.mean(x2 * x2, axis=-1, keepdims=True) + EPS) * ln_ref[...]
        h_hi = h2.astype(BF16)
        h_hi32 = h_hi.astype(F32)
        _pack_store(h2_ref.at[pl.ds(r0 * s, sub_rows * s), :], h_hi32, rounded=True)

        h_lo = (h2 - h_hi32).astype(BF16)
        logits = (jnp.dot(h_hi, wrh_ref[...], preferred_element_type=F32)
                  + jnp.dot(h_lo, wrh_ref[...], preferred_element_type=F32)
                  + jnp.dot(h_hi, wrl_ref[...], preferred_element_type=F32)) + br_ref[...]
        vals, idxs = [], []
        cur = logits
        for _ in range(TOP_K):
            m = jnp.max(cur, axis=1, keepdims=True)
            ix = jnp.min(jnp.where(cur == m, lane, float(LANES)), axis=1, keepdims=True)
            vals.append(m)
            idxs.append(ix)
            cur = jnp.where(lane == ix, -jnp.inf, cur)
        es = [jnp.exp(v - vals[0]) for v in vals]
        inv = 1.0 / (es[0] + es[1] + es[2] + es[3])
        gate = jnp.zeros(logits.shape, F32)
        idx = jnp.zeros(logits.shape, F32)
        for k in range(TOP_K):
            gate = jnp.where(lane == float(k), es[k] * inv, gate)
            idx = jnp.where(lane == float(k), idxs[k], idx)
            onehots[k].append((lane == idxs[k]).astype(F32))
        gate_ref[rows, :] = gate
        idx_ref[rows, :] = idx.astype(jnp.int32)

    oh_k = [jnp.concatenate(o, axis=0) for o in onehots]
    oh = oh_k[0] + oh_k[1] + oh_k[2] + oh_k[3]
    r_i = lax.broadcasted_iota(jnp.int32, (tm, tm), 0)
    c_i = lax.broadcasted_iota(jnp.int32, (tm, tm), 1)
    before = jnp.dot((c_i < r_i).astype(BF16), oh.astype(BF16), preferred_element_type=F32) + cnt_scr[...]
    lane_t = lax.broadcasted_iota(jnp.int32, (tm, LANES), 1)
    rank = jnp.zeros((tm, LANES), F32)
    for k in range(TOP_K):
        rank = jnp.where(lane_t == k, jnp.sum(oh_k[k] * before, axis=1, keepdims=True), rank)
    rank_ref[...] = rank.astype(jnp.int32)
    cnt = cnt_scr[...] + jnp.sum(oh, axis=0, keepdims=True)
    cnt_scr[...] = cnt
    cnt_ref[...] = cnt.astype(jnp.int32)


def _outproj(a_out, b_out, x2d, w_out_bf, ln_w, wr_hi, wr_lo, b_router_pad):
    T, D = x2d.shape
    tm = min(OUTPROJ_TM, T)
    ka, kb = a_out.shape[1], b_out.shape[1]
    s = _packed_rows(D)
    row = lambda w: pl.BlockSpec((tm, w), lambda i: (i, 0))
    full = lambda r, c: pl.BlockSpec((r, c), lambda i: (0, 0))
    return pl.pallas_call(
        functools.partial(_outproj_body, sub_rows=min(OUTPROJ_SUB, tm)),
        grid=(T // tm,),
        in_specs=[row(ka), row(kb), row(D), full(ka + kb, D), full(1, D), full(D, LANES), full(D, LANES),
                  full(1, LANES)],
        out_specs=[row(D), pl.BlockSpec((tm * s, LANES), lambda i: (i, 0)), row(LANES), row(LANES), row(LANES),
                   full(1, LANES)],
        out_shape=[
            jax.ShapeDtypeStruct((T, D), F32),
            jax.ShapeDtypeStruct((T * s, LANES), jnp.uint32),
            jax.ShapeDtypeStruct((T, LANES), jnp.int32),
            jax.ShapeDtypeStruct((T, LANES), F32),
            jax.ShapeDtypeStruct((T, LANES), jnp.int32),
            jax.ShapeDtypeStruct((1, LANES), jnp.int32),
        ],
        scratch_shapes=[pltpu.VMEM((1, LANES), F32)],
        compiler_params=_cparams(("arbitrary",)),
        name="outproj_router",
    )(a_out, b_out, x2d, w_out_bf, ln_w, wr_hi, wr_lo, b_router_pad)


def _start_row_gather(idx_at, src_ref, buf, sem, n, s):
    def start(pair, carry):
        for p in range(DMA_QUEUES):
            i = pair * DMA_QUEUES + p
            src_row = pl.multiple_of(idx_at(i) * s, s)
            dst_row = pl.multiple_of(i * s, s)
            pltpu.make_async_copy(src_ref.at[pl.ds(src_row, s), :], buf.at[pl.ds(dst_row, s), :],
                                  sem).start(priority=p)
        return carry

    lax.fori_loop(0, n // DMA_QUEUES, start, 0, unroll=DMA_UNROLL // DMA_QUEUES)


def _wait_row_gather(buf, sem):
    pltpu.make_async_copy(buf, buf, sem).wait()


def _dispatch_body(zero_ref, dest_ref, h_ref, xs_ref, zbuf, sem, zsem, *, ntok, s, tm, n_blocks):
    blk = tm * s

    def zero_copy(b):
        return pltpu.make_async_copy(zbuf, xs_ref.at[pl.ds(pl.multiple_of(b * blk, blk), blk), :], zsem)

    @pl.when(pl.program_id(0) == 0)
    def _():
        zbuf[...] = jnp.zeros_like(zbuf)

        def zstart(b, carry):
            @pl.when(zero_ref[b] == 1)
            def _():
                zero_copy(b).start()
            return carry

        def zwait(b, carry):
            @pl.when(zero_ref[b] == 1)
            def _():
                zero_copy(b).wait()
            return carry

        lax.fori_loop(0, n_blocks, zstart, 0)
        lax.fori_loop(0, n_blocks, zwait, 0)

    def start(t, carry):
        src = h_ref.at[pl.ds(pl.multiple_of(t * s, s), s), :]
        for k in range(TOP_K):
            dst_row = pl.multiple_of(dest_ref[0, 0, t * TOP_K + k] * s, s)
            pltpu.make_async_copy(src, xs_ref.at[pl.ds(dst_row, s), :], sem).start(priority=k % DMA_QUEUES)
        return carry

    lax.fori_loop(0, ntok, start, 0, unroll=DISPATCH_UNROLL // TOP_K)
    for _ in range(TOP_K):
        pltpu.make_async_copy(h_ref, h_ref, sem).wait()


def _dispatch(dest, zero_blk, h_packed, d, n_slots, tm):
    s = _packed_rows(d)
    T = h_packed.shape[0] // s
    ntok = min(DISPATCH_TOKENS, T)
    n = ntok * TOP_K
    grid_spec = pltpu.PrefetchScalarGridSpec(
        num_scalar_prefetch=1,
        grid=(T // ntok,),
        in_specs=[
            pl.BlockSpec((1, 1, n), lambda i, z: (i, 0, 0), memory_space=pltpu.SMEM),
            pl.BlockSpec((ntok * s, LANES), lambda i, z: (i, 0)),
        ],
        out_specs=pl.BlockSpec(memory_space=pl.ANY),
        scratch_shapes=[pltpu.VMEM((tm * s, LANES), jnp.uint32), pltpu.SemaphoreType.DMA,
                        pltpu.SemaphoreType.DMA],
    )
    return pl.pallas_call(
        functools.partial(_dispatch_body, ntok=ntok, s=s, tm=tm, n_blocks=zero_blk.shape[0]),
        grid_spec=grid_spec,
        out_shape=jax.ShapeDtypeStruct((n_slots * s, LANES), jnp.uint32),
        compiler_params=_cparams(("arbitrary",)),
        name="moe_dispatch",
    )(zero_blk, dest.reshape(T // ntok, 1, n), h_packed)


def _staged_weights(sched_refs, hbm_refs, stage_refs, bf_refs, sem):
    e_ref, wt_ref, first_ref, ne_ref, nt_ref, more_ref = sched_refs
    w = pl.program_id(0)
    tn = stage_refs[0].shape[1]

    def copies(e, t):
        col = pl.multiple_of(t * tn, tn)
        return [pltpu.make_async_copy(h.at[e, :, pl.ds(col, tn)], st, sem.at[k])
                for k, (h, st) in enumerate(zip(hbm_refs, stage_refs))]

    def start_all(cs):
        for c in cs:
            c.start()

    @pl.when(w == 0)
    def _():
        start_all(copies(e_ref[0], wt_ref[0]))

    @pl.when(first_ref[w] == 1)
    def _():
        for c in copies(e_ref[w], wt_ref[w]):
            c.wait()
        for st, bf in zip(stage_refs, bf_refs):
            for r in range(0, st.shape[0], 256):
                bf[r:r + 256, :] = st[r:r + 256, :].astype(BF16)

        @pl.when(more_ref[w] == 1)
        def _():
            start_all(copies(ne_ref[w], nt_ref[w]))


def _row_groups(nrows, tm, compute, clear):
    sub = min(MOE_SUB_ROWS, tm)
    groups = tm // sub
    for live in range(groups + 1):
        @pl.when((nrows > (live - 1) * sub) & (nrows <= live * sub))
        def _(live=live):
            for g in range(live):
                compute(g * sub, sub)
            for g in range(live, groups):
                clear(g * sub, sub)


def _moe_up_body(e_ref, wt_ref, r_ref, ot_ref, first_ref, nrows_ref, ne_ref, nt_ref, more_ref,
                 x_ref, wg_ref, wu_ref, bg_ref, bu_ref, o_ref, wg_stage, wu_stage, w_scr, x_scr, sem):
    w = pl.program_id(0)
    tf = o_ref.shape[1]
    _staged_weights((e_ref, wt_ref, first_ref, ne_ref, nt_ref, more_ref), (wg_ref, wu_ref),
                    (wg_stage, wu_stage), (w_scr.at[:, pl.ds(0, tf)], w_scr.at[:, pl.ds(tf, tf)]), sem)
    tm, d = x_scr.shape
    s = _packed_rows(d)

    def compute(r0, n):
        rows = pl.ds(r0, n)
        for c in range(s):
            hi, lo = _unpack_load(x_ref, r0, n, s, c)
            x_scr[rows, c * LANES:(c + 1) * LANES] = hi.astype(BF16)
            x_scr[rows, d // 2 + c * LANES:d // 2 + (c + 1) * LANES] = lo.astype(BF16)
        x = x_scr[rows, :]
        both = jnp.dot(x, w_scr[...], preferred_element_type=F32)
        gt = both[:, 0:tf] + bg_ref[0]
        up = both[:, tf:2 * tf] + bu_ref[0]
        gt = jnp.minimum(gt, SWIGLU_LIMIT)
        up = jnp.clip(up, -SWIGLU_LIMIT, SWIGLU_LIMIT)
        swish = (0.5 * gt) * (1.0 + jnp.tanh((0.5 * SWIGLU_ALPHA) * gt))
        o_ref[rows, :] = ((up + 1.0) * swish).astype(o_ref.dtype)

    def clear(r0, n):
        o_ref[pl.ds(r0, n), :] = jnp.zeros((n, o_ref.shape[1]), o_ref.dtype)

    _row_groups(nrows_ref[w], tm, compute, clear)


def _moe_down_body(e_ref, wt_ref, r_ref, ot_ref, first_ref, nrows_ref, ne_ref, nt_ref, more_ref,
                   a_ref, wd_ref, bd_ref, o_ref, wd_stage, wd_scr, sem):
    w = pl.program_id(0)
    _staged_weights((e_ref, wt_ref, first_ref, ne_ref, nt_ref, more_ref), (wd_ref,), (wd_stage,), (wd_scr,), sem)
    tm = a_ref.shape[0]
    s = o_ref.shape[0] // tm

    def compute(r0, n):
        y = jnp.dot(a_ref[pl.ds(r0, n), :], wd_scr[...], preferred_element_type=F32) + bd_ref[0]
        _pack_store(o_ref.at[pl.ds(r0 * s, n * s), :], y)

    def clear(r0, n):
        o_ref[pl.ds(r0 * s, n * s), :] = jnp.zeros((n * s, LANES), o_ref.dtype)

    _row_groups(nrows_ref[w], tm, compute, clear)


def _moe_schedule(counts, tm, n_tiles, n_blocks):
    n_items = n_tiles * n_blocks
    experts = jnp.arange(N_EXPERTS, dtype=jnp.int32)
    blocks_e = (counts + tm - 1) // tm
    bend = jnp.cumsum(blocks_e)
    bstart = bend - blocks_e
    item_end = n_tiles * bend
    total = item_end[-1]
    later = (experts[None, :] > experts[:, None]) & (blocks_e[None, :] > 0)
    next_e = jnp.min(jnp.where(later, experts[None, :], N_EXPERTS - 1), axis=1)

    w = jnp.arange(n_items, dtype=jnp.int32)
    valid = w < total
    wc = jnp.minimum(w, jnp.maximum(total - 1, 0))
    e = jnp.minimum(jnp.sum((item_end[None, :] <= wc[:, None]).astype(jnp.int32), axis=1), N_EXPERTS - 1)
    sel = e[:, None] == experts[None, :]
    pick = lambda table: jnp.sum(jnp.where(sel, table[None, :], 0), axis=1)
    nb = jnp.maximum(pick(blocks_e), 1)
    local = wc - n_tiles * pick(bstart)
    wtile = sum((local >= t * nb).astype(jnp.int32) for t in range(1, n_tiles)) if n_tiles > 1 else 0 * local
    jblk = local - wtile * nb
    spare = jnp.maximum(w - total, 0)
    rblk = jnp.where(valid, pick(bstart) + jblk, bend[-1] + spare // n_tiles)
    otile = jnp.where(valid, wtile, spare % n_tiles)
    nrows = jnp.where(valid, jnp.clip(pick(counts) - jblk * tm, 0, tm), 0)
    first = (jblk == 0) & valid
    last_tile = wtile == n_tiles - 1
    more = first & (w + nb < total)
    i32 = lambda a: a.astype(jnp.int32)
    return (e, i32(wtile), i32(rblk), i32(otile), i32(first), i32(nrows),
            i32(jnp.where(last_tile, pick(next_e), e)), i32(jnp.where(last_tile, 0, wtile + 1)), i32(more))


def _moe_up(sched, xs_packed, w_gate, w_up, b_gate, b_up):
    D, d_ff = w_gate.shape[1], w_gate.shape[2]
    s = _packed_rows(D)
    n_slots = xs_packed.shape[0] // s
    tm, tf = MOE_TM, min(MOE_TF, d_ff)
    n_items = sched[0].shape[0]
    wspec = pl.BlockSpec(memory_space=pl.ANY)
    bspec = pl.BlockSpec((1, 1, tf), lambda w, e, wt, r, ot, *_: (e[w], 0, wt[w]))
    grid_spec = pltpu.PrefetchScalarGridSpec(
        num_scalar_prefetch=len(sched),
        grid=(n_items,),
        in_specs=[pl.BlockSpec((tm * s, LANES), lambda w, e, wt, r, ot, *_: (r[w], 0)),
                  wspec, wspec, bspec, bspec],
        out_specs=pl.BlockSpec((tm, tf), lambda w, e, wt, r, ot, *_: (r[w], ot[w])),
        scratch_shapes=[pltpu.VMEM((D, tf), F32), pltpu.VMEM((D, tf), F32),
                        pltpu.VMEM((D, 2 * tf), BF16), pltpu.VMEM((tm, D), BF16),
                        pltpu.SemaphoreType.DMA((2,))],
    )
    return pl.pallas_call(
        _moe_up_body,
        grid_spec=grid_spec,
        out_shape=jax.ShapeDtypeStruct((n_slots, d_ff), BF16),
        compiler_params=_cparams(("arbitrary",)),
        name="moe_up",
    )(*sched, xs_packed, w_gate, w_up, b_gate.reshape(N_EXPERTS, 1, d_ff), b_up.reshape(N_EXPERTS, 1, d_ff))


def _moe_down(sched, act, w_down, b_down):
    n_slots, d_ff = act.shape
    D = w_down.shape[2]
    tm = MOE_TM
    s = _packed_rows(D)
    n_items = sched[0].shape[0]
    grid_spec = pltpu.PrefetchScalarGridSpec(
        num_scalar_prefetch=len(sched),
        grid=(n_items,),
        in_specs=[
            pl.BlockSpec((tm, d_ff), lambda w, e, wt, r, ot, *_: (r[w], 0)),
            pl.BlockSpec(memory_space=pl.ANY),
            pl.BlockSpec((1, 1, D), lambda w, e, wt, r, ot, *_: (e[w], 0, 0)),
        ],
        out_specs=pl.BlockSpec((tm * s, LANES), lambda w, e, wt, r, ot, *_: (r[w], 0)),
        scratch_shapes=[pltpu.VMEM((d_ff, D), F32), pltpu.VMEM((d_ff, D), BF16), pltpu.SemaphoreType.DMA((1,))],
    )
    return pl.pallas_call(
        _moe_down_body,
        grid_spec=grid_spec,
        out_shape=jax.ShapeDtypeStruct((n_slots * s, LANES), jnp.uint32),
        compiler_params=_cparams(("arbitrary",)),
        name="moe_down",
    )(*sched, act, w_down, b_down.reshape(N_EXPERTS, 1, D))


def _final_body(dest_ref, next_ref, x2_ref, gate_ref, w_ref, ys_ref, o_ref, buf0, buf1, sem0, sem1, *, tm, s):
    n = tm * TOP_K
    i = pl.program_id(0)

    def gather(idx_ref, tile, buf, sem):
        _start_row_gather(lambda r: idx_ref[0, 0, tile * n + r], ys_ref, buf, sem, n, s)

    def combine(tile, buf):
        rows = pl.ds(tile * tm, tm)
        gate = gate_ref[rows, :]
        half = o_ref.shape[1] // 2
        gates = [jnp.broadcast_to(gate[:, k:k + 1], (tm, LANES)) for k in range(TOP_K)]
        for c in range(s):
            acc_hi = x2_ref[rows, c * LANES:(c + 1) * LANES]
            acc_lo = x2_ref[rows, half + c * LANES:half + (c + 1) * LANES]
            for k in range(TOP_K):
                hi, lo = _unpack_load(buf, k * tm, tm, s, c)
                acc_hi = acc_hi + gates[k] * hi
                acc_lo = acc_lo + gates[k] * lo
            o_ref[rows, c * LANES:(c + 1) * LANES] = acc_hi
            o_ref[rows, half + c * LANES:half + (c + 1) * LANES] = acc_lo
        acc = o_ref[rows, :]
        o_ref[rows, :] = acc * lax.rsqrt(jnp.mean(acc * acc, axis=-1, keepdims=True) + EPS) * w_ref[...]

    @pl.when(i == 0)
    def _():
        gather(dest_ref, 0, buf0, sem0)

    gather(dest_ref, 1, buf1, sem1)
    _wait_row_gather(buf0, sem0)
    combine(0, buf0)
    gather(next_ref, 0, buf0, sem0)
    _wait_row_gather(buf1, sem1)
    combine(1, buf1)

    @pl.when(i == pl.num_programs(0) - 1)
    def _():
        _wait_row_gather(buf0, sem0)


def _final(x2, ys_packed, dest, gates_pad, w):
    T, D = x2.shape
    tm = min(FINAL_TM, T // 2)
    s = _packed_rows(D)
    n = tm * TOP_K
    steps = T // (2 * tm)
    dest_km = dest.reshape(steps, 2, tm, TOP_K).transpose(0, 1, 3, 2).reshape(steps, 1, 2 * n)
    buf = pltpu.VMEM((n * s, LANES), jnp.uint32)
    return pl.pallas_call(
        functools.partial(_final_body, tm=tm, s=s),
        grid=(steps,),
        in_specs=[
            pl.BlockSpec((1, 1, 2 * n), lambda i: (i, 0, 0), memory_space=pltpu.SMEM),
            pl.BlockSpec((1, 1, 2 * n), lambda i: (jnp.minimum(i + 1, steps - 1), 0, 0), memory_space=pltpu.SMEM),
            pl.BlockSpec((2 * tm, D), lambda i: (i, 0)),
            pl.BlockSpec((2 * tm, LANES), lambda i: (i, 0)),
            pl.BlockSpec((1, D), lambda i: (0, 0)),
            pl.BlockSpec(memory_space=pl.ANY),
        ],
        out_specs=pl.BlockSpec((2 * tm, D), lambda i: (i, 0)),
        out_shape=jax.ShapeDtypeStruct((T, D), F32),
        scratch_shapes=[buf, buf, pltpu.SemaphoreType.DMA, pltpu.SemaphoreType.DMA],
        compiler_params=_cparams(("arbitrary",)),
        name="final_norm",
    )(dest_km, dest_km, x2, gates_pad, w, ys_packed)


def _moe(h2_packed, T, D, top_idx, rank, counts, w_gate, b_gate, w_up, b_up, w_down, b_down):
    A = T * TOP_K
    tm = MOE_TM
    n_blocks = (A + N_EXPERTS * (tm - 1) + tm - 1) // tm
    n_slots = n_blocks * tm

    blocks_e = (counts + tm - 1) // tm
    bend = jnp.cumsum(blocks_e)
    bstart = bend - blocks_e
    experts = jnp.arange(N_EXPERTS, dtype=jnp.int32)
    first_slot = jnp.sum(jnp.where(top_idx[:, :, None] == experts, bstart * tm, 0), axis=-1)
    dest = (first_slot + rank).astype(jnp.int32).reshape(A)
    blk = jnp.arange(n_blocks, dtype=jnp.int32)
    is_last = jnp.any((blk[:, None] == bend[None, :] - 1) & (blocks_e[None, :] > 0), axis=1)
    zero_blk = ((blk >= bend[-1]) | is_last).astype(jnp.int32)

    xs = _dispatch(dest, zero_blk, h2_packed, D, n_slots, tm)
    d_ff = w_gate.shape[2]
    act = _moe_up(_moe_schedule(counts, tm, -(-d_ff // MOE_TF), n_blocks), xs, w_gate, w_up, b_gate, b_up)
    ys = _moe_down(_moe_schedule(counts, tm, 1, n_blocks), act, w_down, b_down)
    return ys, dest


def kernel(x, ln1_w, w_in, hg_lb_logits, hg_norm_w, ml_conv_w, ml_conv_b, ml_igate_b, ml_fgate_b, ml_norm_w,
           w_out, ln2_w, w_router, b_router, w_gate, b_gate, w_up, b_up, w_down, b_down, final_norm_w):
    B, S, D = x.shape
    T = B * S
    depth = w_in.shape[0]
    hg_w = HG_HEADS * HG_DK
    n_main = 4 * hg_w + 2 * ML_HEADS * ML_DQK + 2 * ML_HEADS * ML_DV
    lb_all = jnp.cumsum(jax.nn.softmax(hg_lb_logits.astype(F32), axis=0), axis=0)

    xc = x.reshape(T, D)
    for l in range(depth):
        w_gates_pad = jnp.pad(w_in[l][:, n_main:], ((0, 0), (0, LANES - 2 * ML_HEADS))).astype(BF16)
        w_bf = w_in[l].astype(BF16)
        ln1 = ln1_w[l].reshape(1, D)
        hf_tiles = hg_w // INPROJ_TN
        proj_f = _inproj(xc, ln1, w_bf, hf_tiles, lambda j: j + hf_tiles, F32)[0]
        proj_b, gates = _inproj(xc, ln1, w_bf, n_main // INPROJ_TN - hf_tiles,
                                lambda j: jnp.where(j >= hf_tiles, j + hf_tiles, j), BF16, w_gates_pad)
        a_out = _hgrn(proj_b, proj_f, lb_all[l], hg_norm_w[l], B, S)
        gate_bias = jnp.pad(jnp.concatenate([ml_igate_b[l], ml_fgate_b[l]]), (0, LANES - 2 * ML_HEADS))
        gates_h = jnp.stack([jnp.roll(gates, -hd, axis=1) for hd in range(ML_HEADS)])
        bias_h = jnp.stack([jnp.roll(gate_bias, -hd) for hd in range(ML_HEADS)]).reshape(ML_HEADS, 1, LANES)
        b_out = _mlstm(proj_b, gates_h, bias_h, ml_conv_w[l], ml_conv_b[l], ml_norm_w[l], B, S, 3 * HG_HEADS)
        wr_pad = jnp.pad(w_router[l], ((0, 0), (0, LANES - N_EXPERTS)))
        wr_hi = wr_pad.astype(BF16)
        wr_lo = (wr_pad - wr_hi.astype(F32)).astype(BF16)
        br_pad = jnp.pad(b_router[l], (0, LANES - N_EXPERTS), constant_values=-1e30).reshape(1, LANES)
        x2, h2, idx_pad, gates_pad, rank_pad, cnt = _outproj(
            a_out, b_out, xc, w_out[l].astype(BF16), ln2_w[l].reshape(1, D), wr_hi, wr_lo, br_pad)
        ys, dest = _moe(h2, T, D, idx_pad[:, :TOP_K], rank_pad[:, :TOP_K], cnt[0, :N_EXPERTS],
                        w_gate[l], b_gate[l], w_up[l], b_up[l], w_down[l], b_down[l])
        if l + 1 < depth:
            raise NotImplementedError("only the final layer fuses the output norm")
        xc = _final(x2, ys, dest, gates_pad, final_norm_w.reshape(1, D))
    return xc.reshape(B, S, D)
```
